```python
import math
import jax, jax.numpy as jnp
from jax import lax
import numpy as np

D_MODEL = 1024
BATCH = 8
SEQ = 8192
DEPTH = 4

GRID_W = 64
CHUNK = 128
EPS = 1e-6
MIX_WIDTH = D_MODEL
RET_HEADS = 8
RET_DH = (MIX_WIDTH // 2) // RET_HEADS
RET_WIDTH = RET_HEADS * RET_DH
ROPE_BASE = 10000.0
NA_HEADS = 8
NA_DH = (MIX_WIDTH // 2) // NA_HEADS
NA_WIDTH = NA_HEADS * NA_DH
NA_WIN_R = 8
NA_WIN_C = 16
NA_QBLK = 16
AB_IN_DIM = 4 * RET_WIDTH + 3 * NA_WIDTH
SSD_INNER = 2 * D_MODEL
SSD_HEADDIM = 64
SSD_HEADS = SSD_INNER // SSD_HEADDIM
SSD_GROUPS = 4
SSD_HPG = SSD_HEADS // SSD_GROUPS
SSD_STATE = 128
SSD_CONV = 5
SSD_XBC = SSD_INNER + 2 * SSD_GROUPS * SSD_STATE
SSD_IN_DIM = SSD_INNER + SSD_XBC + 2 * SSD_HEADS
FFN_DIM = 2816
FFN_CONV = 3
N_EVEN = (DEPTH + 1) // 2
N_ODD = DEPTH // 2

kernel_name = 'hybrid_retention_natten_ssd_encoder'


def rms_norm(x, g):
    xf = x.astype(jnp.float32)
    y = xf * lax.rsqrt(jnp.mean(xf * xf, axis=-1, keepdims=True) + EPS)
    return (y * g.astype(jnp.float32)).astype(x.dtype)


def depthwise_conv_centered(x, w, b):
    width, ch = w.shape
    pad = width // 2
    y = lax.conv_general_dilated(x, w[:, None, :].astype(x.dtype), window_strides=(1,),
                                 padding=[(pad, pad)], dimension_numbers=('NWC', 'WIO', 'NWC'),
                                 feature_group_count=ch)
    return y + b.astype(x.dtype)


def rotary(x, pos):
    half = x.shape[-1] // 2
    inv = 1.0 / (ROPE_BASE ** (jnp.arange(half, dtype=jnp.float32) / half))
    ang = pos.astype(jnp.float32)[:, None] * inv[None, :]
    cos = jnp.cos(ang)[None, :, None, :]
    sin = jnp.sin(ang)[None, :, None, :]
    xf = x.astype(jnp.float32)
    x1, x2 = xf[..., :half], xf[..., half:]
    return jnp.concatenate([x1 * cos - x2 * sin, x1 * sin + x2 * cos], axis=-1).astype(x.dtype)


def chunked_scan(q, k, v, log_a, include_diag):
    f32 = jnp.float32
    bsz, s, g, n = q.shape
    hg, p = v.shape[3], v.shape[4]
    n_chunks = s // CHUNK

    def to_chunks(t):
        return jnp.moveaxis(t.astype(f32).reshape(bsz, n_chunks, CHUNK, *t.shape[2:]), 1, 0)

    qc, kc, vc, ac = to_chunks(q), to_chunks(k), to_chunks(v), to_chunks(log_a)
    idx = jnp.arange(CHUNK)
    mask = (idx[:, None] >= idx[None, :]) if include_diag else (idx[:, None] > idx[None, :])

    def step(h, inp):
        qq, kk, vv, aa = inp
        cs = jnp.cumsum(aa, axis=1)
        seg = cs[:, :, None] - cs[:, None, :]
        decay = jnp.exp(jnp.where(mask[None, :, :, None, None], seg, -jnp.inf))
        qk = jnp.einsum('bjgn,blgn->bjlg', qq, kk)
        y_intra = jnp.einsum('bjlgh,blghp->bjghp', qk[..., None] * decay, vv)
        y_inter = jnp.einsum('bjgn,bghnp->bjghp', qq, h) * jnp.exp(cs)[..., None]
        tail = jnp.exp(cs[:, -1:] - cs)
        h_new = h * jnp.exp(cs[:, -1])[..., None, None] + jnp.einsum('blgn,blghp->bghnp', kk, vv * tail[..., None])
        return h_new, y_intra + y_inter

    h0 = jnp.zeros((bsz, g, hg, n, p), f32)
    _, ys = lax.scan(step, h0, (qc, kc, vc, ac))
    return jnp.moveaxis(ys, 0, 1).reshape(bsz, s, g, hg, p)


def bidir_scan(q, k, v_f, v_b, a_f, a_b):
    y_f = chunked_scan(q, k, v_f, a_f, True)
    flip = lambda t: jnp.flip(t, axis=1)
    y_b = flip(chunked_scan(flip(q), flip(k), flip(v_b), flip(a_b), False))
    return y_f + y_b


def neighborhood_attention(q, k, v, rpb):
    f32 = jnp.float32
    bsz, s, h, dh = q.shape
    rows = s // GRID_W
    win_r = min(NA_WIN_R, rows)
    n_cb = GRID_W // NA_QBLK
    span = NA_QBLK + NA_WIN_C
    qb_all = (q.astype(f32) * dh ** -0.5).reshape(bsz, rows, n_cb, NA_QBLK, h, dh)
    qb_all = jnp.moveaxis(qb_all, 2, 0)
    k = k.reshape(bsz, rows, GRID_W, h, dh)
    v = v.reshape(bsz, rows, GRID_W, h, dh)
    r = jnp.arange(rows)
    key_rows = jnp.clip(r - win_r // 2, 0, rows - win_r)[:, None] + jnp.arange(win_r)[None, :]
    dr = key_rows - r[:, None] + (NA_WIN_R - 1)
    c0s = jnp.arange(n_cb) * NA_QBLK

    def block(args):
        qb, c0 = args
        qcols = c0 + jnp.arange(NA_QBLK)
        kcols = jnp.clip(c0 - NA_WIN_C // 2, 0, GRID_W - span) + jnp.arange(span)
        cstart = jnp.clip(qcols - NA_WIN_C // 2, 0, GRID_W - NA_WIN_C)
        valid = (kcols[None, :] >= cstart[:, None]) & (kcols[None, :] < cstart[:, None] + NA_WIN_C)
        dc = jnp.clip(kcols[None, :] - qcols[:, None], -(NA_WIN_C - 1), NA_WIN_C - 1) + (NA_WIN_C - 1)
        kb = k[:, key_rows[:, :, None], kcols[None, None, :]].astype(f32)
        vb = v[:, key_rows[:, :, None], kcols[None, None, :]].astype(f32)
        bias = rpb[:, dr[:, None, :, None], dc[None, :, None, :]].astype(f32)
        sc = jnp.einsum('brqhd,brwchd->bhrqwc', qb, kb) + bias[None]
        sc = jnp.where(valid[:, None, :], sc, -jnp.inf)
        pr = jax.nn.softmax(sc.reshape(*sc.shape[:4], -1), axis=-1).reshape(sc.shape)
        return jnp.einsum('bhrqwc,brwchd->brqhd', pr, vb)

    out = lax.map(block, (qb_all, c0s))
    return jnp.moveaxis(out, 0, 2).reshape(bsz, s, h * dh)


def retention_na_mixer(hn, w_in, ret_decay_logit, ret_gn_g, na_rpb, w_out):
    f32 = jnp.float32
    bsz, s, _ = hn.shape
    proj = hn @ w_in
    R, N = RET_WIDTH, NA_WIDTH
    rq, rk, rv, rg, nq, nk, nv = jnp.split(proj, [R, 2 * R, 3 * R, 4 * R, 4 * R + N, 4 * R + 2 * N], axis=-1)
    pos = jnp.arange(s)
    rshape = (bsz, s, RET_HEADS, RET_DH)
    rq = rotary(rq.reshape(rshape), pos)
    rk = rotary(rk.reshape(rshape), pos) * (RET_DH ** -0.5)
    rv = rv.reshape(bsz, s, RET_HEADS, 1, RET_DH)
    log_gamma = -jax.nn.softplus(-ret_decay_logit.astype(f32))
    a_f = jnp.broadcast_to(log_gamma[0][None, None, :, None], (bsz, s, RET_HEADS, 1))
    a_b = jnp.broadcast_to(log_gamma[1][None, None, :, None], (bsz, s, RET_HEADS, 1))
    y = bidir_scan(rq, rk, rv, rv, a_f, a_b).reshape(rshape)
    mu = jnp.mean(y, axis=-1, keepdims=True)
    var = jnp.mean(jnp.square(y - mu), axis=-1, keepdims=True)
    y = ((y - mu) * lax.rsqrt(var + EPS)).reshape(bsz, s, RET_WIDTH) * ret_gn_g.astype(f32)
    ret_out = (jax.nn.silu(rg.astype(f32)) * y).astype(hn.dtype)
    nshape = (bsz, s, NA_HEADS, NA_DH)
    na_out = neighborhood_attention(nq.reshape(nshape), nk.reshape(nshape), nv.reshape(nshape), na_rpb).astype(hn.dtype)
    return jnp.concatenate([ret_out, na_out], axis=-1) @ w_out


def ssd_mixer(hn, w_in, conv_w, conv_b, dt_bias, a_log, d_skip, norm_g, w_out):
    f32 = jnp.float32
    bsz, s, _ = hn.shape
    proj = hn @ w_in
    z, xbc, dt_raw = jnp.split(proj, [SSD_INNER, SSD_INNER + SSD_XBC], axis=-1)
    xbc = jax.nn.silu(depthwise_conv_centered(xbc, conv_w, conv_b))
    xs, bm, cm = jnp.split(xbc, [SSD_INNER, SSD_INNER + SSD_GROUPS * SSD_STATE], axis=-1)
    dt = jax.nn.softplus(dt_raw.astype(f32).reshape(bsz, s, 2, SSD_HEADS) + dt_bias.astype(f32))
    A = -jnp.exp(a_log.astype(f32))
    log_a = dt * A
    grp = (bsz, s, SSD_GROUPS, SSD_HPG)
    xh = xs.astype(f32).reshape(bsz, s, SSD_GROUPS, SSD_HPG, SSD_HEADDIM)
    v_f = xh * dt[:, :, 0].reshape(grp)[..., None]
    v_b = xh * dt[:, :, 1].reshape(grp)[..., None]
    qc = cm.reshape(bsz, s, SSD_GROUPS, SSD_STATE)
    kb = bm.reshape(bsz, s, SSD_GROUPS, SSD_STATE)
    y = bidir_scan(qc, kb, v_f, v_b, log_a[:, :, 0].reshape(grp), log_a[:, :, 1].reshape(grp))
    y = y + xh * d_skip.astype(f32).reshape(SSD_GROUPS, SSD_HPG)[..., None]
    y = y.reshape(bsz, s, SSD_INNER) * jax.nn.silu(z.astype(f32))
    yg = y.reshape(bsz, s, SSD_GROUPS, SSD_INNER // SSD_GROUPS)
    yg = yg * lax.rsqrt(jnp.mean(yg * yg, axis=-1, keepdims=True) + EPS)
    y = yg.reshape(bsz, s, SSD_INNER) * norm_g.astype(f32)
    return y.astype(hn.dtype) @ w_out


def conv_geglu_ffn(hn, w_up, conv_w, conv_b, w_down):
    u = depthwise_conv_centered(hn @ w_up, conv_w, conv_b)
    gate, val = jnp.split(u, 2, axis=-1)
    return (jax.nn.gelu(gate, approximate=True) * val) @ w_down


def _fwd_setup_inputs(seed: int = 0) -> dict:
    key = jax.random.key(seed)
    ks = jax.random.split(key, 24)
    f32 = jnp.float32

    def nrm(k, shape, scale):
        return jax.random.normal(k, shape, f32) * scale

    def gain(k, shape):
        return 1.0 + 0.05 * jax.random.normal(k, shape, f32)

    x = nrm(ks[0], (BATCH, SEQ, D_MODEL), 1.0)
    norm_mix_pre = gain(ks[1], (DEPTH, D_MODEL))
    norm_mix_post = gain(ks[2], (DEPTH, D_MODEL))
    norm_ffn_pre = gain(ks[3], (DEPTH, D_MODEL))
    norm_ffn_post = gain(ks[4], (DEPTH, D_MODEL))
    ab_w_in = nrm(ks[5], (N_EVEN, D_MODEL, AB_IN_DIM), D_MODEL ** -0.5)
    gamma0 = 1.0 - 2.0 ** (-5.0 - jnp.arange(RET_HEADS, dtype=f32))
    ab_ret_decay_logit = (jnp.log(gamma0) - jnp.log1p(-gamma0))[None, None, :] + nrm(ks[6], (N_EVEN, 2, RET_HEADS), 0.05)
    ab_ret_gn_g = gain(ks[7], (N_EVEN, RET_WIDTH))
    ab_na_rpb = nrm(ks[8], (N_EVEN, NA_HEADS, 2 * NA_WIN_R - 1, 2 * NA_WIN_C - 1), 0.1)
    ab_w_out = nrm(ks[9], (N_EVEN, RET_WIDTH + NA_WIDTH, D_MODEL), (RET_WIDTH + NA_WIDTH) ** -0.5)
    c_w_in = nrm(ks[10], (N_ODD, D_MODEL, SSD_IN_DIM), D_MODEL ** -0.5)
    c_conv_w = nrm(ks[11], (N_ODD, SSD_CONV, SSD_XBC), SSD_CONV ** -0.5)
    c_conv_b = nrm(ks[12], (N_ODD, SSD_XBC), 0.02)
    dt0 = jnp.exp(jax.random.uniform(ks[13], (N_ODD, 2, SSD_HEADS), f32, math.log(1e-3), math.log(1e-1)))
    c_dt_bias = dt0 + jnp.log(-jnp.expm1(-dt0))
    c_a_log = jnp.log(jax.random.uniform(ks[14], (N_ODD, 2, SSD_HEADS), f32, 1.0, 16.0))
    c_d_skip = 1.0 + 0.1 * jax.random.normal(ks[15], (N_ODD, SSD_HEADS), f32)
    c_norm_g = gain(ks[16], (N_ODD, SSD_INNER))
    c_w_out = nrm(ks[17], (N_ODD, SSD_INNER, D_MODEL), SSD_INNER ** -0.5)
    ffn_w_up = nrm(ks[18], (DEPTH, D_MODEL, 2 * FFN_DIM), D_MODEL ** -0.5)
    ffn_conv_w = nrm(ks[19], (DEPTH, FFN_CONV, 2 * FFN_DIM), FFN_CONV ** -0.5)
    ffn_conv_b = nrm(ks[20], (DEPTH, 2 * FFN_DIM), 0.02)
    ffn_w_down = nrm(ks[21], (DEPTH, FFN_DIM, D_MODEL), FFN_DIM ** -0.5)
    return {'x': x, 'norm_mix_pre': norm_mix_pre, 'norm_mix_post': norm_mix_post,
            'norm_ffn_pre': norm_ffn_pre, 'norm_ffn_post': norm_ffn_post,
            'ab_w_in': ab_w_in, 'ab_ret_decay_logit': ab_ret_decay_logit, 'ab_ret_gn_g': ab_ret_gn_g,
            'ab_na_rpb': ab_na_rpb, 'ab_w_out': ab_w_out,
            'c_w_in': c_w_in, 'c_conv_w': c_conv_w, 'c_conv_b': c_conv_b, 'c_dt_bias': c_dt_bias,
            'c_a_log': c_a_log, 'c_d_skip': c_d_skip, 'c_norm_g': c_norm_g, 'c_w_out': c_w_out,
            'ffn_w_up': ffn_w_up, 'ffn_conv_w': ffn_conv_w, 'ffn_conv_b': ffn_conv_b, 'ffn_w_down': ffn_w_down}


def _fwd_reference(x, norm_mix_pre, norm_mix_post, norm_ffn_pre, norm_ffn_post,
              ab_w_in, ab_ret_decay_logit, ab_ret_gn_g, ab_na_rpb, ab_w_out,
              c_w_in, c_conv_w, c_conv_b, c_dt_bias, c_a_log, c_d_skip, c_norm_g, c_w_out,
              ffn_w_up, ffn_conv_w, ffn_conv_b, ffn_w_down):
    for layer in range(DEPTH):
        i = layer // 2
        hn = rms_norm(x, norm_mix_pre[layer])
        if layer % 2 == 0:
            m = retention_na_mixer(hn, ab_w_in[i], ab_ret_decay_logit[i], ab_ret_gn_g[i], ab_na_rpb[i], ab_w_out[i])
        else:
            m = ssd_mixer(hn, c_w_in[i], c_conv_w[i], c_conv_b[i], c_dt_bias[i], c_a_log[i],
                          c_d_skip[i], c_norm_g[i], c_w_out[i])
        x = x + rms_norm(m, norm_mix_post[layer])
        f = conv_geglu_ffn(rms_norm(x, norm_ffn_pre[layer]), ffn_w_up[layer], ffn_conv_w[layer],
                           ffn_conv_b[layer], ffn_w_down[layer])
        x = x + rms_norm(f, norm_ffn_post[layer])
    return x


import jax as _jax
import jax.numpy as _jnp

TWIN_FORMAT = 'train_step'
FWD_PARAMS = ['x', 'norm_mix_pre', 'norm_mix_post', 'norm_ffn_pre', 'norm_ffn_post', 'ab_w_in', 'ab_ret_decay_logit', 'ab_ret_gn_g', 'ab_na_rpb', 'ab_w_out', 'c_w_in', 'c_conv_w', 'c_conv_b', 'c_dt_bias', 'c_a_log', 'c_d_skip', 'c_norm_g', 'c_w_out', 'ffn_w_up', 'ffn_conv_w', 'ffn_conv_b', 'ffn_w_down']
TWIN_WEIGHTS = ['norm_mix_pre', 'norm_mix_post', 'norm_ffn_pre', 'norm_ffn_post', 'ab_w_in', 'ab_ret_decay_logit', 'ab_ret_gn_g', 'ab_na_rpb', 'ab_w_out', 'c_w_in', 'c_conv_w', 'c_conv_b', 'c_dt_bias', 'c_a_log', 'c_d_skip', 'c_norm_g', 'c_w_out', 'ffn_w_up', 'ffn_conv_w', 'ffn_conv_b', 'ffn_w_down']
TWIN_DIFF_INPUT = 'x'
TWIN_INPUTS = ['x', 'norm_mix_pre', 'norm_mix_post', 'norm_ffn_pre', 'norm_ffn_post', 'ab_w_in', 'ab_ret_decay_logit', 'ab_ret_gn_g', 'ab_na_rpb', 'ab_w_out', 'c_w_in', 'c_conv_w', 'c_conv_b', 'c_dt_bias', 'c_a_log', 'c_d_skip', 'c_norm_g', 'c_w_out', 'ffn_w_up', 'ffn_conv_w', 'ffn_conv_b', 'ffn_w_down', 'loss_target', 'm_norm_mix_pre', 'm_norm_mix_post', 'm_norm_ffn_pre', 'm_norm_ffn_post', 'm_ab_w_in', 'm_ab_ret_decay_logit', 'm_ab_ret_gn_g', 'm_ab_na_rpb', 'm_ab_w_out', 'm_c_w_in', 'm_c_conv_w', 'm_c_conv_b', 'm_c_dt_bias', 'm_c_a_log', 'm_c_d_skip', 'm_c_norm_g', 'm_c_w_out', 'm_ffn_w_up', 'm_ffn_conv_w', 'm_ffn_conv_b', 'm_ffn_w_down', 'v_norm_mix_pre', 'v_norm_mix_post', 'v_norm_ffn_pre', 'v_norm_ffn_post', 'v_ab_w_in', 'v_ab_ret_decay_logit', 'v_ab_ret_gn_g', 'v_ab_na_rpb', 'v_ab_w_out', 'v_c_w_in', 'v_c_conv_w', 'v_c_conv_b', 'v_c_dt_bias', 'v_c_a_log', 'v_c_d_skip', 'v_c_norm_g', 'v_c_w_out', 'v_ffn_w_up', 'v_ffn_conv_w', 'v_ffn_conv_b', 'v_ffn_w_down']
TWIN_OUTPUTS = ['loss', 'grad_x', 'grad_norm_mix_pre', 'grad_norm_mix_post', 'grad_norm_ffn_pre', 'grad_norm_ffn_post', 'grad_ab_w_in', 'grad_ab_ret_decay_logit', 'grad_ab_ret_gn_g', 'grad_ab_na_rpb', 'grad_ab_w_out', 'grad_c_w_in', 'grad_c_conv_w', 'grad_c_conv_b', 'grad_c_dt_bias', 'grad_c_a_log', 'grad_c_d_skip', 'grad_c_norm_g', 'grad_c_w_out', 'grad_ffn_w_up', 'grad_ffn_conv_w', 'grad_ffn_conv_b', 'grad_ffn_w_down', 'delta_norm_mix_pre', 'delta_norm_mix_post', 'delta_norm_ffn_pre', 'delta_norm_ffn_post', 'delta_ab_w_in', 'delta_ab_ret_decay_logit', 'delta_ab_ret_gn_g', 'delta_ab_na_rpb', 'delta_ab_w_out', 'delta_c_w_in', 'delta_c_conv_w', 'delta_c_conv_b', 'delta_c_dt_bias', 'delta_c_a_log', 'delta_c_d_skip', 'delta_c_norm_g', 'delta_c_w_out', 'delta_ffn_w_up', 'delta_ffn_conv_w', 'delta_ffn_conv_b', 'delta_ffn_w_down', 'new_m_norm_mix_pre', 'new_m_norm_mix_post', 'new_m_norm_ffn_pre', 'new_m_norm_ffn_post', 'new_m_ab_w_in', 'new_m_ab_ret_decay_logit', 'new_m_ab_ret_gn_g', 'new_m_ab_na_rpb', 'new_m_ab_w_out', 'new_m_c_w_in', 'new_m_c_conv_w', 'new_m_c_conv_b', 'new_m_c_dt_bias', 'new_m_c_a_log', 'new_m_c_d_skip', 'new_m_c_norm_g', 'new_m_c_w_out', 'new_m_ffn_w_up', 'new_m_ffn_conv_w', 'new_m_ffn_conv_b', 'new_m_ffn_w_down', 'new_v_norm_mix_pre', 'new_v_norm_mix_post', 'new_v_norm_ffn_pre', 'new_v_norm_ffn_post', 'new_v_ab_w_in', 'new_v_ab_ret_decay_logit', 'new_v_ab_ret_gn_g', 'new_v_ab_na_rpb', 'new_v_ab_w_out', 'new_v_c_w_in', 'new_v_c_conv_w', 'new_v_c_conv_b', 'new_v_c_dt_bias', 'new_v_c_a_log', 'new_v_c_d_skip', 'new_v_c_norm_g', 'new_v_c_w_out', 'new_v_ffn_w_up', 'new_v_ffn_conv_w', 'new_v_ffn_conv_b', 'new_v_ffn_w_down']
TWIN_LEAF_KINDS = {'loss': 'loss', 'grad_x': 'grad_x', 'grad_norm_mix_pre': 'grad_w', 'grad_norm_mix_post': 'grad_w', 'grad_norm_ffn_pre': 'grad_w', 'grad_norm_ffn_post': 'grad_w', 'grad_ab_w_in': 'grad_w', 'grad_ab_ret_decay_logit': 'grad_w', 'grad_ab_ret_gn_g': 'grad_w', 'grad_ab_na_rpb': 'grad_w', 'grad_ab_w_out': 'grad_w', 'grad_c_w_in': 'grad_w', 'grad_c_conv_w': 'grad_w', 'grad_c_conv_b': 'grad_w', 'grad_c_dt_bias': 'grad_w', 'grad_c_a_log': 'grad_w', 'grad_c_d_skip': 'grad_w', 'grad_c_norm_g': 'grad_w', 'grad_c_w_out': 'grad_w', 'grad_ffn_w_up': 'grad_w', 'grad_ffn_conv_w': 'grad_w', 'grad_ffn_conv_b': 'grad_w', 'grad_ffn_w_down': 'grad_w', 'delta_norm_mix_pre': 'delta_w', 'delta_norm_mix_post': 'delta_w', 'delta_norm_ffn_pre': 'delta_w', 'delta_norm_ffn_post': 'delta_w', 'delta_ab_w_in': 'delta_w', 'delta_ab_ret_decay_logit': 'delta_w', 'delta_ab_ret_gn_g': 'delta_w', 'delta_ab_na_rpb': 'delta_w', 'delta_ab_w_out': 'delta_w', 'delta_c_w_in': 'delta_w', 'delta_c_conv_w': 'delta_w', 'delta_c_conv_b': 'delta_w', 'delta_c_dt_bias': 'delta_w', 'delta_c_a_log': 'delta_w', 'delta_c_d_skip': 'delta_w', 'delta_c_norm_g': 'delta_w', 'delta_c_w_out': 'delta_w', 'delta_ffn_w_up': 'delta_w', 'delta_ffn_conv_w': 'delta_w', 'delta_ffn_conv_b': 'delta_w', 'delta_ffn_w_down': 'delta_w', 'new_m_norm_mix_pre': 'new_m', 'new_m_norm_mix_post': 'new_m', 'new_m_norm_ffn_pre': 'new_m', 'new_m_norm_ffn_post': 'new_m', 'new_m_ab_w_in': 'new_m', 'new_m_ab_ret_decay_logit': 'new_m', 'new_m_ab_ret_gn_g': 'new_m', 'new_m_ab_na_rpb': 'new_m', 'new_m_ab_w_out': 'new_m', 'new_m_c_w_in': 'new_m', 'new_m_c_conv_w': 'new_m', 'new_m_c_conv_b': 'new_m', 'new_m_c_dt_bias': 'new_m', 'new_m_c_a_log': 'new_m', 'new_m_c_d_skip': 'new_m', 'new_m_c_norm_g': 'new_m', 'new_m_c_w_out': 'new_m', 'new_m_ffn_w_up': 'new_m', 'new_m_ffn_conv_w': 'new_m', 'new_m_ffn_conv_b': 'new_m', 'new_m_ffn_w_down': 'new_m', 'new_v_norm_mix_pre': 'new_v', 'new_v_norm_mix_post': 'new_v', 'new_v_norm_ffn_pre': 'new_v', 'new_v_norm_ffn_post': 'new_v', 'new_v_ab_w_in': 'new_v', 'new_v_ab_ret_decay_logit': 'new_v', 'new_v_ab_ret_gn_g': 'new_v', 'new_v_ab_na_rpb': 'new_v', 'new_v_ab_w_out': 'new_v', 'new_v_c_w_in': 'new_v', 'new_v_c_conv_w': 'new_v', 'new_v_c_conv_b': 'new_v', 'new_v_c_dt_bias': 'new_v', 'new_v_c_a_log': 'new_v', 'new_v_c_d_skip': 'new_v', 'new_v_c_norm_g': 'new_v', 'new_v_c_w_out': 'new_v', 'new_v_ffn_w_up': 'new_v', 'new_v_ffn_conv_w': 'new_v', 'new_v_ffn_conv_b': 'new_v', 'new_v_ffn_w_down': 'new_v'}


def _forward(args):
    return _fwd_reference(*[args[k] for k in FWD_PARAMS])


def _output_shape():
    def fwd():
        inp = _fwd_setup_inputs(0)
        return _fwd_reference(*[inp[k] for k in FWD_PARAMS])
    out = _jax.eval_shape(fwd)
    return out.shape, out.dtype

N_MICROBATCH = 1
ADAM_LR = 0.001
ADAM_B1 = 0.9
ADAM_B2 = 0.999
ADAM_EPS = 1e-08
ADAM_WD = 0.01
ADAM_STEP = 10
PER_EXAMPLE_BATCH_AXIS = {'x': 0, 'loss_target': 0}
SHARED_INPUTS = []
_WEIGHT_DTYPES = {'norm_mix_pre': _jnp.float32, 'norm_mix_post': _jnp.float32, 'norm_ffn_pre': _jnp.float32, 'norm_ffn_post': _jnp.float32, 'ab_w_in': _jnp.float32, 'ab_ret_decay_logit': _jnp.float32, 'ab_ret_gn_g': _jnp.float32, 'ab_na_rpb': _jnp.float32, 'ab_w_out': _jnp.float32, 'c_w_in': _jnp.float32, 'c_conv_w': _jnp.float32, 'c_conv_b': _jnp.float32, 'c_dt_bias': _jnp.float32, 'c_a_log': _jnp.float32, 'c_d_skip': _jnp.float32, 'c_norm_g': _jnp.float32, 'c_w_out': _jnp.float32, 'ffn_w_up': _jnp.float32, 'ffn_conv_w': _jnp.float32, 'ffn_conv_b': _jnp.float32, 'ffn_w_down': _jnp.float32}
MOMENT_SCALE = {'norm_mix_pre': 4.196287e+00, 'norm_mix_post': 6.350228e+01, 'norm_ffn_pre': 2.467460e+00, 'norm_ffn_post': 6.367579e+01, 'ab_w_in': 2.901296e+00, 'ab_ret_decay_logit': 1.688530e+01, 'ab_ret_gn_g': 3.779567e+00, 'ab_na_rpb': 4.126890e-01, 'ab_w_out': 3.109718e+00, 'c_w_in': 1.210327e+00, 'c_conv_w': 1.710832e+00, 'c_conv_b': 6.603173e+00, 'c_dt_bias': 2.829489e+00, 'c_a_log': 7.688706e+00, 'c_d_skip': 7.108698e+00, 'c_norm_g': 3.138519e+00, 'c_w_out': 4.680627e+00, 'ffn_w_up': 1.057210e+00, 'ffn_conv_w': 1.109515e+00, 'ffn_conv_b': 4.751359e+00, 'ffn_w_down': 1.964145e+00}


def _to_microbatches(a, axis):
    t = _jnp.moveaxis(a, axis, 0)
    t = t.reshape((N_MICROBATCH, t.shape[0] // N_MICROBATCH) + t.shape[1:])
    return _jnp.moveaxis(t, 1, axis + 1)


def setup_inputs(seed: int = 0) -> dict:
    inp = _fwd_setup_inputs(seed)
    key = _jax.random.fold_in(_jax.random.key(seed), 7919)
    shape, _ = _output_shape()
    out = dict(inp)
    out["loss_target"] = _jax.random.normal(_jax.random.fold_in(key, 0), shape, _jnp.float32)
    for i, name in enumerate(TWIN_WEIGHTS):
        w = inp[name].astype(_jnp.float32)
        if MOMENT_SCALE is None:
            s = _jnp.sqrt(_jnp.mean(_jnp.square(w)) + 1e-30)
        else:
            s = MOMENT_SCALE[name]
        km, kv = _jax.random.split(_jax.random.fold_in(key, i + 1))
        out[name] = w
        out["m_" + name] = s * _jax.random.normal(km, w.shape, _jnp.float32)
        out["v_" + name] = (s * s) * _jax.random.uniform(kv, w.shape, _jnp.float32, 0.5, 1.5)
    if N_MICROBATCH > 1:
        for name, axis in PER_EXAMPLE_BATCH_AXIS.items():
            out[name] = _to_microbatches(out[name], axis)
    return {'x': out['x'], 'norm_mix_pre': out['norm_mix_pre'], 'norm_mix_post': out['norm_mix_post'], 'norm_ffn_pre': out['norm_ffn_pre'], 'norm_ffn_post': out['norm_ffn_post'], 'ab_w_in': out['ab_w_in'], 'ab_ret_decay_logit': out['ab_ret_decay_logit'], 'ab_ret_gn_g': out['ab_ret_gn_g'], 'ab_na_rpb': out['ab_na_rpb'], 'ab_w_out': out['ab_w_out'], 'c_w_in': out['c_w_in'], 'c_conv_w': out['c_conv_w'], 'c_conv_b': out['c_conv_b'], 'c_dt_bias': out['c_dt_bias'], 'c_a_log': out['c_a_log'], 'c_d_skip': out['c_d_skip'], 'c_norm_g': out['c_norm_g'], 'c_w_out': out['c_w_out'], 'ffn_w_up': out['ffn_w_up'], 'ffn_conv_w': out['ffn_conv_w'], 'ffn_conv_b': out['ffn_conv_b'], 'ffn_w_down': out['ffn_w_down'], 'loss_target': out['loss_target'], 'm_norm_mix_pre': out['m_norm_mix_pre'], 'm_norm_mix_post': out['m_norm_mix_post'], 'm_norm_ffn_pre': out['m_norm_ffn_pre'], 'm_norm_ffn_post': out['m_norm_ffn_post'], 'm_ab_w_in': out['m_ab_w_in'], 'm_ab_ret_decay_logit': out['m_ab_ret_decay_logit'], 'm_ab_ret_gn_g': out['m_ab_ret_gn_g'], 'm_ab_na_rpb': out['m_ab_na_rpb'], 'm_ab_w_out': out['m_ab_w_out'], 'm_c_w_in': out['m_c_w_in'], 'm_c_conv_w': out['m_c_conv_w'], 'm_c_conv_b': out['m_c_conv_b'], 'm_c_dt_bias': out['m_c_dt_bias'], 'm_c_a_log': out['m_c_a_log'], 'm_c_d_skip': out['m_c_d_skip'], 'm_c_norm_g': out['m_c_norm_g'], 'm_c_w_out': out['m_c_w_out'], 'm_ffn_w_up': out['m_ffn_w_up'], 'm_ffn_conv_w': out['m_ffn_conv_w'], 'm_ffn_conv_b': out['m_ffn_conv_b'], 'm_ffn_w_down': out['m_ffn_w_down'], 'v_norm_mix_pre': out['v_norm_mix_pre'], 'v_norm_mix_post': out['v_norm_mix_post'], 'v_norm_ffn_pre': out['v_norm_ffn_pre'], 'v_norm_ffn_post': out['v_norm_ffn_post'], 'v_ab_w_in': out['v_ab_w_in'], 'v_ab_ret_decay_logit': out['v_ab_ret_decay_logit'], 'v_ab_ret_gn_g': out['v_ab_ret_gn_g'], 'v_ab_na_rpb': out['v_ab_na_rpb'], 'v_ab_w_out': out['v_ab_w_out'], 'v_c_w_in': out['v_c_w_in'], 'v_c_conv_w': out['v_c_conv_w'], 'v_c_conv_b': out['v_c_conv_b'], 'v_c_dt_bias': out['v_c_dt_bias'], 'v_c_a_log': out['v_c_a_log'], 'v_c_d_skip': out['v_c_d_skip'], 'v_c_norm_g': out['v_c_norm_g'], 'v_c_w_out': out['v_c_w_out'], 'v_ffn_w_up': out['v_ffn_w_up'], 'v_ffn_conv_w': out['v_ffn_conv_w'], 'v_ffn_conv_b': out['v_ffn_conv_b'], 'v_ffn_w_down': out['v_ffn_w_down']}


def _loss(weights, diff, rest, loss_target):
    with _jax.named_scope("forward"):
        args = {**rest, TWIN_DIFF_INPUT: diff, **{k: w.astype(_WEIGHT_DTYPES[k]) for k, w in weights.items()}}
        y = _forward(args)
    with _jax.named_scope("loss_head"):
        err = _jnp.square(y.astype(_jnp.float32) - loss_target)
        return 0.5 * _jnp.sum(_jnp.mean(err, axis=-1)) if err.ndim else 0.5 * err


def _adamw(w, g, m, v):
    m = ADAM_B1 * m + (1.0 - ADAM_B1) * g
    v = ADAM_B2 * v + (1.0 - ADAM_B2) * _jnp.square(g)
    m_hat = m / (1.0 - ADAM_B1 ** ADAM_STEP)
    v_hat = v / (1.0 - ADAM_B2 ** ADAM_STEP)
    delta = -ADAM_LR * (m_hat / (_jnp.sqrt(v_hat) + ADAM_EPS) + ADAM_WD * w)
    return delta, m, v


def reference(x, norm_mix_pre, norm_mix_post, norm_ffn_pre, norm_ffn_post, ab_w_in, ab_ret_decay_logit, ab_ret_gn_g, ab_na_rpb, ab_w_out, c_w_in, c_conv_w, c_conv_b, c_dt_bias, c_a_log, c_d_skip, c_norm_g, c_w_out, ffn_w_up, ffn_conv_w, ffn_conv_b, ffn_w_down, loss_target, m_norm_mix_pre, m_norm_mix_post, m_norm_ffn_pre, m_norm_ffn_post, m_ab_w_in, m_ab_ret_decay_logit, m_ab_ret_gn_g, m_ab_na_rpb, m_ab_w_out, m_c_w_in, m_c_conv_w, m_c_conv_b, m_c_dt_bias, m_c_a_log, m_c_d_skip, m_c_norm_g, m_c_w_out, m_ffn_w_up, m_ffn_conv_w, m_ffn_conv_b, m_ffn_w_down, v_norm_mix_pre, v_norm_mix_post, v_norm_ffn_pre, v_norm_ffn_post, v_ab_w_in, v_ab_ret_decay_logit, v_ab_ret_gn_g, v_ab_na_rpb, v_ab_w_out, v_c_w_in, v_c_conv_w, v_c_conv_b, v_c_dt_bias, v_c_a_log, v_c_d_skip, v_c_norm_g, v_c_w_out, v_ffn_w_up, v_ffn_conv_w, v_ffn_conv_b, v_ffn_w_down):
    given = dict(x=x, norm_mix_pre=norm_mix_pre, norm_mix_post=norm_mix_post, norm_ffn_pre=norm_ffn_pre, norm_ffn_post=norm_ffn_post, ab_w_in=ab_w_in, ab_ret_decay_logit=ab_ret_decay_logit, ab_ret_gn_g=ab_ret_gn_g, ab_na_rpb=ab_na_rpb, ab_w_out=ab_w_out, c_w_in=c_w_in, c_conv_w=c_conv_w, c_conv_b=c_conv_b, c_dt_bias=c_dt_bias, c_a_log=c_a_log, c_d_skip=c_d_skip, c_norm_g=c_norm_g, c_w_out=c_w_out, ffn_w_up=ffn_w_up, ffn_conv_w=ffn_conv_w, ffn_conv_b=ffn_conv_b, ffn_w_down=ffn_w_down, loss_target=loss_target, m_norm_mix_pre=m_norm_mix_pre, m_norm_mix_post=m_norm_mix_post, m_norm_ffn_pre=m_norm_ffn_pre, m_norm_ffn_post=m_norm_ffn_post, m_ab_w_in=m_ab_w_in, m_ab_ret_decay_logit=m_ab_ret_decay_logit, m_ab_ret_gn_g=m_ab_ret_gn_g, m_ab_na_rpb=m_ab_na_rpb, m_ab_w_out=m_ab_w_out, m_c_w_in=m_c_w_in, m_c_conv_w=m_c_conv_w, m_c_conv_b=m_c_conv_b, m_c_dt_bias=m_c_dt_bias, m_c_a_log=m_c_a_log, m_c_d_skip=m_c_d_skip, m_c_norm_g=m_c_norm_g, m_c_w_out=m_c_w_out, m_ffn_w_up=m_ffn_w_up, m_ffn_conv_w=m_ffn_conv_w, m_ffn_conv_b=m_ffn_conv_b, m_ffn_w_down=m_ffn_w_down, v_norm_mix_pre=v_norm_mix_pre, v_norm_mix_post=v_norm_mix_post, v_norm_ffn_pre=v_norm_ffn_pre, v_norm_ffn_post=v_norm_ffn_post, v_ab_w_in=v_ab_w_in, v_ab_ret_decay_logit=v_ab_ret_decay_logit, v_ab_ret_gn_g=v_ab_ret_gn_g, v_ab_na_rpb=v_ab_na_rpb, v_ab_w_out=v_ab_w_out, v_c_w_in=v_c_w_in, v_c_conv_w=v_c_conv_w, v_c_conv_b=v_c_conv_b, v_c_dt_bias=v_c_dt_bias, v_c_a_log=v_c_a_log, v_c_d_skip=v_c_d_skip, v_c_norm_g=v_c_norm_g, v_c_w_out=v_c_w_out, v_ffn_w_up=v_ffn_w_up, v_ffn_conv_w=v_ffn_conv_w, v_ffn_conv_b=v_ffn_conv_b, v_ffn_w_down=v_ffn_w_down)
    weights = {n: given[n] for n in TWIN_WEIGHTS}
    shared = {n: given[n] for n in SHARED_INPUTS}
    per_example = {n: given[n] for n in ['x']}
    grad_fn = _jax.value_and_grad(_loss, argnums=(0, 1))

    def one_microbatch(ex, loss_target):
        ex = dict(ex)
        diff = ex.pop(TWIN_DIFF_INPUT)
        return grad_fn(weights, diff, {**shared, **ex}, loss_target)

    if N_MICROBATCH == 1:
        loss, (grad_w, grad_x) = one_microbatch(per_example, given["loss_target"])
    else:
        def body(carry, xs):
            loss_sum, grad_sum = carry
            l_k, (gw_k, gx_k) = one_microbatch(xs[0], xs[1])
            with _jax.named_scope("update"):
                return (loss_sum + l_k, _jax.tree.map(_jnp.add, grad_sum, gw_k)), gx_k

        init = (_jnp.zeros((), _jnp.float32), _jax.tree.map(_jnp.zeros_like, weights))
        (loss, grad_w), grad_x = _jax.lax.scan(body, init, (per_example, given["loss_target"]))
    with _jax.named_scope("update"):
        delta_w, new_m, new_v = {}, {}, {}
        for n in TWIN_WEIGHTS:
            delta_w[n], new_m[n], new_v[n] = _adamw(weights[n], grad_w[n], given["m_" + n], given["v_" + n])
    return (loss, grad_x, *[grad_w[n] for n in TWIN_WEIGHTS], *[delta_w[n] for n in TWIN_WEIGHTS],
            *[new_m[n] for n in TWIN_WEIGHTS], *[new_v[n] for n in TWIN_WEIGHTS])
```

```python
import functools
import numpy as np
import jax
import jax.numpy as jnp
from jax import lax
from jax.experimental import pallas as pl
from jax.experimental.pallas import tpu as pltpu

f32, bf16 = jnp.float32, jnp.bfloat16
S = jax.ShapeDtypeStruct
HI = lax.Precision.HIGHEST

D = 1024
DEPTH = 4
GRID_W = 64
CHUNK = 128
EPS = 1e-6
RH, RDH, RW = 8, 64, 512
NAH, NADH, NAW = 8, 64, 512
NA_WR, NA_WC = 8, 16
NA_QROWS = 8
NA_KROWS = 16
SSD_INNER, SSD_HD, SSD_H, SSD_G, SSD_HPG, SSD_N, SSD_CONV = 2048, 64, 32, 4, 8, 128, 5
SSD_XBC = SSD_INNER + 2 * SSD_G * SSD_N
FFN, FFN_CONV = 2816, 3
FFN_TC = 512
ROPE_BASE = 10000.0
LR, B1, B2, AEPS, WD, STEP = 0.001, 0.9, 0.999, 1e-08, 0.01, 10
NDEV = 8
LANES = 128
VMEM_LIMIT = 56 * 1024 * 1024

NT = (((1,), (1,)), ((), ()))
TN = (((0,), (0,)), ((), ()))

SHARDED = [("ab_w_in", 2), ("ab_w_out", 1), ("c_w_in", 2), ("c_w_out", 1), ("ffn_w_up", 2), ("ffn_w_down", 1),
           ("c_conv_w", 2), ("c_conv_b", 1), ("c_norm_g", 1), ("ffn_conv_w", 2)]
N_BIG = 6
REPLICATED = ["norm_mix_pre", "norm_mix_post", "norm_ffn_pre", "norm_ffn_post", "ab_ret_decay_logit", "ab_ret_gn_g",
              "ab_na_rpb", "c_dt_bias", "c_a_log", "c_d_skip", "ffn_conv_b"]
WEIGHTS = ["norm_mix_pre", "norm_mix_post", "norm_ffn_pre", "norm_ffn_post", "ab_w_in", "ab_ret_decay_logit",
           "ab_ret_gn_g", "ab_na_rpb", "ab_w_out", "c_w_in", "c_conv_w", "c_conv_b", "c_dt_bias", "c_a_log", "c_d_skip",
           "c_norm_g", "c_w_out", "ffn_w_up", "ffn_conv_w", "ffn_conv_b", "ffn_w_down"]


def _params(sem=None):
    return pltpu.CompilerParams(dimension_semantics=sem, vmem_limit_bytes=VMEM_LIMIT)


def _mm_nn(a, w, *, name, tm=512, tn=512, out_dtype=f32):
    M, K = a.shape
    N = w.shape[1]
    tn = min(tn, N)

    def body(a_ref, w_ref, o_ref):
        o_ref[...] = jnp.dot(a_ref[...], w_ref[...], preferred_element_type=f32).astype(o_ref.dtype)

    return pl.pallas_call(
        body, name=name, grid=(M // tm, N // tn),
        in_specs=[pl.BlockSpec((tm, K), lambda i, j: (i, 0)), pl.BlockSpec((K, tn), lambda i, j: (0, j))],
        out_specs=pl.BlockSpec((tm, tn), lambda i, j: (i, j)),
        out_shape=S((M, N), out_dtype), compiler_params=_params(("parallel", "parallel")))(a, w)


def _mm_nt(dy, w, *, name, add=None, tm=512, tk=512):
    M, N = dy.shape
    K = w.shape[0]
    tk = next(t for t in (tk, 256, 128, K) if K % t == 0)

    def body(*refs):
        if add is None:
            d_ref, w_ref, o_ref = refs
            o_ref[...] = lax.dot_general(d_ref[...], w_ref[...], NT, preferred_element_type=f32)
        else:
            d_ref, w_ref, a_ref, o_ref = refs
            o_ref[...] = lax.dot_general(d_ref[...], w_ref[...], NT, preferred_element_type=f32) + a_ref[...]

    in_specs = [pl.BlockSpec((tm, N), lambda i, j: (i, 0)), pl.BlockSpec((tk, N), lambda i, j: (j, 0))]
    args = [dy, w]
    if add is not None:
        in_specs.append(pl.BlockSpec((tm, tk), lambda i, j: (i, j)))
        args.append(add)
    return pl.pallas_call(
        body, name=name, grid=(M // tm, K // tk), in_specs=in_specs,
        out_specs=pl.BlockSpec((tm, tk), lambda i, j: (i, j)),
        out_shape=S((M, K), f32), compiler_params=_params(("parallel", "parallel")))(*args)


def _mm_tn(a, dy, *, name, tt=1024):
    M, K = a.shape
    N = dy.shape[1]
    tk = K if K <= 1024 else (1024 if K % 1024 == 0 else K // 2)
    tn = min(512, N)
    tt = min(tt, M)

    def body(a_ref, d_ref, o_ref):
        t = pl.program_id(2)
        part = lax.dot_general(a_ref[...], d_ref[...], TN, preferred_element_type=f32)

        @pl.when(t == 0)
        def _():
            o_ref[...] = part

        @pl.when(t > 0)
        def _():
            o_ref[...] += part

    return pl.pallas_call(
        body, name=name, grid=(K // tk, N // tn, M // tt),
        in_specs=[pl.BlockSpec((tt, tk), lambda k, n, t: (t, k)), pl.BlockSpec((tt, tn), lambda k, n, t: (t, n))],
        out_specs=pl.BlockSpec((tk, tn), lambda k, n, t: (k, n)),
        out_shape=S((K, N), f32), compiler_params=_params(("parallel", "parallel", "arbitrary")))(a, dy)


def _tile_spec(tm, width, base):
    return pl.BlockSpec((tm, width), lambda j, i: (i, base + j))


def _par_spec(width, base):
    return pl.BlockSpec((1, width), lambda j, i: (0, base + j))


def _full_spec(a):
    nd = a.ndim
    return pl.BlockSpec(a.shape, lambda j, i: (0,) * nd)


def _rowwise(name, f, tiles, ctiles, params, consts, outs, *, tm, J=1):
    T = tiles[0][0].shape[0]
    nt, nct, npar, nc = len(tiles), len(ctiles), len(params), len(consts)

    def body(*refs):
        tv = [r[...].astype(f32) for r in refs[:nt + nct]]
        pv = [r[...] for r in refs[nt + nct:nt + nct + npar + nc]]
        res = f(*tv, *pv)
        for o, v in zip(refs[nt + nct + npar + nc:], res):
            o[...] = v.astype(o.dtype)

    in_specs = ([_tile_spec(tm, w, b) for _, w, b in tiles + ctiles] + [_par_spec(w, b) for _, w, b in params]
                + [_full_spec(c) for c in consts])
    return pl.pallas_call(
        body, name=name, grid=(J, T // tm), in_specs=in_specs,
        out_specs=[_tile_spec(tm, w, 0) for w, _ in outs],
        out_shape=[S((T, J * w), dt) for w, dt in outs],
        compiler_params=_params(("parallel", "parallel")))(
            *[a for a, _, _ in tiles + ctiles], *[a for a, _, _ in params], *consts)


def _rowwise_bwd(name, f, tiles, ctiles, params, consts, douts, dtile_dtypes, *, tm, J=1):
    T = tiles[0][0].shape[0]
    nt, nct, npar, nc, nd = len(tiles), len(ctiles), len(params), len(consts), len(douts)

    def body(*refs):
        i = pl.program_id(1)
        k = 0
        tv = [r[...].astype(f32) for r in refs[k:k + nt]]; k += nt
        cv = [r[...].astype(f32) for r in refs[k:k + nct]]; k += nct
        pv = [r[...] for r in refs[k:k + npar]]; k += npar
        kv = [r[...] for r in refs[k:k + nc]]; k += nc
        dv = [r[...].astype(f32) for r in refs[k:k + nd]]; k += nd
        dt_refs = refs[k:k + nt]; k += nt
        dp_refs = refs[k:k + npar]
        _, vjp = jax.vjp(lambda tv_, pv_: tuple(f(*tv_, *cv, *pv_, *kv)), tv, pv)
        dts, dps = vjp(tuple(dv))
        for r, g in zip(dt_refs, dts):
            r[...] = g.astype(r.dtype)
        for r, g in zip(dp_refs, dps):
            @pl.when(i == 0)
            def _(r=r, g=g):
                r[...] = g

            @pl.when(i > 0)
            def _(r=r, g=g):
                r[...] += g

    in_specs = ([_tile_spec(tm, w, b) for _, w, b in tiles + ctiles] + [_par_spec(w, b) for _, w, b in params]
                + [_full_spec(c) for c in consts] + [_tile_spec(tm, w, b) for _, w, b in douts])
    res = pl.pallas_call(
        body, name=name, grid=(J, T // tm), in_specs=in_specs,
        out_specs=[_tile_spec(tm, w, 0) for _, w, _ in tiles] + [_par_spec(w, b) for _, w, b in params],
        out_shape=[S((T, J * w), dt) for (_, w, _), dt in zip(tiles, dtile_dtypes)] + [S(a.shape, f32) for a, _, _ in params],
        compiler_params=_params(("parallel", "arbitrary")))(
            *[a for a, _, _ in tiles + ctiles], *[a for a, _, _ in params], *consts, *[a for a, _, _ in douts])
    return res[:nt], res[nt:]


def _rms(x, g):
    return x * lax.rsqrt(jnp.mean(x * x, axis=-1, keepdims=True) + EPS) * g


def _f_first(x, g1):
    return (_rms(x, g1),)


def _f_first_bwd(x, g1):
    return (x, _rms(x, g1))


def _f_mid(x, m, g2, g3):
    x1 = x + _rms(m, g2)
    return (x1, _rms(x1, g3))


def _f_end(x1, fo, g4, g1n):
    x2 = x1 + _rms(fo, g4)
    return (x2, _rms(x2, g1n))


def _f_last(x1, fo, g4):
    return (x1 + _rms(fo, g4),)


@jax.custom_vjp
def _swap_halves(x):
    c = x.shape[1]
    lane = lax.broadcasted_iota(jnp.int32, x.shape, 1) % RDH
    return jnp.where(lane < RDH // 2, pltpu.roll(x, c - RDH // 2, axis=1), pltpu.roll(x, RDH // 2, axis=1))


_swap_halves.defvjp(lambda x: (_swap_halves(x), None), lambda _, g: (_swap_halves(g),))


def _f_rprep(rq, rk, cos, sin):
    rot = lambda t: t * cos + _swap_halves(t) * sin
    return (rot(rq), rot(rk) * (RDH ** -0.5))


def _f_rpost(yf, yb, rg, gn, gavg):
    y = yf + yb
    mu = jnp.dot(y, gavg, precision=HI, preferred_element_type=f32)
    yc = y - mu
    var = jnp.dot(yc * yc, gavg, precision=HI, preferred_element_type=f32)
    return (jax.nn.silu(rg) * (yc * lax.rsqrt(var + EPS) * gn),)


def _f_sprep(xs, dtr, dtb, alog, ex0, ex1):
    dt = jax.nn.softplus(dtr + dtb)
    la = dt * (-jnp.exp(alog))
    e0 = jnp.dot(dt, ex0, precision=HI, preferred_element_type=f32)
    e1 = jnp.dot(dt, ex1, precision=HI, preferred_element_type=f32)
    return (xs * e0, xs * e1, la)


def _f_spost(yf, yb, xs, z, dsk, ng):
    y = (yf + yb + xs * dsk) * jax.nn.silu(z)
    y = y * lax.rsqrt(jnp.mean(y * y, axis=-1, keepdims=True) + EPS)
    return (y * ng,)


def _loss_call(y, tgt, *, tm=256):
    T = y.shape[0]

    def body(y_ref, t_ref, dy_ref, l_ref):
        i = pl.program_id(0)
        e = y_ref[...] - t_ref[...]
        dy_ref[...] = e * (1.0 / D)
        part = jnp.zeros((8, LANES), f32) + 0.5 * jnp.sum(jnp.mean(e * e, axis=-1, keepdims=True))

        @pl.when(i == 0)
        def _():
            l_ref[...] = part

        @pl.when(i > 0)
        def _():
            l_ref[...] += part

    return pl.pallas_call(
        body, name="loss_head", grid=(T // tm,),
        in_specs=[pl.BlockSpec((tm, D), lambda i: (i, 0))] * 2,
        out_specs=[pl.BlockSpec((tm, D), lambda i: (i, 0)), pl.BlockSpec((8, LANES), lambda i: (0, 0))],
        out_shape=[S((T, D), f32), S((8, LANES), f32)], compiler_params=_params(("arbitrary",)))(y, tgt)


def _colsum(x, *, name, tm=512):
    T, C = x.shape

    def body(x_ref, o_ref):
        i = pl.program_id(0)
        part = jnp.sum(x_ref[...], axis=0, keepdims=True)

        @pl.when(i == 0)
        def _():
            o_ref[...] = part

        @pl.when(i > 0)
        def _():
            o_ref[...] += part

    return pl.pallas_call(
        body, name=name, grid=(T // tm,), in_specs=[pl.BlockSpec((tm, C), lambda i: (i, 0))],
        out_specs=pl.BlockSpec((1, C), lambda i: (0, 0)), out_shape=S((1, C), f32),
        compiler_params=_params(("arbitrary",)))(x)


def _scan_step(h, q, k, v, a, rev):
    L = q.shape[0]
    Hg = v.shape[0]
    ii = lax.broadcasted_iota(jnp.int32, (L, L), 0)
    jj = lax.broadcasted_iota(jnp.int32, (L, L), 1)
    if rev:
        tri, tri_t, dmask = (jj >= ii), (ii >= jj), (jj > ii)
    else:
        tri, tri_t, dmask = (jj <= ii), (ii <= jj), (jj <= ii)
    cs = jnp.dot(tri.astype(f32), a, precision=HI, preferred_element_type=f32)
    cs_t = lax.dot_general(a, tri_t.astype(f32), TN, precision=HI, preferred_element_type=f32)
    tot = jnp.sum(a, axis=0, keepdims=True)
    qk = lax.dot_general(q, k, NT, preferred_element_type=f32)
    hs, ys = [], []
    for hh in range(Hg):
        c_col = cs[:, hh:hh + 1]
        dec = jnp.exp(jnp.where(dmask, c_col - cs_t[hh:hh + 1, :], -1e30))
        y = jnp.dot(qk * dec, v[hh], preferred_element_type=f32)
        y = y + jnp.dot(q, h[hh], preferred_element_type=f32) * jnp.exp(c_col)
        t_all = tot[:, hh:hh + 1]
        hn = h[hh] * jnp.exp(t_all) + lax.dot_general(k, v[hh] * jnp.exp(t_all - c_col), TN, preferred_element_type=f32)
        hs.append(hn)
        ys.append(y)
    return jnp.stack(hs), jnp.stack(ys)


def _scan_specs(G, T, N, Hg, P, Ha, cm):
    qs = pl.BlockSpec((1, CHUNK, N), lambda g, c: (g, cm(c), 0))
    vs = pl.BlockSpec((Hg, CHUNK, P), lambda g, c: (g, cm(c), 0))
    as_ = pl.BlockSpec((1, CHUNK, Ha), lambda g, c: (g, cm(c), 0))
    hs = pl.BlockSpec((1, 1, Hg, N, P), lambda g, c: (g, cm(c), 0, 0, 0))
    return qs, vs, as_, hs


def _scan_fwd(q, k, v, a, *, rev, name):
    G, T, N = q.shape
    Ht, _, P = v.shape
    Hg, Ha, NC = Ht // G, a.shape[2], T // CHUNK
    cm = (lambda c: NC - 1 - c) if rev else (lambda c: c)
    qs, vs, as_, hs = _scan_specs(G, T, N, Hg, P, Ha, cm)

    def body(q_ref, k_ref, v_ref, a_ref, y_ref, hs_ref, h_scr):
        @pl.when(pl.program_id(1) == 0)
        def _():
            h_scr[...] = jnp.zeros_like(h_scr)

        h = h_scr[...]
        hs_ref[0, 0] = h
        hn, y = _scan_step(h, q_ref[0], k_ref[0], v_ref[...], a_ref[0], rev)
        y_ref[...] = y
        h_scr[...] = hn

    return pl.pallas_call(
        body, name=name, grid=(G, NC), in_specs=[qs, qs, vs, as_], out_specs=[vs, hs],
        out_shape=[S((Ht, T, P), f32), S((G, NC, Hg, N, P), f32)],
        scratch_shapes=[pltpu.VMEM((Hg, N, P), f32)],
        compiler_params=_params(("parallel", "arbitrary")))(q, k, v, a)


def _scan_bwd(q, k, v, a, hsave, dy, *, rev, name):
    G, T, N = q.shape
    Ht, _, P = v.shape
    Hg, Ha, NC = Ht // G, a.shape[2], T // CHUNK
    cm = (lambda c: c) if rev else (lambda c: NC - 1 - c)
    qs, vs, as_, hs = _scan_specs(G, T, N, Hg, P, Ha, cm)

    def body(q_ref, k_ref, v_ref, a_ref, hs_ref, dy_ref, dq_ref, dk_ref, dv_ref, da_ref, dh_scr):
        @pl.when(pl.program_id(1) == 0)
        def _():
            dh_scr[...] = jnp.zeros_like(dh_scr)

        _, vjp = jax.vjp(functools.partial(_scan_step, rev=rev), hs_ref[0, 0], q_ref[0], k_ref[0], v_ref[...], a_ref[0])
        dh, dq, dk, dv, da = vjp((dh_scr[...], dy_ref[...]))
        dq_ref[0] = dq
        dk_ref[0] = dk
        dv_ref[...] = dv
        da_ref[0] = da
        dh_scr[...] = dh

    return pl.pallas_call(
        body, name=name, grid=(G, NC), in_specs=[qs, qs, vs, as_, hs, vs], out_specs=[qs, qs, vs, as_],
        out_shape=[S((G, T, N), f32), S((G, T, N), f32), S((Ht, T, P), f32), S((G, T, Ha), f32)],
        scratch_shapes=[pltpu.VMEM((Hg, N, P), f32)],
        compiler_params=_params(("parallel", "arbitrary")))(q, k, v, a, hsave, dy)


def _na_block_case(rb, nrb):
    return jnp.where(rb == 0, 0, jnp.where(rb == nrb - 1, 2, 1))


def _na_key_start(rb, rows):
    return pl.multiple_of(jnp.clip(rb * NA_QROWS - NA_WR // 2, 0, rows - NA_KROWS) * GRID_W, 256)


def _na_fwd(q, k, v, bias, *, name):
    H, T, P = q.shape
    rows = T // GRID_W
    nq, nk = NA_QROWS * GRID_W, NA_KROWS * GRID_W
    nrb = T // nq
    scale = NADH ** -0.5

    def body(q_ref, k_ref, v_ref, b_ref, o_ref, l_ref):
        ks = _na_key_start(pl.program_id(1), rows)
        kw = k_ref[0, pl.ds(ks, nk), :]
        vw = v_ref[0, pl.ds(ks, nk), :]
        s = lax.dot_general(q_ref[0], kw, NT, preferred_element_type=f32) * scale + b_ref[0, 0]
        m = jnp.max(s, axis=1, keepdims=True)
        p = jnp.exp(s - m)
        l = jnp.sum(p, axis=1, keepdims=True)
        o_ref[0] = jnp.dot(p.astype(bf16), vw, preferred_element_type=f32) / l
        l_ref[0] = m + jnp.log(l)

    return pl.pallas_call(
        body, name=name, grid=(H, nrb),
        in_specs=[pl.BlockSpec((1, nq, P), lambda h, r: (h, r, 0)),
                  pl.BlockSpec((1, T, P), lambda h, r: (h, 0, 0)), pl.BlockSpec((1, T, P), lambda h, r: (h, 0, 0)),
                  pl.BlockSpec((1, 1, nq, nk), lambda h, r: (h, _na_block_case(r, nrb), 0, 0))],
        out_specs=[pl.BlockSpec((1, nq, P), lambda h, r: (h, r, 0)), pl.BlockSpec((1, nq, 1), lambda h, r: (h, r, 0))],
        out_shape=[S((H, T, P), f32), S((H, T, 1), f32)],
        compiler_params=_params(("parallel", "arbitrary")))(q, k, v, bias)


def _na_bwd(q, k, v, bias, o, lse, do, *, name):
    H, T, P = q.shape
    rows = T // GRID_W
    nq, nk = NA_QROWS * GRID_W, NA_KROWS * GRID_W
    nrb = T // nq
    scale = NADH ** -0.5

    def body(q_ref, k_ref, v_ref, b_ref, o_ref, l_ref, do_ref, dq_ref, dk_ref, dv_ref, db_ref):
        rb = pl.program_id(1)

        @pl.when(rb == 0)
        def _():
            dk_ref[...] = jnp.zeros_like(dk_ref)
            dv_ref[...] = jnp.zeros_like(dv_ref)

        ks = _na_key_start(rb, rows)
        qv = q_ref[0]
        kw = k_ref[0, pl.ds(ks, nk), :]
        vw = v_ref[0, pl.ds(ks, nk), :]
        s = lax.dot_general(qv, kw, NT, preferred_element_type=f32) * scale + b_ref[0, 0]
        p = jnp.exp(s - l_ref[0])
        do_ = do_ref[0]
        dob = do_.astype(bf16)
        dp = lax.dot_general(dob, vw, NT, preferred_element_type=f32)
        ds = p * (dp - jnp.sum(do_ * o_ref[0], axis=1, keepdims=True))
        dsb = ds.astype(bf16)
        dq_ref[0] = jnp.dot(dsb, kw, preferred_element_type=f32) * scale
        dk_ref[0, pl.ds(ks, nk), :] += lax.dot_general(dsb, qv, TN, preferred_element_type=f32) * scale
        dv_ref[0, pl.ds(ks, nk), :] += lax.dot_general(p.astype(bf16), dob, TN, preferred_element_type=f32)
        first = (rb == 0) | (rb == 1) | (rb == nrb - 1)

        @pl.when(first)
        def _():
            db_ref[0, 0] = ds

        @pl.when(jnp.logical_not(first))
        def _():
            db_ref[0, 0] += ds

    qspec = pl.BlockSpec((1, nq, P), lambda h, r: (h, r, 0))
    fspec = pl.BlockSpec((1, T, P), lambda h, r: (h, 0, 0))
    bspec = pl.BlockSpec((1, 1, nq, nk), lambda h, r: (h, _na_block_case(r, nrb), 0, 0))
    return pl.pallas_call(
        body, name=name, grid=(H, nrb),
        in_specs=[qspec, fspec, fspec, bspec, qspec, pl.BlockSpec((1, nq, 1), lambda h, r: (h, r, 0)), qspec],
        out_specs=[qspec, fspec, fspec, bspec],
        out_shape=[S((H, T, P), f32), S((H, T, P), f32), S((H, T, P), f32), S(bias.shape, f32)],
        compiler_params=_params(("parallel", "arbitrary")))(q, k, v, bias, o, lse, do)


def _na_bias_tables(rows):
    c = np.arange(GRID_W)[:, None]
    kc = np.arange(GRID_W)[None, :]
    cstart = np.clip(c - NA_WC // 2, 0, GRID_W - NA_WC)
    valid_c = (kc >= cstart) & (kc < cstart + NA_WC)
    dc = kc - c + NA_WC - 1
    E = (valid_c[:, :, None] & (dc[:, :, None] == np.arange(2 * NA_WC - 1)[None, None, :])).astype(np.float32)
    A = np.zeros((3, NA_QROWS, NA_KROWS, 2 * NA_WR - 1), np.float32)
    valid_r = np.zeros((3, NA_QROWS, NA_KROWS), bool)
    for z, r0 in enumerate((0, NA_QROWS, rows - NA_QROWS)):
        ks = int(np.clip(r0 - NA_WR // 2, 0, rows - NA_KROWS))
        for ri in range(NA_QROWS):
            r = r0 + ri
            rs = int(np.clip(r - NA_WR // 2, 0, rows - NA_WR))
            for kri in range(NA_KROWS):
                kr = ks + kri
                if rs <= kr < rs + NA_WR:
                    valid_r[z, ri, kri] = True
                    A[z, ri, kri, kr - r + NA_WR - 1] = 1.0
    mask = np.where(valid_r[:, :, None, :, None] & valid_c[None, None, :, None, :], 0.0, -1e30).astype(np.float32)
    return E, A, mask


def _na_bias(rpb, rows):
    E, A, mask = _na_bias_tables(rows)
    r1 = jnp.einsum("hde,cke->hdck", rpb, E, precision=HI)
    b = jnp.einsum("hdck,zabd->hzacbk", r1, A, precision=HI) + mask[None]
    return b.reshape(rpb.shape[0], 3, NA_QROWS * GRID_W, NA_KROWS * GRID_W)


def _conv_shifts(prev, cur, nxt, i, n_i, W):
    tm = cur.shape[0]
    prev = jnp.where(i > 0, prev, 0.0)
    nxt = jnp.where(i < n_i - 1, nxt, 0.0)
    ext = jnp.concatenate([prev, cur, nxt], axis=0)
    out = []
    for w in range(W):
        s = (W // 2 - w) % (tm + 16)
        out.append((ext if s == 0 else pltpu.roll(ext, s, axis=0))[8:8 + tm])
    return out


def _conv_act(u, mode):
    if mode == "silu":
        return jax.nn.silu(u)
    if mode == "geglu":
        half = u.shape[1] // 2
        return jax.nn.gelu(u[:, :half], approximate=True) * u[:, half:]
    return u


def _conv_specs(T, tm, tc, xbase):
    r8 = tm // 8
    last = T // 8 - 1
    cur = pl.BlockSpec((tm, tc), lambda j, i: (i, xbase + j))
    prev = pl.BlockSpec((8, tc), lambda j, i: (jnp.maximum(i * r8 - 1, 0), xbase + j))
    nxt = pl.BlockSpec((8, tc), lambda j, i: (jnp.minimum((i + 1) * r8, last), xbase + j))
    return cur, prev, nxt


def _conv(x, w8, b, *, mode, W, name, C, xbase=0, tm=512, tc=512, out_dtype=f32):
    T = x.shape[0]
    NI, J = T // tm, C // tc
    tco = tc // 2 if mode == "geglu" else tc
    cur, prev, nxt = _conv_specs(T, tm, tc, xbase)

    def body(xc, xp, xn, w_ref, b_ref, o_ref):
        sh = _conv_shifts(xp[...].astype(f32), xc[...].astype(f32), xn[...].astype(f32), pl.program_id(1), NI, W)
        wv = w_ref[...]
        u = sh[0] * wv[0:1, :]
        for w in range(1, W):
            u = u + sh[w] * wv[w:w + 1, :]
        if mode != "none":
            u = u + b_ref[...]
        o_ref[...] = _conv_act(u, mode).astype(o_ref.dtype)

    return pl.pallas_call(
        body, name=name, grid=(J, NI),
        in_specs=[cur, prev, nxt, pl.BlockSpec((8, tc), lambda j, i: (0, j)), pl.BlockSpec((1, tc), lambda j, i: (0, j))],
        out_specs=pl.BlockSpec((tm, tco), lambda j, i: (i, j)), out_shape=S((T, J * tco), out_dtype),
        compiler_params=_params(("parallel", "parallel")))(x, x, x, w8, b)


def _conv_bwd_a(x, w8, b, dact, *, mode, W, name, C, xbase=0, tm=512, tc=512):
    T = x.shape[0]
    NI, J = T // tm, C // tc
    tco = tc // 2 if mode == "geglu" else tc
    cur, prev, nxt = _conv_specs(T, tm, tc, xbase)

    def body(xc, xp, xn, w_ref, b_ref, d_ref, du_ref, dw_ref, db_ref):
        i = pl.program_id(1)
        sh = _conv_shifts(xp[...].astype(f32), xc[...].astype(f32), xn[...].astype(f32), i, NI, W)
        wv = w_ref[...]
        u = b_ref[...] + sh[0] * wv[0:1, :]
        for w in range(1, W):
            u = u + sh[w] * wv[w:w + 1, :]
        _, vjp = jax.vjp(functools.partial(_conv_act, mode=mode), u)
        du = vjp(d_ref[...].astype(f32))[0]
        du_ref[...] = du

        @pl.when(i == 0)
        def _():
            dw_ref[...] = jnp.zeros_like(dw_ref)
            db_ref[...] = jnp.zeros_like(db_ref)

        db_ref[...] += jnp.sum(du, axis=0, keepdims=True)
        for w in range(W):
            dw_ref[w:w + 1, :] += jnp.sum(du * sh[w], axis=0, keepdims=True)

    return pl.pallas_call(
        body, name=name, grid=(J, NI),
        in_specs=[cur, prev, nxt, pl.BlockSpec((8, tc), lambda j, i: (0, j)), pl.BlockSpec((1, tc), lambda j, i: (0, j)),
                  pl.BlockSpec((tm, tco), lambda j, i: (i, j))],
        out_specs=[pl.BlockSpec((tm, tc), lambda j, i: (i, j)), pl.BlockSpec((8, tc), lambda j, i: (0, j)),
                   pl.BlockSpec((1, tc), lambda j, i: (0, j))],
        out_shape=[S((T, C), f32), S((8, C), f32), S((1, C), f32)],
        compiler_params=_params(("parallel", "arbitrary")))(x, x, x, w8, b, dact)


def _pad8(w):
    return jnp.concatenate([w, jnp.zeros((8 - w.shape[0], w.shape[1]), w.dtype)], axis=0)


def _exchange(arrs, *, name):
    n = len(arrs)
    ncopy = (NDEV - 1) * n

    def body(*refs):
        ins, outs = refs[:n], refs[n:2 * n]
        send_sems, recv_sems, loc_sems = refs[2 * n:]
        x, y, c = lax.axis_index("x"), lax.axis_index("y"), lax.axis_index("c")
        me = 4 * x + 2 * y + c

        def src(a, p):
            return ins[a].at[p] if arrs[a][1] else ins[a]

        local = [pltpu.make_async_copy(src(a, me), outs[a].at[me], loc_sems.at[a]) for a in range(n)]
        for cp in local:
            cp.start()
        sent = []
        for kk in range(1, NDEV):
            px = 1 - x if kk & 4 else x
            py = 1 - y if kk & 2 else y
            pc = 1 - c if kk & 1 else c
            peer = 4 * px + 2 * py + pc
            for a in range(n):
                idx = (kk - 1) * n + a
                mk = lambda dst_slot, a=a, idx=idx, peer=peer, dev=(px, py, pc): pltpu.make_async_remote_copy(
                    src_ref=src(a, peer), dst_ref=outs[a].at[dst_slot], send_sem=send_sems.at[idx], recv_sem=recv_sems.at[idx],
                    device_id=dev, device_id_type=pl.DeviceIdType.MESH)
                mk(me).start()
                sent.append((mk, peer))
        for mk, peer in sent:
            mk(peer).wait_recv()
        for mk, peer in sent:
            mk(peer).wait_send()
        for cp in local:
            cp.wait()

    any_spec = pl.BlockSpec(memory_space=pl.ANY)
    return pl.pallas_call(
        body, name=name, in_specs=[any_spec] * n, out_specs=[any_spec] * n,
        out_shape=[S(a.shape if pp else (NDEV,) + a.shape, a.dtype) for a, pp in arrs],
        scratch_shapes=[pltpu.SemaphoreType.DMA((ncopy,)), pltpu.SemaphoreType.DMA((ncopy,)), pltpu.SemaphoreType.DMA((n,))],
        )(*[a for a, _ in arrs])


def _adamw(r, w, m, v, *, name, tr=512):
    M = w.shape[0]
    tr = min(tr, M)

    def body(r_ref, w_ref, m_ref, v_ref, g_ref, d_ref, nm_ref, nv_ref):
        g = r_ref[0]
        for s in range(1, NDEV):
            g = g + r_ref[s]
        m_ = B1 * m_ref[...] + (1.0 - B1) * g
        v_ = B2 * v_ref[...] + (1.0 - B2) * jnp.square(g)
        m_hat = m_ / (1.0 - B1 ** STEP)
        v_hat = v_ / (1.0 - B2 ** STEP)
        g_ref[...] = g
        d_ref[...] = -LR * (m_hat / (jnp.sqrt(v_hat) + AEPS) + WD * w_ref[...])
        nm_ref[...] = m_
        nv_ref[...] = v_

    row = pl.BlockSpec((tr, LANES), lambda i: (i, 0))
    return pl.pallas_call(
        body, name=name, grid=(M // tr,),
        in_specs=[pl.BlockSpec((NDEV, tr, LANES), lambda i: (0, i, 0)), row, row, row],
        out_specs=[row] * 4, out_shape=[S((M, LANES), f32)] * 4, compiler_params=_params(("parallel",)))(r, w, m, v)


def _tm2hm(a, H):
    T = a.shape[0]
    return a.reshape(T, H, -1).transpose(1, 0, 2)


def _hm2tm(a):
    H, T, P = a.shape
    return a.transpose(1, 0, 2).reshape(T, H * P)


def _pack(parts, dtype, row_mult):
    flat = jnp.concatenate([p.reshape(-1).astype(dtype) for p in parts])
    rows = -(-flat.shape[0] // LANES)
    rows = -(-rows // row_mult) * row_mult
    return jnp.pad(flat, (0, rows * LANES - flat.shape[0])).reshape(rows, LANES)


def _unpack(buf, shapes, lead=()):
    flat = buf.reshape(lead + (-1,))
    out, off = [], 0
    for shp in shapes:
        n = int(np.prod(shp))
        out.append(flat[..., off:off + n].reshape(lead + tuple(shp)))
        off += n
    return out


def _to_slots(full, ax):
    shp = full.shape
    return jnp.moveaxis(full.reshape(shp[:ax] + (NDEV, shp[ax] // NDEV) + shp[ax + 1:]), ax, 0)


def _from_slots(g, ax):
    t = jnp.moveaxis(g, 0, ax)
    shp = t.shape
    return t.reshape(shp[:ax] + (shp[ax] * shp[ax + 1],) + shp[ax + 2:])


def _ffn_perm(a):
    lead = a.shape[:-1]
    h = FFN_TC // 2
    return jnp.swapaxes(a.reshape(lead + (2, FFN // h, h)), -3, -2).reshape(lead + (2 * FFN,))


def _ffn_unperm(a):
    lead = a.shape[:-1]
    h = FFN_TC // 2
    return jnp.swapaxes(a.reshape(lead + (FFN // h, 2, h)), -3, -2).reshape(lead + (2 * FFN,))


def _rope_tables(T):
    half = RDH // 2
    inv = 1.0 / (ROPE_BASE ** (jnp.arange(half, dtype=f32) / half))
    ang = jnp.arange(T, dtype=f32)[:, None] * inv[None, :]
    cos, sin = jnp.cos(ang), jnp.sin(ang)
    cos_t = jnp.tile(jnp.concatenate([cos, cos], axis=1), (1, RH))
    sin_t = jnp.tile(jnp.concatenate([-sin, sin], axis=1), (1, RH))
    return cos_t, sin_t


def _group_avg():
    g = np.arange(RW) // RDH
    return jnp.asarray((g[:, None] == g[None, :]).astype(np.float32) / RDH)


def _head_expand():
    hd = np.arange(SSD_INNER) // SSD_HD
    rows = np.arange(2 * SSD_H)
    ex0 = (rows[:, None] == hd[None, :]).astype(np.float32)
    ex1 = (rows[:, None] == SSD_H + hd[None, :]).astype(np.float32)
    return jnp.asarray(ex0), jnp.asarray(ex1)


def kernel(x, norm_mix_pre, norm_mix_post, norm_ffn_pre, norm_ffn_post, ab_w_in, ab_ret_decay_logit, ab_ret_gn_g, ab_na_rpb, ab_w_out, c_w_in, c_conv_w, c_conv_b, c_dt_bias, c_a_log, c_d_skip, c_norm_g, c_w_out, ffn_w_up, ffn_conv_w, ffn_conv_b, ffn_w_down, loss_target, m_norm_mix_pre, m_norm_mix_post, m_norm_ffn_pre, m_norm_ffn_post, m_ab_w_in, m_ab_ret_decay_logit, m_ab_ret_gn_g, m_ab_na_rpb, m_ab_w_out, m_c_w_in, m_c_conv_w, m_c_conv_b, m_c_dt_bias, m_c_a_log, m_c_d_skip, m_c_norm_g, m_c_w_out, m_ffn_w_up, m_ffn_conv_w, m_ffn_conv_b, m_ffn_w_down, v_norm_mix_pre, v_norm_mix_post, v_norm_ffn_pre, v_norm_ffn_post, v_ab_w_in, v_ab_ret_decay_logit, v_ab_ret_gn_g, v_ab_na_rpb, v_ab_w_out, v_c_w_in, v_c_conv_w, v_c_conv_b, v_c_dt_bias, v_c_a_log, v_c_d_skip, v_c_norm_g, v_c_w_out, v_ffn_w_up, v_ffn_conv_w, v_ffn_conv_b, v_ffn_w_down):
    W = dict(norm_mix_pre=norm_mix_pre, norm_mix_post=norm_mix_post, norm_ffn_pre=norm_ffn_pre, norm_ffn_post=norm_ffn_post, ab_w_in=ab_w_in, ab_ret_decay_logit=ab_ret_decay_logit, ab_ret_gn_g=ab_ret_gn_g, ab_na_rpb=ab_na_rpb, ab_w_out=ab_w_out, c_w_in=c_w_in, c_conv_w=c_conv_w, c_conv_b=c_conv_b, c_dt_bias=c_dt_bias, c_a_log=c_a_log, c_d_skip=c_d_skip, c_norm_g=c_norm_g, c_w_out=c_w_out, ffn_w_up=ffn_w_up, ffn_conv_w=ffn_conv_w, ffn_conv_b=ffn_conv_b, ffn_w_down=ffn_w_down)
    Mo = dict(norm_mix_pre=m_norm_mix_pre, norm_mix_post=m_norm_mix_post, norm_ffn_pre=m_norm_ffn_pre, norm_ffn_post=m_norm_ffn_post, ab_w_in=m_ab_w_in, ab_ret_decay_logit=m_ab_ret_decay_logit, ab_ret_gn_g=m_ab_ret_gn_g, ab_na_rpb=m_ab_na_rpb, ab_w_out=m_ab_w_out, c_w_in=m_c_w_in, c_conv_w=m_c_conv_w, c_conv_b=m_c_conv_b, c_dt_bias=m_c_dt_bias, c_a_log=m_c_a_log, c_d_skip=m_c_d_skip, c_norm_g=m_c_norm_g, c_w_out=m_c_w_out, ffn_w_up=m_ffn_w_up, ffn_conv_w=m_ffn_conv_w, ffn_conv_b=m_ffn_conv_b, ffn_w_down=m_ffn_w_down)
    Vo = dict(norm_mix_pre=v_norm_mix_pre, norm_mix_post=v_norm_mix_post, norm_ffn_pre=v_norm_ffn_pre, norm_ffn_post=v_norm_ffn_post, ab_w_in=v_ab_w_in, ab_ret_decay_logit=v_ab_ret_decay_logit, ab_ret_gn_g=v_ab_ret_gn_g, ab_na_rpb=v_ab_na_rpb, ab_w_out=v_ab_w_out, c_w_in=v_c_w_in, c_conv_w=v_c_conv_w, c_conv_b=v_c_conv_b, c_dt_bias=v_c_dt_bias, c_a_log=v_c_a_log, c_d_skip=v_c_d_skip, c_norm_g=v_c_norm_g, c_w_out=v_c_w_out, ffn_w_up=v_ffn_w_up, ffn_conv_w=v_ffn_conv_w, ffn_conv_b=v_ffn_conv_b, ffn_w_down=v_ffn_w_down)
    return _train_step(x[0], loss_target[0], W, Mo, Vo)


def _train_step(x, tgt, W, Mo, Vo):
    T = x.shape[0]
    rows = T // GRID_W

    big = _pack([W[n] for n, _ in SHARDED[:N_BIG]], bf16, 16)
    small = _pack([W[n] for n, _ in SHARDED[N_BIG:]], f32, 8)
    big_g, small_g = _exchange([(big, False), (small, False)], name="gather_weights")
    gb = _unpack(big_g, [W[n].shape for n, _ in SHARDED[:N_BIG]], (NDEV,))
    gs = _unpack(small_g, [W[n].shape for n, _ in SHARDED[N_BIG:]], (NDEV,))
    full = {n: _from_slots(g, ax) for (n, ax), g in zip(SHARDED, gb + gs)}
    w_ab_in, w_ab_out = full["ab_w_in"], full["ab_w_out"]
    w_zx, w_dt = full["c_w_in"][:, :, :SSD_INNER + SSD_XBC], full["c_w_in"][:, :, SSD_INNER + SSD_XBC:]
    w_c_out = full["c_w_out"]
    w_up, w_down = _ffn_perm(full["ffn_w_up"]), full["ffn_w_down"]
    c_cw8 = [_pad8(full["c_conv_w"][i]) for i in range(2)]
    c_cb = [full["c_conv_b"][i][None] for i in range(2)]
    c_ng = [full["c_norm_g"][i][None] for i in range(2)]
    f_cw8 = [_pad8(_ffn_perm(full["ffn_conv_w"][l])) for l in range(DEPTH)]
    f_cb = [_ffn_perm(W["ffn_conv_b"][l])[None] for l in range(DEPTH)]

    g1 = [W["norm_mix_pre"][l][None] for l in range(DEPTH)]
    g2 = [W["norm_mix_post"][l][None] for l in range(DEPTH)]
    g3 = [W["norm_ffn_pre"][l][None] for l in range(DEPTH)]
    g4 = [W["norm_ffn_post"][l][None] for l in range(DEPTH)]
    cos_t, sin_t = _rope_tables(T)
    gavg = _group_avg()
    ex0, ex1 = _head_expand()

    def log_gamma(logit):
        return -jax.nn.softplus(-logit)

    def ret_decays(lg):
        return [jnp.broadcast_to(lg[d][:, None, None], (RH, T, 8)) for d in range(2)]

    saved = []
    xs_ = x
    hn = _rowwise("norm_first", _f_first, [(x, D, 0)], [], [(g1[0], D, 0)], [], [(D, bf16)], tm=256)[0]
    for l in range(DEPTH):
        i = l // 2
        sv = dict(x=xs_, hn=hn)
        if l % 2 == 0:
            proj = _mm_nn(hn, w_ab_in[i], name=f"ab_in_{l}")
            qr, kr = _rowwise(f"ret_prep_{l}", _f_rprep, [(proj, RW, 0), (proj, RW, 1)], [(cos_t, RW, 0), (sin_t, RW, 0)], [], [],
                              [(RW, f32), (RW, f32)], tm=256)
            q_h, k_h, v_h = _tm2hm(qr, RH), _tm2hm(kr, RH), _tm2hm(proj[:, 2 * RW:3 * RW], RH)
            lg, lg_vjp = jax.vjp(log_gamma, W["ab_ret_decay_logit"][i])
            a_f, a_b = ret_decays(lg)
            yf, hsf = _scan_fwd(q_h, k_h, v_h, a_f, rev=False, name=f"ret_scan_f_{l}")
            yb, hsb = _scan_fwd(q_h, k_h, v_h, a_b, rev=True, name=f"ret_scan_b_{l}")
            yf_t, yb_t = _hm2tm(yf), _hm2tm(yb)
            gn = W["ab_ret_gn_g"][i][None]
            ret = _rowwise(f"ret_post_{l}", _f_rpost, [(yf_t, RW, 0), (yb_t, RW, 0), (proj, RW, 3)], [], [(gn, RW, 0)], [gavg],
                           [(RW, bf16)], tm=256)[0]
            nq, nk, nv = [_tm2hm(proj[:, (4 + j) * RW:(5 + j) * RW], NAH).astype(bf16) for j in range(3)]
            bias, bias_vjp = jax.vjp(functools.partial(_na_bias, rows=rows), W["ab_na_rpb"][i])
            na_o, na_l = _na_fwd(nq, nk, nv, bias, name=f"na_fwd_{l}")
            cat = jnp.concatenate([ret, _hm2tm(na_o).astype(bf16)], axis=1)
            mo = _mm_nn(cat, w_ab_out[i], name=f"ab_out_{l}")
            sv.update(proj=proj, q_h=q_h, k_h=k_h, v_h=v_h, a_f=a_f, a_b=a_b, hsf=hsf, hsb=hsb, yf_t=yf_t, yb_t=yb_t, gn=gn,
                      nq=nq, nk=nk, nv=nv, bias=bias, bias_vjp=bias_vjp, lg_vjp=lg_vjp, na_o=na_o, na_l=na_l, cat=cat)
        else:
            zx = _mm_nn(hn, w_zx[i], name=f"c_in_{l}")
            dtr = _mm_nn(hn, w_dt[i], name=f"c_in_dt_{l}")
            xa = _conv(zx, c_cw8[i], c_cb[i], mode="silu", W=SSD_CONV, name=f"c_conv_{l}", C=SSD_XBC, xbase=SSD_INNER // 512)
            dtb, alog = W["c_dt_bias"][i].reshape(1, 2 * SSD_H), W["c_a_log"][i].reshape(1, 2 * SSD_H)
            vf, vb, la = _rowwise(f"ssd_prep_{l}", _f_sprep, [(xa, SSD_INNER, 0), (dtr, 2 * SSD_H, 0)], [],
                                  [(dtb, 2 * SSD_H, 0), (alog, 2 * SSD_H, 0)], [ex0, ex1],
                                  [(SSD_INNER, f32), (SSD_INNER, f32), (2 * SSD_H, f32)], tm=128)
            k_h = _tm2hm(xa[:, SSD_INNER:SSD_INNER + SSD_G * SSD_N], SSD_G)
            q_h = _tm2hm(xa[:, SSD_INNER + SSD_G * SSD_N:], SSD_G)
            a_f = la[:, :SSD_H].reshape(T, SSD_G, SSD_HPG).transpose(1, 0, 2)
            a_b = la[:, SSD_H:].reshape(T, SSD_G, SSD_HPG).transpose(1, 0, 2)
            vf_h, vb_h = _tm2hm(vf, SSD_H), _tm2hm(vb, SSD_H)
            yf, hsf = _scan_fwd(q_h, k_h, vf_h, a_f, rev=False, name=f"ssd_scan_f_{l}")
            yb, hsb = _scan_fwd(q_h, k_h, vb_h, a_b, rev=True, name=f"ssd_scan_b_{l}")
            yf_t, yb_t = _hm2tm(yf), _hm2tm(yb)
            dsk = jnp.repeat(W["c_d_skip"][i], SSD_HD)[None]
            yo = _rowwise(f"ssd_post_{l}", _f_spost, [(yf_t, 512, 0), (yb_t, 512, 0), (xa, 512, 0), (zx, 512, 0)], [],
                          [(dsk, 512, 0), (c_ng[i], 512, 0)], [], [(512, bf16)], tm=256, J=SSD_G)[0]
            mo = _mm_nn(yo, w_c_out[i], name=f"c_out_{l}")
            sv.update(zx=zx, dtr=dtr, xa=xa, dtb=dtb, alog=alog, q_h=q_h, k_h=k_h, a_f=a_f, a_b=a_b, vf_h=vf_h, vb_h=vb_h,
                      hsf=hsf, hsb=hsb, yf_t=yf_t, yb_t=yb_t, dsk=dsk, yo=yo)
        x1, hf = _rowwise(f"norm_mid_{l}", _f_mid, [(xs_, D, 0), (mo, D, 0)], [], [(g2[l], D, 0), (g3[l], D, 0)], [],
                          [(D, f32), (D, bf16)], tm=256)
        pre = _mm_nn(hf, w_up[l], name=f"ffn_up_{l}")
        act = _conv(pre, f_cw8[l], f_cb[l], mode="geglu", W=FFN_CONV, name=f"ffn_conv_{l}", C=2 * FFN, tc=FFN_TC, out_dtype=bf16)
        fo = _mm_nn(act, w_down[l], name=f"ffn_down_{l}")
        sv.update(mo=mo, x1=x1, hf=hf, pre=pre, act=act, fo=fo)
        if l < DEPTH - 1:
            xs_, hn = _rowwise(f"norm_end_{l}", _f_end, [(x1, D, 0), (fo, D, 0)], [], [(g4[l], D, 0), (g1[l + 1], D, 0)], [],
                               [(D, f32), (D, bf16)], tm=256)
        else:
            xs_ = _rowwise(f"norm_end_{l}", _f_last, [(x1, D, 0), (fo, D, 0)], [], [(g4[l], D, 0)], [], [(D, f32)], tm=256)[0]
        saved.append(sv)

    dx, lpart = _loss_call(xs_, tgt)
    loss = lax.psum(lpart[0, 0], ("x", "y", "c"))

    G = {n: [None] * W[n].shape[0] for n in WEIGHTS}
    dhn = None
    for l in reversed(range(DEPTH)):
        i = l // 2
        sv = saved[l]
        if l == DEPTH - 1:
            (dx1, dfo), (dg4,) = _rowwise_bwd(f"norm_end_bwd_{l}", _f_last, [(sv["x1"], D, 0), (sv["fo"], D, 0)], [],
                                              [(g4[l], D, 0)], [], [(dx, D, 0)], [f32, bf16], tm=256)
        else:
            (dx1, dfo), (dg4, dg1n) = _rowwise_bwd(f"norm_end_bwd_{l}", _f_end, [(sv["x1"], D, 0), (sv["fo"], D, 0)], [],
                                                   [(g4[l], D, 0), (g1[l + 1], D, 0)], [], [(dx, D, 0), (dhn, D, 0)],
                                                   [f32, bf16], tm=256)
            G["norm_mix_pre"][l + 1] = dg1n[0]
        G["norm_ffn_post"][l] = dg4[0]
        dact = _mm_nt(dfo, w_down[l], name=f"ffn_down_dx_{l}")
        G["ffn_w_down"][l] = _mm_tn(sv["act"], dfo, name=f"ffn_down_dw_{l}")
        du, dfw, dfb = _conv_bwd_a(sv["pre"], f_cw8[l], f_cb[l], dact, mode="geglu", W=FFN_CONV, name=f"ffn_conv_bwd_{l}",
                                   C=2 * FFN, tc=FFN_TC)
        flip = lambda w8, width: _pad8(w8[:width][::-1])
        dpre = _conv(du, flip(f_cw8[l], FFN_CONV), f_cb[l], mode="none", W=FFN_CONV, name=f"ffn_conv_dx_{l}", C=2 * FFN,
                     tc=FFN_TC, out_dtype=bf16)
        dhf = _mm_nt(dpre, w_up[l], name=f"ffn_up_dx_{l}")
        G["ffn_w_up"][l] = _ffn_unperm(_mm_tn(sv["hf"], dpre, name=f"ffn_up_dw_{l}"))
        G["ffn_conv_w"][l] = _ffn_unperm(dfw[:FFN_CONV])
        G["ffn_conv_b"][l] = _ffn_unperm(dfb[0])
        (dxl, dmo), (dg2, dg3) = _rowwise_bwd(f"norm_mid_bwd_{l}", _f_mid, [(sv["x"], D, 0), (sv["mo"], D, 0)], [],
                                              [(g2[l], D, 0), (g3[l], D, 0)], [], [(dx1, D, 0), (dhf, D, 0)], [f32, bf16], tm=256)
        G["norm_mix_post"][l] = dg2[0]
        G["norm_ffn_pre"][l] = dg3[0]
        if l % 2 == 0:
            dcat = _mm_nt(dmo, w_ab_out[i], name=f"ab_out_dx_{l}")
            G["ab_w_out"][i] = _mm_tn(sv["cat"], dmo, name=f"ab_out_dw_{l}")
            (dyf, _, drg), (dgn,) = _rowwise_bwd(
                f"ret_post_bwd_{l}", _f_rpost, [(sv["yf_t"], RW, 0), (sv["yb_t"], RW, 0), (sv["proj"], RW, 3)], [],
                [(sv["gn"], RW, 0)], [gavg], [(dcat, RW, 0)], [f32, f32, bf16], tm=256)
            G["ab_ret_gn_g"][i] = dgn[0]
            dy_h = _tm2hm(dyf, RH)
            dqf, dkf, dvf, daf = _scan_bwd(sv["q_h"], sv["k_h"], sv["v_h"], sv["a_f"], sv["hsf"], dy_h, rev=False,
                                           name=f"ret_scan_f_bwd_{l}")
            dqb, dkb, dvb, dab = _scan_bwd(sv["q_h"], sv["k_h"], sv["v_h"], sv["a_b"], sv["hsb"], dy_h, rev=True,
                                           name=f"ret_scan_b_bwd_{l}")
            dq_t, dk_t, drv = _hm2tm(dqf + dqb), _hm2tm(dkf + dkb), _hm2tm(dvf + dvb).astype(bf16)
            (drq, drk), _ = _rowwise_bwd(f"ret_prep_bwd_{l}", _f_rprep, [(sv["proj"], RW, 0), (sv["proj"], RW, 1)],
                                         [(cos_t, RW, 0), (sin_t, RW, 0)], [], [], [(dq_t, RW, 0), (dk_t, RW, 0)], [bf16, bf16], tm=256)
            da_cols = jnp.concatenate([_hm2tm(daf), _hm2tm(dab)], axis=1)
            dlg = _colsum(da_cols, name=f"ret_decay_sum_{l}").reshape(2, RH, 8)[:, :, 0]
            G["ab_ret_decay_logit"][i] = sv["lg_vjp"](dlg)[0]
            dna = _tm2hm(dcat[:, RW:], NAH)
            dnq, dnk, dnv, dbias = _na_bwd(sv["nq"], sv["nk"], sv["nv"], sv["bias"], sv["na_o"], sv["na_l"], dna, name=f"na_bwd_{l}")
            G["ab_na_rpb"][i] = sv["bias_vjp"](dbias)[0]
            dproj = jnp.concatenate([drq, drk, drv, drg] + [_hm2tm(t).astype(bf16) for t in (dnq, dnk, dnv)], axis=1)
            dhn = _mm_nt(dproj, w_ab_in[i], name=f"ab_in_dx_{l}")
            G["ab_w_in"][i] = _mm_tn(sv["hn"], dproj, name=f"ab_in_dw_{l}")
        else:
            dyo = _mm_nt(dmo, w_c_out[i], name=f"c_out_dx_{l}")
            G["c_w_out"][i] = _mm_tn(sv["yo"], dmo, name=f"c_out_dw_{l}")
            (dyf, _, dxs1, dz), (ddsk, dng) = _rowwise_bwd(
                f"ssd_post_bwd_{l}", _f_spost, [(sv["yf_t"], 512, 0), (sv["yb_t"], 512, 0), (sv["xa"], 512, 0), (sv["zx"], 512, 0)],
                [], [(sv["dsk"], 512, 0), (c_ng[i], 512, 0)], [], [(dyo, 512, 0)], [f32, f32, f32, bf16], tm=256, J=SSD_G)
            G["c_d_skip"][i] = ddsk.reshape(SSD_H, SSD_HD).sum(axis=1)
            G["c_norm_g"][i] = dng[0]
            dy_h = _tm2hm(dyf, SSD_H)
            dqf, dkf, dvf, daf = _scan_bwd(sv["q_h"], sv["k_h"], sv["vf_h"], sv["a_f"], sv["hsf"], dy_h, rev=False,
                                           name=f"ssd_scan_f_bwd_{l}")
            dqb, dkb, dvb, dab = _scan_bwd(sv["q_h"], sv["k_h"], sv["vb_h"], sv["a_b"], sv["hsb"], dy_h, rev=True,
                                           name=f"ssd_scan_b_bwd_{l}")
            dla = jnp.concatenate([daf.transpose(1, 0, 2).reshape(T, SSD_H), dab.transpose(1, 0, 2).reshape(T, SSD_H)], axis=1)
            (dxs2, ddtr), (ddtb, dalog) = _rowwise_bwd(
                f"ssd_prep_bwd_{l}", _f_sprep, [(sv["xa"], SSD_INNER, 0), (sv["dtr"], 2 * SSD_H, 0)], [],
                [(sv["dtb"], 2 * SSD_H, 0), (sv["alog"], 2 * SSD_H, 0)], [ex0, ex1],
                [(_hm2tm(dvf), SSD_INNER, 0), (_hm2tm(dvb), SSD_INNER, 0), (dla, 2 * SSD_H, 0)], [f32, bf16], tm=128)
            G["c_dt_bias"][i] = ddtb.reshape(2, SSD_H)
            G["c_a_log"][i] = dalog.reshape(2, SSD_H)
            dxa = jnp.concatenate([dxs1 + dxs2, _hm2tm(dkf + dkb), _hm2tm(dqf + dqb)], axis=1)
            du, dcw, dcb = _conv_bwd_a(sv["zx"], c_cw8[i], c_cb[i], dxa, mode="silu", W=SSD_CONV, name=f"c_conv_bwd_{l}",
                                       C=SSD_XBC, xbase=SSD_INNER // 512)
            G["c_conv_w"][i] = dcw[:SSD_CONV]
            G["c_conv_b"][i] = dcb[0]
            dxbc = _conv(du, _pad8(c_cw8[i][:SSD_CONV][::-1]), c_cb[i], mode="none", W=SSD_CONV, name=f"c_conv_dx_{l}",
                         C=SSD_XBC, out_dtype=bf16)
            dzx = jnp.concatenate([dz, dxbc], axis=1)
            t1 = _mm_nt(ddtr, w_dt[i], name=f"c_in_dt_dx_{l}")
            dhn = _mm_nt(dzx, w_zx[i], add=t1, name=f"c_in_dx_{l}")
            G["c_w_in"][i] = jnp.concatenate([_mm_tn(sv["hn"], dzx, name=f"c_in_dw_{l}"),
                                              _mm_tn(sv["hn"], ddtr, name=f"c_in_dt_dw_{l}")], axis=1)
        dx = dxl
    (grad_x,), (dg1,) = _rowwise_bwd("norm_first_bwd", _f_first_bwd, [(x, D, 0)], [], [(g1[0], D, 0)], [], [(dx, D, 0), (dhn, D, 0)],
                                     [f32], tm=256)
    G["norm_mix_pre"][0] = dg1[0]
    grads = {n: jnp.stack(G[n]).reshape(W[n].shape[:1] + tuple(G[n][0].shape)) for n in WEIGHTS}

    rs = _pack_slots([_to_slots(grads[n].reshape(_full_shape(W[n].shape, ax)), ax) for n, ax in SHARDED], 512)
    ar = _pack([grads[n].reshape(W[n].shape) for n in REPLICATED], f32, 8)
    rs_g, ar_g = _exchange([(rs, True), (ar, False)], name="exchange_grads")
    pk = lambda d, names, mult: _pack([d[n] for n in names], f32, mult)
    sh_names = [n for n, _ in SHARDED]
    out_s = _adamw(rs_g, pk(W, sh_names, 512), pk(Mo, sh_names, 512), pk(Vo, sh_names, 512), name="adamw_sharded")
    out_r = _adamw(ar_g, pk(W, REPLICATED, 8), pk(Mo, REPLICATED, 8), pk(Vo, REPLICATED, 8), name="adamw_replicated", tr=ar.shape[0])
    res = []
    for k in range(4):
        d = dict(zip(sh_names, _unpack(out_s[k], [W[n].shape for n in sh_names])))
        d.update(zip(REPLICATED, _unpack(out_r[k], [W[n].shape for n in REPLICATED])))
        res.append(d)
    outs = [loss, grad_x[None]]
    for k in range(4):
        outs += [res[k][n] for n in WEIGHTS]
    return tuple(outs)


def _full_shape(shard_shape, ax):
    return tuple(s * NDEV if a == ax else s for a, s in enumerate(shard_shape))


def _pack_slots(slot_arrays, row_mult):
    flat = jnp.concatenate([a.reshape(NDEV, -1) for a in slot_arrays], axis=1)
    rows = -(-flat.shape[1] // LANES)
    rows = -(-rows // row_mult) * row_mult
    return jnp.pad(flat, ((0, 0), (0, rows * LANES - flat.shape[1]))).reshape(NDEV, rows, LANES)
```

```python
import functools
import numpy as np
import jax
import jax.numpy as jnp
from jax import lax
from jax.experimental import pallas as pl
from jax.experimental.pallas import tpu as pltpu

f32, bf16 = jnp.float32, jnp.bfloat16
S = jax.ShapeDtypeStruct
HI = lax.Precision.HIGHEST

D = 1024
DEPTH = 4
GRID_W = 64
CHUNK = 128
EPS = 1e-6
RH, RDH, RW = 8, 64, 512
NAH, NADH, NAW = 8, 64, 512
NA_WR, NA_WC = 8, 16
NA_QROWS = 8
NA_KROWS = 16
SSD_INNER, SSD_HD, SSD_H, SSD_G, SSD_HPG, SSD_N, SSD_CONV = 2048, 64, 32, 4, 8, 128, 5
SSD_XBC = SSD_INNER + 2 * SSD_G * SSD_N
FFN, FFN_CONV = 2816, 3
FFN_TC = 512
SCAN_HEADS_PER_STEP = 8
ROPE_BASE = 10000.0
LR, B1, B2, AEPS, WD, STEP = 0.001, 0.9, 0.999, 1e-08, 0.01, 10
NDEV = 8
LANES = 128
VMEM_LIMIT = 56 * 1024 * 1024

NT = (((1,), (1,)), ((), ()))
TN = (((0,), (0,)), ((), ()))

SHARDED = [("ab_w_in", 2), ("ab_w_out", 1), ("c_w_in", 2), ("c_w_out", 1), ("ffn_w_up", 2), ("ffn_w_down", 1),
           ("c_conv_w", 2), ("c_conv_b", 1), ("c_norm_g", 1), ("ffn_conv_w", 2)]
N_BIG = 6
COL_SHARDED = ["ab_w_in", "c_w_in", "ffn_w_up"]
ROW_SHARDED = ["ab_w_out", "c_w_out", "ffn_w_down"]
REPLICATED = ["norm_mix_pre", "norm_mix_post", "norm_ffn_pre", "norm_ffn_post", "ab_ret_decay_logit", "ab_ret_gn_g",
              "ab_na_rpb", "c_dt_bias", "c_a_log", "c_d_skip", "ffn_conv_b"]
WEIGHTS = ["norm_mix_pre", "norm_mix_post", "norm_ffn_pre", "norm_ffn_post", "ab_w_in", "ab_ret_decay_logit",
           "ab_ret_gn_g", "ab_na_rpb", "ab_w_out", "c_w_in", "c_conv_w", "c_conv_b", "c_dt_bias", "c_a_log", "c_d_skip",
           "c_norm_g", "c_w_out", "ffn_w_up", "ffn_conv_w", "ffn_conv_b", "ffn_w_down"]


def _params(sem=None):
    return pltpu.CompilerParams(dimension_semantics=sem, vmem_limit_bytes=VMEM_LIMIT)


def _mm_nn(a, w, *, name, tm=512, tn=512, out_dtype=f32):
    M, K = a.shape
    N = w.shape[1]
    tn = min(tn, N)

    def body(a_ref, w_ref, o_ref):
        o_ref[...] = jnp.dot(a_ref[...], w_ref[...], preferred_element_type=f32).astype(o_ref.dtype)

    return pl.pallas_call(
        body, name=name, grid=(M // tm, N // tn),
        in_specs=[pl.BlockSpec((tm, K), lambda i, j: (i, 0)), pl.BlockSpec((K, tn), lambda i, j: (0, j))],
        out_specs=pl.BlockSpec((tm, tn), lambda i, j: (i, j)),
        out_shape=S((M, N), out_dtype), compiler_params=_params(("parallel", "parallel")))(a, w)


def _mm_nt(dy, w, *, name, add=None, tm=512, tk=512):
    M, N = dy.shape
    K = w.shape[0]
    tk = next(t for t in (tk, 256, 128, K) if K % t == 0)

    def body(*refs):
        if add is None:
            d_ref, w_ref, o_ref = refs
            o_ref[...] = lax.dot_general(d_ref[...], w_ref[...], NT, preferred_element_type=f32)
        else:
            d_ref, w_ref, a_ref, o_ref = refs
            o_ref[...] = lax.dot_general(d_ref[...], w_ref[...], NT, preferred_element_type=f32) + a_ref[...]

    in_specs = [pl.BlockSpec((tm, N), lambda i, j: (i, 0)), pl.BlockSpec((tk, N), lambda i, j: (j, 0))]
    args = [dy, w]
    if add is not None:
        in_specs.append(pl.BlockSpec((tm, tk), lambda i, j: (i, j)))
        args.append(add)
    return pl.pallas_call(
        body, name=name, grid=(M // tm, K // tk), in_specs=in_specs,
        out_specs=pl.BlockSpec((tm, tk), lambda i, j: (i, j)),
        out_shape=S((M, K), f32), compiler_params=_params(("parallel", "parallel")))(*args)


def _mm_tn(a, dy, *, name, tt=1024):
    M, K = a.shape
    N = dy.shape[1]
    tk = K if K <= 1024 else (1024 if K % 1024 == 0 else K // 2)
    tn = min(512, N)
    tt = min(tt, M)

    def body(a_ref, d_ref, o_ref):
        t = pl.program_id(2)
        part = lax.dot_general(a_ref[...], d_ref[...], TN, preferred_element_type=f32)

        @pl.when(t == 0)
        def _():
            o_ref[...] = part

        @pl.when(t > 0)
        def _():
            o_ref[...] += part

    return pl.pallas_call(
        body, name=name, grid=(K // tk, N // tn, M // tt),
        in_specs=[pl.BlockSpec((tt, tk), lambda k, n, t: (t, k)), pl.BlockSpec((tt, tn), lambda k, n, t: (t, n))],
        out_specs=pl.BlockSpec((tk, tn), lambda k, n, t: (k, n)),
        out_shape=S((K, N), f32), compiler_params=_params(("parallel", "parallel", "arbitrary")))(a, dy)


def _tile_spec(tm, width, base):
    return pl.BlockSpec((tm, width), lambda j, i: (i, base + j))


def _par_spec(width, base):
    return pl.BlockSpec((1, width), lambda j, i: (0, base + j))


def _full_spec(a):
    nd = a.ndim
    return pl.BlockSpec(a.shape, lambda j, i: (0,) * nd)


def _rowwise(name, f, tiles, ctiles, params, consts, outs, *, tm, J=1):
    T = tiles[0][0].shape[0]
    nt, nct, npar, nc = len(tiles), len(ctiles), len(params), len(consts)

    def body(*refs):
        tv = [r[...].astype(f32) for r in refs[:nt + nct]]
        pv = [r[...] for r in refs[nt + nct:nt + nct + npar + nc]]
        res = f(*tv, *pv)
        for o, v in zip(refs[nt + nct + npar + nc:], res):
            o[...] = v.astype(o.dtype)

    in_specs = ([_tile_spec(tm, w, b) for _, w, b in tiles + ctiles] + [_par_spec(w, b) for _, w, b in params]
                + [_full_spec(c) for c in consts])
    return pl.pallas_call(
        body, name=name, grid=(J, T // tm), in_specs=in_specs,
        out_specs=[_tile_spec(tm, w, 0) for w, _ in outs],
        out_shape=[S((T, J * w), dt) for w, dt in outs],
        compiler_params=_params(("parallel", "parallel")))(
            *[a for a, _, _ in tiles + ctiles], *[a for a, _, _ in params], *consts)


def _rowwise_bwd(name, f, tiles, ctiles, params, consts, douts, dtile_dtypes, *, tm, J=1):
    T = tiles[0][0].shape[0]
    nt, nct, npar, nc, nd = len(tiles), len(ctiles), len(params), len(consts), len(douts)

    def body(*refs):
        i = pl.program_id(1)
        k = 0
        tv = [r[...].astype(f32) for r in refs[k:k + nt]]; k += nt
        cv = [r[...].astype(f32) for r in refs[k:k + nct]]; k += nct
        pv = [r[...] for r in refs[k:k + npar]]; k += npar
        kv = [r[...] for r in refs[k:k + nc]]; k += nc
        dv = [r[...].astype(f32) for r in refs[k:k + nd]]; k += nd
        dt_refs = refs[k:k + nt]; k += nt
        dp_refs = refs[k:k + npar]
        _, vjp = jax.vjp(lambda tv_, pv_: tuple(f(*tv_, *cv, *pv_, *kv)), tv, pv)
        dts, dps = vjp(tuple(dv))
        for r, g in zip(dt_refs, dts):
            r[...] = g.astype(r.dtype)
        for r, g in zip(dp_refs, dps):
            @pl.when(i == 0)
            def _(r=r, g=g):
                r[...] = g

            @pl.when(i > 0)
            def _(r=r, g=g):
                r[...] += g

    in_specs = ([_tile_spec(tm, w, b) for _, w, b in tiles + ctiles] + [_par_spec(w, b) for _, w, b in params]
                + [_full_spec(c) for c in consts] + [_tile_spec(tm, w, b) for _, w, b in douts])
    res = pl.pallas_call(
        body, name=name, grid=(J, T // tm), in_specs=in_specs,
        out_specs=[_tile_spec(tm, w, 0) for _, w, _ in tiles] + [_par_spec(w, b) for _, w, b in params],
        out_shape=[S((T, J * w), dt) for (_, w, _), dt in zip(tiles, dtile_dtypes)] + [S(a.shape, f32) for a, _, _ in params],
        compiler_params=_params(("parallel", "arbitrary")))(
            *[a for a, _, _ in tiles + ctiles], *[a for a, _, _ in params], *consts, *[a for a, _, _ in douts])
    return res[:nt], res[nt:]


def _rms(x, g):
    return x * lax.rsqrt(jnp.mean(x * x, axis=-1, keepdims=True) + EPS) * g


def _f_first(x, g1):
    return (_rms(x, g1),)


def _f_first_bwd(x, g1):
    return (x, _rms(x, g1))


def _f_mid(x, m, g2, g3):
    x1 = x + _rms(m, g2)
    return (x1, _rms(x1, g3))


def _f_end(x1, fo, g4, g1n):
    x2 = x1 + _rms(fo, g4)
    return (x2, _rms(x2, g1n))


def _f_last(x1, fo, g4):
    return (x1 + _rms(fo, g4),)


@jax.custom_vjp
def _swap_halves(x):
    c = x.shape[1]
    lane = lax.broadcasted_iota(jnp.int32, x.shape, 1) % RDH
    return jnp.where(lane < RDH // 2, pltpu.roll(x, c - RDH // 2, axis=1), pltpu.roll(x, RDH // 2, axis=1))


_swap_halves.defvjp(lambda x: (_swap_halves(x), None), lambda _, g: (_swap_halves(g),))


def _f_rprep(rq, rk, cos, sin):
    rot = lambda t: t * cos + _swap_halves(t) * sin
    return (rot(rq), rot(rk) * (RDH ** -0.5))


def _f_rpost(yf, yb, rg, gn, gavg):
    y = yf + yb
    mu = jnp.dot(y, gavg, precision=HI, preferred_element_type=f32)
    yc = y - mu
    var = jnp.dot(yc * yc, gavg, precision=HI, preferred_element_type=f32)
    return (jax.nn.silu(rg) * (yc * lax.rsqrt(var + EPS) * gn),)


def _f_sprep(xs, dtr, dtb, alog, ex0, ex1):
    dt = jax.nn.softplus(dtr + dtb)
    la = dt * (-jnp.exp(alog))
    e0 = jnp.dot(dt, ex0, precision=HI, preferred_element_type=f32)
    e1 = jnp.dot(dt, ex1, precision=HI, preferred_element_type=f32)
    return (xs * e0, xs * e1, la)


def _f_spost(yf, yb, xs, z, dsk, ng):
    y = (yf + yb + xs * dsk) * jax.nn.silu(z)
    y = y * lax.rsqrt(jnp.mean(y * y, axis=-1, keepdims=True) + EPS)
    return (y * ng,)


def _loss_call(y, tgt, *, tm=256):
    T = y.shape[0]

    def body(y_ref, t_ref, dy_ref, l_ref):
        i = pl.program_id(0)
        e = y_ref[...] - t_ref[...]
        dy_ref[...] = e * (1.0 / D)
        part = jnp.zeros((8, LANES), f32) + 0.5 * jnp.sum(jnp.mean(e * e, axis=-1, keepdims=True))

        @pl.when(i == 0)
        def _():
            l_ref[...] = part

        @pl.when(i > 0)
        def _():
            l_ref[...] += part

    return pl.pallas_call(
        body, name="loss_head", grid=(T // tm,),
        in_specs=[pl.BlockSpec((tm, D), lambda i: (i, 0))] * 2,
        out_specs=[pl.BlockSpec((tm, D), lambda i: (i, 0)), pl.BlockSpec((8, LANES), lambda i: (0, 0))],
        out_shape=[S((T, D), f32), S((8, LANES), f32)], compiler_params=_params(("arbitrary",)))(y, tgt)


def _colsum(x, *, name, tm=512):
    T, C = x.shape

    def body(x_ref, o_ref):
        i = pl.program_id(0)
        part = jnp.sum(x_ref[...], axis=0, keepdims=True)

        @pl.when(i == 0)
        def _():
            o_ref[...] = part

        @pl.when(i > 0)
        def _():
            o_ref[...] += part

    return pl.pallas_call(
        body, name=name, grid=(T // tm,), in_specs=[pl.BlockSpec((tm, C), lambda i: (i, 0))],
        out_specs=pl.BlockSpec((1, C), lambda i: (0, 0)), out_shape=S((1, C), f32),
        compiler_params=_params(("arbitrary",)))(x)


def _scan_step(h, q, k, v, a, rev):
    L = q.shape[0]
    Hg = v.shape[0]
    ii = lax.broadcasted_iota(jnp.int32, (L, L), 0)
    jj = lax.broadcasted_iota(jnp.int32, (L, L), 1)
    if rev:
        tri, tri_t, dmask = (jj >= ii), (ii >= jj), (jj > ii)
    else:
        tri, tri_t, dmask = (jj <= ii), (ii <= jj), (jj <= ii)
    cs = jnp.dot(tri.astype(f32), a, precision=HI, preferred_element_type=f32)
    cs_t = lax.dot_general(a, tri_t.astype(f32), TN, precision=HI, preferred_element_type=f32)
    tot = jnp.sum(a, axis=0, keepdims=True)
    qk = lax.dot_general(q, k, NT, preferred_element_type=f32)
    hs, ys = [], []
    for hh in range(Hg):
        c_col = cs[:, hh:hh + 1]
        dec = jnp.exp(jnp.where(dmask, c_col - cs_t[hh:hh + 1, :], -1e30))
        y = jnp.dot(qk * dec, v[hh], preferred_element_type=f32)
        y = y + jnp.dot(q, h[hh], preferred_element_type=f32) * jnp.exp(c_col)
        t_all = tot[:, hh:hh + 1]
        hn = h[hh] * jnp.exp(t_all) + lax.dot_general(k, v[hh] * jnp.exp(t_all - c_col), TN, preferred_element_type=f32)
        hs.append(hn)
        ys.append(y)
    return jnp.stack(hs), jnp.stack(ys)


def _scan_specs(gb, N, Hg, P, Ha, cm):
    qs = pl.BlockSpec((gb, CHUNK, N), lambda g, c: (g, cm(c), 0))
    vs = pl.BlockSpec((gb * Hg, CHUNK, P), lambda g, c: (g, cm(c), 0))
    as_ = pl.BlockSpec((gb, CHUNK, Ha), lambda g, c: (g, cm(c), 0))
    hs = pl.BlockSpec((gb, 1, Hg, N, P), lambda g, c: (g, cm(c), 0, 0, 0))
    return qs, vs, as_, hs


def _scan_fwd(q, k, v, a, *, rev, name):
    G, T, N = q.shape
    Ht, _, P = v.shape
    Hg, Ha, NC = Ht // G, a.shape[2], T // CHUNK
    gb = SCAN_HEADS_PER_STEP // Hg
    cm = (lambda c: NC - 1 - c) if rev else (lambda c: c)
    qs, vs, as_, hs = _scan_specs(gb, N, Hg, P, Ha, cm)

    def body(q_ref, k_ref, v_ref, a_ref, y_ref, hs_ref, h_scr):
        @pl.when(pl.program_id(1) == 0)
        def _():
            h_scr[...] = jnp.zeros_like(h_scr)

        for j in range(gb):
            h = h_scr[j]
            hs_ref[j, 0] = h
            hn, y = _scan_step(h, q_ref[j], k_ref[j], v_ref[j * Hg:(j + 1) * Hg], a_ref[j], rev)
            y_ref[j * Hg:(j + 1) * Hg] = y
            h_scr[j] = hn

    return pl.pallas_call(
        body, name=name, grid=(G // gb, NC), in_specs=[qs, qs, vs, as_], out_specs=[vs, hs],
        out_shape=[S((Ht, T, P), f32), S((G, NC, Hg, N, P), f32)],
        scratch_shapes=[pltpu.VMEM((gb, Hg, N, P), f32)],
        compiler_params=_params(("parallel", "arbitrary")))(q, k, v, a)


def _scan_bwd(q, k, v, a, hsave, dy, *, rev, name):
    G, T, N = q.shape
    Ht, _, P = v.shape
    Hg, Ha, NC = Ht // G, a.shape[2], T // CHUNK
    gb = SCAN_HEADS_PER_STEP // Hg
    cm = (lambda c: c) if rev else (lambda c: NC - 1 - c)
    qs, vs, as_, hs = _scan_specs(gb, N, Hg, P, Ha, cm)

    def body(q_ref, k_ref, v_ref, a_ref, hs_ref, dy_ref, dq_ref, dk_ref, dv_ref, da_ref, dh_scr):
        @pl.when(pl.program_id(1) == 0)
        def _():
            dh_scr[...] = jnp.zeros_like(dh_scr)

        for j in range(gb):
            heads = slice(j * Hg, (j + 1) * Hg)
            _, vjp = jax.vjp(functools.partial(_scan_step, rev=rev), hs_ref[j, 0], q_ref[j], k_ref[j], v_ref[heads], a_ref[j])
            dh, dq, dk, dv, da = vjp((dh_scr[j], dy_ref[heads]))
            dq_ref[j] = dq
            dk_ref[j] = dk
            dv_ref[heads] = dv
            da_ref[j] = da
            dh_scr[j] = dh

    return pl.pallas_call(
        body, name=name, grid=(G // gb, NC), in_specs=[qs, qs, vs, as_, hs, vs], out_specs=[qs, qs, vs, as_],
        out_shape=[S((G, T, N), f32), S((G, T, N), f32), S((Ht, T, P), f32), S((G, T, Ha), f32)],
        scratch_shapes=[pltpu.VMEM((gb, Hg, N, P), f32)],
        compiler_params=_params(("parallel", "arbitrary")))(q, k, v, a, hsave, dy)


def _na_block_case(rb, nrb):
    return jnp.where(rb == 0, 0, jnp.where(rb == nrb - 1, 2, 1))


def _na_key_start(rb, rows):
    return pl.multiple_of(jnp.clip(rb * NA_QROWS - NA_WR // 2, 0, rows - NA_KROWS) * GRID_W, 256)


def _na_fwd(q, k, v, bias, *, name):
    H, T, P = q.shape
    rows = T // GRID_W
    nq, nk = NA_QROWS * GRID_W, NA_KROWS * GRID_W
    nrb = T // nq
    scale = NADH ** -0.5

    def body(q_ref, k_ref, v_ref, b_ref, o_ref, l_ref):
        ks = _na_key_start(pl.program_id(1), rows)
        kw = k_ref[0, pl.ds(ks, nk), :]
        vw = v_ref[0, pl.ds(ks, nk), :]
        s = lax.dot_general(q_ref[0], kw, NT, preferred_element_type=f32) * scale + b_ref[0, 0]
        m = jnp.max(s, axis=1, keepdims=True)
        p = jnp.exp(s - m)
        l = jnp.sum(p, axis=1, keepdims=True)
        o_ref[0] = jnp.dot(p.astype(bf16), vw, preferred_element_type=f32) / l
        l_ref[0] = m + jnp.log(l)

    return pl.pallas_call(
        body, name=name, grid=(H, nrb),
        in_specs=[pl.BlockSpec((1, nq, P), lambda h, r: (h, r, 0)),
                  pl.BlockSpec((1, T, P), lambda h, r: (h, 0, 0)), pl.BlockSpec((1, T, P), lambda h, r: (h, 0, 0)),
                  pl.BlockSpec((1, 1, nq, nk), lambda h, r: (h, _na_block_case(r, nrb), 0, 0))],
        out_specs=[pl.BlockSpec((1, nq, P), lambda h, r: (h, r, 0)), pl.BlockSpec((1, nq, 1), lambda h, r: (h, r, 0))],
        out_shape=[S((H, T, P), f32), S((H, T, 1), f32)],
        compiler_params=_params(("parallel", "arbitrary")))(q, k, v, bias)


def _na_bwd(q, k, v, bias, o, lse, do, *, name):
    H, T, P = q.shape
    rows = T // GRID_W
    nq, nk = NA_QROWS * GRID_W, NA_KROWS * GRID_W
    nrb = T // nq
    scale = NADH ** -0.5

    def body(q_ref, k_ref, v_ref, b_ref, o_ref, l_ref, do_ref, dq_ref, dk_ref, dv_ref, db_ref):
        rb = pl.program_id(1)

        @pl.when(rb == 0)
        def _():
            dk_ref[...] = jnp.zeros_like(dk_ref)
            dv_ref[...] = jnp.zeros_like(dv_ref)

        ks = _na_key_start(rb, rows)
        qv = q_ref[0]
        kw = k_ref[0, pl.ds(ks, nk), :]
        vw = v_ref[0, pl.ds(ks, nk), :]
        s = lax.dot_general(qv, kw, NT, preferred_element_type=f32) * scale + b_ref[0, 0]
        p = jnp.exp(s - l_ref[0])
        do_ = do_ref[0]
        dob = do_.astype(bf16)
        dp = lax.dot_general(dob, vw, NT, preferred_element_type=f32)
        ds = p * (dp - jnp.sum(do_ * o_ref[0], axis=1, keepdims=True))
        dsb = ds.astype(bf16)
        dq_ref[0] = jnp.dot(dsb, kw, preferred_element_type=f32) * scale
        dk_ref[0, pl.ds(ks, nk), :] += lax.dot_general(dsb, qv, TN, preferred_element_type=f32) * scale
        dv_ref[0, pl.ds(ks, nk), :] += lax.dot_general(p.astype(bf16), dob, TN, preferred_element_type=f32)
        first = (rb == 0) | (rb == 1) | (rb == nrb - 1)

        @pl.when(first)
        def _():
            db_ref[0, 0] = ds

        @pl.when(jnp.logical_not(first))
        def _():
            db_ref[0, 0] += ds

    qspec = pl.BlockSpec((1, nq, P), lambda h, r: (h, r, 0))
    fspec = pl.BlockSpec((1, T, P), lambda h, r: (h, 0, 0))
    bspec = pl.BlockSpec((1, 1, nq, nk), lambda h, r: (h, _na_block_case(r, nrb), 0, 0))
    return pl.pallas_call(
        body, name=name, grid=(H, nrb),
        in_specs=[qspec, fspec, fspec, bspec, qspec, pl.BlockSpec((1, nq, 1), lambda h, r: (h, r, 0)), qspec],
        out_specs=[qspec, fspec, fspec, bspec],
        out_shape=[S((H, T, P), f32), S((H, T, P), f32), S((H, T, P), f32), S(bias.shape, f32)],
        compiler_params=_params(("parallel", "arbitrary")))(q, k, v, bias, o, lse, do)


def _na_bias_tables(rows):
    c = np.arange(GRID_W)[:, None]
    kc = np.arange(GRID_W)[None, :]
    cstart = np.clip(c - NA_WC // 2, 0, GRID_W - NA_WC)
    valid_c = (kc >= cstart) & (kc < cstart + NA_WC)
    dc = kc - c + NA_WC - 1
    E = (valid_c[:, :, None] & (dc[:, :, None] == np.arange(2 * NA_WC - 1)[None, None, :])).astype(np.float32)
    A = np.zeros((3, NA_QROWS, NA_KROWS, 2 * NA_WR - 1), np.float32)
    valid_r = np.zeros((3, NA_QROWS, NA_KROWS), bool)
    for z, r0 in enumerate((0, NA_QROWS, rows - NA_QROWS)):
        ks = int(np.clip(r0 - NA_WR // 2, 0, rows - NA_KROWS))
        for ri in range(NA_QROWS):
            r = r0 + ri
            rs = int(np.clip(r - NA_WR // 2, 0, rows - NA_WR))
            for kri in range(NA_KROWS):
                kr = ks + kri
                if rs <= kr < rs + NA_WR:
                    valid_r[z, ri, kri] = True
                    A[z, ri, kri, kr - r + NA_WR - 1] = 1.0
    mask = np.where(valid_r[:, :, None, :, None] & valid_c[None, None, :, None, :], 0.0, -1e30).astype(np.float32)
    return E, A, mask


def _na_bias(rpb, rows):
    E, A, mask = _na_bias_tables(rows)
    r1 = jnp.einsum("hde,cke->hdck", rpb, E, precision=HI)
    b = jnp.einsum("hdck,zabd->hzacbk", r1, A, precision=HI) + mask[None]
    return b.reshape(rpb.shape[0], 3, NA_QROWS * GRID_W, NA_KROWS * GRID_W)


def _conv_shifts(prev, cur, nxt, i, n_i, W):
    tm = cur.shape[0]
    prev = jnp.where(i > 0, prev, 0.0)
    nxt = jnp.where(i < n_i - 1, nxt, 0.0)
    ext = jnp.concatenate([prev, cur, nxt], axis=0)
    out = []
    for w in range(W):
        s = (W // 2 - w) % (tm + 16)
        out.append((ext if s == 0 else pltpu.roll(ext, s, axis=0))[8:8 + tm])
    return out


def _conv_act(u, mode):
    if mode == "silu":
        return jax.nn.silu(u)
    if mode == "geglu":
        half = u.shape[1] // 2
        return jax.nn.gelu(u[:, :half], approximate=True) * u[:, half:]
    return u


def _conv_specs(T, tm, tc, xbase):
    r8 = tm // 8
    last = T // 8 - 1
    cur = pl.BlockSpec((tm, tc), lambda j, i: (i, xbase + j))
    prev = pl.BlockSpec((8, tc), lambda j, i: (jnp.maximum(i * r8 - 1, 0), xbase + j))
    nxt = pl.BlockSpec((8, tc), lambda j, i: (jnp.minimum((i + 1) * r8, last), xbase + j))
    return cur, prev, nxt


def _conv(x, w8, b, *, mode, W, name, C, xbase=0, tm=512, tc=512, out_dtype=f32):
    T = x.shape[0]
    NI, J = T // tm, C // tc
    tco = tc // 2 if mode == "geglu" else tc
    cur, prev, nxt = _conv_specs(T, tm, tc, xbase)

    def body(xc, xp, xn, w_ref, b_ref, o_ref):
        sh = _conv_shifts(xp[...].astype(f32), xc[...].astype(f32), xn[...].astype(f32), pl.program_id(1), NI, W)
        wv = w_ref[...]
        u = sh[0] * wv[0:1, :]
        for w in range(1, W):
            u = u + sh[w] * wv[w:w + 1, :]
        if mode != "none":
            u = u + b_ref[...]
        o_ref[...] = _conv_act(u, mode).astype(o_ref.dtype)

    return pl.pallas_call(
        body, name=name, grid=(J, NI),
        in_specs=[cur, prev, nxt, pl.BlockSpec((8, tc), lambda j, i: (0, j)), pl.BlockSpec((1, tc), lambda j, i: (0, j))],
        out_specs=pl.BlockSpec((tm, tco), lambda j, i: (i, j)), out_shape=S((T, J * tco), out_dtype),
        compiler_params=_params(("parallel", "parallel")))(x, x, x, w8, b)


def _conv_bwd_a(x, w8, b, dact, *, mode, W, name, C, xbase=0, tm=512, tc=512):
    T = x.shape[0]
    NI, J = T // tm, C // tc
    tco = tc // 2 if mode == "geglu" else tc
    cur, prev, nxt = _conv_specs(T, tm, tc, xbase)

    def body(xc, xp, xn, w_ref, b_ref, d_ref, du_ref, dw_ref, db_ref):
        i = pl.program_id(1)
        sh = _conv_shifts(xp[...].astype(f32), xc[...].astype(f32), xn[...].astype(f32), i, NI, W)
        wv = w_ref[...]
        u = b_ref[...] + sh[0] * wv[0:1, :]
        for w in range(1, W):
            u = u + sh[w] * wv[w:w + 1, :]
        _, vjp = jax.vjp(functools.partial(_conv_act, mode=mode), u)
        du = vjp(d_ref[...].astype(f32))[0]
        du_ref[...] = du

        @pl.when(i == 0)
        def _():
            dw_ref[...] = jnp.zeros_like(dw_ref)
            db_ref[...] = jnp.zeros_like(db_ref)

        db_ref[...] += jnp.sum(du, axis=0, keepdims=True)
        for w in range(W):
            dw_ref[w:w + 1, :] += jnp.sum(du * sh[w], axis=0, keepdims=True)

    return pl.pallas_call(
        body, name=name, grid=(J, NI),
        in_specs=[cur, prev, nxt, pl.BlockSpec((8, tc), lambda j, i: (0, j)), pl.BlockSpec((1, tc), lambda j, i: (0, j)),
                  pl.BlockSpec((tm, tco), lambda j, i: (i, j))],
        out_specs=[pl.BlockSpec((tm, tc), lambda j, i: (i, j)), pl.BlockSpec((8, tc), lambda j, i: (0, j)),
                   pl.BlockSpec((1, tc), lambda j, i: (0, j))],
        out_shape=[S((T, C), f32), S((8, C), f32), S((1, C), f32)],
        compiler_params=_params(("parallel", "arbitrary")))(x, x, x, w8, b, dact)


def _pad8(w):
    return jnp.concatenate([w, jnp.zeros((8 - w.shape[0], w.shape[1]), w.dtype)], axis=0)


def _exchange(arrs, *, name):
    n = len(arrs)
    ncopy = (NDEV - 1) * n

    def body(*refs):
        ins, outs = refs[:n], refs[n:2 * n]
        send_sems, recv_sems, loc_sems = refs[2 * n:]
        x, y, c = lax.axis_index("x"), lax.axis_index("y"), lax.axis_index("c")
        me = 4 * x + 2 * y + c

        def src(a, p):
            return ins[a].at[p] if arrs[a][1] else ins[a]

        local = [pltpu.make_async_copy(src(a, me), outs[a].at[me], loc_sems.at[a]) for a in range(n)]
        for cp in local:
            cp.start()
        sent = []
        for kk in range(1, NDEV):
            px = 1 - x if kk & 4 else x
            py = 1 - y if kk & 2 else y
            pc = 1 - c if kk & 1 else c
            peer = 4 * px + 2 * py + pc
            for a in range(n):
                idx = (kk - 1) * n + a
                mk = lambda dst_slot, a=a, idx=idx, peer=peer, dev=(px, py, pc): pltpu.make_async_remote_copy(
                    src_ref=src(a, peer), dst_ref=outs[a].at[dst_slot], send_sem=send_sems.at[idx], recv_sem=recv_sems.at[idx],
                    device_id=dev, device_id_type=pl.DeviceIdType.MESH)
                mk(me).start()
                sent.append((mk, peer))
        for mk, peer in sent:
            mk(peer).wait_recv()
        for mk, peer in sent:
            mk(peer).wait_send()
        for cp in local:
            cp.wait()

    any_spec = pl.BlockSpec(memory_space=pl.ANY)
    return pl.pallas_call(
        body, name=name, in_specs=[any_spec] * n, out_specs=[any_spec] * n,
        out_shape=[S(a.shape if pp else (NDEV,) + a.shape, a.dtype) for a, pp in arrs],
        scratch_shapes=[pltpu.SemaphoreType.DMA((ncopy,)), pltpu.SemaphoreType.DMA((ncopy,)), pltpu.SemaphoreType.DMA((n,))],
        )(*[a for a, _ in arrs])


def _adamw(r, w, m, v, *, name, tr):
    M, C = w.shape

    def body(r_ref, w_ref, m_ref, v_ref, g_ref, d_ref, nm_ref, nv_ref):
        g = r_ref[0].astype(f32)
        for s in range(1, NDEV):
            g = g + r_ref[s].astype(f32)
        m_ = B1 * m_ref[...] + (1.0 - B1) * g
        v_ = B2 * v_ref[...] + (1.0 - B2) * jnp.square(g)
        m_hat = m_ / (1.0 - B1 ** STEP)
        v_hat = v_ / (1.0 - B2 ** STEP)
        g_ref[...] = g
        d_ref[...] = -LR * (m_hat / (jnp.sqrt(v_hat) + AEPS) + WD * w_ref[...])
        nm_ref[...] = m_
        nv_ref[...] = v_

    row = pl.BlockSpec((tr, C), lambda i: (i, 0))
    return pl.pallas_call(
        body, name=name, grid=(M // tr,),
        in_specs=[pl.BlockSpec((NDEV, tr, C), lambda i: (0, i, 0)), row, row, row],
        out_specs=[row] * 4, out_shape=[S((M, C), f32)] * 4, compiler_params=_params(("parallel",)))(r, w, m, v)


def _colmove(ins, in_slots, outs, moves, *, tk, name):
    R = ins[0].shape[1] if in_slots[0] else ins[0].shape[0]
    n_in = len(ins)

    def body(*refs):
        for ii, isl, ic, oi, osl, oc, w in moves:
            src, dst = refs[ii], refs[n_in + oi]
            val = src[:, ic:ic + w] if isl is None else src[isl, :, ic:ic + w]
            if osl is None:
                dst[:, oc:oc + w] = val.astype(dst.dtype)
            else:
                dst[osl, :, oc:oc + w] = val.astype(dst.dtype)

    def spec(is_slots, C):
        return pl.BlockSpec((NDEV, tk, C), lambda i: (0, i, 0)) if is_slots else pl.BlockSpec((tk, C), lambda i: (i, 0))

    return pl.pallas_call(
        body, name=name, grid=(R // tk,),
        in_specs=[spec(sl, a.shape[-1]) for a, sl in zip(ins, in_slots)],
        out_specs=[spec(sl, C) for sl, C, _ in outs],
        out_shape=[S((NDEV, R, C) if sl else (R, C), dt) for sl, C, dt in outs],
        compiler_params=_params(("parallel",)))(*ins)


def _col_pieces(n8, cuts, place):
    out = []
    for p in range(NDEV):
        lo, hi = p * n8, (p + 1) * n8
        edges = [lo] + [c for c in cuts if lo < c < hi] + [hi]
        for a, b in zip(edges[:-1], edges[1:]):
            out.append((p, a - lo) + place(a) + (b - a,))
    return out


def _place_plain(c):
    return (0, c)


def _place_ssd_in(c):
    return (0, c) if c < SSD_INNER + SSD_XBC else (1, c - (SSD_INNER + SSD_XBC))


def _place_ffn_up(c):
    h = FFN_TC // 2
    return (0, (c // h) * FFN_TC + c % h) if c < FFN else (0, ((c - FFN) // h) * FFN_TC + h + (c - FFN) % h)


_COL_LAYOUTS = {
    "ab_w_in": ([], _place_plain, [4 * RW + 3 * NAW]),
    "c_w_in": ([SSD_INNER + SSD_XBC], _place_ssd_in, [SSD_INNER + SSD_XBC, 2 * SSD_H]),
    "ffn_w_up": (list(range(FFN_TC // 2, 2 * FFN, FFN_TC // 2)), _place_ffn_up, [2 * FFN]),
}


def _cols_from_slots(g, which, *, name):
    cuts, place, widths = _COL_LAYOUTS[which]
    moves = [(0, p, sc, mi, None, mc, w) for p, sc, mi, mc, w in _col_pieces(g.shape[2], cuts, place)]
    return _colmove([g], [True], [(False, w, g.dtype) for w in widths], moves, tk=256, name=name)


def _cols_to_slots(mats, which, dtype, *, name):
    cuts, place, widths = _COL_LAYOUTS[which]
    n8 = sum(widths) // NDEV
    moves = [(mi, None, mc, 0, p, sc, w) for p, sc, mi, mc, w in _col_pieces(n8, cuts, place)]
    return _colmove(list(mats), [False] * len(mats), [(True, n8, dtype)], moves, tk=256, name=name)[0]


def _tm2hm(a, H):
    T = a.shape[0]
    return a.reshape(T, H, -1).transpose(1, 0, 2)


def _hm2tm(a):
    H, T, P = a.shape
    return a.transpose(1, 0, 2).reshape(T, H * P)


def _pack(parts, dtype, row_mult):
    flat = jnp.concatenate([p.reshape(-1).astype(dtype) for p in parts])
    rows = -(-flat.shape[0] // LANES)
    rows = -(-rows // row_mult) * row_mult
    return jnp.pad(flat, (0, rows * LANES - flat.shape[0])).reshape(rows, LANES)


def _unpack(buf, shapes, lead=()):
    flat = buf.reshape(lead + (-1,))
    out, off = [], 0
    for shp in shapes:
        n = int(np.prod(shp))
        out.append(flat[..., off:off + n].reshape(lead + tuple(shp)))
        off += n
    return out


def _to_slots(full, ax):
    shp = full.shape
    return jnp.moveaxis(full.reshape(shp[:ax] + (NDEV, shp[ax] // NDEV) + shp[ax + 1:]), ax, 0)


def _from_slots(g, ax):
    t = jnp.moveaxis(g, 0, ax)
    shp = t.shape
    return t.reshape(shp[:ax] + (shp[ax] * shp[ax + 1],) + shp[ax + 2:])


def _ffn_perm(a):
    lead = a.shape[:-1]
    h = FFN_TC // 2
    return jnp.swapaxes(a.reshape(lead + (2, FFN // h, h)), -3, -2).reshape(lead + (2 * FFN,))


def _ffn_unperm(a):
    lead = a.shape[:-1]
    h = FFN_TC // 2
    return jnp.swapaxes(a.reshape(lead + (FFN // h, 2, h)), -3, -2).reshape(lead + (2 * FFN,))


def _rope_tables(T):
    half = RDH // 2
    inv = 1.0 / (ROPE_BASE ** (jnp.arange(half, dtype=f32) / half))
    ang = jnp.arange(T, dtype=f32)[:, None] * inv[None, :]
    cos, sin = jnp.cos(ang), jnp.sin(ang)
    cos_t = jnp.tile(jnp.concatenate([cos, cos], axis=1), (1, RH))
    sin_t = jnp.tile(jnp.concatenate([-sin, sin], axis=1), (1, RH))
    return cos_t, sin_t


def _group_avg():
    g = np.arange(RW) // RDH
    return jnp.asarray((g[:, None] == g[None, :]).astype(np.float32) / RDH)


def _head_expand():
    hd = np.arange(SSD_INNER) // SSD_HD
    rows = np.arange(2 * SSD_H)
    ex0 = (rows[:, None] == hd[None, :]).astype(np.float32)
    ex1 = (rows[:, None] == SSD_H + hd[None, :]).astype(np.float32)
    return jnp.asarray(ex0), jnp.asarray(ex1)


def kernel(x, norm_mix_pre, norm_mix_post, norm_ffn_pre, norm_ffn_post, ab_w_in, ab_ret_decay_logit, ab_ret_gn_g, ab_na_rpb, ab_w_out, c_w_in, c_conv_w, c_conv_b, c_dt_bias, c_a_log, c_d_skip, c_norm_g, c_w_out, ffn_w_up, ffn_conv_w, ffn_conv_b, ffn_w_down, loss_target, m_norm_mix_pre, m_norm_mix_post, m_norm_ffn_pre, m_norm_ffn_post, m_ab_w_in, m_ab_ret_decay_logit, m_ab_ret_gn_g, m_ab_na_rpb, m_ab_w_out, m_c_w_in, m_c_conv_w, m_c_conv_b, m_c_dt_bias, m_c_a_log, m_c_d_skip, m_c_norm_g, m_c_w_out, m_ffn_w_up, m_ffn_conv_w, m_ffn_conv_b, m_ffn_w_down, v_norm_mix_pre, v_norm_mix_post, v_norm_ffn_pre, v_norm_ffn_post, v_ab_w_in, v_ab_ret_decay_logit, v_ab_ret_gn_g, v_ab_na_rpb, v_ab_w_out, v_c_w_in, v_c_conv_w, v_c_conv_b, v_c_dt_bias, v_c_a_log, v_c_d_skip, v_c_norm_g, v_c_w_out, v_ffn_w_up, v_ffn_conv_w, v_ffn_conv_b, v_ffn_w_down):
    W = dict(norm_mix_pre=norm_mix_pre, norm_mix_post=norm_mix_post, norm_ffn_pre=norm_ffn_pre, norm_ffn_post=norm_ffn_post, ab_w_in=ab_w_in, ab_ret_decay_logit=ab_ret_decay_logit, ab_ret_gn_g=ab_ret_gn_g, ab_na_rpb=ab_na_rpb, ab_w_out=ab_w_out, c_w_in=c_w_in, c_conv_w=c_conv_w, c_conv_b=c_conv_b, c_dt_bias=c_dt_bias, c_a_log=c_a_log, c_d_skip=c_d_skip, c_norm_g=c_norm_g, c_w_out=c_w_out, ffn_w_up=ffn_w_up, ffn_conv_w=ffn_conv_w, ffn_conv_b=ffn_conv_b, ffn_w_down=ffn_w_down)
    Mo = dict(norm_mix_pre=m_norm_mix_pre, norm_mix_post=m_norm_mix_post, norm_ffn_pre=m_norm_ffn_pre, norm_ffn_post=m_norm_ffn_post, ab_w_in=m_ab_w_in, ab_ret_decay_logit=m_ab_ret_decay_logit, ab_ret_gn_g=m_ab_ret_gn_g, ab_na_rpb=m_ab_na_rpb, ab_w_out=m_ab_w_out, c_w_in=m_c_w_in, c_conv_w=m_c_conv_w, c_conv_b=m_c_conv_b, c_dt_bias=m_c_dt_bias, c_a_log=m_c_a_log, c_d_skip=m_c_d_skip, c_norm_g=m_c_norm_g, c_w_out=m_c_w_out, ffn_w_up=m_ffn_w_up, ffn_conv_w=m_ffn_conv_w, ffn_conv_b=m_ffn_conv_b, ffn_w_down=m_ffn_w_down)
    Vo = dict(norm_mix_pre=v_norm_mix_pre, norm_mix_post=v_norm_mix_post, norm_ffn_pre=v_norm_ffn_pre, norm_ffn_post=v_norm_ffn_post, ab_w_in=v_ab_w_in, ab_ret_decay_logit=v_ab_ret_decay_logit, ab_ret_gn_g=v_ab_ret_gn_g, ab_na_rpb=v_ab_na_rpb, ab_w_out=v_ab_w_out, c_w_in=v_c_w_in, c_conv_w=v_c_conv_w, c_conv_b=v_c_conv_b, c_dt_bias=v_c_dt_bias, c_a_log=v_c_a_log, c_d_skip=v_c_d_skip, c_norm_g=v_c_norm_g, c_w_out=v_c_w_out, ffn_w_up=v_ffn_w_up, ffn_conv_w=v_ffn_conv_w, ffn_conv_b=v_ffn_conv_b, ffn_w_down=v_ffn_w_down)
    return _train_step(x[0], loss_target[0], W, Mo, Vo)


def _train_step(x, tgt, W, Mo, Vo):
    T = x.shape[0]
    rows = T // GRID_W

    col = lambda d, n, dt: d[n].reshape(-1, d[n].shape[-1]).astype(dt)
    rows_of = lambda d, dt: jnp.concatenate([col(d, n, dt) for n in ROW_SHARDED], axis=0)
    small = _pack([W[n] for n, _ in SHARDED[N_BIG:]], f32, 8)
    gat = _exchange([(col(W, n, bf16), False) for n in COL_SHARDED] + [(rows_of(W, bf16), False), (small, False)],
                    name="gather_weights")
    per_layer = lambda m: m.reshape(-1, D, m.shape[-1])
    w_ab_in = per_layer(_cols_from_slots(gat[0], "ab_w_in", name="cols_ab_w_in")[0])
    w_zx, w_dt = [per_layer(m) for m in _cols_from_slots(gat[1], "c_w_in", name="cols_c_w_in")]
    w_up = per_layer(_cols_from_slots(gat[2], "ffn_w_up", name="cols_ffn_w_up")[0])
    full, off = {}, 0
    for n in ROW_SHARDED:
        L, r = W[n].shape[0], W[n].shape[1]
        full[n] = jnp.swapaxes(gat[3][:, off:off + L * r].reshape(NDEV, L, r, D), 0, 1).reshape(L, NDEV * r, D)
        off += L * r
    gs = _unpack(gat[4], [W[n].shape for n, _ in SHARDED[N_BIG:]], (NDEV,))
    full.update({n: _from_slots(g, ax) for (n, ax), g in zip(SHARDED[N_BIG:], gs)})
    w_ab_out, w_c_out, w_down = full["ab_w_out"], full["c_w_out"], full["ffn_w_down"]
    c_cw8 = [_pad8(full["c_conv_w"][i]) for i in range(2)]
    c_cb = [full["c_conv_b"][i][None] for i in range(2)]
    c_ng = [full["c_norm_g"][i][None] for i in range(2)]
    f_cw8 = [_pad8(_ffn_perm(full["ffn_conv_w"][l])) for l in range(DEPTH)]
    f_cb = [_ffn_perm(W["ffn_conv_b"][l])[None] for l in range(DEPTH)]

    g1 = [W["norm_mix_pre"][l][None] for l in range(DEPTH)]
    g2 = [W["norm_mix_post"][l][None] for l in range(DEPTH)]
    g3 = [W["norm_ffn_pre"][l][None] for l in range(DEPTH)]
    g4 = [W["norm_ffn_post"][l][None] for l in range(DEPTH)]
    cos_t, sin_t = _rope_tables(T)
    gavg = _group_avg()
    ex0, ex1 = _head_expand()

    def log_gamma(logit):
        return -jax.nn.softplus(-logit)

    def ret_decays(lg):
        return [jnp.broadcast_to(lg[d][:, None, None], (RH, T, 8)) for d in range(2)]

    saved = []
    xs_ = x
    hn = _rowwise("norm_first", _f_first, [(x, D, 0)], [], [(g1[0], D, 0)], [], [(D, bf16)], tm=256)[0]
    for l in range(DEPTH):
        i = l // 2
        sv = dict(x=xs_, hn=hn)
        if l % 2 == 0:
            proj = _mm_nn(hn, w_ab_in[i], name=f"ab_in_{l}")
            qr, kr = _rowwise(f"ret_prep_{l}", _f_rprep, [(proj, RW, 0), (proj, RW, 1)], [(cos_t, RW, 0), (sin_t, RW, 0)], [], [],
                              [(RW, f32), (RW, f32)], tm=256)
            q_h, k_h, v_h = _tm2hm(qr, RH), _tm2hm(kr, RH), _tm2hm(proj[:, 2 * RW:3 * RW], RH)
            lg, lg_vjp = jax.vjp(log_gamma, W["ab_ret_decay_logit"][i])
            a_f, a_b = ret_decays(lg)
            yf, hsf = _scan_fwd(q_h, k_h, v_h, a_f, rev=False, name=f"ret_scan_f_{l}")
            yb, hsb = _scan_fwd(q_h, k_h, v_h, a_b, rev=True, name=f"ret_scan_b_{l}")
            yf_t, yb_t = _hm2tm(yf), _hm2tm(yb)
            gn = W["ab_ret_gn_g"][i][None]
            ret = _rowwise(f"ret_post_{l}", _f_rpost, [(yf_t, RW, 0), (yb_t, RW, 0), (proj, RW, 3)], [], [(gn, RW, 0)], [gavg],
                           [(RW, bf16)], tm=256)[0]
            nq, nk, nv = [_tm2hm(proj[:, (4 + j) * RW:(5 + j) * RW], NAH).astype(bf16) for j in range(3)]
            bias, bias_vjp = jax.vjp(functools.partial(_na_bias, rows=rows), W["ab_na_rpb"][i])
            na_o, na_l = _na_fwd(nq, nk, nv, bias, name=f"na_fwd_{l}")
            cat = jnp.concatenate([ret, _hm2tm(na_o).astype(bf16)], axis=1)
            mo = _mm_nn(cat, w_ab_out[i], name=f"ab_out_{l}")
            sv.update(proj=proj, q_h=q_h, k_h=k_h, v_h=v_h, a_f=a_f, a_b=a_b, hsf=hsf, hsb=hsb, yf_t=yf_t, yb_t=yb_t, gn=gn,
                      nq=nq, nk=nk, nv=nv, bias=bias, bias_vjp=bias_vjp, lg_vjp=lg_vjp, na_o=na_o, na_l=na_l, cat=cat)
        else:
            zx = _mm_nn(hn, w_zx[i], name=f"c_in_{l}")
            dtr = _mm_nn(hn, w_dt[i], name=f"c_in_dt_{l}")
            xa = _conv(zx, c_cw8[i], c_cb[i], mode="silu", W=SSD_CONV, name=f"c_conv_{l}", C=SSD_XBC, xbase=SSD_INNER // 512)
            dtb, alog = W["c_dt_bias"][i].reshape(1, 2 * SSD_H), W["c_a_log"][i].reshape(1, 2 * SSD_H)
            vf, vb, la = _rowwise(f"ssd_prep_{l}", _f_sprep, [(xa, SSD_INNER, 0), (dtr, 2 * SSD_H, 0)], [],
                                  [(dtb, 2 * SSD_H, 0), (alog, 2 * SSD_H, 0)], [ex0, ex1],
                                  [(SSD_INNER, f32), (SSD_INNER, f32), (2 * SSD_H, f32)], tm=128)
            k_h = _tm2hm(xa[:, SSD_INNER:SSD_INNER + SSD_G * SSD_N], SSD_G)
            q_h = _tm2hm(xa[:, SSD_INNER + SSD_G * SSD_N:], SSD_G)
            a_f = la[:, :SSD_H].reshape(T, SSD_G, SSD_HPG).transpose(1, 0, 2)
            a_b = la[:, SSD_H:].reshape(T, SSD_G, SSD_HPG).transpose(1, 0, 2)
            vf_h, vb_h = _tm2hm(vf, SSD_H), _tm2hm(vb, SSD_H)
            yf, hsf = _scan_fwd(q_h, k_h, vf_h, a_f, rev=False, name=f"ssd_scan_f_{l}")
            yb, hsb = _scan_fwd(q_h, k_h, vb_h, a_b, rev=True, name=f"ssd_scan_b_{l}")
            yf_t, yb_t = _hm2tm(yf), _hm2tm(yb)
            dsk = jnp.repeat(W["c_d_skip"][i], SSD_HD)[None]
            yo = _rowwise(f"ssd_post_{l}", _f_spost, [(yf_t, 512, 0), (yb_t, 512, 0), (xa, 512, 0), (zx, 512, 0)], [],
                          [(dsk, 512, 0), (c_ng[i], 512, 0)], [], [(512, bf16)], tm=256, J=SSD_G)[0]
            mo = _mm_nn(yo, w_c_out[i], name=f"c_out_{l}")
            sv.update(zx=zx, dtr=dtr, xa=xa, dtb=dtb, alog=alog, q_h=q_h, k_h=k_h, a_f=a_f, a_b=a_b, vf_h=vf_h, vb_h=vb_h,
                      hsf=hsf, hsb=hsb, yf_t=yf_t, yb_t=yb_t, dsk=dsk, yo=yo)
        x1, hf = _rowwise(f"norm_mid_{l}", _f_mid, [(xs_, D, 0), (mo, D, 0)], [], [(g2[l], D, 0), (g3[l], D, 0)], [],
                          [(D, f32), (D, bf16)], tm=256)
        pre = _mm_nn(hf, w_up[l], name=f"ffn_up_{l}")
        act = _conv(pre, f_cw8[l], f_cb[l], mode="geglu", W=FFN_CONV, name=f"ffn_conv_{l}", C=2 * FFN, tc=FFN_TC, out_dtype=bf16)
        fo = _mm_nn(act, w_down[l], name=f"ffn_down_{l}")
        sv.update(mo=mo, x1=x1, hf=hf, pre=pre, act=act, fo=fo)
        if l < DEPTH - 1:
            xs_, hn = _rowwise(f"norm_end_{l}", _f_end, [(x1, D, 0), (fo, D, 0)], [], [(g4[l], D, 0), (g1[l + 1], D, 0)], [],
                               [(D, f32), (D, bf16)], tm=256)
        else:
            xs_ = _rowwise(f"norm_end_{l}", _f_last, [(x1, D, 0), (fo, D, 0)], [], [(g4[l], D, 0)], [], [(D, f32)], tm=256)[0]
        saved.append(sv)

    dx, lpart = _loss_call(xs_, tgt)
    loss = lax.psum(lpart[0, 0], ("x", "y", "c"))

    G = {n: [None] * W[n].shape[0] for n in WEIGHTS}
    dhn = None
    for l in reversed(range(DEPTH)):
        i = l // 2
        sv = saved[l]
        if l == DEPTH - 1:
            (dx1, dfo), (dg4,) = _rowwise_bwd(f"norm_end_bwd_{l}", _f_last, [(sv["x1"], D, 0), (sv["fo"], D, 0)], [],
                                              [(g4[l], D, 0)], [], [(dx, D, 0)], [f32, bf16], tm=256)
        else:
            (dx1, dfo), (dg4, dg1n) = _rowwise_bwd(f"norm_end_bwd_{l}", _f_end, [(sv["x1"], D, 0), (sv["fo"], D, 0)], [],
                                                   [(g4[l], D, 0), (g1[l + 1], D, 0)], [], [(dx, D, 0), (dhn, D, 0)],
                                                   [f32, bf16], tm=256)
            G["norm_mix_pre"][l + 1] = dg1n[0]
        G["norm_ffn_post"][l] = dg4[0]
        dact = _mm_nt(dfo, w_down[l], name=f"ffn_down_dx_{l}")
        G["ffn_w_down"][l] = _mm_tn(sv["act"], dfo, name=f"ffn_down_dw_{l}")
        du, dfw, dfb = _conv_bwd_a(sv["pre"], f_cw8[l], f_cb[l], dact, mode="geglu", W=FFN_CONV, name=f"ffn_conv_bwd_{l}",
                                   C=2 * FFN, tc=FFN_TC)
        flip = lambda w8, width: _pad8(w8[:width][::-1])
        dpre = _conv(du, flip(f_cw8[l], FFN_CONV), f_cb[l], mode="none", W=FFN_CONV, name=f"ffn_conv_dx_{l}", C=2 * FFN,
                     tc=FFN_TC, out_dtype=bf16)
        dhf = _mm_nt(dpre, w_up[l], name=f"ffn_up_dx_{l}")
        G["ffn_w_up"][l] = _cols_to_slots([_mm_tn(sv["hf"], dpre, name=f"ffn_up_dw_{l}")], "ffn_w_up", bf16, name=f"slots_ffn_up_{l}")
        G["ffn_conv_w"][l] = _ffn_unperm(dfw[:FFN_CONV])
        G["ffn_conv_b"][l] = _ffn_unperm(dfb[0])
        (dxl, dmo), (dg2, dg3) = _rowwise_bwd(f"norm_mid_bwd_{l}", _f_mid, [(sv["x"], D, 0), (sv["mo"], D, 0)], [],
                                              [(g2[l], D, 0), (g3[l], D, 0)], [], [(dx1, D, 0), (dhf, D, 0)], [f32, bf16], tm=256)
        G["norm_mix_post"][l] = dg2[0]
        G["norm_ffn_pre"][l] = dg3[0]
        if l % 2 == 0:
            dcat = _mm_nt(dmo, w_ab_out[i], name=f"ab_out_dx_{l}")
            G["ab_w_out"][i] = _mm_tn(sv["cat"], dmo, name=f"ab_out_dw_{l}")
            (dyf, _, drg), (dgn,) = _rowwise_bwd(
                f"ret_post_bwd_{l}", _f_rpost, [(sv["yf_t"], RW, 0), (sv["yb_t"], RW, 0), (sv["proj"], RW, 3)], [],
                [(sv["gn"], RW, 0)], [gavg], [(dcat, RW, 0)], [f32, f32, bf16], tm=256)
            G["ab_ret_gn_g"][i] = dgn[0]
            dy_h = _tm2hm(dyf, RH)
            dqf, dkf, dvf, daf = _scan_bwd(sv["q_h"], sv["k_h"], sv["v_h"], sv["a_f"], sv["hsf"], dy_h, rev=False,
                                           name=f"ret_scan_f_bwd_{l}")
            dqb, dkb, dvb, dab = _scan_bwd(sv["q_h"], sv["k_h"], sv["v_h"], sv["a_b"], sv["hsb"], dy_h, rev=True,
                                           name=f"ret_scan_b_bwd_{l}")
            dq_t, dk_t, drv = _hm2tm(dqf + dqb), _hm2tm(dkf + dkb), _hm2tm(dvf + dvb).astype(bf16)
            (drq, drk), _ = _rowwise_bwd(f"ret_prep_bwd_{l}", _f_rprep, [(sv["proj"], RW, 0), (sv["proj"], RW, 1)],
                                         [(cos_t, RW, 0), (sin_t, RW, 0)], [], [], [(dq_t, RW, 0), (dk_t, RW, 0)], [bf16, bf16], tm=256)
            da_cols = jnp.concatenate([_hm2tm(daf), _hm2tm(dab)], axis=1)
            dlg = _colsum(da_cols, name=f"ret_decay_sum_{l}").reshape(2, RH, 8)[:, :, 0]
            G["ab_ret_decay_logit"][i] = sv["lg_vjp"](dlg)[0]
            dna = _tm2hm(dcat[:, RW:], NAH)
            dnq, dnk, dnv, dbias = _na_bwd(sv["nq"], sv["nk"], sv["nv"], sv["bias"], sv["na_o"], sv["na_l"], dna, name=f"na_bwd_{l}")
            G["ab_na_rpb"][i] = sv["bias_vjp"](dbias)[0]
            dproj = jnp.concatenate([drq, drk, drv, drg] + [_hm2tm(t).astype(bf16) for t in (dnq, dnk, dnv)], axis=1)
            dhn = _mm_nt(dproj, w_ab_in[i], name=f"ab_in_dx_{l}")
            G["ab_w_in"][i] = _cols_to_slots([_mm_tn(sv["hn"], dproj, name=f"ab_in_dw_{l}")], "ab_w_in", bf16, name=f"slots_ab_in_{l}")
        else:
            dyo = _mm_nt(dmo, w_c_out[i], name=f"c_out_dx_{l}")
            G["c_w_out"][i] = _mm_tn(sv["yo"], dmo, name=f"c_out_dw_{l}")
            (dyf, _, dxs1, dz), (ddsk, dng) = _rowwise_bwd(
                f"ssd_post_bwd_{l}", _f_spost, [(sv["yf_t"], 512, 0), (sv["yb_t"], 512, 0), (sv["xa"], 512, 0), (sv["zx"], 512, 0)],
                [], [(sv["dsk"], 512, 0), (c_ng[i], 512, 0)], [], [(dyo, 512, 0)], [f32, f32, f32, bf16], tm=256, J=SSD_G)
            G["c_d_skip"][i] = ddsk.reshape(SSD_H, SSD_HD).sum(axis=1)
            G["c_norm_g"][i] = dng[0]
            dy_h = _tm2hm(dyf, SSD_H)
            dqf, dkf, dvf, daf = _scan_bwd(sv["q_h"], sv["k_h"], sv["vf_h"], sv["a_f"], sv["hsf"], dy_h, rev=False,
                                           name=f"ssd_scan_f_bwd_{l}")
            dqb, dkb, dvb, dab = _scan_bwd(sv["q_h"], sv["k_h"], sv["vb_h"], sv["a_b"], sv["hsb"], dy_h, rev=True,
                                           name=f"ssd_scan_b_bwd_{l}")
            dla = jnp.concatenate([daf.transpose(1, 0, 2).reshape(T, SSD_H), dab.transpose(1, 0, 2).reshape(T, SSD_H)], axis=1)
            (dxs2, ddtr), (ddtb, dalog) = _rowwise_bwd(
                f"ssd_prep_bwd_{l}", _f_sprep, [(sv["xa"], SSD_INNER, 0), (sv["dtr"], 2 * SSD_H, 0)], [],
                [(sv["dtb"], 2 * SSD_H, 0), (sv["alog"], 2 * SSD_H, 0)], [ex0, ex1],
                [(_hm2tm(dvf), SSD_INNER, 0), (_hm2tm(dvb), SSD_INNER, 0), (dla, 2 * SSD_H, 0)], [f32, bf16], tm=128)
            G["c_dt_bias"][i] = ddtb.reshape(2, SSD_H)
            G["c_a_log"][i] = dalog.reshape(2, SSD_H)
            dxa = jnp.concatenate([dxs1 + dxs2, _hm2tm(dkf + dkb), _hm2tm(dqf + dqb)], axis=1)
            du, dcw, dcb = _conv_bwd_a(sv["zx"], c_cw8[i], c_cb[i], dxa, mode="silu", W=SSD_CONV, name=f"c_conv_bwd_{l}",
                                       C=SSD_XBC, xbase=SSD_INNER // 512)
            G["c_conv_w"][i] = dcw[:SSD_CONV]
            G["c_conv_b"][i] = dcb[0]
            dxbc = _conv(du, _pad8(c_cw8[i][:SSD_CONV][::-1]), c_cb[i], mode="none", W=SSD_CONV, name=f"c_conv_dx_{l}",
                         C=SSD_XBC, out_dtype=bf16)
            dzx = jnp.concatenate([dz, dxbc], axis=1)
            t1 = _mm_nt(ddtr, w_dt[i], name=f"c_in_dt_dx_{l}")
            dhn = _mm_nt(dzx, w_zx[i], add=t1, name=f"c_in_dx_{l}")
            G["c_w_in"][i] = _cols_to_slots([_mm_tn(sv["hn"], dzx, name=f"c_in_dw_{l}"), _mm_tn(sv["hn"], ddtr, name=f"c_in_dt_dw_{l}")],
                                            "c_w_in", bf16, name=f"slots_c_in_{l}")
        dx = dxl
    (grad_x,), (dg1,) = _rowwise_bwd("norm_first_bwd", _f_first_bwd, [(x, D, 0)], [], [(g1[0], D, 0)], [], [(dx, D, 0), (dhn, D, 0)],
                                     [f32], tm=256)
    G["norm_mix_pre"][0] = dg1[0]

    small_names = [n for n, _ in SHARDED[N_BIG:]]
    col_slots = [jnp.concatenate(G[n], axis=1) for n in COL_SHARDED]
    row_slots = jnp.concatenate([g.reshape(NDEV, -1, D).astype(bf16) for n in ROW_SHARDED for g in G[n]], axis=1)
    small_slots = _pack_slots([_to_slots(jnp.stack(G[n]), ax) for n, ax in SHARDED[N_BIG:]], 8)
    ar = _pack([jnp.stack(G[n]) for n in REPLICATED], f32, 8)
    exch = _exchange([(a, True) for a in col_slots + [row_slots, small_slots]] + [(ar, False)], name="exchange_grads")
    pk = lambda d, names: _pack([d[n] for n in names], f32, 8)
    upd = [_adamw(exch[j], col(W, n, f32), col(Mo, n, f32), col(Vo, n, f32), name=f"adamw_{n}", tr=256)
           for j, n in enumerate(COL_SHARDED)]
    upd_rows = _adamw(exch[3], rows_of(W, f32), rows_of(Mo, f32), rows_of(Vo, f32), name="adamw_rows", tr=64)
    upd_small = _adamw(exch[4], pk(W, small_names), pk(Mo, small_names), pk(Vo, small_names), name="adamw_small",
                       tr=small_slots.shape[1])
    upd_rep = _adamw(exch[5], pk(W, REPLICATED), pk(Mo, REPLICATED), pk(Vo, REPLICATED), name="adamw_replicated", tr=ar.shape[0])
    res = []
    for k in range(4):
        d = {n: upd[j][k].reshape(W[n].shape) for j, n in enumerate(COL_SHARDED)}
        off = 0
        for n in ROW_SHARDED:
            cnt = W[n].shape[0] * W[n].shape[1]
            d[n] = upd_rows[k][off:off + cnt].reshape(W[n].shape)
            off += cnt
        d.update(zip(small_names, _unpack(upd_small[k], [W[n].shape for n in small_names])))
        d.update(zip(REPLICATED, _unpack(upd_rep[k], [W[n].shape for n in REPLICATED])))
        res.append(d)
    outs = [loss, grad_x[None]]
    for k in range(4):
        outs += [res[k][n] for n in WEIGHTS]
    return tuple(outs)


def _pack_slots(slot_arrays, row_mult):
    flat = jnp.concatenate([a.reshape(NDEV, -1) for a in slot_arrays], axis=1)
    rows = -(-flat.shape[1] // LANES)
    rows = -(-rows // row_mult) * row_mult
    return jnp.pad(flat, ((0, 0), (0, rows * LANES - flat.shape[1]))).reshape(NDEV, rows, LANES)
```

```python
import functools
import numpy as np
import jax
import jax.numpy as jnp
from jax import lax
from jax.experimental import pallas as pl
from jax.experimental.pallas import tpu as pltpu

f32, bf16 = jnp.float32, jnp.bfloat16
S = jax.ShapeDtypeStruct
HI = lax.Precision.HIGHEST

D = 1024
DEPTH = 4
GRID_W = 64
CHUNK = 128
EPS = 1e-6
RH, RDH, RW = 8, 64, 512
NAH, NADH, NAW = 8, 64, 512
NA_WR, NA_WC = 8, 16
NA_QROWS = 8
NA_KROWS = 16
SSD_INNER, SSD_HD, SSD_H, SSD_G, SSD_HPG, SSD_N, SSD_CONV = 2048, 64, 32, 4, 8, 128, 5
SSD_XBC = SSD_INNER + 2 * SSD_G * SSD_N
FFN, FFN_CONV = 2816, 3
FFN_TC = 512
SCAN_HEADS_PER_STEP = 8
ROPE_BASE = 10000.0
LR, B1, B2, AEPS, WD, STEP = 0.001, 0.9, 0.999, 1e-08, 0.01, 10
NDEV = 8
LANES = 128
VMEM_LIMIT = 56 * 1024 * 1024
MM_BLOCK_BYTES = 6 * 1024 * 1024

NT = (((1,), (1,)), ((), ()))
TN = (((0,), (0,)), ((), ()))

SHARDED = [("ab_w_in", 2), ("ab_w_out", 1), ("c_w_in", 2), ("c_w_out", 1), ("ffn_w_up", 2), ("ffn_w_down", 1),
           ("c_conv_w", 2), ("c_conv_b", 1), ("c_norm_g", 1), ("ffn_conv_w", 2)]
N_BIG = 6
COL_SHARDED = ["ab_w_in", "c_w_in", "ffn_w_up"]
ROW_SHARDED = ["ab_w_out", "c_w_out", "ffn_w_down"]
REPLICATED = ["norm_mix_pre", "norm_mix_post", "norm_ffn_pre", "norm_ffn_post", "ab_ret_decay_logit", "ab_ret_gn_g",
              "ab_na_rpb", "c_dt_bias", "c_a_log", "c_d_skip", "ffn_conv_b"]
WEIGHTS = ["norm_mix_pre", "norm_mix_post", "norm_ffn_pre", "norm_ffn_post", "ab_w_in", "ab_ret_decay_logit",
           "ab_ret_gn_g", "ab_na_rpb", "ab_w_out", "c_w_in", "c_conv_w", "c_conv_b", "c_dt_bias", "c_a_log", "c_d_skip",
           "c_norm_g", "c_w_out", "ffn_w_up", "ffn_conv_w", "ffn_conv_b", "ffn_w_down"]


def _params(sem=None):
    return pltpu.CompilerParams(dimension_semantics=sem, vmem_limit_bytes=VMEM_LIMIT)


def _mm_nn(a, w, *, name, tm=1024, tn=512, out_dtype=f32):
    M, K = a.shape
    N = w.shape[1]
    tn = min(tn, N)

    def body(a_ref, w_ref, o_ref):
        o_ref[...] = jnp.dot(a_ref[...], w_ref[...], preferred_element_type=f32).astype(o_ref.dtype)

    return pl.pallas_call(
        body, name=name, grid=(M // tm, N // tn),
        in_specs=[pl.BlockSpec((tm, K), lambda i, j: (i, 0)), pl.BlockSpec((K, tn), lambda i, j: (0, j))],
        out_specs=pl.BlockSpec((tm, tn), lambda i, j: (i, j)),
        out_shape=S((M, N), out_dtype), compiler_params=_params(("parallel", "parallel")))(a, w)


def _mm_nt(dy, w, *, name, add=None, tm=512):
    M, N = dy.shape
    K = w.shape[0]
    tk = next((t for t in (1024, 1408, 512, 256, 128) if K % t == 0 and (t <= 512 or t * N * 2 <= MM_BLOCK_BYTES)), K)

    def body(*refs):
        if add is None:
            d_ref, w_ref, o_ref = refs
            o_ref[...] = lax.dot_general(d_ref[...], w_ref[...], NT, preferred_element_type=f32)
        else:
            d_ref, w_ref, a_ref, o_ref = refs
            o_ref[...] = lax.dot_general(d_ref[...], w_ref[...], NT, preferred_element_type=f32) + a_ref[...]

    in_specs = [pl.BlockSpec((tm, N), lambda i, j: (i, 0)), pl.BlockSpec((tk, N), lambda i, j: (j, 0))]
    args = [dy, w]
    if add is not None:
        in_specs.append(pl.BlockSpec((tm, tk), lambda i, j: (i, j)))
        args.append(add)
    return pl.pallas_call(
        body, name=name, grid=(M // tm, K // tk), in_specs=in_specs,
        out_specs=pl.BlockSpec((tm, tk), lambda i, j: (i, j)),
        out_shape=S((M, K), f32), compiler_params=_params(("parallel", "parallel")))(*args)


def _mm_tn(a, dy, *, name, tt=1024):
    M, K = a.shape
    N = dy.shape[1]
    tk = K if K <= 1024 else (1024 if K % 1024 == 0 else K // 2)
    tn = min(512, N)
    tt = min(tt, M)

    def body(a_ref, d_ref, o_ref):
        t = pl.program_id(2)
        part = lax.dot_general(a_ref[...], d_ref[...], TN, preferred_element_type=f32)

        @pl.when(t == 0)
        def _():
            o_ref[...] = part

        @pl.when(t > 0)
        def _():
            o_ref[...] += part

    return pl.pallas_call(
        body, name=name, grid=(K // tk, N // tn, M // tt),
        in_specs=[pl.BlockSpec((tt, tk), lambda k, n, t: (t, k)), pl.BlockSpec((tt, tn), lambda k, n, t: (t, n))],
        out_specs=pl.BlockSpec((tk, tn), lambda k, n, t: (k, n)),
        out_shape=S((K, N), f32), compiler_params=_params(("parallel", "parallel", "arbitrary")))(a, dy)


def _tile_spec(tm, width, base):
    return pl.BlockSpec((tm, width), lambda j, i: (i, base + j))


def _par_spec(width, base):
    return pl.BlockSpec((1, width), lambda j, i: (0, base + j))


def _full_spec(a):
    nd = a.ndim
    return pl.BlockSpec(a.shape, lambda j, i: (0,) * nd)


def _rowwise(name, f, tiles, ctiles, params, consts, outs, *, tm, J=1):
    T = tiles[0][0].shape[0]
    nt, nct, npar, nc = len(tiles), len(ctiles), len(params), len(consts)

    def body(*refs):
        tv = [r[...].astype(f32) for r in refs[:nt + nct]]
        pv = [r[...] for r in refs[nt + nct:nt + nct + npar + nc]]
        res = f(*tv, *pv)
        for o, v in zip(refs[nt + nct + npar + nc:], res):
            o[...] = v.astype(o.dtype)

    in_specs = ([_tile_spec(tm, w, b) for _, w, b in tiles + ctiles] + [_par_spec(w, b) for _, w, b in params]
                + [_full_spec(c) for c in consts])
    return pl.pallas_call(
        body, name=name, grid=(J, T // tm), in_specs=in_specs,
        out_specs=[_tile_spec(tm, w, 0) for w, _ in outs],
        out_shape=[S((T, J * w), dt) for w, dt in outs],
        compiler_params=_params(("parallel", "parallel")))(
            *[a for a, _, _ in tiles + ctiles], *[a for a, _, _ in params], *consts)


def _rowwise_bwd(name, f, tiles, ctiles, params, consts, douts, dtile_dtypes, *, tm, J=1):
    T = tiles[0][0].shape[0]
    nt, nct, npar, nc, nd = len(tiles), len(ctiles), len(params), len(consts), len(douts)

    def body(*refs):
        i = pl.program_id(1)
        k = 0
        tv = [r[...].astype(f32) for r in refs[k:k + nt]]; k += nt
        cv = [r[...].astype(f32) for r in refs[k:k + nct]]; k += nct
        pv = [r[...] for r in refs[k:k + npar]]; k += npar
        kv = [r[...] for r in refs[k:k + nc]]; k += nc
        dv = [r[...].astype(f32) for r in refs[k:k + nd]]; k += nd
        dt_refs = refs[k:k + nt]; k += nt
        dp_refs = refs[k:k + npar]
        _, vjp = jax.vjp(lambda tv_, pv_: tuple(f(*tv_, *cv, *pv_, *kv)), tv, pv)
        dts, dps = vjp(tuple(dv))
        for r, g in zip(dt_refs, dts):
            r[...] = g.astype(r.dtype)
        for r, g in zip(dp_refs, dps):
            @pl.when(i == 0)
            def _(r=r, g=g):
                r[...] = g

            @pl.when(i > 0)
            def _(r=r, g=g):
                r[...] += g

    in_specs = ([_tile_spec(tm, w, b) for _, w, b in tiles + ctiles] + [_par_spec(w, b) for _, w, b in params]
                + [_full_spec(c) for c in consts] + [_tile_spec(tm, w, b) for _, w, b in douts])
    res = pl.pallas_call(
        body, name=name, grid=(J, T // tm), in_specs=in_specs,
        out_specs=[_tile_spec(tm, w, 0) for _, w, _ in tiles] + [_par_spec(w, b) for _, w, b in params],
        out_shape=[S((T, J * w), dt) for (_, w, _), dt in zip(tiles, dtile_dtypes)] + [S(a.shape, f32) for a, _, _ in params],
        compiler_params=_params(("parallel", "arbitrary")))(
            *[a for a, _, _ in tiles + ctiles], *[a for a, _, _ in params], *consts, *[a for a, _, _ in douts])
    return res[:nt], res[nt:]


def _rms(x, g):
    return x * lax.rsqrt(jnp.mean(x * x, axis=-1, keepdims=True) + EPS) * g


def _f_first(x, g1):
    return (_rms(x, g1),)


def _f_first_bwd(x, g1):
    return (x, _rms(x, g1))


def _f_mid(x, m, g2, g3):
    x1 = x + _rms(m, g2)
    return (x1, _rms(x1, g3))


def _f_end(x1, fo, g4, g1n):
    x2 = x1 + _rms(fo, g4)
    return (x2, _rms(x2, g1n))


def _f_last(x1, fo, g4):
    return (x1 + _rms(fo, g4),)


@jax.custom_vjp
def _swap_halves(x):
    c = x.shape[1]
    lane = lax.broadcasted_iota(jnp.int32, x.shape, 1) % RDH
    return jnp.where(lane < RDH // 2, pltpu.roll(x, c - RDH // 2, axis=1), pltpu.roll(x, RDH // 2, axis=1))


_swap_halves.defvjp(lambda x: (_swap_halves(x), None), lambda _, g: (_swap_halves(g),))


def _f_rprep(rq, rk, cos, sin):
    rot = lambda t: t * cos + _swap_halves(t) * sin
    return (rot(rq), rot(rk) * (RDH ** -0.5))


def _f_rpost(yf, yb, rg, gn, gavg):
    y = yf + yb
    mu = jnp.dot(y, gavg, precision=HI, preferred_element_type=f32)
    yc = y - mu
    var = jnp.dot(yc * yc, gavg, precision=HI, preferred_element_type=f32)
    return (jax.nn.silu(rg) * (yc * lax.rsqrt(var + EPS) * gn),)


def _f_sprep(xs, dtr, dtb, alog, ex0, ex1):
    dt = jax.nn.softplus(dtr + dtb)
    la = dt * (-jnp.exp(alog))
    e0 = jnp.dot(dt, ex0, precision=HI, preferred_element_type=f32)
    e1 = jnp.dot(dt, ex1, precision=HI, preferred_element_type=f32)
    return (xs * e0, xs * e1, la)


def _f_spost(yf, yb, xs, z, dsk, ng):
    y = (yf + yb + xs * dsk) * jax.nn.silu(z)
    y = y * lax.rsqrt(jnp.mean(y * y, axis=-1, keepdims=True) + EPS)
    return (y * ng,)


def _loss_call(y, tgt, *, tm=256):
    T = y.shape[0]

    def body(y_ref, t_ref, dy_ref, l_ref):
        i = pl.program_id(0)
        e = y_ref[...] - t_ref[...]
        dy_ref[...] = e * (1.0 / D)
        part = jnp.zeros((8, LANES), f32) + 0.5 * jnp.sum(jnp.mean(e * e, axis=-1, keepdims=True))

        @pl.when(i == 0)
        def _():
            l_ref[...] = part

        @pl.when(i > 0)
        def _():
            l_ref[...] += part

    return pl.pallas_call(
        body, name="loss_head", grid=(T // tm,),
        in_specs=[pl.BlockSpec((tm, D), lambda i: (i, 0))] * 2,
        out_specs=[pl.BlockSpec((tm, D), lambda i: (i, 0)), pl.BlockSpec((8, LANES), lambda i: (0, 0))],
        out_shape=[S((T, D), f32), S((8, LANES), f32)], compiler_params=_params(("arbitrary",)))(y, tgt)


def _colsum(x, *, name, tm=512):
    T, C = x.shape

    def body(x_ref, o_ref):
        i = pl.program_id(0)
        part = jnp.sum(x_ref[...], axis=0, keepdims=True)

        @pl.when(i == 0)
        def _():
            o_ref[...] = part

        @pl.when(i > 0)
        def _():
            o_ref[...] += part

    return pl.pallas_call(
        body, name=name, grid=(T // tm,), in_specs=[pl.BlockSpec((tm, C), lambda i: (i, 0))],
        out_specs=pl.BlockSpec((1, C), lambda i: (0, 0)), out_shape=S((1, C), f32),
        compiler_params=_params(("arbitrary",)))(x)


def _nn(a, b):
    return jnp.dot(a, b, preferred_element_type=f32)


def _nt(a, b):
    return lax.dot_general(a, b, NT, preferred_element_type=f32)


@jax.custom_vjp
def _mm_lt(a, a_t, b):
    return _nn(a_t, b)


_mm_lt.defvjp(lambda a, a_t, b: (_nn(a_t, b), (a, b)),
              lambda res, g: (jnp.zeros_like(res[0]), _nt(g, res[1]), _nn(res[0], g)))


@jax.custom_vjp
def _mm_rt(a, a_t, b):
    return _nn(a, b)


_mm_rt.defvjp(lambda a, a_t, b: (_nn(a, b), (a_t, b)),
              lambda res, g: (_nt(g, res[1]), jnp.zeros_like(res[0]), _nn(res[0], g)))


@jax.custom_vjp
def _masked_mm(s, s_t, d, d_t, v):
    return _nn(s * d, v)


def _masked_mm_bwd(res, g):
    s, s_t, d, d_t, v = res
    da = _nt(g, v)
    return (da * d, jnp.zeros_like(s_t), da * s, jnp.zeros_like(d_t), _nn(s_t * d_t, g))


_masked_mm.defvjp(lambda s, s_t, d, d_t, v: (_nn(s * d, v), (s, s_t, d, d_t, v)), _masked_mm_bwd)


def _scan_step(h, q, k, v, a, rev, for_vjp=False):
    L = q.shape[0]
    Hg = v.shape[0]
    ii = lax.broadcasted_iota(jnp.int32, (L, L), 0)
    jj = lax.broadcasted_iota(jnp.int32, (L, L), 1)
    if rev:
        tri, tri_t, dmask, dmask_t = (jj >= ii), (ii >= jj), (jj > ii), (ii > jj)
    else:
        tri, tri_t, dmask, dmask_t = (jj <= ii), (ii <= jj), (jj <= ii), (ii <= jj)
    cs = jnp.dot(tri.astype(f32), a, precision=HI, preferred_element_type=f32)
    cs_t = lax.dot_general(a, tri_t.astype(f32), TN, precision=HI, preferred_element_type=f32)
    tot = jnp.sum(a, axis=0, keepdims=True)
    qk = _nt(q, k)
    k_t = k.T
    if for_vjp:
        q_t = lax.stop_gradient(q.T)
        qk_t = lax.stop_gradient(_nt(k, q))
    hs, ys = [], []
    for hh in range(Hg):
        c_col, c_row = cs[:, hh:hh + 1], cs_t[hh:hh + 1, :]
        dec = jnp.exp(jnp.where(dmask, c_col - c_row, -1e30))
        t_all = tot[:, hh:hh + 1]
        w = v[hh] * jnp.exp(t_all - c_col)
        if for_vjp:
            dec_t = lax.stop_gradient(jnp.exp(jnp.where(dmask_t, c_row - c_col, -1e30)))
            y = _masked_mm(qk, qk_t, dec, dec_t, v[hh]) + _mm_rt(q, q_t, h[hh]) * jnp.exp(c_col)
            hn = h[hh] * jnp.exp(t_all) + _mm_lt(lax.stop_gradient(k), k_t, w)
        else:
            y = _nn(qk * dec, v[hh]) + _nn(q, h[hh]) * jnp.exp(c_col)
            hn = h[hh] * jnp.exp(t_all) + _nn(k_t, w)
        hs.append(hn)
        ys.append(y)
    return jnp.stack(hs), jnp.stack(ys)


def _scan_specs(gb, N, Hg, P, Ha, cm):
    qs = pl.BlockSpec((gb, CHUNK, N), lambda g, c: (g, cm(c), 0))
    vs = pl.BlockSpec((gb * Hg, CHUNK, P), lambda g, c: (g, cm(c), 0))
    as_ = pl.BlockSpec((gb, CHUNK, Ha), lambda g, c: (g, cm(c), 0))
    hs = pl.BlockSpec((gb, 1, Hg, N, P), lambda g, c: (g, cm(c), 0, 0, 0))
    return qs, vs, as_, hs


def _scan_fwd(q, k, v, a, *, rev, name):
    G, T, N = q.shape
    Ht, _, P = v.shape
    Hg, Ha, NC = Ht // G, a.shape[2], T // CHUNK
    gb = SCAN_HEADS_PER_STEP // Hg
    cm = (lambda c: NC - 1 - c) if rev else (lambda c: c)
    qs, vs, as_, hs = _scan_specs(gb, N, Hg, P, Ha, cm)

    def body(q_ref, k_ref, v_ref, a_ref, y_ref, hs_ref, h_scr):
        @pl.when(pl.program_id(1) == 0)
        def _():
            h_scr[...] = jnp.zeros_like(h_scr)

        for j in range(gb):
            h = h_scr[j]
            hs_ref[j, 0] = h
            hn, y = _scan_step(h, q_ref[j], k_ref[j], v_ref[j * Hg:(j + 1) * Hg], a_ref[j], rev)
            y_ref[j * Hg:(j + 1) * Hg] = y
            h_scr[j] = hn

    return pl.pallas_call(
        body, name=name, grid=(G // gb, NC), in_specs=[qs, qs, vs, as_], out_specs=[vs, hs],
        out_shape=[S((Ht, T, P), f32), S((G, NC, Hg, N, P), f32)],
        scratch_shapes=[pltpu.VMEM((gb, Hg, N, P), f32)],
        compiler_params=_params(("parallel", "arbitrary")))(q, k, v, a)


def _scan_bwd(q, k, v, a, hsave, dy, *, rev, name):
    G, T, N = q.shape
    Ht, _, P = v.shape
    Hg, Ha, NC = Ht // G, a.shape[2], T // CHUNK
    gb = SCAN_HEADS_PER_STEP // Hg
    cm = (lambda c: c) if rev else (lambda c: NC - 1 - c)
    qs, vs, as_, hs = _scan_specs(gb, N, Hg, P, Ha, cm)

    def body(q_ref, k_ref, v_ref, a_ref, hs_ref, dy_ref, dq_ref, dk_ref, dv_ref, da_ref, dh_scr):
        @pl.when(pl.program_id(1) == 0)
        def _():
            dh_scr[...] = jnp.zeros_like(dh_scr)

        for j in range(gb):
            heads = slice(j * Hg, (j + 1) * Hg)
            _, vjp = jax.vjp(functools.partial(_scan_step, rev=rev, for_vjp=True), hs_ref[j, 0], q_ref[j], k_ref[j], v_ref[heads],
                             a_ref[j])
            dh, dq, dk, dv, da = vjp((dh_scr[j], dy_ref[heads]))
            dq_ref[j] = dq
            dk_ref[j] = dk
            dv_ref[heads] = dv
            da_ref[j] = da
            dh_scr[j] = dh

    return pl.pallas_call(
        body, name=name, grid=(G // gb, NC), in_specs=[qs, qs, vs, as_, hs, vs], out_specs=[qs, qs, vs, as_],
        out_shape=[S((G, T, N), f32), S((G, T, N), f32), S((Ht, T, P), f32), S((G, T, Ha), f32)],
        scratch_shapes=[pltpu.VMEM((gb, Hg, N, P), f32)],
        compiler_params=_params(("parallel", "arbitrary")))(q, k, v, a, hsave, dy)


def _na_block_case(rb, nrb):
    return jnp.where(rb == 0, 0, jnp.where(rb == nrb - 1, 2, 1))


def _na_key_start(rb, rows):
    return pl.multiple_of(jnp.clip(rb * NA_QROWS - NA_WR // 2, 0, rows - NA_KROWS) * GRID_W, 256)


def _na_fwd(q, k, v, bias, *, name):
    H, T, P = q.shape
    rows = T // GRID_W
    nq, nk = NA_QROWS * GRID_W, NA_KROWS * GRID_W
    nrb = T // nq
    scale = NADH ** -0.5

    def body(q_ref, k_ref, v_ref, b_ref, o_ref, l_ref):
        ks = _na_key_start(pl.program_id(1), rows)
        kw = k_ref[0, pl.ds(ks, nk), :]
        vw = v_ref[0, pl.ds(ks, nk), :]
        s = lax.dot_general(q_ref[0], kw, NT, preferred_element_type=f32) * scale + b_ref[0, 0]
        m = jnp.max(s, axis=1, keepdims=True)
        p = jnp.exp(s - m)
        l = jnp.sum(p, axis=1, keepdims=True)
        o_ref[0] = jnp.dot(p.astype(bf16), vw, preferred_element_type=f32) / l
        l_ref[0] = m + jnp.log(l)

    return pl.pallas_call(
        body, name=name, grid=(H, nrb),
        in_specs=[pl.BlockSpec((1, nq, P), lambda h, r: (h, r, 0)),
                  pl.BlockSpec((1, T, P), lambda h, r: (h, 0, 0)), pl.BlockSpec((1, T, P), lambda h, r: (h, 0, 0)),
                  pl.BlockSpec((1, 1, nq, nk), lambda h, r: (h, _na_block_case(r, nrb), 0, 0))],
        out_specs=[pl.BlockSpec((1, nq, P), lambda h, r: (h, r, 0)), pl.BlockSpec((1, nq, 1), lambda h, r: (h, r, 0))],
        out_shape=[S((H, T, P), f32), S((H, T, 1), f32)],
        compiler_params=_params(("parallel", "arbitrary")))(q, k, v, bias)


def _na_bwd(q, k, v, bias, o, lse, do, *, name):
    H, T, P = q.shape
    rows = T // GRID_W
    nq, nk = NA_QROWS * GRID_W, NA_KROWS * GRID_W
    nrb = T // nq
    scale = NADH ** -0.5

    def body(q_ref, k_ref, v_ref, b_ref, o_ref, l_ref, do_ref, dq_ref, dk_ref, dv_ref, db_ref):
        rb = pl.program_id(1)

        @pl.when(rb == 0)
        def _():
            dk_ref[...] = jnp.zeros_like(dk_ref)
            dv_ref[...] = jnp.zeros_like(dv_ref)

        ks = _na_key_start(rb, rows)
        qv = q_ref[0]
        kw = k_ref[0, pl.ds(ks, nk), :]
        vw = v_ref[0, pl.ds(ks, nk), :]
        s = lax.dot_general(qv, kw, NT, preferred_element_type=f32) * scale + b_ref[0, 0]
        p = jnp.exp(s - l_ref[0])
        do_ = do_ref[0]
        dob = do_.astype(bf16)
        dp = lax.dot_general(dob, vw, NT, preferred_element_type=f32)
        ds = p * (dp - jnp.sum(do_ * o_ref[0], axis=1, keepdims=True))
        dsb = ds.astype(bf16)
        dq_ref[0] = jnp.dot(dsb, kw, preferred_element_type=f32) * scale
        dk_ref[0, pl.ds(ks, nk), :] += lax.dot_general(dsb, qv, TN, preferred_element_type=f32) * scale
        dv_ref[0, pl.ds(ks, nk), :] += lax.dot_general(p.astype(bf16), dob, TN, preferred_element_type=f32)
        first = (rb == 0) | (rb == 1) | (rb == nrb - 1)

        @pl.when(first)
        def _():
            db_ref[0, 0] = ds

        @pl.when(jnp.logical_not(first))
        def _():
            db_ref[0, 0] += ds

    qspec = pl.BlockSpec((1, nq, P), lambda h, r: (h, r, 0))
    fspec = pl.BlockSpec((1, T, P), lambda h, r: (h, 0, 0))
    bspec = pl.BlockSpec((1, 1, nq, nk), lambda h, r: (h, _na_block_case(r, nrb), 0, 0))
    return pl.pallas_call(
        body, name=name, grid=(H, nrb),
        in_specs=[qspec, fspec, fspec, bspec, qspec, pl.BlockSpec((1, nq, 1), lambda h, r: (h, r, 0)), qspec],
        out_specs=[qspec, fspec, fspec, bspec],
        out_shape=[S((H, T, P), f32), S((H, T, P), f32), S((H, T, P), f32), S(bias.shape, f32)],
        compiler_params=_params(("parallel", "arbitrary")))(q, k, v, bias, o, lse, do)


def _na_bias_tables(rows):
    c = np.arange(GRID_W)[:, None]
    kc = np.arange(GRID_W)[None, :]
    cstart = np.clip(c - NA_WC // 2, 0, GRID_W - NA_WC)
    valid_c = (kc >= cstart) & (kc < cstart + NA_WC)
    dc = kc - c + NA_WC - 1
    E = (valid_c[:, :, None] & (dc[:, :, None] == np.arange(2 * NA_WC - 1)[None, None, :])).astype(np.float32)
    A = np.zeros((3, NA_QROWS, NA_KROWS, 2 * NA_WR - 1), np.float32)
    valid_r = np.zeros((3, NA_QROWS, NA_KROWS), bool)
    for z, r0 in enumerate((0, NA_QROWS, rows - NA_QROWS)):
        ks = int(np.clip(r0 - NA_WR // 2, 0, rows - NA_KROWS))
        for ri in range(NA_QROWS):
            r = r0 + ri
            rs = int(np.clip(r - NA_WR // 2, 0, rows - NA_WR))
            for kri in range(NA_KROWS):
                kr = ks + kri
                if rs <= kr < rs + NA_WR:
                    valid_r[z, ri, kri] = True
                    A[z, ri, kri, kr - r + NA_WR - 1] = 1.0
    mask = np.where(valid_r[:, :, None, :, None] & valid_c[None, None, :, None, :], 0.0, -1e30).astype(np.float32)
    return E, A, mask


def _na_bias(rpb, rows):
    E, A, mask = _na_bias_tables(rows)
    r1 = jnp.einsum("hde,cke->hdck", rpb, E, precision=HI)
    b = jnp.einsum("hdck,zabd->hzacbk", r1, A, precision=HI) + mask[None]
    return b.reshape(rpb.shape[0], 3, NA_QROWS * GRID_W, NA_KROWS * GRID_W)


def _conv_shifts(prev, cur, nxt, i, n_i, W):
    tm = cur.shape[0]
    prev = jnp.where(i > 0, prev, 0.0)
    nxt = jnp.where(i < n_i - 1, nxt, 0.0)
    ext = jnp.concatenate([prev, cur, nxt], axis=0)
    out = []
    for w in range(W):
        s = (W // 2 - w) % (tm + 16)
        out.append((ext if s == 0 else pltpu.roll(ext, s, axis=0))[8:8 + tm])
    return out


def _conv_act(u, mode):
    if mode == "silu":
        return jax.nn.silu(u)
    if mode == "geglu":
        half = u.shape[1] // 2
        return jax.nn.gelu(u[:, :half], approximate=True) * u[:, half:]
    return u


def _conv_specs(T, tm, tc, xbase):
    r8 = tm // 8
    last = T // 8 - 1
    cur = pl.BlockSpec((tm, tc), lambda j, i: (i, xbase + j))
    prev = pl.BlockSpec((8, tc), lambda j, i: (jnp.maximum(i * r8 - 1, 0), xbase + j))
    nxt = pl.BlockSpec((8, tc), lambda j, i: (jnp.minimum((i + 1) * r8, last), xbase + j))
    return cur, prev, nxt


def _conv(x, w8, b, *, mode, W, name, C, xbase=0, tm=512, tc=512, out_dtype=f32):
    T = x.shape[0]
    NI, J = T // tm, C // tc
    tco = tc // 2 if mode == "geglu" else tc
    cur, prev, nxt = _conv_specs(T, tm, tc, xbase)

    def body(xc, xp, xn, w_ref, b_ref, o_ref):
        sh = _conv_shifts(xp[...].astype(f32), xc[...].astype(f32), xn[...].astype(f32), pl.program_id(1), NI, W)
        wv = w_ref[...]
        u = sh[0] * wv[0:1, :]
        for w in range(1, W):
            u = u + sh[w] * wv[w:w + 1, :]
        if mode != "none":
            u = u + b_ref[...]
        o_ref[...] = _conv_act(u, mode).astype(o_ref.dtype)

    return pl.pallas_call(
        body, name=name, grid=(J, NI),
        in_specs=[cur, prev, nxt, pl.BlockSpec((8, tc), lambda j, i: (0, j)), pl.BlockSpec((1, tc), lambda j, i: (0, j))],
        out_specs=pl.BlockSpec((tm, tco), lambda j, i: (i, j)), out_shape=S((T, J * tco), out_dtype),
        compiler_params=_params(("parallel", "parallel")))(x, x, x, w8, b)


def _conv_bwd(x, w8, b, dact, *, mode, W, name, C, xbase=0, tm=512, tc=512):
    T = x.shape[0]
    NI, J = T // tm, C // tc
    tco = tc // 2 if mode == "geglu" else tc
    rows = tm + 16
    pad = W // 2
    cur, prev, nxt = _conv_specs(T, tm, tc, xbase)
    dcur, dprev, dnxt = _conv_specs(T, tm, tco, 0)

    def body(xc, xp, xn, w_ref, b_ref, dc, dp, dn, dx_ref, dw_ref, db_ref):
        i = pl.program_id(1)
        ext = jnp.concatenate([jnp.where(i > 0, xp[...], 0.0), xc[...], jnp.where(i < NI - 1, xn[...], 0.0)], axis=0)
        dext = jnp.concatenate([jnp.where(i > 0, dp[...], 0.0), dc[...], jnp.where(i < NI - 1, dn[...], 0.0)], axis=0)
        wv = w_ref[...]
        shift = lambda t, w: t if w == pad else pltpu.roll(t, (pad - w) % rows, axis=0)
        xs = [shift(ext, w) for w in range(W)]
        u = b_ref[...] + xs[0] * wv[0:1, :]
        for w in range(1, W):
            u = u + xs[w] * wv[w:w + 1, :]
        _, vjp = jax.vjp(functools.partial(_conv_act, mode=mode), u)
        du = vjp(dext.astype(f32))[0]
        dx = shift(du, 0)[8:8 + tm] * wv[W - 1:W, :]
        for w in range(1, W):
            dx = dx + shift(du, w)[8:8 + tm] * wv[W - 1 - w:W - w, :]
        dx_ref[...] = dx.astype(dx_ref.dtype)

        @pl.when(i == 0)
        def _():
            dw_ref[...] = jnp.zeros_like(dw_ref)
            db_ref[...] = jnp.zeros_like(db_ref)

        dum = du[8:8 + tm]
        db_ref[...] += jnp.sum(dum, axis=0, keepdims=True)
        for w in range(W):
            dw_ref[w:w + 1, :] += jnp.sum(dum * xs[w][8:8 + tm], axis=0, keepdims=True)

    return pl.pallas_call(
        body, name=name, grid=(J, NI),
        in_specs=[cur, prev, nxt, pl.BlockSpec((8, tc), lambda j, i: (0, j)), pl.BlockSpec((1, tc), lambda j, i: (0, j)),
                  dcur, dprev, dnxt],
        out_specs=[pl.BlockSpec((tm, tc), lambda j, i: (i, j)), pl.BlockSpec((8, tc), lambda j, i: (0, j)),
                   pl.BlockSpec((1, tc), lambda j, i: (0, j))],
        out_shape=[S((T, C), bf16), S((8, C), f32), S((1, C), f32)],
        compiler_params=_params(("parallel", "arbitrary")))(x, x, x, w8, b, dact, dact, dact)


def _pad8(w):
    return jnp.concatenate([w, jnp.zeros((8 - w.shape[0], w.shape[1]), w.dtype)], axis=0)


def _exchange(arrs, *, name):
    n = len(arrs)
    ncopy = (NDEV - 1) * n

    def body(*refs):
        ins, outs = refs[:n], refs[n:2 * n]
        send_sems, recv_sems, loc_sems = refs[2 * n:]
        x, y, c = lax.axis_index("x"), lax.axis_index("y"), lax.axis_index("c")
        me = 4 * x + 2 * y + c

        def src(a, p):
            return ins[a].at[p] if arrs[a][1] else ins[a]

        local = [pltpu.make_async_copy(src(a, me), outs[a].at[me], loc_sems.at[a]) for a in range(n)]
        for cp in local:
            cp.start()
        sent = []
        for kk in range(1, NDEV):
            px = 1 - x if kk & 4 else x
            py = 1 - y if kk & 2 else y
            pc = 1 - c if kk & 1 else c
            peer = 4 * px + 2 * py + pc
            for a in range(n):
                idx = (kk - 1) * n + a
                mk = lambda dst_slot, a=a, idx=idx, peer=peer, dev=(px, py, pc): pltpu.make_async_remote_copy(
                    src_ref=src(a, peer), dst_ref=outs[a].at[dst_slot], send_sem=send_sems.at[idx], recv_sem=recv_sems.at[idx],
                    device_id=dev, device_id_type=pl.DeviceIdType.MESH)
                mk(me).start()
                sent.append((mk, peer))
        for mk, peer in sent:
            mk(peer).wait_recv()
        for mk, peer in sent:
            mk(peer).wait_send()
        for cp in local:
            cp.wait()

    any_spec = pl.BlockSpec(memory_space=pl.ANY)
    return pl.pallas_call(
        body, name=name, in_specs=[any_spec] * n, out_specs=[any_spec] * n,
        out_shape=[S(a.shape if pp else (NDEV,) + a.shape, a.dtype) for a, pp in arrs],
        scratch_shapes=[pltpu.SemaphoreType.DMA((ncopy,)), pltpu.SemaphoreType.DMA((ncopy,)), pltpu.SemaphoreType.DMA((n,))],
        )(*[a for a, _ in arrs])


def _adamw(r, w, m, v, *, name, tr):
    M, C = w.shape

    def body(r_ref, w_ref, m_ref, v_ref, g_ref, d_ref, nm_ref, nv_ref):
        g = r_ref[0].astype(f32)
        for s in range(1, NDEV):
            g = g + r_ref[s].astype(f32)
        m_ = B1 * m_ref[...] + (1.0 - B1) * g
        v_ = B2 * v_ref[...] + (1.0 - B2) * jnp.square(g)
        m_hat = m_ / (1.0 - B1 ** STEP)
        v_hat = v_ / (1.0 - B2 ** STEP)
        g_ref[...] = g
        d_ref[...] = -LR * (m_hat / (jnp.sqrt(v_hat) + AEPS) + WD * w_ref[...])
        nm_ref[...] = m_
        nv_ref[...] = v_

    row = pl.BlockSpec((tr, C), lambda i: (i, 0))
    return pl.pallas_call(
        body, name=name, grid=(M // tr,),
        in_specs=[pl.BlockSpec((NDEV, tr, C), lambda i: (0, i, 0)), row, row, row],
        out_specs=[row] * 4, out_shape=[S((M, C), f32)] * 4, compiler_params=_params(("parallel",)))(r, w, m, v)


def _colmove(ins, in_slots, outs, moves, *, tk, name):
    R = ins[0].shape[1] if in_slots[0] else ins[0].shape[0]
    n_in = len(ins)

    def body(*refs):
        for ii, isl, ic, oi, osl, oc, w in moves:
            src, dst = refs[ii], refs[n_in + oi]
            val = src[:, ic:ic + w] if isl is None else src[isl, :, ic:ic + w]
            if osl is None:
                dst[:, oc:oc + w] = val.astype(dst.dtype)
            else:
                dst[osl, :, oc:oc + w] = val.astype(dst.dtype)

    def spec(is_slots, C):
        return pl.BlockSpec((NDEV, tk, C), lambda i: (0, i, 0)) if is_slots else pl.BlockSpec((tk, C), lambda i: (i, 0))

    return pl.pallas_call(
        body, name=name, grid=(R // tk,),
        in_specs=[spec(sl, a.shape[-1]) for a, sl in zip(ins, in_slots)],
        out_specs=[spec(sl, C) for sl, C, _ in outs],
        out_shape=[S((NDEV, R, C) if sl else (R, C), dt) for sl, C, dt in outs],
        compiler_params=_params(("parallel",)))(*ins)


def _col_pieces(n8, cuts, place):
    out = []
    for p in range(NDEV):
        lo, hi = p * n8, (p + 1) * n8
        edges = [lo] + [c for c in cuts if lo < c < hi] + [hi]
        for a, b in zip(edges[:-1], edges[1:]):
            out.append((p, a - lo) + place(a) + (b - a,))
    return out


def _place_plain(c):
    return (0, c)


def _place_ssd_in(c):
    return (0, c) if c < SSD_INNER + SSD_XBC else (1, c - (SSD_INNER + SSD_XBC))


def _place_ffn_up(c):
    h = FFN_TC // 2
    return (0, (c // h) * FFN_TC + c % h) if c < FFN else (0, ((c - FFN) // h) * FFN_TC + h + (c - FFN) % h)


_COL_LAYOUTS = {
    "ab_w_in": ([], _place_plain, [4 * RW + 3 * NAW]),
    "c_w_in": ([SSD_INNER + SSD_XBC], _place_ssd_in, [SSD_INNER + SSD_XBC, 2 * SSD_H]),
    "ffn_w_up": (list(range(FFN_TC // 2, 2 * FFN, FFN_TC // 2)), _place_ffn_up, [2 * FFN]),
}


def _cols_from_slots(g, which, *, name):
    cuts, place, widths = _COL_LAYOUTS[which]
    moves = [(0, p, sc, mi, None, mc, w) for p, sc, mi, mc, w in _col_pieces(g.shape[2], cuts, place)]
    return _colmove([g], [True], [(False, w, g.dtype) for w in widths], moves, tk=256, name=name)


def _cols_to_slots(mats, which, dtype, *, name):
    cuts, place, widths = _COL_LAYOUTS[which]
    n8 = sum(widths) // NDEV
    moves = [(mi, None, mc, 0, p, sc, w) for p, sc, mi, mc, w in _col_pieces(n8, cuts, place)]
    return _colmove(list(mats), [False] * len(mats), [(True, n8, dtype)], moves, tk=256, name=name)[0]


def _tm2hm(a, H):
    T = a.shape[0]
    return a.reshape(T, H, -1).transpose(1, 0, 2)


def _hm2tm(a):
    H, T, P = a.shape
    return a.transpose(1, 0, 2).reshape(T, H * P)


def _pack(parts, dtype, row_mult):
    flat = jnp.concatenate([p.reshape(-1).astype(dtype) for p in parts])
    rows = -(-flat.shape[0] // LANES)
    rows = -(-rows // row_mult) * row_mult
    return jnp.pad(flat, (0, rows * LANES - flat.shape[0])).reshape(rows, LANES)


def _unpack(buf, shapes, lead=()):
    flat = buf.reshape(lead + (-1,))
    out, off = [], 0
    for shp in shapes:
        n = int(np.prod(shp))
        out.append(flat[..., off:off + n].reshape(lead + tuple(shp)))
        off += n
    return out


def _to_slots(full, ax):
    shp = full.shape
    return jnp.moveaxis(full.reshape(shp[:ax] + (NDEV, shp[ax] // NDEV) + shp[ax + 1:]), ax, 0)


def _from_slots(g, ax):
    t = jnp.moveaxis(g, 0, ax)
    shp = t.shape
    return t.reshape(shp[:ax] + (shp[ax] * shp[ax + 1],) + shp[ax + 2:])


def _ffn_perm(a):
    lead = a.shape[:-1]
    h = FFN_TC // 2
    return jnp.swapaxes(a.reshape(lead + (2, FFN // h, h)), -3, -2).reshape(lead + (2 * FFN,))


def _ffn_unperm(a):
    lead = a.shape[:-1]
    h = FFN_TC // 2
    return jnp.swapaxes(a.reshape(lead + (FFN // h, 2, h)), -3, -2).reshape(lead + (2 * FFN,))


def _rope_tables(T):
    half = RDH // 2
    inv = 1.0 / (ROPE_BASE ** (jnp.arange(half, dtype=f32) / half))
    ang = jnp.arange(T, dtype=f32)[:, None] * inv[None, :]
    cos, sin = jnp.cos(ang), jnp.sin(ang)
    cos_t = jnp.tile(jnp.concatenate([cos, cos], axis=1), (1, RH))
    sin_t = jnp.tile(jnp.concatenate([-sin, sin], axis=1), (1, RH))
    return cos_t, sin_t


def _group_avg():
    g = np.arange(RW) // RDH
    return jnp.asarray((g[:, None] == g[None, :]).astype(np.float32) / RDH)


def _head_expand():
    hd = np.arange(SSD_INNER) // SSD_HD
    rows = np.arange(2 * SSD_H)
    ex0 = (rows[:, None] == hd[None, :]).astype(np.float32)
    ex1 = (rows[:, None] == SSD_H + hd[None, :]).astype(np.float32)
    return jnp.asarray(ex0), jnp.asarray(ex1)


def kernel(x, norm_mix_pre, norm_mix_post, norm_ffn_pre, norm_ffn_post, ab_w_in, ab_ret_decay_logit, ab_ret_gn_g, ab_na_rpb, ab_w_out, c_w_in, c_conv_w, c_conv_b, c_dt_bias, c_a_log, c_d_skip, c_norm_g, c_w_out, ffn_w_up, ffn_conv_w, ffn_conv_b, ffn_w_down, loss_target, m_norm_mix_pre, m_norm_mix_post, m_norm_ffn_pre, m_norm_ffn_post, m_ab_w_in, m_ab_ret_decay_logit, m_ab_ret_gn_g, m_ab_na_rpb, m_ab_w_out, m_c_w_in, m_c_conv_w, m_c_conv_b, m_c_dt_bias, m_c_a_log, m_c_d_skip, m_c_norm_g, m_c_w_out, m_ffn_w_up, m_ffn_conv_w, m_ffn_conv_b, m_ffn_w_down, v_norm_mix_pre, v_norm_mix_post, v_norm_ffn_pre, v_norm_ffn_post, v_ab_w_in, v_ab_ret_decay_logit, v_ab_ret_gn_g, v_ab_na_rpb, v_ab_w_out, v_c_w_in, v_c_conv_w, v_c_conv_b, v_c_dt_bias, v_c_a_log, v_c_d_skip, v_c_norm_g, v_c_w_out, v_ffn_w_up, v_ffn_conv_w, v_ffn_conv_b, v_ffn_w_down):
    W = dict(norm_mix_pre=norm_mix_pre, norm_mix_post=norm_mix_post, norm_ffn_pre=norm_ffn_pre, norm_ffn_post=norm_ffn_post, ab_w_in=ab_w_in, ab_ret_decay_logit=ab_ret_decay_logit, ab_ret_gn_g=ab_ret_gn_g, ab_na_rpb=ab_na_rpb, ab_w_out=ab_w_out, c_w_in=c_w_in, c_conv_w=c_conv_w, c_conv_b=c_conv_b, c_dt_bias=c_dt_bias, c_a_log=c_a_log, c_d_skip=c_d_skip, c_norm_g=c_norm_g, c_w_out=c_w_out, ffn_w_up=ffn_w_up, ffn_conv_w=ffn_conv_w, ffn_conv_b=ffn_conv_b, ffn_w_down=ffn_w_down)
    Mo = dict(norm_mix_pre=m_norm_mix_pre, norm_mix_post=m_norm_mix_post, norm_ffn_pre=m_norm_ffn_pre, norm_ffn_post=m_norm_ffn_post, ab_w_in=m_ab_w_in, ab_ret_decay_logit=m_ab_ret_decay_logit, ab_ret_gn_g=m_ab_ret_gn_g, ab_na_rpb=m_ab_na_rpb, ab_w_out=m_ab_w_out, c_w_in=m_c_w_in, c_conv_w=m_c_conv_w, c_conv_b=m_c_conv_b, c_dt_bias=m_c_dt_bias, c_a_log=m_c_a_log, c_d_skip=m_c_d_skip, c_norm_g=m_c_norm_g, c_w_out=m_c_w_out, ffn_w_up=m_ffn_w_up, ffn_conv_w=m_ffn_conv_w, ffn_conv_b=m_ffn_conv_b, ffn_w_down=m_ffn_w_down)
    Vo = dict(norm_mix_pre=v_norm_mix_pre, norm_mix_post=v_norm_mix_post, norm_ffn_pre=v_norm_ffn_pre, norm_ffn_post=v_norm_ffn_post, ab_w_in=v_ab_w_in, ab_ret_decay_logit=v_ab_ret_decay_logit, ab_ret_gn_g=v_ab_ret_gn_g, ab_na_rpb=v_ab_na_rpb, ab_w_out=v_ab_w_out, c_w_in=v_c_w_in, c_conv_w=v_c_conv_w, c_conv_b=v_c_conv_b, c_dt_bias=v_c_dt_bias, c_a_log=v_c_a_log, c_d_skip=v_c_d_skip, c_norm_g=v_c_norm_g, c_w_out=v_c_w_out, ffn_w_up=v_ffn_w_up, ffn_conv_w=v_ffn_conv_w, ffn_conv_b=v_ffn_conv_b, ffn_w_down=v_ffn_w_down)
    return _train_step(x[0], loss_target[0], W, Mo, Vo)


def _train_step(x, tgt, W, Mo, Vo):
    T = x.shape[0]
    rows = T // GRID_W

    col = lambda d, n, dt: d[n].reshape(-1, d[n].shape[-1]).astype(dt)
    rows_of = lambda d, dt: jnp.concatenate([col(d, n, dt) for n in ROW_SHARDED], axis=0)
    small = _pack([W[n] for n, _ in SHARDED[N_BIG:]], f32, 8)
    gat = _exchange([(col(W, n, bf16), False) for n in COL_SHARDED] + [(rows_of(W, bf16), False), (small, False)],
                    name="gather_weights")
    per_layer = lambda m: m.reshape(-1, D, m.shape[-1])
    w_ab_in = per_layer(_cols_from_slots(gat[0], "ab_w_in", name="cols_ab_w_in")[0])
    w_zx, w_dt = [per_layer(m) for m in _cols_from_slots(gat[1], "c_w_in", name="cols_c_w_in")]
    w_up = per_layer(_cols_from_slots(gat[2], "ffn_w_up", name="cols_ffn_w_up")[0])
    full, off = {}, 0
    for n in ROW_SHARDED:
        L, r = W[n].shape[0], W[n].shape[1]
        full[n] = jnp.swapaxes(gat[3][:, off:off + L * r].reshape(NDEV, L, r, D), 0, 1).reshape(L, NDEV * r, D)
        off += L * r
    gs = _unpack(gat[4], [W[n].shape for n, _ in SHARDED[N_BIG:]], (NDEV,))
    full.update({n: _from_slots(g, ax) for (n, ax), g in zip(SHARDED[N_BIG:], gs)})
    w_ab_out, w_c_out, w_down = full["ab_w_out"], full["c_w_out"], full["ffn_w_down"]
    c_cw8 = [_pad8(full["c_conv_w"][i]) for i in range(2)]
    c_cb = [full["c_conv_b"][i][None] for i in range(2)]
    c_ng = [full["c_norm_g"][i][None] for i in range(2)]
    f_cw8 = [_pad8(_ffn_perm(full["ffn_conv_w"][l])) for l in range(DEPTH)]
    f_cb = [_ffn_perm(W["ffn_conv_b"][l])[None] for l in range(DEPTH)]

    g1 = [W["norm_mix_pre"][l][None] for l in range(DEPTH)]
    g2 = [W["norm_mix_post"][l][None] for l in range(DEPTH)]
    g3 = [W["norm_ffn_pre"][l][None] for l in range(DEPTH)]
    g4 = [W["norm_ffn_post"][l][None] for l in range(DEPTH)]
    cos_t, sin_t = _rope_tables(T)
    gavg = _group_avg()
    ex0, ex1 = _head_expand()

    def log_gamma(logit):
        return -jax.nn.softplus(-logit)

    def ret_decays(lg):
        return [jnp.broadcast_to(lg[d][:, None, None], (RH, T, 8)) for d in range(2)]

    saved = []
    xs_ = x
    hn = _rowwise("norm_first", _f_first, [(x, D, 0)], [], [(g1[0], D, 0)], [], [(D, bf16)], tm=256)[0]
    for l in range(DEPTH):
        i = l // 2
        sv = dict(x=xs_, hn=hn)
        if l % 2 == 0:
            proj = _mm_nn(hn, w_ab_in[i], name=f"ab_in_{l}")
            qr, kr = _rowwise(f"ret_prep_{l}", _f_rprep, [(proj, RW, 0), (proj, RW, 1)], [(cos_t, RW, 0), (sin_t, RW, 0)], [], [],
                              [(RW, f32), (RW, f32)], tm=256)
            q_h, k_h, v_h = _tm2hm(qr, RH), _tm2hm(kr, RH), _tm2hm(proj[:, 2 * RW:3 * RW], RH)
            lg, lg_vjp = jax.vjp(log_gamma, W["ab_ret_decay_logit"][i])
            a_f, a_b = ret_decays(lg)
            yf, hsf = _scan_fwd(q_h, k_h, v_h, a_f, rev=False, name=f"ret_scan_f_{l}")
            yb, hsb = _scan_fwd(q_h, k_h, v_h, a_b, rev=True, name=f"ret_scan_b_{l}")
            yf_t, yb_t = _hm2tm(yf), _hm2tm(yb)
            gn = W["ab_ret_gn_g"][i][None]
            ret = _rowwise(f"ret_post_{l}", _f_rpost, [(yf_t, RW, 0), (yb_t, RW, 0), (proj, RW, 3)], [], [(gn, RW, 0)], [gavg],
                           [(RW, bf16)], tm=256)[0]
            nq, nk, nv = [_tm2hm(proj[:, (4 + j) * RW:(5 + j) * RW], NAH).astype(bf16) for j in range(3)]
            bias, bias_vjp = jax.vjp(functools.partial(_na_bias, rows=rows), W["ab_na_rpb"][i])
            na_o, na_l = _na_fwd(nq, nk, nv, bias, name=f"na_fwd_{l}")
            cat = jnp.concatenate([ret, _hm2tm(na_o).astype(bf16)], axis=1)
            mo = _mm_nn(cat, w_ab_out[i], name=f"ab_out_{l}")
            sv.update(proj=proj, q_h=q_h, k_h=k_h, v_h=v_h, a_f=a_f, a_b=a_b, hsf=hsf, hsb=hsb, yf_t=yf_t, yb_t=yb_t, gn=gn,
                      nq=nq, nk=nk, nv=nv, bias=bias, bias_vjp=bias_vjp, lg_vjp=lg_vjp, na_o=na_o, na_l=na_l, cat=cat)
        else:
            zx = _mm_nn(hn, w_zx[i], name=f"c_in_{l}")
            dtr = _mm_nn(hn, w_dt[i], name=f"c_in_dt_{l}")
            xa = _conv(zx, c_cw8[i], c_cb[i], mode="silu", W=SSD_CONV, name=f"c_conv_{l}", C=SSD_XBC, xbase=SSD_INNER // 512)
            dtb, alog = W["c_dt_bias"][i].reshape(1, 2 * SSD_H), W["c_a_log"][i].reshape(1, 2 * SSD_H)
            vf, vb, la = _rowwise(f"ssd_prep_{l}", _f_sprep, [(xa, SSD_INNER, 0), (dtr, 2 * SSD_H, 0)], [],
                                  [(dtb, 2 * SSD_H, 0), (alog, 2 * SSD_H, 0)], [ex0, ex1],
                                  [(SSD_INNER, f32), (SSD_INNER, f32), (2 * SSD_H, f32)], tm=128)
            k_h = _tm2hm(xa[:, SSD_INNER:SSD_INNER + SSD_G * SSD_N], SSD_G)
            q_h = _tm2hm(xa[:, SSD_INNER + SSD_G * SSD_N:], SSD_G)
            a_f = la[:, :SSD_H].reshape(T, SSD_G, SSD_HPG).transpose(1, 0, 2)
            a_b = la[:, SSD_H:].reshape(T, SSD_G, SSD_HPG).transpose(1, 0, 2)
            vf_h, vb_h = _tm2hm(vf, SSD_H), _tm2hm(vb, SSD_H)
            yf, hsf = _scan_fwd(q_h, k_h, vf_h, a_f, rev=False, name=f"ssd_scan_f_{l}")
            yb, hsb = _scan_fwd(q_h, k_h, vb_h, a_b, rev=True, name=f"ssd_scan_b_{l}")
            yf_t, yb_t = _hm2tm(yf), _hm2tm(yb)
            dsk = jnp.repeat(W["c_d_skip"][i], SSD_HD)[None]
            yo = _rowwise(f"ssd_post_{l}", _f_spost, [(yf_t, 512, 0), (yb_t, 512, 0), (xa, 512, 0), (zx, 512, 0)], [],
                          [(dsk, 512, 0), (c_ng[i], 512, 0)], [], [(512, bf16)], tm=256, J=SSD_G)[0]
            mo = _mm_nn(yo, w_c_out[i], name=f"c_out_{l}")
            sv.update(zx=zx, dtr=dtr, xa=xa, dtb=dtb, alog=alog, q_h=q_h, k_h=k_h, a_f=a_f, a_b=a_b, vf_h=vf_h, vb_h=vb_h,
                      hsf=hsf, hsb=hsb, yf_t=yf_t, yb_t=yb_t, dsk=dsk, yo=yo)
        x1, hf = _rowwise(f"norm_mid_{l}", _f_mid, [(xs_, D, 0), (mo, D, 0)], [], [(g2[l], D, 0), (g3[l], D, 0)], [],
                          [(D, f32), (D, bf16)], tm=256)
        pre = _mm_nn(hf, w_up[l], name=f"ffn_up_{l}")
        act = _conv(pre, f_cw8[l], f_cb[l], mode="geglu", W=FFN_CONV, name=f"ffn_conv_{l}", C=2 * FFN, tc=FFN_TC, out_dtype=bf16)
        fo = _mm_nn(act, w_down[l], name=f"ffn_down_{l}")
        sv.update(mo=mo, x1=x1, hf=hf, pre=pre, act=act, fo=fo)
        if l < DEPTH - 1:
            xs_, hn = _rowwise(f"norm_end_{l}", _f_end, [(x1, D, 0), (fo, D, 0)], [], [(g4[l], D, 0), (g1[l + 1], D, 0)], [],
                               [(D, f32), (D, bf16)], tm=256)
        else:
            xs_ = _rowwise(f"norm_end_{l}", _f_last, [(x1, D, 0), (fo, D, 0)], [], [(g4[l], D, 0)], [], [(D, f32)], tm=256)[0]
        saved.append(sv)

    dx, lpart = _loss_call(xs_, tgt)
    loss = lax.psum(lpart[0, 0], ("x", "y", "c"))

    G = {n: [None] * W[n].shape[0] for n in WEIGHTS}
    dhn = None
    for l in reversed(range(DEPTH)):
        i = l // 2
        sv = saved[l]
        if l == DEPTH - 1:
            (dx1, dfo), (dg4,) = _rowwise_bwd(f"norm_end_bwd_{l}", _f_last, [(sv["x1"], D, 0), (sv["fo"], D, 0)], [],
                                              [(g4[l], D, 0)], [], [(dx, D, 0)], [f32, bf16], tm=256)
        else:
            (dx1, dfo), (dg4, dg1n) = _rowwise_bwd(f"norm_end_bwd_{l}", _f_end, [(sv["x1"], D, 0), (sv["fo"], D, 0)], [],
                                                   [(g4[l], D, 0), (g1[l + 1], D, 0)], [], [(dx, D, 0), (dhn, D, 0)],
                                                   [f32, bf16], tm=256)
            G["norm_mix_pre"][l + 1] = dg1n[0]
        G["norm_ffn_post"][l] = dg4[0]
        dact = _mm_nt(dfo, w_down[l], name=f"ffn_down_dx_{l}")
        G["ffn_w_down"][l] = _mm_tn(sv["act"], dfo, name=f"ffn_down_dw_{l}")
        dpre, dfw, dfb = _conv_bwd(sv["pre"], f_cw8[l], f_cb[l], dact, mode="geglu", W=FFN_CONV, name=f"ffn_conv_bwd_{l}",
                                   C=2 * FFN, tc=FFN_TC)
        dhf = _mm_nt(dpre, w_up[l], name=f"ffn_up_dx_{l}")
        G["ffn_w_up"][l] = _cols_to_slots([_mm_tn(sv["hf"], dpre, name=f"ffn_up_dw_{l}")], "ffn_w_up", bf16, name=f"slots_ffn_up_{l}")
        G["ffn_conv_w"][l] = _ffn_unperm(dfw[:FFN_CONV])
        G["ffn_conv_b"][l] = _ffn_unperm(dfb[0])
        (dxl, dmo), (dg2, dg3) = _rowwise_bwd(f"norm_mid_bwd_{l}", _f_mid, [(sv["x"], D, 0), (sv["mo"], D, 0)], [],
                                              [(g2[l], D, 0), (g3[l], D, 0)], [], [(dx1, D, 0), (dhf, D, 0)], [f32, bf16], tm=256)
        G["norm_mix_post"][l] = dg2[0]
        G["norm_ffn_pre"][l] = dg3[0]
        if l % 2 == 0:
            dcat = _mm_nt(dmo, w_ab_out[i], name=f"ab_out_dx_{l}")
            G["ab_w_out"][i] = _mm_tn(sv["cat"], dmo, name=f"ab_out_dw_{l}")
            (dyf, _, drg), (dgn,) = _rowwise_bwd(
                f"ret_post_bwd_{l}", _f_rpost, [(sv["yf_t"], RW, 0), (sv["yb_t"], RW, 0), (sv["proj"], RW, 3)], [],
                [(sv["gn"], RW, 0)], [gavg], [(dcat, RW, 0)], [f32, f32, bf16], tm=256)
            G["ab_ret_gn_g"][i] = dgn[0]
            dy_h = _tm2hm(dyf, RH)
            dqf, dkf, dvf, daf = _scan_bwd(sv["q_h"], sv["k_h"], sv["v_h"], sv["a_f"], sv["hsf"], dy_h, rev=False,
                                           name=f"ret_scan_f_bwd_{l}")
            dqb, dkb, dvb, dab = _scan_bwd(sv["q_h"], sv["k_h"], sv["v_h"], sv["a_b"], sv["hsb"], dy_h, rev=True,
                                           name=f"ret_scan_b_bwd_{l}")
            dq_t, dk_t, drv = _hm2tm(dqf + dqb), _hm2tm(dkf + dkb), _hm2tm(dvf + dvb).astype(bf16)
            (drq, drk), _ = _rowwise_bwd(f"ret_prep_bwd_{l}", _f_rprep, [(sv["proj"], RW, 0), (sv["proj"], RW, 1)],
                                         [(cos_t, RW, 0), (sin_t, RW, 0)], [], [], [(dq_t, RW, 0), (dk_t, RW, 0)], [bf16, bf16], tm=256)
            da_cols = jnp.concatenate([_hm2tm(daf), _hm2tm(dab)], axis=1)
            dlg = _colsum(da_cols, name=f"ret_decay_sum_{l}").reshape(2, RH, 8)[:, :, 0]
            G["ab_ret_decay_logit"][i] = sv["lg_vjp"](dlg)[0]
            dna = _tm2hm(dcat[:, RW:], NAH)
            dnq, dnk, dnv, dbias = _na_bwd(sv["nq"], sv["nk"], sv["nv"], sv["bias"], sv["na_o"], sv["na_l"], dna, name=f"na_bwd_{l}")
            G["ab_na_rpb"][i] = sv["bias_vjp"](dbias)[0]
            dproj = jnp.concatenate([drq, drk, drv, drg] + [_hm2tm(t).astype(bf16) for t in (dnq, dnk, dnv)], axis=1)
            dhn = _mm_nt(dproj, w_ab_in[i], name=f"ab_in_dx_{l}")
            G["ab_w_in"][i] = _cols_to_slots([_mm_tn(sv["hn"], dproj, name=f"ab_in_dw_{l}")], "ab_w_in", bf16, name=f"slots_ab_in_{l}")
        else:
            dyo = _mm_nt(dmo, w_c_out[i], name=f"c_out_dx_{l}")
            G["c_w_out"][i] = _mm_tn(sv["yo"], dmo, name=f"c_out_dw_{l}")
            (dyf, _, dxs1, dz), (ddsk, dng) = _rowwise_bwd(
                f"ssd_post_bwd_{l}", _f_spost, [(sv["yf_t"], 512, 0), (sv["yb_t"], 512, 0), (sv["xa"], 512, 0), (sv["zx"], 512, 0)],
                [], [(sv["dsk"], 512, 0), (c_ng[i], 512, 0)], [], [(dyo, 512, 0)], [f32, f32, f32, bf16], tm=256, J=SSD_G)
            G["c_d_skip"][i] = ddsk.reshape(SSD_H, SSD_HD).sum(axis=1)
            G["c_norm_g"][i] = dng[0]
            dy_h = _tm2hm(dyf, SSD_H)
            dqf, dkf, dvf, daf = _scan_bwd(sv["q_h"], sv["k_h"], sv["vf_h"], sv["a_f"], sv["hsf"], dy_h, rev=False,
                                           name=f"ssd_scan_f_bwd_{l}")
            dqb, dkb, dvb, dab = _scan_bwd(sv["q_h"], sv["k_h"], sv["vb_h"], sv["a_b"], sv["hsb"], dy_h, rev=True,
                                           name=f"ssd_scan_b_bwd_{l}")
            dla = jnp.concatenate([daf.transpose(1, 0, 2).reshape(T, SSD_H), dab.transpose(1, 0, 2).reshape(T, SSD_H)], axis=1)
            (dxs2, ddtr), (ddtb, dalog) = _rowwise_bwd(
                f"ssd_prep_bwd_{l}", _f_sprep, [(sv["xa"], SSD_INNER, 0), (sv["dtr"], 2 * SSD_H, 0)], [],
                [(sv["dtb"], 2 * SSD_H, 0), (sv["alog"], 2 * SSD_H, 0)], [ex0, ex1],
                [(_hm2tm(dvf), SSD_INNER, 0), (_hm2tm(dvb), SSD_INNER, 0), (dla, 2 * SSD_H, 0)], [f32, bf16], tm=128)
            G["c_dt_bias"][i] = ddtb.reshape(2, SSD_H)
            G["c_a_log"][i] = dalog.reshape(2, SSD_H)
            dxa = jnp.concatenate([dxs1 + dxs2, _hm2tm(dkf + dkb), _hm2tm(dqf + dqb)], axis=1)
            dxbc, dcw, dcb = _conv_bwd(sv["zx"], c_cw8[i], c_cb[i], dxa, mode="silu", W=SSD_CONV, name=f"c_conv_bwd_{l}",
                                       C=SSD_XBC, xbase=SSD_INNER // 512)
            G["c_conv_w"][i] = dcw[:SSD_CONV]
            G["c_conv_b"][i] = dcb[0]
            dzx = jnp.concatenate([dz, dxbc], axis=1)
            t1 = _mm_nt(ddtr, w_dt[i], name=f"c_in_dt_dx_{l}")
            dhn = _mm_nt(dzx, w_zx[i], add=t1, name=f"c_in_dx_{l}")
            G["c_w_in"][i] = _cols_to_slots([_mm_tn(sv["hn"], dzx, name=f"c_in_dw_{l}"), _mm_tn(sv["hn"], ddtr, name=f"c_in_dt_dw_{l}")],
                                            "c_w_in", bf16, name=f"slots_c_in_{l}")
        dx = dxl
    (grad_x,), (dg1,) = _rowwise_bwd("norm_first_bwd", _f_first_bwd, [(x, D, 0)], [], [(g1[0], D, 0)], [], [(dx, D, 0), (dhn, D, 0)],
                                     [f32], tm=256)
    G["norm_mix_pre"][0] = dg1[0]

    small_names = [n for n, _ in SHARDED[N_BIG:]]
    col_slots = [jnp.concatenate(G[n], axis=1) for n in COL_SHARDED]
    row_slots = jnp.concatenate([g.reshape(NDEV, -1, D).astype(bf16) for n in ROW_SHARDED for g in G[n]], axis=1)
    small_slots = _pack_slots([_to_slots(jnp.stack(G[n]), ax) for n, ax in SHARDED[N_BIG:]], 8)
    ar = _pack([jnp.stack(G[n]) for n in REPLICATED], f32, 8)
    exch = _exchange([(a, True) for a in col_slots + [row_slots, small_slots]] + [(ar, False)], name="exchange_grads")
    pk = lambda d, names: _pack([d[n] for n in names], f32, 8)
    upd = [_adamw(exch[j], col(W, n, f32), col(Mo, n, f32), col(Vo, n, f32), name=f"adamw_{n}", tr=256)
           for j, n in enumerate(COL_SHARDED)]
    upd_rows = _adamw(exch[3], rows_of(W, f32), rows_of(Mo, f32), rows_of(Vo, f32), name="adamw_rows", tr=64)
    upd_small = _adamw(exch[4], pk(W, small_names), pk(Mo, small_names), pk(Vo, small_names), name="adamw_small",
                       tr=small_slots.shape[1])
    upd_rep = _adamw(exch[5], pk(W, REPLICATED), pk(Mo, REPLICATED), pk(Vo, REPLICATED), name="adamw_replicated", tr=ar.shape[0])
    res = []
    for k in range(4):
        d = {n: upd[j][k].reshape(W[n].shape) for j, n in enumerate(COL_SHARDED)}
        off = 0
        for n in ROW_SHARDED:
            cnt = W[n].shape[0] * W[n].shape[1]
            d[n] = upd_rows[k][off:off + cnt].reshape(W[n].shape)
            off += cnt
        d.update(zip(small_names, _unpack(upd_small[k], [W[n].shape for n in small_names])))
        d.update(zip(REPLICATED, _unpack(upd_rep[k], [W[n].shape for n in REPLICATED])))
        res.append(d)
    outs = [loss, grad_x[None]]
    for k in range(4):
        outs += [res[k][n] for n in WEIGHTS]
    return tuple(outs)


def _pack_slots(slot_arrays, row_mult):
    flat = jnp.concatenate([a.reshape(NDEV, -1) for a in slot_arrays], axis=1)
    rows = -(-flat.shape[1] // LANES)
    rows = -(-rows // row_mult) * row_mult
    return jnp.pad(flat, ((0, 0), (0, rows * LANES - flat.shape[1]))).reshape(NDEV, rows, LANES)
```

```python
import functools
import numpy as np
import jax
import jax.numpy as jnp
from jax import lax
from jax.experimental import pallas as pl
from jax.experimental.pallas import tpu as pltpu

f32, bf16 = jnp.float32, jnp.bfloat16
S = jax.ShapeDtypeStruct
HI = lax.Precision.HIGHEST

D = 1024
DEPTH = 4
GRID_W = 64
CHUNK = 128
EPS = 1e-6
RH, RDH, RW = 8, 64, 512
NAH, NADH, NAW = 8, 64, 512
NA_WR, NA_WC = 8, 16
NA_QROWS = 8
NA_KROWS = 16
NA_PAIR = 2
SSD_INNER, SSD_HD, SSD_H, SSD_G, SSD_HPG, SSD_N, SSD_CONV = 2048, 64, 32, 4, 8, 128, 5
SSD_XBC = SSD_INNER + 2 * SSD_G * SSD_N
FFN, FFN_CONV = 2816, 3
FFN_TC = 512
SCAN_HEADS_PER_STEP = 8
ROPE_BASE = 10000.0
LR, B1, B2, AEPS, WD, STEP = 0.001, 0.9, 0.999, 1e-08, 0.01, 10
NDEV = 8
LANES = 128
VMEM_LIMIT = 56 * 1024 * 1024
MM_BLOCK_BYTES = 6 * 1024 * 1024

NT = (((1,), (1,)), ((), ()))
TN = (((0,), (0,)), ((), ()))

SHARDED = [("ab_w_in", 2), ("ab_w_out", 1), ("c_w_in", 2), ("c_w_out", 1), ("ffn_w_up", 2), ("ffn_w_down", 1),
           ("c_conv_w", 2), ("c_conv_b", 1), ("c_norm_g", 1), ("ffn_conv_w", 2)]
N_BIG = 6
COL_SHARDED = ["ab_w_in", "c_w_in", "ffn_w_up"]
ROW_SHARDED = ["ab_w_out", "c_w_out", "ffn_w_down"]
REPLICATED = ["norm_mix_pre", "norm_mix_post", "norm_ffn_pre", "norm_ffn_post", "ab_ret_decay_logit", "ab_ret_gn_g",
              "ab_na_rpb", "c_dt_bias", "c_a_log", "c_d_skip", "ffn_conv_b"]
WEIGHTS = ["norm_mix_pre", "norm_mix_post", "norm_ffn_pre", "norm_ffn_post", "ab_w_in", "ab_ret_decay_logit",
           "ab_ret_gn_g", "ab_na_rpb", "ab_w_out", "c_w_in", "c_conv_w", "c_conv_b", "c_dt_bias", "c_a_log", "c_d_skip",
           "c_norm_g", "c_w_out", "ffn_w_up", "ffn_conv_w", "ffn_conv_b", "ffn_w_down"]


def _params(sem=None):
    return pltpu.CompilerParams(dimension_semantics=sem, vmem_limit_bytes=VMEM_LIMIT)


def _mm_nn(a, w, *, name, tm=1024, tn=512, out_dtype=f32):
    M, K = a.shape
    N = w.shape[1]
    tn = min(tn, N)

    def body(a_ref, w_ref, o_ref):
        o_ref[...] = jnp.dot(a_ref[...], w_ref[...], preferred_element_type=f32).astype(o_ref.dtype)

    return pl.pallas_call(
        body, name=name, grid=(M // tm, N // tn),
        in_specs=[pl.BlockSpec((tm, K), lambda i, j: (i, 0)), pl.BlockSpec((K, tn), lambda i, j: (0, j))],
        out_specs=pl.BlockSpec((tm, tn), lambda i, j: (i, j)),
        out_shape=S((M, N), out_dtype), compiler_params=_params(("parallel", "parallel")))(a, w)


def _mm_nt(dy, w, *, name, add=None, tm=512):
    M, N = dy.shape
    K = w.shape[0]
    tk = next((t for t in (1024, 1408, 512, 256, 128) if K % t == 0 and (t <= 512 or t * N * 2 <= MM_BLOCK_BYTES)), K)

    def body(*refs):
        if add is None:
            d_ref, w_ref, o_ref = refs
            o_ref[...] = lax.dot_general(d_ref[...], w_ref[...], NT, preferred_element_type=f32)
        else:
            d_ref, w_ref, a_ref, o_ref = refs
            o_ref[...] = lax.dot_general(d_ref[...], w_ref[...], NT, preferred_element_type=f32) + a_ref[...]

    in_specs = [pl.BlockSpec((tm, N), lambda i, j: (i, 0)), pl.BlockSpec((tk, N), lambda i, j: (j, 0))]
    args = [dy, w]
    if add is not None:
        in_specs.append(pl.BlockSpec((tm, tk), lambda i, j: (i, j)))
        args.append(add)
    return pl.pallas_call(
        body, name=name, grid=(M // tm, K // tk), in_specs=in_specs,
        out_specs=pl.BlockSpec((tm, tk), lambda i, j: (i, j)),
        out_shape=S((M, K), f32), compiler_params=_params(("parallel", "parallel")))(*args)


def _mm_tn(a, dy, *, name, tt=1024):
    M, K = a.shape
    N = dy.shape[1]
    tk = K if K <= 1024 else (1024 if K % 1024 == 0 else K // 2)
    tn = min(512, N)
    tt = min(tt, M)

    def body(a_ref, d_ref, o_ref):
        t = pl.program_id(2)
        part = lax.dot_general(a_ref[...], d_ref[...], TN, preferred_element_type=f32)

        @pl.when(t == 0)
        def _():
            o_ref[...] = part

        @pl.when(t > 0)
        def _():
            o_ref[...] += part

    return pl.pallas_call(
        body, name=name, grid=(K // tk, N // tn, M // tt),
        in_specs=[pl.BlockSpec((tt, tk), lambda k, n, t: (t, k)), pl.BlockSpec((tt, tn), lambda k, n, t: (t, n))],
        out_specs=pl.BlockSpec((tk, tn), lambda k, n, t: (k, n)),
        out_shape=S((K, N), f32), compiler_params=_params(("parallel", "parallel", "arbitrary")))(a, dy)


def _tile_spec(tm, width, base):
    return pl.BlockSpec((tm, width), lambda j, i: (i, base + j))


def _par_spec(width, base):
    return pl.BlockSpec((1, width), lambda j, i: (0, base + j))


def _full_spec(a):
    nd = a.ndim
    return pl.BlockSpec(a.shape, lambda j, i: (0,) * nd)


def _rowwise(name, f, tiles, ctiles, params, consts, outs, *, tm, J=1):
    T = tiles[0][0].shape[0]
    nt, nct, npar, nc = len(tiles), len(ctiles), len(params), len(consts)

    def body(*refs):
        tv = [r[...].astype(f32) for r in refs[:nt + nct]]
        pv = [r[...] for r in refs[nt + nct:nt + nct + npar + nc]]
        res = f(*tv, *pv)
        for o, v in zip(refs[nt + nct + npar + nc:], res):
            o[...] = v.astype(o.dtype)

    in_specs = ([_tile_spec(tm, w, b) for _, w, b in tiles + ctiles] + [_par_spec(w, b) for _, w, b in params]
                + [_full_spec(c) for c in consts])
    return pl.pallas_call(
        body, name=name, grid=(J, T // tm), in_specs=in_specs,
        out_specs=[_tile_spec(tm, w, 0) for w, _ in outs],
        out_shape=[S((T, J * w), dt) for w, dt in outs],
        compiler_params=_params(("parallel", "parallel")))(
            *[a for a, _, _ in tiles + ctiles], *[a for a, _, _ in params], *consts)


def _rowwise_bwd(name, f, tiles, ctiles, params, consts, douts, dtile_dtypes, *, tm, J=1):
    T = tiles[0][0].shape[0]
    nt, nct, npar, nc, nd = len(tiles), len(ctiles), len(params), len(consts), len(douts)

    def body(*refs):
        i = pl.program_id(1)
        k = 0
        tv = [r[...].astype(f32) for r in refs[k:k + nt]]; k += nt
        cv = [r[...].astype(f32) for r in refs[k:k + nct]]; k += nct
        pv = [r[...] for r in refs[k:k + npar]]; k += npar
        kv = [r[...] for r in refs[k:k + nc]]; k += nc
        dv = [r[...].astype(f32) for r in refs[k:k + nd]]; k += nd
        dt_refs = refs[k:k + nt]; k += nt
        dp_refs = refs[k:k + npar]
        _, vjp = jax.vjp(lambda tv_, pv_: tuple(f(*tv_, *cv, *pv_, *kv)), tv, pv)
        dts, dps = vjp(tuple(dv))
        for r, g in zip(dt_refs, dts):
            r[...] = g.astype(r.dtype)
        for r, g in zip(dp_refs, dps):
            @pl.when(i == 0)
            def _(r=r, g=g):
                r[...] = g

            @pl.when(i > 0)
            def _(r=r, g=g):
                r[...] += g

    in_specs = ([_tile_spec(tm, w, b) for _, w, b in tiles + ctiles] + [_par_spec(w, b) for _, w, b in params]
                + [_full_spec(c) for c in consts] + [_tile_spec(tm, w, b) for _, w, b in douts])
    res = pl.pallas_call(
        body, name=name, grid=(J, T // tm), in_specs=in_specs,
        out_specs=[_tile_spec(tm, w, 0) for _, w, _ in tiles] + [_par_spec(w, b) for _, w, b in params],
        out_shape=[S((T, J * w), dt) for (_, w, _), dt in zip(tiles, dtile_dtypes)] + [S(a.shape, f32) for a, _, _ in params],
        compiler_params=_params(("parallel", "arbitrary")))(
            *[a for a, _, _ in tiles + ctiles], *[a for a, _, _ in params], *consts, *[a for a, _, _ in douts])
    return res[:nt], res[nt:]


def _rms(x, g):
    return x * lax.rsqrt(jnp.mean(x * x, axis=-1, keepdims=True) + EPS) * g


def _f_first(x, g1):
    return (_rms(x, g1),)


def _f_first_bwd(x, g1):
    return (x, _rms(x, g1))


def _f_mid(x, m, g2, g3):
    x1 = x + _rms(m, g2)
    return (x1, _rms(x1, g3))


def _f_end(x1, fo, g4, g1n):
    x2 = x1 + _rms(fo, g4)
    return (x2, _rms(x2, g1n))


def _f_last(x1, fo, g4):
    return (x1 + _rms(fo, g4),)


@jax.custom_vjp
def _swap_halves(x):
    c = x.shape[1]
    lane = lax.broadcasted_iota(jnp.int32, x.shape, 1) % RDH
    return jnp.where(lane < RDH // 2, pltpu.roll(x, c - RDH // 2, axis=1), pltpu.roll(x, RDH // 2, axis=1))


_swap_halves.defvjp(lambda x: (_swap_halves(x), None), lambda _, g: (_swap_halves(g),))


def _f_rprep(rq, rk, cos, sin):
    rot = lambda t: t * cos + _swap_halves(t) * sin
    return (rot(rq), rot(rk) * (RDH ** -0.5))


def _f_rpost(yf, yb, rg, gn, gavg):
    y = yf + yb
    mu = jnp.dot(y, gavg, precision=HI, preferred_element_type=f32)
    yc = y - mu
    var = jnp.dot(yc * yc, gavg, precision=HI, preferred_element_type=f32)
    return (jax.nn.silu(rg) * (yc * lax.rsqrt(var + EPS) * gn),)


def _f_sprep(xs, dtr, dtb, alog, ex0, ex1):
    dt = jax.nn.softplus(dtr + dtb)
    la = dt * (-jnp.exp(alog))
    e0 = jnp.dot(dt, ex0, precision=HI, preferred_element_type=f32)
    e1 = jnp.dot(dt, ex1, precision=HI, preferred_element_type=f32)
    return (xs * e0, xs * e1, la)


def _f_spost(yf, yb, xs, z, dsk, ng):
    y = (yf + yb + xs * dsk) * jax.nn.silu(z)
    y = y * lax.rsqrt(jnp.mean(y * y, axis=-1, keepdims=True) + EPS)
    return (y * ng,)


def _loss_call(y, tgt, *, tm=256):
    T = y.shape[0]

    def body(y_ref, t_ref, dy_ref, l_ref):
        i = pl.program_id(0)
        e = y_ref[...] - t_ref[...]
        dy_ref[...] = e * (1.0 / D)
        part = jnp.zeros((8, LANES), f32) + 0.5 * jnp.sum(jnp.mean(e * e, axis=-1, keepdims=True))

        @pl.when(i == 0)
        def _():
            l_ref[...] = part

        @pl.when(i > 0)
        def _():
            l_ref[...] += part

    return pl.pallas_call(
        body, name="loss_head", grid=(T // tm,),
        in_specs=[pl.BlockSpec((tm, D), lambda i: (i, 0))] * 2,
        out_specs=[pl.BlockSpec((tm, D), lambda i: (i, 0)), pl.BlockSpec((8, LANES), lambda i: (0, 0))],
        out_shape=[S((T, D), f32), S((8, LANES), f32)], compiler_params=_params(("arbitrary",)))(y, tgt)


def _colsum(x, *, name, tm=512):
    T, C = x.shape

    def body(x_ref, o_ref):
        i = pl.program_id(0)
        part = jnp.sum(x_ref[...], axis=0, keepdims=True)

        @pl.when(i == 0)
        def _():
            o_ref[...] = part

        @pl.when(i > 0)
        def _():
            o_ref[...] += part

    return pl.pallas_call(
        body, name=name, grid=(T // tm,), in_specs=[pl.BlockSpec((tm, C), lambda i: (i, 0))],
        out_specs=pl.BlockSpec((1, C), lambda i: (0, 0)), out_shape=S((1, C), f32),
        compiler_params=_params(("arbitrary",)))(x)


def _nn(a, b):
    return jnp.dot(a, b, preferred_element_type=f32)


def _nt(a, b):
    return lax.dot_general(a, b, NT, preferred_element_type=f32)


@jax.custom_vjp
def _mm_lt(a, a_t, b):
    return _nn(a_t, b)


_mm_lt.defvjp(lambda a, a_t, b: (_nn(a_t, b), (a, b)),
              lambda res, g: (jnp.zeros_like(res[0]), _nt(g, res[1]), _nn(res[0], g)))


@jax.custom_vjp
def _mm_rt(a, a_t, b):
    return _nn(a, b)


_mm_rt.defvjp(lambda a, a_t, b: (_nn(a, b), (a_t, b)),
              lambda res, g: (_nt(g, res[1]), jnp.zeros_like(res[0]), _nn(res[0], g)))


@jax.custom_vjp
def _masked_mm(s, s_t, d, d_t, v):
    return _nn(s * d, v)


def _masked_mm_bwd(res, g):
    s, s_t, d, d_t, v = res
    da = _nt(g, v)
    return (da * d, jnp.zeros_like(s_t), da * s, jnp.zeros_like(d_t), _nn(s_t * d_t, g))


_masked_mm.defvjp(lambda s, s_t, d, d_t, v: (_nn(s * d, v), (s, s_t, d, d_t, v)), _masked_mm_bwd)


def _scan_step(h, q, k, v, a, rev, for_vjp=False):
    L = q.shape[0]
    Hg = v.shape[0]
    ii = lax.broadcasted_iota(jnp.int32, (L, L), 0)
    jj = lax.broadcasted_iota(jnp.int32, (L, L), 1)
    if rev:
        tri, tri_t, dmask, dmask_t = (jj >= ii), (ii >= jj), (jj > ii), (ii > jj)
    else:
        tri, tri_t, dmask, dmask_t = (jj <= ii), (ii <= jj), (jj <= ii), (ii <= jj)
    cs = jnp.dot(tri.astype(f32), a, precision=HI, preferred_element_type=f32)
    cs_t = lax.dot_general(a, tri_t.astype(f32), TN, precision=HI, preferred_element_type=f32)
    tot = jnp.sum(a, axis=0, keepdims=True)
    qk = _nt(q, k)
    k_t = k.T
    if for_vjp:
        q_t = lax.stop_gradient(q.T)
        qk_t = lax.stop_gradient(_nt(k, q))
    hs, ys = [], []
    for hh in range(Hg):
        c_col, c_row = cs[:, hh:hh + 1], cs_t[hh:hh + 1, :]
        dec = jnp.exp(jnp.where(dmask, c_col - c_row, -1e30))
        t_all = tot[:, hh:hh + 1]
        w = v[hh] * jnp.exp(t_all - c_col)
        if for_vjp:
            dec_t = lax.stop_gradient(jnp.exp(jnp.where(dmask_t, c_row - c_col, -1e30)))
            y = _masked_mm(qk, qk_t, dec, dec_t, v[hh]) + _mm_rt(q, q_t, h[hh]) * jnp.exp(c_col)
            hn = h[hh] * jnp.exp(t_all) + _mm_lt(lax.stop_gradient(k), k_t, w)
        else:
            y = _nn(qk * dec, v[hh]) + _nn(q, h[hh]) * jnp.exp(c_col)
            hn = h[hh] * jnp.exp(t_all) + _nn(k_t, w)
        hs.append(hn)
        ys.append(y)
    return jnp.stack(hs), jnp.stack(ys)


def _scan_dims(a, N, P, T):
    G, Ha = a.shape[0], a.shape[2]
    return G, Ha, T // CHUNK


def _scan_specs(gb, N, Hg, P, Ha, cm, qcol, kcol, vcol):
    qs = lambda col: pl.BlockSpec((CHUNK, gb * N), lambda g, c: (cm(c), col + g))
    vs = lambda col: pl.BlockSpec((CHUNK, gb * Hg * P), lambda g, c: (cm(c), col + g))
    as_ = pl.BlockSpec((gb, CHUNK, Ha), lambda g, c: (g, cm(c), 0))
    hs = pl.BlockSpec((gb, 1, Hg, N, P), lambda g, c: (g, cm(c), 0, 0, 0))
    return qs(qcol), qs(kcol), vs(vcol), qs(0), vs(0), as_, hs


def _heads(ref, j, Hg, P):
    return jnp.stack([ref[:, (j * Hg + hh) * P:(j * Hg + hh + 1) * P] for hh in range(Hg)])


def _scan_fwd(q, k, v, a, *, N, Hg, P, qcol=0, kcol=0, vcol=0, rev, name):
    T = q.shape[0]
    G, Ha, NC = _scan_dims(a, N, P, T)
    gb = SCAN_HEADS_PER_STEP // Hg
    cm = (lambda c: NC - 1 - c) if rev else (lambda c: c)
    qs, ks, vs, _, ys, as_, hs = _scan_specs(gb, N, Hg, P, Ha, cm, qcol, kcol, vcol)

    def body(q_ref, k_ref, v_ref, a_ref, y_ref, hs_ref, h_scr):
        @pl.when(pl.program_id(1) == 0)
        def _():
            h_scr[...] = jnp.zeros_like(h_scr)

        for j in range(gb):
            h = h_scr[j]
            hs_ref[j, 0] = h
            hn, y = _scan_step(h, q_ref[:, j * N:(j + 1) * N], k_ref[:, j * N:(j + 1) * N], _heads(v_ref, j, Hg, P), a_ref[j], rev)
            for hh in range(Hg):
                y_ref[:, (j * Hg + hh) * P:(j * Hg + hh + 1) * P] = y[hh]
            h_scr[j] = hn

    return pl.pallas_call(
        body, name=name, grid=(G // gb, NC), in_specs=[qs, ks, vs, as_], out_specs=[ys, hs],
        out_shape=[S((T, G * Hg * P), f32), S((G, NC, Hg, N, P), f32)],
        scratch_shapes=[pltpu.VMEM((gb, Hg, N, P), f32)],
        compiler_params=_params(("parallel", "arbitrary")))(q, k, v, a)


def _scan_bwd(q, k, v, a, hsave, dy, *, N, Hg, P, qcol=0, kcol=0, vcol=0, rev, name):
    T = q.shape[0]
    G, Ha, NC = _scan_dims(a, N, P, T)
    gb = SCAN_HEADS_PER_STEP // Hg
    cm = (lambda c: c) if rev else (lambda c: NC - 1 - c)
    qs, ks, vs, dqs, dvs, as_, hs = _scan_specs(gb, N, Hg, P, Ha, cm, qcol, kcol, vcol)

    def body(q_ref, k_ref, v_ref, a_ref, hs_ref, dy_ref, dq_ref, dk_ref, dv_ref, da_ref, dh_scr):
        @pl.when(pl.program_id(1) == 0)
        def _():
            dh_scr[...] = jnp.zeros_like(dh_scr)

        for j in range(gb):
            cols = slice(j * N, (j + 1) * N)
            _, vjp = jax.vjp(functools.partial(_scan_step, rev=rev, for_vjp=True), hs_ref[j, 0], q_ref[:, cols], k_ref[:, cols],
                             _heads(v_ref, j, Hg, P), a_ref[j])
            dh, dq, dk, dv, da = vjp((dh_scr[j], _heads(dy_ref, j, Hg, P)))
            dq_ref[:, cols] = dq
            dk_ref[:, cols] = dk
            for hh in range(Hg):
                dv_ref[:, (j * Hg + hh) * P:(j * Hg + hh + 1) * P] = dv[hh]
            da_ref[j] = da
            dh_scr[j] = dh

    return pl.pallas_call(
        body, name=name, grid=(G // gb, NC), in_specs=[qs, ks, vs, as_, hs, dvs], out_specs=[dqs, dqs, dvs, as_],
        out_shape=[S((T, G * N), f32), S((T, G * N), f32), S((T, G * Hg * P), f32), S((G, T, Ha), f32)],
        scratch_shapes=[pltpu.VMEM((gb, Hg, N, P), f32)],
        compiler_params=_params(("parallel", "arbitrary")))(q, k, v, a, hsave, dy)


def _na_block_case(rb, nrb):
    return jnp.where(rb == 0, 0, jnp.where(rb == nrb - 1, 2, 1))


def _na_key_start(rb, rows):
    return pl.multiple_of(jnp.clip(rb * NA_QROWS - NA_WR // 2, 0, rows - NA_KROWS) * GRID_W, 256)


def _na_specs(T, nrb):
    nq, nk, wb = NA_QROWS * GRID_W, NA_KROWS * GRID_W, NA_PAIR * NADH
    qs = lambda col: pl.BlockSpec((nq, wb), lambda p, r: (r, col + p))
    fs = lambda col: pl.BlockSpec((T, wb), lambda p, r: (0, col + p))
    bs = pl.BlockSpec((NA_PAIR, 1, nq, nk), lambda p, r: (p, _na_block_case(r, nrb), 0, 0))
    ls = pl.BlockSpec((1, nq, NA_PAIR), lambda p, r: (p, r, 0))
    return qs, fs, bs, ls


def _na_fwd(q, k, v, bias, *, qcol, kcol, vcol, name):
    T = q.shape[0]
    rows = T // GRID_W
    nq, nk = NA_QROWS * GRID_W, NA_KROWS * GRID_W
    nrb = T // nq
    scale = NADH ** -0.5
    qs, fs, bs, ls = _na_specs(T, nrb)

    def body(q_ref, k_ref, v_ref, b_ref, o_ref, l_ref):
        ks = _na_key_start(pl.program_id(1), rows)
        for hh in range(NA_PAIR):
            sl = slice(hh * NADH, (hh + 1) * NADH)
            kw = k_ref[pl.ds(ks, nk), sl]
            vw = v_ref[pl.ds(ks, nk), sl]
            s = lax.dot_general(q_ref[:, sl], kw, NT, preferred_element_type=f32) * scale + b_ref[hh, 0]
            m = jnp.max(s, axis=1, keepdims=True)
            p = jnp.exp(s - m)
            l = jnp.sum(p, axis=1, keepdims=True)
            o_ref[:, sl] = jnp.dot(p.astype(bf16), vw, preferred_element_type=f32) / l
            l_ref[0, :, hh:hh + 1] = m + jnp.log(l)

    return pl.pallas_call(
        body, name=name, grid=(NAH // NA_PAIR, nrb), in_specs=[qs(qcol), fs(kcol), fs(vcol), bs],
        out_specs=[qs(0), ls], out_shape=[S((T, NAW), f32), S((NAH // NA_PAIR, T, NA_PAIR), f32)],
        compiler_params=_params(("parallel", "arbitrary")))(q, k, v, bias)


def _na_bwd(q, k, v, bias, o, lse, do, *, qcol, kcol, vcol, docol, name):
    T = q.shape[0]
    rows = T // GRID_W
    nq, nk = NA_QROWS * GRID_W, NA_KROWS * GRID_W
    nrb = T // nq
    scale = NADH ** -0.5
    qs, fs, bs, ls = _na_specs(T, nrb)

    def body(q_ref, k_ref, v_ref, b_ref, o_ref, l_ref, do_ref, dq_ref, dk_ref, dv_ref, db_ref):
        rb = pl.program_id(1)

        @pl.when(rb == 0)
        def _():
            dk_ref[...] = jnp.zeros_like(dk_ref)
            dv_ref[...] = jnp.zeros_like(dv_ref)

        ks = _na_key_start(rb, rows)
        first = (rb == 0) | (rb == 1) | (rb == nrb - 1)
        for hh in range(NA_PAIR):
            sl = slice(hh * NADH, (hh + 1) * NADH)
            qv = q_ref[:, sl]
            kw = k_ref[pl.ds(ks, nk), sl]
            vw = v_ref[pl.ds(ks, nk), sl]
            s = lax.dot_general(qv, kw, NT, preferred_element_type=f32) * scale + b_ref[hh, 0]
            p = jnp.exp(s - l_ref[0, :, hh:hh + 1])
            do_ = do_ref[:, sl]
            dob = do_.astype(bf16)
            dp = lax.dot_general(dob, vw, NT, preferred_element_type=f32)
            ds = p * (dp - jnp.sum(do_ * o_ref[:, sl], axis=1, keepdims=True))
            dsb = ds.astype(bf16)
            dq_ref[:, sl] = jnp.dot(dsb, kw, preferred_element_type=f32) * scale
            dk_ref[pl.ds(ks, nk), sl] += lax.dot_general(dsb, qv, TN, preferred_element_type=f32) * scale
            dv_ref[pl.ds(ks, nk), sl] += lax.dot_general(p.astype(bf16), dob, TN, preferred_element_type=f32)

            @pl.when(first)
            def _(hh=hh, ds=ds):
                db_ref[hh, 0] = ds

            @pl.when(jnp.logical_not(first))
            def _(hh=hh, ds=ds):
                db_ref[hh, 0] += ds

    return pl.pallas_call(
        body, name=name, grid=(NAH // NA_PAIR, nrb),
        in_specs=[qs(qcol), fs(kcol), fs(vcol), bs, qs(0), ls, qs(docol)],
        out_specs=[qs(0), fs(0), fs(0), bs],
        out_shape=[S((T, NAW), f32), S((T, NAW), f32), S((T, NAW), f32), S(bias.shape, f32)],
        compiler_params=_params(("parallel", "arbitrary")))(q, k, v, bias, o, lse, do)


def _na_bias_tables(rows):
    c = np.arange(GRID_W)[:, None]
    kc = np.arange(GRID_W)[None, :]
    cstart = np.clip(c - NA_WC // 2, 0, GRID_W - NA_WC)
    valid_c = (kc >= cstart) & (kc < cstart + NA_WC)
    dc = kc - c + NA_WC - 1
    E = (valid_c[:, :, None] & (dc[:, :, None] == np.arange(2 * NA_WC - 1)[None, None, :])).astype(np.float32)
    A = np.zeros((3, NA_QROWS, NA_KROWS, 2 * NA_WR - 1), np.float32)
    valid_r = np.zeros((3, NA_QROWS, NA_KROWS), bool)
    for z, r0 in enumerate((0, NA_QROWS, rows - NA_QROWS)):
        ks = int(np.clip(r0 - NA_WR // 2, 0, rows - NA_KROWS))
        for ri in range(NA_QROWS):
            r = r0 + ri
            rs = int(np.clip(r - NA_WR // 2, 0, rows - NA_WR))
            for kri in range(NA_KROWS):
                kr = ks + kri
                if rs <= kr < rs + NA_WR:
                    valid_r[z, ri, kri] = True
                    A[z, ri, kri, kr - r + NA_WR - 1] = 1.0
    mask = np.where(valid_r[:, :, None, :, None] & valid_c[None, None, :, None, :], 0.0, -1e30).astype(np.float32)
    return E, A, mask


def _na_bias(rpb, rows):
    E, A, mask = _na_bias_tables(rows)
    r1 = jnp.einsum("hde,cke->hdck", rpb, E, precision=HI)
    b = jnp.einsum("hdck,zabd->hzacbk", r1, A, precision=HI) + mask[None]
    return b.reshape(rpb.shape[0], 3, NA_QROWS * GRID_W, NA_KROWS * GRID_W)


def _conv_shifts(prev, cur, nxt, i, n_i, W):
    tm = cur.shape[0]
    prev = jnp.where(i > 0, prev, 0.0)
    nxt = jnp.where(i < n_i - 1, nxt, 0.0)
    ext = jnp.concatenate([prev, cur, nxt], axis=0)
    out = []
    for w in range(W):
        s = (W // 2 - w) % (tm + 16)
        out.append((ext if s == 0 else pltpu.roll(ext, s, axis=0))[8:8 + tm])
    return out


def _conv_act(u, mode):
    if mode == "silu":
        return jax.nn.silu(u)
    if mode == "geglu":
        half = u.shape[1] // 2
        return jax.nn.gelu(u[:, :half], approximate=True) * u[:, half:]
    return u


def _conv_specs(T, tm, tc, xbase):
    r8 = tm // 8
    last = T // 8 - 1
    cur = pl.BlockSpec((tm, tc), lambda j, i: (i, xbase + j))
    prev = pl.BlockSpec((8, tc), lambda j, i: (jnp.maximum(i * r8 - 1, 0), xbase + j))
    nxt = pl.BlockSpec((8, tc), lambda j, i: (jnp.minimum((i + 1) * r8, last), xbase + j))
    return cur, prev, nxt


def _conv(x, w8, b, *, mode, W, name, C, xbase=0, tm=512, tc=512, out_dtype=f32):
    T = x.shape[0]
    NI, J = T // tm, C // tc
    tco = tc // 2 if mode == "geglu" else tc
    cur, prev, nxt = _conv_specs(T, tm, tc, xbase)

    def body(xc, xp, xn, w_ref, b_ref, o_ref):
        sh = _conv_shifts(xp[...].astype(f32), xc[...].astype(f32), xn[...].astype(f32), pl.program_id(1), NI, W)
        wv = w_ref[...]
        u = sh[0] * wv[0:1, :]
        for w in range(1, W):
            u = u + sh[w] * wv[w:w + 1, :]
        if mode != "none":
            u = u + b_ref[...]
        o_ref[...] = _conv_act(u, mode).astype(o_ref.dtype)

    return pl.pallas_call(
        body, name=name, grid=(J, NI),
        in_specs=[cur, prev, nxt, pl.BlockSpec((8, tc), lambda j, i: (0, j)), pl.BlockSpec((1, tc), lambda j, i: (0, j))],
        out_specs=pl.BlockSpec((tm, tco), lambda j, i: (i, j)), out_shape=S((T, J * tco), out_dtype),
        compiler_params=_params(("parallel", "parallel")))(x, x, x, w8, b)


def _conv_bwd(x, w8, b, dact, *, mode, W, name, C, xbase=0, tm=512, tc=512):
    T = x.shape[0]
    NI, J = T // tm, C // tc
    tco = tc // 2 if mode == "geglu" else tc
    rows = tm + 16
    pad = W // 2
    cur, prev, nxt = _conv_specs(T, tm, tc, xbase)
    dcur, dprev, dnxt = _conv_specs(T, tm, tco, 0)

    def body(xc, xp, xn, w_ref, b_ref, dc, dp, dn, dx_ref, dw_ref, db_ref):
        i = pl.program_id(1)
        ext = jnp.concatenate([jnp.where(i > 0, xp[...], 0.0), xc[...], jnp.where(i < NI - 1, xn[...], 0.0)], axis=0)
        dext = jnp.concatenate([jnp.where(i > 0, dp[...], 0.0), dc[...], jnp.where(i < NI - 1, dn[...], 0.0)], axis=0)
        wv = w_ref[...]
        shift = lambda t, w: t if w == pad else pltpu.roll(t, (pad - w) % rows, axis=0)
        xs = [shift(ext, w) for w in range(W)]
        u = b_ref[...] + xs[0] * wv[0:1, :]
        for w in range(1, W):
            u = u + xs[w] * wv[w:w + 1, :]
        _, vjp = jax.vjp(functools.partial(_conv_act, mode=mode), u)
        du = vjp(dext.astype(f32))[0]
        dx = shift(du, 0)[8:8 + tm] * wv[W - 1:W, :]
        for w in range(1, W):
            dx = dx + shift(du, w)[8:8 + tm] * wv[W - 1 - w:W - w, :]
        dx_ref[...] = dx.astype(dx_ref.dtype)

        @pl.when(i == 0)
        def _():
            dw_ref[...] = jnp.zeros_like(dw_ref)
            db_ref[...] = jnp.zeros_like(db_ref)

        dum = du[8:8 + tm]
        db_ref[...] += jnp.sum(dum, axis=0, keepdims=True)
        for w in range(W):
            dw_ref[w:w + 1, :] += jnp.sum(dum * xs[w][8:8 + tm], axis=0, keepdims=True)

    return pl.pallas_call(
        body, name=name, grid=(J, NI),
        in_specs=[cur, prev, nxt, pl.BlockSpec((8, tc), lambda j, i: (0, j)), pl.BlockSpec((1, tc), lambda j, i: (0, j)),
                  dcur, dprev, dnxt],
        out_specs=[pl.BlockSpec((tm, tc), lambda j, i: (i, j)), pl.BlockSpec((8, tc), lambda j, i: (0, j)),
                   pl.BlockSpec((1, tc), lambda j, i: (0, j))],
        out_shape=[S((T, C), bf16), S((8, C), f32), S((1, C), f32)],
        compiler_params=_params(("parallel", "arbitrary")))(x, x, x, w8, b, dact, dact, dact)


def _pad8(w):
    return jnp.concatenate([w, jnp.zeros((8 - w.shape[0], w.shape[1]), w.dtype)], axis=0)


def _exchange(arrs, *, name):
    n = len(arrs)
    ncopy = (NDEV - 1) * n

    def body(*refs):
        ins, outs = refs[:n], refs[n:2 * n]
        send_sems, recv_sems, loc_sems = refs[2 * n:]
        x, y, c = lax.axis_index("x"), lax.axis_index("y"), lax.axis_index("c")
        me = 4 * x + 2 * y + c

        def src(a, p):
            return ins[a].at[p] if arrs[a][1] else ins[a]

        local = [pltpu.make_async_copy(src(a, me), outs[a].at[me], loc_sems.at[a]) for a in range(n)]
        for cp in local:
            cp.start()
        sent = []
        for kk in range(1, NDEV):
            px = 1 - x if kk & 4 else x
            py = 1 - y if kk & 2 else y
            pc = 1 - c if kk & 1 else c
            peer = 4 * px + 2 * py + pc
            for a in range(n):
                idx = (kk - 1) * n + a
                mk = lambda dst_slot, a=a, idx=idx, peer=peer, dev=(px, py, pc): pltpu.make_async_remote_copy(
                    src_ref=src(a, peer), dst_ref=outs[a].at[dst_slot], send_sem=send_sems.at[idx], recv_sem=recv_sems.at[idx],
                    device_id=dev, device_id_type=pl.DeviceIdType.MESH)
                mk(me).start()
                sent.append((mk, peer))
        for mk, peer in sent:
            mk(peer).wait_recv()
        for mk, peer in sent:
            mk(peer).wait_send()
        for cp in local:
            cp.wait()

    any_spec = pl.BlockSpec(memory_space=pl.ANY)
    return pl.pallas_call(
        body, name=name, in_specs=[any_spec] * n, out_specs=[any_spec] * n,
        out_shape=[S(a.shape if pp else (NDEV,) + a.shape, a.dtype) for a, pp in arrs],
        scratch_shapes=[pltpu.SemaphoreType.DMA((ncopy,)), pltpu.SemaphoreType.DMA((ncopy,)), pltpu.SemaphoreType.DMA((n,))],
        )(*[a for a, _ in arrs])


def _adamw(r, w, m, v, *, name, tr):
    M, C = w.shape

    def body(r_ref, w_ref, m_ref, v_ref, g_ref, d_ref, nm_ref, nv_ref):
        g = r_ref[0].astype(f32)
        for s in range(1, NDEV):
            g = g + r_ref[s].astype(f32)
        m_ = B1 * m_ref[...] + (1.0 - B1) * g
        v_ = B2 * v_ref[...] + (1.0 - B2) * jnp.square(g)
        m_hat = m_ / (1.0 - B1 ** STEP)
        v_hat = v_ / (1.0 - B2 ** STEP)
        g_ref[...] = g
        d_ref[...] = -LR * (m_hat / (jnp.sqrt(v_hat) + AEPS) + WD * w_ref[...])
        nm_ref[...] = m_
        nv_ref[...] = v_

    row = pl.BlockSpec((tr, C), lambda i: (i, 0))
    return pl.pallas_call(
        body, name=name, grid=(M // tr,),
        in_specs=[pl.BlockSpec((NDEV, tr, C), lambda i: (0, i, 0)), row, row, row],
        out_specs=[row] * 4, out_shape=[S((M, C), f32)] * 4, compiler_params=_params(("parallel",)))(r, w, m, v)


def _colmove(ins, in_slots, outs, moves, *, tk, name):
    R = ins[0].shape[1] if in_slots[0] else ins[0].shape[0]
    n_in = len(ins)

    def body(*refs):
        for ii, isl, ic, oi, osl, oc, w in moves:
            src, dst = refs[ii], refs[n_in + oi]
            val = src[:, ic:ic + w] if isl is None else src[isl, :, ic:ic + w]
            if osl is None:
                dst[:, oc:oc + w] = val.astype(dst.dtype)
            else:
                dst[osl, :, oc:oc + w] = val.astype(dst.dtype)

    def spec(is_slots, C):
        return pl.BlockSpec((NDEV, tk, C), lambda i: (0, i, 0)) if is_slots else pl.BlockSpec((tk, C), lambda i: (i, 0))

    return pl.pallas_call(
        body, name=name, grid=(R // tk,),
        in_specs=[spec(sl, a.shape[-1]) for a, sl in zip(ins, in_slots)],
        out_specs=[spec(sl, C) for sl, C, _ in outs],
        out_shape=[S((NDEV, R, C) if sl else (R, C), dt) for sl, C, dt in outs],
        compiler_params=_params(("parallel",)))(*ins)


def _col_pieces(n8, cuts, place):
    out = []
    for p in range(NDEV):
        lo, hi = p * n8, (p + 1) * n8
        edges = [lo] + [c for c in cuts if lo < c < hi] + [hi]
        for a, b in zip(edges[:-1], edges[1:]):
            out.append((p, a - lo) + place(a) + (b - a,))
    return out


def _place_plain(c):
    return (0, c)


def _place_ssd_in(c):
    return (0, c) if c < SSD_INNER + SSD_XBC else (1, c - (SSD_INNER + SSD_XBC))


def _place_ffn_up(c):
    h = FFN_TC // 2
    return (0, (c // h) * FFN_TC + c % h) if c < FFN else (0, ((c - FFN) // h) * FFN_TC + h + (c - FFN) % h)


_COL_LAYOUTS = {
    "ab_w_in": ([], _place_plain, [4 * RW + 3 * NAW]),
    "c_w_in": ([SSD_INNER + SSD_XBC], _place_ssd_in, [SSD_INNER + SSD_XBC, 2 * SSD_H]),
    "ffn_w_up": (list(range(FFN_TC // 2, 2 * FFN, FFN_TC // 2)), _place_ffn_up, [2 * FFN]),
}


def _cols_from_slots(g, which, *, name):
    cuts, place, widths = _COL_LAYOUTS[which]
    moves = [(0, p, sc, mi, None, mc, w) for p, sc, mi, mc, w in _col_pieces(g.shape[2], cuts, place)]
    return _colmove([g], [True], [(False, w, g.dtype) for w in widths], moves, tk=256, name=name)


def _cols_to_slots(mats, which, dtype, *, name):
    cuts, place, widths = _COL_LAYOUTS[which]
    n8 = sum(widths) // NDEV
    moves = [(mi, None, mc, 0, p, sc, w) for p, sc, mi, mc, w in _col_pieces(n8, cuts, place)]
    return _colmove(list(mats), [False] * len(mats), [(True, n8, dtype)], moves, tk=256, name=name)[0]


def _tm2hm(a, H):
    T = a.shape[0]
    return a.reshape(T, H, -1).transpose(1, 0, 2)


def _hm2tm(a):
    H, T, P = a.shape
    return a.transpose(1, 0, 2).reshape(T, H * P)


def _pack(parts, dtype, row_mult):
    flat = jnp.concatenate([p.reshape(-1).astype(dtype) for p in parts])
    rows = -(-flat.shape[0] // LANES)
    rows = -(-rows // row_mult) * row_mult
    return jnp.pad(flat, (0, rows * LANES - flat.shape[0])).reshape(rows, LANES)


def _unpack(buf, shapes, lead=()):
    flat = buf.reshape(lead + (-1,))
    out, off = [], 0
    for shp in shapes:
        n = int(np.prod(shp))
        out.append(flat[..., off:off + n].reshape(lead + tuple(shp)))
        off += n
    return out


def _to_slots(full, ax):
    shp = full.shape
    return jnp.moveaxis(full.reshape(shp[:ax] + (NDEV, shp[ax] // NDEV) + shp[ax + 1:]), ax, 0)


def _from_slots(g, ax):
    t = jnp.moveaxis(g, 0, ax)
    shp = t.shape
    return t.reshape(shp[:ax] + (shp[ax] * shp[ax + 1],) + shp[ax + 2:])


def _ffn_perm(a):
    lead = a.shape[:-1]
    h = FFN_TC // 2
    return jnp.swapaxes(a.reshape(lead + (2, FFN // h, h)), -3, -2).reshape(lead + (2 * FFN,))


def _ffn_unperm(a):
    lead = a.shape[:-1]
    h = FFN_TC // 2
    return jnp.swapaxes(a.reshape(lead + (FFN // h, 2, h)), -3, -2).reshape(lead + (2 * FFN,))


def _rope_tables(T):
    half = RDH // 2
    inv = 1.0 / (ROPE_BASE ** (jnp.arange(half, dtype=f32) / half))
    ang = jnp.arange(T, dtype=f32)[:, None] * inv[None, :]
    cos, sin = jnp.cos(ang), jnp.sin(ang)
    cos_t = jnp.tile(jnp.concatenate([cos, cos], axis=1), (1, RH))
    sin_t = jnp.tile(jnp.concatenate([-sin, sin], axis=1), (1, RH))
    return cos_t, sin_t


def _group_avg():
    g = np.arange(RW) // RDH
    return jnp.asarray((g[:, None] == g[None, :]).astype(np.float32) / RDH)


def _head_expand():
    hd = np.arange(SSD_INNER) // SSD_HD
    rows = np.arange(2 * SSD_H)
    ex0 = (rows[:, None] == hd[None, :]).astype(np.float32)
    ex1 = (rows[:, None] == SSD_H + hd[None, :]).astype(np.float32)
    return jnp.asarray(ex0), jnp.asarray(ex1)


def kernel(x, norm_mix_pre, norm_mix_post, norm_ffn_pre, norm_ffn_post, ab_w_in, ab_ret_decay_logit, ab_ret_gn_g, ab_na_rpb, ab_w_out, c_w_in, c_conv_w, c_conv_b, c_dt_bias, c_a_log, c_d_skip, c_norm_g, c_w_out, ffn_w_up, ffn_conv_w, ffn_conv_b, ffn_w_down, loss_target, m_norm_mix_pre, m_norm_mix_post, m_norm_ffn_pre, m_norm_ffn_post, m_ab_w_in, m_ab_ret_decay_logit, m_ab_ret_gn_g, m_ab_na_rpb, m_ab_w_out, m_c_w_in, m_c_conv_w, m_c_conv_b, m_c_dt_bias, m_c_a_log, m_c_d_skip, m_c_norm_g, m_c_w_out, m_ffn_w_up, m_ffn_conv_w, m_ffn_conv_b, m_ffn_w_down, v_norm_mix_pre, v_norm_mix_post, v_norm_ffn_pre, v_norm_ffn_post, v_ab_w_in, v_ab_ret_decay_logit, v_ab_ret_gn_g, v_ab_na_rpb, v_ab_w_out, v_c_w_in, v_c_conv_w, v_c_conv_b, v_c_dt_bias, v_c_a_log, v_c_d_skip, v_c_norm_g, v_c_w_out, v_ffn_w_up, v_ffn_conv_w, v_ffn_conv_b, v_ffn_w_down):
    W = dict(norm_mix_pre=norm_mix_pre, norm_mix_post=norm_mix_post, norm_ffn_pre=norm_ffn_pre, norm_ffn_post=norm_ffn_post, ab_w_in=ab_w_in, ab_ret_decay_logit=ab_ret_decay_logit, ab_ret_gn_g=ab_ret_gn_g, ab_na_rpb=ab_na_rpb, ab_w_out=ab_w_out, c_w_in=c_w_in, c_conv_w=c_conv_w, c_conv_b=c_conv_b, c_dt_bias=c_dt_bias, c_a_log=c_a_log, c_d_skip=c_d_skip, c_norm_g=c_norm_g, c_w_out=c_w_out, ffn_w_up=ffn_w_up, ffn_conv_w=ffn_conv_w, ffn_conv_b=ffn_conv_b, ffn_w_down=ffn_w_down)
    Mo = dict(norm_mix_pre=m_norm_mix_pre, norm_mix_post=m_norm_mix_post, norm_ffn_pre=m_norm_ffn_pre, norm_ffn_post=m_norm_ffn_post, ab_w_in=m_ab_w_in, ab_ret_decay_logit=m_ab_ret_decay_logit, ab_ret_gn_g=m_ab_ret_gn_g, ab_na_rpb=m_ab_na_rpb, ab_w_out=m_ab_w_out, c_w_in=m_c_w_in, c_conv_w=m_c_conv_w, c_conv_b=m_c_conv_b, c_dt_bias=m_c_dt_bias, c_a_log=m_c_a_log, c_d_skip=m_c_d_skip, c_norm_g=m_c_norm_g, c_w_out=m_c_w_out, ffn_w_up=m_ffn_w_up, ffn_conv_w=m_ffn_conv_w, ffn_conv_b=m_ffn_conv_b, ffn_w_down=m_ffn_w_down)
    Vo = dict(norm_mix_pre=v_norm_mix_pre, norm_mix_post=v_norm_mix_post, norm_ffn_pre=v_norm_ffn_pre, norm_ffn_post=v_norm_ffn_post, ab_w_in=v_ab_w_in, ab_ret_decay_logit=v_ab_ret_decay_logit, ab_ret_gn_g=v_ab_ret_gn_g, ab_na_rpb=v_ab_na_rpb, ab_w_out=v_ab_w_out, c_w_in=v_c_w_in, c_conv_w=v_c_conv_w, c_conv_b=v_c_conv_b, c_dt_bias=v_c_dt_bias, c_a_log=v_c_a_log, c_d_skip=v_c_d_skip, c_norm_g=v_c_norm_g, c_w_out=v_c_w_out, ffn_w_up=v_ffn_w_up, ffn_conv_w=v_ffn_conv_w, ffn_conv_b=v_ffn_conv_b, ffn_w_down=v_ffn_w_down)
    return _train_step(x[0], loss_target[0], W, Mo, Vo)


def _train_step(x, tgt, W, Mo, Vo):
    T = x.shape[0]
    rows = T // GRID_W

    col = lambda d, n, dt: d[n].reshape(-1, d[n].shape[-1]).astype(dt)
    rows_of = lambda d, dt: jnp.concatenate([col(d, n, dt) for n in ROW_SHARDED], axis=0)
    small = _pack([W[n] for n, _ in SHARDED[N_BIG:]], f32, 8)
    gat = _exchange([(col(W, n, bf16), False) for n in COL_SHARDED] + [(rows_of(W, bf16), False), (small, False)],
                    name="gather_weights")
    per_layer = lambda m: m.reshape(-1, D, m.shape[-1])
    w_ab_in = per_layer(_cols_from_slots(gat[0], "ab_w_in", name="cols_ab_w_in")[0])
    w_zx, w_dt = [per_layer(m) for m in _cols_from_slots(gat[1], "c_w_in", name="cols_c_w_in")]
    w_up = per_layer(_cols_from_slots(gat[2], "ffn_w_up", name="cols_ffn_w_up")[0])
    full, off = {}, 0
    for n in ROW_SHARDED:
        L, r = W[n].shape[0], W[n].shape[1]
        full[n] = jnp.swapaxes(gat[3][:, off:off + L * r].reshape(NDEV, L, r, D), 0, 1).reshape(L, NDEV * r, D)
        off += L * r
    gs = _unpack(gat[4], [W[n].shape for n, _ in SHARDED[N_BIG:]], (NDEV,))
    full.update({n: _from_slots(g, ax) for (n, ax), g in zip(SHARDED[N_BIG:], gs)})
    w_ab_out, w_c_out, w_down = full["ab_w_out"], full["c_w_out"], full["ffn_w_down"]
    c_cw8 = [_pad8(full["c_conv_w"][i]) for i in range(2)]
    c_cb = [full["c_conv_b"][i][None] for i in range(2)]
    c_ng = [full["c_norm_g"][i][None] for i in range(2)]
    f_cw8 = [_pad8(_ffn_perm(full["ffn_conv_w"][l])) for l in range(DEPTH)]
    f_cb = [_ffn_perm(W["ffn_conv_b"][l])[None] for l in range(DEPTH)]

    g1 = [W["norm_mix_pre"][l][None] for l in range(DEPTH)]
    g2 = [W["norm_mix_post"][l][None] for l in range(DEPTH)]
    g3 = [W["norm_ffn_pre"][l][None] for l in range(DEPTH)]
    g4 = [W["norm_ffn_post"][l][None] for l in range(DEPTH)]
    cos_t, sin_t = _rope_tables(T)
    gavg = _group_avg()
    ex0, ex1 = _head_expand()

    def log_gamma(logit):
        return -jax.nn.softplus(-logit)

    def ret_decays(lg):
        return [jnp.broadcast_to(lg[d][:, None, None], (RH, T, 8)) for d in range(2)]

    saved = []
    xs_ = x
    hn = _rowwise("norm_first", _f_first, [(x, D, 0)], [], [(g1[0], D, 0)], [], [(D, bf16)], tm=256)[0]
    for l in range(DEPTH):
        i = l // 2
        sv = dict(x=xs_, hn=hn)
        if l % 2 == 0:
            proj = _mm_nn(hn, w_ab_in[i], name=f"ab_in_{l}")
            qr, kr = _rowwise(f"ret_prep_{l}", _f_rprep, [(proj, RW, 0), (proj, RW, 1)], [(cos_t, RW, 0), (sin_t, RW, 0)], [], [],
                              [(RW, f32), (RW, f32)], tm=256)
            lg, lg_vjp = jax.vjp(log_gamma, W["ab_ret_decay_logit"][i])
            a_f, a_b = ret_decays(lg)
            rscan = dict(N=RDH, Hg=1, P=RDH, vcol=2)
            yf_t, hsf = _scan_fwd(qr, kr, proj, a_f, rev=False, name=f"ret_scan_f_{l}", **rscan)
            yb_t, hsb = _scan_fwd(qr, kr, proj, a_b, rev=True, name=f"ret_scan_b_{l}", **rscan)
            gn = W["ab_ret_gn_g"][i][None]
            ret = _rowwise(f"ret_post_{l}", _f_rpost, [(yf_t, RW, 0), (yb_t, RW, 0), (proj, RW, 3)], [], [(gn, RW, 0)], [gavg],
                           [(RW, bf16)], tm=256)[0]
            nqkv = proj[:, 4 * RW:].astype(bf16)
            ncols = dict(qcol=0, kcol=NAW // 128, vcol=2 * NAW // 128)
            bias, bias_vjp = jax.vjp(functools.partial(_na_bias, rows=rows), W["ab_na_rpb"][i])
            na_o, na_l = _na_fwd(nqkv, nqkv, nqkv, bias, name=f"na_fwd_{l}", **ncols)
            cat = jnp.concatenate([ret, na_o.astype(bf16)], axis=1)
            mo = _mm_nn(cat, w_ab_out[i], name=f"ab_out_{l}")
            sv.update(proj=proj, qr=qr, kr=kr, a_f=a_f, a_b=a_b, hsf=hsf, hsb=hsb, yf_t=yf_t, yb_t=yb_t, gn=gn, rscan=rscan,
                      nqkv=nqkv, ncols=ncols, bias=bias, bias_vjp=bias_vjp, lg_vjp=lg_vjp, na_o=na_o, na_l=na_l, cat=cat)
        else:
            zx = _mm_nn(hn, w_zx[i], name=f"c_in_{l}")
            dtr = _mm_nn(hn, w_dt[i], name=f"c_in_dt_{l}")
            xa = _conv(zx, c_cw8[i], c_cb[i], mode="silu", W=SSD_CONV, name=f"c_conv_{l}", C=SSD_XBC, xbase=SSD_INNER // 512)
            dtb, alog = W["c_dt_bias"][i].reshape(1, 2 * SSD_H), W["c_a_log"][i].reshape(1, 2 * SSD_H)
            vf, vb, la = _rowwise(f"ssd_prep_{l}", _f_sprep, [(xa, SSD_INNER, 0), (dtr, 2 * SSD_H, 0)], [],
                                  [(dtb, 2 * SSD_H, 0), (alog, 2 * SSD_H, 0)], [ex0, ex1],
                                  [(SSD_INNER, f32), (SSD_INNER, f32), (2 * SSD_H, f32)], tm=128)
            a_f = la[:, :SSD_H].reshape(T, SSD_G, SSD_HPG).transpose(1, 0, 2)
            a_b = la[:, SSD_H:].reshape(T, SSD_G, SSD_HPG).transpose(1, 0, 2)
            sscan = dict(N=SSD_N, Hg=SSD_HPG, P=SSD_HD, qcol=(SSD_INNER + SSD_G * SSD_N) // SSD_N, kcol=SSD_INNER // SSD_N)
            yf_t, hsf = _scan_fwd(xa, xa, vf, a_f, rev=False, name=f"ssd_scan_f_{l}", **sscan)
            yb_t, hsb = _scan_fwd(xa, xa, vb, a_b, rev=True, name=f"ssd_scan_b_{l}", **sscan)
            dsk = jnp.repeat(W["c_d_skip"][i], SSD_HD)[None]
            yo = _rowwise(f"ssd_post_{l}", _f_spost, [(yf_t, 512, 0), (yb_t, 512, 0), (xa, 512, 0), (zx, 512, 0)], [],
                          [(dsk, 512, 0), (c_ng[i], 512, 0)], [], [(512, bf16)], tm=256, J=SSD_G)[0]
            mo = _mm_nn(yo, w_c_out[i], name=f"c_out_{l}")
            sv.update(zx=zx, dtr=dtr, xa=xa, dtb=dtb, alog=alog, a_f=a_f, a_b=a_b, vf=vf, vb=vb, sscan=sscan,
                      hsf=hsf, hsb=hsb, yf_t=yf_t, yb_t=yb_t, dsk=dsk, yo=yo)
        x1, hf = _rowwise(f"norm_mid_{l}", _f_mid, [(xs_, D, 0), (mo, D, 0)], [], [(g2[l], D, 0), (g3[l], D, 0)], [],
                          [(D, f32), (D, bf16)], tm=256)
        pre = _mm_nn(hf, w_up[l], name=f"ffn_up_{l}")
        act = _conv(pre, f_cw8[l], f_cb[l], mode="geglu", W=FFN_CONV, name=f"ffn_conv_{l}", C=2 * FFN, tc=FFN_TC, out_dtype=bf16)
        fo = _mm_nn(act, w_down[l], name=f"ffn_down_{l}")
        sv.update(mo=mo, x1=x1, hf=hf, pre=pre, act=act, fo=fo)
        if l < DEPTH - 1:
            xs_, hn = _rowwise(f"norm_end_{l}", _f_end, [(x1, D, 0), (fo, D, 0)], [], [(g4[l], D, 0), (g1[l + 1], D, 0)], [],
                               [(D, f32), (D, bf16)], tm=256)
        else:
            xs_ = _rowwise(f"norm_end_{l}", _f_last, [(x1, D, 0), (fo, D, 0)], [], [(g4[l], D, 0)], [], [(D, f32)], tm=256)[0]
        saved.append(sv)

    dx, lpart = _loss_call(xs_, tgt)
    loss = lax.psum(lpart[0, 0], ("x", "y", "c"))

    G = {n: [None] * W[n].shape[0] for n in WEIGHTS}
    dhn = None
    for l in reversed(range(DEPTH)):
        i = l // 2
        sv = saved[l]
        if l == DEPTH - 1:
            (dx1, dfo), (dg4,) = _rowwise_bwd(f"norm_end_bwd_{l}", _f_last, [(sv["x1"], D, 0), (sv["fo"], D, 0)], [],
                                              [(g4[l], D, 0)], [], [(dx, D, 0)], [f32, bf16], tm=256)
        else:
            (dx1, dfo), (dg4, dg1n) = _rowwise_bwd(f"norm_end_bwd_{l}", _f_end, [(sv["x1"], D, 0), (sv["fo"], D, 0)], [],
                                                   [(g4[l], D, 0), (g1[l + 1], D, 0)], [], [(dx, D, 0), (dhn, D, 0)],
                                                   [f32, bf16], tm=256)
            G["norm_mix_pre"][l + 1] = dg1n[0]
        G["norm_ffn_post"][l] = dg4[0]
        dact = _mm_nt(dfo, w_down[l], name=f"ffn_down_dx_{l}")
        G["ffn_w_down"][l] = _mm_tn(sv["act"], dfo, name=f"ffn_down_dw_{l}")
        dpre, dfw, dfb = _conv_bwd(sv["pre"], f_cw8[l], f_cb[l], dact, mode="geglu", W=FFN_CONV, name=f"ffn_conv_bwd_{l}",
                                   C=2 * FFN, tc=FFN_TC)
        dhf = _mm_nt(dpre, w_up[l], name=f"ffn_up_dx_{l}")
        G["ffn_w_up"][l] = _cols_to_slots([_mm_tn(sv["hf"], dpre, name=f"ffn_up_dw_{l}")], "ffn_w_up", bf16, name=f"slots_ffn_up_{l}")
        G["ffn_conv_w"][l] = _ffn_unperm(dfw[:FFN_CONV])
        G["ffn_conv_b"][l] = _ffn_unperm(dfb[0])
        (dxl, dmo), (dg2, dg3) = _rowwise_bwd(f"norm_mid_bwd_{l}", _f_mid, [(sv["x"], D, 0), (sv["mo"], D, 0)], [],
                                              [(g2[l], D, 0), (g3[l], D, 0)], [], [(dx1, D, 0), (dhf, D, 0)], [f32, bf16], tm=256)
        G["norm_mix_post"][l] = dg2[0]
        G["norm_ffn_pre"][l] = dg3[0]
        if l % 2 == 0:
            dcat = _mm_nt(dmo, w_ab_out[i], name=f"ab_out_dx_{l}")
            G["ab_w_out"][i] = _mm_tn(sv["cat"], dmo, name=f"ab_out_dw_{l}")
            (dyf, _, drg), (dgn,) = _rowwise_bwd(
                f"ret_post_bwd_{l}", _f_rpost, [(sv["yf_t"], RW, 0), (sv["yb_t"], RW, 0), (sv["proj"], RW, 3)], [],
                [(sv["gn"], RW, 0)], [gavg], [(dcat, RW, 0)], [f32, f32, bf16], tm=256)
            G["ab_ret_gn_g"][i] = dgn[0]
            dqf, dkf, dvf, daf = _scan_bwd(sv["qr"], sv["kr"], sv["proj"], sv["a_f"], sv["hsf"], dyf, rev=False,
                                           name=f"ret_scan_f_bwd_{l}", **sv["rscan"])
            dqb, dkb, dvb, dab = _scan_bwd(sv["qr"], sv["kr"], sv["proj"], sv["a_b"], sv["hsb"], dyf, rev=True,
                                           name=f"ret_scan_b_bwd_{l}", **sv["rscan"])
            dq_t, dk_t, drv = dqf + dqb, dkf + dkb, (dvf + dvb).astype(bf16)
            (drq, drk), _ = _rowwise_bwd(f"ret_prep_bwd_{l}", _f_rprep, [(sv["proj"], RW, 0), (sv["proj"], RW, 1)],
                                         [(cos_t, RW, 0), (sin_t, RW, 0)], [], [], [(dq_t, RW, 0), (dk_t, RW, 0)], [bf16, bf16], tm=256)
            da_cols = jnp.concatenate([_hm2tm(daf), _hm2tm(dab)], axis=1)
            dlg = _colsum(da_cols, name=f"ret_decay_sum_{l}").reshape(2, RH, 8)[:, :, 0]
            G["ab_ret_decay_logit"][i] = sv["lg_vjp"](dlg)[0]
            dnq, dnk, dnv, dbias = _na_bwd(sv["nqkv"], sv["nqkv"], sv["nqkv"], sv["bias"], sv["na_o"], sv["na_l"], dcat,
                                           docol=RW // 128, name=f"na_bwd_{l}", **sv["ncols"])
            G["ab_na_rpb"][i] = sv["bias_vjp"](dbias)[0]
            dproj = jnp.concatenate([drq, drk, drv, drg] + [t.astype(bf16) for t in (dnq, dnk, dnv)], axis=1)
            dhn = _mm_nt(dproj, w_ab_in[i], name=f"ab_in_dx_{l}")
            G["ab_w_in"][i] = _cols_to_slots([_mm_tn(sv["hn"], dproj, name=f"ab_in_dw_{l}")], "ab_w_in", bf16, name=f"slots_ab_in_{l}")
        else:
            dyo = _mm_nt(dmo, w_c_out[i], name=f"c_out_dx_{l}")
            G["c_w_out"][i] = _mm_tn(sv["yo"], dmo, name=f"c_out_dw_{l}")
            (dyf, _, dxs1, dz), (ddsk, dng) = _rowwise_bwd(
                f"ssd_post_bwd_{l}", _f_spost, [(sv["yf_t"], 512, 0), (sv["yb_t"], 512, 0), (sv["xa"], 512, 0), (sv["zx"], 512, 0)],
                [], [(sv["dsk"], 512, 0), (c_ng[i], 512, 0)], [], [(dyo, 512, 0)], [f32, f32, f32, bf16], tm=256, J=SSD_G)
            G["c_d_skip"][i] = ddsk.reshape(SSD_H, SSD_HD).sum(axis=1)
            G["c_norm_g"][i] = dng[0]
            dqf, dkf, dvf, daf = _scan_bwd(sv["xa"], sv["xa"], sv["vf"], sv["a_f"], sv["hsf"], dyf, rev=False,
                                           name=f"ssd_scan_f_bwd_{l}", **sv["sscan"])
            dqb, dkb, dvb, dab = _scan_bwd(sv["xa"], sv["xa"], sv["vb"], sv["a_b"], sv["hsb"], dyf, rev=True,
                                           name=f"ssd_scan_b_bwd_{l}", **sv["sscan"])
            dla = jnp.concatenate([daf.transpose(1, 0, 2).reshape(T, SSD_H), dab.transpose(1, 0, 2).reshape(T, SSD_H)], axis=1)
            (dxs2, ddtr), (ddtb, dalog) = _rowwise_bwd(
                f"ssd_prep_bwd_{l}", _f_sprep, [(sv["xa"], SSD_INNER, 0), (sv["dtr"], 2 * SSD_H, 0)], [],
                [(sv["dtb"], 2 * SSD_H, 0), (sv["alog"], 2 * SSD_H, 0)], [ex0, ex1],
                [(dvf, SSD_INNER, 0), (dvb, SSD_INNER, 0), (dla, 2 * SSD_H, 0)], [f32, bf16], tm=128)
            G["c_dt_bias"][i] = ddtb.reshape(2, SSD_H)
            G["c_a_log"][i] = dalog.reshape(2, SSD_H)
            dxa = jnp.concatenate([dxs1 + dxs2, dkf + dkb, dqf + dqb], axis=1)
            dxbc, dcw, dcb = _conv_bwd(sv["zx"], c_cw8[i], c_cb[i], dxa, mode="silu", W=SSD_CONV, name=f"c_conv_bwd_{l}",
                                       C=SSD_XBC, xbase=SSD_INNER // 512)
            G["c_conv_w"][i] = dcw[:SSD_CONV]
            G["c_conv_b"][i] = dcb[0]
            dzx = jnp.concatenate([dz, dxbc], axis=1)
            t1 = _mm_nt(ddtr, w_dt[i], name=f"c_in_dt_dx_{l}")
            dhn = _mm_nt(dzx, w_zx[i], add=t1, name=f"c_in_dx_{l}")
            G["c_w_in"][i] = _cols_to_slots([_mm_tn(sv["hn"], dzx, name=f"c_in_dw_{l}"), _mm_tn(sv["hn"], ddtr, name=f"c_in_dt_dw_{l}")],
                                            "c_w_in", bf16, name=f"slots_c_in_{l}")
        dx = dxl
    (grad_x,), (dg1,) = _rowwise_bwd("norm_first_bwd", _f_first_bwd, [(x, D, 0)], [], [(g1[0], D, 0)], [], [(dx, D, 0), (dhn, D, 0)],
                                     [f32], tm=256)
    G["norm_mix_pre"][0] = dg1[0]

    small_names = [n for n, _ in SHARDED[N_BIG:]]
    col_slots = [jnp.concatenate(G[n], axis=1) for n in COL_SHARDED]
    row_slots = jnp.concatenate([g.reshape(NDEV, -1, D).astype(bf16) for n in ROW_SHARDED for g in G[n]], axis=1)
    small_slots = _pack_slots([_to_slots(jnp.stack(G[n]), ax) for n, ax in SHARDED[N_BIG:]], 8)
    ar = _pack([jnp.stack(G[n]) for n in REPLICATED], f32, 8)
    exch = _exchange([(a, True) for a in col_slots + [row_slots, small_slots]] + [(ar, False)], name="exchange_grads")
    pk = lambda d, names: _pack([d[n] for n in names], f32, 8)
    upd = [_adamw(exch[j], col(W, n, f32), col(Mo, n, f32), col(Vo, n, f32), name=f"adamw_{n}", tr=256)
           for j, n in enumerate(COL_SHARDED)]
    upd_rows = _adamw(exch[3], rows_of(W, f32), rows_of(Mo, f32), rows_of(Vo, f32), name="adamw_rows", tr=64)
    upd_small = _adamw(exch[4], pk(W, small_names), pk(Mo, small_names), pk(Vo, small_names), name="adamw_small",
                       tr=small_slots.shape[1])
    upd_rep = _adamw(exch[5], pk(W, REPLICATED), pk(Mo, REPLICATED), pk(Vo, REPLICATED), name="adamw_replicated", tr=ar.shape[0])
    res = []
    for k in range(4):
        d = {n: upd[j][k].reshape(W[n].shape) for j, n in enumerate(COL_SHARDED)}
        off = 0
        for n in ROW_SHARDED:
            cnt = W[n].shape[0] * W[n].shape[1]
            d[n] = upd_rows[k][off:off + cnt].reshape(W[n].shape)
            off += cnt
        d.update(zip(small_names, _unpack(upd_small[k], [W[n].shape for n in small_names])))
        d.update(zip(REPLICATED, _unpack(upd_rep[k], [W[n].shape for n in REPLICATED])))
        res.append(d)
    outs = [loss, grad_x[None]]
    for k in range(4):
        outs += [res[k][n] for n in WEIGHTS]
    return tuple(outs)


def _pack_slots(slot_arrays, row_mult):
    flat = jnp.concatenate([a.reshape(NDEV, -1) for a in slot_arrays], axis=1)
    rows = -(-flat.shape[1] // LANES)
    rows = -(-rows // row_mult) * row_mult
    return jnp.pad(flat, ((0, 0), (0, rows * LANES - flat.shape[1]))).reshape(NDEV, rows, LANES)
```

```python
import functools
import numpy as np
import jax
import jax.numpy as jnp
from jax import lax
from jax.experimental import pallas as pl
from jax.experimental.pallas import tpu as pltpu

f32, bf16 = jnp.float32, jnp.bfloat16
S = jax.ShapeDtypeStruct
HI = lax.Precision.HIGHEST

D = 1024
DEPTH = 4
GRID_W = 64
CHUNK = 128
EPS = 1e-6
RH, RDH, RW = 8, 64, 512
NAH, NADH, NAW = 8, 64, 512
NA_WR, NA_WC = 8, 16
NA_QROWS = 8
NA_KROWS = 16
NA_PAIR = 2
SSD_INNER, SSD_HD, SSD_H, SSD_G, SSD_HPG, SSD_N, SSD_CONV = 2048, 64, 32, 4, 8, 128, 5
SSD_XBC = SSD_INNER + 2 * SSD_G * SSD_N
FFN, FFN_CONV = 2816, 3
FFN_TC = 512
SCAN_HEADS_PER_STEP = 8
ROPE_BASE = 10000.0
LR, B1, B2, AEPS, WD, STEP = 0.001, 0.9, 0.999, 1e-08, 0.01, 10
NDEV = 8
LANES = 128
VMEM_LIMIT = 56 * 1024 * 1024
MM_BLOCK_BYTES = 6 * 1024 * 1024

NT = (((1,), (1,)), ((), ()))
TN = (((0,), (0,)), ((), ()))

SHARDED = [("ab_w_in", 2), ("ab_w_out", 1), ("c_w_in", 2), ("c_w_out", 1), ("ffn_w_up", 2), ("ffn_w_down", 1),
           ("c_conv_w", 2), ("c_conv_b", 1), ("c_norm_g", 1), ("ffn_conv_w", 2)]
N_BIG = 6
COL_SHARDED = ["ab_w_in", "c_w_in", "ffn_w_up"]
ROW_SHARDED = ["ab_w_out", "c_w_out", "ffn_w_down"]
REPLICATED = ["norm_mix_pre", "norm_mix_post", "norm_ffn_pre", "norm_ffn_post", "ab_ret_decay_logit", "ab_ret_gn_g",
              "ab_na_rpb", "c_dt_bias", "c_a_log", "c_d_skip", "ffn_conv_b"]
WEIGHTS = ["norm_mix_pre", "norm_mix_post", "norm_ffn_pre", "norm_ffn_post", "ab_w_in", "ab_ret_decay_logit",
           "ab_ret_gn_g", "ab_na_rpb", "ab_w_out", "c_w_in", "c_conv_w", "c_conv_b", "c_dt_bias", "c_a_log", "c_d_skip",
           "c_norm_g", "c_w_out", "ffn_w_up", "ffn_conv_w", "ffn_conv_b", "ffn_w_down"]


def _params(sem=None):
    return pltpu.CompilerParams(dimension_semantics=sem, vmem_limit_bytes=VMEM_LIMIT)


def _mm_nn(a, w, *, name, tm=1024, tn=512, out_dtype=f32):
    M, K = a.shape
    N = w.shape[1]
    tn = min(tn, N)

    def body(a_ref, w_ref, o_ref):
        o_ref[...] = jnp.dot(a_ref[...], w_ref[...], preferred_element_type=f32).astype(o_ref.dtype)

    return pl.pallas_call(
        body, name=name, grid=(M // tm, N // tn),
        in_specs=[pl.BlockSpec((tm, K), lambda i, j: (i, 0)), pl.BlockSpec((K, tn), lambda i, j: (0, j))],
        out_specs=pl.BlockSpec((tm, tn), lambda i, j: (i, j)),
        out_shape=S((M, N), out_dtype), compiler_params=_params(("parallel", "parallel")))(a, w)


def _mm_nt(dy, w, *, name, add=None, tm=512):
    M, N = dy.shape
    K = w.shape[0]
    tk = next((t for t in (1024, 1408, 512, 256, 128) if K % t == 0 and (t <= 512 or t * N * 2 <= MM_BLOCK_BYTES)), K)

    def body(*refs):
        if add is None:
            d_ref, w_ref, o_ref = refs
            o_ref[...] = lax.dot_general(d_ref[...], w_ref[...], NT, preferred_element_type=f32)
        else:
            d_ref, w_ref, a_ref, o_ref = refs
            o_ref[...] = lax.dot_general(d_ref[...], w_ref[...], NT, preferred_element_type=f32) + a_ref[...]

    in_specs = [pl.BlockSpec((tm, N), lambda i, j: (i, 0)), pl.BlockSpec((tk, N), lambda i, j: (j, 0))]
    args = [dy, w]
    if add is not None:
        in_specs.append(pl.BlockSpec((tm, tk), lambda i, j: (i, j)))
        args.append(add)
    return pl.pallas_call(
        body, name=name, grid=(M // tm, K // tk), in_specs=in_specs,
        out_specs=pl.BlockSpec((tm, tk), lambda i, j: (i, j)),
        out_shape=S((M, K), f32), compiler_params=_params(("parallel", "parallel")))(*args)


def _mm_tn(a, dy, *, name, tt=1024):
    M, K = a.shape
    N = dy.shape[1]
    tk = K if K <= 1024 else (1024 if K % 1024 == 0 else K // 2)
    tn = min(512, N)
    tt = min(tt, M)

    def body(a_ref, d_ref, o_ref):
        t = pl.program_id(2)
        part = lax.dot_general(a_ref[...], d_ref[...], TN, preferred_element_type=f32)

        @pl.when(t == 0)
        def _():
            o_ref[...] = part

        @pl.when(t > 0)
        def _():
            o_ref[...] += part

    return pl.pallas_call(
        body, name=name, grid=(K // tk, N // tn, M // tt),
        in_specs=[pl.BlockSpec((tt, tk), lambda k, n, t: (t, k)), pl.BlockSpec((tt, tn), lambda k, n, t: (t, n))],
        out_specs=pl.BlockSpec((tk, tn), lambda k, n, t: (k, n)),
        out_shape=S((K, N), f32), compiler_params=_params(("parallel", "parallel", "arbitrary")))(a, dy)


def _tile_spec(tm, width, base):
    return pl.BlockSpec((tm, width), lambda j, i: (i, base + j))


def _par_spec(width, base):
    return pl.BlockSpec((1, width), lambda j, i: (0, base + j))


def _full_spec(a):
    nd = a.ndim
    return pl.BlockSpec(a.shape, lambda j, i: (0,) * nd)


def _rowwise(name, f, tiles, ctiles, params, consts, outs, *, tm, J=1):
    T = tiles[0][0].shape[0]
    nt, nct, npar, nc = len(tiles), len(ctiles), len(params), len(consts)

    def body(*refs):
        tv = [r[...].astype(f32) for r in refs[:nt + nct]]
        pv = [r[...] for r in refs[nt + nct:nt + nct + npar + nc]]
        res = f(*tv, *pv)
        for o, v in zip(refs[nt + nct + npar + nc:], res):
            o[...] = v.astype(o.dtype)

    in_specs = ([_tile_spec(tm, w, b) for _, w, b in tiles + ctiles] + [_par_spec(w, b) for _, w, b in params]
                + [_full_spec(c) for c in consts])
    return pl.pallas_call(
        body, name=name, grid=(J, T // tm), in_specs=in_specs,
        out_specs=[_tile_spec(tm, w, 0) for w, _ in outs],
        out_shape=[S((T, J * w), dt) for w, dt in outs],
        compiler_params=_params(("parallel", "parallel")))(
            *[a for a, _, _ in tiles + ctiles], *[a for a, _, _ in params], *consts)


def _rowwise_bwd(name, f, tiles, ctiles, params, consts, douts, dtile_dtypes, *, tm, J=1):
    T = tiles[0][0].shape[0]
    nt, nct, npar, nc, nd = len(tiles), len(ctiles), len(params), len(consts), len(douts)

    def body(*refs):
        i = pl.program_id(1)
        k = 0
        tv = [r[...].astype(f32) for r in refs[k:k + nt]]; k += nt
        cv = [r[...].astype(f32) for r in refs[k:k + nct]]; k += nct
        pv = [r[...] for r in refs[k:k + npar]]; k += npar
        kv = [r[...] for r in refs[k:k + nc]]; k += nc
        dv = [r[...].astype(f32) for r in refs[k:k + nd]]; k += nd
        dt_refs = refs[k:k + nt]; k += nt
        dp_refs = refs[k:k + npar]
        _, vjp = jax.vjp(lambda tv_, pv_: tuple(f(*tv_, *cv, *pv_, *kv)), tv, pv)
        dts, dps = vjp(tuple(dv))
        for r, g in zip(dt_refs, dts):
            r[...] = g.astype(r.dtype)
        for r, g in zip(dp_refs, dps):
            @pl.when(i == 0)
            def _(r=r, g=g):
                r[...] = g

            @pl.when(i > 0)
            def _(r=r, g=g):
                r[...] += g

    in_specs = ([_tile_spec(tm, w, b) for _, w, b in tiles + ctiles] + [_par_spec(w, b) for _, w, b in params]
                + [_full_spec(c) for c in consts] + [_tile_spec(tm, w, b) for _, w, b in douts])
    res = pl.pallas_call(
        body, name=name, grid=(J, T // tm), in_specs=in_specs,
        out_specs=[_tile_spec(tm, w, 0) for _, w, _ in tiles] + [_par_spec(w, b) for _, w, b in params],
        out_shape=[S((T, J * w), dt) for (_, w, _), dt in zip(tiles, dtile_dtypes)] + [S(a.shape, f32) for a, _, _ in params],
        compiler_params=_params(("parallel", "arbitrary")))(
            *[a for a, _, _ in tiles + ctiles], *[a for a, _, _ in params], *consts, *[a for a, _, _ in douts])
    return res[:nt], res[nt:]


def _rms(x, g):
    return x * lax.rsqrt(jnp.mean(x * x, axis=-1, keepdims=True) + EPS) * g


def _f_first(x, g1):
    return (_rms(x, g1),)


def _f_first_bwd(x, g1):
    return (x, _rms(x, g1))


def _f_mid(x, m, g2, g3):
    x1 = x + _rms(m, g2)
    return (x1, _rms(x1, g3))


def _f_end(x1, fo, g4, g1n):
    x2 = x1 + _rms(fo, g4)
    return (x2, _rms(x2, g1n))


def _f_last(x1, fo, g4):
    return (x1 + _rms(fo, g4),)


@jax.custom_vjp
def _swap_halves(x):
    c = x.shape[1]
    lane = lax.broadcasted_iota(jnp.int32, x.shape, 1) % RDH
    return jnp.where(lane < RDH // 2, pltpu.roll(x, c - RDH // 2, axis=1), pltpu.roll(x, RDH // 2, axis=1))


_swap_halves.defvjp(lambda x: (_swap_halves(x), None), lambda _, g: (_swap_halves(g),))


def _f_rprep(rq, rk, cos, sin):
    rot = lambda t: t * cos + _swap_halves(t) * sin
    return (rot(rq), rot(rk) * (RDH ** -0.5))


def _f_rpost(yf, yb, rg, gn, gavg):
    y = yf + yb
    mu = jnp.dot(y, gavg, precision=HI, preferred_element_type=f32)
    yc = y - mu
    var = jnp.dot(yc * yc, gavg, precision=HI, preferred_element_type=f32)
    return (jax.nn.silu(rg) * (yc * lax.rsqrt(var + EPS) * gn),)


def _f_sprep(xs, dtr, dtb, alog, ex0, ex1):
    dt = jax.nn.softplus(dtr + dtb)
    la = dt * (-jnp.exp(alog))
    e0 = jnp.dot(dt, ex0, precision=HI, preferred_element_type=f32)
    e1 = jnp.dot(dt, ex1, precision=HI, preferred_element_type=f32)
    return (xs * e0, xs * e1, la)


def _f_spost(yf, yb, xs, z, dsk, ng):
    y = (yf + yb + xs * dsk) * jax.nn.silu(z)
    y = y * lax.rsqrt(jnp.mean(y * y, axis=-1, keepdims=True) + EPS)
    return (y * ng,)


def _loss_call(y, tgt, *, tm=256):
    T = y.shape[0]

    def body(y_ref, t_ref, dy_ref, l_ref):
        i = pl.program_id(0)
        e = y_ref[...] - t_ref[...]
        dy_ref[...] = e * (1.0 / D)
        part = jnp.zeros((8, LANES), f32) + 0.5 * jnp.sum(jnp.mean(e * e, axis=-1, keepdims=True))

        @pl.when(i == 0)
        def _():
            l_ref[...] = part

        @pl.when(i > 0)
        def _():
            l_ref[...] += part

    return pl.pallas_call(
        body, name="loss_head", grid=(T // tm,),
        in_specs=[pl.BlockSpec((tm, D), lambda i: (i, 0))] * 2,
        out_specs=[pl.BlockSpec((tm, D), lambda i: (i, 0)), pl.BlockSpec((8, LANES), lambda i: (0, 0))],
        out_shape=[S((T, D), f32), S((8, LANES), f32)], compiler_params=_params(("arbitrary",)))(y, tgt)


def _colsum(x, *, name, tm=512):
    T, C = x.shape

    def body(x_ref, o_ref):
        i = pl.program_id(0)
        part = jnp.sum(x_ref[...], axis=0, keepdims=True)

        @pl.when(i == 0)
        def _():
            o_ref[...] = part

        @pl.when(i > 0)
        def _():
            o_ref[...] += part

    return pl.pallas_call(
        body, name=name, grid=(T // tm,), in_specs=[pl.BlockSpec((tm, C), lambda i: (i, 0))],
        out_specs=pl.BlockSpec((1, C), lambda i: (0, 0)), out_shape=S((1, C), f32),
        compiler_params=_params(("arbitrary",)))(x)


def _nn(a, b):
    return jnp.dot(a, b, preferred_element_type=f32)


def _nt(a, b):
    return lax.dot_general(a, b, NT, preferred_element_type=f32)


@jax.custom_vjp
def _mm_lt(a, a_t, b):
    return _nn(a_t, b)


_mm_lt.defvjp(lambda a, a_t, b: (_nn(a_t, b), (a, b)),
              lambda res, g: (jnp.zeros_like(res[0]), _nt(g, res[1]), _nn(res[0], g)))


@jax.custom_vjp
def _mm_rt(a, a_t, b):
    return _nn(a, b)


_mm_rt.defvjp(lambda a, a_t, b: (_nn(a, b), (a_t, b)),
              lambda res, g: (_nt(g, res[1]), jnp.zeros_like(res[0]), _nn(res[0], g)))


@jax.custom_vjp
def _masked_mm(s, s_t, d, d_t, v):
    return _nn(s * d, v)


def _masked_mm_bwd(res, g):
    s, s_t, d, d_t, v = res
    da = _nt(g, v)
    return (da * d, jnp.zeros_like(s_t), da * s, jnp.zeros_like(d_t), _nn(s_t * d_t, g))


_masked_mm.defvjp(lambda s, s_t, d, d_t, v: (_nn(s * d, v), (s, s_t, d, d_t, v)), _masked_mm_bwd)


def _scan_step(h, q, k, v, a, rev, for_vjp=False):
    L = q.shape[0]
    Hg = v.shape[0]
    ii = lax.broadcasted_iota(jnp.int32, (L, L), 0)
    jj = lax.broadcasted_iota(jnp.int32, (L, L), 1)
    if rev:
        tri, tri_t, dmask, dmask_t = (jj >= ii), (ii >= jj), (jj > ii), (ii > jj)
    else:
        tri, tri_t, dmask, dmask_t = (jj <= ii), (ii <= jj), (jj <= ii), (ii <= jj)
    cs = jnp.dot(tri.astype(f32), a, precision=HI, preferred_element_type=f32)
    cs_t = lax.dot_general(a, tri_t.astype(f32), TN, precision=HI, preferred_element_type=f32)
    tot = jnp.sum(a, axis=0, keepdims=True)
    qk = _nt(q, k)
    k_t = k.T
    if for_vjp:
        q_t = lax.stop_gradient(q.T)
        qk_t = lax.stop_gradient(_nt(k, q))
    hs, ys = [], []
    for hh in range(Hg):
        c_col, c_row = cs[:, hh:hh + 1], cs_t[hh:hh + 1, :]
        dec = jnp.exp(jnp.where(dmask, c_col - c_row, -1e30))
        t_all = tot[:, hh:hh + 1]
        w = v[hh] * jnp.exp(t_all - c_col)
        if for_vjp:
            dec_t = lax.stop_gradient(jnp.exp(jnp.where(dmask_t, c_row - c_col, -1e30)))
            y = _masked_mm(qk, qk_t, dec, dec_t, v[hh]) + _mm_rt(q, q_t, h[hh]) * jnp.exp(c_col)
            hn = h[hh] * jnp.exp(t_all) + _mm_lt(lax.stop_gradient(k), k_t, w)
        else:
            y = _nn(qk * dec, v[hh]) + _nn(q, h[hh]) * jnp.exp(c_col)
            hn = h[hh] * jnp.exp(t_all) + _nn(k_t, w)
        hs.append(hn)
        ys.append(y)
    return jnp.stack(hs), jnp.stack(ys)


def _scan_dims(a, N, P, T):
    G, Ha = a.shape[0], a.shape[2]
    return G, Ha, T // CHUNK


def _scan_specs(gb, N, Hg, P, Ha, cm, qcol, kcol, vcol):
    qs = lambda col: pl.BlockSpec((CHUNK, gb * N), lambda g, c: (cm(c), col + g))
    vs = lambda col: pl.BlockSpec((CHUNK, gb * Hg * P), lambda g, c: (cm(c), col + g))
    as_ = pl.BlockSpec((gb, CHUNK, Ha), lambda g, c: (g, cm(c), 0))
    hs = pl.BlockSpec((gb, 1, Hg, N, P), lambda g, c: (g, cm(c), 0, 0, 0))
    return qs(qcol), qs(kcol), vs(vcol), qs(0), vs(0), as_, hs


def _heads(ref, j, Hg, P):
    return jnp.stack([ref[:, (j * Hg + hh) * P:(j * Hg + hh + 1) * P] for hh in range(Hg)])


def _scan_fwd(q, k, v, a, *, N, Hg, P, qcol=0, kcol=0, vcol=0, rev, name):
    T = q.shape[0]
    G, Ha, NC = _scan_dims(a, N, P, T)
    gb = SCAN_HEADS_PER_STEP // Hg
    cm = (lambda c: NC - 1 - c) if rev else (lambda c: c)
    qs, ks, vs, _, ys, as_, hs = _scan_specs(gb, N, Hg, P, Ha, cm, qcol, kcol, vcol)

    def body(q_ref, k_ref, v_ref, a_ref, y_ref, hs_ref, h_scr):
        @pl.when(pl.program_id(1) == 0)
        def _():
            h_scr[...] = jnp.zeros_like(h_scr)

        for j in range(gb):
            h = h_scr[j]
            hs_ref[j, 0] = h
            hn, y = _scan_step(h, q_ref[:, j * N:(j + 1) * N], k_ref[:, j * N:(j + 1) * N], _heads(v_ref, j, Hg, P), a_ref[j], rev)
            for hh in range(Hg):
                y_ref[:, (j * Hg + hh) * P:(j * Hg + hh + 1) * P] = y[hh]
            h_scr[j] = hn

    return pl.pallas_call(
        body, name=name, grid=(G // gb, NC), in_specs=[qs, ks, vs, as_], out_specs=[ys, hs],
        out_shape=[S((T, G * Hg * P), f32), S((G, NC, Hg, N, P), f32)],
        scratch_shapes=[pltpu.VMEM((gb, Hg, N, P), f32)],
        compiler_params=_params(("parallel", "arbitrary")))(q, k, v, a)


def _scan_bwd(q, k, v, a, hsave, dy, *, N, Hg, P, qcol=0, kcol=0, vcol=0, rev, name):
    T = q.shape[0]
    G, Ha, NC = _scan_dims(a, N, P, T)
    gb = SCAN_HEADS_PER_STEP // Hg
    cm = (lambda c: c) if rev else (lambda c: NC - 1 - c)
    qs, ks, vs, dqs, dvs, as_, hs = _scan_specs(gb, N, Hg, P, Ha, cm, qcol, kcol, vcol)

    def body(q_ref, k_ref, v_ref, a_ref, hs_ref, dy_ref, dq_ref, dk_ref, dv_ref, da_ref, dh_scr):
        @pl.when(pl.program_id(1) == 0)
        def _():
            dh_scr[...] = jnp.zeros_like(dh_scr)

        for j in range(gb):
            cols = slice(j * N, (j + 1) * N)
            _, vjp = jax.vjp(functools.partial(_scan_step, rev=rev, for_vjp=True), hs_ref[j, 0], q_ref[:, cols], k_ref[:, cols],
                             _heads(v_ref, j, Hg, P), a_ref[j])
            dh, dq, dk, dv, da = vjp((dh_scr[j], _heads(dy_ref, j, Hg, P)))
            dq_ref[:, cols] = dq
            dk_ref[:, cols] = dk
            for hh in range(Hg):
                dv_ref[:, (j * Hg + hh) * P:(j * Hg + hh + 1) * P] = dv[hh]
            da_ref[j] = da
            dh_scr[j] = dh

    return pl.pallas_call(
        body, name=name, grid=(G // gb, NC), in_specs=[qs, ks, vs, as_, hs, dvs], out_specs=[dqs, dqs, dvs, as_],
        out_shape=[S((T, G * N), f32), S((T, G * N), f32), S((T, G * Hg * P), f32), S((G, T, Ha), f32)],
        scratch_shapes=[pltpu.VMEM((gb, Hg, N, P), f32)],
        compiler_params=_params(("parallel", "arbitrary")))(q, k, v, a, hsave, dy)


def _na_block_case(rb, nrb):
    return jnp.where(rb == 0, 0, jnp.where(rb == nrb - 1, 2, 1))


def _na_key_start(rb, rows):
    return pl.multiple_of(jnp.clip(rb * NA_QROWS - NA_WR // 2, 0, rows - NA_KROWS) * GRID_W, 256)


def _na_specs(T, nrb):
    nq, nk, wb = NA_QROWS * GRID_W, NA_KROWS * GRID_W, NA_PAIR * NADH
    qs = lambda col: pl.BlockSpec((nq, wb), lambda p, r: (r, col + p))
    fs = lambda col: pl.BlockSpec((T, wb), lambda p, r: (0, col + p))
    bs = pl.BlockSpec((NA_PAIR, 1, nq, nk), lambda p, r: (p, _na_block_case(r, nrb), 0, 0))
    ls = pl.BlockSpec((1, nq, NA_PAIR), lambda p, r: (p, r, 0))
    return qs, fs, bs, ls


def _na_fwd(q, k, v, bias, *, qcol, kcol, vcol, name):
    T = q.shape[0]
    rows = T // GRID_W
    nq, nk = NA_QROWS * GRID_W, NA_KROWS * GRID_W
    nrb = T // nq
    scale = NADH ** -0.5
    qs, fs, bs, ls = _na_specs(T, nrb)

    def body(q_ref, k_ref, v_ref, b_ref, o_ref, l_ref):
        ks = _na_key_start(pl.program_id(1), rows)
        for hh in range(NA_PAIR):
            sl = slice(hh * NADH, (hh + 1) * NADH)
            kw = k_ref[pl.ds(ks, nk), sl]
            vw = v_ref[pl.ds(ks, nk), sl]
            s = lax.dot_general(q_ref[:, sl], kw, NT, preferred_element_type=f32) * scale + b_ref[hh, 0]
            m = jnp.max(s, axis=1, keepdims=True)
            p = jnp.exp(s - m)
            l = jnp.sum(p, axis=1, keepdims=True)
            o_ref[:, sl] = jnp.dot(p.astype(bf16), vw, preferred_element_type=f32) / l
            l_ref[0, :, hh:hh + 1] = m + jnp.log(l)

    return pl.pallas_call(
        body, name=name, grid=(NAH // NA_PAIR, nrb), in_specs=[qs(qcol), fs(kcol), fs(vcol), bs],
        out_specs=[qs(0), ls], out_shape=[S((T, NAW), f32), S((NAH // NA_PAIR, T, NA_PAIR), f32)],
        compiler_params=_params(("parallel", "arbitrary")))(q, k, v, bias)


def _na_bwd(q, k, v, bias, o, lse, do, *, qcol, kcol, vcol, docol, name):
    T = q.shape[0]
    rows = T // GRID_W
    nq, nk = NA_QROWS * GRID_W, NA_KROWS * GRID_W
    nrb = T // nq
    scale = NADH ** -0.5
    qs, fs, bs, ls = _na_specs(T, nrb)

    def body(q_ref, k_ref, v_ref, b_ref, o_ref, l_ref, do_ref, dq_ref, dk_ref, dv_ref, db_ref):
        rb = pl.program_id(1)

        @pl.when(rb == 0)
        def _():
            dk_ref[...] = jnp.zeros_like(dk_ref)
            dv_ref[...] = jnp.zeros_like(dv_ref)

        ks = _na_key_start(rb, rows)
        first = (rb == 0) | (rb == 1) | (rb == nrb - 1)
        for hh in range(NA_PAIR):
            sl = slice(hh * NADH, (hh + 1) * NADH)
            qv = q_ref[:, sl]
            kw = k_ref[pl.ds(ks, nk), sl]
            vw = v_ref[pl.ds(ks, nk), sl]
            s = lax.dot_general(qv, kw, NT, preferred_element_type=f32) * scale + b_ref[hh, 0]
            p = jnp.exp(s - l_ref[0, :, hh:hh + 1])
            do_ = do_ref[:, sl]
            dob = do_.astype(bf16)
            dp = lax.dot_general(dob, vw, NT, preferred_element_type=f32)
            ds = p * (dp - jnp.sum(do_ * o_ref[:, sl], axis=1, keepdims=True))
            dsb = ds.astype(bf16)
            dq_ref[:, sl] = jnp.dot(dsb, kw, preferred_element_type=f32) * scale
            dk_ref[pl.ds(ks, nk), sl] += lax.dot_general(dsb, qv, TN, preferred_element_type=f32) * scale
            dv_ref[pl.ds(ks, nk), sl] += lax.dot_general(p.astype(bf16), dob, TN, preferred_element_type=f32)

            @pl.when(first)
            def _(hh=hh, ds=ds):
                db_ref[hh, 0] = ds

            @pl.when(jnp.logical_not(first))
            def _(hh=hh, ds=ds):
                db_ref[hh, 0] += ds

    return pl.pallas_call(
        body, name=name, grid=(NAH // NA_PAIR, nrb),
        in_specs=[qs(qcol), fs(kcol), fs(vcol), bs, qs(0), ls, qs(docol)],
        out_specs=[qs(0), fs(0), fs(0), bs],
        out_shape=[S((T, NAW), f32), S((T, NAW), f32), S((T, NAW), f32), S(bias.shape, f32)],
        compiler_params=_params(("parallel", "arbitrary")))(q, k, v, bias, o, lse, do)


def _na_bias_tables(rows):
    c = np.arange(GRID_W)[:, None]
    kc = np.arange(GRID_W)[None, :]
    cstart = np.clip(c - NA_WC // 2, 0, GRID_W - NA_WC)
    valid_c = (kc >= cstart) & (kc < cstart + NA_WC)
    dc = kc - c + NA_WC - 1
    E = (valid_c[:, :, None] & (dc[:, :, None] == np.arange(2 * NA_WC - 1)[None, None, :])).astype(np.float32)
    A = np.zeros((3, NA_QROWS, NA_KROWS, 2 * NA_WR - 1), np.float32)
    valid_r = np.zeros((3, NA_QROWS, NA_KROWS), bool)
    for z, r0 in enumerate((0, NA_QROWS, rows - NA_QROWS)):
        ks = int(np.clip(r0 - NA_WR // 2, 0, rows - NA_KROWS))
        for ri in range(NA_QROWS):
            r = r0 + ri
            rs = int(np.clip(r - NA_WR // 2, 0, rows - NA_WR))
            for kri in range(NA_KROWS):
                kr = ks + kri
                if rs <= kr < rs + NA_WR:
                    valid_r[z, ri, kri] = True
                    A[z, ri, kri, kr - r + NA_WR - 1] = 1.0
    mask = np.where(valid_r[:, :, None, :, None] & valid_c[None, None, :, None, :], 0.0, -1e30).astype(np.float32)
    return E, A, mask


def _na_bias(rpb, rows):
    E, A, mask = _na_bias_tables(rows)
    r1 = jnp.einsum("hde,cke->hdck", rpb, E, precision=HI)
    b = jnp.einsum("hdck,zabd->hzacbk", r1, A, precision=HI) + mask[None]
    return b.reshape(rpb.shape[0], 3, NA_QROWS * GRID_W, NA_KROWS * GRID_W)


def _conv_shifts(prev, cur, nxt, i, n_i, W):
    tm = cur.shape[0]
    prev = jnp.where(i > 0, prev, 0.0)
    nxt = jnp.where(i < n_i - 1, nxt, 0.0)
    ext = jnp.concatenate([prev, cur, nxt], axis=0)
    out = []
    for w in range(W):
        s = (W // 2 - w) % (tm + 16)
        out.append((ext if s == 0 else pltpu.roll(ext, s, axis=0))[8:8 + tm])
    return out


def _conv_act(u, mode):
    if mode == "silu":
        return jax.nn.silu(u)
    if mode == "geglu":
        half = u.shape[1] // 2
        return jax.nn.gelu(u[:, :half], approximate=True) * u[:, half:]
    return u


def _conv_specs(T, tm, tc, xbase):
    r8 = tm // 8
    last = T // 8 - 1
    cur = pl.BlockSpec((tm, tc), lambda j, i: (i, xbase + j))
    prev = pl.BlockSpec((8, tc), lambda j, i: (jnp.maximum(i * r8 - 1, 0), xbase + j))
    nxt = pl.BlockSpec((8, tc), lambda j, i: (jnp.minimum((i + 1) * r8, last), xbase + j))
    return cur, prev, nxt


def _conv(x, w8, b, *, mode, W, name, C, xbase=0, tm=512, tc=512, out_dtype=f32):
    T = x.shape[0]
    NI, J = T // tm, C // tc
    tco = tc // 2 if mode == "geglu" else tc
    cur, prev, nxt = _conv_specs(T, tm, tc, xbase)

    def body(xc, xp, xn, w_ref, b_ref, o_ref):
        sh = _conv_shifts(xp[...].astype(f32), xc[...].astype(f32), xn[...].astype(f32), pl.program_id(1), NI, W)
        wv = w_ref[...]
        u = sh[0] * wv[0:1, :]
        for w in range(1, W):
            u = u + sh[w] * wv[w:w + 1, :]
        if mode != "none":
            u = u + b_ref[...]
        o_ref[...] = _conv_act(u, mode).astype(o_ref.dtype)

    return pl.pallas_call(
        body, name=name, grid=(J, NI),
        in_specs=[cur, prev, nxt, pl.BlockSpec((8, tc), lambda j, i: (0, j)), pl.BlockSpec((1, tc), lambda j, i: (0, j))],
        out_specs=pl.BlockSpec((tm, tco), lambda j, i: (i, j)), out_shape=S((T, J * tco), out_dtype),
        compiler_params=_params(("parallel", "parallel")))(x, x, x, w8, b)


def _conv_bwd(x, w8, b, dact, *, mode, W, name, C, xbase=0, tm=512, tc=512):
    T = x.shape[0]
    NI, J = T // tm, C // tc
    tco = tc // 2 if mode == "geglu" else tc
    rows = tm + 16
    pad = W // 2
    cur, prev, nxt = _conv_specs(T, tm, tc, xbase)
    dcur, dprev, dnxt = _conv_specs(T, tm, tco, 0)

    def body(xc, xp, xn, w_ref, b_ref, dc, dp, dn, dx_ref, dw_ref, db_ref):
        i = pl.program_id(1)
        ext = jnp.concatenate([jnp.where(i > 0, xp[...], 0.0), xc[...], jnp.where(i < NI - 1, xn[...], 0.0)], axis=0)
        dext = jnp.concatenate([jnp.where(i > 0, dp[...], 0.0), dc[...], jnp.where(i < NI - 1, dn[...], 0.0)], axis=0)
        wv = w_ref[...]
        shift = lambda t, w: t if w == pad else pltpu.roll(t, (pad - w) % rows, axis=0)
        xs = [shift(ext, w) for w in range(W)]
        u = b_ref[...] + xs[0] * wv[0:1, :]
        for w in range(1, W):
            u = u + xs[w] * wv[w:w + 1, :]
        _, vjp = jax.vjp(functools.partial(_conv_act, mode=mode), u)
        du = vjp(dext.astype(f32))[0]
        dx = shift(du, 0)[8:8 + tm] * wv[W - 1:W, :]
        for w in range(1, W):
            dx = dx + shift(du, w)[8:8 + tm] * wv[W - 1 - w:W - w, :]
        dx_ref[...] = dx.astype(dx_ref.dtype)

        @pl.when(i == 0)
        def _():
            dw_ref[...] = jnp.zeros_like(dw_ref)
            db_ref[...] = jnp.zeros_like(db_ref)

        dum = du[8:8 + tm]
        db_ref[...] += jnp.sum(dum, axis=0, keepdims=True)
        for w in range(W):
            dw_ref[w:w + 1, :] += jnp.sum(dum * xs[w][8:8 + tm], axis=0, keepdims=True)

    return pl.pallas_call(
        body, name=name, grid=(J, NI),
        in_specs=[cur, prev, nxt, pl.BlockSpec((8, tc), lambda j, i: (0, j)), pl.BlockSpec((1, tc), lambda j, i: (0, j)),
                  dcur, dprev, dnxt],
        out_specs=[pl.BlockSpec((tm, tc), lambda j, i: (i, j)), pl.BlockSpec((8, tc), lambda j, i: (0, j)),
                   pl.BlockSpec((1, tc), lambda j, i: (0, j))],
        out_shape=[S((T, C), bf16), S((8, C), f32), S((1, C), f32)],
        compiler_params=_params(("parallel", "arbitrary")))(x, x, x, w8, b, dact, dact, dact)


def _pad8(w):
    return jnp.concatenate([w, jnp.zeros((8 - w.shape[0], w.shape[1]), w.dtype)], axis=0)


def _all_gather(arrs, *, name):
    n = len(arrs)

    def body(*refs):
        ins, outs = refs[:n], refs[n:2 * n]
        send_sems, recv_sems, loc_sems = refs[2 * n:]
        x, y, c = lax.axis_index("x"), lax.axis_index("y"), lax.axis_index("c")
        ident = lambda px, py, pc: 4 * px + 2 * py + pc
        me, sibling = (x, y, c), (x, y, 1 - c)
        chips = [(1 - x, y), (x, 1 - y), (1 - x, 1 - y)]

        def copy(a, k, block, to, src=None):
            slot = outs[a].at[ident(*block)]
            return pltpu.make_async_remote_copy(
                src_ref=slot if src is None else src, dst_ref=slot, send_sem=send_sems.at[a * 7 + k], recv_sem=recv_sems.at[a * 7 + k],
                device_id=to, device_id_type=pl.DeviceIdType.MESH)

        local = [pltpu.make_async_copy(ins[a], outs[a].at[ident(*me)], loc_sems.at[a]) for a in range(n)]
        for cp in local:
            cp.start()
        first = []
        for a in range(n):
            first.append(copy(a, 0, me, sibling, src=ins[a]))
            first += [copy(a, 1 + j, me, (*chip, c), src=ins[a]) for j, chip in enumerate(chips)]
        for cp in first:
            cp.start()
        passed = []
        for j, chip in enumerate(chips):
            for a in range(n):
                copy(a, 1 + j, (*chip, c), me).wait_recv()
                fwd = copy(a, 4 + j, (*chip, c), sibling)
                fwd.start()
                passed.append(fwd)
        for a in range(n):
            copy(a, 0, sibling, me).wait_recv()
            for j, chip in enumerate(chips):
                copy(a, 4 + j, (*chip, 1 - c), me).wait_recv()
        for cp in first + passed:
            cp.wait_send()
        for cp in local:
            cp.wait()

    any_spec = pl.BlockSpec(memory_space=pl.ANY)
    return pl.pallas_call(
        body, name=name, in_specs=[any_spec] * n, out_specs=[any_spec] * n,
        out_shape=[S((NDEV,) + a.shape, a.dtype) for a in arrs],
        scratch_shapes=[pltpu.SemaphoreType.DMA((7 * n,)), pltpu.SemaphoreType.DMA((7 * n,)), pltpu.SemaphoreType.DMA((n,))],
        )(*arrs)


NCHIP = NDEV // 2


def _to_sibling(arrs, *, name):
    n = len(arrs)
    ncopy = sum(NCHIP if ps else 1 for _, ps in arrs)

    def body(*refs):
        ins, outs = refs[:n], refs[n:2 * n]
        send_sems, recv_sems = refs[2 * n:]
        x, y, c = lax.axis_index("x"), lax.axis_index("y"), lax.axis_index("c")
        copies, idx = [], 0
        for a, (_, per_slot) in enumerate(arrs):
            pairs = [(ins[a].at[2 * q + (1 - c)], outs[a].at[q]) for q in range(NCHIP)] if per_slot else [(ins[a], outs[a])]
            for src, dst in pairs:
                copies.append(pltpu.make_async_remote_copy(
                    src_ref=src, dst_ref=dst, send_sem=send_sems.at[idx], recv_sem=recv_sems.at[idx],
                    device_id=(x, y, 1 - c), device_id_type=pl.DeviceIdType.MESH))
                idx += 1
        for cp in copies:
            cp.start()
        for cp in copies:
            cp.wait_recv()
        for cp in copies:
            cp.wait_send()

    any_spec = pl.BlockSpec(memory_space=pl.ANY)
    return pl.pallas_call(
        body, name=name, in_specs=[any_spec] * n, out_specs=[any_spec] * n,
        out_shape=[S((NCHIP,) + a.shape[1:] if ps else a.shape, a.dtype) for a, ps in arrs],
        scratch_shapes=[pltpu.SemaphoreType.DMA((ncopy,)), pltpu.SemaphoreType.DMA((ncopy,))])(*[a for a, _ in arrs])


def _add_partials(mine, theirs, *, per_slot, tr, name):
    R, C = mine.shape[-2:]

    def body(a_ref, b_ref, o_ref):
        a = a_ref[lax.axis_index("c")] if per_slot else a_ref[...]
        b = b_ref[0] if per_slot else b_ref[...]
        s = a.astype(f32) + b.astype(f32)
        if per_slot:
            o_ref[0] = s.astype(o_ref.dtype)
        else:
            o_ref[...] = s.astype(o_ref.dtype)

    if per_slot:
        grid = (NCHIP, R // tr)
        in_specs = [pl.BlockSpec((2, tr, C), lambda q, i: (q, i, 0)), pl.BlockSpec((1, tr, C), lambda q, i: (q, i, 0))]
        out_spec, out_shape = pl.BlockSpec((1, tr, C), lambda q, i: (q, i, 0)), S((NCHIP, R, C), mine.dtype)
    else:
        grid = (1, R // tr)
        in_specs = [pl.BlockSpec((tr, C), lambda q, i: (i, 0))] * 2
        out_spec, out_shape = pl.BlockSpec((tr, C), lambda q, i: (i, 0)), S((R, C), mine.dtype)
    return pl.pallas_call(body, name=name, grid=grid, in_specs=in_specs, out_specs=out_spec, out_shape=out_shape,
                          compiler_params=_params(("parallel", "parallel")))(mine, theirs)


def _to_chips(arrs, *, name):
    n = len(arrs)

    def body(*refs):
        ins, outs = refs[:n], refs[n:2 * n]
        send_sems, recv_sems, loc_sems = refs[2 * n:]
        x, y, c = lax.axis_index("x"), lax.axis_index("y"), lax.axis_index("c")
        my_q = 2 * x + y
        src = lambda a, q: ins[a].at[q] if arrs[a][1] else ins[a]
        local = [pltpu.make_async_copy(src(a, my_q), outs[a].at[my_q], loc_sems.at[a]) for a in range(n)]
        for cp in local:
            cp.start()
        sent = []
        for j, (px, py) in enumerate([(1 - x, y), (x, 1 - y), (1 - x, 1 - y)]):
            q = 2 * px + py
            for a in range(n):
                mk = lambda slot, a=a, j=j, q=q, dev=(px, py, c): pltpu.make_async_remote_copy(
                    src_ref=src(a, q), dst_ref=outs[a].at[slot], send_sem=send_sems.at[3 * a + j], recv_sem=recv_sems.at[3 * a + j],
                    device_id=dev, device_id_type=pl.DeviceIdType.MESH)
                mk(my_q).start()
                sent.append((mk, q))
        for mk, q in sent:
            mk(q).wait_recv()
        for mk, q in sent:
            mk(q).wait_send()
        for cp in local:
            cp.wait()

    any_spec = pl.BlockSpec(memory_space=pl.ANY)
    return pl.pallas_call(
        body, name=name, in_specs=[any_spec] * n, out_specs=[any_spec] * n,
        out_shape=[S(a.shape if ps else (NCHIP,) + a.shape, a.dtype) for a, ps in arrs],
        scratch_shapes=[pltpu.SemaphoreType.DMA((3 * n,)), pltpu.SemaphoreType.DMA((3 * n,)), pltpu.SemaphoreType.DMA((n,))],
        )(*[a for a, _ in arrs])


def _adamw(r, w, m, v, *, name, tr):
    M, C = w.shape
    nparts = r.shape[0]

    def body(r_ref, w_ref, m_ref, v_ref, g_ref, d_ref, nm_ref, nv_ref):
        g = r_ref[0].astype(f32)
        for s in range(1, nparts):
            g = g + r_ref[s].astype(f32)
        m_ = B1 * m_ref[...] + (1.0 - B1) * g
        v_ = B2 * v_ref[...] + (1.0 - B2) * jnp.square(g)
        m_hat = m_ / (1.0 - B1 ** STEP)
        v_hat = v_ / (1.0 - B2 ** STEP)
        g_ref[...] = g
        d_ref[...] = -LR * (m_hat / (jnp.sqrt(v_hat) + AEPS) + WD * w_ref[...])
        nm_ref[...] = m_
        nv_ref[...] = v_

    row = pl.BlockSpec((tr, C), lambda i: (i, 0))
    return pl.pallas_call(
        body, name=name, grid=(M // tr,),
        in_specs=[pl.BlockSpec((nparts, tr, C), lambda i: (0, i, 0)), row, row, row],
        out_specs=[row] * 4, out_shape=[S((M, C), f32)] * 4, compiler_params=_params(("parallel",)))(r, w, m, v)


def _colmove(ins, in_slots, outs, moves, *, tk, name):
    R = ins[0].shape[1] if in_slots[0] else ins[0].shape[0]
    n_in = len(ins)

    def body(*refs):
        for ii, isl, ic, oi, osl, oc, w in moves:
            src, dst = refs[ii], refs[n_in + oi]
            val = src[:, ic:ic + w] if isl is None else src[isl, :, ic:ic + w]
            if osl is None:
                dst[:, oc:oc + w] = val.astype(dst.dtype)
            else:
                dst[osl, :, oc:oc + w] = val.astype(dst.dtype)

    def spec(is_slots, C):
        return pl.BlockSpec((NDEV, tk, C), lambda i: (0, i, 0)) if is_slots else pl.BlockSpec((tk, C), lambda i: (i, 0))

    return pl.pallas_call(
        body, name=name, grid=(R // tk,),
        in_specs=[spec(sl, a.shape[-1]) for a, sl in zip(ins, in_slots)],
        out_specs=[spec(sl, C) for sl, C, _ in outs],
        out_shape=[S((NDEV, R, C) if sl else (R, C), dt) for sl, C, dt in outs],
        compiler_params=_params(("parallel",)))(*ins)


def _col_pieces(n8, cuts, place):
    out = []
    for p in range(NDEV):
        lo, hi = p * n8, (p + 1) * n8
        edges = [lo] + [c for c in cuts if lo < c < hi] + [hi]
        for a, b in zip(edges[:-1], edges[1:]):
            out.append((p, a - lo) + place(a) + (b - a,))
    return out


def _place_plain(c):
    return (0, c)


def _place_ssd_in(c):
    return (0, c) if c < SSD_INNER + SSD_XBC else (1, c - (SSD_INNER + SSD_XBC))


def _place_ffn_up(c):
    h = FFN_TC // 2
    return (0, (c // h) * FFN_TC + c % h) if c < FFN else (0, ((c - FFN) // h) * FFN_TC + h + (c - FFN) % h)


_COL_LAYOUTS = {
    "ab_w_in": ([], _place_plain, [4 * RW + 3 * NAW]),
    "c_w_in": ([SSD_INNER + SSD_XBC], _place_ssd_in, [SSD_INNER + SSD_XBC, 2 * SSD_H]),
    "ffn_w_up": (list(range(FFN_TC // 2, 2 * FFN, FFN_TC // 2)), _place_ffn_up, [2 * FFN]),
}


def _cols_from_slots(g, which, *, name):
    cuts, place, widths = _COL_LAYOUTS[which]
    moves = [(0, p, sc, mi, None, mc, w) for p, sc, mi, mc, w in _col_pieces(g.shape[2], cuts, place)]
    return _colmove([g], [True], [(False, w, g.dtype) for w in widths], moves, tk=256, name=name)


def _cols_to_slots(mats, which, dtype, *, name):
    cuts, place, widths = _COL_LAYOUTS[which]
    n8 = sum(widths) // NDEV
    moves = [(mi, None, mc, 0, p, sc, w) for p, sc, mi, mc, w in _col_pieces(n8, cuts, place)]
    return _colmove(list(mats), [False] * len(mats), [(True, n8, dtype)], moves, tk=256, name=name)[0]


def _tm2hm(a, H):
    T = a.shape[0]
    return a.reshape(T, H, -1).transpose(1, 0, 2)


def _hm2tm(a):
    H, T, P = a.shape
    return a.transpose(1, 0, 2).reshape(T, H * P)


def _pack(parts, dtype, row_mult):
    flat = jnp.concatenate([p.reshape(-1).astype(dtype) for p in parts])
    rows = -(-flat.shape[0] // LANES)
    rows = -(-rows // row_mult) * row_mult
    return jnp.pad(flat, (0, rows * LANES - flat.shape[0])).reshape(rows, LANES)


def _unpack(buf, shapes, lead=()):
    flat = buf.reshape(lead + (-1,))
    out, off = [], 0
    for shp in shapes:
        n = int(np.prod(shp))
        out.append(flat[..., off:off + n].reshape(lead + tuple(shp)))
        off += n
    return out


def _to_slots(full, ax):
    shp = full.shape
    return jnp.moveaxis(full.reshape(shp[:ax] + (NDEV, shp[ax] // NDEV) + shp[ax + 1:]), ax, 0)


def _from_slots(g, ax):
    t = jnp.moveaxis(g, 0, ax)
    shp = t.shape
    return t.reshape(shp[:ax] + (shp[ax] * shp[ax + 1],) + shp[ax + 2:])


def _ffn_perm(a):
    lead = a.shape[:-1]
    h = FFN_TC // 2
    return jnp.swapaxes(a.reshape(lead + (2, FFN // h, h)), -3, -2).reshape(lead + (2 * FFN,))


def _ffn_unperm(a):
    lead = a.shape[:-1]
    h = FFN_TC // 2
    return jnp.swapaxes(a.reshape(lead + (FFN // h, 2, h)), -3, -2).reshape(lead + (2 * FFN,))


def _rope_tables(T):
    half = RDH // 2
    inv = 1.0 / (ROPE_BASE ** (jnp.arange(half, dtype=f32) / half))
    ang = jnp.arange(T, dtype=f32)[:, None] * inv[None, :]
    cos, sin = jnp.cos(ang), jnp.sin(ang)
    cos_t = jnp.tile(jnp.concatenate([cos, cos], axis=1), (1, RH))
    sin_t = jnp.tile(jnp.concatenate([-sin, sin], axis=1), (1, RH))
    return cos_t, sin_t


def _group_avg():
    g = np.arange(RW) // RDH
    return jnp.asarray((g[:, None] == g[None, :]).astype(np.float32) / RDH)


def _head_expand():
    hd = np.arange(SSD_INNER) // SSD_HD
    rows = np.arange(2 * SSD_H)
    ex0 = (rows[:, None] == hd[None, :]).astype(np.float32)
    ex1 = (rows[:, None] == SSD_H + hd[None, :]).astype(np.float32)
    return jnp.asarray(ex0), jnp.asarray(ex1)


def kernel(x, norm_mix_pre, norm_mix_post, norm_ffn_pre, norm_ffn_post, ab_w_in, ab_ret_decay_logit, ab_ret_gn_g, ab_na_rpb, ab_w_out, c_w_in, c_conv_w, c_conv_b, c_dt_bias, c_a_log, c_d_skip, c_norm_g, c_w_out, ffn_w_up, ffn_conv_w, ffn_conv_b, ffn_w_down, loss_target, m_norm_mix_pre, m_norm_mix_post, m_norm_ffn_pre, m_norm_ffn_post, m_ab_w_in, m_ab_ret_decay_logit, m_ab_ret_gn_g, m_ab_na_rpb, m_ab_w_out, m_c_w_in, m_c_conv_w, m_c_conv_b, m_c_dt_bias, m_c_a_log, m_c_d_skip, m_c_norm_g, m_c_w_out, m_ffn_w_up, m_ffn_conv_w, m_ffn_conv_b, m_ffn_w_down, v_norm_mix_pre, v_norm_mix_post, v_norm_ffn_pre, v_norm_ffn_post, v_ab_w_in, v_ab_ret_decay_logit, v_ab_ret_gn_g, v_ab_na_rpb, v_ab_w_out, v_c_w_in, v_c_conv_w, v_c_conv_b, v_c_dt_bias, v_c_a_log, v_c_d_skip, v_c_norm_g, v_c_w_out, v_ffn_w_up, v_ffn_conv_w, v_ffn_conv_b, v_ffn_w_down):
    W = dict(norm_mix_pre=norm_mix_pre, norm_mix_post=norm_mix_post, norm_ffn_pre=norm_ffn_pre, norm_ffn_post=norm_ffn_post, ab_w_in=ab_w_in, ab_ret_decay_logit=ab_ret_decay_logit, ab_ret_gn_g=ab_ret_gn_g, ab_na_rpb=ab_na_rpb, ab_w_out=ab_w_out, c_w_in=c_w_in, c_conv_w=c_conv_w, c_conv_b=c_conv_b, c_dt_bias=c_dt_bias, c_a_log=c_a_log, c_d_skip=c_d_skip, c_norm_g=c_norm_g, c_w_out=c_w_out, ffn_w_up=ffn_w_up, ffn_conv_w=ffn_conv_w, ffn_conv_b=ffn_conv_b, ffn_w_down=ffn_w_down)
    Mo = dict(norm_mix_pre=m_norm_mix_pre, norm_mix_post=m_norm_mix_post, norm_ffn_pre=m_norm_ffn_pre, norm_ffn_post=m_norm_ffn_post, ab_w_in=m_ab_w_in, ab_ret_decay_logit=m_ab_ret_decay_logit, ab_ret_gn_g=m_ab_ret_gn_g, ab_na_rpb=m_ab_na_rpb, ab_w_out=m_ab_w_out, c_w_in=m_c_w_in, c_conv_w=m_c_conv_w, c_conv_b=m_c_conv_b, c_dt_bias=m_c_dt_bias, c_a_log=m_c_a_log, c_d_skip=m_c_d_skip, c_norm_g=m_c_norm_g, c_w_out=m_c_w_out, ffn_w_up=m_ffn_w_up, ffn_conv_w=m_ffn_conv_w, ffn_conv_b=m_ffn_conv_b, ffn_w_down=m_ffn_w_down)
    Vo = dict(norm_mix_pre=v_norm_mix_pre, norm_mix_post=v_norm_mix_post, norm_ffn_pre=v_norm_ffn_pre, norm_ffn_post=v_norm_ffn_post, ab_w_in=v_ab_w_in, ab_ret_decay_logit=v_ab_ret_decay_logit, ab_ret_gn_g=v_ab_ret_gn_g, ab_na_rpb=v_ab_na_rpb, ab_w_out=v_ab_w_out, c_w_in=v_c_w_in, c_conv_w=v_c_conv_w, c_conv_b=v_c_conv_b, c_dt_bias=v_c_dt_bias, c_a_log=v_c_a_log, c_d_skip=v_c_d_skip, c_norm_g=v_c_norm_g, c_w_out=v_c_w_out, ffn_w_up=v_ffn_w_up, ffn_conv_w=v_ffn_conv_w, ffn_conv_b=v_ffn_conv_b, ffn_w_down=v_ffn_w_down)
    return _train_step(x[0], loss_target[0], W, Mo, Vo)


def _train_step(x, tgt, W, Mo, Vo):
    T = x.shape[0]
    rows = T // GRID_W

    col = lambda d, n, dt: d[n].reshape(-1, d[n].shape[-1]).astype(dt)
    rows_of = lambda d, dt: jnp.concatenate([col(d, n, dt) for n in ROW_SHARDED], axis=0)
    small = _pack([W[n] for n, _ in SHARDED[N_BIG:]], f32, 8)
    gat = _all_gather([col(W, n, bf16) for n in COL_SHARDED] + [rows_of(W, bf16), small], name="gather_weights")
    per_layer = lambda m: m.reshape(-1, D, m.shape[-1])
    w_ab_in = per_layer(_cols_from_slots(gat[0], "ab_w_in", name="cols_ab_w_in")[0])
    w_zx, w_dt = [per_layer(m) for m in _cols_from_slots(gat[1], "c_w_in", name="cols_c_w_in")]
    w_up = per_layer(_cols_from_slots(gat[2], "ffn_w_up", name="cols_ffn_w_up")[0])
    full, off = {}, 0
    for n in ROW_SHARDED:
        L, r = W[n].shape[0], W[n].shape[1]
        full[n] = jnp.swapaxes(gat[3][:, off:off + L * r].reshape(NDEV, L, r, D), 0, 1).reshape(L, NDEV * r, D)
        off += L * r
    gs = _unpack(gat[4], [W[n].shape for n, _ in SHARDED[N_BIG:]], (NDEV,))
    full.update({n: _from_slots(g, ax) for (n, ax), g in zip(SHARDED[N_BIG:], gs)})
    w_ab_out, w_c_out, w_down = full["ab_w_out"], full["c_w_out"], full["ffn_w_down"]
    c_cw8 = [_pad8(full["c_conv_w"][i]) for i in range(2)]
    c_cb = [full["c_conv_b"][i][None] for i in range(2)]
    c_ng = [full["c_norm_g"][i][None] for i in range(2)]
    f_cw8 = [_pad8(_ffn_perm(full["ffn_conv_w"][l])) for l in range(DEPTH)]
    f_cb = [_ffn_perm(W["ffn_conv_b"][l])[None] for l in range(DEPTH)]

    g1 = [W["norm_mix_pre"][l][None] for l in range(DEPTH)]
    g2 = [W["norm_mix_post"][l][None] for l in range(DEPTH)]
    g3 = [W["norm_ffn_pre"][l][None] for l in range(DEPTH)]
    g4 = [W["norm_ffn_post"][l][None] for l in range(DEPTH)]
    cos_t, sin_t = _rope_tables(T)
    gavg = _group_avg()
    ex0, ex1 = _head_expand()

    def log_gamma(logit):
        return -jax.nn.softplus(-logit)

    def ret_decays(lg):
        return [jnp.broadcast_to(lg[d][:, None, None], (RH, T, 8)) for d in range(2)]

    saved = []
    xs_ = x
    hn = _rowwise("norm_first", _f_first, [(x, D, 0)], [], [(g1[0], D, 0)], [], [(D, bf16)], tm=256)[0]
    for l in range(DEPTH):
        i = l // 2
        sv = dict(x=xs_, hn=hn)
        if l % 2 == 0:
            proj = _mm_nn(hn, w_ab_in[i], name=f"ab_in_{l}")
            qr, kr = _rowwise(f"ret_prep_{l}", _f_rprep, [(proj, RW, 0), (proj, RW, 1)], [(cos_t, RW, 0), (sin_t, RW, 0)], [], [],
                              [(RW, f32), (RW, f32)], tm=256)
            lg, lg_vjp = jax.vjp(log_gamma, W["ab_ret_decay_logit"][i])
            a_f, a_b = ret_decays(lg)
            rscan = dict(N=RDH, Hg=1, P=RDH, vcol=2)
            yf_t, hsf = _scan_fwd(qr, kr, proj, a_f, rev=False, name=f"ret_scan_f_{l}", **rscan)
            yb_t, hsb = _scan_fwd(qr, kr, proj, a_b, rev=True, name=f"ret_scan_b_{l}", **rscan)
            gn = W["ab_ret_gn_g"][i][None]
            ret = _rowwise(f"ret_post_{l}", _f_rpost, [(yf_t, RW, 0), (yb_t, RW, 0), (proj, RW, 3)], [], [(gn, RW, 0)], [gavg],
                           [(RW, bf16)], tm=256)[0]
            nqkv = proj[:, 4 * RW:].astype(bf16)
            ncols = dict(qcol=0, kcol=NAW // 128, vcol=2 * NAW // 128)
            bias, bias_vjp = jax.vjp(functools.partial(_na_bias, rows=rows), W["ab_na_rpb"][i])
            na_o, na_l = _na_fwd(nqkv, nqkv, nqkv, bias, name=f"na_fwd_{l}", **ncols)
            cat = jnp.concatenate([ret, na_o.astype(bf16)], axis=1)
            mo = _mm_nn(cat, w_ab_out[i], name=f"ab_out_{l}")
            sv.update(proj=proj, qr=qr, kr=kr, a_f=a_f, a_b=a_b, hsf=hsf, hsb=hsb, yf_t=yf_t, yb_t=yb_t, gn=gn, rscan=rscan,
                      nqkv=nqkv, ncols=ncols, bias=bias, bias_vjp=bias_vjp, lg_vjp=lg_vjp, na_o=na_o, na_l=na_l, cat=cat)
        else:
            zx = _mm_nn(hn, w_zx[i], name=f"c_in_{l}")
            dtr = _mm_nn(hn, w_dt[i], name=f"c_in_dt_{l}")
            xa = _conv(zx, c_cw8[i], c_cb[i], mode="silu", W=SSD_CONV, name=f"c_conv_{l}", C=SSD_XBC, xbase=SSD_INNER // 512)
            dtb, alog = W["c_dt_bias"][i].reshape(1, 2 * SSD_H), W["c_a_log"][i].reshape(1, 2 * SSD_H)
            vf, vb, la = _rowwise(f"ssd_prep_{l}", _f_sprep, [(xa, SSD_INNER, 0), (dtr, 2 * SSD_H, 0)], [],
                                  [(dtb, 2 * SSD_H, 0), (alog, 2 * SSD_H, 0)], [ex0, ex1],
                                  [(SSD_INNER, f32), (SSD_INNER, f32), (2 * SSD_H, f32)], tm=128)
            a_f = la[:, :SSD_H].reshape(T, SSD_G, SSD_HPG).transpose(1, 0, 2)
            a_b = la[:, SSD_H:].reshape(T, SSD_G, SSD_HPG).transpose(1, 0, 2)
            sscan = dict(N=SSD_N, Hg=SSD_HPG, P=SSD_HD, qcol=(SSD_INNER + SSD_G * SSD_N) // SSD_N, kcol=SSD_INNER // SSD_N)
            yf_t, hsf = _scan_fwd(xa, xa, vf, a_f, rev=False, name=f"ssd_scan_f_{l}", **sscan)
            yb_t, hsb = _scan_fwd(xa, xa, vb, a_b, rev=True, name=f"ssd_scan_b_{l}", **sscan)
            dsk = jnp.repeat(W["c_d_skip"][i], SSD_HD)[None]
            yo = _rowwise(f"ssd_post_{l}", _f_spost, [(yf_t, 512, 0), (yb_t, 512, 0), (xa, 512, 0), (zx, 512, 0)], [],
                          [(dsk, 512, 0), (c_ng[i], 512, 0)], [], [(512, bf16)], tm=256, J=SSD_G)[0]
            mo = _mm_nn(yo, w_c_out[i], name=f"c_out_{l}")
            sv.update(zx=zx, dtr=dtr, xa=xa, dtb=dtb, alog=alog, a_f=a_f, a_b=a_b, vf=vf, vb=vb, sscan=sscan,
                      hsf=hsf, hsb=hsb, yf_t=yf_t, yb_t=yb_t, dsk=dsk, yo=yo)
        x1, hf = _rowwise(f"norm_mid_{l}", _f_mid, [(xs_, D, 0), (mo, D, 0)], [], [(g2[l], D, 0), (g3[l], D, 0)], [],
                          [(D, f32), (D, bf16)], tm=256)
        pre = _mm_nn(hf, w_up[l], name=f"ffn_up_{l}")
        act = _conv(pre, f_cw8[l], f_cb[l], mode="geglu", W=FFN_CONV, name=f"ffn_conv_{l}", C=2 * FFN, tc=FFN_TC, out_dtype=bf16)
        fo = _mm_nn(act, w_down[l], name=f"ffn_down_{l}")
        sv.update(mo=mo, x1=x1, hf=hf, pre=pre, act=act, fo=fo)
        if l < DEPTH - 1:
            xs_, hn = _rowwise(f"norm_end_{l}", _f_end, [(x1, D, 0), (fo, D, 0)], [], [(g4[l], D, 0), (g1[l + 1], D, 0)], [],
                               [(D, f32), (D, bf16)], tm=256)
        else:
            xs_ = _rowwise(f"norm_end_{l}", _f_last, [(x1, D, 0), (fo, D, 0)], [], [(g4[l], D, 0)], [], [(D, f32)], tm=256)[0]
        saved.append(sv)

    dx, lpart = _loss_call(xs_, tgt)
    loss = lax.psum(lpart[0, 0], ("x", "y", "c"))

    G = {n: [None] * W[n].shape[0] for n in WEIGHTS}
    dhn = None
    for l in reversed(range(DEPTH)):
        i = l // 2
        sv = saved[l]
        if l == DEPTH - 1:
            (dx1, dfo), (dg4,) = _rowwise_bwd(f"norm_end_bwd_{l}", _f_last, [(sv["x1"], D, 0), (sv["fo"], D, 0)], [],
                                              [(g4[l], D, 0)], [], [(dx, D, 0)], [f32, bf16], tm=256)
        else:
            (dx1, dfo), (dg4, dg1n) = _rowwise_bwd(f"norm_end_bwd_{l}", _f_end, [(sv["x1"], D, 0), (sv["fo"], D, 0)], [],
                                                   [(g4[l], D, 0), (g1[l + 1], D, 0)], [], [(dx, D, 0), (dhn, D, 0)],
                                                   [f32, bf16], tm=256)
            G["norm_mix_pre"][l + 1] = dg1n[0]
        G["norm_ffn_post"][l] = dg4[0]
        dact = _mm_nt(dfo, w_down[l], name=f"ffn_down_dx_{l}")
        G["ffn_w_down"][l] = _mm_tn(sv["act"], dfo, name=f"ffn_down_dw_{l}")
        dpre, dfw, dfb = _conv_bwd(sv["pre"], f_cw8[l], f_cb[l], dact, mode="geglu", W=FFN_CONV, name=f"ffn_conv_bwd_{l}",
                                   C=2 * FFN, tc=FFN_TC)
        dhf = _mm_nt(dpre, w_up[l], name=f"ffn_up_dx_{l}")
        G["ffn_w_up"][l] = _cols_to_slots([_mm_tn(sv["hf"], dpre, name=f"ffn_up_dw_{l}")], "ffn_w_up", bf16, name=f"slots_ffn_up_{l}")
        G["ffn_conv_w"][l] = _ffn_unperm(dfw[:FFN_CONV])
        G["ffn_conv_b"][l] = _ffn_unperm(dfb[0])
        (dxl, dmo), (dg2, dg3) = _rowwise_bwd(f"norm_mid_bwd_{l}", _f_mid, [(sv["x"], D, 0), (sv["mo"], D, 0)], [],
                                              [(g2[l], D, 0), (g3[l], D, 0)], [], [(dx1, D, 0), (dhf, D, 0)], [f32, bf16], tm=256)
        G["norm_mix_post"][l] = dg2[0]
        G["norm_ffn_pre"][l] = dg3[0]
        if l % 2 == 0:
            dcat = _mm_nt(dmo, w_ab_out[i], name=f"ab_out_dx_{l}")
            G["ab_w_out"][i] = _mm_tn(sv["cat"], dmo, name=f"ab_out_dw_{l}")
            (dyf, _, drg), (dgn,) = _rowwise_bwd(
                f"ret_post_bwd_{l}", _f_rpost, [(sv["yf_t"], RW, 0), (sv["yb_t"], RW, 0), (sv["proj"], RW, 3)], [],
                [(sv["gn"], RW, 0)], [gavg], [(dcat, RW, 0)], [f32, f32, bf16], tm=256)
            G["ab_ret_gn_g"][i] = dgn[0]
            dqf, dkf, dvf, daf = _scan_bwd(sv["qr"], sv["kr"], sv["proj"], sv["a_f"], sv["hsf"], dyf, rev=False,
                                           name=f"ret_scan_f_bwd_{l}", **sv["rscan"])
            dqb, dkb, dvb, dab = _scan_bwd(sv["qr"], sv["kr"], sv["proj"], sv["a_b"], sv["hsb"], dyf, rev=True,
                                           name=f"ret_scan_b_bwd_{l}", **sv["rscan"])
            dq_t, dk_t, drv = dqf + dqb, dkf + dkb, (dvf + dvb).astype(bf16)
            (drq, drk), _ = _rowwise_bwd(f"ret_prep_bwd_{l}", _f_rprep, [(sv["proj"], RW, 0), (sv["proj"], RW, 1)],
                                         [(cos_t, RW, 0), (sin_t, RW, 0)], [], [], [(dq_t, RW, 0), (dk_t, RW, 0)], [bf16, bf16], tm=256)
            da_cols = jnp.concatenate([_hm2tm(daf), _hm2tm(dab)], axis=1)
            dlg = _colsum(da_cols, name=f"ret_decay_sum_{l}").reshape(2, RH, 8)[:, :, 0]
            G["ab_ret_decay_logit"][i] = sv["lg_vjp"](dlg)[0]
            dnq, dnk, dnv, dbias = _na_bwd(sv["nqkv"], sv["nqkv"], sv["nqkv"], sv["bias"], sv["na_o"], sv["na_l"], dcat,
                                           docol=RW // 128, name=f"na_bwd_{l}", **sv["ncols"])
            G["ab_na_rpb"][i] = sv["bias_vjp"](dbias)[0]
            dproj = jnp.concatenate([drq, drk, drv, drg] + [t.astype(bf16) for t in (dnq, dnk, dnv)], axis=1)
            dhn = _mm_nt(dproj, w_ab_in[i], name=f"ab_in_dx_{l}")
            G["ab_w_in"][i] = _cols_to_slots([_mm_tn(sv["hn"], dproj, name=f"ab_in_dw_{l}")], "ab_w_in", bf16, name=f"slots_ab_in_{l}")
        else:
            dyo = _mm_nt(dmo, w_c_out[i], name=f"c_out_dx_{l}")
            G["c_w_out"][i] = _mm_tn(sv["yo"], dmo, name=f"c_out_dw_{l}")
            (dyf, _, dxs1, dz), (ddsk, dng) = _rowwise_bwd(
                f"ssd_post_bwd_{l}", _f_spost, [(sv["yf_t"], 512, 0), (sv["yb_t"], 512, 0), (sv["xa"], 512, 0), (sv["zx"], 512, 0)],
                [], [(sv["dsk"], 512, 0), (c_ng[i], 512, 0)], [], [(dyo, 512, 0)], [f32, f32, f32, bf16], tm=256, J=SSD_G)
            G["c_d_skip"][i] = ddsk.reshape(SSD_H, SSD_HD).sum(axis=1)
            G["c_norm_g"][i] = dng[0]
            dqf, dkf, dvf, daf = _scan_bwd(sv["xa"], sv["xa"], sv["vf"], sv["a_f"], sv["hsf"], dyf, rev=False,
                                           name=f"ssd_scan_f_bwd_{l}", **sv["sscan"])
            dqb, dkb, dvb, dab = _scan_bwd(sv["xa"], sv["xa"], sv["vb"], sv["a_b"], sv["hsb"], dyf, rev=True,
                                           name=f"ssd_scan_b_bwd_{l}", **sv["sscan"])
            dla = jnp.concatenate([daf.transpose(1, 0, 2).reshape(T, SSD_H), dab.transpose(1, 0, 2).reshape(T, SSD_H)], axis=1)
            (dxs2, ddtr), (ddtb, dalog) = _rowwise_bwd(
                f"ssd_prep_bwd_{l}", _f_sprep, [(sv["xa"], SSD_INNER, 0), (sv["dtr"], 2 * SSD_H, 0)], [],
                [(sv["dtb"], 2 * SSD_H, 0), (sv["alog"], 2 * SSD_H, 0)], [ex0, ex1],
                [(dvf, SSD_INNER, 0), (dvb, SSD_INNER, 0), (dla, 2 * SSD_H, 0)], [f32, bf16], tm=128)
            G["c_dt_bias"][i] = ddtb.reshape(2, SSD_H)
            G["c_a_log"][i] = dalog.reshape(2, SSD_H)
            dxa = jnp.concatenate([dxs1 + dxs2, dkf + dkb, dqf + dqb], axis=1)
            dxbc, dcw, dcb = _conv_bwd(sv["zx"], c_cw8[i], c_cb[i], dxa, mode="silu", W=SSD_CONV, name=f"c_conv_bwd_{l}",
                                       C=SSD_XBC, xbase=SSD_INNER // 512)
            G["c_conv_w"][i] = dcw[:SSD_CONV]
            G["c_conv_b"][i] = dcb[0]
            dzx = jnp.concatenate([dz, dxbc], axis=1)
            t1 = _mm_nt(ddtr, w_dt[i], name=f"c_in_dt_dx_{l}")
            dhn = _mm_nt(dzx, w_zx[i], add=t1, name=f"c_in_dx_{l}")
            G["c_w_in"][i] = _cols_to_slots([_mm_tn(sv["hn"], dzx, name=f"c_in_dw_{l}"), _mm_tn(sv["hn"], ddtr, name=f"c_in_dt_dw_{l}")],
                                            "c_w_in", bf16, name=f"slots_c_in_{l}")
        dx = dxl
    (grad_x,), (dg1,) = _rowwise_bwd("norm_first_bwd", _f_first_bwd, [(x, D, 0)], [], [(g1[0], D, 0)], [], [(dx, D, 0), (dhn, D, 0)],
                                     [f32], tm=256)
    G["norm_mix_pre"][0] = dg1[0]

    small_names = [n for n, _ in SHARDED[N_BIG:]]
    col_slots = [jnp.concatenate(G[n], axis=1) for n in COL_SHARDED]
    row_slots = jnp.concatenate([g.reshape(NDEV, -1, D).astype(bf16) for n in ROW_SHARDED for g in G[n]], axis=1)
    small_slots = _pack_slots([_to_slots(jnp.stack(G[n]), ax) for n, ax in SHARDED[N_BIG:]], 8)
    ar = _pack([jnp.stack(G[n]) for n in REPLICATED], f32, 8)
    parts = [(a, True) for a in col_slots + [row_slots, small_slots]] + [(ar, False)]
    from_sib = _to_sibling(parts, name="grads_to_sibling")
    tiles = [256, 256, 256, 64, small_slots.shape[1], ar.shape[0]]
    chip = [_add_partials(a, b, per_slot=ps, tr=t, name=f"grads_add_{j}")
            for j, ((a, ps), b, t) in enumerate(zip(parts, from_sib, tiles))]
    exch = _to_chips([(a, ps) for a, (_, ps) in zip(chip, parts)], name="grads_to_chips")
    pk = lambda d, names: _pack([d[n] for n in names], f32, 8)
    upd = [_adamw(exch[j], col(W, n, f32), col(Mo, n, f32), col(Vo, n, f32), name=f"adamw_{n}", tr=256)
           for j, n in enumerate(COL_SHARDED)]
    upd_rows = _adamw(exch[3], rows_of(W, f32), rows_of(Mo, f32), rows_of(Vo, f32), name="adamw_rows", tr=64)
    upd_small = _adamw(exch[4], pk(W, small_names), pk(Mo, small_names), pk(Vo, small_names), name="adamw_small",
                       tr=small_slots.shape[1])
    upd_rep = _adamw(exch[5], pk(W, REPLICATED), pk(Mo, REPLICATED), pk(Vo, REPLICATED), name="adamw_replicated", tr=ar.shape[0])
    res = []
    for k in range(4):
        d = {n: upd[j][k].reshape(W[n].shape) for j, n in enumerate(COL_SHARDED)}
        off = 0
        for n in ROW_SHARDED:
            cnt = W[n].shape[0] * W[n].shape[1]
            d[n] = upd_rows[k][off:off + cnt].reshape(W[n].shape)
            off += cnt
        d.update(zip(small_names, _unpack(upd_small[k], [W[n].shape for n in small_names])))
        d.update(zip(REPLICATED, _unpack(upd_rep[k], [W[n].shape for n in REPLICATED])))
        res.append(d)
    outs = [loss, grad_x[None]]
    for k in range(4):
        outs += [res[k][n] for n in WEIGHTS]
    return tuple(outs)


def _pack_slots(slot_arrays, row_mult):
    flat = jnp.concatenate([a.reshape(NDEV, -1) for a in slot_arrays], axis=1)
    rows = -(-flat.shape[1] // LANES)
    rows = -(-rows // row_mult) * row_mult
    return jnp.pad(flat, ((0, 0), (0, rows * LANES - flat.shape[1]))).reshape(NDEV, rows, LANES)
```

```python
import functools
import numpy as np
import jax
import jax.numpy as jnp
from jax import lax
from jax.experimental import pallas as pl
from jax.experimental.pallas import tpu as pltpu

f32, bf16 = jnp.float32, jnp.bfloat16
S = jax.ShapeDtypeStruct
HI = lax.Precision.HIGHEST

D = 1024
DEPTH = 4
GRID_W = 64
CHUNK = 128
EPS = 1e-6
RH, RDH, RW = 8, 64, 512
NAH, NADH, NAW = 8, 64, 512
NA_WR, NA_WC = 8, 16
NA_QROWS = 8
NA_KROWS = 16
NA_PAIR = 2
SSD_INNER, SSD_HD, SSD_H, SSD_G, SSD_HPG, SSD_N, SSD_CONV = 2048, 64, 32, 4, 8, 128, 5
SSD_XBC = SSD_INNER + 2 * SSD_G * SSD_N
FFN, FFN_CONV = 2816, 3
FFN_TC = 512
SCAN_HEADS_PER_STEP = 8
ROPE_BASE = 10000.0
LR, B1, B2, AEPS, WD, STEP = 0.001, 0.9, 0.999, 1e-08, 0.01, 10
NDEV = 8
LANES = 128
VMEM_LIMIT = 56 * 1024 * 1024
MM_BLOCK_BYTES = 6 * 1024 * 1024

NT = (((1,), (1,)), ((), ()))
TN = (((0,), (0,)), ((), ()))

SHARDED = [("ab_w_in", 2), ("ab_w_out", 1), ("c_w_in", 2), ("c_w_out", 1), ("ffn_w_up", 2), ("ffn_w_down", 1),
           ("c_conv_w", 2), ("c_conv_b", 1), ("c_norm_g", 1), ("ffn_conv_w", 2)]
N_BIG = 6
COL_SHARDED = ["ab_w_in", "c_w_in", "ffn_w_up"]
ROW_SHARDED = ["ab_w_out", "c_w_out", "ffn_w_down"]
REPLICATED = ["norm_mix_pre", "norm_mix_post", "norm_ffn_pre", "norm_ffn_post", "ab_ret_decay_logit", "ab_ret_gn_g",
              "ab_na_rpb", "c_dt_bias", "c_a_log", "c_d_skip", "ffn_conv_b"]
WEIGHTS = ["norm_mix_pre", "norm_mix_post", "norm_ffn_pre", "norm_ffn_post", "ab_w_in", "ab_ret_decay_logit",
           "ab_ret_gn_g", "ab_na_rpb", "ab_w_out", "c_w_in", "c_conv_w", "c_conv_b", "c_dt_bias", "c_a_log", "c_d_skip",
           "c_norm_g", "c_w_out", "ffn_w_up", "ffn_conv_w", "ffn_conv_b", "ffn_w_down"]


def _params(sem=None):
    return pltpu.CompilerParams(dimension_semantics=sem, vmem_limit_bytes=VMEM_LIMIT)


def _mm_nn(a, w, *, name, tm=1024, tn=512, out_dtype=f32):
    M, K = a.shape
    N = w.shape[1]
    tn = min(tn, N)

    def body(a_ref, w_ref, o_ref):
        o_ref[...] = jnp.dot(a_ref[...], w_ref[...], preferred_element_type=f32).astype(o_ref.dtype)

    return pl.pallas_call(
        body, name=name, grid=(M // tm, N // tn),
        in_specs=[pl.BlockSpec((tm, K), lambda i, j: (i, 0)), pl.BlockSpec((K, tn), lambda i, j: (0, j))],
        out_specs=pl.BlockSpec((tm, tn), lambda i, j: (i, j)),
        out_shape=S((M, N), out_dtype), compiler_params=_params(("parallel", "parallel")))(a, w)


def _mm_nt(dy, w, *, name, add=None, tm=512):
    M, N = dy.shape
    K = w.shape[0]
    tk = next((t for t in (1024, 1408, 512, 256, 128) if K % t == 0 and (t <= 512 or t * N * 2 <= MM_BLOCK_BYTES)), K)

    def body(*refs):
        if add is None:
            d_ref, w_ref, o_ref = refs
            o_ref[...] = lax.dot_general(d_ref[...], w_ref[...], NT, preferred_element_type=f32)
        else:
            d_ref, w_ref, a_ref, o_ref = refs
            o_ref[...] = lax.dot_general(d_ref[...], w_ref[...], NT, preferred_element_type=f32) + a_ref[...]

    in_specs = [pl.BlockSpec((tm, N), lambda i, j: (i, 0)), pl.BlockSpec((tk, N), lambda i, j: (j, 0))]
    args = [dy, w]
    if add is not None:
        in_specs.append(pl.BlockSpec((tm, tk), lambda i, j: (i, j)))
        args.append(add)
    return pl.pallas_call(
        body, name=name, grid=(M // tm, K // tk), in_specs=in_specs,
        out_specs=pl.BlockSpec((tm, tk), lambda i, j: (i, j)),
        out_shape=S((M, K), f32), compiler_params=_params(("parallel", "parallel")))(*args)


def _mm_tn(a, dy, *, name, tt=1024):
    M, K = a.shape
    N = dy.shape[1]
    tk = K if K <= 1024 else (1024 if K % 1024 == 0 else K // 2)
    tn = min(512, N)
    tt = min(tt, M)

    def body(a_ref, d_ref, o_ref):
        t = pl.program_id(2)
        part = lax.dot_general(a_ref[...], d_ref[...], TN, preferred_element_type=f32)

        @pl.when(t == 0)
        def _():
            o_ref[...] = part

        @pl.when(t > 0)
        def _():
            o_ref[...] += part

    return pl.pallas_call(
        body, name=name, grid=(K // tk, N // tn, M // tt),
        in_specs=[pl.BlockSpec((tt, tk), lambda k, n, t: (t, k)), pl.BlockSpec((tt, tn), lambda k, n, t: (t, n))],
        out_specs=pl.BlockSpec((tk, tn), lambda k, n, t: (k, n)),
        out_shape=S((K, N), f32), compiler_params=_params(("parallel", "parallel", "arbitrary")))(a, dy)


def _tile_spec(tm, width, base):
    return pl.BlockSpec((tm, width), lambda j, i: (i, base + j))


def _par_spec(width, base):
    return pl.BlockSpec((1, width), lambda j, i: (0, base + j))


def _full_spec(a):
    nd = a.ndim
    return pl.BlockSpec(a.shape, lambda j, i: (0,) * nd)


def _rowwise(name, f, tiles, ctiles, params, consts, outs, *, tm, J=1):
    T = tiles[0][0].shape[0]
    nt, nct, npar, nc = len(tiles), len(ctiles), len(params), len(consts)

    def body(*refs):
        tv = [r[...].astype(f32) for r in refs[:nt + nct]]
        pv = [r[...] for r in refs[nt + nct:nt + nct + npar + nc]]
        res = f(*tv, *pv)
        for o, v in zip(refs[nt + nct + npar + nc:], res):
            o[...] = v.astype(o.dtype)

    in_specs = ([_tile_spec(tm, w, b) for _, w, b in tiles + ctiles] + [_par_spec(w, b) for _, w, b in params]
                + [_full_spec(c) for c in consts])
    return pl.pallas_call(
        body, name=name, grid=(J, T // tm), in_specs=in_specs,
        out_specs=[_tile_spec(tm, w, 0) for w, _ in outs],
        out_shape=[S((T, J * w), dt) for w, dt in outs],
        compiler_params=_params(("parallel", "parallel")))(
            *[a for a, _, _ in tiles + ctiles], *[a for a, _, _ in params], *consts)


def _rowwise_bwd(name, f, tiles, ctiles, params, consts, douts, dtile_dtypes, *, tm, J=1):
    T = tiles[0][0].shape[0]
    nt, nct, npar, nc, nd = len(tiles), len(ctiles), len(params), len(consts), len(douts)

    def body(*refs):
        i = pl.program_id(1)
        k = 0
        tv = [r[...].astype(f32) for r in refs[k:k + nt]]; k += nt
        cv = [r[...].astype(f32) for r in refs[k:k + nct]]; k += nct
        pv = [r[...] for r in refs[k:k + npar]]; k += npar
        kv = [r[...] for r in refs[k:k + nc]]; k += nc
        dv = [r[...].astype(f32) for r in refs[k:k + nd]]; k += nd
        dt_refs = refs[k:k + nt]; k += nt
        dp_refs = refs[k:k + npar]
        _, vjp = jax.vjp(lambda tv_, pv_: tuple(f(*tv_, *cv, *pv_, *kv)), tv, pv)
        dts, dps = vjp(tuple(dv))
        for r, g in zip(dt_refs, dts):
            r[...] = g.astype(r.dtype)
        for r, g in zip(dp_refs, dps):
            @pl.when(i == 0)
            def _(r=r, g=g):
                r[...] = g

            @pl.when(i > 0)
            def _(r=r, g=g):
                r[...] += g

    in_specs = ([_tile_spec(tm, w, b) for _, w, b in tiles + ctiles] + [_par_spec(w, b) for _, w, b in params]
                + [_full_spec(c) for c in consts] + [_tile_spec(tm, w, b) for _, w, b in douts])
    res = pl.pallas_call(
        body, name=name, grid=(J, T // tm), in_specs=in_specs,
        out_specs=[_tile_spec(tm, w, 0) for _, w, _ in tiles] + [_par_spec(w, b) for _, w, b in params],
        out_shape=[S((T, J * w), dt) for (_, w, _), dt in zip(tiles, dtile_dtypes)] + [S(a.shape, f32) for a, _, _ in params],
        compiler_params=_params(("parallel", "arbitrary")))(
            *[a for a, _, _ in tiles + ctiles], *[a for a, _, _ in params], *consts, *[a for a, _, _ in douts])
    return res[:nt], res[nt:]


def _rms(x, g):
    return x * lax.rsqrt(jnp.mean(x * x, axis=-1, keepdims=True) + EPS) * g


def _f_first(x, g1):
    return (_rms(x, g1),)


def _f_first_bwd(x, g1):
    return (x, _rms(x, g1))


def _f_mid(x, m, g2, g3):
    x1 = x + _rms(m, g2)
    return (x1, _rms(x1, g3))


def _f_end(x1, fo, g4, g1n):
    x2 = x1 + _rms(fo, g4)
    return (x2, _rms(x2, g1n))


def _f_last(x1, fo, g4):
    return (x1 + _rms(fo, g4),)


@jax.custom_vjp
def _swap_halves(x):
    c = x.shape[1]
    lane = lax.broadcasted_iota(jnp.int32, x.shape, 1) % RDH
    return jnp.where(lane < RDH // 2, pltpu.roll(x, c - RDH // 2, axis=1), pltpu.roll(x, RDH // 2, axis=1))


_swap_halves.defvjp(lambda x: (_swap_halves(x), None), lambda _, g: (_swap_halves(g),))


def _f_rprep(rq, rk, cos, sin):
    rot = lambda t: t * cos + _swap_halves(t) * sin
    return (rot(rq), rot(rk) * (RDH ** -0.5))


def _f_rpost(yf, yb, rg, gn, gavg):
    y = yf + yb
    mu = jnp.dot(y, gavg, precision=HI, preferred_element_type=f32)
    yc = y - mu
    var = jnp.dot(yc * yc, gavg, precision=HI, preferred_element_type=f32)
    return (jax.nn.silu(rg) * (yc * lax.rsqrt(var + EPS) * gn),)


def _f_sprep(xs, dtr, dtb, alog, ex0, ex1):
    dt = jax.nn.softplus(dtr + dtb)
    la = dt * (-jnp.exp(alog))
    e0 = jnp.dot(dt, ex0, precision=HI, preferred_element_type=f32)
    e1 = jnp.dot(dt, ex1, precision=HI, preferred_element_type=f32)
    return (xs * e0, xs * e1, la)


def _f_spost(yf, yb, xs, z, dsk, ng):
    y = (yf + yb + xs * dsk) * jax.nn.silu(z)
    y = y * lax.rsqrt(jnp.mean(y * y, axis=-1, keepdims=True) + EPS)
    return (y * ng,)


def _loss_call(y, tgt, *, tm=256):
    T = y.shape[0]

    def body(y_ref, t_ref, dy_ref, l_ref):
        i = pl.program_id(0)
        e = y_ref[...] - t_ref[...]
        dy_ref[...] = e * (1.0 / D)
        part = jnp.zeros((8, LANES), f32) + 0.5 * jnp.sum(jnp.mean(e * e, axis=-1, keepdims=True))

        @pl.when(i == 0)
        def _():
            l_ref[...] = part

        @pl.when(i > 0)
        def _():
            l_ref[...] += part

    return pl.pallas_call(
        body, name="loss_head", grid=(T // tm,),
        in_specs=[pl.BlockSpec((tm, D), lambda i: (i, 0))] * 2,
        out_specs=[pl.BlockSpec((tm, D), lambda i: (i, 0)), pl.BlockSpec((8, LANES), lambda i: (0, 0))],
        out_shape=[S((T, D), f32), S((8, LANES), f32)], compiler_params=_params(("arbitrary",)))(y, tgt)


def _colsum(x, *, name, tm=512):
    T, C = x.shape

    def body(x_ref, o_ref):
        i = pl.program_id(0)
        part = jnp.sum(x_ref[...], axis=0, keepdims=True)

        @pl.when(i == 0)
        def _():
            o_ref[...] = part

        @pl.when(i > 0)
        def _():
            o_ref[...] += part

    return pl.pallas_call(
        body, name=name, grid=(T // tm,), in_specs=[pl.BlockSpec((tm, C), lambda i: (i, 0))],
        out_specs=pl.BlockSpec((1, C), lambda i: (0, 0)), out_shape=S((1, C), f32),
        compiler_params=_params(("arbitrary",)))(x)


def _nn(a, b):
    return jnp.dot(a, b, preferred_element_type=f32)


def _nt(a, b):
    return lax.dot_general(a, b, NT, preferred_element_type=f32)


@jax.custom_vjp
def _mm_lt(a, a_t, b):
    return _nn(a_t, b)


_mm_lt.defvjp(lambda a, a_t, b: (_nn(a_t, b), (a, b)),
              lambda res, g: (jnp.zeros_like(res[0]), _nt(g, res[1]), _nn(res[0], g)))


@jax.custom_vjp
def _mm_rt(a, a_t, b):
    return _nn(a, b)


_mm_rt.defvjp(lambda a, a_t, b: (_nn(a, b), (a_t, b)),
              lambda res, g: (_nt(g, res[1]), jnp.zeros_like(res[0]), _nn(res[0], g)))


@jax.custom_vjp
def _masked_mm(s, s_t, d, d_t, v):
    return _nn(s * d, v)


def _masked_mm_bwd(res, g):
    s, s_t, d, d_t, v = res
    da = _nt(g, v)
    return (da * d, jnp.zeros_like(s_t), da * s, jnp.zeros_like(d_t), _nn(s_t * d_t, g))


_masked_mm.defvjp(lambda s, s_t, d, d_t, v: (_nn(s * d, v), (s, s_t, d, d_t, v)), _masked_mm_bwd)


def _scan_step(h, q, k, v, a, rev, for_vjp=False):
    L = q.shape[0]
    Hg = v.shape[0]
    ii = lax.broadcasted_iota(jnp.int32, (L, L), 0)
    jj = lax.broadcasted_iota(jnp.int32, (L, L), 1)
    if rev:
        tri, tri_t, dmask, dmask_t = (jj >= ii), (ii >= jj), (jj > ii), (ii > jj)
    else:
        tri, tri_t, dmask, dmask_t = (jj <= ii), (ii <= jj), (jj <= ii), (ii <= jj)
    cs = jnp.dot(tri.astype(f32), a, precision=HI, preferred_element_type=f32)
    cs_t = lax.dot_general(a, tri_t.astype(f32), TN, precision=HI, preferred_element_type=f32)
    tot = jnp.sum(a, axis=0, keepdims=True)
    qk = _nt(q, k)
    k_t = k.T
    if for_vjp:
        q_t = lax.stop_gradient(q.T)
        qk_t = lax.stop_gradient(_nt(k, q))
    P = v.shape[2]
    hs, ys = [], []
    for hh in range(Hg):
        c_col, c_row = jnp.broadcast_to(cs[:, hh:hh + 1], (L, L)), cs_t[hh:hh + 1, :]
        dec = jnp.exp(jnp.where(dmask, c_col - c_row, -1e30))
        t_all = tot[:, hh:hh + 1]
        e_in, e_out = jnp.exp(c_col)[:, :P], jnp.exp(t_all - c_col)[:, :P]
        w = v[hh] * e_out
        if for_vjp:
            dec_t = lax.stop_gradient(jnp.exp(jnp.where(dmask_t, c_row - c_col, -1e30)))
            y = _masked_mm(qk, qk_t, dec, dec_t, v[hh]) + _mm_rt(q, q_t, h[hh]) * e_in
            hn = h[hh] * jnp.exp(t_all) + _mm_lt(lax.stop_gradient(k), k_t, w)
        else:
            y = _nn(qk * dec, v[hh]) + _nn(q, h[hh]) * e_in
            hn = h[hh] * jnp.exp(t_all) + _nn(k_t, w)
        hs.append(hn)
        ys.append(y)
    return jnp.stack(hs), jnp.stack(ys)


def _scan_dims(a, N, P, T):
    G, Ha = a.shape[0], a.shape[2]
    return G, Ha, T // CHUNK


def _scan_specs(gb, N, Hg, P, Ha, cm, qcol, kcol, vcol):
    qs = lambda col: pl.BlockSpec((CHUNK, gb * N), lambda g, c: (cm(c), col + g))
    vs = lambda col: pl.BlockSpec((CHUNK, gb * Hg * P), lambda g, c: (cm(c), col + g))
    as_ = pl.BlockSpec((gb, CHUNK, Ha), lambda g, c: (g, cm(c), 0))
    hs = pl.BlockSpec((gb, 1, Hg, N, P), lambda g, c: (g, cm(c), 0, 0, 0))
    return qs(qcol), qs(kcol), vs(vcol), qs(0), vs(0), as_, hs


def _heads(ref, j, Hg, P):
    return jnp.stack([ref[:, (j * Hg + hh) * P:(j * Hg + hh + 1) * P] for hh in range(Hg)])


def _scan_fwd(q, k, v, a, *, N, Hg, P, qcol=0, kcol=0, vcol=0, rev, name):
    T = q.shape[0]
    G, Ha, NC = _scan_dims(a, N, P, T)
    gb = SCAN_HEADS_PER_STEP // Hg
    cm = (lambda c: NC - 1 - c) if rev else (lambda c: c)
    qs, ks, vs, _, ys, as_, hs = _scan_specs(gb, N, Hg, P, Ha, cm, qcol, kcol, vcol)

    def body(q_ref, k_ref, v_ref, a_ref, y_ref, hs_ref, h_scr):
        @pl.when(pl.program_id(1) == 0)
        def _():
            h_scr[...] = jnp.zeros_like(h_scr)

        for j in range(gb):
            h = h_scr[j]
            hs_ref[j, 0] = h
            hn, y = _scan_step(h, q_ref[:, j * N:(j + 1) * N], k_ref[:, j * N:(j + 1) * N], _heads(v_ref, j, Hg, P), a_ref[j], rev)
            for hh in range(Hg):
                y_ref[:, (j * Hg + hh) * P:(j * Hg + hh + 1) * P] = y[hh]
            h_scr[j] = hn

    return pl.pallas_call(
        body, name=name, grid=(G // gb, NC), in_specs=[qs, ks, vs, as_], out_specs=[ys, hs],
        out_shape=[S((T, G * Hg * P), f32), S((G, NC, Hg, N, P), f32)],
        scratch_shapes=[pltpu.VMEM((gb, Hg, N, P), f32)],
        compiler_params=_params(("parallel", "arbitrary")))(q, k, v, a)


def _scan_bwd(q, k, v, a, hsave, dy, *, N, Hg, P, qcol=0, kcol=0, vcol=0, rev, name):
    T = q.shape[0]
    G, Ha, NC = _scan_dims(a, N, P, T)
    gb = SCAN_HEADS_PER_STEP // Hg
    cm = (lambda c: c) if rev else (lambda c: NC - 1 - c)
    qs, ks, vs, dqs, dvs, as_, hs = _scan_specs(gb, N, Hg, P, Ha, cm, qcol, kcol, vcol)

    def body(q_ref, k_ref, v_ref, a_ref, hs_ref, dy_ref, dq_ref, dk_ref, dv_ref, da_ref, dh_scr):
        @pl.when(pl.program_id(1) == 0)
        def _():
            dh_scr[...] = jnp.zeros_like(dh_scr)

        for j in range(gb):
            cols = slice(j * N, (j + 1) * N)
            _, vjp = jax.vjp(functools.partial(_scan_step, rev=rev, for_vjp=True), hs_ref[j, 0], q_ref[:, cols], k_ref[:, cols],
                             _heads(v_ref, j, Hg, P), a_ref[j])
            dh, dq, dk, dv, da = vjp((dh_scr[j], _heads(dy_ref, j, Hg, P)))
            dq_ref[:, cols] = dq
            dk_ref[:, cols] = dk
            for hh in range(Hg):
                dv_ref[:, (j * Hg + hh) * P:(j * Hg + hh + 1) * P] = dv[hh]
            da_ref[j] = da
            dh_scr[j] = dh

    return pl.pallas_call(
        body, name=name, grid=(G // gb, NC), in_specs=[qs, ks, vs, as_, hs, dvs], out_specs=[dqs, dqs, dvs, as_],
        out_shape=[S((T, G * N), f32), S((T, G * N), f32), S((T, G * Hg * P), f32), S((G, T, Ha), f32)],
        scratch_shapes=[pltpu.VMEM((gb, Hg, N, P), f32)],
        compiler_params=_params(("parallel", "arbitrary")))(q, k, v, a, hsave, dy)


def _na_block_case(rb, nrb):
    return jnp.where(rb == 0, 0, jnp.where(rb == nrb - 1, 2, 1))


def _na_key_start(rb, rows):
    return pl.multiple_of(jnp.clip(rb * NA_QROWS - NA_WR // 2, 0, rows - NA_KROWS) * GRID_W, 256)


def _na_specs(T, nrb):
    nq, nk, wb = NA_QROWS * GRID_W, NA_KROWS * GRID_W, NA_PAIR * NADH
    qs = lambda col: pl.BlockSpec((nq, wb), lambda p, r: (r, col + p))
    fs = lambda col: pl.BlockSpec((T, wb), lambda p, r: (0, col + p))
    bs = pl.BlockSpec((NA_PAIR, 1, nq, nk), lambda p, r: (p, _na_block_case(r, nrb), 0, 0))
    ls = pl.BlockSpec((1, nq, NA_PAIR), lambda p, r: (p, r, 0))
    return qs, fs, bs, ls


def _na_fwd(q, k, v, bias, *, qcol, kcol, vcol, name):
    T = q.shape[0]
    rows = T // GRID_W
    nq, nk = NA_QROWS * GRID_W, NA_KROWS * GRID_W
    nrb = T // nq
    scale = NADH ** -0.5
    qs, fs, bs, ls = _na_specs(T, nrb)

    def body(q_ref, k_ref, v_ref, b_ref, o_ref, l_ref):
        ks = _na_key_start(pl.program_id(1), rows)
        for hh in range(NA_PAIR):
            sl = slice(hh * NADH, (hh + 1) * NADH)
            kw = k_ref[pl.ds(ks, nk), sl]
            vw = v_ref[pl.ds(ks, nk), sl]
            s = lax.dot_general(q_ref[:, sl], kw, NT, preferred_element_type=f32) * scale + b_ref[hh, 0]
            m = jnp.max(s, axis=1, keepdims=True)
            p = jnp.exp(s - m)
            l = jnp.sum(p, axis=1, keepdims=True)
            o_ref[:, sl] = jnp.dot(p.astype(bf16), vw, preferred_element_type=f32) / l
            l_ref[0, :, hh:hh + 1] = m + jnp.log(l)

    return pl.pallas_call(
        body, name=name, grid=(NAH // NA_PAIR, nrb), in_specs=[qs(qcol), fs(kcol), fs(vcol), bs],
        out_specs=[qs(0), ls], out_shape=[S((T, NAW), f32), S((NAH // NA_PAIR, T, NA_PAIR), f32)],
        compiler_params=_params(("parallel", "arbitrary")))(q, k, v, bias)


def _na_bwd(q, k, v, bias, o, lse, do, *, qcol, kcol, vcol, docol, name):
    T = q.shape[0]
    rows = T // GRID_W
    nq, nk = NA_QROWS * GRID_W, NA_KROWS * GRID_W
    nrb = T // nq
    scale = NADH ** -0.5
    qs, fs, bs, ls = _na_specs(T, nrb)

    def body(q_ref, k_ref, v_ref, b_ref, o_ref, l_ref, do_ref, dq_ref, dk_ref, dv_ref, db_ref):
        rb = pl.program_id(1)

        @pl.when(rb == 0)
        def _():
            dk_ref[...] = jnp.zeros_like(dk_ref)
            dv_ref[...] = jnp.zeros_like(dv_ref)

        ks = _na_key_start(rb, rows)
        first = (rb == 0) | (rb == 1) | (rb == nrb - 1)
        for hh in range(NA_PAIR):
            sl = slice(hh * NADH, (hh + 1) * NADH)
            qv = q_ref[:, sl]
            kw = k_ref[pl.ds(ks, nk), sl]
            vw = v_ref[pl.ds(ks, nk), sl]
            s = lax.dot_general(qv, kw, NT, preferred_element_type=f32) * scale + b_ref[hh, 0]
            p = jnp.exp(s - l_ref[0, :, hh:hh + 1])
            do_ = do_ref[:, sl]
            dob = do_.astype(bf16)
            dp = lax.dot_general(dob, vw, NT, preferred_element_type=f32)
            ds = p * (dp - jnp.sum(do_ * o_ref[:, sl], axis=1, keepdims=True))
            dsb = ds.astype(bf16)
            dq_ref[:, sl] = jnp.dot(dsb, kw, preferred_element_type=f32) * scale
            dk_ref[pl.ds(ks, nk), sl] += lax.dot_general(dsb, qv, TN, preferred_element_type=f32) * scale
            dv_ref[pl.ds(ks, nk), sl] += lax.dot_general(p.astype(bf16), dob, TN, preferred_element_type=f32)

            @pl.when(first)
            def _(hh=hh, ds=ds):
                db_ref[hh, 0] = ds

            @pl.when(jnp.logical_not(first))
            def _(hh=hh, ds=ds):
                db_ref[hh, 0] += ds

    return pl.pallas_call(
        body, name=name, grid=(NAH // NA_PAIR, nrb),
        in_specs=[qs(qcol), fs(kcol), fs(vcol), bs, qs(0), ls, qs(docol)],
        out_specs=[qs(0), fs(0), fs(0), bs],
        out_shape=[S((T, NAW), f32), S((T, NAW), f32), S((T, NAW), f32), S(bias.shape, f32)],
        compiler_params=_params(("parallel", "arbitrary")))(q, k, v, bias, o, lse, do)


def _na_col_tables():
    c = np.arange(GRID_W)[:, None]
    kc = np.arange(GRID_W)[None, :]
    cstart = np.clip(c - NA_WC // 2, 0, GRID_W - NA_WC)
    valid_c = (kc >= cstart) & (kc < cstart + NA_WC)
    dc = kc - c + NA_WC - 1
    E = (valid_c[:, :, None] & (dc[:, :, None] == np.arange(2 * NA_WC - 1)[None, None, :])).astype(np.float32)
    return E, np.where(valid_c, 0.0, -1e30).astype(np.float32)


def _na_row_offsets(rows):
    table = []
    for r0 in (0, NA_QROWS, rows - NA_QROWS):
        ks = int(np.clip(r0 - NA_WR // 2, 0, rows - NA_KROWS))
        case = []
        for ri in range(NA_QROWS):
            r = r0 + ri
            rs = int(np.clip(r - NA_WR // 2, 0, rows - NA_WR))
            case.append([ks + kri - r + NA_WR - 1 if rs <= ks + kri < rs + NA_WR else None for kri in range(NA_KROWS)])
        table.append(case)
    return table


def _na_col_bias(rpb):
    E, cmask = _na_col_tables()
    return jnp.einsum("hde,cke->hdck", rpb, E, precision=HI) + cmask


def _na_bias_build(r1, rows, *, name):
    H = r1.shape[0]
    offs = _na_row_offsets(rows)

    def body(r_ref, o_ref):
        outside = jnp.full((GRID_W, GRID_W), -1e30, f32)
        for z in range(3):
            for a in range(NA_QROWS):
                for b in range(NA_KROWS):
                    d = offs[z][a][b]
                    o_ref[0, z, a * GRID_W:(a + 1) * GRID_W, b * GRID_W:(b + 1) * GRID_W] = outside if d is None else r_ref[0, d]

    return pl.pallas_call(
        body, name=name, grid=(H,), in_specs=[pl.BlockSpec((1,) + r1.shape[1:], lambda h: (h, 0, 0, 0))],
        out_specs=pl.BlockSpec((1, 3, NA_QROWS * GRID_W, NA_KROWS * GRID_W), lambda h: (h, 0, 0, 0)),
        out_shape=S((H, 3, NA_QROWS * GRID_W, NA_KROWS * GRID_W), f32), compiler_params=_params(("parallel",)))(r1)


def _na_bias_fold(dbias, rows, *, name):
    H = dbias.shape[0]
    offs = _na_row_offsets(rows)
    nd = 2 * NA_WR - 1

    def body(d_ref, o_ref):
        acc = [None] * nd
        for z in range(3):
            for a in range(NA_QROWS):
                for b in range(NA_KROWS):
                    d = offs[z][a][b]
                    if d is not None:
                        t = d_ref[0, z, a * GRID_W:(a + 1) * GRID_W, b * GRID_W:(b + 1) * GRID_W]
                        acc[d] = t if acc[d] is None else acc[d] + t
        for d in range(nd):
            o_ref[0, d] = acc[d]

    return pl.pallas_call(
        body, name=name, grid=(H,), in_specs=[pl.BlockSpec((1,) + dbias.shape[1:], lambda h: (h, 0, 0, 0))],
        out_specs=pl.BlockSpec((1, nd, GRID_W, GRID_W), lambda h: (h, 0, 0, 0)),
        out_shape=S((H, nd, GRID_W, GRID_W), f32), compiler_params=_params(("parallel",)))(dbias)


def _conv_shifts(prev, cur, nxt, i, n_i, W):
    tm = cur.shape[0]
    prev = jnp.where(i > 0, prev, 0.0)
    nxt = jnp.where(i < n_i - 1, nxt, 0.0)
    ext = jnp.concatenate([prev, cur, nxt], axis=0)
    out = []
    for w in range(W):
        s = (W // 2 - w) % (tm + 16)
        out.append((ext if s == 0 else pltpu.roll(ext, s, axis=0))[8:8 + tm])
    return out


def _conv_act(u, mode):
    if mode == "silu":
        return jax.nn.silu(u)
    if mode == "geglu":
        half = u.shape[1] // 2
        return jax.nn.gelu(u[:, :half], approximate=True) * u[:, half:]
    return u


def _conv_specs(T, tm, tc, xbase):
    r8 = tm // 8
    last = T // 8 - 1
    cur = pl.BlockSpec((tm, tc), lambda j, i: (i, xbase + j))
    prev = pl.BlockSpec((8, tc), lambda j, i: (jnp.maximum(i * r8 - 1, 0), xbase + j))
    nxt = pl.BlockSpec((8, tc), lambda j, i: (jnp.minimum((i + 1) * r8, last), xbase + j))
    return cur, prev, nxt


def _conv(x, w8, b, *, mode, W, name, C, xbase=0, tm=512, tc=512, out_dtype=f32):
    T = x.shape[0]
    NI, J = T // tm, C // tc
    tco = tc // 2 if mode == "geglu" else tc
    cur, prev, nxt = _conv_specs(T, tm, tc, xbase)

    def body(xc, xp, xn, w_ref, b_ref, o_ref):
        sh = _conv_shifts(xp[...].astype(f32), xc[...].astype(f32), xn[...].astype(f32), pl.program_id(1), NI, W)
        wv = w_ref[...]
        u = sh[0] * wv[0:1, :]
        for w in range(1, W):
            u = u + sh[w] * wv[w:w + 1, :]
        if mode != "none":
            u = u + b_ref[...]
        o_ref[...] = _conv_act(u, mode).astype(o_ref.dtype)

    return pl.pallas_call(
        body, name=name, grid=(J, NI),
        in_specs=[cur, prev, nxt, pl.BlockSpec((8, tc), lambda j, i: (0, j)), pl.BlockSpec((1, tc), lambda j, i: (0, j))],
        out_specs=pl.BlockSpec((tm, tco), lambda j, i: (i, j)), out_shape=S((T, J * tco), out_dtype),
        compiler_params=_params(("parallel", "parallel")))(x, x, x, w8, b)


def _conv_bwd(x, w8, b, dact, *, mode, W, name, C, xbase=0, tm=512, tc=512):
    T = x.shape[0]
    NI, J = T // tm, C // tc
    tco = tc // 2 if mode == "geglu" else tc
    rows = tm + 16
    pad = W // 2
    cur, prev, nxt = _conv_specs(T, tm, tc, xbase)
    dcur, dprev, dnxt = _conv_specs(T, tm, tco, 0)

    def body(xc, xp, xn, w_ref, b_ref, dc, dp, dn, dx_ref, dw_ref, db_ref):
        i = pl.program_id(1)
        ext = jnp.concatenate([jnp.where(i > 0, xp[...], 0.0), xc[...], jnp.where(i < NI - 1, xn[...], 0.0)], axis=0)
        dext = jnp.concatenate([jnp.where(i > 0, dp[...], 0.0), dc[...], jnp.where(i < NI - 1, dn[...], 0.0)], axis=0)
        wv = w_ref[...]
        shift = lambda t, w: t if w == pad else pltpu.roll(t, (pad - w) % rows, axis=0)
        xs = [shift(ext, w) for w in range(W)]
        u = b_ref[...] + xs[0] * wv[0:1, :]
        for w in range(1, W):
            u = u + xs[w] * wv[w:w + 1, :]
        _, vjp = jax.vjp(functools.partial(_conv_act, mode=mode), u)
        du = vjp(dext.astype(f32))[0]
        dx = shift(du, 0)[8:8 + tm] * wv[W - 1:W, :]
        for w in range(1, W):
            dx = dx + shift(du, w)[8:8 + tm] * wv[W - 1 - w:W - w, :]
        dx_ref[...] = dx.astype(dx_ref.dtype)

        @pl.when(i == 0)
        def _():
            dw_ref[...] = jnp.zeros_like(dw_ref)
            db_ref[...] = jnp.zeros_like(db_ref)

        dum = du[8:8 + tm]
        db_ref[...] += jnp.sum(dum, axis=0, keepdims=True)
        for w in range(W):
            dw_ref[w:w + 1, :] += jnp.sum(dum * xs[w][8:8 + tm], axis=0, keepdims=True)

    return pl.pallas_call(
        body, name=name, grid=(J, NI),
        in_specs=[cur, prev, nxt, pl.BlockSpec((8, tc), lambda j, i: (0, j)), pl.BlockSpec((1, tc), lambda j, i: (0, j)),
                  dcur, dprev, dnxt],
        out_specs=[pl.BlockSpec((tm, tc), lambda j, i: (i, j)), pl.BlockSpec((8, tc), lambda j, i: (0, j)),
                   pl.BlockSpec((1, tc), lambda j, i: (0, j))],
        out_shape=[S((T, C), bf16), S((8, C), f32), S((1, C), f32)],
        compiler_params=_params(("parallel", "arbitrary")))(x, x, x, w8, b, dact, dact, dact)


def _pad8(w):
    return jnp.concatenate([w, jnp.zeros((8 - w.shape[0], w.shape[1]), w.dtype)], axis=0)


def _all_gather(arrs, *, name):
    n = len(arrs)

    def body(*refs):
        ins, outs = refs[:n], refs[n:2 * n]
        send_sems, recv_sems, loc_sems = refs[2 * n:]
        x, y, c = lax.axis_index("x"), lax.axis_index("y"), lax.axis_index("c")
        ident = lambda px, py, pc: 4 * px + 2 * py + pc
        me, sibling = (x, y, c), (x, y, 1 - c)
        chips = [(1 - x, y), (x, 1 - y), (1 - x, 1 - y)]

        def copy(a, k, block, to, src=None):
            slot = outs[a].at[ident(*block)]
            return pltpu.make_async_remote_copy(
                src_ref=slot if src is None else src, dst_ref=slot, send_sem=send_sems.at[a * 7 + k], recv_sem=recv_sems.at[a * 7 + k],
                device_id=to, device_id_type=pl.DeviceIdType.MESH)

        local = [pltpu.make_async_copy(ins[a], outs[a].at[ident(*me)], loc_sems.at[a]) for a in range(n)]
        for cp in local:
            cp.start()
        first = []
        for a in range(n):
            first.append(copy(a, 0, me, sibling, src=ins[a]))
            first += [copy(a, 1 + j, me, (*chip, c), src=ins[a]) for j, chip in enumerate(chips)]
        for cp in first:
            cp.start()
        passed = []
        for j, chip in enumerate(chips):
            for a in range(n):
                copy(a, 1 + j, (*chip, c), me).wait_recv()
                fwd = copy(a, 4 + j, (*chip, c), sibling)
                fwd.start()
                passed.append(fwd)
        for a in range(n):
            copy(a, 0, sibling, me).wait_recv()
            for j, chip in enumerate(chips):
                copy(a, 4 + j, (*chip, 1 - c), me).wait_recv()
        for cp in first + passed:
            cp.wait_send()
        for cp in local:
            cp.wait()

    any_spec = pl.BlockSpec(memory_space=pl.ANY)
    return pl.pallas_call(
        body, name=name, in_specs=[any_spec] * n, out_specs=[any_spec] * n,
        out_shape=[S((NDEV,) + a.shape, a.dtype) for a in arrs],
        scratch_shapes=[pltpu.SemaphoreType.DMA((7 * n,)), pltpu.SemaphoreType.DMA((7 * n,)), pltpu.SemaphoreType.DMA((n,))],
        )(*arrs)


NCHIP = NDEV // 2


def _to_sibling(arrs, *, name):
    n = len(arrs)
    ncopy = sum(NCHIP if ps else 1 for _, ps in arrs)

    def body(*refs):
        ins, outs = refs[:n], refs[n:2 * n]
        send_sems, recv_sems = refs[2 * n:]
        x, y, c = lax.axis_index("x"), lax.axis_index("y"), lax.axis_index("c")
        copies, idx = [], 0
        for a, (_, per_slot) in enumerate(arrs):
            pairs = [(ins[a].at[2 * q + (1 - c)], outs[a].at[q]) for q in range(NCHIP)] if per_slot else [(ins[a], outs[a])]
            for src, dst in pairs:
                copies.append(pltpu.make_async_remote_copy(
                    src_ref=src, dst_ref=dst, send_sem=send_sems.at[idx], recv_sem=recv_sems.at[idx],
                    device_id=(x, y, 1 - c), device_id_type=pl.DeviceIdType.MESH))
                idx += 1
        for cp in copies:
            cp.start()
        for cp in copies:
            cp.wait_recv()
        for cp in copies:
            cp.wait_send()

    any_spec = pl.BlockSpec(memory_space=pl.ANY)
    return pl.pallas_call(
        body, name=name, in_specs=[any_spec] * n, out_specs=[any_spec] * n,
        out_shape=[S((NCHIP,) + a.shape[1:] if ps else a.shape, a.dtype) for a, ps in arrs],
        scratch_shapes=[pltpu.SemaphoreType.DMA((ncopy,)), pltpu.SemaphoreType.DMA((ncopy,))])(*[a for a, _ in arrs])


def _add_partials(mine, theirs, *, per_slot, tr, name):
    R, C = mine.shape[-2:]

    def body(a_ref, b_ref, o_ref):
        a = a_ref[lax.axis_index("c")] if per_slot else a_ref[...]
        b = b_ref[0] if per_slot else b_ref[...]
        s = a.astype(f32) + b.astype(f32)
        if per_slot:
            o_ref[0] = s.astype(o_ref.dtype)
        else:
            o_ref[...] = s.astype(o_ref.dtype)

    if per_slot:
        grid = (NCHIP, R // tr)
        in_specs = [pl.BlockSpec((2, tr, C), lambda q, i: (q, i, 0)), pl.BlockSpec((1, tr, C), lambda q, i: (q, i, 0))]
        out_spec, out_shape = pl.BlockSpec((1, tr, C), lambda q, i: (q, i, 0)), S((NCHIP, R, C), mine.dtype)
    else:
        grid = (1, R // tr)
        in_specs = [pl.BlockSpec((tr, C), lambda q, i: (i, 0))] * 2
        out_spec, out_shape = pl.BlockSpec((tr, C), lambda q, i: (i, 0)), S((R, C), mine.dtype)
    return pl.pallas_call(body, name=name, grid=grid, in_specs=in_specs, out_specs=out_spec, out_shape=out_shape,
                          compiler_params=_params(("parallel", "parallel")))(mine, theirs)


def _to_chips(arrs, *, name):
    n = len(arrs)

    def body(*refs):
        ins, outs = refs[:n], refs[n:2 * n]
        send_sems, recv_sems, loc_sems = refs[2 * n:]
        x, y, c = lax.axis_index("x"), lax.axis_index("y"), lax.axis_index("c")
        my_q = 2 * x + y
        src = lambda a, q: ins[a].at[q] if arrs[a][1] else ins[a]
        local = [pltpu.make_async_copy(src(a, my_q), outs[a].at[my_q], loc_sems.at[a]) for a in range(n)]
        for cp in local:
            cp.start()
        sent = []
        for j, (px, py) in enumerate([(1 - x, y), (x, 1 - y), (1 - x, 1 - y)]):
            q = 2 * px + py
            for a in range(n):
                mk = lambda slot, a=a, j=j, q=q, dev=(px, py, c): pltpu.make_async_remote_copy(
                    src_ref=src(a, q), dst_ref=outs[a].at[slot], send_sem=send_sems.at[3 * a + j], recv_sem=recv_sems.at[3 * a + j],
                    device_id=dev, device_id_type=pl.DeviceIdType.MESH)
                mk(my_q).start()
                sent.append((mk, q))
        for mk, q in sent:
            mk(q).wait_recv()
        for mk, q in sent:
            mk(q).wait_send()
        for cp in local:
            cp.wait()

    any_spec = pl.BlockSpec(memory_space=pl.ANY)
    return pl.pallas_call(
        body, name=name, in_specs=[any_spec] * n, out_specs=[any_spec] * n,
        out_shape=[S(a.shape if ps else (NCHIP,) + a.shape, a.dtype) for a, ps in arrs],
        scratch_shapes=[pltpu.SemaphoreType.DMA((3 * n,)), pltpu.SemaphoreType.DMA((3 * n,)), pltpu.SemaphoreType.DMA((n,))],
        )(*[a for a, _ in arrs])


def _adamw(r, w, m, v, *, name, tr):
    M, C = w.shape
    nparts = r.shape[0]

    def body(r_ref, w_ref, m_ref, v_ref, g_ref, d_ref, nm_ref, nv_ref):
        g = r_ref[0].astype(f32)
        for s in range(1, nparts):
            g = g + r_ref[s].astype(f32)
        m_ = B1 * m_ref[...] + (1.0 - B1) * g
        v_ = B2 * v_ref[...] + (1.0 - B2) * jnp.square(g)
        m_hat = m_ / (1.0 - B1 ** STEP)
        v_hat = v_ / (1.0 - B2 ** STEP)
        g_ref[...] = g
        d_ref[...] = -LR * (m_hat / (jnp.sqrt(v_hat) + AEPS) + WD * w_ref[...])
        nm_ref[...] = m_
        nv_ref[...] = v_

    row = pl.BlockSpec((tr, C), lambda i: (i, 0))
    return pl.pallas_call(
        body, name=name, grid=(M // tr,),
        in_specs=[pl.BlockSpec((nparts, tr, C), lambda i: (0, i, 0)), row, row, row],
        out_specs=[row] * 4, out_shape=[S((M, C), f32)] * 4, compiler_params=_params(("parallel",)))(r, w, m, v)


def _colmove(ins, in_slots, outs, moves, *, tk, name):
    R = ins[0].shape[1] if in_slots[0] else ins[0].shape[0]
    n_in = len(ins)

    def body(*refs):
        for ii, isl, ic, oi, osl, oc, w in moves:
            src, dst = refs[ii], refs[n_in + oi]
            val = src[:, ic:ic + w] if isl is None else src[isl, :, ic:ic + w]
            if osl is None:
                dst[:, oc:oc + w] = val.astype(dst.dtype)
            else:
                dst[osl, :, oc:oc + w] = val.astype(dst.dtype)

    def spec(is_slots, C):
        return pl.BlockSpec((NDEV, tk, C), lambda i: (0, i, 0)) if is_slots else pl.BlockSpec((tk, C), lambda i: (i, 0))

    return pl.pallas_call(
        body, name=name, grid=(R // tk,),
        in_specs=[spec(sl, a.shape[-1]) for a, sl in zip(ins, in_slots)],
        out_specs=[spec(sl, C) for sl, C, _ in outs],
        out_shape=[S((NDEV, R, C) if sl else (R, C), dt) for sl, C, dt in outs],
        compiler_params=_params(("parallel",)))(*ins)


def _col_pieces(n8, cuts, place):
    out = []
    for p in range(NDEV):
        lo, hi = p * n8, (p + 1) * n8
        edges = [lo] + [c for c in cuts if lo < c < hi] + [hi]
        for a, b in zip(edges[:-1], edges[1:]):
            out.append((p, a - lo) + place(a) + (b - a,))
    return out


def _place_plain(c):
    return (0, c)


def _place_ssd_in(c):
    return (0, c) if c < SSD_INNER + SSD_XBC else (1, c - (SSD_INNER + SSD_XBC))


def _place_ffn_up(c):
    h = FFN_TC // 2
    return (0, (c // h) * FFN_TC + c % h) if c < FFN else (0, ((c - FFN) // h) * FFN_TC + h + (c - FFN) % h)


_COL_LAYOUTS = {
    "ab_w_in": ([], _place_plain, [4 * RW + 3 * NAW]),
    "c_w_in": ([SSD_INNER + SSD_XBC], _place_ssd_in, [SSD_INNER + SSD_XBC, 2 * SSD_H]),
    "ffn_w_up": (list(range(FFN_TC // 2, 2 * FFN, FFN_TC // 2)), _place_ffn_up, [2 * FFN]),
}


def _cols_from_slots(g, which, *, name):
    cuts, place, widths = _COL_LAYOUTS[which]
    moves = [(0, p, sc, mi, None, mc, w) for p, sc, mi, mc, w in _col_pieces(g.shape[2], cuts, place)]
    return _colmove([g], [True], [(False, w, g.dtype) for w in widths], moves, tk=256, name=name)


def _cols_to_slots(mats, which, dtype, *, name):
    cuts, place, widths = _COL_LAYOUTS[which]
    n8 = sum(widths) // NDEV
    moves = [(mi, None, mc, 0, p, sc, w) for p, sc, mi, mc, w in _col_pieces(n8, cuts, place)]
    return _colmove(list(mats), [False] * len(mats), [(True, n8, dtype)], moves, tk=256, name=name)[0]


def _tm2hm(a, H):
    T = a.shape[0]
    return a.reshape(T, H, -1).transpose(1, 0, 2)


def _hm2tm(a):
    H, T, P = a.shape
    return a.transpose(1, 0, 2).reshape(T, H * P)


def _pack(parts, dtype, row_mult):
    flat = jnp.concatenate([p.reshape(-1).astype(dtype) for p in parts])
    rows = -(-flat.shape[0] // LANES)
    rows = -(-rows // row_mult) * row_mult
    return jnp.pad(flat, (0, rows * LANES - flat.shape[0])).reshape(rows, LANES)


def _unpack(buf, shapes, lead=()):
    flat = buf.reshape(lead + (-1,))
    out, off = [], 0
    for shp in shapes:
        n = int(np.prod(shp))
        out.append(flat[..., off:off + n].reshape(lead + tuple(shp)))
        off += n
    return out


def _to_slots(full, ax):
    shp = full.shape
    return jnp.moveaxis(full.reshape(shp[:ax] + (NDEV, shp[ax] // NDEV) + shp[ax + 1:]), ax, 0)


def _from_slots(g, ax):
    t = jnp.moveaxis(g, 0, ax)
    shp = t.shape
    return t.reshape(shp[:ax] + (shp[ax] * shp[ax + 1],) + shp[ax + 2:])


def _ffn_perm(a):
    lead = a.shape[:-1]
    h = FFN_TC // 2
    return jnp.swapaxes(a.reshape(lead + (2, FFN // h, h)), -3, -2).reshape(lead + (2 * FFN,))


def _ffn_unperm(a):
    lead = a.shape[:-1]
    h = FFN_TC // 2
    return jnp.swapaxes(a.reshape(lead + (FFN // h, 2, h)), -3, -2).reshape(lead + (2 * FFN,))


def _rope_tables(T):
    half = RDH // 2
    inv = 1.0 / (ROPE_BASE ** (jnp.arange(half, dtype=f32) / half))
    ang = jnp.arange(T, dtype=f32)[:, None] * inv[None, :]
    cos, sin = jnp.cos(ang), jnp.sin(ang)
    cos_t = jnp.tile(jnp.concatenate([cos, cos], axis=1), (1, RH))
    sin_t = jnp.tile(jnp.concatenate([-sin, sin], axis=1), (1, RH))
    return cos_t, sin_t


def _group_avg():
    g = np.arange(RW) // RDH
    return jnp.asarray((g[:, None] == g[None, :]).astype(np.float32) / RDH)


def _head_expand():
    hd = np.arange(SSD_INNER) // SSD_HD
    rows = np.arange(2 * SSD_H)
    ex0 = (rows[:, None] == hd[None, :]).astype(np.float32)
    ex1 = (rows[:, None] == SSD_H + hd[None, :]).astype(np.float32)
    return jnp.asarray(ex0), jnp.asarray(ex1)


def kernel(x, norm_mix_pre, norm_mix_post, norm_ffn_pre, norm_ffn_post, ab_w_in, ab_ret_decay_logit, ab_ret_gn_g, ab_na_rpb, ab_w_out, c_w_in, c_conv_w, c_conv_b, c_dt_bias, c_a_log, c_d_skip, c_norm_g, c_w_out, ffn_w_up, ffn_conv_w, ffn_conv_b, ffn_w_down, loss_target, m_norm_mix_pre, m_norm_mix_post, m_norm_ffn_pre, m_norm_ffn_post, m_ab_w_in, m_ab_ret_decay_logit, m_ab_ret_gn_g, m_ab_na_rpb, m_ab_w_out, m_c_w_in, m_c_conv_w, m_c_conv_b, m_c_dt_bias, m_c_a_log, m_c_d_skip, m_c_norm_g, m_c_w_out, m_ffn_w_up, m_ffn_conv_w, m_ffn_conv_b, m_ffn_w_down, v_norm_mix_pre, v_norm_mix_post, v_norm_ffn_pre, v_norm_ffn_post, v_ab_w_in, v_ab_ret_decay_logit, v_ab_ret_gn_g, v_ab_na_rpb, v_ab_w_out, v_c_w_in, v_c_conv_w, v_c_conv_b, v_c_dt_bias, v_c_a_log, v_c_d_skip, v_c_norm_g, v_c_w_out, v_ffn_w_up, v_ffn_conv_w, v_ffn_conv_b, v_ffn_w_down):
    W = dict(norm_mix_pre=norm_mix_pre, norm_mix_post=norm_mix_post, norm_ffn_pre=norm_ffn_pre, norm_ffn_post=norm_ffn_post, ab_w_in=ab_w_in, ab_ret_decay_logit=ab_ret_decay_logit, ab_ret_gn_g=ab_ret_gn_g, ab_na_rpb=ab_na_rpb, ab_w_out=ab_w_out, c_w_in=c_w_in, c_conv_w=c_conv_w, c_conv_b=c_conv_b, c_dt_bias=c_dt_bias, c_a_log=c_a_log, c_d_skip=c_d_skip, c_norm_g=c_norm_g, c_w_out=c_w_out, ffn_w_up=ffn_w_up, ffn_conv_w=ffn_conv_w, ffn_conv_b=ffn_conv_b, ffn_w_down=ffn_w_down)
    Mo = dict(norm_mix_pre=m_norm_mix_pre, norm_mix_post=m_norm_mix_post, norm_ffn_pre=m_norm_ffn_pre, norm_ffn_post=m_norm_ffn_post, ab_w_in=m_ab_w_in, ab_ret_decay_logit=m_ab_ret_decay_logit, ab_ret_gn_g=m_ab_ret_gn_g, ab_na_rpb=m_ab_na_rpb, ab_w_out=m_ab_w_out, c_w_in=m_c_w_in, c_conv_w=m_c_conv_w, c_conv_b=m_c_conv_b, c_dt_bias=m_c_dt_bias, c_a_log=m_c_a_log, c_d_skip=m_c_d_skip, c_norm_g=m_c_norm_g, c_w_out=m_c_w_out, ffn_w_up=m_ffn_w_up, ffn_conv_w=m_ffn_conv_w, ffn_conv_b=m_ffn_conv_b, ffn_w_down=m_ffn_w_down)
    Vo = dict(norm_mix_pre=v_norm_mix_pre, norm_mix_post=v_norm_mix_post, norm_ffn_pre=v_norm_ffn_pre, norm_ffn_post=v_norm_ffn_post, ab_w_in=v_ab_w_in, ab_ret_decay_logit=v_ab_ret_decay_logit, ab_ret_gn_g=v_ab_ret_gn_g, ab_na_rpb=v_ab_na_rpb, ab_w_out=v_ab_w_out, c_w_in=v_c_w_in, c_conv_w=v_c_conv_w, c_conv_b=v_c_conv_b, c_dt_bias=v_c_dt_bias, c_a_log=v_c_a_log, c_d_skip=v_c_d_skip, c_norm_g=v_c_norm_g, c_w_out=v_c_w_out, ffn_w_up=v_ffn_w_up, ffn_conv_w=v_ffn_conv_w, ffn_conv_b=v_ffn_conv_b, ffn_w_down=v_ffn_w_down)
    return _train_step(x[0], loss_target[0], W, Mo, Vo)


def _train_step(x, tgt, W, Mo, Vo):
    T = x.shape[0]
    rows = T // GRID_W

    col = lambda d, n, dt: d[n].reshape(-1, d[n].shape[-1]).astype(dt)
    rows_of = lambda d, dt: jnp.concatenate([col(d, n, dt) for n in ROW_SHARDED], axis=0)
    small = _pack([W[n] for n, _ in SHARDED[N_BIG:]], f32, 8)
    gat = _all_gather([col(W, n, bf16) for n in COL_SHARDED] + [rows_of(W, bf16), small], name="gather_weights")
    per_layer = lambda m: m.reshape(-1, D, m.shape[-1])
    w_ab_in = per_layer(_cols_from_slots(gat[0], "ab_w_in", name="cols_ab_w_in")[0])
    w_zx, w_dt = [per_layer(m) for m in _cols_from_slots(gat[1], "c_w_in", name="cols_c_w_in")]
    w_up = per_layer(_cols_from_slots(gat[2], "ffn_w_up", name="cols_ffn_w_up")[0])
    full, off = {}, 0
    for n in ROW_SHARDED:
        L, r = W[n].shape[0], W[n].shape[1]
        full[n] = jnp.swapaxes(gat[3][:, off:off + L * r].reshape(NDEV, L, r, D), 0, 1).reshape(L, NDEV * r, D)
        off += L * r
    gs = _unpack(gat[4], [W[n].shape for n, _ in SHARDED[N_BIG:]], (NDEV,))
    full.update({n: _from_slots(g, ax) for (n, ax), g in zip(SHARDED[N_BIG:], gs)})
    w_ab_out, w_c_out, w_down = full["ab_w_out"], full["c_w_out"], full["ffn_w_down"]
    c_cw8 = [_pad8(full["c_conv_w"][i]) for i in range(2)]
    c_cb = [full["c_conv_b"][i][None] for i in range(2)]
    c_ng = [full["c_norm_g"][i][None] for i in range(2)]
    f_cw8 = [_pad8(_ffn_perm(full["ffn_conv_w"][l])) for l in range(DEPTH)]
    f_cb = [_ffn_perm(W["ffn_conv_b"][l])[None] for l in range(DEPTH)]

    g1 = [W["norm_mix_pre"][l][None] for l in range(DEPTH)]
    g2 = [W["norm_mix_post"][l][None] for l in range(DEPTH)]
    g3 = [W["norm_ffn_pre"][l][None] for l in range(DEPTH)]
    g4 = [W["norm_ffn_post"][l][None] for l in range(DEPTH)]
    cos_t, sin_t = _rope_tables(T)
    gavg = _group_avg()
    ex0, ex1 = _head_expand()

    def log_gamma(logit):
        return -jax.nn.softplus(-logit)

    def ret_decays(lg):
        return [jnp.broadcast_to(lg[d][:, None, None], (RH, T, 8)) for d in range(2)]

    saved = []
    xs_ = x
    hn = _rowwise("norm_first", _f_first, [(x, D, 0)], [], [(g1[0], D, 0)], [], [(D, bf16)], tm=256)[0]
    for l in range(DEPTH):
        i = l // 2
        sv = dict(x=xs_, hn=hn)
        if l % 2 == 0:
            proj = _mm_nn(hn, w_ab_in[i], name=f"ab_in_{l}")
            qr, kr = _rowwise(f"ret_prep_{l}", _f_rprep, [(proj, RW, 0), (proj, RW, 1)], [(cos_t, RW, 0), (sin_t, RW, 0)], [], [],
                              [(RW, f32), (RW, f32)], tm=256)
            lg, lg_vjp = jax.vjp(log_gamma, W["ab_ret_decay_logit"][i])
            a_f, a_b = ret_decays(lg)
            rscan = dict(N=RDH, Hg=1, P=RDH, vcol=2)
            yf_t, hsf = _scan_fwd(qr, kr, proj, a_f, rev=False, name=f"ret_scan_f_{l}", **rscan)
            yb_t, hsb = _scan_fwd(qr, kr, proj, a_b, rev=True, name=f"ret_scan_b_{l}", **rscan)
            gn = W["ab_ret_gn_g"][i][None]
            ret = _rowwise(f"ret_post_{l}", _f_rpost, [(yf_t, RW, 0), (yb_t, RW, 0), (proj, RW, 3)], [], [(gn, RW, 0)], [gavg],
                           [(RW, bf16)], tm=256)[0]
            nqkv = proj[:, 4 * RW:].astype(bf16)
            ncols = dict(qcol=0, kcol=NAW // 128, vcol=2 * NAW // 128)
            r1, bias_vjp = jax.vjp(_na_col_bias, W["ab_na_rpb"][i])
            bias = _na_bias_build(r1, rows, name=f"na_bias_{l}")
            na_o, na_l = _na_fwd(nqkv, nqkv, nqkv, bias, name=f"na_fwd_{l}", **ncols)
            cat = jnp.concatenate([ret, na_o.astype(bf16)], axis=1)
            mo = _mm_nn(cat, w_ab_out[i], name=f"ab_out_{l}")
            sv.update(proj=proj, qr=qr, kr=kr, a_f=a_f, a_b=a_b, hsf=hsf, hsb=hsb, yf_t=yf_t, yb_t=yb_t, gn=gn, rscan=rscan,
                      nqkv=nqkv, ncols=ncols, bias=bias, bias_vjp=bias_vjp, lg_vjp=lg_vjp, na_o=na_o, na_l=na_l, cat=cat)
        else:
            zx = _mm_nn(hn, w_zx[i], name=f"c_in_{l}")
            dtr = _mm_nn(hn, w_dt[i], name=f"c_in_dt_{l}")
            xa = _conv(zx, c_cw8[i], c_cb[i], mode="silu", W=SSD_CONV, name=f"c_conv_{l}", C=SSD_XBC, xbase=SSD_INNER // 512)
            dtb, alog = W["c_dt_bias"][i].reshape(1, 2 * SSD_H), W["c_a_log"][i].reshape(1, 2 * SSD_H)
            vf, vb, la = _rowwise(f"ssd_prep_{l}", _f_sprep, [(xa, SSD_INNER, 0), (dtr, 2 * SSD_H, 0)], [],
                                  [(dtb, 2 * SSD_H, 0), (alog, 2 * SSD_H, 0)], [ex0, ex1],
                                  [(SSD_INNER, f32), (SSD_INNER, f32), (2 * SSD_H, f32)], tm=128)
            a_f = la[:, :SSD_H].reshape(T, SSD_G, SSD_HPG).transpose(1, 0, 2)
            a_b = la[:, SSD_H:].reshape(T, SSD_G, SSD_HPG).transpose(1, 0, 2)
            sscan = dict(N=SSD_N, Hg=SSD_HPG, P=SSD_HD, qcol=(SSD_INNER + SSD_G * SSD_N) // SSD_N, kcol=SSD_INNER // SSD_N)
            yf_t, hsf = _scan_fwd(xa, xa, vf, a_f, rev=False, name=f"ssd_scan_f_{l}", **sscan)
            yb_t, hsb = _scan_fwd(xa, xa, vb, a_b, rev=True, name=f"ssd_scan_b_{l}", **sscan)
            dsk = jnp.repeat(W["c_d_skip"][i], SSD_HD)[None]
            yo = _rowwise(f"ssd_post_{l}", _f_spost, [(yf_t, 512, 0), (yb_t, 512, 0), (xa, 512, 0), (zx, 512, 0)], [],
                          [(dsk, 512, 0), (c_ng[i], 512, 0)], [], [(512, bf16)], tm=256, J=SSD_G)[0]
            mo = _mm_nn(yo, w_c_out[i], name=f"c_out_{l}")
            sv.update(zx=zx, dtr=dtr, xa=xa, dtb=dtb, alog=alog, a_f=a_f, a_b=a_b, vf=vf, vb=vb, sscan=sscan,
                      hsf=hsf, hsb=hsb, yf_t=yf_t, yb_t=yb_t, dsk=dsk, yo=yo)
        x1, hf = _rowwise(f"norm_mid_{l}", _f_mid, [(xs_, D, 0), (mo, D, 0)], [], [(g2[l], D, 0), (g3[l], D, 0)], [],
                          [(D, f32), (D, bf16)], tm=256)
        pre = _mm_nn(hf, w_up[l], name=f"ffn_up_{l}")
        act = _conv(pre, f_cw8[l], f_cb[l], mode="geglu", W=FFN_CONV, name=f"ffn_conv_{l}", C=2 * FFN, tc=FFN_TC, out_dtype=bf16)
        fo = _mm_nn(act, w_down[l], name=f"ffn_down_{l}")
        sv.update(mo=mo, x1=x1, hf=hf, pre=pre, act=act, fo=fo)
        if l < DEPTH - 1:
            xs_, hn = _rowwise(f"norm_end_{l}", _f_end, [(x1, D, 0), (fo, D, 0)], [], [(g4[l], D, 0), (g1[l + 1], D, 0)], [],
                               [(D, f32), (D, bf16)], tm=256)
        else:
            xs_ = _rowwise(f"norm_end_{l}", _f_last, [(x1, D, 0), (fo, D, 0)], [], [(g4[l], D, 0)], [], [(D, f32)], tm=256)[0]
        saved.append(sv)

    dx, lpart = _loss_call(xs_, tgt)
    loss = lax.psum(lpart[0, 0], ("x", "y", "c"))

    G = {n: [None] * W[n].shape[0] for n in WEIGHTS}
    dhn = None
    for l in reversed(range(DEPTH)):
        i = l // 2
        sv = saved[l]
        if l == DEPTH - 1:
            (dx1, dfo), (dg4,) = _rowwise_bwd(f"norm_end_bwd_{l}", _f_last, [(sv["x1"], D, 0), (sv["fo"], D, 0)], [],
                                              [(g4[l], D, 0)], [], [(dx, D, 0)], [f32, bf16], tm=256)
        else:
            (dx1, dfo), (dg4, dg1n) = _rowwise_bwd(f"norm_end_bwd_{l}", _f_end, [(sv["x1"], D, 0), (sv["fo"], D, 0)], [],
                                                   [(g4[l], D, 0), (g1[l + 1], D, 0)], [], [(dx, D, 0), (dhn, D, 0)],
                                                   [f32, bf16], tm=256)
            G["norm_mix_pre"][l + 1] = dg1n[0]
        G["norm_ffn_post"][l] = dg4[0]
        dact = _mm_nt(dfo, w_down[l], name=f"ffn_down_dx_{l}")
        G["ffn_w_down"][l] = _mm_tn(sv["act"], dfo, name=f"ffn_down_dw_{l}")
        dpre, dfw, dfb = _conv_bwd(sv["pre"], f_cw8[l], f_cb[l], dact, mode="geglu", W=FFN_CONV, name=f"ffn_conv_bwd_{l}",
                                   C=2 * FFN, tc=FFN_TC)
        dhf = _mm_nt(dpre, w_up[l], name=f"ffn_up_dx_{l}")
        G["ffn_w_up"][l] = _cols_to_slots([_mm_tn(sv["hf"], dpre, name=f"ffn_up_dw_{l}")], "ffn_w_up", bf16, name=f"slots_ffn_up_{l}")
        G["ffn_conv_w"][l] = _ffn_unperm(dfw[:FFN_CONV])
        G["ffn_conv_b"][l] = _ffn_unperm(dfb[0])
        (dxl, dmo), (dg2, dg3) = _rowwise_bwd(f"norm_mid_bwd_{l}", _f_mid, [(sv["x"], D, 0), (sv["mo"], D, 0)], [],
                                              [(g2[l], D, 0), (g3[l], D, 0)], [], [(dx1, D, 0), (dhf, D, 0)], [f32, bf16], tm=256)
        G["norm_mix_post"][l] = dg2[0]
        G["norm_ffn_pre"][l] = dg3[0]
        if l % 2 == 0:
            dcat = _mm_nt(dmo, w_ab_out[i], name=f"ab_out_dx_{l}")
            G["ab_w_out"][i] = _mm_tn(sv["cat"], dmo, name=f"ab_out_dw_{l}")
            (dyf, _, drg), (dgn,) = _rowwise_bwd(
                f"ret_post_bwd_{l}", _f_rpost, [(sv["yf_t"], RW, 0), (sv["yb_t"], RW, 0), (sv["proj"], RW, 3)], [],
                [(sv["gn"], RW, 0)], [gavg], [(dcat, RW, 0)], [f32, f32, bf16], tm=256)
            G["ab_ret_gn_g"][i] = dgn[0]
            dqf, dkf, dvf, daf = _scan_bwd(sv["qr"], sv["kr"], sv["proj"], sv["a_f"], sv["hsf"], dyf, rev=False,
                                           name=f"ret_scan_f_bwd_{l}", **sv["rscan"])
            dqb, dkb, dvb, dab = _scan_bwd(sv["qr"], sv["kr"], sv["proj"], sv["a_b"], sv["hsb"], dyf, rev=True,
                                           name=f"ret_scan_b_bwd_{l}", **sv["rscan"])
            dq_t, dk_t, drv = dqf + dqb, dkf + dkb, (dvf + dvb).astype(bf16)
            (drq, drk), _ = _rowwise_bwd(f"ret_prep_bwd_{l}", _f_rprep, [(sv["proj"], RW, 0), (sv["proj"], RW, 1)],
                                         [(cos_t, RW, 0), (sin_t, RW, 0)], [], [], [(dq_t, RW, 0), (dk_t, RW, 0)], [bf16, bf16], tm=256)
            da_cols = jnp.concatenate([_hm2tm(daf), _hm2tm(dab)], axis=1)
            dlg = _colsum(da_cols, name=f"ret_decay_sum_{l}").reshape(2, RH, 8)[:, :, 0]
            G["ab_ret_decay_logit"][i] = sv["lg_vjp"](dlg)[0]
            dnq, dnk, dnv, dbias = _na_bwd(sv["nqkv"], sv["nqkv"], sv["nqkv"], sv["bias"], sv["na_o"], sv["na_l"], dcat,
                                           docol=RW // 128, name=f"na_bwd_{l}", **sv["ncols"])
            G["ab_na_rpb"][i] = sv["bias_vjp"](_na_bias_fold(dbias, rows, name=f"na_bias_fold_{l}"))[0]
            dproj = jnp.concatenate([drq, drk, drv, drg] + [t.astype(bf16) for t in (dnq, dnk, dnv)], axis=1)
            dhn = _mm_nt(dproj, w_ab_in[i], name=f"ab_in_dx_{l}")
            G["ab_w_in"][i] = _cols_to_slots([_mm_tn(sv["hn"], dproj, name=f"ab_in_dw_{l}")], "ab_w_in", bf16, name=f"slots_ab_in_{l}")
        else:
            dyo = _mm_nt(dmo, w_c_out[i], name=f"c_out_dx_{l}")
            G["c_w_out"][i] = _mm_tn(sv["yo"], dmo, name=f"c_out_dw_{l}")
            (dyf, _, dxs1, dz), (ddsk, dng) = _rowwise_bwd(
                f"ssd_post_bwd_{l}", _f_spost, [(sv["yf_t"], 512, 0), (sv["yb_t"], 512, 0), (sv["xa"], 512, 0), (sv["zx"], 512, 0)],
                [], [(sv["dsk"], 512, 0), (c_ng[i], 512, 0)], [], [(dyo, 512, 0)], [f32, f32, f32, bf16], tm=256, J=SSD_G)
            G["c_d_skip"][i] = ddsk.reshape(SSD_H, SSD_HD).sum(axis=1)
            G["c_norm_g"][i] = dng[0]
            dqf, dkf, dvf, daf = _scan_bwd(sv["xa"], sv["xa"], sv["vf"], sv["a_f"], sv["hsf"], dyf, rev=False,
                                           name=f"ssd_scan_f_bwd_{l}", **sv["sscan"])
            dqb, dkb, dvb, dab = _scan_bwd(sv["xa"], sv["xa"], sv["vb"], sv["a_b"], sv["hsb"], dyf, rev=True,
                                           name=f"ssd_scan_b_bwd_{l}", **sv["sscan"])
            dla = jnp.concatenate([daf.transpose(1, 0, 2).reshape(T, SSD_H), dab.transpose(1, 0, 2).reshape(T, SSD_H)], axis=1)
            (dxs2, ddtr), (ddtb, dalog) = _rowwise_bwd(
                f"ssd_prep_bwd_{l}", _f_sprep, [(sv["xa"], SSD_INNER, 0), (sv["dtr"], 2 * SSD_H, 0)], [],
                [(sv["dtb"], 2 * SSD_H, 0), (sv["alog"], 2 * SSD_H, 0)], [ex0, ex1],
                [(dvf, SSD_INNER, 0), (dvb, SSD_INNER, 0), (dla, 2 * SSD_H, 0)], [f32, bf16], tm=128)
            G["c_dt_bias"][i] = ddtb.reshape(2, SSD_H)
            G["c_a_log"][i] = dalog.reshape(2, SSD_H)
            dxa = jnp.concatenate([dxs1 + dxs2, dkf + dkb, dqf + dqb], axis=1)
            dxbc, dcw, dcb = _conv_bwd(sv["zx"], c_cw8[i], c_cb[i], dxa, mode="silu", W=SSD_CONV, name=f"c_conv_bwd_{l}",
                                       C=SSD_XBC, xbase=SSD_INNER // 512)
            G["c_conv_w"][i] = dcw[:SSD_CONV]
            G["c_conv_b"][i] = dcb[0]
            dzx = jnp.concatenate([dz, dxbc], axis=1)
            t1 = _mm_nt(ddtr, w_dt[i], name=f"c_in_dt_dx_{l}")
            dhn = _mm_nt(dzx, w_zx[i], add=t1, name=f"c_in_dx_{l}")
            G["c_w_in"][i] = _cols_to_slots([_mm_tn(sv["hn"], dzx, name=f"c_in_dw_{l}"), _mm_tn(sv["hn"], ddtr, name=f"c_in_dt_dw_{l}")],
                                            "c_w_in", bf16, name=f"slots_c_in_{l}")
        dx = dxl
    (grad_x,), (dg1,) = _rowwise_bwd("norm_first_bwd", _f_first_bwd, [(x, D, 0)], [], [(g1[0], D, 0)], [], [(dx, D, 0), (dhn, D, 0)],
                                     [f32], tm=256)
    G["norm_mix_pre"][0] = dg1[0]

    small_names = [n for n, _ in SHARDED[N_BIG:]]
    col_slots = [jnp.concatenate(G[n], axis=1) for n in COL_SHARDED]
    row_slots = jnp.concatenate([g.reshape(NDEV, -1, D).astype(bf16) for n in ROW_SHARDED for g in G[n]], axis=1)
    small_slots = _pack_slots([_to_slots(jnp.stack(G[n]), ax) for n, ax in SHARDED[N_BIG:]], 8)
    ar = _pack([jnp.stack(G[n]) for n in REPLICATED], f32, 8)
    parts = [(a, True) for a in col_slots + [row_slots, small_slots]] + [(ar, False)]
    from_sib = _to_sibling(parts, name="grads_to_sibling")
    tiles = [256, 256, 256, 64, small_slots.shape[1], ar.shape[0]]
    chip = [_add_partials(a, b, per_slot=ps, tr=t, name=f"grads_add_{j}")
            for j, ((a, ps), b, t) in enumerate(zip(parts, from_sib, tiles))]
    exch = _to_chips([(a, ps) for a, (_, ps) in zip(chip, parts)], name="grads_to_chips")
    pk = lambda d, names: _pack([d[n] for n in names], f32, 8)
    upd = [_adamw(exch[j], col(W, n, f32), col(Mo, n, f32), col(Vo, n, f32), name=f"adamw_{n}", tr=256)
           for j, n in enumerate(COL_SHARDED)]
    upd_rows = _adamw(exch[3], rows_of(W, f32), rows_of(Mo, f32), rows_of(Vo, f32), name="adamw_rows", tr=64)
    upd_small = _adamw(exch[4], pk(W, small_names), pk(Mo, small_names), pk(Vo, small_names), name="adamw_small",
                       tr=small_slots.shape[1])
    upd_rep = _adamw(exch[5], pk(W, REPLICATED), pk(Mo, REPLICATED), pk(Vo, REPLICATED), name="adamw_replicated", tr=ar.shape[0])
    res = []
    for k in range(4):
        d = {n: upd[j][k].reshape(W[n].shape) for j, n in enumerate(COL_SHARDED)}
        off = 0
        for n in ROW_SHARDED:
            cnt = W[n].shape[0] * W[n].shape[1]
            d[n] = upd_rows[k][off:off + cnt].reshape(W[n].shape)
            off += cnt
        d.update(zip(small_names, _unpack(upd_small[k], [W[n].shape for n in small_names])))
        d.update(zip(REPLICATED, _unpack(upd_rep[k], [W[n].shape for n in REPLICATED])))
        res.append(d)
    outs = [loss, grad_x[None]]
    for k in range(4):
        outs += [res[k][n] for n in WEIGHTS]
    return tuple(outs)


def _pack_slots(slot_arrays, row_mult):
    flat = jnp.concatenate([a.reshape(NDEV, -1) for a in slot_arrays], axis=1)
    rows = -(-flat.shape[1] // LANES)
    rows = -(-rows // row_mult) * row_mult
    return jnp.pad(flat, ((0, 0), (0, rows * LANES - flat.shape[1]))).reshape(NDEV, rows, LANES)
```

```python
import functools
import numpy as np
import jax
import jax.numpy as jnp
from jax import lax
from jax.experimental import pallas as pl
from jax.experimental.pallas import tpu as pltpu

f32, bf16 = jnp.float32, jnp.bfloat16
S = jax.ShapeDtypeStruct
HI = lax.Precision.HIGHEST

D = 1024
DEPTH = 4
GRID_W = 64
CHUNK = 128
EPS = 1e-6
RH, RDH, RW = 8, 64, 512
NAH, NADH, NAW = 8, 64, 512
NA_WR, NA_WC = 8, 16
NA_QROWS = 8
NA_KROWS = 16
NA_PAIR = 2
SSD_INNER, SSD_HD, SSD_H, SSD_G, SSD_HPG, SSD_N, SSD_CONV = 2048, 64, 32, 4, 8, 128, 5
SSD_XBC = SSD_INNER + 2 * SSD_G * SSD_N
FFN, FFN_CONV = 2816, 3
FFN_TC = 512
SCAN_HEADS_PER_STEP = 8
ROPE_BASE = 10000.0
LR, B1, B2, AEPS, WD, STEP = 0.001, 0.9, 0.999, 1e-08, 0.01, 10
NDEV = 8
LANES = 128
VMEM_LIMIT = 56 * 1024 * 1024
MM_BLOCK_BYTES = 6 * 1024 * 1024

NT = (((1,), (1,)), ((), ()))
TN = (((0,), (0,)), ((), ()))

SHARDED = [("ab_w_in", 2), ("ab_w_out", 1), ("c_w_in", 2), ("c_w_out", 1), ("ffn_w_up", 2), ("ffn_w_down", 1),
           ("c_conv_w", 2), ("c_conv_b", 1), ("c_norm_g", 1), ("ffn_conv_w", 2)]
N_BIG = 6
COL_SHARDED = ["ab_w_in", "c_w_in", "ffn_w_up"]
ROW_SHARDED = ["ab_w_out", "c_w_out", "ffn_w_down"]
REPLICATED = ["norm_mix_pre", "norm_mix_post", "norm_ffn_pre", "norm_ffn_post", "ab_ret_decay_logit", "ab_ret_gn_g",
              "ab_na_rpb", "c_dt_bias", "c_a_log", "c_d_skip", "ffn_conv_b"]
WEIGHTS = ["norm_mix_pre", "norm_mix_post", "norm_ffn_pre", "norm_ffn_post", "ab_w_in", "ab_ret_decay_logit",
           "ab_ret_gn_g", "ab_na_rpb", "ab_w_out", "c_w_in", "c_conv_w", "c_conv_b", "c_dt_bias", "c_a_log", "c_d_skip",
           "c_norm_g", "c_w_out", "ffn_w_up", "ffn_conv_w", "ffn_conv_b", "ffn_w_down"]


def _params(sem=None):
    return pltpu.CompilerParams(dimension_semantics=sem, vmem_limit_bytes=VMEM_LIMIT)


def _mm_nn(a, w, *, name, tm=1024, tn=512, out_dtype=f32):
    M, K = a.shape
    N = w.shape[1]
    tn = min(tn, N)

    def body(a_ref, w_ref, o_ref):
        o_ref[...] = jnp.dot(a_ref[...], w_ref[...], preferred_element_type=f32).astype(o_ref.dtype)

    return pl.pallas_call(
        body, name=name, grid=(M // tm, N // tn),
        in_specs=[pl.BlockSpec((tm, K), lambda i, j: (i, 0)), pl.BlockSpec((K, tn), lambda i, j: (0, j))],
        out_specs=pl.BlockSpec((tm, tn), lambda i, j: (i, j)),
        out_shape=S((M, N), out_dtype), compiler_params=_params(("parallel", "parallel")))(a, w)


def _mm_nt(dy, w, *, name, add=None, tm=512):
    M, N = dy.shape
    K = w.shape[0]
    tk = next((t for t in (1024, 1408, 512, 256, 128) if K % t == 0 and (t <= 512 or t * N * 2 <= MM_BLOCK_BYTES)), K)

    def body(*refs):
        if add is None:
            d_ref, w_ref, o_ref = refs
            o_ref[...] = lax.dot_general(d_ref[...], w_ref[...], NT, preferred_element_type=f32)
        else:
            d_ref, w_ref, a_ref, o_ref = refs
            o_ref[...] = lax.dot_general(d_ref[...], w_ref[...], NT, preferred_element_type=f32) + a_ref[...]

    in_specs = [pl.BlockSpec((tm, N), lambda i, j: (i, 0)), pl.BlockSpec((tk, N), lambda i, j: (j, 0))]
    args = [dy, w]
    if add is not None:
        in_specs.append(pl.BlockSpec((tm, tk), lambda i, j: (i, j)))
        args.append(add)
    return pl.pallas_call(
        body, name=name, grid=(M // tm, K // tk), in_specs=in_specs,
        out_specs=pl.BlockSpec((tm, tk), lambda i, j: (i, j)),
        out_shape=S((M, K), f32), compiler_params=_params(("parallel", "parallel")))(*args)


def _mm_tn(a, dy, *, name, tt=1024):
    M, K = a.shape
    N = dy.shape[1]
    tk = K if K <= 1024 else (1024 if K % 1024 == 0 else K // 2)
    tn = min(512, N)
    tt = min(tt, M)

    def body(a_ref, d_ref, o_ref):
        t = pl.program_id(2)
        part = lax.dot_general(a_ref[...], d_ref[...], TN, preferred_element_type=f32)

        @pl.when(t == 0)
        def _():
            o_ref[...] = part

        @pl.when(t > 0)
        def _():
            o_ref[...] += part

    return pl.pallas_call(
        body, name=name, grid=(K // tk, N // tn, M // tt),
        in_specs=[pl.BlockSpec((tt, tk), lambda k, n, t: (t, k)), pl.BlockSpec((tt, tn), lambda k, n, t: (t, n))],
        out_specs=pl.BlockSpec((tk, tn), lambda k, n, t: (k, n)),
        out_shape=S((K, N), f32), compiler_params=_params(("parallel", "parallel", "arbitrary")))(a, dy)


def _tile_spec(tm, width, base):
    return pl.BlockSpec((tm, width), lambda j, i: (i, base + j))


def _par_spec(width, base):
    return pl.BlockSpec((1, width), lambda j, i: (0, base + j))


def _full_spec(a):
    nd = a.ndim
    return pl.BlockSpec(a.shape, lambda j, i: (0,) * nd)


def _rowwise(name, f, tiles, ctiles, params, consts, outs, *, tm, J=1):
    T = tiles[0][0].shape[0]
    nt, nct, npar, nc = len(tiles), len(ctiles), len(params), len(consts)

    def body(*refs):
        tv = [r[...].astype(f32) for r in refs[:nt + nct]]
        pv = [r[...] for r in refs[nt + nct:nt + nct + npar + nc]]
        res = f(*tv, *pv)
        for o, v in zip(refs[nt + nct + npar + nc:], res):
            o[...] = v.astype(o.dtype)

    in_specs = ([_tile_spec(tm, w, b) for _, w, b in tiles + ctiles] + [_par_spec(w, b) for _, w, b in params]
                + [_full_spec(c) for c in consts])
    return pl.pallas_call(
        body, name=name, grid=(J, T // tm), in_specs=in_specs,
        out_specs=[_tile_spec(tm, w, 0) for w, _ in outs],
        out_shape=[S((T, J * w), dt) for w, dt in outs],
        compiler_params=_params(("parallel", "parallel")))(
            *[a for a, _, _ in tiles + ctiles], *[a for a, _, _ in params], *consts)


def _rowwise_bwd(name, f, tiles, ctiles, params, consts, douts, dtile_dtypes, *, tm, J=1):
    T = tiles[0][0].shape[0]
    nt, nct, npar, nc, nd = len(tiles), len(ctiles), len(params), len(consts), len(douts)

    def body(*refs):
        i = pl.program_id(1)
        k = 0
        tv = [r[...].astype(f32) for r in refs[k:k + nt]]; k += nt
        cv = [r[...].astype(f32) for r in refs[k:k + nct]]; k += nct
        pv = [r[...] for r in refs[k:k + npar]]; k += npar
        kv = [r[...] for r in refs[k:k + nc]]; k += nc
        dv = [r[...].astype(f32) for r in refs[k:k + nd]]; k += nd
        dt_refs = refs[k:k + nt]; k += nt
        dp_refs = refs[k:k + npar]
        _, vjp = jax.vjp(lambda tv_, pv_: tuple(f(*tv_, *cv, *pv_, *kv)), tv, pv)
        dts, dps = vjp(tuple(dv))
        for r, g in zip(dt_refs, dts):
            r[...] = g.astype(r.dtype)
        for r, g in zip(dp_refs, dps):
            @pl.when(i == 0)
            def _(r=r, g=g):
                r[...] = g

            @pl.when(i > 0)
            def _(r=r, g=g):
                r[...] += g

    in_specs = ([_tile_spec(tm, w, b) for _, w, b in tiles + ctiles] + [_par_spec(w, b) for _, w, b in params]
                + [_full_spec(c) for c in consts] + [_tile_spec(tm, w, b) for _, w, b in douts])
    res = pl.pallas_call(
        body, name=name, grid=(J, T // tm), in_specs=in_specs,
        out_specs=[_tile_spec(tm, w, 0) for _, w, _ in tiles] + [_par_spec(w, b) for _, w, b in params],
        out_shape=[S((T, J * w), dt) for (_, w, _), dt in zip(tiles, dtile_dtypes)] + [S(a.shape, f32) for a, _, _ in params],
        compiler_params=_params(("parallel", "arbitrary")))(
            *[a for a, _, _ in tiles + ctiles], *[a for a, _, _ in params], *consts, *[a for a, _, _ in douts])
    return res[:nt], res[nt:]


def _rms(x, g):
    return x * lax.rsqrt(jnp.mean(x * x, axis=-1, keepdims=True) + EPS) * g


def _f_first(x, g1):
    return (_rms(x, g1),)


def _f_first_bwd(x, g1):
    return (x, _rms(x, g1))


def _f_mid(x, m, g2, g3):
    x1 = x + _rms(m, g2)
    return (x1, _rms(x1, g3))


def _f_end(x1, fo, g4, g1n):
    x2 = x1 + _rms(fo, g4)
    return (x2, _rms(x2, g1n))


def _f_last(x1, fo, g4):
    return (x1 + _rms(fo, g4),)


@jax.custom_vjp
def _swap_halves(x):
    c = x.shape[1]
    lane = lax.broadcasted_iota(jnp.int32, x.shape, 1) % RDH
    return jnp.where(lane < RDH // 2, pltpu.roll(x, c - RDH // 2, axis=1), pltpu.roll(x, RDH // 2, axis=1))


_swap_halves.defvjp(lambda x: (_swap_halves(x), None), lambda _, g: (_swap_halves(g),))


def _f_rprep(rq, rk, cos, sin):
    rot = lambda t: t * cos + _swap_halves(t) * sin
    return (rot(rq), rot(rk) * (RDH ** -0.5))


def _f_rpost(yf, yb, rg, gn, gavg):
    y = yf + yb
    mu = jnp.dot(y, gavg, precision=HI, preferred_element_type=f32)
    yc = y - mu
    var = jnp.dot(yc * yc, gavg, precision=HI, preferred_element_type=f32)
    return (jax.nn.silu(rg) * (yc * lax.rsqrt(var + EPS) * gn),)


def _f_sprep(xs, dtr, dtb, alog, ex0, ex1):
    dt = jax.nn.softplus(dtr + dtb)
    la = dt * (-jnp.exp(alog))
    e0 = jnp.dot(dt, ex0, precision=HI, preferred_element_type=f32)
    e1 = jnp.dot(dt, ex1, precision=HI, preferred_element_type=f32)
    return (xs * e0, xs * e1, la)


def _f_spost(yf, yb, xs, z, dsk, ng):
    y = (yf + yb + xs * dsk) * jax.nn.silu(z)
    y = y * lax.rsqrt(jnp.mean(y * y, axis=-1, keepdims=True) + EPS)
    return (y * ng,)


def _loss_call(y, tgt, *, tm=256):
    T = y.shape[0]

    def body(y_ref, t_ref, dy_ref, l_ref):
        i = pl.program_id(0)
        e = y_ref[...] - t_ref[...]
        dy_ref[...] = e * (1.0 / D)
        part = jnp.zeros((8, LANES), f32) + 0.5 * jnp.sum(jnp.mean(e * e, axis=-1, keepdims=True))

        @pl.when(i == 0)
        def _():
            l_ref[...] = part

        @pl.when(i > 0)
        def _():
            l_ref[...] += part

    return pl.pallas_call(
        body, name="loss_head", grid=(T // tm,),
        in_specs=[pl.BlockSpec((tm, D), lambda i: (i, 0))] * 2,
        out_specs=[pl.BlockSpec((tm, D), lambda i: (i, 0)), pl.BlockSpec((8, LANES), lambda i: (0, 0))],
        out_shape=[S((T, D), f32), S((8, LANES), f32)], compiler_params=_params(("arbitrary",)))(y, tgt)


def _colsum(x, *, name, tm=512):
    T, C = x.shape

    def body(x_ref, o_ref):
        i = pl.program_id(0)
        part = jnp.sum(x_ref[...], axis=0, keepdims=True)

        @pl.when(i == 0)
        def _():
            o_ref[...] = part

        @pl.when(i > 0)
        def _():
            o_ref[...] += part

    return pl.pallas_call(
        body, name=name, grid=(T // tm,), in_specs=[pl.BlockSpec((tm, C), lambda i: (i, 0))],
        out_specs=pl.BlockSpec((1, C), lambda i: (0, 0)), out_shape=S((1, C), f32),
        compiler_params=_params(("arbitrary",)))(x)


def _nn(a, b):
    if a.ndim == 3:
        return lax.dot_general(a, b, (((2,), (1,)), ((0,), (0,))), preferred_element_type=f32)
    return jnp.dot(a, b, preferred_element_type=f32)


def _nt(a, b):
    if a.ndim == 3:
        return lax.dot_general(a, b, (((2,), (2,)), ((0,), (0,))), preferred_element_type=f32)
    return lax.dot_general(a, b, NT, preferred_element_type=f32)


@jax.custom_vjp
def _mm_lt(a, a_t, b):
    return _nn(a_t, b)


_mm_lt.defvjp(lambda a, a_t, b: (_nn(a_t, b), (a, b)),
              lambda res, g: (jnp.zeros_like(res[0]), _nt(g, res[1]), _nn(res[0], g)))


@jax.custom_vjp
def _mm_rt(a, a_t, b):
    return _nn(a, b)


_mm_rt.defvjp(lambda a, a_t, b: (_nn(a, b), (a_t, b)),
              lambda res, g: (_nt(g, res[1]), jnp.zeros_like(res[0]), _nn(res[0], g)))


@jax.custom_vjp
def _masked_mm(s, s_t, d, d_t, v):
    return _nn(s * d, v)


def _masked_mm_bwd(res, g):
    s, s_t, d, d_t, v = res
    da = _nt(g, v)
    return (da * d, jnp.zeros_like(s_t), da * s, jnp.zeros_like(d_t), _nn(s_t * d_t, g))


_masked_mm.defvjp(lambda s, s_t, d, d_t, v: (_nn(s * d, v), (s, s_t, d, d_t, v)), _masked_mm_bwd)


def _scan_step(h, q, k, v, a, rev, for_vjp=False):
    L = q.shape[0]
    Hg = v.shape[0]
    ii = lax.broadcasted_iota(jnp.int32, (L, L), 0)
    jj = lax.broadcasted_iota(jnp.int32, (L, L), 1)
    if rev:
        tri, tri_t, dmask, dmask_t = (jj >= ii), (ii >= jj), (jj > ii), (ii > jj)
    else:
        tri, tri_t, dmask, dmask_t = (jj <= ii), (ii <= jj), (jj <= ii), (ii <= jj)
    cs = jnp.dot(tri.astype(f32), a, precision=HI, preferred_element_type=f32)
    cs_t = lax.dot_general(a, tri_t.astype(f32), TN, precision=HI, preferred_element_type=f32)
    tot = jnp.sum(a, axis=0, keepdims=True)
    qk = _nt(q, k)
    k_t = k.T
    if for_vjp:
        q_t = lax.stop_gradient(q.T)
        qk_t = lax.stop_gradient(_nt(k, q))
    P = v.shape[2]
    hs, ys = [], []
    for hh in range(Hg):
        c_col, c_row = jnp.broadcast_to(cs[:, hh:hh + 1], (L, L)), cs_t[hh:hh + 1, :]
        dec = jnp.exp(jnp.where(dmask, c_col - c_row, -1e30))
        t_all = tot[:, hh:hh + 1]
        e_in, e_out = jnp.exp(c_col)[:, :P], jnp.exp(t_all - c_col)[:, :P]
        w = v[hh] * e_out
        if for_vjp:
            dec_t = lax.stop_gradient(jnp.exp(jnp.where(dmask_t, c_row - c_col, -1e30)))
            y = _masked_mm(qk, qk_t, dec, dec_t, v[hh]) + _mm_rt(q, q_t, h[hh]) * e_in
            hn = h[hh] * jnp.exp(t_all) + _mm_lt(lax.stop_gradient(k), k_t, w)
        else:
            y = _nn(qk * dec, v[hh]) + _nn(q, h[hh]) * e_in
            hn = h[hh] * jnp.exp(t_all) + _nn(k_t, w)
        hs.append(hn)
        ys.append(y)
    return jnp.stack(hs), jnp.stack(ys)


def _scan_step_heads(h, q, k, v, a, rev, for_vjp=False):
    B, L, P = v.shape
    ii = lax.broadcasted_iota(jnp.int32, (L, L), 0)
    jj = lax.broadcasted_iota(jnp.int32, (L, L), 1)
    if rev:
        tri, tri_t, dmask, dmask_t = (jj >= ii), (ii >= jj), (jj > ii), (ii > jj)
    else:
        tri, tri_t, dmask, dmask_t = (jj <= ii), (ii <= jj), (jj <= ii), (ii <= jj)
    cs = jnp.dot(tri.astype(f32), a, precision=HI, preferred_element_type=f32)
    cs_t = lax.dot_general(a, tri_t.astype(f32), TN, precision=HI, preferred_element_type=f32)
    tot = jnp.sum(a, axis=0, keepdims=True)
    c_col = jnp.stack([jnp.broadcast_to(cs[:, b:b + 1], (L, L)) for b in range(B)])
    c_row = jnp.stack([cs_t[b:b + 1, :] for b in range(B)])
    t_all = jnp.stack([tot[:, b:b + 1] for b in range(B)])
    dec = jnp.exp(jnp.where(dmask[None], c_col - c_row, -1e30))
    e_in, e_out = jnp.exp(c_col)[:, :, :P], jnp.exp(t_all - c_col)[:, :, :P]
    qk = _nt(q, k)
    k_t = jnp.swapaxes(k, 1, 2)
    w = v * e_out
    if for_vjp:
        q_t = lax.stop_gradient(jnp.swapaxes(q, 1, 2))
        qk_t = lax.stop_gradient(_nt(k, q))
        dec_t = lax.stop_gradient(jnp.exp(jnp.where(dmask_t[None], c_row - c_col, -1e30)))
        y = _masked_mm(qk, qk_t, dec, dec_t, v) + _mm_rt(q, q_t, h) * e_in
        hn = h * jnp.exp(t_all) + _mm_lt(lax.stop_gradient(k), k_t, w)
    else:
        y = _nn(qk * dec, v) + _nn(q, h) * e_in
        hn = h * jnp.exp(t_all) + _nn(k_t, w)
    return hn, y


def _scan_specs(gb, N, Hg, P, Ha, cm, qcol, kcol, vcol):
    qs = lambda col: pl.BlockSpec((CHUNK, gb * N), lambda g, c: (cm(c), col + g))
    vs = lambda col: pl.BlockSpec((CHUNK, gb * Hg * P), lambda g, c: (cm(c), col + g))
    as_ = pl.BlockSpec((1, CHUNK, Ha), lambda g, c: (g, cm(c), 0))
    hs = pl.BlockSpec((gb, 1, Hg, N, P), lambda g, c: (g, cm(c), 0, 0, 0))
    return qs(qcol), qs(kcol), vs(vcol), qs(0), vs(0), as_, hs


def _lanes(ref, n, width):
    return jnp.stack([ref[:, j * width:(j + 1) * width] for j in range(n)])


def _scan_fwd(q, k, v, a, *, G, N, Hg, P, qcol=0, kcol=0, vcol=0, rev, name):
    T, Ha, NC = q.shape[0], a.shape[2], q.shape[0] // CHUNK
    gb = SCAN_HEADS_PER_STEP // Hg
    cm = (lambda c: NC - 1 - c) if rev else (lambda c: c)
    qs, ks, vs, _, ys, as_, hs = _scan_specs(gb, N, Hg, P, Ha, cm, qcol, kcol, vcol)

    def body(q_ref, k_ref, v_ref, a_ref, y_ref, hs_ref, h_scr):
        @pl.when(pl.program_id(1) == 0)
        def _():
            h_scr[...] = jnp.zeros_like(h_scr)

        if Hg == 1:
            h = h_scr[:, 0]
            hs_ref[:, 0, 0] = h
            hn, y = _scan_step_heads(h, _lanes(q_ref, gb, N), _lanes(k_ref, gb, N), _lanes(v_ref, gb, P), a_ref[0], rev)
            h_scr[:, 0] = hn
        else:
            h = h_scr[0]
            hs_ref[0, 0] = h
            hn, y = _scan_step(h, q_ref[...], k_ref[...], _lanes(v_ref, Hg, P), a_ref[0], rev)
            h_scr[0] = hn
        for j in range(gb * Hg):
            y_ref[:, j * P:(j + 1) * P] = y[j]

    return pl.pallas_call(
        body, name=name, grid=(G // gb, NC), in_specs=[qs, ks, vs, as_], out_specs=[ys, hs],
        out_shape=[S((T, G * Hg * P), f32), S((G, NC, Hg, N, P), f32)],
        scratch_shapes=[pltpu.VMEM((gb, Hg, N, P), f32)],
        compiler_params=_params(("parallel", "arbitrary")))(q, k, v, a)


def _scan_bwd(q, k, v, a, hsave, dy, *, G, N, Hg, P, qcol=0, kcol=0, vcol=0, rev, name):
    T, Ha, NC = q.shape[0], a.shape[2], q.shape[0] // CHUNK
    gb = SCAN_HEADS_PER_STEP // Hg
    cm = (lambda c: c) if rev else (lambda c: NC - 1 - c)
    qs, ks, vs, dqs, dvs, as_, hs = _scan_specs(gb, N, Hg, P, Ha, cm, qcol, kcol, vcol)

    def body(q_ref, k_ref, v_ref, a_ref, hs_ref, dy_ref, dq_ref, dk_ref, dv_ref, da_ref, dh_scr):
        @pl.when(pl.program_id(1) == 0)
        def _():
            dh_scr[...] = jnp.zeros_like(dh_scr)

        if Hg == 1:
            _, vjp = jax.vjp(functools.partial(_scan_step_heads, rev=rev, for_vjp=True), hs_ref[:, 0, 0], _lanes(q_ref, gb, N),
                             _lanes(k_ref, gb, N), _lanes(v_ref, gb, P), a_ref[0])
            dh, dq, dk, dv, da = vjp((dh_scr[:, 0], _lanes(dy_ref, gb, P)))
            dh_scr[:, 0] = dh
            for j in range(gb):
                dq_ref[:, j * N:(j + 1) * N] = dq[j]
                dk_ref[:, j * N:(j + 1) * N] = dk[j]
        else:
            _, vjp = jax.vjp(functools.partial(_scan_step, rev=rev, for_vjp=True), hs_ref[0, 0], q_ref[...], k_ref[...],
                             _lanes(v_ref, Hg, P), a_ref[0])
            dh, dq, dk, dv, da = vjp((dh_scr[0], _lanes(dy_ref, Hg, P)))
            dh_scr[0] = dh
            dq_ref[...] = dq
            dk_ref[...] = dk
        for j in range(gb * Hg):
            dv_ref[:, j * P:(j + 1) * P] = dv[j]
        da_ref[0] = da

    return pl.pallas_call(
        body, name=name, grid=(G // gb, NC), in_specs=[qs, ks, vs, as_, hs, dvs], out_specs=[dqs, dqs, dvs, as_],
        out_shape=[S((T, G * N), f32), S((T, G * N), f32), S((T, G * Hg * P), f32), S(a.shape, f32)],
        scratch_shapes=[pltpu.VMEM((gb, Hg, N, P), f32)],
        compiler_params=_params(("parallel", "arbitrary")))(q, k, v, a, hsave, dy)


def _na_block_case(rb, nrb):
    return jnp.where(rb == 0, 0, jnp.where(rb == nrb - 1, 2, 1))


def _na_key_start(rb, rows):
    return pl.multiple_of(jnp.clip(rb * NA_QROWS - NA_WR // 2, 0, rows - NA_KROWS) * GRID_W, 256)


def _na_specs(T, nrb):
    nq, nk, wb = NA_QROWS * GRID_W, NA_KROWS * GRID_W, NA_PAIR * NADH
    qs = lambda col: pl.BlockSpec((nq, wb), lambda p, r: (r, col + p))
    fs = lambda col: pl.BlockSpec((T, wb), lambda p, r: (0, col + p))
    bs = pl.BlockSpec((NA_PAIR, 1, nq, nk), lambda p, r: (p, _na_block_case(r, nrb), 0, 0))
    ls = pl.BlockSpec((1, nq, NA_PAIR), lambda p, r: (p, r, 0))
    return qs, fs, bs, ls


def _na_fwd(q, k, v, bias, *, qcol, kcol, vcol, name):
    T = q.shape[0]
    rows = T // GRID_W
    nq, nk = NA_QROWS * GRID_W, NA_KROWS * GRID_W
    nrb = T // nq
    scale = NADH ** -0.5
    qs, fs, bs, ls = _na_specs(T, nrb)

    def body(q_ref, k_ref, v_ref, b_ref, o_ref, l_ref):
        ks = _na_key_start(pl.program_id(1), rows)
        for hh in range(NA_PAIR):
            sl = slice(hh * NADH, (hh + 1) * NADH)
            kw = k_ref[pl.ds(ks, nk), sl]
            vw = v_ref[pl.ds(ks, nk), sl]
            s = lax.dot_general(q_ref[:, sl], kw, NT, preferred_element_type=f32) * scale + b_ref[hh, 0]
            m = jnp.max(s, axis=1, keepdims=True)
            p = jnp.exp(s - m)
            l = jnp.sum(p, axis=1, keepdims=True)
            o_ref[:, sl] = jnp.dot(p.astype(bf16), vw, preferred_element_type=f32) / l
            l_ref[0, :, hh:hh + 1] = m + jnp.log(l)

    return pl.pallas_call(
        body, name=name, grid=(NAH // NA_PAIR, nrb), in_specs=[qs(qcol), fs(kcol), fs(vcol), bs],
        out_specs=[qs(0), ls], out_shape=[S((T, NAW), f32), S((NAH // NA_PAIR, T, NA_PAIR), f32)],
        compiler_params=_params(("parallel", "arbitrary")))(q, k, v, bias)


def _na_bwd(q, k, v, bias, o, lse, do, *, qcol, kcol, vcol, docol, name):
    T = q.shape[0]
    rows = T // GRID_W
    nq, nk = NA_QROWS * GRID_W, NA_KROWS * GRID_W
    nrb = T // nq
    scale = NADH ** -0.5
    qs, fs, bs, ls = _na_specs(T, nrb)

    def body(q_ref, k_ref, v_ref, b_ref, o_ref, l_ref, do_ref, dq_ref, dk_ref, dv_ref, db_ref):
        rb = pl.program_id(1)

        @pl.when(rb == 0)
        def _():
            dk_ref[...] = jnp.zeros_like(dk_ref)
            dv_ref[...] = jnp.zeros_like(dv_ref)

        ks = _na_key_start(rb, rows)
        first = (rb == 0) | (rb == 1) | (rb == nrb - 1)
        for hh in range(NA_PAIR):
            sl = slice(hh * NADH, (hh + 1) * NADH)
            qv = q_ref[:, sl]
            kw = k_ref[pl.ds(ks, nk), sl]
            vw = v_ref[pl.ds(ks, nk), sl]
            s = lax.dot_general(qv, kw, NT, preferred_element_type=f32) * scale + b_ref[hh, 0]
            p = jnp.exp(s - l_ref[0, :, hh:hh + 1])
            do_ = do_ref[:, sl]
            dob = do_.astype(bf16)
            dp = lax.dot_general(dob, vw, NT, preferred_element_type=f32)
            ds = p * (dp - jnp.sum(do_ * o_ref[:, sl], axis=1, keepdims=True))
            dsb = ds.astype(bf16)
            dq_ref[:, sl] = jnp.dot(dsb, kw, preferred_element_type=f32) * scale
            dk_ref[pl.ds(ks, nk), sl] += lax.dot_general(dsb, qv, TN, preferred_element_type=f32) * scale
            dv_ref[pl.ds(ks, nk), sl] += lax.dot_general(p.astype(bf16), dob, TN, preferred_element_type=f32)

            @pl.when(first)
            def _(hh=hh, ds=ds):
                db_ref[hh, 0] = ds

            @pl.when(jnp.logical_not(first))
            def _(hh=hh, ds=ds):
                db_ref[hh, 0] += ds

    return pl.pallas_call(
        body, name=name, grid=(NAH // NA_PAIR, nrb),
        in_specs=[qs(qcol), fs(kcol), fs(vcol), bs, qs(0), ls, qs(docol)],
        out_specs=[qs(0), fs(0), fs(0), bs],
        out_shape=[S((T, NAW), f32), S((T, NAW), f32), S((T, NAW), f32), S(bias.shape, f32)],
        compiler_params=_params(("parallel", "arbitrary")))(q, k, v, bias, o, lse, do)


def _na_col_tables():
    c = np.arange(GRID_W)[:, None]
    kc = np.arange(GRID_W)[None, :]
    cstart = np.clip(c - NA_WC // 2, 0, GRID_W - NA_WC)
    valid_c = (kc >= cstart) & (kc < cstart + NA_WC)
    dc = kc - c + NA_WC - 1
    E = (valid_c[:, :, None] & (dc[:, :, None] == np.arange(2 * NA_WC - 1)[None, None, :])).astype(np.float32)
    return E, np.where(valid_c, 0.0, -1e30).astype(np.float32)


def _na_row_offsets(rows):
    table = []
    for r0 in (0, NA_QROWS, rows - NA_QROWS):
        ks = int(np.clip(r0 - NA_WR // 2, 0, rows - NA_KROWS))
        case = []
        for ri in range(NA_QROWS):
            r = r0 + ri
            rs = int(np.clip(r - NA_WR // 2, 0, rows - NA_WR))
            case.append([ks + kri - r + NA_WR - 1 if rs <= ks + kri < rs + NA_WR else None for kri in range(NA_KROWS)])
        table.append(case)
    return table


def _na_col_bias(rpb):
    E, cmask = _na_col_tables()
    return jnp.einsum("hde,cke->hdck", rpb, E, precision=HI) + cmask


def _na_bias_build(r1, rows, *, name):
    H = r1.shape[0]
    offs = _na_row_offsets(rows)

    def body(r_ref, o_ref):
        outside = jnp.full((GRID_W, GRID_W), -1e30, f32)
        for z in range(3):
            for a in range(NA_QROWS):
                for b in range(NA_KROWS):
                    d = offs[z][a][b]
                    o_ref[0, z, a * GRID_W:(a + 1) * GRID_W, b * GRID_W:(b + 1) * GRID_W] = outside if d is None else r_ref[0, d]

    return pl.pallas_call(
        body, name=name, grid=(H,), in_specs=[pl.BlockSpec((1,) + r1.shape[1:], lambda h: (h, 0, 0, 0))],
        out_specs=pl.BlockSpec((1, 3, NA_QROWS * GRID_W, NA_KROWS * GRID_W), lambda h: (h, 0, 0, 0)),
        out_shape=S((H, 3, NA_QROWS * GRID_W, NA_KROWS * GRID_W), f32), compiler_params=_params(("parallel",)))(r1)


def _na_bias_fold(dbias, rows, *, name):
    H = dbias.shape[0]
    offs = _na_row_offsets(rows)
    nd = 2 * NA_WR - 1

    def body(d_ref, o_ref):
        acc = [None] * nd
        for z in range(3):
            for a in range(NA_QROWS):
                for b in range(NA_KROWS):
                    d = offs[z][a][b]
                    if d is not None:
                        t = d_ref[0, z, a * GRID_W:(a + 1) * GRID_W, b * GRID_W:(b + 1) * GRID_W]
                        acc[d] = t if acc[d] is None else acc[d] + t
        for d in range(nd):
            o_ref[0, d] = acc[d]

    return pl.pallas_call(
        body, name=name, grid=(H,), in_specs=[pl.BlockSpec((1,) + dbias.shape[1:], lambda h: (h, 0, 0, 0))],
        out_specs=pl.BlockSpec((1, nd, GRID_W, GRID_W), lambda h: (h, 0, 0, 0)),
        out_shape=S((H, nd, GRID_W, GRID_W), f32), compiler_params=_params(("parallel",)))(dbias)


def _conv_shifts(prev, cur, nxt, i, n_i, W):
    tm = cur.shape[0]
    prev = jnp.where(i > 0, prev, 0.0)
    nxt = jnp.where(i < n_i - 1, nxt, 0.0)
    ext = jnp.concatenate([prev, cur, nxt], axis=0)
    out = []
    for w in range(W):
        s = (W // 2 - w) % (tm + 16)
        out.append((ext if s == 0 else pltpu.roll(ext, s, axis=0))[8:8 + tm])
    return out


def _conv_act(u, mode):
    if mode == "silu":
        return jax.nn.silu(u)
    if mode == "geglu":
        half = u.shape[1] // 2
        return jax.nn.gelu(u[:, :half], approximate=True) * u[:, half:]
    return u


def _conv_specs(T, tm, tc, xbase):
    r8 = tm // 8
    last = T // 8 - 1
    cur = pl.BlockSpec((tm, tc), lambda j, i: (i, xbase + j))
    prev = pl.BlockSpec((8, tc), lambda j, i: (jnp.maximum(i * r8 - 1, 0), xbase + j))
    nxt = pl.BlockSpec((8, tc), lambda j, i: (jnp.minimum((i + 1) * r8, last), xbase + j))
    return cur, prev, nxt


def _conv(x, w8, b, *, mode, W, name, C, xbase=0, tm=512, tc=512, out_dtype=f32):
    T = x.shape[0]
    NI, J = T // tm, C // tc
    tco = tc // 2 if mode == "geglu" else tc
    cur, prev, nxt = _conv_specs(T, tm, tc, xbase)

    def body(xc, xp, xn, w_ref, b_ref, o_ref):
        sh = _conv_shifts(xp[...].astype(f32), xc[...].astype(f32), xn[...].astype(f32), pl.program_id(1), NI, W)
        wv = w_ref[...]
        u = sh[0] * wv[0:1, :]
        for w in range(1, W):
            u = u + sh[w] * wv[w:w + 1, :]
        if mode != "none":
            u = u + b_ref[...]
        o_ref[...] = _conv_act(u, mode).astype(o_ref.dtype)

    return pl.pallas_call(
        body, name=name, grid=(J, NI),
        in_specs=[cur, prev, nxt, pl.BlockSpec((8, tc), lambda j, i: (0, j)), pl.BlockSpec((1, tc), lambda j, i: (0, j))],
        out_specs=pl.BlockSpec((tm, tco), lambda j, i: (i, j)), out_shape=S((T, J * tco), out_dtype),
        compiler_params=_params(("parallel", "parallel")))(x, x, x, w8, b)


def _conv_bwd(x, w8, b, dact, *, mode, W, name, C, xbase=0, tm=512, tc=512):
    T = x.shape[0]
    NI, J = T // tm, C // tc
    tco = tc // 2 if mode == "geglu" else tc
    rows = tm + 16
    pad = W // 2
    cur, prev, nxt = _conv_specs(T, tm, tc, xbase)
    dcur, dprev, dnxt = _conv_specs(T, tm, tco, 0)

    def body(xc, xp, xn, w_ref, b_ref, dc, dp, dn, dx_ref, dw_ref, db_ref):
        i = pl.program_id(1)
        ext = jnp.concatenate([jnp.where(i > 0, xp[...], 0.0), xc[...], jnp.where(i < NI - 1, xn[...], 0.0)], axis=0)
        dext = jnp.concatenate([jnp.where(i > 0, dp[...], 0.0), dc[...], jnp.where(i < NI - 1, dn[...], 0.0)], axis=0)
        wv = w_ref[...]
        shift = lambda t, w: t if w == pad else pltpu.roll(t, (pad - w) % rows, axis=0)
        xs = [shift(ext, w) for w in range(W)]
        u = b_ref[...] + xs[0] * wv[0:1, :]
        for w in range(1, W):
            u = u + xs[w] * wv[w:w + 1, :]
        _, vjp = jax.vjp(functools.partial(_conv_act, mode=mode), u)
        du = vjp(dext.astype(f32))[0]
        dx = shift(du, 0)[8:8 + tm] * wv[W - 1:W, :]
        for w in range(1, W):
            dx = dx + shift(du, w)[8:8 + tm] * wv[W - 1 - w:W - w, :]
        dx_ref[...] = dx.astype(dx_ref.dtype)

        @pl.when(i == 0)
        def _():
            dw_ref[...] = jnp.zeros_like(dw_ref)
            db_ref[...] = jnp.zeros_like(db_ref)

        dum = du[8:8 + tm]
        db_ref[...] += jnp.sum(dum, axis=0, keepdims=True)
        for w in range(W):
            dw_ref[w:w + 1, :] += jnp.sum(dum * xs[w][8:8 + tm], axis=0, keepdims=True)

    return pl.pallas_call(
        body, name=name, grid=(J, NI),
        in_specs=[cur, prev, nxt, pl.BlockSpec((8, tc), lambda j, i: (0, j)), pl.BlockSpec((1, tc), lambda j, i: (0, j)),
                  dcur, dprev, dnxt],
        out_specs=[pl.BlockSpec((tm, tc), lambda j, i: (i, j)), pl.BlockSpec((8, tc), lambda j, i: (0, j)),
                   pl.BlockSpec((1, tc), lambda j, i: (0, j))],
        out_shape=[S((T, C), bf16), S((8, C), f32), S((1, C), f32)],
        compiler_params=_params(("parallel", "arbitrary")))(x, x, x, w8, b, dact, dact, dact)


def _pad8(w):
    return jnp.concatenate([w, jnp.zeros((8 - w.shape[0], w.shape[1]), w.dtype)], axis=0)


def _all_gather(arrs, *, name):
    n = len(arrs)

    def body(*refs):
        ins, outs = refs[:n], refs[n:2 * n]
        send_sems, recv_sems, loc_sems = refs[2 * n:]
        x, y, c = lax.axis_index("x"), lax.axis_index("y"), lax.axis_index("c")
        ident = lambda px, py, pc: 4 * px + 2 * py + pc
        me, sibling = (x, y, c), (x, y, 1 - c)
        chips = [(1 - x, y), (x, 1 - y), (1 - x, 1 - y)]

        def copy(a, k, block, to, src=None):
            slot = outs[a].at[ident(*block)]
            return pltpu.make_async_remote_copy(
                src_ref=slot if src is None else src, dst_ref=slot, send_sem=send_sems.at[a * 7 + k], recv_sem=recv_sems.at[a * 7 + k],
                device_id=to, device_id_type=pl.DeviceIdType.MESH)

        local = [pltpu.make_async_copy(ins[a], outs[a].at[ident(*me)], loc_sems.at[a]) for a in range(n)]
        for cp in local:
            cp.start()
        first = []
        for a in range(n):
            first.append(copy(a, 0, me, sibling, src=ins[a]))
            first += [copy(a, 1 + j, me, (*chip, c), src=ins[a]) for j, chip in enumerate(chips)]
        for cp in first:
            cp.start()
        passed = []
        for j, chip in enumerate(chips):
            for a in range(n):
                copy(a, 1 + j, (*chip, c), me).wait_recv()
                fwd = copy(a, 4 + j, (*chip, c), sibling)
                fwd.start()
                passed.append(fwd)
        for a in range(n):
            copy(a, 0, sibling, me).wait_recv()
            for j, chip in enumerate(chips):
                copy(a, 4 + j, (*chip, 1 - c), me).wait_recv()
        for cp in first + passed:
            cp.wait_send()
        for cp in local:
            cp.wait()

    any_spec = pl.BlockSpec(memory_space=pl.ANY)
    return pl.pallas_call(
        body, name=name, in_specs=[any_spec] * n, out_specs=[any_spec] * n,
        out_shape=[S((NDEV,) + a.shape, a.dtype) for a in arrs],
        scratch_shapes=[pltpu.SemaphoreType.DMA((7 * n,)), pltpu.SemaphoreType.DMA((7 * n,)), pltpu.SemaphoreType.DMA((n,))],
        )(*arrs)


NCHIP = NDEV // 2


def _to_sibling(arrs, *, name):
    n = len(arrs)
    ncopy = sum(NCHIP if ps else 1 for _, ps in arrs)

    def body(*refs):
        ins, outs = refs[:n], refs[n:2 * n]
        send_sems, recv_sems = refs[2 * n:]
        x, y, c = lax.axis_index("x"), lax.axis_index("y"), lax.axis_index("c")
        copies, idx = [], 0
        for a, (_, per_slot) in enumerate(arrs):
            pairs = [(ins[a].at[2 * q + (1 - c)], outs[a].at[q]) for q in range(NCHIP)] if per_slot else [(ins[a], outs[a])]
            for src, dst in pairs:
                copies.append(pltpu.make_async_remote_copy(
                    src_ref=src, dst_ref=dst, send_sem=send_sems.at[idx], recv_sem=recv_sems.at[idx],
                    device_id=(x, y, 1 - c), device_id_type=pl.DeviceIdType.MESH))
                idx += 1
        for cp in copies:
            cp.start()
        for cp in copies:
            cp.wait_recv()
        for cp in copies:
            cp.wait_send()

    any_spec = pl.BlockSpec(memory_space=pl.ANY)
    return pl.pallas_call(
        body, name=name, in_specs=[any_spec] * n, out_specs=[any_spec] * n,
        out_shape=[S((NCHIP,) + a.shape[1:] if ps else a.shape, a.dtype) for a, ps in arrs],
        scratch_shapes=[pltpu.SemaphoreType.DMA((ncopy,)), pltpu.SemaphoreType.DMA((ncopy,))])(*[a for a, _ in arrs])


def _add_partials(mine, theirs, *, per_slot, tr, name):
    R, C = mine.shape[-2:]

    def body(a_ref, b_ref, o_ref):
        a = a_ref[lax.axis_index("c")] if per_slot else a_ref[...]
        b = b_ref[0] if per_slot else b_ref[...]
        s = a.astype(f32) + b.astype(f32)
        if per_slot:
            o_ref[0] = s.astype(o_ref.dtype)
        else:
            o_ref[...] = s.astype(o_ref.dtype)

    if per_slot:
        grid = (NCHIP, R // tr)
        in_specs = [pl.BlockSpec((2, tr, C), lambda q, i: (q, i, 0)), pl.BlockSpec((1, tr, C), lambda q, i: (q, i, 0))]
        out_spec, out_shape = pl.BlockSpec((1, tr, C), lambda q, i: (q, i, 0)), S((NCHIP, R, C), mine.dtype)
    else:
        grid = (1, R // tr)
        in_specs = [pl.BlockSpec((tr, C), lambda q, i: (i, 0))] * 2
        out_spec, out_shape = pl.BlockSpec((tr, C), lambda q, i: (i, 0)), S((R, C), mine.dtype)
    return pl.pallas_call(body, name=name, grid=grid, in_specs=in_specs, out_specs=out_spec, out_shape=out_shape,
                          compiler_params=_params(("parallel", "parallel")))(mine, theirs)


def _to_chips(arrs, *, name):
    n = len(arrs)

    def body(*refs):
        ins, outs = refs[:n], refs[n:2 * n]
        send_sems, recv_sems, loc_sems = refs[2 * n:]
        x, y, c = lax.axis_index("x"), lax.axis_index("y"), lax.axis_index("c")
        my_q = 2 * x + y
        src = lambda a, q: ins[a].at[q] if arrs[a][1] else ins[a]
        local = [pltpu.make_async_copy(src(a, my_q), outs[a].at[my_q], loc_sems.at[a]) for a in range(n)]
        for cp in local:
            cp.start()
        sent = []
        for j, (px, py) in enumerate([(1 - x, y), (x, 1 - y), (1 - x, 1 - y)]):
            q = 2 * px + py
            for a in range(n):
                mk = lambda slot, a=a, j=j, q=q, dev=(px, py, c): pltpu.make_async_remote_copy(
                    src_ref=src(a, q), dst_ref=outs[a].at[slot], send_sem=send_sems.at[3 * a + j], recv_sem=recv_sems.at[3 * a + j],
                    device_id=dev, device_id_type=pl.DeviceIdType.MESH)
                mk(my_q).start()
                sent.append((mk, q))
        for mk, q in sent:
            mk(q).wait_recv()
        for mk, q in sent:
            mk(q).wait_send()
        for cp in local:
            cp.wait()

    any_spec = pl.BlockSpec(memory_space=pl.ANY)
    return pl.pallas_call(
        body, name=name, in_specs=[any_spec] * n, out_specs=[any_spec] * n,
        out_shape=[S(a.shape if ps else (NCHIP,) + a.shape, a.dtype) for a, ps in arrs],
        scratch_shapes=[pltpu.SemaphoreType.DMA((3 * n,)), pltpu.SemaphoreType.DMA((3 * n,)), pltpu.SemaphoreType.DMA((n,))],
        )(*[a for a, _ in arrs])


def _adamw(r, w, m, v, *, name, tr):
    M, C = w.shape
    nparts = r.shape[0]

    def body(r_ref, w_ref, m_ref, v_ref, g_ref, d_ref, nm_ref, nv_ref):
        g = r_ref[0].astype(f32)
        for s in range(1, nparts):
            g = g + r_ref[s].astype(f32)
        m_ = B1 * m_ref[...] + (1.0 - B1) * g
        v_ = B2 * v_ref[...] + (1.0 - B2) * jnp.square(g)
        m_hat = m_ / (1.0 - B1 ** STEP)
        v_hat = v_ / (1.0 - B2 ** STEP)
        g_ref[...] = g
        d_ref[...] = -LR * (m_hat / (jnp.sqrt(v_hat) + AEPS) + WD * w_ref[...])
        nm_ref[...] = m_
        nv_ref[...] = v_

    row = pl.BlockSpec((tr, C), lambda i: (i, 0))
    return pl.pallas_call(
        body, name=name, grid=(M // tr,),
        in_specs=[pl.BlockSpec((nparts, tr, C), lambda i: (0, i, 0)), row, row, row],
        out_specs=[row] * 4, out_shape=[S((M, C), f32)] * 4, compiler_params=_params(("parallel",)))(r, w, m, v)


def _colmove(ins, in_slots, outs, moves, *, tk, name):
    R = ins[0].shape[1] if in_slots[0] else ins[0].shape[0]
    n_in = len(ins)

    def body(*refs):
        for ii, isl, ic, oi, osl, oc, w in moves:
            src, dst = refs[ii], refs[n_in + oi]
            val = src[:, ic:ic + w] if isl is None else src[isl, :, ic:ic + w]
            if osl is None:
                dst[:, oc:oc + w] = val.astype(dst.dtype)
            else:
                dst[osl, :, oc:oc + w] = val.astype(dst.dtype)

    def spec(is_slots, C):
        return pl.BlockSpec((NDEV, tk, C), lambda i: (0, i, 0)) if is_slots else pl.BlockSpec((tk, C), lambda i: (i, 0))

    return pl.pallas_call(
        body, name=name, grid=(R // tk,),
        in_specs=[spec(sl, a.shape[-1]) for a, sl in zip(ins, in_slots)],
        out_specs=[spec(sl, C) for sl, C, _ in outs],
        out_shape=[S((NDEV, R, C) if sl else (R, C), dt) for sl, C, dt in outs],
        compiler_params=_params(("parallel",)))(*ins)


def _col_pieces(n8, cuts, place):
    out = []
    for p in range(NDEV):
        lo, hi = p * n8, (p + 1) * n8
        edges = [lo] + [c for c in cuts if lo < c < hi] + [hi]
        for a, b in zip(edges[:-1], edges[1:]):
            out.append((p, a - lo) + place(a) + (b - a,))
    return out


def _place_plain(c):
    return (0, c)


def _place_ssd_in(c):
    return (0, c) if c < SSD_INNER + SSD_XBC else (1, c - (SSD_INNER + SSD_XBC))


def _place_ffn_up(c):
    h = FFN_TC // 2
    return (0, (c // h) * FFN_TC + c % h) if c < FFN else (0, ((c - FFN) // h) * FFN_TC + h + (c - FFN) % h)


_COL_LAYOUTS = {
    "ab_w_in": ([], _place_plain, [4 * RW + 3 * NAW]),
    "c_w_in": ([SSD_INNER + SSD_XBC], _place_ssd_in, [SSD_INNER + SSD_XBC, 2 * SSD_H]),
    "ffn_w_up": (list(range(FFN_TC // 2, 2 * FFN, FFN_TC // 2)), _place_ffn_up, [2 * FFN]),
}


def _cols_from_slots(g, which, *, name):
    cuts, place, widths = _COL_LAYOUTS[which]
    moves = [(0, p, sc, mi, None, mc, w) for p, sc, mi, mc, w in _col_pieces(g.shape[2], cuts, place)]
    return _colmove([g], [True], [(False, w, g.dtype) for w in widths], moves, tk=256, name=name)


def _cols_to_slots(mats, which, dtype, *, name):
    cuts, place, widths = _COL_LAYOUTS[which]
    n8 = sum(widths) // NDEV
    moves = [(mi, None, mc, 0, p, sc, w) for p, sc, mi, mc, w in _col_pieces(n8, cuts, place)]
    return _colmove(list(mats), [False] * len(mats), [(True, n8, dtype)], moves, tk=256, name=name)[0]


def _tm2hm(a, H):
    T = a.shape[0]
    return a.reshape(T, H, -1).transpose(1, 0, 2)


def _hm2tm(a):
    H, T, P = a.shape
    return a.transpose(1, 0, 2).reshape(T, H * P)


def _pack(parts, dtype, row_mult):
    flat = jnp.concatenate([p.reshape(-1).astype(dtype) for p in parts])
    rows = -(-flat.shape[0] // LANES)
    rows = -(-rows // row_mult) * row_mult
    return jnp.pad(flat, (0, rows * LANES - flat.shape[0])).reshape(rows, LANES)


def _unpack(buf, shapes, lead=()):
    flat = buf.reshape(lead + (-1,))
    out, off = [], 0
    for shp in shapes:
        n = int(np.prod(shp))
        out.append(flat[..., off:off + n].reshape(lead + tuple(shp)))
        off += n
    return out


def _to_slots(full, ax):
    shp = full.shape
    return jnp.moveaxis(full.reshape(shp[:ax] + (NDEV, shp[ax] // NDEV) + shp[ax + 1:]), ax, 0)


def _from_slots(g, ax):
    t = jnp.moveaxis(g, 0, ax)
    shp = t.shape
    return t.reshape(shp[:ax] + (shp[ax] * shp[ax + 1],) + shp[ax + 2:])


def _ffn_perm(a):
    lead = a.shape[:-1]
    h = FFN_TC // 2
    return jnp.swapaxes(a.reshape(lead + (2, FFN // h, h)), -3, -2).reshape(lead + (2 * FFN,))


def _ffn_unperm(a):
    lead = a.shape[:-1]
    h = FFN_TC // 2
    return jnp.swapaxes(a.reshape(lead + (FFN // h, 2, h)), -3, -2).reshape(lead + (2 * FFN,))


def _rope_tables(T):
    half = RDH // 2
    inv = 1.0 / (ROPE_BASE ** (jnp.arange(half, dtype=f32) / half))
    ang = jnp.arange(T, dtype=f32)[:, None] * inv[None, :]
    cos, sin = jnp.cos(ang), jnp.sin(ang)
    cos_t = jnp.tile(jnp.concatenate([cos, cos], axis=1), (1, RH))
    sin_t = jnp.tile(jnp.concatenate([-sin, sin], axis=1), (1, RH))
    return cos_t, sin_t


def _group_avg():
    g = np.arange(RW) // RDH
    return jnp.asarray((g[:, None] == g[None, :]).astype(np.float32) / RDH)


def _head_expand():
    hd = np.arange(SSD_INNER) // SSD_HD
    rows = np.arange(2 * SSD_H)
    ex0 = (rows[:, None] == hd[None, :]).astype(np.float32)
    ex1 = (rows[:, None] == SSD_H + hd[None, :]).astype(np.float32)
    return jnp.asarray(ex0), jnp.asarray(ex1)


def kernel(x, norm_mix_pre, norm_mix_post, norm_ffn_pre, norm_ffn_post, ab_w_in, ab_ret_decay_logit, ab_ret_gn_g, ab_na_rpb, ab_w_out, c_w_in, c_conv_w, c_conv_b, c_dt_bias, c_a_log, c_d_skip, c_norm_g, c_w_out, ffn_w_up, ffn_conv_w, ffn_conv_b, ffn_w_down, loss_target, m_norm_mix_pre, m_norm_mix_post, m_norm_ffn_pre, m_norm_ffn_post, m_ab_w_in, m_ab_ret_decay_logit, m_ab_ret_gn_g, m_ab_na_rpb, m_ab_w_out, m_c_w_in, m_c_conv_w, m_c_conv_b, m_c_dt_bias, m_c_a_log, m_c_d_skip, m_c_norm_g, m_c_w_out, m_ffn_w_up, m_ffn_conv_w, m_ffn_conv_b, m_ffn_w_down, v_norm_mix_pre, v_norm_mix_post, v_norm_ffn_pre, v_norm_ffn_post, v_ab_w_in, v_ab_ret_decay_logit, v_ab_ret_gn_g, v_ab_na_rpb, v_ab_w_out, v_c_w_in, v_c_conv_w, v_c_conv_b, v_c_dt_bias, v_c_a_log, v_c_d_skip, v_c_norm_g, v_c_w_out, v_ffn_w_up, v_ffn_conv_w, v_ffn_conv_b, v_ffn_w_down):
    W = dict(norm_mix_pre=norm_mix_pre, norm_mix_post=norm_mix_post, norm_ffn_pre=norm_ffn_pre, norm_ffn_post=norm_ffn_post, ab_w_in=ab_w_in, ab_ret_decay_logit=ab_ret_decay_logit, ab_ret_gn_g=ab_ret_gn_g, ab_na_rpb=ab_na_rpb, ab_w_out=ab_w_out, c_w_in=c_w_in, c_conv_w=c_conv_w, c_conv_b=c_conv_b, c_dt_bias=c_dt_bias, c_a_log=c_a_log, c_d_skip=c_d_skip, c_norm_g=c_norm_g, c_w_out=c_w_out, ffn_w_up=ffn_w_up, ffn_conv_w=ffn_conv_w, ffn_conv_b=ffn_conv_b, ffn_w_down=ffn_w_down)
    Mo = dict(norm_mix_pre=m_norm_mix_pre, norm_mix_post=m_norm_mix_post, norm_ffn_pre=m_norm_ffn_pre, norm_ffn_post=m_norm_ffn_post, ab_w_in=m_ab_w_in, ab_ret_decay_logit=m_ab_ret_decay_logit, ab_ret_gn_g=m_ab_ret_gn_g, ab_na_rpb=m_ab_na_rpb, ab_w_out=m_ab_w_out, c_w_in=m_c_w_in, c_conv_w=m_c_conv_w, c_conv_b=m_c_conv_b, c_dt_bias=m_c_dt_bias, c_a_log=m_c_a_log, c_d_skip=m_c_d_skip, c_norm_g=m_c_norm_g, c_w_out=m_c_w_out, ffn_w_up=m_ffn_w_up, ffn_conv_w=m_ffn_conv_w, ffn_conv_b=m_ffn_conv_b, ffn_w_down=m_ffn_w_down)
    Vo = dict(norm_mix_pre=v_norm_mix_pre, norm_mix_post=v_norm_mix_post, norm_ffn_pre=v_norm_ffn_pre, norm_ffn_post=v_norm_ffn_post, ab_w_in=v_ab_w_in, ab_ret_decay_logit=v_ab_ret_decay_logit, ab_ret_gn_g=v_ab_ret_gn_g, ab_na_rpb=v_ab_na_rpb, ab_w_out=v_ab_w_out, c_w_in=v_c_w_in, c_conv_w=v_c_conv_w, c_conv_b=v_c_conv_b, c_dt_bias=v_c_dt_bias, c_a_log=v_c_a_log, c_d_skip=v_c_d_skip, c_norm_g=v_c_norm_g, c_w_out=v_c_w_out, ffn_w_up=v_ffn_w_up, ffn_conv_w=v_ffn_conv_w, ffn_conv_b=v_ffn_conv_b, ffn_w_down=v_ffn_w_down)
    return _train_step(x[0], loss_target[0], W, Mo, Vo)


def _train_step(x, tgt, W, Mo, Vo):
    T = x.shape[0]
    rows = T // GRID_W

    col = lambda d, n, dt: d[n].reshape(-1, d[n].shape[-1]).astype(dt)
    rows_of = lambda d, dt: jnp.concatenate([col(d, n, dt) for n in ROW_SHARDED], axis=0)
    small = _pack([W[n] for n, _ in SHARDED[N_BIG:]], f32, 8)
    gat = _all_gather([col(W, n, bf16) for n in COL_SHARDED] + [rows_of(W, bf16), small], name="gather_weights")
    per_layer = lambda m: m.reshape(-1, D, m.shape[-1])
    w_ab_in = per_layer(_cols_from_slots(gat[0], "ab_w_in", name="cols_ab_w_in")[0])
    w_zx, w_dt = [per_layer(m) for m in _cols_from_slots(gat[1], "c_w_in", name="cols_c_w_in")]
    w_up = per_layer(_cols_from_slots(gat[2], "ffn_w_up", name="cols_ffn_w_up")[0])
    full, off = {}, 0
    for n in ROW_SHARDED:
        L, r = W[n].shape[0], W[n].shape[1]
        full[n] = jnp.swapaxes(gat[3][:, off:off + L * r].reshape(NDEV, L, r, D), 0, 1).reshape(L, NDEV * r, D)
        off += L * r
    gs = _unpack(gat[4], [W[n].shape for n, _ in SHARDED[N_BIG:]], (NDEV,))
    full.update({n: _from_slots(g, ax) for (n, ax), g in zip(SHARDED[N_BIG:], gs)})
    w_ab_out, w_c_out, w_down = full["ab_w_out"], full["c_w_out"], full["ffn_w_down"]
    c_cw8 = [_pad8(full["c_conv_w"][i]) for i in range(2)]
    c_cb = [full["c_conv_b"][i][None] for i in range(2)]
    c_ng = [full["c_norm_g"][i][None] for i in range(2)]
    f_cw8 = [_pad8(_ffn_perm(full["ffn_conv_w"][l])) for l in range(DEPTH)]
    f_cb = [_ffn_perm(W["ffn_conv_b"][l])[None] for l in range(DEPTH)]

    g1 = [W["norm_mix_pre"][l][None] for l in range(DEPTH)]
    g2 = [W["norm_mix_post"][l][None] for l in range(DEPTH)]
    g3 = [W["norm_ffn_pre"][l][None] for l in range(DEPTH)]
    g4 = [W["norm_ffn_post"][l][None] for l in range(DEPTH)]
    cos_t, sin_t = _rope_tables(T)
    gavg = _group_avg()
    ex0, ex1 = _head_expand()

    def log_gamma(logit):
        return -jax.nn.softplus(-logit)

    def ret_decays(lg):
        return [jnp.broadcast_to(lg[d][None, None, :], (1, T, RH)) for d in range(2)]

    saved = []
    xs_ = x
    hn = _rowwise("norm_first", _f_first, [(x, D, 0)], [], [(g1[0], D, 0)], [], [(D, bf16)], tm=256)[0]
    for l in range(DEPTH):
        i = l // 2
        sv = dict(x=xs_, hn=hn)
        if l % 2 == 0:
            proj = _mm_nn(hn, w_ab_in[i], name=f"ab_in_{l}")
            qr, kr = _rowwise(f"ret_prep_{l}", _f_rprep, [(proj, RW, 0), (proj, RW, 1)], [(cos_t, RW, 0), (sin_t, RW, 0)], [], [],
                              [(RW, f32), (RW, f32)], tm=256)
            lg, lg_vjp = jax.vjp(log_gamma, W["ab_ret_decay_logit"][i])
            a_f, a_b = ret_decays(lg)
            rscan = dict(G=RH, N=RDH, Hg=1, P=RDH, vcol=2)
            yf_t, hsf = _scan_fwd(qr, kr, proj, a_f, rev=False, name=f"ret_scan_f_{l}", **rscan)
            yb_t, hsb = _scan_fwd(qr, kr, proj, a_b, rev=True, name=f"ret_scan_b_{l}", **rscan)
            gn = W["ab_ret_gn_g"][i][None]
            ret = _rowwise(f"ret_post_{l}", _f_rpost, [(yf_t, RW, 0), (yb_t, RW, 0), (proj, RW, 3)], [], [(gn, RW, 0)], [gavg],
                           [(RW, bf16)], tm=256)[0]
            nqkv = proj[:, 4 * RW:].astype(bf16)
            ncols = dict(qcol=0, kcol=NAW // 128, vcol=2 * NAW // 128)
            r1, bias_vjp = jax.vjp(_na_col_bias, W["ab_na_rpb"][i])
            bias = _na_bias_build(r1, rows, name=f"na_bias_{l}")
            na_o, na_l = _na_fwd(nqkv, nqkv, nqkv, bias, name=f"na_fwd_{l}", **ncols)
            cat = jnp.concatenate([ret, na_o.astype(bf16)], axis=1)
            mo = _mm_nn(cat, w_ab_out[i], name=f"ab_out_{l}")
            sv.update(proj=proj, qr=qr, kr=kr, a_f=a_f, a_b=a_b, hsf=hsf, hsb=hsb, yf_t=yf_t, yb_t=yb_t, gn=gn, rscan=rscan,
                      nqkv=nqkv, ncols=ncols, bias=bias, bias_vjp=bias_vjp, lg_vjp=lg_vjp, na_o=na_o, na_l=na_l, cat=cat)
        else:
            zx = _mm_nn(hn, w_zx[i], name=f"c_in_{l}")
            dtr = _mm_nn(hn, w_dt[i], name=f"c_in_dt_{l}")
            xa = _conv(zx, c_cw8[i], c_cb[i], mode="silu", W=SSD_CONV, name=f"c_conv_{l}", C=SSD_XBC, xbase=SSD_INNER // 512)
            dtb, alog = W["c_dt_bias"][i].reshape(1, 2 * SSD_H), W["c_a_log"][i].reshape(1, 2 * SSD_H)
            vf, vb, la = _rowwise(f"ssd_prep_{l}", _f_sprep, [(xa, SSD_INNER, 0), (dtr, 2 * SSD_H, 0)], [],
                                  [(dtb, 2 * SSD_H, 0), (alog, 2 * SSD_H, 0)], [ex0, ex1],
                                  [(SSD_INNER, f32), (SSD_INNER, f32), (2 * SSD_H, f32)], tm=128)
            a_f = la[:, :SSD_H].reshape(T, SSD_G, SSD_HPG).transpose(1, 0, 2)
            a_b = la[:, SSD_H:].reshape(T, SSD_G, SSD_HPG).transpose(1, 0, 2)
            sscan = dict(G=SSD_G, N=SSD_N, Hg=SSD_HPG, P=SSD_HD, qcol=(SSD_INNER + SSD_G * SSD_N) // SSD_N, kcol=SSD_INNER // SSD_N)
            yf_t, hsf = _scan_fwd(xa, xa, vf, a_f, rev=False, name=f"ssd_scan_f_{l}", **sscan)
            yb_t, hsb = _scan_fwd(xa, xa, vb, a_b, rev=True, name=f"ssd_scan_b_{l}", **sscan)
            dsk = jnp.repeat(W["c_d_skip"][i], SSD_HD)[None]
            yo = _rowwise(f"ssd_post_{l}", _f_spost, [(yf_t, 512, 0), (yb_t, 512, 0), (xa, 512, 0), (zx, 512, 0)], [],
                          [(dsk, 512, 0), (c_ng[i], 512, 0)], [], [(512, bf16)], tm=256, J=SSD_G)[0]
            mo = _mm_nn(yo, w_c_out[i], name=f"c_out_{l}")
            sv.update(zx=zx, dtr=dtr, xa=xa, dtb=dtb, alog=alog, a_f=a_f, a_b=a_b, vf=vf, vb=vb, sscan=sscan,
                      hsf=hsf, hsb=hsb, yf_t=yf_t, yb_t=yb_t, dsk=dsk, yo=yo)
        x1, hf = _rowwise(f"norm_mid_{l}", _f_mid, [(xs_, D, 0), (mo, D, 0)], [], [(g2[l], D, 0), (g3[l], D, 0)], [],
                          [(D, f32), (D, bf16)], tm=256)
        pre = _mm_nn(hf, w_up[l], name=f"ffn_up_{l}")
        act = _conv(pre, f_cw8[l], f_cb[l], mode="geglu", W=FFN_CONV, name=f"ffn_conv_{l}", C=2 * FFN, tc=FFN_TC, out_dtype=bf16)
        fo = _mm_nn(act, w_down[l], name=f"ffn_down_{l}")
        sv.update(mo=mo, x1=x1, hf=hf, pre=pre, act=act, fo=fo)
        if l < DEPTH - 1:
            xs_, hn = _rowwise(f"norm_end_{l}", _f_end, [(x1, D, 0), (fo, D, 0)], [], [(g4[l], D, 0), (g1[l + 1], D, 0)], [],
                               [(D, f32), (D, bf16)], tm=256)
        else:
            xs_ = _rowwise(f"norm_end_{l}", _f_last, [(x1, D, 0), (fo, D, 0)], [], [(g4[l], D, 0)], [], [(D, f32)], tm=256)[0]
        saved.append(sv)

    dx, lpart = _loss_call(xs_, tgt)
    loss = lax.psum(lpart[0, 0], ("x", "y", "c"))

    G = {n: [None] * W[n].shape[0] for n in WEIGHTS}
    dhn = None
    for l in reversed(range(DEPTH)):
        i = l // 2
        sv = saved[l]
        if l == DEPTH - 1:
            (dx1, dfo), (dg4,) = _rowwise_bwd(f"norm_end_bwd_{l}", _f_last, [(sv["x1"], D, 0), (sv["fo"], D, 0)], [],
                                              [(g4[l], D, 0)], [], [(dx, D, 0)], [f32, bf16], tm=256)
        else:
            (dx1, dfo), (dg4, dg1n) = _rowwise_bwd(f"norm_end_bwd_{l}", _f_end, [(sv["x1"], D, 0), (sv["fo"], D, 0)], [],
                                                   [(g4[l], D, 0), (g1[l + 1], D, 0)], [], [(dx, D, 0), (dhn, D, 0)],
                                                   [f32, bf16], tm=256)
            G["norm_mix_pre"][l + 1] = dg1n[0]
        G["norm_ffn_post"][l] = dg4[0]
        dact = _mm_nt(dfo, w_down[l], name=f"ffn_down_dx_{l}")
        G["ffn_w_down"][l] = _mm_tn(sv["act"], dfo, name=f"ffn_down_dw_{l}")
        dpre, dfw, dfb = _conv_bwd(sv["pre"], f_cw8[l], f_cb[l], dact, mode="geglu", W=FFN_CONV, name=f"ffn_conv_bwd_{l}",
                                   C=2 * FFN, tc=FFN_TC)
        dhf = _mm_nt(dpre, w_up[l], name=f"ffn_up_dx_{l}")
        G["ffn_w_up"][l] = _cols_to_slots([_mm_tn(sv["hf"], dpre, name=f"ffn_up_dw_{l}")], "ffn_w_up", bf16, name=f"slots_ffn_up_{l}")
        G["ffn_conv_w"][l] = _ffn_unperm(dfw[:FFN_CONV])
        G["ffn_conv_b"][l] = _ffn_unperm(dfb[0])
        (dxl, dmo), (dg2, dg3) = _rowwise_bwd(f"norm_mid_bwd_{l}", _f_mid, [(sv["x"], D, 0), (sv["mo"], D, 0)], [],
                                              [(g2[l], D, 0), (g3[l], D, 0)], [], [(dx1, D, 0), (dhf, D, 0)], [f32, bf16], tm=256)
        G["norm_mix_post"][l] = dg2[0]
        G["norm_ffn_pre"][l] = dg3[0]
        if l % 2 == 0:
            dcat = _mm_nt(dmo, w_ab_out[i], name=f"ab_out_dx_{l}")
            G["ab_w_out"][i] = _mm_tn(sv["cat"], dmo, name=f"ab_out_dw_{l}")
            (dyf, _, drg), (dgn,) = _rowwise_bwd(
                f"ret_post_bwd_{l}", _f_rpost, [(sv["yf_t"], RW, 0), (sv["yb_t"], RW, 0), (sv["proj"], RW, 3)], [],
                [(sv["gn"], RW, 0)], [gavg], [(dcat, RW, 0)], [f32, f32, bf16], tm=256)
            G["ab_ret_gn_g"][i] = dgn[0]
            dqf, dkf, dvf, daf = _scan_bwd(sv["qr"], sv["kr"], sv["proj"], sv["a_f"], sv["hsf"], dyf, rev=False,
                                           name=f"ret_scan_f_bwd_{l}", **sv["rscan"])
            dqb, dkb, dvb, dab = _scan_bwd(sv["qr"], sv["kr"], sv["proj"], sv["a_b"], sv["hsb"], dyf, rev=True,
                                           name=f"ret_scan_b_bwd_{l}", **sv["rscan"])
            dq_t, dk_t, drv = dqf + dqb, dkf + dkb, (dvf + dvb).astype(bf16)
            (drq, drk), _ = _rowwise_bwd(f"ret_prep_bwd_{l}", _f_rprep, [(sv["proj"], RW, 0), (sv["proj"], RW, 1)],
                                         [(cos_t, RW, 0), (sin_t, RW, 0)], [], [], [(dq_t, RW, 0), (dk_t, RW, 0)], [bf16, bf16], tm=256)
            da_cols = jnp.concatenate([daf[0], dab[0]], axis=1)
            dlg = _colsum(da_cols, name=f"ret_decay_sum_{l}").reshape(2, RH)
            G["ab_ret_decay_logit"][i] = sv["lg_vjp"](dlg)[0]
            dnq, dnk, dnv, dbias = _na_bwd(sv["nqkv"], sv["nqkv"], sv["nqkv"], sv["bias"], sv["na_o"], sv["na_l"], dcat,
                                           docol=RW // 128, name=f"na_bwd_{l}", **sv["ncols"])
            G["ab_na_rpb"][i] = sv["bias_vjp"](_na_bias_fold(dbias, rows, name=f"na_bias_fold_{l}"))[0]
            dproj = jnp.concatenate([drq, drk, drv, drg] + [t.astype(bf16) for t in (dnq, dnk, dnv)], axis=1)
            dhn = _mm_nt(dproj, w_ab_in[i], name=f"ab_in_dx_{l}")
            G["ab_w_in"][i] = _cols_to_slots([_mm_tn(sv["hn"], dproj, name=f"ab_in_dw_{l}")], "ab_w_in", bf16, name=f"slots_ab_in_{l}")
        else:
            dyo = _mm_nt(dmo, w_c_out[i], name=f"c_out_dx_{l}")
            G["c_w_out"][i] = _mm_tn(sv["yo"], dmo, name=f"c_out_dw_{l}")
            (dyf, _, dxs1, dz), (ddsk, dng) = _rowwise_bwd(
                f"ssd_post_bwd_{l}", _f_spost, [(sv["yf_t"], 512, 0), (sv["yb_t"], 512, 0), (sv["xa"], 512, 0), (sv["zx"], 512, 0)],
                [], [(sv["dsk"], 512, 0), (c_ng[i], 512, 0)], [], [(dyo, 512, 0)], [f32, f32, f32, bf16], tm=256, J=SSD_G)
            G["c_d_skip"][i] = ddsk.reshape(SSD_H, SSD_HD).sum(axis=1)
            G["c_norm_g"][i] = dng[0]
            dqf, dkf, dvf, daf = _scan_bwd(sv["xa"], sv["xa"], sv["vf"], sv["a_f"], sv["hsf"], dyf, rev=False,
                                           name=f"ssd_scan_f_bwd_{l}", **sv["sscan"])
            dqb, dkb, dvb, dab = _scan_bwd(sv["xa"], sv["xa"], sv["vb"], sv["a_b"], sv["hsb"], dyf, rev=True,
                                           name=f"ssd_scan_b_bwd_{l}", **sv["sscan"])
            dla = jnp.concatenate([daf.transpose(1, 0, 2).reshape(T, SSD_H), dab.transpose(1, 0, 2).reshape(T, SSD_H)], axis=1)
            (dxs2, ddtr), (ddtb, dalog) = _rowwise_bwd(
                f"ssd_prep_bwd_{l}", _f_sprep, [(sv["xa"], SSD_INNER, 0), (sv["dtr"], 2 * SSD_H, 0)], [],
                [(sv["dtb"], 2 * SSD_H, 0), (sv["alog"], 2 * SSD_H, 0)], [ex0, ex1],
                [(dvf, SSD_INNER, 0), (dvb, SSD_INNER, 0), (dla, 2 * SSD_H, 0)], [f32, bf16], tm=128)
            G["c_dt_bias"][i] = ddtb.reshape(2, SSD_H)
            G["c_a_log"][i] = dalog.reshape(2, SSD_H)
            dxa = jnp.concatenate([dxs1 + dxs2, dkf + dkb, dqf + dqb], axis=1)
            dxbc, dcw, dcb = _conv_bwd(sv["zx"], c_cw8[i], c_cb[i], dxa, mode="silu", W=SSD_CONV, name=f"c_conv_bwd_{l}",
                                       C=SSD_XBC, xbase=SSD_INNER // 512)
            G["c_conv_w"][i] = dcw[:SSD_CONV]
            G["c_conv_b"][i] = dcb[0]
            dzx = jnp.concatenate([dz, dxbc], axis=1)
            t1 = _mm_nt(ddtr, w_dt[i], name=f"c_in_dt_dx_{l}")
            dhn = _mm_nt(dzx, w_zx[i], add=t1, name=f"c_in_dx_{l}")
            G["c_w_in"][i] = _cols_to_slots([_mm_tn(sv["hn"], dzx, name=f"c_in_dw_{l}"), _mm_tn(sv["hn"], ddtr, name=f"c_in_dt_dw_{l}")],
                                            "c_w_in", bf16, name=f"slots_c_in_{l}")
        dx = dxl
    (grad_x,), (dg1,) = _rowwise_bwd("norm_first_bwd", _f_first_bwd, [(x, D, 0)], [], [(g1[0], D, 0)], [], [(dx, D, 0), (dhn, D, 0)],
                                     [f32], tm=256)
    G["norm_mix_pre"][0] = dg1[0]

    small_names = [n for n, _ in SHARDED[N_BIG:]]
    col_slots = [jnp.concatenate(G[n], axis=1) for n in COL_SHARDED]
    row_slots = jnp.concatenate([g.reshape(NDEV, -1, D).astype(bf16) for n in ROW_SHARDED for g in G[n]], axis=1)
    small_slots = _pack_slots([_to_slots(jnp.stack(G[n]), ax) for n, ax in SHARDED[N_BIG:]], 8)
    ar = _pack([jnp.stack(G[n]) for n in REPLICATED], f32, 8)
    parts = [(a, True) for a in col_slots + [row_slots, small_slots]] + [(ar, False)]
    from_sib = _to_sibling(parts, name="grads_to_sibling")
    tiles = [256, 256, 256, 64, small_slots.shape[1], ar.shape[0]]
    chip = [_add_partials(a, b, per_slot=ps, tr=t, name=f"grads_add_{j}")
            for j, ((a, ps), b, t) in enumerate(zip(parts, from_sib, tiles))]
    exch = _to_chips([(a, ps) for a, (_, ps) in zip(chip, parts)], name="grads_to_chips")
    pk = lambda d, names: _pack([d[n] for n in names], f32, 8)
    upd = [_adamw(exch[j], col(W, n, f32), col(Mo, n, f32), col(Vo, n, f32), name=f"adamw_{n}", tr=256)
           for j, n in enumerate(COL_SHARDED)]
    upd_rows = _adamw(exch[3], rows_of(W, f32), rows_of(Mo, f32), rows_of(Vo, f32), name="adamw_rows", tr=64)
    upd_small = _adamw(exch[4], pk(W, small_names), pk(Mo, small_names), pk(Vo, small_names), name="adamw_small",
                       tr=small_slots.shape[1])
    upd_rep = _adamw(exch[5], pk(W, REPLICATED), pk(Mo, REPLICATED), pk(Vo, REPLICATED), name="adamw_replicated", tr=ar.shape[0])
    res = []
    for k in range(4):
        d = {n: upd[j][k].reshape(W[n].shape) for j, n in enumerate(COL_SHARDED)}
        off = 0
        for n in ROW_SHARDED:
            cnt = W[n].shape[0] * W[n].shape[1]
            d[n] = upd_rows[k][off:off + cnt].reshape(W[n].shape)
            off += cnt
        d.update(zip(small_names, _unpack(upd_small[k], [W[n].shape for n in small_names])))
        d.update(zip(REPLICATED, _unpack(upd_rep[k], [W[n].shape for n in REPLICATED])))
        res.append(d)
    outs = [loss, grad_x[None]]
    for k in range(4):
        outs += [res[k][n] for n in WEIGHTS]
    return tuple(outs)


def _pack_slots(slot_arrays, row_mult):
    flat = jnp.concatenate([a.reshape(NDEV, -1) for a in slot_arrays], axis=1)
    rows = -(-flat.shape[1] // LANES)
    rows = -(-rows // row_mult) * row_mult
    return jnp.pad(flat, ((0, 0), (0, rows * LANES - flat.shape[1]))).reshape(NDEV, rows, LANES)
```

```python
import functools
import numpy as np
import jax
import jax.numpy as jnp
from jax import lax
from jax.experimental import pallas as pl
from jax.experimental.pallas import tpu as pltpu

f32, bf16 = jnp.float32, jnp.bfloat16
S = jax.ShapeDtypeStruct
HI = lax.Precision.HIGHEST

D = 1024
DEPTH = 4
GRID_W = 64
CHUNK = 128
EPS = 1e-6
RH, RDH, RW = 8, 64, 512
NAH, NADH, NAW = 8, 64, 512
NA_WR, NA_WC = 8, 16
NA_QROWS = 8
NA_KROWS = 16
NA_PAIR = 2
SSD_INNER, SSD_HD, SSD_H, SSD_G, SSD_HPG, SSD_N, SSD_CONV = 2048, 64, 32, 4, 8, 128, 5
SSD_XBC = SSD_INNER + 2 * SSD_G * SSD_N
FFN, FFN_CONV = 2816, 3
FFN_TC = 512
SCAN_HEADS_PER_STEP = 8
ROPE_BASE = 10000.0
LR, B1, B2, AEPS, WD, STEP = 0.001, 0.9, 0.999, 1e-08, 0.01, 10
NDEV = 8
LANES = 128
VMEM_LIMIT = 56 * 1024 * 1024
MM_BLOCK_BYTES = 6 * 1024 * 1024

NT = (((1,), (1,)), ((), ()))
TN = (((0,), (0,)), ((), ()))

SHARDED = [("ab_w_in", 2), ("ab_w_out", 1), ("c_w_in", 2), ("c_w_out", 1), ("ffn_w_up", 2), ("ffn_w_down", 1),
           ("c_conv_w", 2), ("c_conv_b", 1), ("c_norm_g", 1), ("ffn_conv_w", 2)]
N_BIG = 6
COL_SHARDED = ["ab_w_in", "c_w_in", "ffn_w_up"]
ROW_SHARDED = ["ab_w_out", "c_w_out", "ffn_w_down"]
REPLICATED = ["norm_mix_pre", "norm_mix_post", "norm_ffn_pre", "norm_ffn_post", "ab_ret_decay_logit", "ab_ret_gn_g",
              "ab_na_rpb", "c_dt_bias", "c_a_log", "c_d_skip", "ffn_conv_b"]
WEIGHTS = ["norm_mix_pre", "norm_mix_post", "norm_ffn_pre", "norm_ffn_post", "ab_w_in", "ab_ret_decay_logit",
           "ab_ret_gn_g", "ab_na_rpb", "ab_w_out", "c_w_in", "c_conv_w", "c_conv_b", "c_dt_bias", "c_a_log", "c_d_skip",
           "c_norm_g", "c_w_out", "ffn_w_up", "ffn_conv_w", "ffn_conv_b", "ffn_w_down"]


def _params(sem=None):
    return pltpu.CompilerParams(dimension_semantics=sem, vmem_limit_bytes=VMEM_LIMIT)


def _mm_nn(a, w, *, name, tm=1024, tn=512, out_dtype=f32):
    M, K = a.shape
    N = w.shape[1]
    tn = min(tn, N)

    def body(a_ref, w_ref, o_ref):
        o_ref[...] = jnp.dot(a_ref[...], w_ref[...], preferred_element_type=f32).astype(o_ref.dtype)

    return pl.pallas_call(
        body, name=name, grid=(M // tm, N // tn),
        in_specs=[pl.BlockSpec((tm, K), lambda i, j: (i, 0)), pl.BlockSpec((K, tn), lambda i, j: (0, j))],
        out_specs=pl.BlockSpec((tm, tn), lambda i, j: (i, j)),
        out_shape=S((M, N), out_dtype), compiler_params=_params(("parallel", "parallel")))(a, w)


def _mm_nt(dy, w, *, name, add=None, tm=512):
    M, N = dy.shape
    K = w.shape[0]
    tk = next((t for t in (1024, 1408, 512, 256, 128) if K % t == 0 and (t <= 512 or t * N * 2 <= MM_BLOCK_BYTES)), K)

    def body(*refs):
        if add is None:
            d_ref, w_ref, o_ref = refs
            o_ref[...] = lax.dot_general(d_ref[...], w_ref[...], NT, preferred_element_type=f32)
        else:
            d_ref, w_ref, a_ref, o_ref = refs
            o_ref[...] = lax.dot_general(d_ref[...], w_ref[...], NT, preferred_element_type=f32) + a_ref[...]

    in_specs = [pl.BlockSpec((tm, N), lambda i, j: (i, 0)), pl.BlockSpec((tk, N), lambda i, j: (j, 0))]
    args = [dy, w]
    if add is not None:
        in_specs.append(pl.BlockSpec((tm, tk), lambda i, j: (i, j)))
        args.append(add)
    return pl.pallas_call(
        body, name=name, grid=(M // tm, K // tk), in_specs=in_specs,
        out_specs=pl.BlockSpec((tm, tk), lambda i, j: (i, j)),
        out_shape=S((M, K), f32), compiler_params=_params(("parallel", "parallel")))(*args)


def _mm_tn(a, dy, *, name, tt=1024):
    M, K = a.shape
    N = dy.shape[1]
    tk = K if K <= 1024 else (1024 if K % 1024 == 0 else K // 2)
    tn = min(512, N)
    tt = min(tt, M)

    def body(a_ref, d_ref, o_ref):
        t = pl.program_id(2)
        part = lax.dot_general(a_ref[...], d_ref[...], TN, preferred_element_type=f32)

        @pl.when(t == 0)
        def _():
            o_ref[...] = part

        @pl.when(t > 0)
        def _():
            o_ref[...] += part

    return pl.pallas_call(
        body, name=name, grid=(K // tk, N // tn, M // tt),
        in_specs=[pl.BlockSpec((tt, tk), lambda k, n, t: (t, k)), pl.BlockSpec((tt, tn), lambda k, n, t: (t, n))],
        out_specs=pl.BlockSpec((tk, tn), lambda k, n, t: (k, n)),
        out_shape=S((K, N), f32), compiler_params=_params(("parallel", "parallel", "arbitrary")))(a, dy)


def _tile_spec(tm, width, base):
    return pl.BlockSpec((tm, width), lambda j, i: (i, base + j))


def _par_spec(width, base):
    return pl.BlockSpec((1, width), lambda j, i: (0, base + j))


def _full_spec(a):
    nd = a.ndim
    return pl.BlockSpec(a.shape, lambda j, i: (0,) * nd)


def _rowwise(name, f, tiles, ctiles, params, consts, outs, *, tm, J=1):
    T = tiles[0][0].shape[0]
    nt, nct, npar, nc = len(tiles), len(ctiles), len(params), len(consts)

    def body(*refs):
        tv = [r[...].astype(f32) for r in refs[:nt + nct]]
        pv = [r[...] for r in refs[nt + nct:nt + nct + npar + nc]]
        res = f(*tv, *pv)
        for o, v in zip(refs[nt + nct + npar + nc:], res):
            o[...] = v.astype(o.dtype)

    in_specs = ([_tile_spec(tm, w, b) for _, w, b in tiles + ctiles] + [_par_spec(w, b) for _, w, b in params]
                + [_full_spec(c) for c in consts])
    return pl.pallas_call(
        body, name=name, grid=(J, T // tm), in_specs=in_specs,
        out_specs=[_tile_spec(tm, w, 0) for w, _ in outs],
        out_shape=[S((T, J * w), dt) for w, dt in outs],
        compiler_params=_params(("parallel", "parallel")))(
            *[a for a, _, _ in tiles + ctiles], *[a for a, _, _ in params], *consts)


def _rowwise_bwd(name, f, tiles, ctiles, params, consts, douts, dtile_dtypes, *, tm, J=1):
    T = tiles[0][0].shape[0]
    nt, nct, npar, nc, nd = len(tiles), len(ctiles), len(params), len(consts), len(douts)

    def body(*refs):
        i = pl.program_id(1)
        k = 0
        tv = [r[...].astype(f32) for r in refs[k:k + nt]]; k += nt
        cv = [r[...].astype(f32) for r in refs[k:k + nct]]; k += nct
        pv = [r[...] for r in refs[k:k + npar]]; k += npar
        kv = [r[...] for r in refs[k:k + nc]]; k += nc
        dv = [r[...].astype(f32) for r in refs[k:k + nd]]; k += nd
        dt_refs = refs[k:k + nt]; k += nt
        dp_refs = refs[k:k + npar]
        _, vjp = jax.vjp(lambda tv_, pv_: tuple(f(*tv_, *cv, *pv_, *kv)), tv, pv)
        dts, dps = vjp(tuple(dv))
        for r, g in zip(dt_refs, dts):
            r[...] = g.astype(r.dtype)
        for r, g in zip(dp_refs, dps):
            @pl.when(i == 0)
            def _(r=r, g=g):
                r[...] = g

            @pl.when(i > 0)
            def _(r=r, g=g):
                r[...] += g

    in_specs = ([_tile_spec(tm, w, b) for _, w, b in tiles + ctiles] + [_par_spec(w, b) for _, w, b in params]
                + [_full_spec(c) for c in consts] + [_tile_spec(tm, w, b) for _, w, b in douts])
    res = pl.pallas_call(
        body, name=name, grid=(J, T // tm), in_specs=in_specs,
        out_specs=[_tile_spec(tm, w, 0) for _, w, _ in tiles] + [_par_spec(w, b) for _, w, b in params],
        out_shape=[S((T, J * w), dt) for (_, w, _), dt in zip(tiles, dtile_dtypes)] + [S(a.shape, f32) for a, _, _ in params],
        compiler_params=_params(("parallel", "arbitrary")))(
            *[a for a, _, _ in tiles + ctiles], *[a for a, _, _ in params], *consts, *[a for a, _, _ in douts])
    return res[:nt], res[nt:]


def _rms(x, g):
    return x * lax.rsqrt(jnp.mean(x * x, axis=-1, keepdims=True) + EPS) * g


def _f_first(x, g1):
    return (_rms(x, g1),)


def _f_first_bwd(x, g1):
    return (x, _rms(x, g1))


def _f_mid(x, m, g2, g3):
    x1 = x + _rms(m, g2)
    return (x1, _rms(x1, g3))


def _f_end(x1, fo, g4, g1n):
    x2 = x1 + _rms(fo, g4)
    return (x2, _rms(x2, g1n))


def _f_last(x1, fo, g4):
    return (x1 + _rms(fo, g4),)


@jax.custom_vjp
def _swap_halves(x):
    c = x.shape[1]
    lane = lax.broadcasted_iota(jnp.int32, x.shape, 1) % RDH
    return jnp.where(lane < RDH // 2, pltpu.roll(x, c - RDH // 2, axis=1), pltpu.roll(x, RDH // 2, axis=1))


_swap_halves.defvjp(lambda x: (_swap_halves(x), None), lambda _, g: (_swap_halves(g),))


def _f_rprep(rq, rk, cos, sin):
    rot = lambda t: t * cos + _swap_halves(t) * sin
    return (rot(rq), rot(rk) * (RDH ** -0.5))


def _f_rpost(y, rg, gn, gavg):
    mu = jnp.dot(y, gavg, precision=HI, preferred_element_type=f32)
    yc = y - mu
    var = jnp.dot(yc * yc, gavg, precision=HI, preferred_element_type=f32)
    return (jax.nn.silu(rg) * (yc * lax.rsqrt(var + EPS) * gn),)


def _f_sprep(xs, dtr, dtb, alog, ex0, ex1):
    dt = jax.nn.softplus(dtr + dtb)
    la = dt * (-jnp.exp(alog))
    e0 = jnp.dot(dt, ex0, precision=HI, preferred_element_type=f32)
    e1 = jnp.dot(dt, ex1, precision=HI, preferred_element_type=f32)
    return (xs * e0, xs * e1, la)


def _f_sprep_bwd(xs, dtr, dtb, alog, ex0, ex1):
    return _f_sprep(xs, dtr, dtb, alog, ex0, ex1) + (xs,)


def _f_spost(y, xs, z, dsk, ng):
    y = (y + xs * dsk) * jax.nn.silu(z)
    y = y * lax.rsqrt(jnp.mean(y * y, axis=-1, keepdims=True) + EPS)
    return (y * ng,)


def _loss_call(y, tgt, *, tm=256):
    T = y.shape[0]

    def body(y_ref, t_ref, dy_ref, l_ref):
        i = pl.program_id(0)
        e = y_ref[...] - t_ref[...]
        dy_ref[...] = e * (1.0 / D)
        part = jnp.zeros((8, LANES), f32) + 0.5 * jnp.sum(jnp.mean(e * e, axis=-1, keepdims=True))

        @pl.when(i == 0)
        def _():
            l_ref[...] = part

        @pl.when(i > 0)
        def _():
            l_ref[...] += part

    return pl.pallas_call(
        body, name="loss_head", grid=(T // tm,),
        in_specs=[pl.BlockSpec((tm, D), lambda i: (i, 0))] * 2,
        out_specs=[pl.BlockSpec((tm, D), lambda i: (i, 0)), pl.BlockSpec((8, LANES), lambda i: (0, 0))],
        out_shape=[S((T, D), f32), S((8, LANES), f32)], compiler_params=_params(("arbitrary",)))(y, tgt)


def _colsum(x, *, name, tm=512):
    T, C = x.shape

    def body(x_ref, o_ref):
        i = pl.program_id(0)
        part = jnp.sum(x_ref[...], axis=0, keepdims=True)

        @pl.when(i == 0)
        def _():
            o_ref[...] = part

        @pl.when(i > 0)
        def _():
            o_ref[...] += part

    return pl.pallas_call(
        body, name=name, grid=(T // tm,), in_specs=[pl.BlockSpec((tm, C), lambda i: (i, 0))],
        out_specs=pl.BlockSpec((1, C), lambda i: (0, 0)), out_shape=S((1, C), f32),
        compiler_params=_params(("arbitrary",)))(x)


def _nn(a, b):
    if a.ndim == 3:
        return lax.dot_general(a, b, (((2,), (1,)), ((0,), (0,))), preferred_element_type=f32)
    return jnp.dot(a, b, preferred_element_type=f32)


def _nt(a, b):
    if a.ndim == 3:
        return lax.dot_general(a, b, (((2,), (2,)), ((0,), (0,))), preferred_element_type=f32)
    return lax.dot_general(a, b, NT, preferred_element_type=f32)


def _lift(x, like):
    return jnp.broadcast_to(x[None], like.shape[:1] + x.shape) if x.ndim < like.ndim else x


def _drop(g, like):
    return jnp.sum(g, axis=0) if like.ndim < g.ndim else g


@jax.custom_vjp
def _mm_lt(a, a_t, b):
    return _nn(_lift(a_t, b), b)


_mm_lt.defvjp(lambda a, a_t, b: (_nn(_lift(a_t, b), b), (a, b)),
              lambda res, g: (jnp.zeros_like(res[0]), _drop(_nt(g, res[1]), res[0]), _nn(_lift(res[0], g), g)))


@jax.custom_vjp
def _mm_rt(a, a_t, b):
    return _nn(_lift(a, b), b)


_mm_rt.defvjp(lambda a, a_t, b: (_nn(_lift(a, b), b), (a_t, b)),
              lambda res, g: (_drop(_nt(g, res[1]), res[0]), jnp.zeros_like(res[0]), _nn(_lift(res[0], g), g)))


@jax.custom_vjp
def _masked_mm(s, s_t, d, d_t, v):
    return _nn(s * d, v)


def _masked_mm_bwd(res, g):
    s, s_t, d, d_t, v = res
    da = _nt(g, v)
    return (_drop(da * d, s), jnp.zeros_like(s_t), da * s, jnp.zeros_like(d_t), _nn(s_t * d_t, g))


_masked_mm.defvjp(lambda s, s_t, d, d_t, v: (_nn(s * d, v), (s, s_t, d, d_t, v)), _masked_mm_bwd)


def _t(x):
    return jnp.swapaxes(x, -1, -2)


def _scan_step_heads(h, q, k, v, a, rev, for_vjp=False):
    B, L, P = v.shape
    ii = lax.broadcasted_iota(jnp.int32, (L, L), 0)
    jj = lax.broadcasted_iota(jnp.int32, (L, L), 1)
    if rev:
        tri, tri_t, dmask, dmask_t = (jj >= ii), (ii >= jj), (jj > ii), (ii > jj)
    else:
        tri, tri_t, dmask, dmask_t = (jj <= ii), (ii <= jj), (jj <= ii), (ii <= jj)
    cs = jnp.dot(tri.astype(f32), a, precision=HI, preferred_element_type=f32)
    cs_t = lax.dot_general(a, tri_t.astype(f32), TN, precision=HI, preferred_element_type=f32)
    tot = jnp.sum(a, axis=0, keepdims=True)
    c_col = jnp.stack([jnp.broadcast_to(cs[:, b:b + 1], (L, L)) for b in range(B)])
    c_row = jnp.stack([cs_t[b:b + 1, :] for b in range(B)])
    t_all = jnp.stack([tot[:, b:b + 1] for b in range(B)])
    dec = jnp.exp(jnp.where(dmask[None], c_col - c_row, -1e30))
    e_in, e_out = jnp.exp(c_col)[:, :, :P], jnp.exp(t_all - c_col)[:, :, :P]
    qk = _nt(q, k)
    k_t = _t(k)
    w = v * e_out
    if for_vjp:
        q_t = lax.stop_gradient(_t(q))
        qk_t = lax.stop_gradient(_nt(k, q))
        dec_t = lax.stop_gradient(jnp.exp(jnp.where(dmask_t[None], c_row - c_col, -1e30)))
        y = _masked_mm(qk, qk_t, dec, dec_t, v) + _mm_rt(q, q_t, h) * e_in
        hn = h * jnp.exp(t_all) + _mm_lt(lax.stop_gradient(k), k_t, w)
    else:
        y = _nn(qk * dec, v) + _nn(_lift(q, h), h) * e_in
        hn = h * jnp.exp(t_all) + _nn(_lift(k_t, w), w)
    return hn, y


def _scan_specs(gb, N, Hg, P, Ha, cm, qcol, kcol, vcol):
    qs = lambda col: pl.BlockSpec((CHUNK, gb * N), lambda g, c: (cm(c), col + g))
    vs = lambda col: pl.BlockSpec((CHUNK, gb * Hg * P), lambda g, c: (cm(c), col + g))
    as_ = pl.BlockSpec((1, CHUNK, Ha), lambda g, c: (g, cm(c), 0))
    hs = pl.BlockSpec((gb, 1, Hg, N, P), lambda g, c: (g, cm(c), 0, 0, 0))
    return qs(qcol), qs(kcol), vs(vcol), qs(0), vs(0), as_, hs


def _lanes(ref, n, width):
    return jnp.stack([ref[:, j * width:(j + 1) * width] for j in range(n)])


def _scan_fwd(q, k, v, a, *, G, N, Hg, P, qcol=0, kcol=0, vcol=0, rev, name, add_y=None):
    T, Ha, NC = q.shape[0], a.shape[2], q.shape[0] // CHUNK
    gb = SCAN_HEADS_PER_STEP // Hg
    cm = (lambda c: NC - 1 - c) if rev else (lambda c: c)
    qs, ks, vs, _, ys, as_, hs = _scan_specs(gb, N, Hg, P, Ha, cm, qcol, kcol, vcol)
    extra = [] if add_y is None else [add_y]

    def body(q_ref, k_ref, v_ref, a_ref, *rest):
        y_ref, hs_ref, h_scr = rest[len(extra):]

        @pl.when(pl.program_id(1) == 0)
        def _():
            h_scr[...] = jnp.zeros_like(h_scr)

        if Hg == 1:
            h = h_scr[:, 0]
            hs_ref[:, 0, 0] = h
            hn, y = _scan_step_heads(h, _lanes(q_ref, gb, N), _lanes(k_ref, gb, N), _lanes(v_ref, gb, P), a_ref[0], rev)
            h_scr[:, 0] = hn
        else:
            h = h_scr[0]
            hs_ref[0, 0] = h
            hn, y = _scan_step_heads(h, q_ref[...], k_ref[...], _lanes(v_ref, Hg, P), a_ref[0], rev)
            h_scr[0] = hn
        for j in range(gb * Hg):
            cols = slice(j * P, (j + 1) * P)
            y_ref[:, cols] = y[j] if add_y is None else y[j] + rest[0][:, cols]

    return pl.pallas_call(
        body, name=name, grid=(G // gb, NC), in_specs=[qs, ks, vs, as_] + [ys] * len(extra), out_specs=[ys, hs],
        out_shape=[S((T, G * Hg * P), f32), S((G, NC, Hg, N, P), f32)],
        scratch_shapes=[pltpu.VMEM((gb, Hg, N, P), f32)],
        compiler_params=_params(("parallel", "arbitrary")))(q, k, v, a, *extra)


def _scan_bwd(q, k, v, a, hsave, dy, *, G, N, Hg, P, qcol=0, kcol=0, vcol=0, rev, name, add_to=(None, None, None)):
    T, Ha, NC = q.shape[0], a.shape[2], q.shape[0] // CHUNK
    gb = SCAN_HEADS_PER_STEP // Hg
    cm = (lambda c: c) if rev else (lambda c: NC - 1 - c)
    qs, ks, vs, dqs, dvs, as_, hs = _scan_specs(gb, N, Hg, P, Ha, cm, qcol, kcol, vcol)
    extra = [(x, s) for x, s in zip(add_to, (dqs, dqs, dvs)) if x is not None]

    def body(q_ref, k_ref, v_ref, a_ref, hs_ref, dy_ref, *rest):
        dq_ref, dk_ref, dv_ref, da_ref, dh_scr = rest[len(extra):]
        prev = iter(rest[:len(extra)])
        pq, pk, pv = [next(prev) if x is not None else None for x in add_to]

        @pl.when(pl.program_id(1) == 0)
        def _():
            dh_scr[...] = jnp.zeros_like(dh_scr)

        if Hg == 1:
            _, vjp = jax.vjp(functools.partial(_scan_step_heads, rev=rev, for_vjp=True), hs_ref[:, 0, 0], _lanes(q_ref, gb, N),
                             _lanes(k_ref, gb, N), _lanes(v_ref, gb, P), a_ref[0])
            dh, dq, dk, dv, da = vjp((dh_scr[:, 0], _lanes(dy_ref, gb, P)))
            dh_scr[:, 0] = dh
            for j in range(gb):
                cols = slice(j * N, (j + 1) * N)
                dq_ref[:, cols] = dq[j] if pq is None else dq[j] + pq[:, cols]
                dk_ref[:, cols] = dk[j] if pk is None else dk[j] + pk[:, cols]
        else:
            _, vjp = jax.vjp(functools.partial(_scan_step_heads, rev=rev, for_vjp=True), hs_ref[0, 0], q_ref[...], k_ref[...],
                             _lanes(v_ref, Hg, P), a_ref[0])
            dh, dq, dk, dv, da = vjp((dh_scr[0], _lanes(dy_ref, Hg, P)))
            dh_scr[0] = dh
            dq_ref[...] = dq if pq is None else dq + pq[...]
            dk_ref[...] = dk if pk is None else dk + pk[...]
        for j in range(gb * Hg):
            cols = slice(j * P, (j + 1) * P)
            dv_ref[:, cols] = dv[j] if pv is None else dv[j] + pv[:, cols]
        da_ref[0] = da

    return pl.pallas_call(
        body, name=name, grid=(G // gb, NC), in_specs=[qs, ks, vs, as_, hs, dvs] + [s for _, s in extra],
        out_specs=[dqs, dqs, dvs, as_],
        out_shape=[S((T, G * N), f32), S((T, G * N), f32), S((T, G * Hg * P), f32), S(a.shape, f32)],
        scratch_shapes=[pltpu.VMEM((gb, Hg, N, P), f32)],
        compiler_params=_params(("parallel", "arbitrary")))(q, k, v, a, hsave, dy, *[x for x, _ in extra])


def _na_block_case(rb, nrb):
    return jnp.where(rb == 0, 0, jnp.where(rb == nrb - 1, 2, 1))


def _na_key_start(rb, rows):
    return pl.multiple_of(jnp.clip(rb * NA_QROWS - NA_WR // 2, 0, rows - NA_KROWS) * GRID_W, 256)


def _na_specs(T, nrb):
    nq, nk, wb = NA_QROWS * GRID_W, NA_KROWS * GRID_W, NA_PAIR * NADH
    qs = lambda col: pl.BlockSpec((nq, wb), lambda p, r: (r, col + p))
    fs = lambda col: pl.BlockSpec((T, wb), lambda p, r: (0, col + p))
    bs = pl.BlockSpec((NA_PAIR, 1, nq, nk), lambda p, r: (p, _na_block_case(r, nrb), 0, 0))
    ls = pl.BlockSpec((1, nq, NA_PAIR), lambda p, r: (p, r, 0))
    return qs, fs, bs, ls


def _na_fwd(q, k, v, bias, *, qcol, kcol, vcol, name):
    T = q.shape[0]
    rows = T // GRID_W
    nq, nk = NA_QROWS * GRID_W, NA_KROWS * GRID_W
    nrb = T // nq
    scale = NADH ** -0.5
    qs, fs, bs, ls = _na_specs(T, nrb)

    def body(q_ref, k_ref, v_ref, b_ref, o_ref, l_ref):
        ks = _na_key_start(pl.program_id(1), rows)
        for hh in range(NA_PAIR):
            sl = slice(hh * NADH, (hh + 1) * NADH)
            kw = k_ref[pl.ds(ks, nk), sl]
            vw = v_ref[pl.ds(ks, nk), sl]
            s = lax.dot_general(q_ref[:, sl], kw, NT, preferred_element_type=f32) * scale + b_ref[hh, 0]
            m = jnp.max(s, axis=1, keepdims=True)
            p = jnp.exp(s - m)
            l = jnp.sum(p, axis=1, keepdims=True)
            o_ref[:, sl] = jnp.dot(p.astype(bf16), vw, preferred_element_type=f32) / l
            l_ref[0, :, hh:hh + 1] = m + jnp.log(l)

    return pl.pallas_call(
        body, name=name, grid=(NAH // NA_PAIR, nrb), in_specs=[qs(qcol), fs(kcol), fs(vcol), bs],
        out_specs=[qs(0), ls], out_shape=[S((T, NAW), f32), S((NAH // NA_PAIR, T, NA_PAIR), f32)],
        compiler_params=_params(("parallel", "arbitrary")))(q, k, v, bias)


def _na_bwd(q, k, v, bias, o, lse, do, *, qcol, kcol, vcol, docol, name):
    T = q.shape[0]
    rows = T // GRID_W
    nq, nk = NA_QROWS * GRID_W, NA_KROWS * GRID_W
    nrb = T // nq
    scale = NADH ** -0.5
    qs, fs, bs, ls = _na_specs(T, nrb)

    def body(q_ref, k_ref, v_ref, b_ref, o_ref, l_ref, do_ref, dq_ref, dk_ref, dv_ref, db_ref):
        rb = pl.program_id(1)

        @pl.when(rb == 0)
        def _():
            dk_ref[...] = jnp.zeros_like(dk_ref)
            dv_ref[...] = jnp.zeros_like(dv_ref)

        ks = _na_key_start(rb, rows)
        first = (rb == 0) | (rb == 1) | (rb == nrb - 1)
        for hh in range(NA_PAIR):
            sl = slice(hh * NADH, (hh + 1) * NADH)
            qv = q_ref[:, sl]
            kw = k_ref[pl.ds(ks, nk), sl]
            vw = v_ref[pl.ds(ks, nk), sl]
            s = lax.dot_general(qv, kw, NT, preferred_element_type=f32) * scale + b_ref[hh, 0]
            p = jnp.exp(s - l_ref[0, :, hh:hh + 1])
            do_ = do_ref[:, sl]
            dob = do_.astype(bf16)
            dp = lax.dot_general(dob, vw, NT, preferred_element_type=f32)
            ds = p * (dp - jnp.sum(do_ * o_ref[:, sl], axis=1, keepdims=True))
            dsb = ds.astype(bf16)
            dq_ref[:, sl] = jnp.dot(dsb, kw, preferred_element_type=f32) * scale
            dk_ref[pl.ds(ks, nk), sl] += lax.dot_general(dsb, qv, TN, preferred_element_type=f32) * scale
            dv_ref[pl.ds(ks, nk), sl] += lax.dot_general(p.astype(bf16), dob, TN, preferred_element_type=f32)

            @pl.when(first)
            def _(hh=hh, ds=ds):
                db_ref[hh, 0] = ds

            @pl.when(jnp.logical_not(first))
            def _(hh=hh, ds=ds):
                db_ref[hh, 0] += ds

    return pl.pallas_call(
        body, name=name, grid=(NAH // NA_PAIR, nrb),
        in_specs=[qs(qcol), fs(kcol), fs(vcol), bs, qs(0), ls, qs(docol)],
        out_specs=[qs(0), fs(0), fs(0), bs],
        out_shape=[S((T, NAW), f32), S((T, NAW), f32), S((T, NAW), f32), S(bias.shape, f32)],
        compiler_params=_params(("parallel", "arbitrary")))(q, k, v, bias, o, lse, do)


def _na_col_tables():
    c = np.arange(GRID_W)[:, None]
    kc = np.arange(GRID_W)[None, :]
    cstart = np.clip(c - NA_WC // 2, 0, GRID_W - NA_WC)
    valid_c = (kc >= cstart) & (kc < cstart + NA_WC)
    dc = kc - c + NA_WC - 1
    E = (valid_c[:, :, None] & (dc[:, :, None] == np.arange(2 * NA_WC - 1)[None, None, :])).astype(np.float32)
    return E, np.where(valid_c, 0.0, -1e30).astype(np.float32)


def _na_row_offsets(rows):
    table = []
    for r0 in (0, NA_QROWS, rows - NA_QROWS):
        ks = int(np.clip(r0 - NA_WR // 2, 0, rows - NA_KROWS))
        case = []
        for ri in range(NA_QROWS):
            r = r0 + ri
            rs = int(np.clip(r - NA_WR // 2, 0, rows - NA_WR))
            case.append([ks + kri - r + NA_WR - 1 if rs <= ks + kri < rs + NA_WR else None for kri in range(NA_KROWS)])
        table.append(case)
    return table


def _na_col_bias(rpb):
    E, cmask = _na_col_tables()
    return jnp.einsum("hde,cke->hdck", rpb, E, precision=HI) + cmask


def _na_bias_build(r1, rows, *, name):
    H = r1.shape[0]
    offs = _na_row_offsets(rows)

    def body(r_ref, o_ref):
        outside = jnp.full((GRID_W, GRID_W), -1e30, f32)
        for z in range(3):
            for a in range(NA_QROWS):
                for b in range(NA_KROWS):
                    d = offs[z][a][b]
                    o_ref[0, z, a * GRID_W:(a + 1) * GRID_W, b * GRID_W:(b + 1) * GRID_W] = outside if d is None else r_ref[0, d]

    return pl.pallas_call(
        body, name=name, grid=(H,), in_specs=[pl.BlockSpec((1,) + r1.shape[1:], lambda h: (h, 0, 0, 0))],
        out_specs=pl.BlockSpec((1, 3, NA_QROWS * GRID_W, NA_KROWS * GRID_W), lambda h: (h, 0, 0, 0)),
        out_shape=S((H, 3, NA_QROWS * GRID_W, NA_KROWS * GRID_W), f32), compiler_params=_params(("parallel",)))(r1)


def _na_bias_fold(dbias, rows, *, name):
    H = dbias.shape[0]
    offs = _na_row_offsets(rows)
    nd = 2 * NA_WR - 1

    def body(d_ref, o_ref):
        acc = [None] * nd
        for z in range(3):
            for a in range(NA_QROWS):
                for b in range(NA_KROWS):
                    d = offs[z][a][b]
                    if d is not None:
                        t = d_ref[0, z, a * GRID_W:(a + 1) * GRID_W, b * GRID_W:(b + 1) * GRID_W]
                        acc[d] = t if acc[d] is None else acc[d] + t
        for d in range(nd):
            o_ref[0, d] = acc[d]

    return pl.pallas_call(
        body, name=name, grid=(H,), in_specs=[pl.BlockSpec((1,) + dbias.shape[1:], lambda h: (h, 0, 0, 0))],
        out_specs=pl.BlockSpec((1, nd, GRID_W, GRID_W), lambda h: (h, 0, 0, 0)),
        out_shape=S((H, nd, GRID_W, GRID_W), f32), compiler_params=_params(("parallel",)))(dbias)


def _conv_shifts(prev, cur, nxt, i, n_i, W):
    tm = cur.shape[0]
    prev = jnp.where(i > 0, prev, 0.0)
    nxt = jnp.where(i < n_i - 1, nxt, 0.0)
    ext = jnp.concatenate([prev, cur, nxt], axis=0)
    out = []
    for w in range(W):
        s = (W // 2 - w) % (tm + 16)
        out.append((ext if s == 0 else pltpu.roll(ext, s, axis=0))[8:8 + tm])
    return out


def _conv_act(u, mode):
    if mode == "silu":
        return jax.nn.silu(u)
    if mode == "geglu":
        half = u.shape[1] // 2
        return jax.nn.gelu(u[:, :half], approximate=True) * u[:, half:]
    return u


def _conv_specs(T, tm, tc, xbase):
    r8 = tm // 8
    last = T // 8 - 1
    cur = pl.BlockSpec((tm, tc), lambda j, i: (i, xbase + j))
    prev = pl.BlockSpec((8, tc), lambda j, i: (jnp.maximum(i * r8 - 1, 0), xbase + j))
    nxt = pl.BlockSpec((8, tc), lambda j, i: (jnp.minimum((i + 1) * r8, last), xbase + j))
    return cur, prev, nxt


def _conv(x, w8, b, *, mode, W, name, C, xbase=0, tm=512, tc=512, out_dtype=f32):
    T = x.shape[0]
    NI, J = T // tm, C // tc
    tco = tc // 2 if mode == "geglu" else tc
    cur, prev, nxt = _conv_specs(T, tm, tc, xbase)

    def body(xc, xp, xn, w_ref, b_ref, o_ref):
        sh = _conv_shifts(xp[...].astype(f32), xc[...].astype(f32), xn[...].astype(f32), pl.program_id(1), NI, W)
        wv = w_ref[...]
        u = sh[0] * wv[0:1, :]
        for w in range(1, W):
            u = u + sh[w] * wv[w:w + 1, :]
        if mode != "none":
            u = u + b_ref[...]
        o_ref[...] = _conv_act(u, mode).astype(o_ref.dtype)

    return pl.pallas_call(
        body, name=name, grid=(J, NI),
        in_specs=[cur, prev, nxt, pl.BlockSpec((8, tc), lambda j, i: (0, j)), pl.BlockSpec((1, tc), lambda j, i: (0, j))],
        out_specs=pl.BlockSpec((tm, tco), lambda j, i: (i, j)), out_shape=S((T, J * tco), out_dtype),
        compiler_params=_params(("parallel", "parallel")))(x, x, x, w8, b)


def _conv_bwd(x, w8, b, dact, *, mode, W, name, C, xbase=0, tm=512, tc=512):
    T = x.shape[0]
    NI, J = T // tm, C // tc
    tco = tc // 2 if mode == "geglu" else tc
    rows = tm + 16
    pad = W // 2
    cur, prev, nxt = _conv_specs(T, tm, tc, xbase)
    dcur, dprev, dnxt = _conv_specs(T, tm, tco, 0)

    def body(xc, xp, xn, w_ref, b_ref, dc, dp, dn, dx_ref, dw_ref, db_ref):
        i = pl.program_id(1)
        ext = jnp.concatenate([jnp.where(i > 0, xp[...], 0.0), xc[...], jnp.where(i < NI - 1, xn[...], 0.0)], axis=0)
        dext = jnp.concatenate([jnp.where(i > 0, dp[...], 0.0), dc[...], jnp.where(i < NI - 1, dn[...], 0.0)], axis=0)
        wv = w_ref[...]
        shift = lambda t, w: t if w == pad else pltpu.roll(t, (pad - w) % rows, axis=0)
        xs = [shift(ext, w) for w in range(W)]
        u = b_ref[...] + xs[0] * wv[0:1, :]
        for w in range(1, W):
            u = u + xs[w] * wv[w:w + 1, :]
        _, vjp = jax.vjp(functools.partial(_conv_act, mode=mode), u)
        du = vjp(dext.astype(f32))[0]
        dx = shift(du, 0)[8:8 + tm] * wv[W - 1:W, :]
        for w in range(1, W):
            dx = dx + shift(du, w)[8:8 + tm] * wv[W - 1 - w:W - w, :]
        dx_ref[...] = dx.astype(dx_ref.dtype)

        @pl.when(i == 0)
        def _():
            dw_ref[...] = jnp.zeros_like(dw_ref)
            db_ref[...] = jnp.zeros_like(db_ref)

        dum = du[8:8 + tm]
        db_ref[...] += jnp.sum(dum, axis=0, keepdims=True)
        for w in range(W):
            dw_ref[w:w + 1, :] += jnp.sum(dum * xs[w][8:8 + tm], axis=0, keepdims=True)

    return pl.pallas_call(
        body, name=name, grid=(J, NI),
        in_specs=[cur, prev, nxt, pl.BlockSpec((8, tc), lambda j, i: (0, j)), pl.BlockSpec((1, tc), lambda j, i: (0, j)),
                  dcur, dprev, dnxt],
        out_specs=[pl.BlockSpec((tm, tc), lambda j, i: (i, j)), pl.BlockSpec((8, tc), lambda j, i: (0, j)),
                   pl.BlockSpec((1, tc), lambda j, i: (0, j))],
        out_shape=[S((T, C), bf16), S((8, C), f32), S((1, C), f32)],
        compiler_params=_params(("parallel", "arbitrary")))(x, x, x, w8, b, dact, dact, dact)


def _pad8(w):
    return jnp.concatenate([w, jnp.zeros((8 - w.shape[0], w.shape[1]), w.dtype)], axis=0)


def _all_gather(arrs, *, name):
    n = len(arrs)

    def body(*refs):
        ins, outs = refs[:n], refs[n:2 * n]
        send_sems, recv_sems, loc_sems = refs[2 * n:]
        x, y, c = lax.axis_index("x"), lax.axis_index("y"), lax.axis_index("c")
        ident = lambda px, py, pc: 4 * px + 2 * py + pc
        me, sibling = (x, y, c), (x, y, 1 - c)
        chips = [(1 - x, y), (x, 1 - y), (1 - x, 1 - y)]

        def copy(a, k, block, to, src=None):
            slot = outs[a].at[ident(*block)]
            return pltpu.make_async_remote_copy(
                src_ref=slot if src is None else src, dst_ref=slot, send_sem=send_sems.at[a * 7 + k], recv_sem=recv_sems.at[a * 7 + k],
                device_id=to, device_id_type=pl.DeviceIdType.MESH)

        local = [pltpu.make_async_copy(ins[a], outs[a].at[ident(*me)], loc_sems.at[a]) for a in range(n)]
        for cp in local:
            cp.start()
        first = []
        for a in range(n):
            first.append(copy(a, 0, me, sibling, src=ins[a]))
            first += [copy(a, 1 + j, me, (*chip, c), src=ins[a]) for j, chip in enumerate(chips)]
        for cp in first:
            cp.start()
        passed = []
        for j, chip in enumerate(chips):
            for a in range(n):
                copy(a, 1 + j, (*chip, c), me).wait_recv()
                fwd = copy(a, 4 + j, (*chip, c), sibling)
                fwd.start()
                passed.append(fwd)
        for a in range(n):
            copy(a, 0, sibling, me).wait_recv()
            for j, chip in enumerate(chips):
                copy(a, 4 + j, (*chip, 1 - c), me).wait_recv()
        for cp in first + passed:
            cp.wait_send()
        for cp in local:
            cp.wait()

    any_spec = pl.BlockSpec(memory_space=pl.ANY)
    return pl.pallas_call(
        body, name=name, in_specs=[any_spec] * n, out_specs=[any_spec] * n,
        out_shape=[S((NDEV,) + a.shape, a.dtype) for a in arrs],
        scratch_shapes=[pltpu.SemaphoreType.DMA((7 * n,)), pltpu.SemaphoreType.DMA((7 * n,)), pltpu.SemaphoreType.DMA((n,))],
        )(*arrs)


NCHIP = NDEV // 2


def _to_sibling(arrs, *, name):
    n = len(arrs)
    ncopy = sum(NCHIP if ps else 1 for _, ps in arrs)

    def body(*refs):
        ins, outs = refs[:n], refs[n:2 * n]
        send_sems, recv_sems = refs[2 * n:]
        x, y, c = lax.axis_index("x"), lax.axis_index("y"), lax.axis_index("c")
        copies, idx = [], 0
        for a, (_, per_slot) in enumerate(arrs):
            pairs = [(ins[a].at[2 * q + (1 - c)], outs[a].at[q]) for q in range(NCHIP)] if per_slot else [(ins[a], outs[a])]
            for src, dst in pairs:
                copies.append(pltpu.make_async_remote_copy(
                    src_ref=src, dst_ref=dst, send_sem=send_sems.at[idx], recv_sem=recv_sems.at[idx],
                    device_id=(x, y, 1 - c), device_id_type=pl.DeviceIdType.MESH))
                idx += 1
        for cp in copies:
            cp.start()
        for cp in copies:
            cp.wait_recv()
        for cp in copies:
            cp.wait_send()

    any_spec = pl.BlockSpec(memory_space=pl.ANY)
    return pl.pallas_call(
        body, name=name, in_specs=[any_spec] * n, out_specs=[any_spec] * n,
        out_shape=[S((NCHIP,) + a.shape[1:] if ps else a.shape, a.dtype) for a, ps in arrs],
        scratch_shapes=[pltpu.SemaphoreType.DMA((ncopy,)), pltpu.SemaphoreType.DMA((ncopy,))])(*[a for a, _ in arrs])


def _add_partials(mine, theirs, *, per_slot, tr, name):
    R, C = mine.shape[-2:]

    def body(a_ref, b_ref, o_ref):
        a = a_ref[lax.axis_index("c")] if per_slot else a_ref[...]
        b = b_ref[0] if per_slot else b_ref[...]
        s = a.astype(f32) + b.astype(f32)
        if per_slot:
            o_ref[0] = s.astype(o_ref.dtype)
        else:
            o_ref[...] = s.astype(o_ref.dtype)

    if per_slot:
        grid = (NCHIP, R // tr)
        in_specs = [pl.BlockSpec((2, tr, C), lambda q, i: (q, i, 0)), pl.BlockSpec((1, tr, C), lambda q, i: (q, i, 0))]
        out_spec, out_shape = pl.BlockSpec((1, tr, C), lambda q, i: (q, i, 0)), S((NCHIP, R, C), mine.dtype)
    else:
        grid = (1, R // tr)
        in_specs = [pl.BlockSpec((tr, C), lambda q, i: (i, 0))] * 2
        out_spec, out_shape = pl.BlockSpec((tr, C), lambda q, i: (i, 0)), S((R, C), mine.dtype)
    return pl.pallas_call(body, name=name, grid=grid, in_specs=in_specs, out_specs=out_spec, out_shape=out_shape,
                          compiler_params=_params(("parallel", "parallel")))(mine, theirs)


def _to_chips(arrs, *, name):
    n = len(arrs)

    def body(*refs):
        ins, outs = refs[:n], refs[n:2 * n]
        send_sems, recv_sems, loc_sems = refs[2 * n:]
        x, y, c = lax.axis_index("x"), lax.axis_index("y"), lax.axis_index("c")
        my_q = 2 * x + y
        src = lambda a, q: ins[a].at[q] if arrs[a][1] else ins[a]
        local = [pltpu.make_async_copy(src(a, my_q), outs[a].at[my_q], loc_sems.at[a]) for a in range(n)]
        for cp in local:
            cp.start()
        sent = []
        for j, (px, py) in enumerate([(1 - x, y), (x, 1 - y), (1 - x, 1 - y)]):
            q = 2 * px + py
            for a in range(n):
                mk = lambda slot, a=a, j=j, q=q, dev=(px, py, c): pltpu.make_async_remote_copy(
                    src_ref=src(a, q), dst_ref=outs[a].at[slot], send_sem=send_sems.at[3 * a + j], recv_sem=recv_sems.at[3 * a + j],
                    device_id=dev, device_id_type=pl.DeviceIdType.MESH)
                mk(my_q).start()
                sent.append((mk, q))
        for mk, q in sent:
            mk(q).wait_recv()
        for mk, q in sent:
            mk(q).wait_send()
        for cp in local:
            cp.wait()

    any_spec = pl.BlockSpec(memory_space=pl.ANY)
    return pl.pallas_call(
        body, name=name, in_specs=[any_spec] * n, out_specs=[any_spec] * n,
        out_shape=[S(a.shape if ps else (NCHIP,) + a.shape, a.dtype) for a, ps in arrs],
        scratch_shapes=[pltpu.SemaphoreType.DMA((3 * n,)), pltpu.SemaphoreType.DMA((3 * n,)), pltpu.SemaphoreType.DMA((n,))],
        )(*[a for a, _ in arrs])


def _adamw(r, w, m, v, *, name, tr):
    M, C = w.shape
    nparts = r.shape[0]

    def body(r_ref, w_ref, m_ref, v_ref, g_ref, d_ref, nm_ref, nv_ref):
        g = r_ref[0].astype(f32)
        for s in range(1, nparts):
            g = g + r_ref[s].astype(f32)
        m_ = B1 * m_ref[...] + (1.0 - B1) * g
        v_ = B2 * v_ref[...] + (1.0 - B2) * jnp.square(g)
        m_hat = m_ / (1.0 - B1 ** STEP)
        v_hat = v_ / (1.0 - B2 ** STEP)
        g_ref[...] = g
        d_ref[...] = -LR * (m_hat / (jnp.sqrt(v_hat) + AEPS) + WD * w_ref[...])
        nm_ref[...] = m_
        nv_ref[...] = v_

    row = pl.BlockSpec((tr, C), lambda i: (i, 0))
    return pl.pallas_call(
        body, name=name, grid=(M // tr,),
        in_specs=[pl.BlockSpec((nparts, tr, C), lambda i: (0, i, 0)), row, row, row],
        out_specs=[row] * 4, out_shape=[S((M, C), f32)] * 4, compiler_params=_params(("parallel",)))(r, w, m, v)


def _colmove(ins, in_slots, outs, moves, *, tk, name):
    R = ins[0].shape[1] if in_slots[0] else ins[0].shape[0]
    n_in = len(ins)

    def body(*refs):
        for ii, isl, ic, oi, osl, oc, w in moves:
            src, dst = refs[ii], refs[n_in + oi]
            val = src[:, ic:ic + w] if isl is None else src[isl, :, ic:ic + w]
            if osl is None:
                dst[:, oc:oc + w] = val.astype(dst.dtype)
            else:
                dst[osl, :, oc:oc + w] = val.astype(dst.dtype)

    def spec(is_slots, C):
        return pl.BlockSpec((NDEV, tk, C), lambda i: (0, i, 0)) if is_slots else pl.BlockSpec((tk, C), lambda i: (i, 0))

    return pl.pallas_call(
        body, name=name, grid=(R // tk,),
        in_specs=[spec(sl, a.shape[-1]) for a, sl in zip(ins, in_slots)],
        out_specs=[spec(sl, C) for sl, C, _ in outs],
        out_shape=[S((NDEV, R, C) if sl else (R, C), dt) for sl, C, dt in outs],
        compiler_params=_params(("parallel",)))(*ins)


def _col_pieces(n8, cuts, place):
    out = []
    for p in range(NDEV):
        lo, hi = p * n8, (p + 1) * n8
        edges = [lo] + [c for c in cuts if lo < c < hi] + [hi]
        for a, b in zip(edges[:-1], edges[1:]):
            out.append((p, a - lo) + place(a) + (b - a,))
    return out


def _place_plain(c):
    return (0, c)


def _place_ssd_in(c):
    return (0, c) if c < SSD_INNER + SSD_XBC else (1, c - (SSD_INNER + SSD_XBC))


def _place_ffn_up(c):
    h = FFN_TC // 2
    return (0, (c // h) * FFN_TC + c % h) if c < FFN else (0, ((c - FFN) // h) * FFN_TC + h + (c - FFN) % h)


_COL_LAYOUTS = {
    "ab_w_in": ([], _place_plain, [4 * RW + 3 * NAW]),
    "c_w_in": ([SSD_INNER + SSD_XBC], _place_ssd_in, [SSD_INNER + SSD_XBC, 2 * SSD_H]),
    "ffn_w_up": (list(range(FFN_TC // 2, 2 * FFN, FFN_TC // 2)), _place_ffn_up, [2 * FFN]),
}


def _cols_from_slots(g, which, *, name):
    cuts, place, widths = _COL_LAYOUTS[which]
    moves = [(0, p, sc, mi, None, mc, w) for p, sc, mi, mc, w in _col_pieces(g.shape[2], cuts, place)]
    return _colmove([g], [True], [(False, w, g.dtype) for w in widths], moves, tk=256, name=name)


def _cols_to_slots(mats, which, dtype, *, name):
    cuts, place, widths = _COL_LAYOUTS[which]
    n8 = sum(widths) // NDEV
    moves = [(mi, None, mc, 0, p, sc, w) for p, sc, mi, mc, w in _col_pieces(n8, cuts, place)]
    return _colmove(list(mats), [False] * len(mats), [(True, n8, dtype)], moves, tk=256, name=name)[0]


def _tm2hm(a, H):
    T = a.shape[0]
    return a.reshape(T, H, -1).transpose(1, 0, 2)


def _hm2tm(a):
    H, T, P = a.shape
    return a.transpose(1, 0, 2).reshape(T, H * P)


def _pack(parts, dtype, row_mult):
    flat = jnp.concatenate([p.reshape(-1).astype(dtype) for p in parts])
    rows = -(-flat.shape[0] // LANES)
    rows = -(-rows // row_mult) * row_mult
    return jnp.pad(flat, (0, rows * LANES - flat.shape[0])).reshape(rows, LANES)


def _unpack(buf, shapes, lead=()):
    flat = buf.reshape(lead + (-1,))
    out, off = [], 0
    for shp in shapes:
        n = int(np.prod(shp))
        out.append(flat[..., off:off + n].reshape(lead + tuple(shp)))
        off += n
    return out


def _to_slots(full, ax):
    shp = full.shape
    return jnp.moveaxis(full.reshape(shp[:ax] + (NDEV, shp[ax] // NDEV) + shp[ax + 1:]), ax, 0)


def _from_slots(g, ax):
    t = jnp.moveaxis(g, 0, ax)
    shp = t.shape
    return t.reshape(shp[:ax] + (shp[ax] * shp[ax + 1],) + shp[ax + 2:])


def _ffn_perm(a):
    lead = a.shape[:-1]
    h = FFN_TC // 2
    return jnp.swapaxes(a.reshape(lead + (2, FFN // h, h)), -3, -2).reshape(lead + (2 * FFN,))


def _ffn_unperm(a):
    lead = a.shape[:-1]
    h = FFN_TC // 2
    return jnp.swapaxes(a.reshape(lead + (FFN // h, 2, h)), -3, -2).reshape(lead + (2 * FFN,))


def _rope_tables(T):
    half = RDH // 2
    inv = 1.0 / (ROPE_BASE ** (jnp.arange(half, dtype=f32) / half))
    ang = jnp.arange(T, dtype=f32)[:, None] * inv[None, :]
    cos, sin = jnp.cos(ang), jnp.sin(ang)
    cos_t = jnp.tile(jnp.concatenate([cos, cos], axis=1), (1, RH))
    sin_t = jnp.tile(jnp.concatenate([-sin, sin], axis=1), (1, RH))
    return cos_t, sin_t


def _group_avg():
    g = np.arange(RW) // RDH
    return jnp.asarray((g[:, None] == g[None, :]).astype(np.float32) / RDH)


def _head_expand():
    hd = np.arange(SSD_INNER) // SSD_HD
    rows = np.arange(2 * SSD_H)
    ex0 = (rows[:, None] == hd[None, :]).astype(np.float32)
    ex1 = (rows[:, None] == SSD_H + hd[None, :]).astype(np.float32)
    return jnp.asarray(ex0), jnp.asarray(ex1)


def kernel(x, norm_mix_pre, norm_mix_post, norm_ffn_pre, norm_ffn_post, ab_w_in, ab_ret_decay_logit, ab_ret_gn_g, ab_na_rpb, ab_w_out, c_w_in, c_conv_w, c_conv_b, c_dt_bias, c_a_log, c_d_skip, c_norm_g, c_w_out, ffn_w_up, ffn_conv_w, ffn_conv_b, ffn_w_down, loss_target, m_norm_mix_pre, m_norm_mix_post, m_norm_ffn_pre, m_norm_ffn_post, m_ab_w_in, m_ab_ret_decay_logit, m_ab_ret_gn_g, m_ab_na_rpb, m_ab_w_out, m_c_w_in, m_c_conv_w, m_c_conv_b, m_c_dt_bias, m_c_a_log, m_c_d_skip, m_c_norm_g, m_c_w_out, m_ffn_w_up, m_ffn_conv_w, m_ffn_conv_b, m_ffn_w_down, v_norm_mix_pre, v_norm_mix_post, v_norm_ffn_pre, v_norm_ffn_post, v_ab_w_in, v_ab_ret_decay_logit, v_ab_ret_gn_g, v_ab_na_rpb, v_ab_w_out, v_c_w_in, v_c_conv_w, v_c_conv_b, v_c_dt_bias, v_c_a_log, v_c_d_skip, v_c_norm_g, v_c_w_out, v_ffn_w_up, v_ffn_conv_w, v_ffn_conv_b, v_ffn_w_down):
    W = dict(norm_mix_pre=norm_mix_pre, norm_mix_post=norm_mix_post, norm_ffn_pre=norm_ffn_pre, norm_ffn_post=norm_ffn_post, ab_w_in=ab_w_in, ab_ret_decay_logit=ab_ret_decay_logit, ab_ret_gn_g=ab_ret_gn_g, ab_na_rpb=ab_na_rpb, ab_w_out=ab_w_out, c_w_in=c_w_in, c_conv_w=c_conv_w, c_conv_b=c_conv_b, c_dt_bias=c_dt_bias, c_a_log=c_a_log, c_d_skip=c_d_skip, c_norm_g=c_norm_g, c_w_out=c_w_out, ffn_w_up=ffn_w_up, ffn_conv_w=ffn_conv_w, ffn_conv_b=ffn_conv_b, ffn_w_down=ffn_w_down)
    Mo = dict(norm_mix_pre=m_norm_mix_pre, norm_mix_post=m_norm_mix_post, norm_ffn_pre=m_norm_ffn_pre, norm_ffn_post=m_norm_ffn_post, ab_w_in=m_ab_w_in, ab_ret_decay_logit=m_ab_ret_decay_logit, ab_ret_gn_g=m_ab_ret_gn_g, ab_na_rpb=m_ab_na_rpb, ab_w_out=m_ab_w_out, c_w_in=m_c_w_in, c_conv_w=m_c_conv_w, c_conv_b=m_c_conv_b, c_dt_bias=m_c_dt_bias, c_a_log=m_c_a_log, c_d_skip=m_c_d_skip, c_norm_g=m_c_norm_g, c_w_out=m_c_w_out, ffn_w_up=m_ffn_w_up, ffn_conv_w=m_ffn_conv_w, ffn_conv_b=m_ffn_conv_b, ffn_w_down=m_ffn_w_down)
    Vo = dict(norm_mix_pre=v_norm_mix_pre, norm_mix_post=v_norm_mix_post, norm_ffn_pre=v_norm_ffn_pre, norm_ffn_post=v_norm_ffn_post, ab_w_in=v_ab_w_in, ab_ret_decay_logit=v_ab_ret_decay_logit, ab_ret_gn_g=v_ab_ret_gn_g, ab_na_rpb=v_ab_na_rpb, ab_w_out=v_ab_w_out, c_w_in=v_c_w_in, c_conv_w=v_c_conv_w, c_conv_b=v_c_conv_b, c_dt_bias=v_c_dt_bias, c_a_log=v_c_a_log, c_d_skip=v_c_d_skip, c_norm_g=v_c_norm_g, c_w_out=v_c_w_out, ffn_w_up=v_ffn_w_up, ffn_conv_w=v_ffn_conv_w, ffn_conv_b=v_ffn_conv_b, ffn_w_down=v_ffn_w_down)
    return _train_step(x[0], loss_target[0], W, Mo, Vo)


def _train_step(x, tgt, W, Mo, Vo):
    T = x.shape[0]
    rows = T // GRID_W

    col = lambda d, n, dt: d[n].reshape(-1, d[n].shape[-1]).astype(dt)
    rows_of = lambda d, dt: jnp.concatenate([col(d, n, dt) for n in ROW_SHARDED], axis=0)
    small = _pack([W[n] for n, _ in SHARDED[N_BIG:]], f32, 8)
    gat = _all_gather([col(W, n, bf16) for n in COL_SHARDED] + [rows_of(W, bf16), small], name="gather_weights")
    per_layer = lambda m: m.reshape(-1, D, m.shape[-1])
    w_ab_in = per_layer(_cols_from_slots(gat[0], "ab_w_in", name="cols_ab_w_in")[0])
    w_zx, w_dt = [per_layer(m) for m in _cols_from_slots(gat[1], "c_w_in", name="cols_c_w_in")]
    w_up = per_layer(_cols_from_slots(gat[2], "ffn_w_up", name="cols_ffn_w_up")[0])
    full, off = {}, 0
    for n in ROW_SHARDED:
        L, r = W[n].shape[0], W[n].shape[1]
        full[n] = jnp.swapaxes(gat[3][:, off:off + L * r].reshape(NDEV, L, r, D), 0, 1).reshape(L, NDEV * r, D)
        off += L * r
    gs = _unpack(gat[4], [W[n].shape for n, _ in SHARDED[N_BIG:]], (NDEV,))
    full.update({n: _from_slots(g, ax) for (n, ax), g in zip(SHARDED[N_BIG:], gs)})
    w_ab_out, w_c_out, w_down = full["ab_w_out"], full["c_w_out"], full["ffn_w_down"]
    c_cw8 = [_pad8(full["c_conv_w"][i]) for i in range(2)]
    c_cb = [full["c_conv_b"][i][None] for i in range(2)]
    c_ng = [full["c_norm_g"][i][None] for i in range(2)]
    f_cw8 = [_pad8(_ffn_perm(full["ffn_conv_w"][l])) for l in range(DEPTH)]
    f_cb = [_ffn_perm(W["ffn_conv_b"][l])[None] for l in range(DEPTH)]

    g1 = [W["norm_mix_pre"][l][None] for l in range(DEPTH)]
    g2 = [W["norm_mix_post"][l][None] for l in range(DEPTH)]
    g3 = [W["norm_ffn_pre"][l][None] for l in range(DEPTH)]
    g4 = [W["norm_ffn_post"][l][None] for l in range(DEPTH)]
    cos_t, sin_t = _rope_tables(T)
    gavg = _group_avg()
    ex0, ex1 = _head_expand()

    def log_gamma(logit):
        return -jax.nn.softplus(-logit)

    def ret_decays(lg):
        return [jnp.broadcast_to(lg[d][None, None, :], (1, T, RH)) for d in range(2)]

    saved = []
    xs_ = x
    hn = _rowwise("norm_first", _f_first, [(x, D, 0)], [], [(g1[0], D, 0)], [], [(D, bf16)], tm=256)[0]
    for l in range(DEPTH):
        i = l // 2
        sv = dict(x=xs_, hn=hn)
        if l % 2 == 0:
            proj = _mm_nn(hn, w_ab_in[i], name=f"ab_in_{l}")
            qr, kr = _rowwise(f"ret_prep_{l}", _f_rprep, [(proj, RW, 0), (proj, RW, 1)], [(cos_t, RW, 0), (sin_t, RW, 0)], [], [],
                              [(RW, f32), (RW, f32)], tm=256)
            lg, lg_vjp = jax.vjp(log_gamma, W["ab_ret_decay_logit"][i])
            a_f, a_b = ret_decays(lg)
            rscan = dict(G=RH, N=RDH, Hg=1, P=RDH, vcol=2)
            yf_t, hsf = _scan_fwd(qr, kr, proj, a_f, rev=False, name=f"ret_scan_f_{l}", **rscan)
            y_t, hsb = _scan_fwd(qr, kr, proj, a_b, rev=True, name=f"ret_scan_b_{l}", add_y=yf_t, **rscan)
            gn = W["ab_ret_gn_g"][i][None]
            ret = _rowwise(f"ret_post_{l}", _f_rpost, [(y_t, RW, 0), (proj, RW, 3)], [], [(gn, RW, 0)], [gavg],
                           [(RW, bf16)], tm=256)[0]
            nqkv = proj[:, 4 * RW:].astype(bf16)
            ncols = dict(qcol=0, kcol=NAW // 128, vcol=2 * NAW // 128)
            r1, bias_vjp = jax.vjp(_na_col_bias, W["ab_na_rpb"][i])
            bias = _na_bias_build(r1, rows, name=f"na_bias_{l}")
            na_o, na_l = _na_fwd(nqkv, nqkv, nqkv, bias, name=f"na_fwd_{l}", **ncols)
            cat = jnp.concatenate([ret, na_o.astype(bf16)], axis=1)
            mo = _mm_nn(cat, w_ab_out[i], name=f"ab_out_{l}")
            sv.update(proj=proj, qr=qr, kr=kr, a_f=a_f, a_b=a_b, hsf=hsf, hsb=hsb, y_t=y_t, gn=gn, rscan=rscan,
                      nqkv=nqkv, ncols=ncols, bias=bias, bias_vjp=bias_vjp, lg_vjp=lg_vjp, na_o=na_o, na_l=na_l, cat=cat)
        else:
            zx = _mm_nn(hn, w_zx[i], name=f"c_in_{l}")
            dtr = _mm_nn(hn, w_dt[i], name=f"c_in_dt_{l}")
            xa = _conv(zx, c_cw8[i], c_cb[i], mode="silu", W=SSD_CONV, name=f"c_conv_{l}", C=SSD_XBC, xbase=SSD_INNER // 512)
            dtb, alog = W["c_dt_bias"][i].reshape(1, 2 * SSD_H), W["c_a_log"][i].reshape(1, 2 * SSD_H)
            vf, vb, la = _rowwise(f"ssd_prep_{l}", _f_sprep, [(xa, SSD_INNER, 0), (dtr, 2 * SSD_H, 0)], [],
                                  [(dtb, 2 * SSD_H, 0), (alog, 2 * SSD_H, 0)], [ex0, ex1],
                                  [(SSD_INNER, f32), (SSD_INNER, f32), (2 * SSD_H, f32)], tm=128)
            a_f = la[:, :SSD_H].reshape(T, SSD_G, SSD_HPG).transpose(1, 0, 2)
            a_b = la[:, SSD_H:].reshape(T, SSD_G, SSD_HPG).transpose(1, 0, 2)
            sscan = dict(G=SSD_G, N=SSD_N, Hg=SSD_HPG, P=SSD_HD, qcol=(SSD_INNER + SSD_G * SSD_N) // SSD_N, kcol=SSD_INNER // SSD_N)
            yf_t, hsf = _scan_fwd(xa, xa, vf, a_f, rev=False, name=f"ssd_scan_f_{l}", **sscan)
            y_t, hsb = _scan_fwd(xa, xa, vb, a_b, rev=True, name=f"ssd_scan_b_{l}", add_y=yf_t, **sscan)
            dsk = jnp.repeat(W["c_d_skip"][i], SSD_HD)[None]
            yo = _rowwise(f"ssd_post_{l}", _f_spost, [(y_t, 512, 0), (xa, 512, 0), (zx, 512, 0)], [],
                          [(dsk, 512, 0), (c_ng[i], 512, 0)], [], [(512, bf16)], tm=256, J=SSD_G)[0]
            mo = _mm_nn(yo, w_c_out[i], name=f"c_out_{l}")
            sv.update(zx=zx, dtr=dtr, xa=xa, dtb=dtb, alog=alog, a_f=a_f, a_b=a_b, vf=vf, vb=vb, sscan=sscan,
                      hsf=hsf, hsb=hsb, y_t=y_t, dsk=dsk, yo=yo)
        x1, hf = _rowwise(f"norm_mid_{l}", _f_mid, [(xs_, D, 0), (mo, D, 0)], [], [(g2[l], D, 0), (g3[l], D, 0)], [],
                          [(D, f32), (D, bf16)], tm=256)
        pre = _mm_nn(hf, w_up[l], name=f"ffn_up_{l}")
        act = _conv(pre, f_cw8[l], f_cb[l], mode="geglu", W=FFN_CONV, name=f"ffn_conv_{l}", C=2 * FFN, tc=FFN_TC, out_dtype=bf16)
        fo = _mm_nn(act, w_down[l], name=f"ffn_down_{l}")
        sv.update(mo=mo, x1=x1, hf=hf, pre=pre, act=act, fo=fo)
        if l < DEPTH - 1:
            xs_, hn = _rowwise(f"norm_end_{l}", _f_end, [(x1, D, 0), (fo, D, 0)], [], [(g4[l], D, 0), (g1[l + 1], D, 0)], [],
                               [(D, f32), (D, bf16)], tm=256)
        else:
            xs_ = _rowwise(f"norm_end_{l}", _f_last, [(x1, D, 0), (fo, D, 0)], [], [(g4[l], D, 0)], [], [(D, f32)], tm=256)[0]
        saved.append(sv)

    dx, lpart = _loss_call(xs_, tgt)
    loss = lax.psum(lpart[0, 0], ("x", "y", "c"))

    G = {n: [None] * W[n].shape[0] for n in WEIGHTS}
    dhn = None
    for l in reversed(range(DEPTH)):
        i = l // 2
        sv = saved[l]
        if l == DEPTH - 1:
            (dx1, dfo), (dg4,) = _rowwise_bwd(f"norm_end_bwd_{l}", _f_last, [(sv["x1"], D, 0), (sv["fo"], D, 0)], [],
                                              [(g4[l], D, 0)], [], [(dx, D, 0)], [f32, bf16], tm=256)
        else:
            (dx1, dfo), (dg4, dg1n) = _rowwise_bwd(f"norm_end_bwd_{l}", _f_end, [(sv["x1"], D, 0), (sv["fo"], D, 0)], [],
                                                   [(g4[l], D, 0), (g1[l + 1], D, 0)], [], [(dx, D, 0), (dhn, D, 0)],
                                                   [f32, bf16], tm=256)
            G["norm_mix_pre"][l + 1] = dg1n[0]
        G["norm_ffn_post"][l] = dg4[0]
        dact = _mm_nt(dfo, w_down[l], name=f"ffn_down_dx_{l}")
        G["ffn_w_down"][l] = _mm_tn(sv["act"], dfo, name=f"ffn_down_dw_{l}")
        dpre, dfw, dfb = _conv_bwd(sv["pre"], f_cw8[l], f_cb[l], dact, mode="geglu", W=FFN_CONV, name=f"ffn_conv_bwd_{l}",
                                   C=2 * FFN, tc=FFN_TC)
        dhf = _mm_nt(dpre, w_up[l], name=f"ffn_up_dx_{l}")
        G["ffn_w_up"][l] = _cols_to_slots([_mm_tn(sv["hf"], dpre, name=f"ffn_up_dw_{l}")], "ffn_w_up", bf16, name=f"slots_ffn_up_{l}")
        G["ffn_conv_w"][l] = _ffn_unperm(dfw[:FFN_CONV])
        G["ffn_conv_b"][l] = _ffn_unperm(dfb[0])
        (dxl, dmo), (dg2, dg3) = _rowwise_bwd(f"norm_mid_bwd_{l}", _f_mid, [(sv["x"], D, 0), (sv["mo"], D, 0)], [],
                                              [(g2[l], D, 0), (g3[l], D, 0)], [], [(dx1, D, 0), (dhf, D, 0)], [f32, bf16], tm=256)
        G["norm_mix_post"][l] = dg2[0]
        G["norm_ffn_pre"][l] = dg3[0]
        if l % 2 == 0:
            dcat = _mm_nt(dmo, w_ab_out[i], name=f"ab_out_dx_{l}")
            G["ab_w_out"][i] = _mm_tn(sv["cat"], dmo, name=f"ab_out_dw_{l}")
            (dy, drg), (dgn,) = _rowwise_bwd(
                f"ret_post_bwd_{l}", _f_rpost, [(sv["y_t"], RW, 0), (sv["proj"], RW, 3)], [],
                [(sv["gn"], RW, 0)], [gavg], [(dcat, RW, 0)], [f32, bf16], tm=256)
            G["ab_ret_gn_g"][i] = dgn[0]
            dqf, dkf, dvf, daf = _scan_bwd(sv["qr"], sv["kr"], sv["proj"], sv["a_f"], sv["hsf"], dy, rev=False,
                                           name=f"ret_scan_f_bwd_{l}", **sv["rscan"])
            dq_t, dk_t, dv_t, dab = _scan_bwd(sv["qr"], sv["kr"], sv["proj"], sv["a_b"], sv["hsb"], dy, rev=True,
                                              name=f"ret_scan_b_bwd_{l}", add_to=(dqf, dkf, dvf), **sv["rscan"])
            drv = dv_t.astype(bf16)
            (drq, drk), _ = _rowwise_bwd(f"ret_prep_bwd_{l}", _f_rprep, [(sv["proj"], RW, 0), (sv["proj"], RW, 1)],
                                         [(cos_t, RW, 0), (sin_t, RW, 0)], [], [], [(dq_t, RW, 0), (dk_t, RW, 0)], [bf16, bf16], tm=256)
            da_cols = jnp.concatenate([daf[0], dab[0]], axis=1)
            dlg = _colsum(da_cols, name=f"ret_decay_sum_{l}").reshape(2, RH)
            G["ab_ret_decay_logit"][i] = sv["lg_vjp"](dlg)[0]
            dnq, dnk, dnv, dbias = _na_bwd(sv["nqkv"], sv["nqkv"], sv["nqkv"], sv["bias"], sv["na_o"], sv["na_l"], dcat,
                                           docol=RW // 128, name=f"na_bwd_{l}", **sv["ncols"])
            G["ab_na_rpb"][i] = sv["bias_vjp"](_na_bias_fold(dbias, rows, name=f"na_bias_fold_{l}"))[0]
            dproj = jnp.concatenate([drq, drk, drv, drg] + [t.astype(bf16) for t in (dnq, dnk, dnv)], axis=1)
            dhn = _mm_nt(dproj, w_ab_in[i], name=f"ab_in_dx_{l}")
            G["ab_w_in"][i] = _cols_to_slots([_mm_tn(sv["hn"], dproj, name=f"ab_in_dw_{l}")], "ab_w_in", bf16, name=f"slots_ab_in_{l}")
        else:
            dyo = _mm_nt(dmo, w_c_out[i], name=f"c_out_dx_{l}")
            G["c_w_out"][i] = _mm_tn(sv["yo"], dmo, name=f"c_out_dw_{l}")
            (dy, dxs1, dz), (ddsk, dng) = _rowwise_bwd(
                f"ssd_post_bwd_{l}", _f_spost, [(sv["y_t"], 512, 0), (sv["xa"], 512, 0), (sv["zx"], 512, 0)],
                [], [(sv["dsk"], 512, 0), (c_ng[i], 512, 0)], [], [(dyo, 512, 0)], [f32, f32, bf16], tm=256, J=SSD_G)
            G["c_d_skip"][i] = ddsk.reshape(SSD_H, SSD_HD).sum(axis=1)
            G["c_norm_g"][i] = dng[0]
            dqf, dkf, dvf, daf = _scan_bwd(sv["xa"], sv["xa"], sv["vf"], sv["a_f"], sv["hsf"], dy, rev=False,
                                           name=f"ssd_scan_f_bwd_{l}", **sv["sscan"])
            dq_t, dk_t, dvb, dab = _scan_bwd(sv["xa"], sv["xa"], sv["vb"], sv["a_b"], sv["hsb"], dy, rev=True,
                                             name=f"ssd_scan_b_bwd_{l}", add_to=(dqf, dkf, None), **sv["sscan"])
            dla = jnp.concatenate([daf.transpose(1, 0, 2).reshape(T, SSD_H), dab.transpose(1, 0, 2).reshape(T, SSD_H)], axis=1)
            (dxs, ddtr), (ddtb, dalog) = _rowwise_bwd(
                f"ssd_prep_bwd_{l}", _f_sprep_bwd, [(sv["xa"], SSD_INNER, 0), (sv["dtr"], 2 * SSD_H, 0)], [],
                [(sv["dtb"], 2 * SSD_H, 0), (sv["alog"], 2 * SSD_H, 0)], [ex0, ex1],
                [(dvf, SSD_INNER, 0), (dvb, SSD_INNER, 0), (dla, 2 * SSD_H, 0), (dxs1, SSD_INNER, 0)], [f32, bf16], tm=128)
            G["c_dt_bias"][i] = ddtb.reshape(2, SSD_H)
            G["c_a_log"][i] = dalog.reshape(2, SSD_H)
            dxa = jnp.concatenate([dxs, dk_t, dq_t], axis=1)
            dxbc, dcw, dcb = _conv_bwd(sv["zx"], c_cw8[i], c_cb[i], dxa, mode="silu", W=SSD_CONV, name=f"c_conv_bwd_{l}",
                                       C=SSD_XBC, xbase=SSD_INNER // 512)
            G["c_conv_w"][i] = dcw[:SSD_CONV]
            G["c_conv_b"][i] = dcb[0]
            dzx = jnp.concatenate([dz, dxbc], axis=1)
            t1 = _mm_nt(ddtr, w_dt[i], name=f"c_in_dt_dx_{l}")
            dhn = _mm_nt(dzx, w_zx[i], add=t1, name=f"c_in_dx_{l}")
            G["c_w_in"][i] = _cols_to_slots([_mm_tn(sv["hn"], dzx, name=f"c_in_dw_{l}"), _mm_tn(sv["hn"], ddtr, name=f"c_in_dt_dw_{l}")],
                                            "c_w_in", bf16, name=f"slots_c_in_{l}")
        dx = dxl
    (grad_x,), (dg1,) = _rowwise_bwd("norm_first_bwd", _f_first_bwd, [(x, D, 0)], [], [(g1[0], D, 0)], [], [(dx, D, 0), (dhn, D, 0)],
                                     [f32], tm=256)
    G["norm_mix_pre"][0] = dg1[0]

    small_names = [n for n, _ in SHARDED[N_BIG:]]
    col_slots = [jnp.concatenate(G[n], axis=1) for n in COL_SHARDED]
    row_slots = jnp.concatenate([g.reshape(NDEV, -1, D).astype(bf16) for n in ROW_SHARDED for g in G[n]], axis=1)
    small_slots = _pack_slots([_to_slots(jnp.stack(G[n]), ax) for n, ax in SHARDED[N_BIG:]], 8)
    ar = _pack([jnp.stack(G[n]) for n in REPLICATED], f32, 8)
    parts = [(a, True) for a in col_slots + [row_slots, small_slots]] + [(ar, False)]
    from_sib = _to_sibling(parts, name="grads_to_sibling")
    tiles = [256, 256, 256, 64, small_slots.shape[1], ar.shape[0]]
    chip = [_add_partials(a, b, per_slot=ps, tr=t, name=f"grads_add_{j}")
            for j, ((a, ps), b, t) in enumerate(zip(parts, from_sib, tiles))]
    exch = _to_chips([(a, ps) for a, (_, ps) in zip(chip, parts)], name="grads_to_chips")
    pk = lambda d, names: _pack([d[n] for n in names], f32, 8)
    upd = [_adamw(exch[j], col(W, n, f32), col(Mo, n, f32), col(Vo, n, f32), name=f"adamw_{n}", tr=256)
           for j, n in enumerate(COL_SHARDED)]
    upd_rows = _adamw(exch[3], rows_of(W, f32), rows_of(Mo, f32), rows_of(Vo, f32), name="adamw_rows", tr=64)
    upd_small = _adamw(exch[4], pk(W, small_names), pk(Mo, small_names), pk(Vo, small_names), name="adamw_small",
                       tr=small_slots.shape[1])
    upd_rep = _adamw(exch[5], pk(W, REPLICATED), pk(Mo, REPLICATED), pk(Vo, REPLICATED), name="adamw_replicated", tr=ar.shape[0])
    res = []
    for k in range(4):
        d = {n: upd[j][k].reshape(W[n].shape) for j, n in enumerate(COL_SHARDED)}
        off = 0
        for n in ROW_SHARDED:
            cnt = W[n].shape[0] * W[n].shape[1]
            d[n] = upd_rows[k][off:off + cnt].reshape(W[n].shape)
            off += cnt
        d.update(zip(small_names, _unpack(upd_small[k], [W[n].shape for n in small_names])))
        d.update(zip(REPLICATED, _unpack(upd_rep[k], [W[n].shape for n in REPLICATED])))
        res.append(d)
    outs = [loss, grad_x[None]]
    for k in range(4):
        outs += [res[k][n] for n in WEIGHTS]
    return tuple(outs)


def _pack_slots(slot_arrays, row_mult):
    flat = jnp.concatenate([a.reshape(NDEV, -1) for a in slot_arrays], axis=1)
    rows = -(-flat.shape[1] // LANES)
    rows = -(-rows // row_mult) * row_mult
    return jnp.pad(flat, ((0, 0), (0, rows * LANES - flat.shape[1]))).reshape(NDEV, rows, LANES)
```

```python
import functools
import numpy as np
import jax
import jax.numpy as jnp
from jax import lax
from jax.experimental import pallas as pl
from jax.experimental.pallas import tpu as pltpu

f32, bf16 = jnp.float32, jnp.bfloat16
S = jax.ShapeDtypeStruct
HI = lax.Precision.HIGHEST

D = 1024
DEPTH = 4
GRID_W = 64
CHUNK = 128
EPS = 1e-6
RH, RDH, RW = 8, 64, 512
NAH, NADH, NAW = 8, 64, 512
NA_WR, NA_WC = 8, 16
NA_QROWS = 8
NA_KROWS = 16
NA_PAIR = 2
SSD_INNER, SSD_HD, SSD_H, SSD_G, SSD_HPG, SSD_N, SSD_CONV = 2048, 64, 32, 4, 8, 128, 5
SSD_XBC = SSD_INNER + 2 * SSD_G * SSD_N
FFN, FFN_CONV = 2816, 3
FFN_TC = 512
SCAN_HEADS_PER_STEP = 8
ROPE_BASE = 10000.0
LR, B1, B2, AEPS, WD, STEP = 0.001, 0.9, 0.999, 1e-08, 0.01, 10
NDEV = 8
LANES = 128
VMEM_LIMIT = 56 * 1024 * 1024
MM_BLOCK_BYTES = 6 * 1024 * 1024

NT = (((1,), (1,)), ((), ()))
TN = (((0,), (0,)), ((), ()))

SHARDED = [("ab_w_in", 2), ("ab_w_out", 1), ("c_w_in", 2), ("c_w_out", 1), ("ffn_w_up", 2), ("ffn_w_down", 1),
           ("c_conv_w", 2), ("c_conv_b", 1), ("c_norm_g", 1), ("ffn_conv_w", 2)]
N_BIG = 6
COL_SHARDED = ["ab_w_in", "c_w_in", "ffn_w_up"]
ROW_SHARDED = ["ab_w_out", "c_w_out", "ffn_w_down"]
REPLICATED = ["norm_mix_pre", "norm_mix_post", "norm_ffn_pre", "norm_ffn_post", "ab_ret_decay_logit", "ab_ret_gn_g",
              "ab_na_rpb", "c_dt_bias", "c_a_log", "c_d_skip", "ffn_conv_b"]
WEIGHTS = ["norm_mix_pre", "norm_mix_post", "norm_ffn_pre", "norm_ffn_post", "ab_w_in", "ab_ret_decay_logit",
           "ab_ret_gn_g", "ab_na_rpb", "ab_w_out", "c_w_in", "c_conv_w", "c_conv_b", "c_dt_bias", "c_a_log", "c_d_skip",
           "c_norm_g", "c_w_out", "ffn_w_up", "ffn_conv_w", "ffn_conv_b", "ffn_w_down"]


def _params(sem=None):
    return pltpu.CompilerParams(dimension_semantics=sem, vmem_limit_bytes=VMEM_LIMIT)


def _mm_nn(a, w, *, name, tm=1024, tn=512, out_dtype=f32):
    M, K = a.shape
    N = w.shape[1]
    tn = min(tn, N)

    def body(a_ref, w_ref, o_ref):
        o_ref[...] = jnp.dot(a_ref[...], w_ref[...], preferred_element_type=f32).astype(o_ref.dtype)

    return pl.pallas_call(
        body, name=name, grid=(M // tm, N // tn),
        in_specs=[pl.BlockSpec((tm, K), lambda i, j: (i, 0)), pl.BlockSpec((K, tn), lambda i, j: (0, j))],
        out_specs=pl.BlockSpec((tm, tn), lambda i, j: (i, j)),
        out_shape=S((M, N), out_dtype), compiler_params=_params(("parallel", "parallel")))(a, w)


def _mm_nt(dy, w, *, name, add=None, tm=512):
    M, N = dy.shape
    K = w.shape[0]
    tk = next((t for t in (1024, 1408, 512, 256, 128) if K % t == 0 and (t <= 512 or t * N * 2 <= MM_BLOCK_BYTES)), K)

    def body(*refs):
        if add is None:
            d_ref, w_ref, o_ref = refs
            o_ref[...] = lax.dot_general(d_ref[...], w_ref[...], NT, preferred_element_type=f32)
        else:
            d_ref, w_ref, a_ref, o_ref = refs
            o_ref[...] = lax.dot_general(d_ref[...], w_ref[...], NT, preferred_element_type=f32) + a_ref[...]

    in_specs = [pl.BlockSpec((tm, N), lambda i, j: (i, 0)), pl.BlockSpec((tk, N), lambda i, j: (j, 0))]
    args = [dy, w]
    if add is not None:
        in_specs.append(pl.BlockSpec((tm, tk), lambda i, j: (i, j)))
        args.append(add)
    return pl.pallas_call(
        body, name=name, grid=(M // tm, K // tk), in_specs=in_specs,
        out_specs=pl.BlockSpec((tm, tk), lambda i, j: (i, j)),
        out_shape=S((M, K), f32), compiler_params=_params(("parallel", "parallel")))(*args)


def _mm_tn(a, dy, *, name, tt=1024):
    M, K = a.shape
    N = dy.shape[1]
    tk = K if K <= 1024 else (1024 if K % 1024 == 0 else K // 2)
    tn = min(512, N)
    tt = min(tt, M)

    def body(a_ref, d_ref, o_ref):
        t = pl.program_id(2)
        part = lax.dot_general(a_ref[...], d_ref[...], TN, preferred_element_type=f32)

        @pl.when(t == 0)
        def _():
            o_ref[...] = part

        @pl.when(t > 0)
        def _():
            o_ref[...] += part

    return pl.pallas_call(
        body, name=name, grid=(K // tk, N // tn, M // tt),
        in_specs=[pl.BlockSpec((tt, tk), lambda k, n, t: (t, k)), pl.BlockSpec((tt, tn), lambda k, n, t: (t, n))],
        out_specs=pl.BlockSpec((tk, tn), lambda k, n, t: (k, n)),
        out_shape=S((K, N), f32), compiler_params=_params(("parallel", "parallel", "arbitrary")))(a, dy)


def _tile_spec(tm, width, base):
    return pl.BlockSpec((tm, width), lambda j, i: (i, base + j))


def _par_spec(width, base):
    return pl.BlockSpec((1, width), lambda j, i: (0, base + j))


def _full_spec(a):
    nd = a.ndim
    return pl.BlockSpec(a.shape, lambda j, i: (0,) * nd)


def _rowwise(name, f, tiles, ctiles, params, consts, outs, *, tm, J=1):
    T = tiles[0][0].shape[0]
    nt, nct, npar, nc = len(tiles), len(ctiles), len(params), len(consts)

    def body(*refs):
        tv = [r[...].astype(f32) for r in refs[:nt + nct]]
        pv = [r[...] for r in refs[nt + nct:nt + nct + npar + nc]]
        res = f(*tv, *pv)
        for o, v in zip(refs[nt + nct + npar + nc:], res):
            o[...] = v.astype(o.dtype)

    in_specs = ([_tile_spec(tm, w, b) for _, w, b in tiles + ctiles] + [_par_spec(w, b) for _, w, b in params]
                + [_full_spec(c) for c in consts])
    return pl.pallas_call(
        body, name=name, grid=(J, T // tm), in_specs=in_specs,
        out_specs=[_tile_spec(tm, w, 0) for w, _ in outs],
        out_shape=[S((T, J * w), dt) for w, dt in outs],
        compiler_params=_params(("parallel", "parallel")))(
            *[a for a, _, _ in tiles + ctiles], *[a for a, _, _ in params], *consts)


def _rowwise_bwd(name, f, tiles, ctiles, params, consts, douts, dtile_dtypes, *, tm, J=1):
    T = tiles[0][0].shape[0]
    nt, nct, npar, nc, nd = len(tiles), len(ctiles), len(params), len(consts), len(douts)

    def body(*refs):
        i = pl.program_id(1)
        k = 0
        tv = [r[...].astype(f32) for r in refs[k:k + nt]]; k += nt
        cv = [r[...].astype(f32) for r in refs[k:k + nct]]; k += nct
        pv = [r[...] for r in refs[k:k + npar]]; k += npar
        kv = [r[...] for r in refs[k:k + nc]]; k += nc
        dv = [r[...].astype(f32) for r in refs[k:k + nd]]; k += nd
        dt_refs = refs[k:k + nt]; k += nt
        dp_refs = refs[k:k + npar]
        _, vjp = jax.vjp(lambda tv_, pv_: tuple(f(*tv_, *cv, *pv_, *kv)), tv, pv)
        dts, dps = vjp(tuple(dv))
        for r, g in zip(dt_refs, dts):
            r[...] = g.astype(r.dtype)
        for r, g in zip(dp_refs, dps):
            @pl.when(i == 0)
            def _(r=r, g=g):
                r[...] = g

            @pl.when(i > 0)
            def _(r=r, g=g):
                r[...] += g

    in_specs = ([_tile_spec(tm, w, b) for _, w, b in tiles + ctiles] + [_par_spec(w, b) for _, w, b in params]
                + [_full_spec(c) for c in consts] + [_tile_spec(tm, w, b) for _, w, b in douts])
    res = pl.pallas_call(
        body, name=name, grid=(J, T // tm), in_specs=in_specs,
        out_specs=[_tile_spec(tm, w, 0) for _, w, _ in tiles] + [_par_spec(w, b) for _, w, b in params],
        out_shape=[S((T, J * w), dt) for (_, w, _), dt in zip(tiles, dtile_dtypes)] + [S(a.shape, f32) for a, _, _ in params],
        compiler_params=_params(("parallel", "arbitrary")))(
            *[a for a, _, _ in tiles + ctiles], *[a for a, _, _ in params], *consts, *[a for a, _, _ in douts])
    return res[:nt], res[nt:]


def _rms(x, g):
    return x * lax.rsqrt(jnp.mean(x * x, axis=-1, keepdims=True) + EPS) * g


def _f_first(x, g1):
    return (_rms(x, g1),)


def _f_first_bwd(x, g1):
    return (x, _rms(x, g1))


def _f_mid(x, m, g2, g3):
    x1 = x + _rms(m, g2)
    return (x1, _rms(x1, g3))


def _f_end(x1, fo, g4, g1n):
    x2 = x1 + _rms(fo, g4)
    return (x2, _rms(x2, g1n))


def _f_last(x1, fo, g4):
    return (x1 + _rms(fo, g4),)


@jax.custom_vjp
def _swap_halves(x):
    c = x.shape[1]
    lane = lax.broadcasted_iota(jnp.int32, x.shape, 1) % RDH
    return jnp.where(lane < RDH // 2, pltpu.roll(x, c - RDH // 2, axis=1), pltpu.roll(x, RDH // 2, axis=1))


_swap_halves.defvjp(lambda x: (_swap_halves(x), None), lambda _, g: (_swap_halves(g),))


def _f_rprep(rq, rk, cos, sin):
    rot = lambda t: t * cos + _swap_halves(t) * sin
    return (rot(rq), rot(rk) * (RDH ** -0.5))


def _split3(x):
    h1 = x.astype(bf16)
    r1 = x - h1.astype(f32)
    h2 = r1.astype(bf16)
    return h1, h2, (r1 - h2.astype(f32)).astype(bf16)


@jax.custom_vjp
def _dot_sel(x, m):
    mb = m.astype(bf16)
    h1, h2, h3 = _split3(x)
    return jnp.dot(h1, mb, preferred_element_type=f32) + jnp.dot(h2, mb, preferred_element_type=f32) + jnp.dot(h3, mb, preferred_element_type=f32)


def _dot_sel_bwd(m, g):
    mb = m.astype(bf16)
    g1, g2, g3 = _split3(g)
    nt = lambda a: lax.dot_general(a, mb, NT, preferred_element_type=f32)
    return nt(g1) + nt(g2) + nt(g3), jnp.zeros_like(m)


_dot_sel.defvjp(lambda x, m: (_dot_sel(x, m), m), _dot_sel_bwd)


def _f_rpost(y, rg, gn, gavg):
    mu = _dot_sel(y, gavg)
    yc = y - mu
    var = _dot_sel(yc * yc, gavg)
    return (jax.nn.silu(rg) * (yc * lax.rsqrt(var + EPS) * gn),)


def _f_sprep(xs, dtr, dtb, alog, ex0, ex1):
    dt = jax.nn.softplus(dtr + dtb)
    la = dt * (-jnp.exp(alog))
    return (xs * _dot_sel(dt, ex0), xs * _dot_sel(dt, ex1), la)


def _f_sprep_bwd(xs, dtr, dtb, alog, ex0, ex1):
    return _f_sprep(xs, dtr, dtb, alog, ex0, ex1) + (xs,)


def _f_spost(y, xs, z, dsk, ng):
    y = (y + xs * dsk) * jax.nn.silu(z)
    y = y * lax.rsqrt(jnp.mean(y * y, axis=-1, keepdims=True) + EPS)
    return (y * ng,)


def _loss_call(y, tgt, *, tm=256):
    T = y.shape[0]

    def body(y_ref, t_ref, dy_ref, l_ref):
        i = pl.program_id(0)
        e = y_ref[...] - t_ref[...]
        dy_ref[...] = e * (1.0 / D)
        part = jnp.zeros((8, LANES), f32) + 0.5 * jnp.sum(jnp.mean(e * e, axis=-1, keepdims=True))

        @pl.when(i == 0)
        def _():
            l_ref[...] = part

        @pl.when(i > 0)
        def _():
            l_ref[...] += part

    return pl.pallas_call(
        body, name="loss_head", grid=(T // tm,),
        in_specs=[pl.BlockSpec((tm, D), lambda i: (i, 0))] * 2,
        out_specs=[pl.BlockSpec((tm, D), lambda i: (i, 0)), pl.BlockSpec((8, LANES), lambda i: (0, 0))],
        out_shape=[S((T, D), f32), S((8, LANES), f32)], compiler_params=_params(("arbitrary",)))(y, tgt)


def _colsum(x, *, name, tm=512):
    T, C = x.shape

    def body(x_ref, o_ref):
        i = pl.program_id(0)
        part = jnp.sum(x_ref[...], axis=0, keepdims=True)

        @pl.when(i == 0)
        def _():
            o_ref[...] = part

        @pl.when(i > 0)
        def _():
            o_ref[...] += part

    return pl.pallas_call(
        body, name=name, grid=(T // tm,), in_specs=[pl.BlockSpec((tm, C), lambda i: (i, 0))],
        out_specs=pl.BlockSpec((1, C), lambda i: (0, 0)), out_shape=S((1, C), f32),
        compiler_params=_params(("arbitrary",)))(x)


def _nn(a, b):
    if a.ndim == 3:
        return lax.dot_general(a, b, (((2,), (1,)), ((0,), (0,))), preferred_element_type=f32)
    return jnp.dot(a, b, preferred_element_type=f32)


def _nt(a, b):
    if a.ndim == 3:
        return lax.dot_general(a, b, (((2,), (2,)), ((0,), (0,))), preferred_element_type=f32)
    return lax.dot_general(a, b, NT, preferred_element_type=f32)


def _lift(x, like):
    return jnp.broadcast_to(x[None], like.shape[:1] + x.shape) if x.ndim < like.ndim else x


def _drop(g, like):
    return jnp.sum(g, axis=0) if like.ndim < g.ndim else g


@jax.custom_vjp
def _mm_lt(a, a_t, b):
    return _nn(_lift(a_t, b), b)


_mm_lt.defvjp(lambda a, a_t, b: (_nn(_lift(a_t, b), b), (a, b)),
              lambda res, g: (jnp.zeros_like(res[0]), _drop(_nt(g, res[1]), res[0]), _nn(_lift(res[0], g), g)))


@jax.custom_vjp
def _mm_rt(a, a_t, b):
    return _nn(_lift(a, b), b)


_mm_rt.defvjp(lambda a, a_t, b: (_nn(_lift(a, b), b), (a_t, b)),
              lambda res, g: (_drop(_nt(g, res[1]), res[0]), jnp.zeros_like(res[0]), _nn(_lift(res[0], g), g)))


@jax.custom_vjp
def _masked_mm(s, s_t, d, d_t, v):
    return _nn(s * d, v)


def _masked_mm_bwd(res, g):
    s, s_t, d, d_t, v = res
    da = _nt(g, v)
    return (_drop(da * d, s), jnp.zeros_like(s_t), da * s, jnp.zeros_like(d_t), _nn(s_t * d_t, g))


_masked_mm.defvjp(lambda s, s_t, d, d_t, v: (_nn(s * d, v), (s, s_t, d, d_t, v)), _masked_mm_bwd)


def _t(x):
    return jnp.swapaxes(x, -1, -2)


def _scan_step_heads(h, q, k, v, a, rev, for_vjp=False):
    B, L, P = v.shape
    ii = lax.broadcasted_iota(jnp.int32, (L, L), 0)
    jj = lax.broadcasted_iota(jnp.int32, (L, L), 1)
    if rev:
        tri, tri_t, dmask, dmask_t = (jj >= ii), (ii >= jj), (jj > ii), (ii > jj)
    else:
        tri, tri_t, dmask, dmask_t = (jj <= ii), (ii <= jj), (jj <= ii), (ii <= jj)
    cs = jnp.dot(tri.astype(f32), a, precision=HI, preferred_element_type=f32)
    cs_t = lax.dot_general(a, tri_t.astype(f32), TN, precision=HI, preferred_element_type=f32)
    tot = jnp.sum(a, axis=0, keepdims=True)
    c_col = jnp.stack([jnp.broadcast_to(cs[:, b:b + 1], (L, L)) for b in range(B)])
    c_row = jnp.stack([cs_t[b:b + 1, :] for b in range(B)])
    t_all = jnp.stack([tot[:, b:b + 1] for b in range(B)])
    dec = jnp.exp(jnp.where(dmask[None], c_col - c_row, -1e30))
    e_in, e_out = jnp.exp(c_col)[:, :, :P], jnp.exp(t_all - c_col)[:, :, :P]
    qk = _nt(q, k)
    k_t = _t(k)
    w = v * e_out
    if for_vjp:
        q_t = lax.stop_gradient(_t(q))
        qk_t = lax.stop_gradient(_nt(k, q))
        dec_t = lax.stop_gradient(jnp.exp(jnp.where(dmask_t[None], c_row - c_col, -1e30)))
        y = _masked_mm(qk, qk_t, dec, dec_t, v) + _mm_rt(q, q_t, h) * e_in
        hn = h * jnp.exp(t_all) + _mm_lt(lax.stop_gradient(k), k_t, w)
    else:
        y = _nn(qk * dec, v) + _nn(_lift(q, h), h) * e_in
        hn = h * jnp.exp(t_all) + _nn(_lift(k_t, w), w)
    return hn, y


def _scan_specs(gb, N, Hg, P, Ha, cm, qcol, kcol, vcol):
    qs = lambda col: pl.BlockSpec((CHUNK, gb * N), lambda g, c: (cm(c), col + g))
    vs = lambda col: pl.BlockSpec((CHUNK, gb * Hg * P), lambda g, c: (cm(c), col + g))
    as_ = pl.BlockSpec((1, CHUNK, Ha), lambda g, c: (g, cm(c), 0))
    hs = pl.BlockSpec((gb, 1, Hg, N, P), lambda g, c: (g, cm(c), 0, 0, 0))
    return qs(qcol), qs(kcol), vs(vcol), qs(0), vs(0), as_, hs


def _lanes(ref, n, width):
    return jnp.stack([ref[:, j * width:(j + 1) * width] for j in range(n)])


def _scan_fwd(q, k, v, a, *, G, N, Hg, P, qcol=0, kcol=0, vcol=0, rev, name, add_y=None):
    T, Ha, NC = q.shape[0], a.shape[2], q.shape[0] // CHUNK
    gb = SCAN_HEADS_PER_STEP // Hg
    cm = (lambda c: NC - 1 - c) if rev else (lambda c: c)
    qs, ks, vs, _, ys, as_, hs = _scan_specs(gb, N, Hg, P, Ha, cm, qcol, kcol, vcol)
    extra = [] if add_y is None else [add_y]

    def body(q_ref, k_ref, v_ref, a_ref, *rest):
        y_ref, hs_ref, h_scr = rest[len(extra):]

        @pl.when(pl.program_id(1) == 0)
        def _():
            h_scr[...] = jnp.zeros_like(h_scr)

        if Hg == 1:
            h = h_scr[:, 0]
            hs_ref[:, 0, 0] = h
            hn, y = _scan_step_heads(h, _lanes(q_ref, gb, N), _lanes(k_ref, gb, N), _lanes(v_ref, gb, P), a_ref[0], rev)
            h_scr[:, 0] = hn
        else:
            h = h_scr[0]
            hs_ref[0, 0] = h
            hn, y = _scan_step_heads(h, q_ref[...], k_ref[...], _lanes(v_ref, Hg, P), a_ref[0], rev)
            h_scr[0] = hn
        for j in range(gb * Hg):
            cols = slice(j * P, (j + 1) * P)
            y_ref[:, cols] = y[j] if add_y is None else y[j] + rest[0][:, cols]

    return pl.pallas_call(
        body, name=name, grid=(G // gb, NC), in_specs=[qs, ks, vs, as_] + [ys] * len(extra), out_specs=[ys, hs],
        out_shape=[S((T, G * Hg * P), f32), S((G, NC, Hg, N, P), f32)],
        scratch_shapes=[pltpu.VMEM((gb, Hg, N, P), f32)],
        compiler_params=_params(("parallel", "arbitrary")))(q, k, v, a, *extra)


def _scan_bwd(q, k, v, a, hsave, dy, *, G, N, Hg, P, qcol=0, kcol=0, vcol=0, rev, name, add_to=(None, None, None)):
    T, Ha, NC = q.shape[0], a.shape[2], q.shape[0] // CHUNK
    gb = SCAN_HEADS_PER_STEP // Hg
    cm = (lambda c: c) if rev else (lambda c: NC - 1 - c)
    qs, ks, vs, dqs, dvs, as_, hs = _scan_specs(gb, N, Hg, P, Ha, cm, qcol, kcol, vcol)
    extra = [(x, s) for x, s in zip(add_to, (dqs, dqs, dvs)) if x is not None]

    def body(q_ref, k_ref, v_ref, a_ref, hs_ref, dy_ref, *rest):
        dq_ref, dk_ref, dv_ref, da_ref, dh_scr = rest[len(extra):]
        prev = iter(rest[:len(extra)])
        pq, pk, pv = [next(prev) if x is not None else None for x in add_to]

        @pl.when(pl.program_id(1) == 0)
        def _():
            dh_scr[...] = jnp.zeros_like(dh_scr)

        if Hg == 1:
            _, vjp = jax.vjp(functools.partial(_scan_step_heads, rev=rev, for_vjp=True), hs_ref[:, 0, 0], _lanes(q_ref, gb, N),
                             _lanes(k_ref, gb, N), _lanes(v_ref, gb, P), a_ref[0])
            dh, dq, dk, dv, da = vjp((dh_scr[:, 0], _lanes(dy_ref, gb, P)))
            dh_scr[:, 0] = dh
            for j in range(gb):
                cols = slice(j * N, (j + 1) * N)
                dq_ref[:, cols] = dq[j] if pq is None else dq[j] + pq[:, cols]
                dk_ref[:, cols] = dk[j] if pk is None else dk[j] + pk[:, cols]
        else:
            _, vjp = jax.vjp(functools.partial(_scan_step_heads, rev=rev, for_vjp=True), hs_ref[0, 0], q_ref[...], k_ref[...],
                             _lanes(v_ref, Hg, P), a_ref[0])
            dh, dq, dk, dv, da = vjp((dh_scr[0], _lanes(dy_ref, Hg, P)))
            dh_scr[0] = dh
            dq_ref[...] = dq if pq is None else dq + pq[...]
            dk_ref[...] = dk if pk is None else dk + pk[...]
        for j in range(gb * Hg):
            cols = slice(j * P, (j + 1) * P)
            dv_ref[:, cols] = dv[j] if pv is None else dv[j] + pv[:, cols]
        da_ref[0] = da

    return pl.pallas_call(
        body, name=name, grid=(G // gb, NC), in_specs=[qs, ks, vs, as_, hs, dvs] + [s for _, s in extra],
        out_specs=[dqs, dqs, dvs, as_],
        out_shape=[S((T, G * N), f32), S((T, G * N), f32), S((T, G * Hg * P), f32), S(a.shape, f32)],
        scratch_shapes=[pltpu.VMEM((gb, Hg, N, P), f32)],
        compiler_params=_params(("parallel", "arbitrary")))(q, k, v, a, hsave, dy, *[x for x, _ in extra])


def _na_block_case(rb, nrb):
    return jnp.where(rb == 0, 0, jnp.where(rb == nrb - 1, 2, 1))


def _na_key_start(rb, rows):
    return pl.multiple_of(jnp.clip(rb * NA_QROWS - NA_WR // 2, 0, rows - NA_KROWS) * GRID_W, 256)


def _na_specs(T, nrb):
    nq, nk, wb = NA_QROWS * GRID_W, NA_KROWS * GRID_W, NA_PAIR * NADH
    qs = lambda col: pl.BlockSpec((nq, wb), lambda p, r: (r, col + p))
    fs = lambda col: pl.BlockSpec((T, wb), lambda p, r: (0, col + p))
    bs = pl.BlockSpec((NA_PAIR, 1, nq, nk), lambda p, r: (p, _na_block_case(r, nrb), 0, 0))
    ls = pl.BlockSpec((1, nq, NA_PAIR), lambda p, r: (p, r, 0))
    return qs, fs, bs, ls


def _na_fwd(q, k, v, bias, *, qcol, kcol, vcol, name):
    T = q.shape[0]
    rows = T // GRID_W
    nq, nk = NA_QROWS * GRID_W, NA_KROWS * GRID_W
    nrb = T // nq
    scale = NADH ** -0.5
    qs, fs, bs, ls = _na_specs(T, nrb)

    def body(q_ref, k_ref, v_ref, b_ref, o_ref, l_ref):
        ks = _na_key_start(pl.program_id(1), rows)
        for hh in range(NA_PAIR):
            sl = slice(hh * NADH, (hh + 1) * NADH)
            kw = k_ref[pl.ds(ks, nk), sl]
            vw = v_ref[pl.ds(ks, nk), sl]
            s = lax.dot_general(q_ref[:, sl], kw, NT, preferred_element_type=f32) * scale + b_ref[hh, 0]
            m = jnp.max(s, axis=1, keepdims=True)
            p = jnp.exp(s - m)
            l = jnp.sum(p, axis=1, keepdims=True)
            o_ref[:, sl] = jnp.dot(p.astype(bf16), vw, preferred_element_type=f32) / l
            l_ref[0, :, hh:hh + 1] = m + jnp.log(l)

    return pl.pallas_call(
        body, name=name, grid=(NAH // NA_PAIR, nrb), in_specs=[qs(qcol), fs(kcol), fs(vcol), bs],
        out_specs=[qs(0), ls], out_shape=[S((T, NAW), f32), S((NAH // NA_PAIR, T, NA_PAIR), f32)],
        compiler_params=_params(("parallel", "arbitrary")))(q, k, v, bias)


def _na_bwd(q, k, v, bias, o, lse, do, *, qcol, kcol, vcol, docol, name):
    T = q.shape[0]
    rows = T // GRID_W
    nq, nk = NA_QROWS * GRID_W, NA_KROWS * GRID_W
    nrb = T // nq
    scale = NADH ** -0.5
    qs, fs, bs, ls = _na_specs(T, nrb)

    def body(q_ref, k_ref, v_ref, b_ref, o_ref, l_ref, do_ref, dq_ref, dk_ref, dv_ref, db_ref):
        rb = pl.program_id(1)

        @pl.when(rb == 0)
        def _():
            dk_ref[...] = jnp.zeros_like(dk_ref)
            dv_ref[...] = jnp.zeros_like(dv_ref)

        ks = _na_key_start(rb, rows)
        first = (rb == 0) | (rb == 1) | (rb == nrb - 1)
        for hh in range(NA_PAIR):
            sl = slice(hh * NADH, (hh + 1) * NADH)
            qv = q_ref[:, sl]
            kw = k_ref[pl.ds(ks, nk), sl]
            vw = v_ref[pl.ds(ks, nk), sl]
            s = lax.dot_general(qv, kw, NT, preferred_element_type=f32) * scale + b_ref[hh, 0]
            p = jnp.exp(s - l_ref[0, :, hh:hh + 1])
            do_ = do_ref[:, sl]
            dob = do_.astype(bf16)
            dp = lax.dot_general(dob, vw, NT, preferred_element_type=f32)
            ds = p * (dp - jnp.sum(do_ * o_ref[:, sl], axis=1, keepdims=True))
            dsb = ds.astype(bf16)
            dq_ref[:, sl] = jnp.dot(dsb, kw, preferred_element_type=f32) * scale
            dk_ref[pl.ds(ks, nk), sl] += lax.dot_general(dsb, qv, TN, preferred_element_type=f32) * scale
            dv_ref[pl.ds(ks, nk), sl] += lax.dot_general(p.astype(bf16), dob, TN, preferred_element_type=f32)

            @pl.when(first)
            def _(hh=hh, ds=ds):
                db_ref[hh, 0] = ds

            @pl.when(jnp.logical_not(first))
            def _(hh=hh, ds=ds):
                db_ref[hh, 0] += ds

    return pl.pallas_call(
        body, name=name, grid=(NAH // NA_PAIR, nrb),
        in_specs=[qs(qcol), fs(kcol), fs(vcol), bs, qs(0), ls, qs(docol)],
        out_specs=[qs(0), fs(0), fs(0), bs],
        out_shape=[S((T, NAW), f32), S((T, NAW), f32), S((T, NAW), f32), S(bias.shape, f32)],
        compiler_params=_params(("parallel", "arbitrary")))(q, k, v, bias, o, lse, do)


def _na_col_tables():
    c = np.arange(GRID_W)[:, None]
    kc = np.arange(GRID_W)[None, :]
    cstart = np.clip(c - NA_WC // 2, 0, GRID_W - NA_WC)
    valid_c = (kc >= cstart) & (kc < cstart + NA_WC)
    dc = kc - c + NA_WC - 1
    E = (valid_c[:, :, None] & (dc[:, :, None] == np.arange(2 * NA_WC - 1)[None, None, :])).astype(np.float32)
    return E, np.where(valid_c, 0.0, -1e30).astype(np.float32)


def _na_row_offsets(rows):
    table = []
    for r0 in (0, NA_QROWS, rows - NA_QROWS):
        ks = int(np.clip(r0 - NA_WR // 2, 0, rows - NA_KROWS))
        case = []
        for ri in range(NA_QROWS):
            r = r0 + ri
            rs = int(np.clip(r - NA_WR // 2, 0, rows - NA_WR))
            case.append([ks + kri - r + NA_WR - 1 if rs <= ks + kri < rs + NA_WR else None for kri in range(NA_KROWS)])
        table.append(case)
    return table


def _na_col_bias(rpb):
    E, cmask = _na_col_tables()
    return jnp.einsum("hde,cke->hdck", rpb, E, precision=HI) + cmask


def _na_bias_build(r1, rows, *, name):
    H = r1.shape[0]
    offs = _na_row_offsets(rows)

    def body(r_ref, o_ref):
        outside = jnp.full((GRID_W, GRID_W), -1e30, f32)
        for z in range(3):
            for a in range(NA_QROWS):
                for b in range(NA_KROWS):
                    d = offs[z][a][b]
                    o_ref[0, z, a * GRID_W:(a + 1) * GRID_W, b * GRID_W:(b + 1) * GRID_W] = outside if d is None else r_ref[0, d]

    return pl.pallas_call(
        body, name=name, grid=(H,), in_specs=[pl.BlockSpec((1,) + r1.shape[1:], lambda h: (h, 0, 0, 0))],
        out_specs=pl.BlockSpec((1, 3, NA_QROWS * GRID_W, NA_KROWS * GRID_W), lambda h: (h, 0, 0, 0)),
        out_shape=S((H, 3, NA_QROWS * GRID_W, NA_KROWS * GRID_W), f32), compiler_params=_params(("parallel",)))(r1)


def _na_bias_fold(dbias, rows, *, name):
    H = dbias.shape[0]
    offs = _na_row_offsets(rows)
    nd = 2 * NA_WR - 1

    def body(d_ref, o_ref):
        acc = [None] * nd
        for z in range(3):
            for a in range(NA_QROWS):
                for b in range(NA_KROWS):
                    d = offs[z][a][b]
                    if d is not None:
                        t = d_ref[0, z, a * GRID_W:(a + 1) * GRID_W, b * GRID_W:(b + 1) * GRID_W]
                        acc[d] = t if acc[d] is None else acc[d] + t
        for d in range(nd):
            o_ref[0, d] = acc[d]

    return pl.pallas_call(
        body, name=name, grid=(H,), in_specs=[pl.BlockSpec((1,) + dbias.shape[1:], lambda h: (h, 0, 0, 0))],
        out_specs=pl.BlockSpec((1, nd, GRID_W, GRID_W), lambda h: (h, 0, 0, 0)),
        out_shape=S((H, nd, GRID_W, GRID_W), f32), compiler_params=_params(("parallel",)))(dbias)


def _conv_shifts(prev, cur, nxt, i, n_i, W):
    tm = cur.shape[0]
    prev = jnp.where(i > 0, prev, 0.0)
    nxt = jnp.where(i < n_i - 1, nxt, 0.0)
    ext = jnp.concatenate([prev, cur, nxt], axis=0)
    out = []
    for w in range(W):
        s = (W // 2 - w) % (tm + 16)
        out.append((ext if s == 0 else pltpu.roll(ext, s, axis=0))[8:8 + tm])
    return out


def _conv_act(u, mode):
    if mode == "silu":
        return jax.nn.silu(u)
    assert mode == "geglu"
    half = u.shape[1] // 2
    return jax.nn.gelu(u[:, :half], approximate=True) * u[:, half:]


def _conv_specs(T, tm, tc, xbase):
    r8 = tm // 8
    last = T // 8 - 1
    cur = pl.BlockSpec((tm, tc), lambda j, i: (i, xbase + j))
    prev = pl.BlockSpec((8, tc), lambda j, i: (jnp.maximum(i * r8 - 1, 0), xbase + j))
    nxt = pl.BlockSpec((8, tc), lambda j, i: (jnp.minimum((i + 1) * r8, last), xbase + j))
    return cur, prev, nxt


def _conv(x, w8, b, *, mode, W, name, C, xbase=0, tm=512, tc=512, out_dtype=f32):
    T = x.shape[0]
    NI, J = T // tm, C // tc
    tco = tc // 2 if mode == "geglu" else tc
    cur, prev, nxt = _conv_specs(T, tm, tc, xbase)

    def body(xc, xp, xn, w_ref, b_ref, o_ref):
        sh = _conv_shifts(xp[...].astype(f32), xc[...].astype(f32), xn[...].astype(f32), pl.program_id(1), NI, W)
        wv = w_ref[...]
        u = sh[0] * wv[0:1, :]
        for w in range(1, W):
            u = u + sh[w] * wv[w:w + 1, :]
        o_ref[...] = _conv_act(u + b_ref[...], mode).astype(o_ref.dtype)

    return pl.pallas_call(
        body, name=name, grid=(J, NI),
        in_specs=[cur, prev, nxt, pl.BlockSpec((8, tc), lambda j, i: (0, j)), pl.BlockSpec((1, tc), lambda j, i: (0, j))],
        out_specs=pl.BlockSpec((tm, tco), lambda j, i: (i, j)), out_shape=S((T, J * tco), out_dtype),
        compiler_params=_params(("parallel", "parallel")))(x, x, x, w8, b)


def _conv_bwd(x, w8, b, dact, *, mode, W, name, C, xbase=0, tm=512, tc=512):
    T = x.shape[0]
    NI, J = T // tm, C // tc
    tco = tc // 2 if mode == "geglu" else tc
    rows = tm + 16
    pad = W // 2
    cur, prev, nxt = _conv_specs(T, tm, tc, xbase)
    dcur, dprev, dnxt = _conv_specs(T, tm, tco, 0)

    def body(xc, xp, xn, w_ref, b_ref, dc, dp, dn, dx_ref, dw_ref, db_ref):
        i = pl.program_id(1)
        ext = jnp.concatenate([jnp.where(i > 0, xp[...], 0.0), xc[...], jnp.where(i < NI - 1, xn[...], 0.0)], axis=0)
        dext = jnp.concatenate([jnp.where(i > 0, dp[...], 0.0), dc[...], jnp.where(i < NI - 1, dn[...], 0.0)], axis=0)
        wv = w_ref[...]
        shift = lambda t, w: t if w == pad else pltpu.roll(t, (pad - w) % rows, axis=0)
        xs = [shift(ext, w) for w in range(W)]
        u = b_ref[...] + xs[0] * wv[0:1, :]
        for w in range(1, W):
            u = u + xs[w] * wv[w:w + 1, :]
        _, vjp = jax.vjp(functools.partial(_conv_act, mode=mode), u)
        du = vjp(dext.astype(f32))[0]
        dx = shift(du, 0)[8:8 + tm] * wv[W - 1:W, :]
        for w in range(1, W):
            dx = dx + shift(du, w)[8:8 + tm] * wv[W - 1 - w:W - w, :]
        dx_ref[...] = dx.astype(dx_ref.dtype)

        @pl.when(i == 0)
        def _():
            dw_ref[...] = jnp.zeros_like(dw_ref)
            db_ref[...] = jnp.zeros_like(db_ref)

        dum = du[8:8 + tm]
        db_ref[...] += jnp.sum(dum, axis=0, keepdims=True)
        for w in range(W):
            dw_ref[w:w + 1, :] += jnp.sum(dum * xs[w][8:8 + tm], axis=0, keepdims=True)

    return pl.pallas_call(
        body, name=name, grid=(J, NI),
        in_specs=[cur, prev, nxt, pl.BlockSpec((8, tc), lambda j, i: (0, j)), pl.BlockSpec((1, tc), lambda j, i: (0, j)),
                  dcur, dprev, dnxt],
        out_specs=[pl.BlockSpec((tm, tc), lambda j, i: (i, j)), pl.BlockSpec((8, tc), lambda j, i: (0, j)),
                   pl.BlockSpec((1, tc), lambda j, i: (0, j))],
        out_shape=[S((T, C), bf16), S((8, C), f32), S((1, C), f32)],
        compiler_params=_params(("parallel", "arbitrary")))(x, x, x, w8, b, dact, dact, dact)


def _pad8(w):
    return jnp.concatenate([w, jnp.zeros((8 - w.shape[0], w.shape[1]), w.dtype)], axis=0)


def _all_gather(arrs, *, name):
    n = len(arrs)

    def body(*refs):
        ins, outs = refs[:n], refs[n:2 * n]
        send_sems, recv_sems, loc_sems = refs[2 * n:]
        x, y, c = lax.axis_index("x"), lax.axis_index("y"), lax.axis_index("c")
        ident = lambda px, py, pc: 4 * px + 2 * py + pc
        me, sibling = (x, y, c), (x, y, 1 - c)
        chips = [(1 - x, y), (x, 1 - y), (1 - x, 1 - y)]

        def copy(a, k, block, to, src=None):
            slot = outs[a].at[ident(*block)]
            return pltpu.make_async_remote_copy(
                src_ref=slot if src is None else src, dst_ref=slot, send_sem=send_sems.at[a * 7 + k], recv_sem=recv_sems.at[a * 7 + k],
                device_id=to, device_id_type=pl.DeviceIdType.MESH)

        local = [pltpu.make_async_copy(ins[a], outs[a].at[ident(*me)], loc_sems.at[a]) for a in range(n)]
        for cp in local:
            cp.start()
        first = []
        for a in range(n):
            first.append(copy(a, 0, me, sibling, src=ins[a]))
            first += [copy(a, 1 + j, me, (*chip, c), src=ins[a]) for j, chip in enumerate(chips)]
        for cp in first:
            cp.start()
        passed = []
        for j, chip in enumerate(chips):
            for a in range(n):
                copy(a, 1 + j, (*chip, c), me).wait_recv()
                fwd = copy(a, 4 + j, (*chip, c), sibling)
                fwd.start()
                passed.append(fwd)
        for a in range(n):
            copy(a, 0, sibling, me).wait_recv()
            for j, chip in enumerate(chips):
                copy(a, 4 + j, (*chip, 1 - c), me).wait_recv()
        for cp in first + passed:
            cp.wait_send()
        for cp in local:
            cp.wait()

    any_spec = pl.BlockSpec(memory_space=pl.ANY)
    return pl.pallas_call(
        body, name=name, in_specs=[any_spec] * n, out_specs=[any_spec] * n,
        out_shape=[S((NDEV,) + a.shape, a.dtype) for a in arrs],
        scratch_shapes=[pltpu.SemaphoreType.DMA((7 * n,)), pltpu.SemaphoreType.DMA((7 * n,)), pltpu.SemaphoreType.DMA((n,))],
        )(*arrs)


NCHIP = NDEV // 2


def _to_sibling(arrs, *, name):
    n = len(arrs)
    ncopy = sum(NCHIP if ps else 1 for _, ps in arrs)

    def body(*refs):
        ins, outs = refs[:n], refs[n:2 * n]
        send_sems, recv_sems = refs[2 * n:]
        x, y, c = lax.axis_index("x"), lax.axis_index("y"), lax.axis_index("c")
        copies, idx = [], 0
        for a, (_, per_slot) in enumerate(arrs):
            pairs = [(ins[a].at[2 * q + (1 - c)], outs[a].at[q]) for q in range(NCHIP)] if per_slot else [(ins[a], outs[a])]
            for src, dst in pairs:
                copies.append(pltpu.make_async_remote_copy(
                    src_ref=src, dst_ref=dst, send_sem=send_sems.at[idx], recv_sem=recv_sems.at[idx],
                    device_id=(x, y, 1 - c), device_id_type=pl.DeviceIdType.MESH))
                idx += 1
        for cp in copies:
            cp.start()
        for cp in copies:
            cp.wait_recv()
        for cp in copies:
            cp.wait_send()

    any_spec = pl.BlockSpec(memory_space=pl.ANY)
    return pl.pallas_call(
        body, name=name, in_specs=[any_spec] * n, out_specs=[any_spec] * n,
        out_shape=[S((NCHIP,) + a.shape[1:] if ps else a.shape, a.dtype) for a, ps in arrs],
        scratch_shapes=[pltpu.SemaphoreType.DMA((ncopy,)), pltpu.SemaphoreType.DMA((ncopy,))])(*[a for a, _ in arrs])


def _add_partials(mine, theirs, *, per_slot, tr, name):
    R, C = mine.shape[-2:]

    def body(a_ref, b_ref, o_ref):
        a = a_ref[lax.axis_index("c")] if per_slot else a_ref[...]
        b = b_ref[0] if per_slot else b_ref[...]
        s = a.astype(f32) + b.astype(f32)
        if per_slot:
            o_ref[0] = s.astype(o_ref.dtype)
        else:
            o_ref[...] = s.astype(o_ref.dtype)

    if per_slot:
        grid = (NCHIP, R // tr)
        in_specs = [pl.BlockSpec((2, tr, C), lambda q, i: (q, i, 0)), pl.BlockSpec((1, tr, C), lambda q, i: (q, i, 0))]
        out_spec, out_shape = pl.BlockSpec((1, tr, C), lambda q, i: (q, i, 0)), S((NCHIP, R, C), mine.dtype)
    else:
        grid = (1, R // tr)
        in_specs = [pl.BlockSpec((tr, C), lambda q, i: (i, 0))] * 2
        out_spec, out_shape = pl.BlockSpec((tr, C), lambda q, i: (i, 0)), S((R, C), mine.dtype)
    return pl.pallas_call(body, name=name, grid=grid, in_specs=in_specs, out_specs=out_spec, out_shape=out_shape,
                          compiler_params=_params(("parallel", "parallel")))(mine, theirs)


def _to_chips(arrs, *, name):
    n = len(arrs)

    def body(*refs):
        ins, outs = refs[:n], refs[n:2 * n]
        send_sems, recv_sems, loc_sems = refs[2 * n:]
        x, y, c = lax.axis_index("x"), lax.axis_index("y"), lax.axis_index("c")
        my_q = 2 * x + y
        src = lambda a, q: ins[a].at[q] if arrs[a][1] else ins[a]
        local = [pltpu.make_async_copy(src(a, my_q), outs[a].at[my_q], loc_sems.at[a]) for a in range(n)]
        for cp in local:
            cp.start()
        sent = []
        for j, (px, py) in enumerate([(1 - x, y), (x, 1 - y), (1 - x, 1 - y)]):
            q = 2 * px + py
            for a in range(n):
                mk = lambda slot, a=a, j=j, q=q, dev=(px, py, c): pltpu.make_async_remote_copy(
                    src_ref=src(a, q), dst_ref=outs[a].at[slot], send_sem=send_sems.at[3 * a + j], recv_sem=recv_sems.at[3 * a + j],
                    device_id=dev, device_id_type=pl.DeviceIdType.MESH)
                mk(my_q).start()
                sent.append((mk, q))
        for mk, q in sent:
            mk(q).wait_recv()
        for mk, q in sent:
            mk(q).wait_send()
        for cp in local:
            cp.wait()

    any_spec = pl.BlockSpec(memory_space=pl.ANY)
    return pl.pallas_call(
        body, name=name, in_specs=[any_spec] * n, out_specs=[any_spec] * n,
        out_shape=[S(a.shape if ps else (NCHIP,) + a.shape, a.dtype) for a, ps in arrs],
        scratch_shapes=[pltpu.SemaphoreType.DMA((3 * n,)), pltpu.SemaphoreType.DMA((3 * n,)), pltpu.SemaphoreType.DMA((n,))],
        )(*[a for a, _ in arrs])


def _adamw(r, w, m, v, *, name, tr):
    M, C = w.shape
    nparts = r.shape[0]

    def body(r_ref, w_ref, m_ref, v_ref, g_ref, d_ref, nm_ref, nv_ref):
        g = r_ref[0].astype(f32)
        for s in range(1, nparts):
            g = g + r_ref[s].astype(f32)
        m_ = B1 * m_ref[...] + (1.0 - B1) * g
        v_ = B2 * v_ref[...] + (1.0 - B2) * jnp.square(g)
        m_hat = m_ / (1.0 - B1 ** STEP)
        v_hat = v_ / (1.0 - B2 ** STEP)
        g_ref[...] = g
        d_ref[...] = -LR * (m_hat / (jnp.sqrt(v_hat) + AEPS) + WD * w_ref[...])
        nm_ref[...] = m_
        nv_ref[...] = v_

    row = pl.BlockSpec((tr, C), lambda i: (i, 0))
    return pl.pallas_call(
        body, name=name, grid=(M // tr,),
        in_specs=[pl.BlockSpec((nparts, tr, C), lambda i: (0, i, 0)), row, row, row],
        out_specs=[row] * 4, out_shape=[S((M, C), f32)] * 4, compiler_params=_params(("parallel",)))(r, w, m, v)


def _colmove(ins, in_slots, outs, moves, *, tk, name):
    R = ins[0].shape[1] if in_slots[0] else ins[0].shape[0]
    n_in = len(ins)

    def body(*refs):
        for ii, isl, ic, oi, osl, oc, w in moves:
            src, dst = refs[ii], refs[n_in + oi]
            val = src[:, ic:ic + w] if isl is None else src[isl, :, ic:ic + w]
            if osl is None:
                dst[:, oc:oc + w] = val.astype(dst.dtype)
            else:
                dst[osl, :, oc:oc + w] = val.astype(dst.dtype)

    def spec(is_slots, C):
        return pl.BlockSpec((NDEV, tk, C), lambda i: (0, i, 0)) if is_slots else pl.BlockSpec((tk, C), lambda i: (i, 0))

    return pl.pallas_call(
        body, name=name, grid=(R // tk,),
        in_specs=[spec(sl, a.shape[-1]) for a, sl in zip(ins, in_slots)],
        out_specs=[spec(sl, C) for sl, C, _ in outs],
        out_shape=[S((NDEV, R, C) if sl else (R, C), dt) for sl, C, dt in outs],
        compiler_params=_params(("parallel",)))(*ins)


def _col_pieces(n8, cuts, place):
    out = []
    for p in range(NDEV):
        lo, hi = p * n8, (p + 1) * n8
        edges = [lo] + [c for c in cuts if lo < c < hi] + [hi]
        for a, b in zip(edges[:-1], edges[1:]):
            out.append((p, a - lo) + place(a) + (b - a,))
    return out


def _place_plain(c):
    return (0, c)


def _place_ssd_in(c):
    return (0, c) if c < SSD_INNER + SSD_XBC else (1, c - (SSD_INNER + SSD_XBC))


def _place_ffn_up(c):
    h = FFN_TC // 2
    return (0, (c // h) * FFN_TC + c % h) if c < FFN else (0, ((c - FFN) // h) * FFN_TC + h + (c - FFN) % h)


_COL_LAYOUTS = {
    "ab_w_in": ([], _place_plain, [4 * RW + 3 * NAW]),
    "c_w_in": ([SSD_INNER + SSD_XBC], _place_ssd_in, [SSD_INNER + SSD_XBC, 2 * SSD_H]),
    "ffn_w_up": (list(range(FFN_TC // 2, 2 * FFN, FFN_TC // 2)), _place_ffn_up, [2 * FFN]),
}


def _cols_from_slots(g, which, *, name):
    cuts, place, widths = _COL_LAYOUTS[which]
    moves = [(0, p, sc, mi, None, mc, w) for p, sc, mi, mc, w in _col_pieces(g.shape[2], cuts, place)]
    return _colmove([g], [True], [(False, w, g.dtype) for w in widths], moves, tk=256, name=name)


def _cols_to_slots(mats, which, dtype, *, name):
    cuts, place, widths = _COL_LAYOUTS[which]
    n8 = sum(widths) // NDEV
    moves = [(mi, None, mc, 0, p, sc, w) for p, sc, mi, mc, w in _col_pieces(n8, cuts, place)]
    return _colmove(list(mats), [False] * len(mats), [(True, n8, dtype)], moves, tk=256, name=name)[0]


def _pack(parts, dtype, row_mult):
    flat = jnp.concatenate([p.reshape(-1).astype(dtype) for p in parts])
    rows = -(-flat.shape[0] // LANES)
    rows = -(-rows // row_mult) * row_mult
    return jnp.pad(flat, (0, rows * LANES - flat.shape[0])).reshape(rows, LANES)


def _unpack(buf, shapes, lead=()):
    flat = buf.reshape(lead + (-1,))
    out, off = [], 0
    for shp in shapes:
        n = int(np.prod(shp))
        out.append(flat[..., off:off + n].reshape(lead + tuple(shp)))
        off += n
    return out


def _to_slots(full, ax):
    shp = full.shape
    return jnp.moveaxis(full.reshape(shp[:ax] + (NDEV, shp[ax] // NDEV) + shp[ax + 1:]), ax, 0)


def _from_slots(g, ax):
    t = jnp.moveaxis(g, 0, ax)
    shp = t.shape
    return t.reshape(shp[:ax] + (shp[ax] * shp[ax + 1],) + shp[ax + 2:])


def _ffn_perm(a):
    lead = a.shape[:-1]
    h = FFN_TC // 2
    return jnp.swapaxes(a.reshape(lead + (2, FFN // h, h)), -3, -2).reshape(lead + (2 * FFN,))


def _ffn_unperm(a):
    lead = a.shape[:-1]
    h = FFN_TC // 2
    return jnp.swapaxes(a.reshape(lead + (FFN // h, 2, h)), -3, -2).reshape(lead + (2 * FFN,))


def _rope_tables(T):
    half = RDH // 2
    inv = 1.0 / (ROPE_BASE ** (jnp.arange(half, dtype=f32) / half))
    ang = jnp.arange(T, dtype=f32)[:, None] * inv[None, :]
    cos, sin = jnp.cos(ang), jnp.sin(ang)
    cos_t = jnp.tile(jnp.concatenate([cos, cos], axis=1), (1, RH))
    sin_t = jnp.tile(jnp.concatenate([-sin, sin], axis=1), (1, RH))
    return cos_t, sin_t


def _group_avg():
    g = np.arange(RW) // RDH
    return jnp.asarray((g[:, None] == g[None, :]).astype(np.float32) / RDH)


def _head_expand():
    hd = np.arange(SSD_INNER) // SSD_HD
    rows = np.arange(2 * SSD_H)
    ex0 = (rows[:, None] == hd[None, :]).astype(np.float32)
    ex1 = (rows[:, None] == SSD_H + hd[None, :]).astype(np.float32)
    return jnp.asarray(ex0), jnp.asarray(ex1)


def kernel(x, norm_mix_pre, norm_mix_post, norm_ffn_pre, norm_ffn_post, ab_w_in, ab_ret_decay_logit, ab_ret_gn_g, ab_na_rpb, ab_w_out, c_w_in, c_conv_w, c_conv_b, c_dt_bias, c_a_log, c_d_skip, c_norm_g, c_w_out, ffn_w_up, ffn_conv_w, ffn_conv_b, ffn_w_down, loss_target, m_norm_mix_pre, m_norm_mix_post, m_norm_ffn_pre, m_norm_ffn_post, m_ab_w_in, m_ab_ret_decay_logit, m_ab_ret_gn_g, m_ab_na_rpb, m_ab_w_out, m_c_w_in, m_c_conv_w, m_c_conv_b, m_c_dt_bias, m_c_a_log, m_c_d_skip, m_c_norm_g, m_c_w_out, m_ffn_w_up, m_ffn_conv_w, m_ffn_conv_b, m_ffn_w_down, v_norm_mix_pre, v_norm_mix_post, v_norm_ffn_pre, v_norm_ffn_post, v_ab_w_in, v_ab_ret_decay_logit, v_ab_ret_gn_g, v_ab_na_rpb, v_ab_w_out, v_c_w_in, v_c_conv_w, v_c_conv_b, v_c_dt_bias, v_c_a_log, v_c_d_skip, v_c_norm_g, v_c_w_out, v_ffn_w_up, v_ffn_conv_w, v_ffn_conv_b, v_ffn_w_down):
    W = dict(norm_mix_pre=norm_mix_pre, norm_mix_post=norm_mix_post, norm_ffn_pre=norm_ffn_pre, norm_ffn_post=norm_ffn_post, ab_w_in=ab_w_in, ab_ret_decay_logit=ab_ret_decay_logit, ab_ret_gn_g=ab_ret_gn_g, ab_na_rpb=ab_na_rpb, ab_w_out=ab_w_out, c_w_in=c_w_in, c_conv_w=c_conv_w, c_conv_b=c_conv_b, c_dt_bias=c_dt_bias, c_a_log=c_a_log, c_d_skip=c_d_skip, c_norm_g=c_norm_g, c_w_out=c_w_out, ffn_w_up=ffn_w_up, ffn_conv_w=ffn_conv_w, ffn_conv_b=ffn_conv_b, ffn_w_down=ffn_w_down)
    Mo = dict(norm_mix_pre=m_norm_mix_pre, norm_mix_post=m_norm_mix_post, norm_ffn_pre=m_norm_ffn_pre, norm_ffn_post=m_norm_ffn_post, ab_w_in=m_ab_w_in, ab_ret_decay_logit=m_ab_ret_decay_logit, ab_ret_gn_g=m_ab_ret_gn_g, ab_na_rpb=m_ab_na_rpb, ab_w_out=m_ab_w_out, c_w_in=m_c_w_in, c_conv_w=m_c_conv_w, c_conv_b=m_c_conv_b, c_dt_bias=m_c_dt_bias, c_a_log=m_c_a_log, c_d_skip=m_c_d_skip, c_norm_g=m_c_norm_g, c_w_out=m_c_w_out, ffn_w_up=m_ffn_w_up, ffn_conv_w=m_ffn_conv_w, ffn_conv_b=m_ffn_conv_b, ffn_w_down=m_ffn_w_down)
    Vo = dict(norm_mix_pre=v_norm_mix_pre, norm_mix_post=v_norm_mix_post, norm_ffn_pre=v_norm_ffn_pre, norm_ffn_post=v_norm_ffn_post, ab_w_in=v_ab_w_in, ab_ret_decay_logit=v_ab_ret_decay_logit, ab_ret_gn_g=v_ab_ret_gn_g, ab_na_rpb=v_ab_na_rpb, ab_w_out=v_ab_w_out, c_w_in=v_c_w_in, c_conv_w=v_c_conv_w, c_conv_b=v_c_conv_b, c_dt_bias=v_c_dt_bias, c_a_log=v_c_a_log, c_d_skip=v_c_d_skip, c_norm_g=v_c_norm_g, c_w_out=v_c_w_out, ffn_w_up=v_ffn_w_up, ffn_conv_w=v_ffn_conv_w, ffn_conv_b=v_ffn_conv_b, ffn_w_down=v_ffn_w_down)
    return _train_step(x[0], loss_target[0], W, Mo, Vo)


def _train_step(x, tgt, W, Mo, Vo):
    T = x.shape[0]
    rows = T // GRID_W

    col = lambda d, n, dt: d[n].reshape(-1, d[n].shape[-1]).astype(dt)
    rows_of = lambda d, dt: jnp.concatenate([col(d, n, dt) for n in ROW_SHARDED], axis=0)
    small = _pack([W[n] for n, _ in SHARDED[N_BIG:]], f32, 8)
    gat = _all_gather([col(W, n, bf16) for n in COL_SHARDED] + [rows_of(W, bf16), small], name="gather_weights")
    per_layer = lambda m: m.reshape(-1, D, m.shape[-1])
    w_ab_in = per_layer(_cols_from_slots(gat[0], "ab_w_in", name="cols_ab_w_in")[0])
    w_zx, w_dt = [per_layer(m) for m in _cols_from_slots(gat[1], "c_w_in", name="cols_c_w_in")]
    w_up = per_layer(_cols_from_slots(gat[2], "ffn_w_up", name="cols_ffn_w_up")[0])
    full, off = {}, 0
    for n in ROW_SHARDED:
        L, r = W[n].shape[0], W[n].shape[1]
        full[n] = jnp.swapaxes(gat[3][:, off:off + L * r].reshape(NDEV, L, r, D), 0, 1).reshape(L, NDEV * r, D)
        off += L * r
    gs = _unpack(gat[4], [W[n].shape for n, _ in SHARDED[N_BIG:]], (NDEV,))
    full.update({n: _from_slots(g, ax) for (n, ax), g in zip(SHARDED[N_BIG:], gs)})
    w_ab_out, w_c_out, w_down = full["ab_w_out"], full["c_w_out"], full["ffn_w_down"]
    c_cw8 = [_pad8(full["c_conv_w"][i]) for i in range(2)]
    c_cb = [full["c_conv_b"][i][None] for i in range(2)]
    c_ng = [full["c_norm_g"][i][None] for i in range(2)]
    f_cw8 = [_pad8(_ffn_perm(full["ffn_conv_w"][l])) for l in range(DEPTH)]
    f_cb = [_ffn_perm(W["ffn_conv_b"][l])[None] for l in range(DEPTH)]

    g1 = [W["norm_mix_pre"][l][None] for l in range(DEPTH)]
    g2 = [W["norm_mix_post"][l][None] for l in range(DEPTH)]
    g3 = [W["norm_ffn_pre"][l][None] for l in range(DEPTH)]
    g4 = [W["norm_ffn_post"][l][None] for l in range(DEPTH)]
    cos_t, sin_t = _rope_tables(T)
    gavg = _group_avg()
    ex0, ex1 = _head_expand()

    def log_gamma(logit):
        return -jax.nn.softplus(-logit)

    def ret_decays(lg):
        return [jnp.broadcast_to(lg[d][None, None, :], (1, T, RH)) for d in range(2)]

    saved = []
    xs_ = x
    hn = _rowwise("norm_first", _f_first, [(x, D, 0)], [], [(g1[0], D, 0)], [], [(D, bf16)], tm=256)[0]
    for l in range(DEPTH):
        i = l // 2
        sv = dict(x=xs_, hn=hn)
        if l % 2 == 0:
            proj = _mm_nn(hn, w_ab_in[i], name=f"ab_in_{l}")
            qr, kr = _rowwise(f"ret_prep_{l}", _f_rprep, [(proj, RW, 0), (proj, RW, 1)], [(cos_t, RW, 0), (sin_t, RW, 0)], [], [],
                              [(RW, f32), (RW, f32)], tm=256)
            lg, lg_vjp = jax.vjp(log_gamma, W["ab_ret_decay_logit"][i])
            a_f, a_b = ret_decays(lg)
            rscan = dict(G=RH, N=RDH, Hg=1, P=RDH, vcol=2)
            yf_t, hsf = _scan_fwd(qr, kr, proj, a_f, rev=False, name=f"ret_scan_f_{l}", **rscan)
            y_t, hsb = _scan_fwd(qr, kr, proj, a_b, rev=True, name=f"ret_scan_b_{l}", add_y=yf_t, **rscan)
            gn = W["ab_ret_gn_g"][i][None]
            ret = _rowwise(f"ret_post_{l}", _f_rpost, [(y_t, RW, 0), (proj, RW, 3)], [], [(gn, RW, 0)], [gavg],
                           [(RW, bf16)], tm=256)[0]
            nqkv = proj[:, 4 * RW:].astype(bf16)
            ncols = dict(qcol=0, kcol=NAW // 128, vcol=2 * NAW // 128)
            r1, bias_vjp = jax.vjp(_na_col_bias, W["ab_na_rpb"][i])
            bias = _na_bias_build(r1, rows, name=f"na_bias_{l}")
            na_o, na_l = _na_fwd(nqkv, nqkv, nqkv, bias, name=f"na_fwd_{l}", **ncols)
            cat = jnp.concatenate([ret, na_o.astype(bf16)], axis=1)
            mo = _mm_nn(cat, w_ab_out[i], name=f"ab_out_{l}")
            sv.update(proj=proj, qr=qr, kr=kr, a_f=a_f, a_b=a_b, hsf=hsf, hsb=hsb, y_t=y_t, gn=gn, rscan=rscan,
                      nqkv=nqkv, ncols=ncols, bias=bias, bias_vjp=bias_vjp, lg_vjp=lg_vjp, na_o=na_o, na_l=na_l, cat=cat)
        else:
            zx = _mm_nn(hn, w_zx[i], name=f"c_in_{l}")
            dtr = _mm_nn(hn, w_dt[i], name=f"c_in_dt_{l}")
            xa = _conv(zx, c_cw8[i], c_cb[i], mode="silu", W=SSD_CONV, name=f"c_conv_{l}", C=SSD_XBC, xbase=SSD_INNER // 512)
            dtb, alog = W["c_dt_bias"][i].reshape(1, 2 * SSD_H), W["c_a_log"][i].reshape(1, 2 * SSD_H)
            vf, vb, la = _rowwise(f"ssd_prep_{l}", _f_sprep, [(xa, SSD_INNER, 0), (dtr, 2 * SSD_H, 0)], [],
                                  [(dtb, 2 * SSD_H, 0), (alog, 2 * SSD_H, 0)], [ex0, ex1],
                                  [(SSD_INNER, f32), (SSD_INNER, f32), (2 * SSD_H, f32)], tm=128)
            a_f = la[:, :SSD_H].reshape(T, SSD_G, SSD_HPG).transpose(1, 0, 2)
            a_b = la[:, SSD_H:].reshape(T, SSD_G, SSD_HPG).transpose(1, 0, 2)
            sscan = dict(G=SSD_G, N=SSD_N, Hg=SSD_HPG, P=SSD_HD, qcol=(SSD_INNER + SSD_G * SSD_N) // SSD_N, kcol=SSD_INNER // SSD_N)
            yf_t, hsf = _scan_fwd(xa, xa, vf, a_f, rev=False, name=f"ssd_scan_f_{l}", **sscan)
            y_t, hsb = _scan_fwd(xa, xa, vb, a_b, rev=True, name=f"ssd_scan_b_{l}", add_y=yf_t, **sscan)
            dsk = jnp.repeat(W["c_d_skip"][i], SSD_HD)[None]
            yo = _rowwise(f"ssd_post_{l}", _f_spost, [(y_t, 512, 0), (xa, 512, 0), (zx, 512, 0)], [],
                          [(dsk, 512, 0), (c_ng[i], 512, 0)], [], [(512, bf16)], tm=256, J=SSD_G)[0]
            mo = _mm_nn(yo, w_c_out[i], name=f"c_out_{l}")
            sv.update(zx=zx, dtr=dtr, xa=xa, dtb=dtb, alog=alog, a_f=a_f, a_b=a_b, vf=vf, vb=vb, sscan=sscan,
                      hsf=hsf, hsb=hsb, y_t=y_t, dsk=dsk, yo=yo)
        x1, hf = _rowwise(f"norm_mid_{l}", _f_mid, [(xs_, D, 0), (mo, D, 0)], [], [(g2[l], D, 0), (g3[l], D, 0)], [],
                          [(D, f32), (D, bf16)], tm=256)
        pre = _mm_nn(hf, w_up[l], name=f"ffn_up_{l}")
        act = _conv(pre, f_cw8[l], f_cb[l], mode="geglu", W=FFN_CONV, name=f"ffn_conv_{l}", C=2 * FFN, tc=FFN_TC, out_dtype=bf16)
        fo = _mm_nn(act, w_down[l], name=f"ffn_down_{l}")
        sv.update(mo=mo, x1=x1, hf=hf, pre=pre, act=act, fo=fo)
        if l < DEPTH - 1:
            xs_, hn = _rowwise(f"norm_end_{l}", _f_end, [(x1, D, 0), (fo, D, 0)], [], [(g4[l], D, 0), (g1[l + 1], D, 0)], [],
                               [(D, f32), (D, bf16)], tm=256)
        else:
            xs_ = _rowwise(f"norm_end_{l}", _f_last, [(x1, D, 0), (fo, D, 0)], [], [(g4[l], D, 0)], [], [(D, f32)], tm=256)[0]
        saved.append(sv)

    dx, lpart = _loss_call(xs_, tgt)
    loss = lax.psum(lpart[0, 0], ("x", "y", "c"))

    G = {n: [None] * W[n].shape[0] for n in WEIGHTS}
    dhn = None
    for l in reversed(range(DEPTH)):
        i = l // 2
        sv = saved[l]
        if l == DEPTH - 1:
            (dx1, dfo), (dg4,) = _rowwise_bwd(f"norm_end_bwd_{l}", _f_last, [(sv["x1"], D, 0), (sv["fo"], D, 0)], [],
                                              [(g4[l], D, 0)], [], [(dx, D, 0)], [f32, bf16], tm=256)
        else:
            (dx1, dfo), (dg4, dg1n) = _rowwise_bwd(f"norm_end_bwd_{l}", _f_end, [(sv["x1"], D, 0), (sv["fo"], D, 0)], [],
                                                   [(g4[l], D, 0), (g1[l + 1], D, 0)], [], [(dx, D, 0), (dhn, D, 0)],
                                                   [f32, bf16], tm=256)
            G["norm_mix_pre"][l + 1] = dg1n[0]
        G["norm_ffn_post"][l] = dg4[0]
        dact = _mm_nt(dfo, w_down[l], name=f"ffn_down_dx_{l}")
        G["ffn_w_down"][l] = _mm_tn(sv["act"], dfo, name=f"ffn_down_dw_{l}")
        dpre, dfw, dfb = _conv_bwd(sv["pre"], f_cw8[l], f_cb[l], dact, mode="geglu", W=FFN_CONV, name=f"ffn_conv_bwd_{l}",
                                   C=2 * FFN, tc=FFN_TC)
        dhf = _mm_nt(dpre, w_up[l], name=f"ffn_up_dx_{l}")
        G["ffn_w_up"][l] = _cols_to_slots([_mm_tn(sv["hf"], dpre, name=f"ffn_up_dw_{l}")], "ffn_w_up", bf16, name=f"slots_ffn_up_{l}")
        G["ffn_conv_w"][l] = _ffn_unperm(dfw[:FFN_CONV])
        G["ffn_conv_b"][l] = _ffn_unperm(dfb[0])
        (dxl, dmo), (dg2, dg3) = _rowwise_bwd(f"norm_mid_bwd_{l}", _f_mid, [(sv["x"], D, 0), (sv["mo"], D, 0)], [],
                                              [(g2[l], D, 0), (g3[l], D, 0)], [], [(dx1, D, 0), (dhf, D, 0)], [f32, bf16], tm=256)
        G["norm_mix_post"][l] = dg2[0]
        G["norm_ffn_pre"][l] = dg3[0]
        if l % 2 == 0:
            dcat = _mm_nt(dmo, w_ab_out[i], name=f"ab_out_dx_{l}")
            G["ab_w_out"][i] = _mm_tn(sv["cat"], dmo, name=f"ab_out_dw_{l}")
            (dy, drg), (dgn,) = _rowwise_bwd(
                f"ret_post_bwd_{l}", _f_rpost, [(sv["y_t"], RW, 0), (sv["proj"], RW, 3)], [],
                [(sv["gn"], RW, 0)], [gavg], [(dcat, RW, 0)], [f32, bf16], tm=256)
            G["ab_ret_gn_g"][i] = dgn[0]
            dqf, dkf, dvf, daf = _scan_bwd(sv["qr"], sv["kr"], sv["proj"], sv["a_f"], sv["hsf"], dy, rev=False,
                                           name=f"ret_scan_f_bwd_{l}", **sv["rscan"])
            dq_t, dk_t, dv_t, dab = _scan_bwd(sv["qr"], sv["kr"], sv["proj"], sv["a_b"], sv["hsb"], dy, rev=True,
                                              name=f"ret_scan_b_bwd_{l}", add_to=(dqf, dkf, dvf), **sv["rscan"])
            drv = dv_t.astype(bf16)
            (drq, drk), _ = _rowwise_bwd(f"ret_prep_bwd_{l}", _f_rprep, [(sv["proj"], RW, 0), (sv["proj"], RW, 1)],
                                         [(cos_t, RW, 0), (sin_t, RW, 0)], [], [], [(dq_t, RW, 0), (dk_t, RW, 0)], [bf16, bf16], tm=256)
            da_cols = jnp.concatenate([daf[0], dab[0]], axis=1)
            dlg = _colsum(da_cols, name=f"ret_decay_sum_{l}").reshape(2, RH)
            G["ab_ret_decay_logit"][i] = sv["lg_vjp"](dlg)[0]
            dnq, dnk, dnv, dbias = _na_bwd(sv["nqkv"], sv["nqkv"], sv["nqkv"], sv["bias"], sv["na_o"], sv["na_l"], dcat,
                                           docol=RW // 128, name=f"na_bwd_{l}", **sv["ncols"])
            G["ab_na_rpb"][i] = sv["bias_vjp"](_na_bias_fold(dbias, rows, name=f"na_bias_fold_{l}"))[0]
            dproj = jnp.concatenate([drq, drk, drv, drg] + [t.astype(bf16) for t in (dnq, dnk, dnv)], axis=1)
            dhn = _mm_nt(dproj, w_ab_in[i], name=f"ab_in_dx_{l}")
            G["ab_w_in"][i] = _cols_to_slots([_mm_tn(sv["hn"], dproj, name=f"ab_in_dw_{l}")], "ab_w_in", bf16, name=f"slots_ab_in_{l}")
        else:
            dyo = _mm_nt(dmo, w_c_out[i], name=f"c_out_dx_{l}")
            G["c_w_out"][i] = _mm_tn(sv["yo"], dmo, name=f"c_out_dw_{l}")
            (dy, dxs1, dz), (ddsk, dng) = _rowwise_bwd(
                f"ssd_post_bwd_{l}", _f_spost, [(sv["y_t"], 512, 0), (sv["xa"], 512, 0), (sv["zx"], 512, 0)],
                [], [(sv["dsk"], 512, 0), (c_ng[i], 512, 0)], [], [(dyo, 512, 0)], [f32, f32, bf16], tm=256, J=SSD_G)
            G["c_d_skip"][i] = ddsk.reshape(SSD_H, SSD_HD).sum(axis=1)
            G["c_norm_g"][i] = dng[0]
            dqf, dkf, dvf, daf = _scan_bwd(sv["xa"], sv["xa"], sv["vf"], sv["a_f"], sv["hsf"], dy, rev=False,
                                           name=f"ssd_scan_f_bwd_{l}", **sv["sscan"])
            dq_t, dk_t, dvb, dab = _scan_bwd(sv["xa"], sv["xa"], sv["vb"], sv["a_b"], sv["hsb"], dy, rev=True,
                                             name=f"ssd_scan_b_bwd_{l}", add_to=(dqf, dkf, None), **sv["sscan"])
            dla = jnp.concatenate([daf.transpose(1, 0, 2).reshape(T, SSD_H), dab.transpose(1, 0, 2).reshape(T, SSD_H)], axis=1)
            (dxs, ddtr), (ddtb, dalog) = _rowwise_bwd(
                f"ssd_prep_bwd_{l}", _f_sprep_bwd, [(sv["xa"], SSD_INNER, 0), (sv["dtr"], 2 * SSD_H, 0)], [],
                [(sv["dtb"], 2 * SSD_H, 0), (sv["alog"], 2 * SSD_H, 0)], [ex0, ex1],
                [(dvf, SSD_INNER, 0), (dvb, SSD_INNER, 0), (dla, 2 * SSD_H, 0), (dxs1, SSD_INNER, 0)], [f32, bf16], tm=128)
            G["c_dt_bias"][i] = ddtb.reshape(2, SSD_H)
            G["c_a_log"][i] = dalog.reshape(2, SSD_H)
            dxa = jnp.concatenate([dxs, dk_t, dq_t], axis=1)
            dxbc, dcw, dcb = _conv_bwd(sv["zx"], c_cw8[i], c_cb[i], dxa, mode="silu", W=SSD_CONV, name=f"c_conv_bwd_{l}",
                                       C=SSD_XBC, xbase=SSD_INNER // 512)
            G["c_conv_w"][i] = dcw[:SSD_CONV]
            G["c_conv_b"][i] = dcb[0]
            dzx = jnp.concatenate([dz, dxbc], axis=1)
            t1 = _mm_nt(ddtr, w_dt[i], name=f"c_in_dt_dx_{l}")
            dhn = _mm_nt(dzx, w_zx[i], add=t1, name=f"c_in_dx_{l}")
            G["c_w_in"][i] = _cols_to_slots([_mm_tn(sv["hn"], dzx, name=f"c_in_dw_{l}"), _mm_tn(sv["hn"], ddtr, name=f"c_in_dt_dw_{l}")],
                                            "c_w_in", bf16, name=f"slots_c_in_{l}")
        dx = dxl
    (grad_x,), (dg1,) = _rowwise_bwd("norm_first_bwd", _f_first_bwd, [(x, D, 0)], [], [(g1[0], D, 0)], [], [(dx, D, 0), (dhn, D, 0)],
                                     [f32], tm=256)
    G["norm_mix_pre"][0] = dg1[0]

    small_names = [n for n, _ in SHARDED[N_BIG:]]
    col_slots = [jnp.concatenate(G[n], axis=1) for n in COL_SHARDED]
    row_slots = jnp.concatenate([g.reshape(NDEV, -1, D).astype(bf16) for n in ROW_SHARDED for g in G[n]], axis=1)
    small_slots = _pack_slots([_to_slots(jnp.stack(G[n]), ax) for n, ax in SHARDED[N_BIG:]], 8)
    ar = _pack([jnp.stack(G[n]) for n in REPLICATED], f32, 8)
    parts = [(a, True) for a in col_slots + [row_slots, small_slots]] + [(ar, False)]
    from_sib = _to_sibling(parts, name="grads_to_sibling")
    tiles = [256, 256, 256, 64, small_slots.shape[1], ar.shape[0]]
    chip = [_add_partials(a, b, per_slot=ps, tr=t, name=f"grads_add_{j}")
            for j, ((a, ps), b, t) in enumerate(zip(parts, from_sib, tiles))]
    exch = _to_chips([(a, ps) for a, (_, ps) in zip(chip, parts)], name="grads_to_chips")
    pk = lambda d, names: _pack([d[n] for n in names], f32, 8)
    upd = [_adamw(exch[j], col(W, n, f32), col(Mo, n, f32), col(Vo, n, f32), name=f"adamw_{n}", tr=256)
           for j, n in enumerate(COL_SHARDED)]
    upd_rows = _adamw(exch[3], rows_of(W, f32), rows_of(Mo, f32), rows_of(Vo, f32), name="adamw_rows", tr=64)
    upd_small = _adamw(exch[4], pk(W, small_names), pk(Mo, small_names), pk(Vo, small_names), name="adamw_small",
                       tr=small_slots.shape[1])
    upd_rep = _adamw(exch[5], pk(W, REPLICATED), pk(Mo, REPLICATED), pk(Vo, REPLICATED), name="adamw_replicated", tr=ar.shape[0])
    res = []
    for k in range(4):
        d = {n: upd[j][k].reshape(W[n].shape) for j, n in enumerate(COL_SHARDED)}
        off = 0
        for n in ROW_SHARDED:
            cnt = W[n].shape[0] * W[n].shape[1]
            d[n] = upd_rows[k][off:off + cnt].reshape(W[n].shape)
            off += cnt
        d.update(zip(small_names, _unpack(upd_small[k], [W[n].shape for n in small_names])))
        d.update(zip(REPLICATED, _unpack(upd_rep[k], [W[n].shape for n in REPLICATED])))
        res.append(d)
    outs = [loss, grad_x[None]]
    for k in range(4):
        outs += [res[k][n] for n in WEIGHTS]
    return tuple(outs)


def _pack_slots(slot_arrays, row_mult):
    flat = jnp.concatenate([a.reshape(NDEV, -1) for a in slot_arrays], axis=1)
    rows = -(-flat.shape[1] // LANES)
    rows = -(-rows // row_mult) * row_mult
    return jnp.pad(flat, ((0, 0), (0, rows * LANES - flat.shape[1]))).reshape(NDEV, rows, LANES)
```

```python
import functools
import numpy as np
import jax
import jax.numpy as jnp
from jax import lax
from jax.experimental import pallas as pl
from jax.experimental.pallas import tpu as pltpu

f32, bf16 = jnp.float32, jnp.bfloat16
S = jax.ShapeDtypeStruct
HI = lax.Precision.HIGHEST

D = 1024
DEPTH = 4
GRID_W = 64
CHUNK = 128
EPS = 1e-6
RH, RDH, RW = 8, 64, 512
NAH, NADH, NAW = 8, 64, 512
NA_WR, NA_WC = 8, 16
NA_QROWS = 8
NA_KROWS = 16
NA_PAIR = 2
SSD_INNER, SSD_HD, SSD_H, SSD_G, SSD_HPG, SSD_N, SSD_CONV = 2048, 64, 32, 4, 8, 128, 5
SSD_XBC = SSD_INNER + 2 * SSD_G * SSD_N
FFN, FFN_CONV = 2816, 3
FFN_TC = 512
SCAN_HEADS_PER_STEP = 8
ROPE_BASE = 10000.0
LR, B1, B2, AEPS, WD, STEP = 0.001, 0.9, 0.999, 1e-08, 0.01, 10
NDEV = 8
LANES = 128
VMEM_LIMIT = 56 * 1024 * 1024
MM_BLOCK_BYTES = 6 * 1024 * 1024

NT = (((1,), (1,)), ((), ()))
TN = (((0,), (0,)), ((), ()))

SHARDED = [("ab_w_in", 2), ("ab_w_out", 1), ("c_w_in", 2), ("c_w_out", 1), ("ffn_w_up", 2), ("ffn_w_down", 1),
           ("c_conv_w", 2), ("c_conv_b", 1), ("c_norm_g", 1), ("ffn_conv_w", 2)]
N_BIG = 6
COL_SHARDED = ["ab_w_in", "c_w_in", "ffn_w_up"]
ROW_SHARDED = ["ab_w_out", "c_w_out", "ffn_w_down"]
REPLICATED = ["norm_mix_pre", "norm_mix_post", "norm_ffn_pre", "norm_ffn_post", "ab_ret_decay_logit", "ab_ret_gn_g",
              "ab_na_rpb", "c_dt_bias", "c_a_log", "c_d_skip", "ffn_conv_b"]
WEIGHTS = ["norm_mix_pre", "norm_mix_post", "norm_ffn_pre", "norm_ffn_post", "ab_w_in", "ab_ret_decay_logit",
           "ab_ret_gn_g", "ab_na_rpb", "ab_w_out", "c_w_in", "c_conv_w", "c_conv_b", "c_dt_bias", "c_a_log", "c_d_skip",
           "c_norm_g", "c_w_out", "ffn_w_up", "ffn_conv_w", "ffn_conv_b", "ffn_w_down"]


def _params(sem=None):
    return pltpu.CompilerParams(dimension_semantics=sem, vmem_limit_bytes=VMEM_LIMIT)


def _mm_nn(a, w, *, name, tm=1024, tn=512, out_dtype=f32):
    M, K = a.shape
    N = w.shape[1]
    tn = min(tn, N)

    def body(a_ref, w_ref, o_ref):
        o_ref[...] = jnp.dot(a_ref[...], w_ref[...], preferred_element_type=f32).astype(o_ref.dtype)

    return pl.pallas_call(
        body, name=name, grid=(M // tm, N // tn),
        in_specs=[pl.BlockSpec((tm, K), lambda i, j: (i, 0)), pl.BlockSpec((K, tn), lambda i, j: (0, j))],
        out_specs=pl.BlockSpec((tm, tn), lambda i, j: (i, j)),
        out_shape=S((M, N), out_dtype), compiler_params=_params(("parallel", "parallel")))(a, w)


def _mm_nt(dy, w, *, name, add=None, tm=512):
    M, N = dy.shape
    K = w.shape[0]
    tk = next((t for t in (1024, 1408, 512, 256, 128) if K % t == 0 and (t <= 512 or t * N * 2 <= MM_BLOCK_BYTES)), K)

    def body(*refs):
        if add is None:
            d_ref, w_ref, o_ref = refs
            o_ref[...] = lax.dot_general(d_ref[...], w_ref[...], NT, preferred_element_type=f32)
        else:
            d_ref, w_ref, a_ref, o_ref = refs
            o_ref[...] = lax.dot_general(d_ref[...], w_ref[...], NT, preferred_element_type=f32) + a_ref[...]

    in_specs = [pl.BlockSpec((tm, N), lambda i, j: (i, 0)), pl.BlockSpec((tk, N), lambda i, j: (j, 0))]
    args = [dy, w]
    if add is not None:
        in_specs.append(pl.BlockSpec((tm, tk), lambda i, j: (i, j)))
        args.append(add)
    return pl.pallas_call(
        body, name=name, grid=(M // tm, K // tk), in_specs=in_specs,
        out_specs=pl.BlockSpec((tm, tk), lambda i, j: (i, j)),
        out_shape=S((M, K), f32), compiler_params=_params(("parallel", "parallel")))(*args)


def _mm_tn(a, dy, *, name, tt=1024):
    M, K = a.shape
    N = dy.shape[1]
    tk = K if K <= 1024 else (1024 if K % 1024 == 0 else K // 2)
    tn = min(512, N)
    tt = min(tt, M)

    def body(a_ref, d_ref, o_ref):
        t = pl.program_id(2)
        part = lax.dot_general(a_ref[...], d_ref[...], TN, preferred_element_type=f32)

        @pl.when(t == 0)
        def _():
            o_ref[...] = part

        @pl.when(t > 0)
        def _():
            o_ref[...] += part

    return pl.pallas_call(
        body, name=name, grid=(K // tk, N // tn, M // tt),
        in_specs=[pl.BlockSpec((tt, tk), lambda k, n, t: (t, k)), pl.BlockSpec((tt, tn), lambda k, n, t: (t, n))],
        out_specs=pl.BlockSpec((tk, tn), lambda k, n, t: (k, n)),
        out_shape=S((K, N), f32), compiler_params=_params(("parallel", "parallel", "arbitrary")))(a, dy)


def _tile_spec(tm, width, base):
    return pl.BlockSpec((tm, width), lambda j, i: (i, base + j))


def _par_spec(width, base):
    return pl.BlockSpec((1, width), lambda j, i: (0, base + j))


def _full_spec(a):
    nd = a.ndim
    return pl.BlockSpec(a.shape, lambda j, i: (0,) * nd)


def _rowwise(name, f, tiles, ctiles, params, consts, outs, *, tm, J=1):
    T = tiles[0][0].shape[0]
    nt, nct, npar, nc = len(tiles), len(ctiles), len(params), len(consts)

    def body(*refs):
        tv = [r[...].astype(f32) for r in refs[:nt + nct]]
        pv = [r[...] for r in refs[nt + nct:nt + nct + npar + nc]]
        res = f(*tv, *pv)
        for o, v in zip(refs[nt + nct + npar + nc:], res):
            o[...] = v.astype(o.dtype)

    in_specs = ([_tile_spec(tm, w, b) for _, w, b in tiles + ctiles] + [_par_spec(w, b) for _, w, b in params]
                + [_full_spec(c) for c in consts])
    return pl.pallas_call(
        body, name=name, grid=(J, T // tm), in_specs=in_specs,
        out_specs=[_tile_spec(tm, w, 0) for w, _ in outs],
        out_shape=[S((T, J * w), dt) for w, dt in outs],
        compiler_params=_params(("parallel", "parallel")))(
            *[a for a, _, _ in tiles + ctiles], *[a for a, _, _ in params], *consts)


def _rowwise_bwd(name, f, tiles, ctiles, params, consts, douts, dtile_dtypes, *, tm, J=1):
    T = tiles[0][0].shape[0]
    nt, nct, npar, nc, nd = len(tiles), len(ctiles), len(params), len(consts), len(douts)

    def body(*refs):
        i = pl.program_id(1)
        k = 0
        tv = [r[...].astype(f32) for r in refs[k:k + nt]]; k += nt
        cv = [r[...].astype(f32) for r in refs[k:k + nct]]; k += nct
        pv = [r[...] for r in refs[k:k + npar]]; k += npar
        kv = [r[...] for r in refs[k:k + nc]]; k += nc
        dv = [r[...].astype(f32) for r in refs[k:k + nd]]; k += nd
        dt_refs = refs[k:k + nt]; k += nt
        dp_refs = refs[k:k + npar]
        _, vjp = jax.vjp(lambda tv_, pv_: tuple(f(*tv_, *cv, *pv_, *kv)), tv, pv)
        dts, dps = vjp(tuple(dv))
        for r, g in zip(dt_refs, dts):
            r[...] = g.astype(r.dtype)
        for r, g in zip(dp_refs, dps):
            @pl.when(i == 0)
            def _(r=r, g=g):
                r[...] = g

            @pl.when(i > 0)
            def _(r=r, g=g):
                r[...] += g

    in_specs = ([_tile_spec(tm, w, b) for _, w, b in tiles + ctiles] + [_par_spec(w, b) for _, w, b in params]
                + [_full_spec(c) for c in consts] + [_tile_spec(tm, w, b) for _, w, b in douts])
    res = pl.pallas_call(
        body, name=name, grid=(J, T // tm), in_specs=in_specs,
        out_specs=[_tile_spec(tm, w, 0) for _, w, _ in tiles] + [_par_spec(w, b) for _, w, b in params],
        out_shape=[S((T, J * w), dt) for (_, w, _), dt in zip(tiles, dtile_dtypes)] + [S(a.shape, f32) for a, _, _ in params],
        compiler_params=_params(("parallel", "arbitrary")))(
            *[a for a, _, _ in tiles + ctiles], *[a for a, _, _ in params], *consts, *[a for a, _, _ in douts])
    return res[:nt], res[nt:]


def _rms(x, g):
    return x * lax.rsqrt(jnp.mean(x * x, axis=-1, keepdims=True) + EPS) * g


def _f_first(x, g1):
    return (_rms(x, g1),)


def _f_first_bwd(x, g1):
    return (x, _rms(x, g1))


def _f_mid(x, m, g2, g3):
    x1 = x + _rms(m, g2)
    return (x1, _rms(x1, g3))


def _f_end(x1, fo, g4, g1n):
    x2 = x1 + _rms(fo, g4)
    return (x2, _rms(x2, g1n))


def _f_last(x1, fo, g4):
    return (x1 + _rms(fo, g4),)


@jax.custom_vjp
def _swap_halves(x):
    c = x.shape[1]
    lane = lax.broadcasted_iota(jnp.int32, x.shape, 1) % RDH
    return jnp.where(lane < RDH // 2, pltpu.roll(x, c - RDH // 2, axis=1), pltpu.roll(x, RDH // 2, axis=1))


_swap_halves.defvjp(lambda x: (_swap_halves(x), None), lambda _, g: (_swap_halves(g),))


def _f_rprep(rq, rk, cos, sin):
    rot = lambda t: t * cos + _swap_halves(t) * sin
    return (rot(rq), rot(rk) * (RDH ** -0.5))


def _split3(x):
    h1 = x.astype(bf16)
    r1 = x - h1.astype(f32)
    h2 = r1.astype(bf16)
    return h1, h2, (r1 - h2.astype(f32)).astype(bf16)


@jax.custom_vjp
def _dot_sel(x, m):
    mb = m.astype(bf16)
    h1, h2, h3 = _split3(x)
    return jnp.dot(h1, mb, preferred_element_type=f32) + jnp.dot(h2, mb, preferred_element_type=f32) + jnp.dot(h3, mb, preferred_element_type=f32)


def _dot_sel_bwd(m, g):
    mb = m.astype(bf16)
    g1, g2, g3 = _split3(g)
    nt = lambda a: lax.dot_general(a, mb, NT, preferred_element_type=f32)
    return nt(g1) + nt(g2) + nt(g3), jnp.zeros_like(m)


_dot_sel.defvjp(lambda x, m: (_dot_sel(x, m), m), _dot_sel_bwd)


def _f_rpost(y, rg, gn, gavg):
    mu = _dot_sel(y, gavg)
    yc = y - mu
    var = _dot_sel(yc * yc, gavg)
    return (jax.nn.silu(rg) * (yc * lax.rsqrt(var + EPS) * gn),)


def _f_sprep(xs, dtr, dtb, alog, ex0, ex1):
    dt = jax.nn.softplus(dtr + dtb)
    la = dt * (-jnp.exp(alog))
    return (xs * _dot_sel(dt, ex0), xs * _dot_sel(dt, ex1), la)


def _f_sprep_bwd(xs, dtr, dtb, alog, ex0, ex1):
    return _f_sprep(xs, dtr, dtb, alog, ex0, ex1) + (xs,)


def _f_spost(y, xs, z, dsk, ng):
    y = (y + xs * dsk) * jax.nn.silu(z)
    y = y * lax.rsqrt(jnp.mean(y * y, axis=-1, keepdims=True) + EPS)
    return (y * ng,)


def _loss_call(y, tgt, *, tm=256):
    T = y.shape[0]

    def body(y_ref, t_ref, dy_ref, l_ref):
        i = pl.program_id(0)
        e = y_ref[...] - t_ref[...]
        dy_ref[...] = e * (1.0 / D)
        part = jnp.zeros((8, LANES), f32) + 0.5 * jnp.sum(jnp.mean(e * e, axis=-1, keepdims=True))

        @pl.when(i == 0)
        def _():
            l_ref[...] = part

        @pl.when(i > 0)
        def _():
            l_ref[...] += part

    return pl.pallas_call(
        body, name="loss_head", grid=(T // tm,),
        in_specs=[pl.BlockSpec((tm, D), lambda i: (i, 0))] * 2,
        out_specs=[pl.BlockSpec((tm, D), lambda i: (i, 0)), pl.BlockSpec((8, LANES), lambda i: (0, 0))],
        out_shape=[S((T, D), f32), S((8, LANES), f32)], compiler_params=_params(("arbitrary",)))(y, tgt)


def _colsum(x, *, name, tm=512):
    T, C = x.shape

    def body(x_ref, o_ref):
        i = pl.program_id(0)
        part = jnp.sum(x_ref[...], axis=0, keepdims=True)

        @pl.when(i == 0)
        def _():
            o_ref[...] = part

        @pl.when(i > 0)
        def _():
            o_ref[...] += part

    return pl.pallas_call(
        body, name=name, grid=(T // tm,), in_specs=[pl.BlockSpec((tm, C), lambda i: (i, 0))],
        out_specs=pl.BlockSpec((1, C), lambda i: (0, 0)), out_shape=S((1, C), f32),
        compiler_params=_params(("arbitrary",)))(x)


def _nn(a, b):
    if a.ndim == 3:
        return lax.dot_general(a, b, (((2,), (1,)), ((0,), (0,))), preferred_element_type=f32)
    return jnp.dot(a, b, preferred_element_type=f32)


def _nt(a, b):
    if a.ndim == 3:
        return lax.dot_general(a, b, (((2,), (2,)), ((0,), (0,))), preferred_element_type=f32)
    return lax.dot_general(a, b, NT, preferred_element_type=f32)


def _lift(x, like):
    return jnp.broadcast_to(x[None], like.shape[:1] + x.shape) if x.ndim < like.ndim else x


def _drop(g, like):
    return jnp.sum(g, axis=0) if like.ndim < g.ndim else g


@jax.custom_vjp
def _mm_lt(a, a_t, b):
    return _nn(_lift(a_t, b), b)


_mm_lt.defvjp(lambda a, a_t, b: (_nn(_lift(a_t, b), b), (a, b)),
              lambda res, g: (jnp.zeros_like(res[0]), _drop(_nt(g, res[1]), res[0]), _nn(_lift(res[0], g), g)))


@jax.custom_vjp
def _mm_rt(a, a_t, b):
    return _nn(_lift(a, b), b)


_mm_rt.defvjp(lambda a, a_t, b: (_nn(_lift(a, b), b), (a_t, b)),
              lambda res, g: (_drop(_nt(g, res[1]), res[0]), jnp.zeros_like(res[0]), _nn(_lift(res[0], g), g)))


@jax.custom_vjp
def _masked_mm(s, s_t, d, d_t, v):
    return _nn(s * d, v)


def _masked_mm_bwd(res, g):
    s, s_t, d, d_t, v = res
    da = _nt(g, v)
    return (_drop(da * d, s), jnp.zeros_like(s_t), da * s, jnp.zeros_like(d_t), _nn(s_t * d_t, g))


_masked_mm.defvjp(lambda s, s_t, d, d_t, v: (_nn(s * d, v), (s, s_t, d, d_t, v)), _masked_mm_bwd)


def _t(x):
    return jnp.swapaxes(x, -1, -2)


@jax.custom_vjp
def _cumsums(a, tri, tri_t):
    pieces = _split3(a)
    cs = sum(jnp.dot(tri, p, preferred_element_type=f32) for p in pieces)
    cs_t = sum(lax.dot_general(p, tri_t, TN, preferred_element_type=f32) for p in pieces)
    return cs, cs_t


def _cumsums_bwd(res, g):
    tri, tri_t = res
    g_cs, g_cs_t = g
    da = sum(jnp.dot(tri_t, p, preferred_element_type=f32) for p in _split3(g_cs))
    da = da + sum(lax.dot_general(tri_t, p, NT, preferred_element_type=f32) for p in _split3(g_cs_t))
    return da, jnp.zeros_like(tri), jnp.zeros_like(tri_t)


_cumsums.defvjp(lambda a, tri, tri_t: (_cumsums(a, tri, tri_t), (tri, tri_t)), _cumsums_bwd)


def _scan_step_heads(h, q, k, v, a, rev, for_vjp=False):
    B, L, P = v.shape
    ii = lax.broadcasted_iota(jnp.int32, (L, L), 0)
    jj = lax.broadcasted_iota(jnp.int32, (L, L), 1)
    if rev:
        tri, tri_t, dmask, dmask_t = (jj >= ii), (ii >= jj), (jj > ii), (ii > jj)
    else:
        tri, tri_t, dmask, dmask_t = (jj <= ii), (ii <= jj), (jj <= ii), (ii <= jj)
    cs, cs_t = _cumsums(a, tri.astype(bf16), tri_t.astype(bf16))
    tot = jnp.sum(a, axis=0, keepdims=True)
    c_col = jnp.stack([jnp.broadcast_to(cs[:, b:b + 1], (L, L)) for b in range(B)])
    c_row = jnp.stack([cs_t[b:b + 1, :] for b in range(B)])
    t_all = jnp.stack([tot[:, b:b + 1] for b in range(B)])
    dec = jnp.exp(jnp.where(dmask[None], c_col - c_row, -1e30))
    e_in, e_out = jnp.exp(c_col)[:, :, :P], jnp.exp(t_all - c_col)[:, :, :P]
    qk = _nt(q, k)
    k_t = _t(k)
    w = v * e_out
    if for_vjp:
        q_t = lax.stop_gradient(_t(q))
        qk_t = lax.stop_gradient(_nt(k, q))
        dec_t = lax.stop_gradient(jnp.exp(jnp.where(dmask_t[None], c_row - c_col, -1e30)))
        y = _masked_mm(qk, qk_t, dec, dec_t, v) + _mm_rt(q, q_t, h) * e_in
        hn = h * jnp.exp(t_all) + _mm_lt(lax.stop_gradient(k), k_t, w)
    else:
        y = _nn(qk * dec, v) + _nn(_lift(q, h), h) * e_in
        hn = h * jnp.exp(t_all) + _nn(_lift(k_t, w), w)
    return hn, y


def _scan_specs(gb, N, Hg, P, Ha, cm, qcol, kcol, vcol):
    qs = lambda col: pl.BlockSpec((CHUNK, gb * N), lambda g, c: (cm(c), col + g))
    vs = lambda col: pl.BlockSpec((CHUNK, gb * Hg * P), lambda g, c: (cm(c), col + g))
    as_ = pl.BlockSpec((1, CHUNK, Ha), lambda g, c: (g, cm(c), 0))
    hs = pl.BlockSpec((gb, 1, Hg, N, P), lambda g, c: (g, cm(c), 0, 0, 0))
    return qs(qcol), qs(kcol), vs(vcol), qs(0), vs(0), as_, hs


def _lanes(ref, n, width):
    return jnp.stack([ref[:, j * width:(j + 1) * width] for j in range(n)])


def _scan_fwd(q, k, v, a, *, G, N, Hg, P, qcol=0, kcol=0, vcol=0, rev, name, add_y=None):
    T, Ha, NC = q.shape[0], a.shape[2], q.shape[0] // CHUNK
    gb = SCAN_HEADS_PER_STEP // Hg
    cm = (lambda c: NC - 1 - c) if rev else (lambda c: c)
    qs, ks, vs, _, ys, as_, hs = _scan_specs(gb, N, Hg, P, Ha, cm, qcol, kcol, vcol)
    extra = [] if add_y is None else [add_y]

    def body(q_ref, k_ref, v_ref, a_ref, *rest):
        y_ref, hs_ref, h_scr = rest[len(extra):]

        @pl.when(pl.program_id(1) == 0)
        def _():
            h_scr[...] = jnp.zeros_like(h_scr)

        if Hg == 1:
            h = h_scr[:, 0]
            hs_ref[:, 0, 0] = h
            hn, y = _scan_step_heads(h, _lanes(q_ref, gb, N), _lanes(k_ref, gb, N), _lanes(v_ref, gb, P), a_ref[0], rev)
            h_scr[:, 0] = hn
        else:
            h = h_scr[0]
            hs_ref[0, 0] = h
            hn, y = _scan_step_heads(h, q_ref[...], k_ref[...], _lanes(v_ref, Hg, P), a_ref[0], rev)
            h_scr[0] = hn
        for j in range(gb * Hg):
            cols = slice(j * P, (j + 1) * P)
            y_ref[:, cols] = y[j] if add_y is None else y[j] + rest[0][:, cols]

    return pl.pallas_call(
        body, name=name, grid=(G // gb, NC), in_specs=[qs, ks, vs, as_] + [ys] * len(extra), out_specs=[ys, hs],
        out_shape=[S((T, G * Hg * P), f32), S((G, NC, Hg, N, P), f32)],
        scratch_shapes=[pltpu.VMEM((gb, Hg, N, P), f32)],
        compiler_params=_params(("parallel", "arbitrary")))(q, k, v, a, *extra)


def _scan_bwd(q, k, v, a, hsave, dy, *, G, N, Hg, P, qcol=0, kcol=0, vcol=0, rev, name, add_to=(None, None, None)):
    T, Ha, NC = q.shape[0], a.shape[2], q.shape[0] // CHUNK
    gb = SCAN_HEADS_PER_STEP // Hg
    cm = (lambda c: c) if rev else (lambda c: NC - 1 - c)
    qs, ks, vs, dqs, dvs, as_, hs = _scan_specs(gb, N, Hg, P, Ha, cm, qcol, kcol, vcol)
    extra = [(x, s) for x, s in zip(add_to, (dqs, dqs, dvs)) if x is not None]

    def body(q_ref, k_ref, v_ref, a_ref, hs_ref, dy_ref, *rest):
        dq_ref, dk_ref, dv_ref, da_ref, dh_scr = rest[len(extra):]
        prev = iter(rest[:len(extra)])
        pq, pk, pv = [next(prev) if x is not None else None for x in add_to]

        @pl.when(pl.program_id(1) == 0)
        def _():
            dh_scr[...] = jnp.zeros_like(dh_scr)

        if Hg == 1:
            _, vjp = jax.vjp(functools.partial(_scan_step_heads, rev=rev, for_vjp=True), hs_ref[:, 0, 0], _lanes(q_ref, gb, N),
                             _lanes(k_ref, gb, N), _lanes(v_ref, gb, P), a_ref[0])
            dh, dq, dk, dv, da = vjp((dh_scr[:, 0], _lanes(dy_ref, gb, P)))
            dh_scr[:, 0] = dh
            for j in range(gb):
                cols = slice(j * N, (j + 1) * N)
                dq_ref[:, cols] = dq[j] if pq is None else dq[j] + pq[:, cols]
                dk_ref[:, cols] = dk[j] if pk is None else dk[j] + pk[:, cols]
        else:
            _, vjp = jax.vjp(functools.partial(_scan_step_heads, rev=rev, for_vjp=True), hs_ref[0, 0], q_ref[...], k_ref[...],
                             _lanes(v_ref, Hg, P), a_ref[0])
            dh, dq, dk, dv, da = vjp((dh_scr[0], _lanes(dy_ref, Hg, P)))
            dh_scr[0] = dh
            dq_ref[...] = dq if pq is None else dq + pq[...]
            dk_ref[...] = dk if pk is None else dk + pk[...]
        for j in range(gb * Hg):
            cols = slice(j * P, (j + 1) * P)
            dv_ref[:, cols] = dv[j] if pv is None else dv[j] + pv[:, cols]
        da_ref[0] = da

    return pl.pallas_call(
        body, name=name, grid=(G // gb, NC), in_specs=[qs, ks, vs, as_, hs, dvs] + [s for _, s in extra],
        out_specs=[dqs, dqs, dvs, as_],
        out_shape=[S((T, G * N), f32), S((T, G * N), f32), S((T, G * Hg * P), f32), S(a.shape, f32)],
        scratch_shapes=[pltpu.VMEM((gb, Hg, N, P), f32)],
        compiler_params=_params(("parallel", "arbitrary")))(q, k, v, a, hsave, dy, *[x for x, _ in extra])


def _na_block_case(rb, nrb):
    return jnp.where(rb == 0, 0, jnp.where(rb == nrb - 1, 2, 1))


def _na_key_start(rb, rows):
    return pl.multiple_of(jnp.clip(rb * NA_QROWS - NA_WR // 2, 0, rows - NA_KROWS) * GRID_W, 256)


def _na_specs(T, nrb):
    nq, nk, wb = NA_QROWS * GRID_W, NA_KROWS * GRID_W, NA_PAIR * NADH
    qs = lambda col: pl.BlockSpec((nq, wb), lambda p, r: (r, col + p))
    fs = lambda col: pl.BlockSpec((T, wb), lambda p, r: (0, col + p))
    bs = pl.BlockSpec((NA_PAIR, 1, nq, nk), lambda p, r: (p, _na_block_case(r, nrb), 0, 0))
    ls = pl.BlockSpec((1, nq, NA_PAIR), lambda p, r: (p, r, 0))
    return qs, fs, bs, ls


def _na_fwd(q, k, v, bias, *, qcol, kcol, vcol, name):
    T = q.shape[0]
    rows = T // GRID_W
    nq, nk = NA_QROWS * GRID_W, NA_KROWS * GRID_W
    nrb = T // nq
    scale = NADH ** -0.5
    qs, fs, bs, ls = _na_specs(T, nrb)

    def body(q_ref, k_ref, v_ref, b_ref, o_ref, l_ref):
        ks = _na_key_start(pl.program_id(1), rows)
        for hh in range(NA_PAIR):
            sl = slice(hh * NADH, (hh + 1) * NADH)
            kw = k_ref[pl.ds(ks, nk), sl]
            vw = v_ref[pl.ds(ks, nk), sl]
            s = lax.dot_general(q_ref[:, sl], kw, NT, preferred_element_type=f32) * scale + b_ref[hh, 0]
            m = jnp.max(s, axis=1, keepdims=True)
            p = jnp.exp(s - m)
            l = jnp.sum(p, axis=1, keepdims=True)
            o_ref[:, sl] = jnp.dot(p.astype(bf16), vw, preferred_element_type=f32) / l
            l_ref[0, :, hh:hh + 1] = m + jnp.log(l)

    return pl.pallas_call(
        body, name=name, grid=(NAH // NA_PAIR, nrb), in_specs=[qs(qcol), fs(kcol), fs(vcol), bs],
        out_specs=[qs(0), ls], out_shape=[S((T, NAW), f32), S((NAH // NA_PAIR, T, NA_PAIR), f32)],
        compiler_params=_params(("parallel", "arbitrary")))(q, k, v, bias)


def _na_bwd(q, k, v, bias, o, lse, do, *, qcol, kcol, vcol, docol, name):
    T = q.shape[0]
    rows = T // GRID_W
    nq, nk = NA_QROWS * GRID_W, NA_KROWS * GRID_W
    nrb = T // nq
    scale = NADH ** -0.5
    qs, fs, bs, ls = _na_specs(T, nrb)

    def body(q_ref, k_ref, v_ref, b_ref, o_ref, l_ref, do_ref, dq_ref, dk_ref, dv_ref, db_ref):
        rb = pl.program_id(1)

        @pl.when(rb == 0)
        def _():
            dk_ref[...] = jnp.zeros_like(dk_ref)
            dv_ref[...] = jnp.zeros_like(dv_ref)

        ks = _na_key_start(rb, rows)
        first = (rb == 0) | (rb == 1) | (rb == nrb - 1)
        for hh in range(NA_PAIR):
            sl = slice(hh * NADH, (hh + 1) * NADH)
            qv = q_ref[:, sl]
            kw = k_ref[pl.ds(ks, nk), sl]
            vw = v_ref[pl.ds(ks, nk), sl]
            s = lax.dot_general(qv, kw, NT, preferred_element_type=f32) * scale + b_ref[hh, 0]
            p = jnp.exp(s - l_ref[0, :, hh:hh + 1])
            do_ = do_ref[:, sl]
            dob = do_.astype(bf16)
            dp = lax.dot_general(dob, vw, NT, preferred_element_type=f32)
            ds = p * (dp - jnp.sum(do_ * o_ref[:, sl], axis=1, keepdims=True))
            dsb = ds.astype(bf16)
            dq_ref[:, sl] = jnp.dot(dsb, kw, preferred_element_type=f32) * scale
            dk_ref[pl.ds(ks, nk), sl] += lax.dot_general(dsb, qv, TN, preferred_element_type=f32) * scale
            dv_ref[pl.ds(ks, nk), sl] += lax.dot_general(p.astype(bf16), dob, TN, preferred_element_type=f32)

            @pl.when(first)
            def _(hh=hh, ds=ds):
                db_ref[hh, 0] = ds

            @pl.when(jnp.logical_not(first))
            def _(hh=hh, ds=ds):
                db_ref[hh, 0] += ds

    return pl.pallas_call(
        body, name=name, grid=(NAH // NA_PAIR, nrb),
        in_specs=[qs(qcol), fs(kcol), fs(vcol), bs, qs(0), ls, qs(docol)],
        out_specs=[qs(0), fs(0), fs(0), bs],
        out_shape=[S((T, NAW), f32), S((T, NAW), f32), S((T, NAW), f32), S(bias.shape, f32)],
        compiler_params=_params(("parallel", "arbitrary")))(q, k, v, bias, o, lse, do)


def _na_col_tables():
    c = np.arange(GRID_W)[:, None]
    kc = np.arange(GRID_W)[None, :]
    cstart = np.clip(c - NA_WC // 2, 0, GRID_W - NA_WC)
    valid_c = (kc >= cstart) & (kc < cstart + NA_WC)
    dc = kc - c + NA_WC - 1
    E = (valid_c[:, :, None] & (dc[:, :, None] == np.arange(2 * NA_WC - 1)[None, None, :])).astype(np.float32)
    return E, np.where(valid_c, 0.0, -1e30).astype(np.float32)


def _na_row_offsets(rows):
    table = []
    for r0 in (0, NA_QROWS, rows - NA_QROWS):
        ks = int(np.clip(r0 - NA_WR // 2, 0, rows - NA_KROWS))
        case = []
        for ri in range(NA_QROWS):
            r = r0 + ri
            rs = int(np.clip(r - NA_WR // 2, 0, rows - NA_WR))
            case.append([ks + kri - r + NA_WR - 1 if rs <= ks + kri < rs + NA_WR else None for kri in range(NA_KROWS)])
        table.append(case)
    return table


def _na_col_bias(rpb):
    E, cmask = _na_col_tables()
    return jnp.einsum("hde,cke->hdck", rpb, E, precision=HI) + cmask


def _na_bias_build(r1, rows, *, name):
    H = r1.shape[0]
    offs = _na_row_offsets(rows)

    def body(r_ref, o_ref):
        outside = jnp.full((GRID_W, GRID_W), -1e30, f32)
        for z in range(3):
            for a in range(NA_QROWS):
                for b in range(NA_KROWS):
                    d = offs[z][a][b]
                    o_ref[0, z, a * GRID_W:(a + 1) * GRID_W, b * GRID_W:(b + 1) * GRID_W] = outside if d is None else r_ref[0, d]

    return pl.pallas_call(
        body, name=name, grid=(H,), in_specs=[pl.BlockSpec((1,) + r1.shape[1:], lambda h: (h, 0, 0, 0))],
        out_specs=pl.BlockSpec((1, 3, NA_QROWS * GRID_W, NA_KROWS * GRID_W), lambda h: (h, 0, 0, 0)),
        out_shape=S((H, 3, NA_QROWS * GRID_W, NA_KROWS * GRID_W), f32), compiler_params=_params(("parallel",)))(r1)


def _na_bias_fold(dbias, rows, *, name):
    H = dbias.shape[0]
    offs = _na_row_offsets(rows)
    nd = 2 * NA_WR - 1

    def body(d_ref, o_ref):
        acc = [None] * nd
        for z in range(3):
            for a in range(NA_QROWS):
                for b in range(NA_KROWS):
                    d = offs[z][a][b]
                    if d is not None:
                        t = d_ref[0, z, a * GRID_W:(a + 1) * GRID_W, b * GRID_W:(b + 1) * GRID_W]
                        acc[d] = t if acc[d] is None else acc[d] + t
        for d in range(nd):
            o_ref[0, d] = acc[d]

    return pl.pallas_call(
        body, name=name, grid=(H,), in_specs=[pl.BlockSpec((1,) + dbias.shape[1:], lambda h: (h, 0, 0, 0))],
        out_specs=pl.BlockSpec((1, nd, GRID_W, GRID_W), lambda h: (h, 0, 0, 0)),
        out_shape=S((H, nd, GRID_W, GRID_W), f32), compiler_params=_params(("parallel",)))(dbias)


def _conv_shifts(prev, cur, nxt, i, n_i, W):
    tm = cur.shape[0]
    prev = jnp.where(i > 0, prev, 0.0)
    nxt = jnp.where(i < n_i - 1, nxt, 0.0)
    ext = jnp.concatenate([prev, cur, nxt], axis=0)
    out = []
    for w in range(W):
        s = (W // 2 - w) % (tm + 16)
        out.append((ext if s == 0 else pltpu.roll(ext, s, axis=0))[8:8 + tm])
    return out


def _conv_act(u, mode):
    if mode == "silu":
        return jax.nn.silu(u)
    assert mode == "geglu"
    half = u.shape[1] // 2
    return jax.nn.gelu(u[:, :half], approximate=True) * u[:, half:]


def _conv_specs(T, tm, tc, xbase):
    r8 = tm // 8
    last = T // 8 - 1
    cur = pl.BlockSpec((tm, tc), lambda j, i: (i, xbase + j))
    prev = pl.BlockSpec((8, tc), lambda j, i: (jnp.maximum(i * r8 - 1, 0), xbase + j))
    nxt = pl.BlockSpec((8, tc), lambda j, i: (jnp.minimum((i + 1) * r8, last), xbase + j))
    return cur, prev, nxt


def _conv(x, w8, b, *, mode, W, name, C, xbase=0, tm=512, tc=512, out_dtype=f32):
    T = x.shape[0]
    NI, J = T // tm, C // tc
    tco = tc // 2 if mode == "geglu" else tc
    cur, prev, nxt = _conv_specs(T, tm, tc, xbase)

    def body(xc, xp, xn, w_ref, b_ref, o_ref):
        sh = _conv_shifts(xp[...].astype(f32), xc[...].astype(f32), xn[...].astype(f32), pl.program_id(1), NI, W)
        wv = w_ref[...]
        u = sh[0] * wv[0:1, :]
        for w in range(1, W):
            u = u + sh[w] * wv[w:w + 1, :]
        o_ref[...] = _conv_act(u + b_ref[...], mode).astype(o_ref.dtype)

    return pl.pallas_call(
        body, name=name, grid=(J, NI),
        in_specs=[cur, prev, nxt, pl.BlockSpec((8, tc), lambda j, i: (0, j)), pl.BlockSpec((1, tc), lambda j, i: (0, j))],
        out_specs=pl.BlockSpec((tm, tco), lambda j, i: (i, j)), out_shape=S((T, J * tco), out_dtype),
        compiler_params=_params(("parallel", "parallel")))(x, x, x, w8, b)


def _conv_bwd(x, w8, b, dact, *, mode, W, name, C, xbase=0, tm=512, tc=512):
    T = x.shape[0]
    NI, J = T // tm, C // tc
    tco = tc // 2 if mode == "geglu" else tc
    rows = tm + 16
    pad = W // 2
    cur, prev, nxt = _conv_specs(T, tm, tc, xbase)
    dcur, dprev, dnxt = _conv_specs(T, tm, tco, 0)

    def body(xc, xp, xn, w_ref, b_ref, dc, dp, dn, dx_ref, dw_ref, db_ref):
        i = pl.program_id(1)
        ext = jnp.concatenate([jnp.where(i > 0, xp[...], 0.0), xc[...], jnp.where(i < NI - 1, xn[...], 0.0)], axis=0)
        dext = jnp.concatenate([jnp.where(i > 0, dp[...], 0.0), dc[...], jnp.where(i < NI - 1, dn[...], 0.0)], axis=0)
        wv = w_ref[...]
        shift = lambda t, w: t if w == pad else pltpu.roll(t, (pad - w) % rows, axis=0)
        xs = [shift(ext, w) for w in range(W)]
        u = b_ref[...] + xs[0] * wv[0:1, :]
        for w in range(1, W):
            u = u + xs[w] * wv[w:w + 1, :]
        _, vjp = jax.vjp(functools.partial(_conv_act, mode=mode), u)
        du = vjp(dext.astype(f32))[0]
        dx = shift(du, 0)[8:8 + tm] * wv[W - 1:W, :]
        for w in range(1, W):
            dx = dx + shift(du, w)[8:8 + tm] * wv[W - 1 - w:W - w, :]
        dx_ref[...] = dx.astype(dx_ref.dtype)

        @pl.when(i == 0)
        def _():
            dw_ref[...] = jnp.zeros_like(dw_ref)
            db_ref[...] = jnp.zeros_like(db_ref)

        dum = du[8:8 + tm]
        db_ref[...] += jnp.sum(dum, axis=0, keepdims=True)
        for w in range(W):
            dw_ref[w:w + 1, :] += jnp.sum(dum * xs[w][8:8 + tm], axis=0, keepdims=True)

    return pl.pallas_call(
        body, name=name, grid=(J, NI),
        in_specs=[cur, prev, nxt, pl.BlockSpec((8, tc), lambda j, i: (0, j)), pl.BlockSpec((1, tc), lambda j, i: (0, j)),
                  dcur, dprev, dnxt],
        out_specs=[pl.BlockSpec((tm, tc), lambda j, i: (i, j)), pl.BlockSpec((8, tc), lambda j, i: (0, j)),
                   pl.BlockSpec((1, tc), lambda j, i: (0, j))],
        out_shape=[S((T, C), bf16), S((8, C), f32), S((1, C), f32)],
        compiler_params=_params(("parallel", "arbitrary")))(x, x, x, w8, b, dact, dact, dact)


def _pad8(w):
    return jnp.concatenate([w, jnp.zeros((8 - w.shape[0], w.shape[1]), w.dtype)], axis=0)


def _all_gather(arrs, *, name):
    n = len(arrs)

    def body(*refs):
        ins, outs = refs[:n], refs[n:2 * n]
        send_sems, recv_sems, loc_sems = refs[2 * n:]
        x, y, c = lax.axis_index("x"), lax.axis_index("y"), lax.axis_index("c")
        ident = lambda px, py, pc: 4 * px + 2 * py + pc
        me, sibling = (x, y, c), (x, y, 1 - c)
        chips = [(1 - x, y), (x, 1 - y), (1 - x, 1 - y)]

        def copy(a, k, block, to, src=None):
            slot = outs[a].at[ident(*block)]
            return pltpu.make_async_remote_copy(
                src_ref=slot if src is None else src, dst_ref=slot, send_sem=send_sems.at[a * 7 + k], recv_sem=recv_sems.at[a * 7 + k],
                device_id=to, device_id_type=pl.DeviceIdType.MESH)

        local = [pltpu.make_async_copy(ins[a], outs[a].at[ident(*me)], loc_sems.at[a]) for a in range(n)]
        for cp in local:
            cp.start()
        first = []
        for a in range(n):
            first.append(copy(a, 0, me, sibling, src=ins[a]))
            first += [copy(a, 1 + j, me, (*chip, c), src=ins[a]) for j, chip in enumerate(chips)]
        for cp in first:
            cp.start()
        passed = []
        for j, chip in enumerate(chips):
            for a in range(n):
                copy(a, 1 + j, (*chip, c), me).wait_recv()
                fwd = copy(a, 4 + j, (*chip, c), sibling)
                fwd.start()
                passed.append(fwd)
        for a in range(n):
            copy(a, 0, sibling, me).wait_recv()
            for j, chip in enumerate(chips):
                copy(a, 4 + j, (*chip, 1 - c), me).wait_recv()
        for cp in first + passed:
            cp.wait_send()
        for cp in local:
            cp.wait()

    any_spec = pl.BlockSpec(memory_space=pl.ANY)
    return pl.pallas_call(
        body, name=name, in_specs=[any_spec] * n, out_specs=[any_spec] * n,
        out_shape=[S((NDEV,) + a.shape, a.dtype) for a in arrs],
        scratch_shapes=[pltpu.SemaphoreType.DMA((7 * n,)), pltpu.SemaphoreType.DMA((7 * n,)), pltpu.SemaphoreType.DMA((n,))],
        )(*arrs)


NCHIP = NDEV // 2


def _to_sibling(arrs, *, name):
    n = len(arrs)
    ncopy = sum(NCHIP if ps else 1 for _, ps in arrs)

    def body(*refs):
        ins, outs = refs[:n], refs[n:2 * n]
        send_sems, recv_sems = refs[2 * n:]
        x, y, c = lax.axis_index("x"), lax.axis_index("y"), lax.axis_index("c")
        copies, idx = [], 0
        for a, (_, per_slot) in enumerate(arrs):
            pairs = [(ins[a].at[2 * q + (1 - c)], outs[a].at[q]) for q in range(NCHIP)] if per_slot else [(ins[a], outs[a])]
            for src, dst in pairs:
                copies.append(pltpu.make_async_remote_copy(
                    src_ref=src, dst_ref=dst, send_sem=send_sems.at[idx], recv_sem=recv_sems.at[idx],
                    device_id=(x, y, 1 - c), device_id_type=pl.DeviceIdType.MESH))
                idx += 1
        for cp in copies:
            cp.start()
        for cp in copies:
            cp.wait_recv()
        for cp in copies:
            cp.wait_send()

    any_spec = pl.BlockSpec(memory_space=pl.ANY)
    return pl.pallas_call(
        body, name=name, in_specs=[any_spec] * n, out_specs=[any_spec] * n,
        out_shape=[S((NCHIP,) + a.shape[1:] if ps else a.shape, a.dtype) for a, ps in arrs],
        scratch_shapes=[pltpu.SemaphoreType.DMA((ncopy,)), pltpu.SemaphoreType.DMA((ncopy,))])(*[a for a, _ in arrs])


def _add_partials(mine, theirs, *, per_slot, tr, name):
    R, C = mine.shape[-2:]

    def body(a_ref, b_ref, o_ref):
        a = a_ref[lax.axis_index("c")] if per_slot else a_ref[...]
        b = b_ref[0] if per_slot else b_ref[...]
        s = a.astype(f32) + b.astype(f32)
        if per_slot:
            o_ref[0] = s.astype(o_ref.dtype)
        else:
            o_ref[...] = s.astype(o_ref.dtype)

    if per_slot:
        grid = (NCHIP, R // tr)
        in_specs = [pl.BlockSpec((2, tr, C), lambda q, i: (q, i, 0)), pl.BlockSpec((1, tr, C), lambda q, i: (q, i, 0))]
        out_spec, out_shape = pl.BlockSpec((1, tr, C), lambda q, i: (q, i, 0)), S((NCHIP, R, C), mine.dtype)
    else:
        grid = (1, R // tr)
        in_specs = [pl.BlockSpec((tr, C), lambda q, i: (i, 0))] * 2
        out_spec, out_shape = pl.BlockSpec((tr, C), lambda q, i: (i, 0)), S((R, C), mine.dtype)
    return pl.pallas_call(body, name=name, grid=grid, in_specs=in_specs, out_specs=out_spec, out_shape=out_shape,
                          compiler_params=_params(("parallel", "parallel")))(mine, theirs)


def _to_chips(arrs, *, name):
    n = len(arrs)

    def body(*refs):
        ins, outs = refs[:n], refs[n:2 * n]
        send_sems, recv_sems, loc_sems = refs[2 * n:]
        x, y, c = lax.axis_index("x"), lax.axis_index("y"), lax.axis_index("c")
        my_q = 2 * x + y
        src = lambda a, q: ins[a].at[q] if arrs[a][1] else ins[a]
        local = [pltpu.make_async_copy(src(a, my_q), outs[a].at[my_q], loc_sems.at[a]) for a in range(n)]
        for cp in local:
            cp.start()
        sent = []
        for j, (px, py) in enumerate([(1 - x, y), (x, 1 - y), (1 - x, 1 - y)]):
            q = 2 * px + py
            for a in range(n):
                mk = lambda slot, a=a, j=j, q=q, dev=(px, py, c): pltpu.make_async_remote_copy(
                    src_ref=src(a, q), dst_ref=outs[a].at[slot], send_sem=send_sems.at[3 * a + j], recv_sem=recv_sems.at[3 * a + j],
                    device_id=dev, device_id_type=pl.DeviceIdType.MESH)
                mk(my_q).start()
                sent.append((mk, q))
        for mk, q in sent:
            mk(q).wait_recv()
        for mk, q in sent:
            mk(q).wait_send()
        for cp in local:
            cp.wait()

    any_spec = pl.BlockSpec(memory_space=pl.ANY)
    return pl.pallas_call(
        body, name=name, in_specs=[any_spec] * n, out_specs=[any_spec] * n,
        out_shape=[S(a.shape if ps else (NCHIP,) + a.shape, a.dtype) for a, ps in arrs],
        scratch_shapes=[pltpu.SemaphoreType.DMA((3 * n,)), pltpu.SemaphoreType.DMA((3 * n,)), pltpu.SemaphoreType.DMA((n,))],
        )(*[a for a, _ in arrs])


def _adamw(r, w, m, v, *, name, tr):
    M, C = w.shape
    nparts = r.shape[0]

    def body(r_ref, w_ref, m_ref, v_ref, g_ref, d_ref, nm_ref, nv_ref):
        g = r_ref[0].astype(f32)
        for s in range(1, nparts):
            g = g + r_ref[s].astype(f32)
        m_ = B1 * m_ref[...] + (1.0 - B1) * g
        v_ = B2 * v_ref[...] + (1.0 - B2) * jnp.square(g)
        m_hat = m_ / (1.0 - B1 ** STEP)
        v_hat = v_ / (1.0 - B2 ** STEP)
        g_ref[...] = g
        d_ref[...] = -LR * (m_hat / (jnp.sqrt(v_hat) + AEPS) + WD * w_ref[...])
        nm_ref[...] = m_
        nv_ref[...] = v_

    row = pl.BlockSpec((tr, C), lambda i: (i, 0))
    return pl.pallas_call(
        body, name=name, grid=(M // tr,),
        in_specs=[pl.BlockSpec((nparts, tr, C), lambda i: (0, i, 0)), row, row, row],
        out_specs=[row] * 4, out_shape=[S((M, C), f32)] * 4, compiler_params=_params(("parallel",)))(r, w, m, v)


def _colmove(ins, in_slots, outs, moves, *, tk, name):
    R = ins[0].shape[1] if in_slots[0] else ins[0].shape[0]
    n_in = len(ins)

    def body(*refs):
        for ii, isl, ic, oi, osl, oc, w in moves:
            src, dst = refs[ii], refs[n_in + oi]
            val = src[:, ic:ic + w] if isl is None else src[isl, :, ic:ic + w]
            if osl is None:
                dst[:, oc:oc + w] = val.astype(dst.dtype)
            else:
                dst[osl, :, oc:oc + w] = val.astype(dst.dtype)

    def spec(is_slots, C):
        return pl.BlockSpec((NDEV, tk, C), lambda i: (0, i, 0)) if is_slots else pl.BlockSpec((tk, C), lambda i: (i, 0))

    return pl.pallas_call(
        body, name=name, grid=(R // tk,),
        in_specs=[spec(sl, a.shape[-1]) for a, sl in zip(ins, in_slots)],
        out_specs=[spec(sl, C) for sl, C, _ in outs],
        out_shape=[S((NDEV, R, C) if sl else (R, C), dt) for sl, C, dt in outs],
        compiler_params=_params(("parallel",)))(*ins)


def _col_pieces(n8, cuts, place):
    out = []
    for p in range(NDEV):
        lo, hi = p * n8, (p + 1) * n8
        edges = [lo] + [c for c in cuts if lo < c < hi] + [hi]
        for a, b in zip(edges[:-1], edges[1:]):
            out.append((p, a - lo) + place(a) + (b - a,))
    return out


def _place_plain(c):
    return (0, c)


def _place_ssd_in(c):
    return (0, c) if c < SSD_INNER + SSD_XBC else (1, c - (SSD_INNER + SSD_XBC))


def _place_ffn_up(c):
    h = FFN_TC // 2
    return (0, (c // h) * FFN_TC + c % h) if c < FFN else (0, ((c - FFN) // h) * FFN_TC + h + (c - FFN) % h)


_COL_LAYOUTS = {
    "ab_w_in": ([], _place_plain, [4 * RW + 3 * NAW]),
    "c_w_in": ([SSD_INNER + SSD_XBC], _place_ssd_in, [SSD_INNER + SSD_XBC, 2 * SSD_H]),
    "ffn_w_up": (list(range(FFN_TC // 2, 2 * FFN, FFN_TC // 2)), _place_ffn_up, [2 * FFN]),
}


def _cols_from_slots(g, which, *, name):
    cuts, place, widths = _COL_LAYOUTS[which]
    moves = [(0, p, sc, mi, None, mc, w) for p, sc, mi, mc, w in _col_pieces(g.shape[2], cuts, place)]
    return _colmove([g], [True], [(False, w, g.dtype) for w in widths], moves, tk=256, name=name)


def _cols_to_slots(mats, which, dtype, *, name):
    cuts, place, widths = _COL_LAYOUTS[which]
    n8 = sum(widths) // NDEV
    moves = [(mi, None, mc, 0, p, sc, w) for p, sc, mi, mc, w in _col_pieces(n8, cuts, place)]
    return _colmove(list(mats), [False] * len(mats), [(True, n8, dtype)], moves, tk=256, name=name)[0]


def _pack(parts, dtype, row_mult):
    flat = jnp.concatenate([p.reshape(-1).astype(dtype) for p in parts])
    rows = -(-flat.shape[0] // LANES)
    rows = -(-rows // row_mult) * row_mult
    return jnp.pad(flat, (0, rows * LANES - flat.shape[0])).reshape(rows, LANES)


def _unpack(buf, shapes, lead=()):
    flat = buf.reshape(lead + (-1,))
    out, off = [], 0
    for shp in shapes:
        n = int(np.prod(shp))
        out.append(flat[..., off:off + n].reshape(lead + tuple(shp)))
        off += n
    return out


def _to_slots(full, ax):
    shp = full.shape
    return jnp.moveaxis(full.reshape(shp[:ax] + (NDEV, shp[ax] // NDEV) + shp[ax + 1:]), ax, 0)


def _from_slots(g, ax):
    t = jnp.moveaxis(g, 0, ax)
    shp = t.shape
    return t.reshape(shp[:ax] + (shp[ax] * shp[ax + 1],) + shp[ax + 2:])


def _ffn_perm(a):
    lead = a.shape[:-1]
    h = FFN_TC // 2
    return jnp.swapaxes(a.reshape(lead + (2, FFN // h, h)), -3, -2).reshape(lead + (2 * FFN,))


def _ffn_unperm(a):
    lead = a.shape[:-1]
    h = FFN_TC // 2
    return jnp.swapaxes(a.reshape(lead + (FFN // h, 2, h)), -3, -2).reshape(lead + (2 * FFN,))


def _rope_tables(T):
    half = RDH // 2
    inv = 1.0 / (ROPE_BASE ** (jnp.arange(half, dtype=f32) / half))
    ang = jnp.arange(T, dtype=f32)[:, None] * inv[None, :]
    cos, sin = jnp.cos(ang), jnp.sin(ang)
    cos_t = jnp.tile(jnp.concatenate([cos, cos], axis=1), (1, RH))
    sin_t = jnp.tile(jnp.concatenate([-sin, sin], axis=1), (1, RH))
    return cos_t, sin_t


def _group_avg():
    g = np.arange(RW) // RDH
    return jnp.asarray((g[:, None] == g[None, :]).astype(np.float32) / RDH)


def _head_expand():
    hd = np.arange(SSD_INNER) // SSD_HD
    rows = np.arange(2 * SSD_H)
    ex0 = (rows[:, None] == hd[None, :]).astype(np.float32)
    ex1 = (rows[:, None] == SSD_H + hd[None, :]).astype(np.float32)
    return jnp.asarray(ex0), jnp.asarray(ex1)


def kernel(x, norm_mix_pre, norm_mix_post, norm_ffn_pre, norm_ffn_post, ab_w_in, ab_ret_decay_logit, ab_ret_gn_g, ab_na_rpb, ab_w_out, c_w_in, c_conv_w, c_conv_b, c_dt_bias, c_a_log, c_d_skip, c_norm_g, c_w_out, ffn_w_up, ffn_conv_w, ffn_conv_b, ffn_w_down, loss_target, m_norm_mix_pre, m_norm_mix_post, m_norm_ffn_pre, m_norm_ffn_post, m_ab_w_in, m_ab_ret_decay_logit, m_ab_ret_gn_g, m_ab_na_rpb, m_ab_w_out, m_c_w_in, m_c_conv_w, m_c_conv_b, m_c_dt_bias, m_c_a_log, m_c_d_skip, m_c_norm_g, m_c_w_out, m_ffn_w_up, m_ffn_conv_w, m_ffn_conv_b, m_ffn_w_down, v_norm_mix_pre, v_norm_mix_post, v_norm_ffn_pre, v_norm_ffn_post, v_ab_w_in, v_ab_ret_decay_logit, v_ab_ret_gn_g, v_ab_na_rpb, v_ab_w_out, v_c_w_in, v_c_conv_w, v_c_conv_b, v_c_dt_bias, v_c_a_log, v_c_d_skip, v_c_norm_g, v_c_w_out, v_ffn_w_up, v_ffn_conv_w, v_ffn_conv_b, v_ffn_w_down):
    W = dict(norm_mix_pre=norm_mix_pre, norm_mix_post=norm_mix_post, norm_ffn_pre=norm_ffn_pre, norm_ffn_post=norm_ffn_post, ab_w_in=ab_w_in, ab_ret_decay_logit=ab_ret_decay_logit, ab_ret_gn_g=ab_ret_gn_g, ab_na_rpb=ab_na_rpb, ab_w_out=ab_w_out, c_w_in=c_w_in, c_conv_w=c_conv_w, c_conv_b=c_conv_b, c_dt_bias=c_dt_bias, c_a_log=c_a_log, c_d_skip=c_d_skip, c_norm_g=c_norm_g, c_w_out=c_w_out, ffn_w_up=ffn_w_up, ffn_conv_w=ffn_conv_w, ffn_conv_b=ffn_conv_b, ffn_w_down=ffn_w_down)
    Mo = dict(norm_mix_pre=m_norm_mix_pre, norm_mix_post=m_norm_mix_post, norm_ffn_pre=m_norm_ffn_pre, norm_ffn_post=m_norm_ffn_post, ab_w_in=m_ab_w_in, ab_ret_decay_logit=m_ab_ret_decay_logit, ab_ret_gn_g=m_ab_ret_gn_g, ab_na_rpb=m_ab_na_rpb, ab_w_out=m_ab_w_out, c_w_in=m_c_w_in, c_conv_w=m_c_conv_w, c_conv_b=m_c_conv_b, c_dt_bias=m_c_dt_bias, c_a_log=m_c_a_log, c_d_skip=m_c_d_skip, c_norm_g=m_c_norm_g, c_w_out=m_c_w_out, ffn_w_up=m_ffn_w_up, ffn_conv_w=m_ffn_conv_w, ffn_conv_b=m_ffn_conv_b, ffn_w_down=m_ffn_w_down)
    Vo = dict(norm_mix_pre=v_norm_mix_pre, norm_mix_post=v_norm_mix_post, norm_ffn_pre=v_norm_ffn_pre, norm_ffn_post=v_norm_ffn_post, ab_w_in=v_ab_w_in, ab_ret_decay_logit=v_ab_ret_decay_logit, ab_ret_gn_g=v_ab_ret_gn_g, ab_na_rpb=v_ab_na_rpb, ab_w_out=v_ab_w_out, c_w_in=v_c_w_in, c_conv_w=v_c_conv_w, c_conv_b=v_c_conv_b, c_dt_bias=v_c_dt_bias, c_a_log=v_c_a_log, c_d_skip=v_c_d_skip, c_norm_g=v_c_norm_g, c_w_out=v_c_w_out, ffn_w_up=v_ffn_w_up, ffn_conv_w=v_ffn_conv_w, ffn_conv_b=v_ffn_conv_b, ffn_w_down=v_ffn_w_down)
    return _train_step(x[0], loss_target[0], W, Mo, Vo)


def _train_step(x, tgt, W, Mo, Vo):
    T = x.shape[0]
    rows = T // GRID_W

    col = lambda d, n, dt: d[n].reshape(-1, d[n].shape[-1]).astype(dt)
    rows_of = lambda d, dt: jnp.concatenate([col(d, n, dt) for n in ROW_SHARDED], axis=0)
    small = _pack([W[n] for n, _ in SHARDED[N_BIG:]], f32, 8)
    gat = _all_gather([col(W, n, bf16) for n in COL_SHARDED] + [rows_of(W, bf16), small], name="gather_weights")
    per_layer = lambda m: m.reshape(-1, D, m.shape[-1])
    w_ab_in = per_layer(_cols_from_slots(gat[0], "ab_w_in", name="cols_ab_w_in")[0])
    w_zx, w_dt = [per_layer(m) for m in _cols_from_slots(gat[1], "c_w_in", name="cols_c_w_in")]
    w_up = per_layer(_cols_from_slots(gat[2], "ffn_w_up", name="cols_ffn_w_up")[0])
    full, off = {}, 0
    for n in ROW_SHARDED:
        L, r = W[n].shape[0], W[n].shape[1]
        full[n] = jnp.swapaxes(gat[3][:, off:off + L * r].reshape(NDEV, L, r, D), 0, 1).reshape(L, NDEV * r, D)
        off += L * r
    gs = _unpack(gat[4], [W[n].shape for n, _ in SHARDED[N_BIG:]], (NDEV,))
    full.update({n: _from_slots(g, ax) for (n, ax), g in zip(SHARDED[N_BIG:], gs)})
    w_ab_out, w_c_out, w_down = full["ab_w_out"], full["c_w_out"], full["ffn_w_down"]
    c_cw8 = [_pad8(full["c_conv_w"][i]) for i in range(2)]
    c_cb = [full["c_conv_b"][i][None] for i in range(2)]
    c_ng = [full["c_norm_g"][i][None] for i in range(2)]
    f_cw8 = [_pad8(_ffn_perm(full["ffn_conv_w"][l])) for l in range(DEPTH)]
    f_cb = [_ffn_perm(W["ffn_conv_b"][l])[None] for l in range(DEPTH)]

    g1 = [W["norm_mix_pre"][l][None] for l in range(DEPTH)]
    g2 = [W["norm_mix_post"][l][None] for l in range(DEPTH)]
    g3 = [W["norm_ffn_pre"][l][None] for l in range(DEPTH)]
    g4 = [W["norm_ffn_post"][l][None] for l in range(DEPTH)]
    cos_t, sin_t = _rope_tables(T)
    gavg = _group_avg()
    ex0, ex1 = _head_expand()

    def log_gamma(logit):
        return -jax.nn.softplus(-logit)

    def ret_decays(lg):
        return [jnp.broadcast_to(lg[d][None, None, :], (1, T, RH)) for d in range(2)]

    saved = []
    xs_ = x
    hn = _rowwise("norm_first", _f_first, [(x, D, 0)], [], [(g1[0], D, 0)], [], [(D, bf16)], tm=256)[0]
    for l in range(DEPTH):
        i = l // 2
        sv = dict(x=xs_, hn=hn)
        if l % 2 == 0:
            proj = _mm_nn(hn, w_ab_in[i], name=f"ab_in_{l}")
            qr, kr = _rowwise(f"ret_prep_{l}", _f_rprep, [(proj, RW, 0), (proj, RW, 1)], [(cos_t, RW, 0), (sin_t, RW, 0)], [], [],
                              [(RW, f32), (RW, f32)], tm=256)
            lg, lg_vjp = jax.vjp(log_gamma, W["ab_ret_decay_logit"][i])
            a_f, a_b = ret_decays(lg)
            rscan = dict(G=RH, N=RDH, Hg=1, P=RDH, vcol=2)
            yf_t, hsf = _scan_fwd(qr, kr, proj, a_f, rev=False, name=f"ret_scan_f_{l}", **rscan)
            y_t, hsb = _scan_fwd(qr, kr, proj, a_b, rev=True, name=f"ret_scan_b_{l}", add_y=yf_t, **rscan)
            gn = W["ab_ret_gn_g"][i][None]
            ret = _rowwise(f"ret_post_{l}", _f_rpost, [(y_t, RW, 0), (proj, RW, 3)], [], [(gn, RW, 0)], [gavg],
                           [(RW, bf16)], tm=256)[0]
            nqkv = proj[:, 4 * RW:].astype(bf16)
            ncols = dict(qcol=0, kcol=NAW // 128, vcol=2 * NAW // 128)
            r1, bias_vjp = jax.vjp(_na_col_bias, W["ab_na_rpb"][i])
            bias = _na_bias_build(r1, rows, name=f"na_bias_{l}")
            na_o, na_l = _na_fwd(nqkv, nqkv, nqkv, bias, name=f"na_fwd_{l}", **ncols)
            cat = jnp.concatenate([ret, na_o.astype(bf16)], axis=1)
            mo = _mm_nn(cat, w_ab_out[i], name=f"ab_out_{l}")
            sv.update(proj=proj, qr=qr, kr=kr, a_f=a_f, a_b=a_b, hsf=hsf, hsb=hsb, y_t=y_t, gn=gn, rscan=rscan,
                      nqkv=nqkv, ncols=ncols, bias=bias, bias_vjp=bias_vjp, lg_vjp=lg_vjp, na_o=na_o, na_l=na_l, cat=cat)
        else:
            zx = _mm_nn(hn, w_zx[i], name=f"c_in_{l}")
            dtr = _mm_nn(hn, w_dt[i], name=f"c_in_dt_{l}")
            xa = _conv(zx, c_cw8[i], c_cb[i], mode="silu", W=SSD_CONV, name=f"c_conv_{l}", C=SSD_XBC, xbase=SSD_INNER // 512)
            dtb, alog = W["c_dt_bias"][i].reshape(1, 2 * SSD_H), W["c_a_log"][i].reshape(1, 2 * SSD_H)
            vf, vb, la = _rowwise(f"ssd_prep_{l}", _f_sprep, [(xa, SSD_INNER, 0), (dtr, 2 * SSD_H, 0)], [],
                                  [(dtb, 2 * SSD_H, 0), (alog, 2 * SSD_H, 0)], [ex0, ex1],
                                  [(SSD_INNER, f32), (SSD_INNER, f32), (2 * SSD_H, f32)], tm=128)
            a_f = la[:, :SSD_H].reshape(T, SSD_G, SSD_HPG).transpose(1, 0, 2)
            a_b = la[:, SSD_H:].reshape(T, SSD_G, SSD_HPG).transpose(1, 0, 2)
            sscan = dict(G=SSD_G, N=SSD_N, Hg=SSD_HPG, P=SSD_HD, qcol=(SSD_INNER + SSD_G * SSD_N) // SSD_N, kcol=SSD_INNER // SSD_N)
            yf_t, hsf = _scan_fwd(xa, xa, vf, a_f, rev=False, name=f"ssd_scan_f_{l}", **sscan)
            y_t, hsb = _scan_fwd(xa, xa, vb, a_b, rev=True, name=f"ssd_scan_b_{l}", add_y=yf_t, **sscan)
            dsk = jnp.repeat(W["c_d_skip"][i], SSD_HD)[None]
            yo = _rowwise(f"ssd_post_{l}", _f_spost, [(y_t, 512, 0), (xa, 512, 0), (zx, 512, 0)], [],
                          [(dsk, 512, 0), (c_ng[i], 512, 0)], [], [(512, bf16)], tm=256, J=SSD_G)[0]
            mo = _mm_nn(yo, w_c_out[i], name=f"c_out_{l}")
            sv.update(zx=zx, dtr=dtr, xa=xa, dtb=dtb, alog=alog, a_f=a_f, a_b=a_b, vf=vf, vb=vb, sscan=sscan,
                      hsf=hsf, hsb=hsb, y_t=y_t, dsk=dsk, yo=yo)
        x1, hf = _rowwise(f"norm_mid_{l}", _f_mid, [(xs_, D, 0), (mo, D, 0)], [], [(g2[l], D, 0), (g3[l], D, 0)], [],
                          [(D, f32), (D, bf16)], tm=256)
        pre = _mm_nn(hf, w_up[l], name=f"ffn_up_{l}")
        act = _conv(pre, f_cw8[l], f_cb[l], mode="geglu", W=FFN_CONV, name=f"ffn_conv_{l}", C=2 * FFN, tc=FFN_TC, out_dtype=bf16)
        fo = _mm_nn(act, w_down[l], name=f"ffn_down_{l}")
        sv.update(mo=mo, x1=x1, hf=hf, pre=pre, act=act, fo=fo)
        if l < DEPTH - 1:
            xs_, hn = _rowwise(f"norm_end_{l}", _f_end, [(x1, D, 0), (fo, D, 0)], [], [(g4[l], D, 0), (g1[l + 1], D, 0)], [],
                               [(D, f32), (D, bf16)], tm=256)
        else:
            xs_ = _rowwise(f"norm_end_{l}", _f_last, [(x1, D, 0), (fo, D, 0)], [], [(g4[l], D, 0)], [], [(D, f32)], tm=256)[0]
        saved.append(sv)

    dx, lpart = _loss_call(xs_, tgt)
    loss = lax.psum(lpart[0, 0], ("x", "y", "c"))

    G = {n: [None] * W[n].shape[0] for n in WEIGHTS}
    dhn = None
    for l in reversed(range(DEPTH)):
        i = l // 2
        sv = saved[l]
        if l == DEPTH - 1:
            (dx1, dfo), (dg4,) = _rowwise_bwd(f"norm_end_bwd_{l}", _f_last, [(sv["x1"], D, 0), (sv["fo"], D, 0)], [],
                                              [(g4[l], D, 0)], [], [(dx, D, 0)], [f32, bf16], tm=256)
        else:
            (dx1, dfo), (dg4, dg1n) = _rowwise_bwd(f"norm_end_bwd_{l}", _f_end, [(sv["x1"], D, 0), (sv["fo"], D, 0)], [],
                                                   [(g4[l], D, 0), (g1[l + 1], D, 0)], [], [(dx, D, 0), (dhn, D, 0)],
                                                   [f32, bf16], tm=256)
            G["norm_mix_pre"][l + 1] = dg1n[0]
        G["norm_ffn_post"][l] = dg4[0]
        dact = _mm_nt(dfo, w_down[l], name=f"ffn_down_dx_{l}")
        G["ffn_w_down"][l] = _mm_tn(sv["act"], dfo, name=f"ffn_down_dw_{l}")
        dpre, dfw, dfb = _conv_bwd(sv["pre"], f_cw8[l], f_cb[l], dact, mode="geglu", W=FFN_CONV, name=f"ffn_conv_bwd_{l}",
                                   C=2 * FFN, tc=FFN_TC)
        dhf = _mm_nt(dpre, w_up[l], name=f"ffn_up_dx_{l}")
        G["ffn_w_up"][l] = _cols_to_slots([_mm_tn(sv["hf"], dpre, name=f"ffn_up_dw_{l}")], "ffn_w_up", bf16, name=f"slots_ffn_up_{l}")
        G["ffn_conv_w"][l] = _ffn_unperm(dfw[:FFN_CONV])
        G["ffn_conv_b"][l] = _ffn_unperm(dfb[0])
        (dxl, dmo), (dg2, dg3) = _rowwise_bwd(f"norm_mid_bwd_{l}", _f_mid, [(sv["x"], D, 0), (sv["mo"], D, 0)], [],
                                              [(g2[l], D, 0), (g3[l], D, 0)], [], [(dx1, D, 0), (dhf, D, 0)], [f32, bf16], tm=256)
        G["norm_mix_post"][l] = dg2[0]
        G["norm_ffn_pre"][l] = dg3[0]
        if l % 2 == 0:
            dcat = _mm_nt(dmo, w_ab_out[i], name=f"ab_out_dx_{l}")
            G["ab_w_out"][i] = _mm_tn(sv["cat"], dmo, name=f"ab_out_dw_{l}")
            (dy, drg), (dgn,) = _rowwise_bwd(
                f"ret_post_bwd_{l}", _f_rpost, [(sv["y_t"], RW, 0), (sv["proj"], RW, 3)], [],
                [(sv["gn"], RW, 0)], [gavg], [(dcat, RW, 0)], [f32, bf16], tm=256)
            G["ab_ret_gn_g"][i] = dgn[0]
            dqf, dkf, dvf, daf = _scan_bwd(sv["qr"], sv["kr"], sv["proj"], sv["a_f"], sv["hsf"], dy, rev=False,
                                           name=f"ret_scan_f_bwd_{l}", **sv["rscan"])
            dq_t, dk_t, dv_t, dab = _scan_bwd(sv["qr"], sv["kr"], sv["proj"], sv["a_b"], sv["hsb"], dy, rev=True,
                                              name=f"ret_scan_b_bwd_{l}", add_to=(dqf, dkf, dvf), **sv["rscan"])
            drv = dv_t.astype(bf16)
            (drq, drk), _ = _rowwise_bwd(f"ret_prep_bwd_{l}", _f_rprep, [(sv["proj"], RW, 0), (sv["proj"], RW, 1)],
                                         [(cos_t, RW, 0), (sin_t, RW, 0)], [], [], [(dq_t, RW, 0), (dk_t, RW, 0)], [bf16, bf16], tm=256)
            da_cols = jnp.concatenate([daf[0], dab[0]], axis=1)
            dlg = _colsum(da_cols, name=f"ret_decay_sum_{l}").reshape(2, RH)
            G["ab_ret_decay_logit"][i] = sv["lg_vjp"](dlg)[0]
            dnq, dnk, dnv, dbias = _na_bwd(sv["nqkv"], sv["nqkv"], sv["nqkv"], sv["bias"], sv["na_o"], sv["na_l"], dcat,
                                           docol=RW // 128, name=f"na_bwd_{l}", **sv["ncols"])
            G["ab_na_rpb"][i] = sv["bias_vjp"](_na_bias_fold(dbias, rows, name=f"na_bias_fold_{l}"))[0]
            dproj = jnp.concatenate([drq, drk, drv, drg] + [t.astype(bf16) for t in (dnq, dnk, dnv)], axis=1)
            dhn = _mm_nt(dproj, w_ab_in[i], name=f"ab_in_dx_{l}")
            G["ab_w_in"][i] = _cols_to_slots([_mm_tn(sv["hn"], dproj, name=f"ab_in_dw_{l}")], "ab_w_in", bf16, name=f"slots_ab_in_{l}")
        else:
            dyo = _mm_nt(dmo, w_c_out[i], name=f"c_out_dx_{l}")
            G["c_w_out"][i] = _mm_tn(sv["yo"], dmo, name=f"c_out_dw_{l}")
            (dy, dxs1, dz), (ddsk, dng) = _rowwise_bwd(
                f"ssd_post_bwd_{l}", _f_spost, [(sv["y_t"], 512, 0), (sv["xa"], 512, 0), (sv["zx"], 512, 0)],
                [], [(sv["dsk"], 512, 0), (c_ng[i], 512, 0)], [], [(dyo, 512, 0)], [f32, f32, bf16], tm=256, J=SSD_G)
            G["c_d_skip"][i] = ddsk.reshape(SSD_H, SSD_HD).sum(axis=1)
            G["c_norm_g"][i] = dng[0]
            dqf, dkf, dvf, daf = _scan_bwd(sv["xa"], sv["xa"], sv["vf"], sv["a_f"], sv["hsf"], dy, rev=False,
                                           name=f"ssd_scan_f_bwd_{l}", **sv["sscan"])
            dq_t, dk_t, dvb, dab = _scan_bwd(sv["xa"], sv["xa"], sv["vb"], sv["a_b"], sv["hsb"], dy, rev=True,
                                             name=f"ssd_scan_b_bwd_{l}", add_to=(dqf, dkf, None), **sv["sscan"])
            dla = jnp.concatenate([daf.transpose(1, 0, 2).reshape(T, SSD_H), dab.transpose(1, 0, 2).reshape(T, SSD_H)], axis=1)
            (dxs, ddtr), (ddtb, dalog) = _rowwise_bwd(
                f"ssd_prep_bwd_{l}", _f_sprep_bwd, [(sv["xa"], SSD_INNER, 0), (sv["dtr"], 2 * SSD_H, 0)], [],
                [(sv["dtb"], 2 * SSD_H, 0), (sv["alog"], 2 * SSD_H, 0)], [ex0, ex1],
                [(dvf, SSD_INNER, 0), (dvb, SSD_INNER, 0), (dla, 2 * SSD_H, 0), (dxs1, SSD_INNER, 0)], [f32, bf16], tm=128)
            G["c_dt_bias"][i] = ddtb.reshape(2, SSD_H)
            G["c_a_log"][i] = dalog.reshape(2, SSD_H)
            dxa = jnp.concatenate([dxs, dk_t, dq_t], axis=1)
            dxbc, dcw, dcb = _conv_bwd(sv["zx"], c_cw8[i], c_cb[i], dxa, mode="silu", W=SSD_CONV, name=f"c_conv_bwd_{l}",
                                       C=SSD_XBC, xbase=SSD_INNER // 512)
            G["c_conv_w"][i] = dcw[:SSD_CONV]
            G["c_conv_b"][i] = dcb[0]
            dzx = jnp.concatenate([dz, dxbc], axis=1)
            t1 = _mm_nt(ddtr, w_dt[i], name=f"c_in_dt_dx_{l}")
            dhn = _mm_nt(dzx, w_zx[i], add=t1, name=f"c_in_dx_{l}")
            G["c_w_in"][i] = _cols_to_slots([_mm_tn(sv["hn"], dzx, name=f"c_in_dw_{l}"), _mm_tn(sv["hn"], ddtr, name=f"c_in_dt_dw_{l}")],
                                            "c_w_in", bf16, name=f"slots_c_in_{l}")
        dx = dxl
    (grad_x,), (dg1,) = _rowwise_bwd("norm_first_bwd", _f_first_bwd, [(x, D, 0)], [], [(g1[0], D, 0)], [], [(dx, D, 0), (dhn, D, 0)],
                                     [f32], tm=256)
    G["norm_mix_pre"][0] = dg1[0]

    small_names = [n for n, _ in SHARDED[N_BIG:]]
    col_slots = [jnp.concatenate(G[n], axis=1) for n in COL_SHARDED]
    row_slots = jnp.concatenate([g.reshape(NDEV, -1, D).astype(bf16) for n in ROW_SHARDED for g in G[n]], axis=1)
    small_slots = _pack_slots([_to_slots(jnp.stack(G[n]), ax) for n, ax in SHARDED[N_BIG:]], 8)
    ar = _pack([jnp.stack(G[n]) for n in REPLICATED], f32, 8)
    parts = [(a, True) for a in col_slots + [row_slots, small_slots]] + [(ar, False)]
    from_sib = _to_sibling(parts, name="grads_to_sibling")
    tiles = [256, 256, 256, 64, small_slots.shape[1], ar.shape[0]]
    chip = [_add_partials(a, b, per_slot=ps, tr=t, name=f"grads_add_{j}")
            for j, ((a, ps), b, t) in enumerate(zip(parts, from_sib, tiles))]
    exch = _to_chips([(a, ps) for a, (_, ps) in zip(chip, parts)], name="grads_to_chips")
    pk = lambda d, names: _pack([d[n] for n in names], f32, 8)
    upd = [_adamw(exch[j], col(W, n, f32), col(Mo, n, f32), col(Vo, n, f32), name=f"adamw_{n}", tr=256)
           for j, n in enumerate(COL_SHARDED)]
    upd_rows = _adamw(exch[3], rows_of(W, f32), rows_of(Mo, f32), rows_of(Vo, f32), name="adamw_rows", tr=64)
    upd_small = _adamw(exch[4], pk(W, small_names), pk(Mo, small_names), pk(Vo, small_names), name="adamw_small",
                       tr=small_slots.shape[1])
    upd_rep = _adamw(exch[5], pk(W, REPLICATED), pk(Mo, REPLICATED), pk(Vo, REPLICATED), name="adamw_replicated", tr=ar.shape[0])
    res = []
    for k in range(4):
        d = {n: upd[j][k].reshape(W[n].shape) for j, n in enumerate(COL_SHARDED)}
        off = 0
        for n in ROW_SHARDED:
            cnt = W[n].shape[0] * W[n].shape[1]
            d[n] = upd_rows[k][off:off + cnt].reshape(W[n].shape)
            off += cnt
        d.update(zip(small_names, _unpack(upd_small[k], [W[n].shape for n in small_names])))
        d.update(zip(REPLICATED, _unpack(upd_rep[k], [W[n].shape for n in REPLICATED])))
        res.append(d)
    outs = [loss, grad_x[None]]
    for k in range(4):
        outs += [res[k][n] for n in WEIGHTS]
    return tuple(outs)


def _pack_slots(slot_arrays, row_mult):
    flat = jnp.concatenate([a.reshape(NDEV, -1) for a in slot_arrays], axis=1)
    rows = -(-flat.shape[1] // LANES)
    rows = -(-rows // row_mult) * row_mult
    return jnp.pad(flat, ((0, 0), (0, rows * LANES - flat.shape[1]))).reshape(NDEV, rows, LANES)
```

```python
import functools
import numpy as np
import jax
import jax.numpy as jnp
from jax import lax
from jax.experimental import pallas as pl
from jax.experimental.pallas import tpu as pltpu

f32, bf16 = jnp.float32, jnp.bfloat16
S = jax.ShapeDtypeStruct
HI = lax.Precision.HIGHEST

D = 1024
DEPTH = 4
GRID_W = 64
CHUNK = 128
EPS = 1e-6
RH, RDH, RW = 8, 64, 512
NAH, NADH, NAW = 8, 64, 512
NA_WR, NA_WC = 8, 16
NA_QROWS = 8
NA_KROWS = 16
NA_PAIR = 2
SSD_INNER, SSD_HD, SSD_H, SSD_G, SSD_HPG, SSD_N, SSD_CONV = 2048, 64, 32, 4, 8, 128, 5
SSD_XBC = SSD_INNER + 2 * SSD_G * SSD_N
FFN, FFN_CONV = 2816, 3
FFN_TC = 512
SCAN_HEADS_PER_STEP = 8
ROPE_BASE = 10000.0
LR, B1, B2, AEPS, WD, STEP = 0.001, 0.9, 0.999, 1e-08, 0.01, 10
NDEV = 8
LANES = 128
VMEM_LIMIT = 56 * 1024 * 1024
MM_BLOCK_BYTES = 6 * 1024 * 1024
ROW_TILE = 512
CONV_TILE = 1024

NT = (((1,), (1,)), ((), ()))
TN = (((0,), (0,)), ((), ()))

SHARDED = [("ab_w_in", 2), ("ab_w_out", 1), ("c_w_in", 2), ("c_w_out", 1), ("ffn_w_up", 2), ("ffn_w_down", 1),
           ("c_conv_w", 2), ("c_conv_b", 1), ("c_norm_g", 1), ("ffn_conv_w", 2)]
N_BIG = 6
COL_SHARDED = ["ab_w_in", "c_w_in", "ffn_w_up"]
ROW_SHARDED = ["ab_w_out", "c_w_out", "ffn_w_down"]
REPLICATED = ["norm_mix_pre", "norm_mix_post", "norm_ffn_pre", "norm_ffn_post", "ab_ret_decay_logit", "ab_ret_gn_g",
              "ab_na_rpb", "c_dt_bias", "c_a_log", "c_d_skip", "ffn_conv_b"]
WEIGHTS = ["norm_mix_pre", "norm_mix_post", "norm_ffn_pre", "norm_ffn_post", "ab_w_in", "ab_ret_decay_logit",
           "ab_ret_gn_g", "ab_na_rpb", "ab_w_out", "c_w_in", "c_conv_w", "c_conv_b", "c_dt_bias", "c_a_log", "c_d_skip",
           "c_norm_g", "c_w_out", "ffn_w_up", "ffn_conv_w", "ffn_conv_b", "ffn_w_down"]


def _params(sem=None):
    return pltpu.CompilerParams(dimension_semantics=sem, vmem_limit_bytes=VMEM_LIMIT)


def _mm_nn(a, w, *, name, tm=1024, tn=512, out_dtype=f32):
    M, K = a.shape
    N = w.shape[1]
    tn = min(tn, N)

    def body(a_ref, w_ref, o_ref):
        o_ref[...] = jnp.dot(a_ref[...], w_ref[...], preferred_element_type=f32).astype(o_ref.dtype)

    return pl.pallas_call(
        body, name=name, grid=(M // tm, N // tn),
        in_specs=[pl.BlockSpec((tm, K), lambda i, j: (i, 0)), pl.BlockSpec((K, tn), lambda i, j: (0, j))],
        out_specs=pl.BlockSpec((tm, tn), lambda i, j: (i, j)),
        out_shape=S((M, N), out_dtype), compiler_params=_params(("parallel", "parallel")))(a, w)


def _mm_nt(dy, w, *, name, add=None, tm=512):
    M, N = dy.shape
    K = w.shape[0]
    tk = next((t for t in (1024, 1408, 512, 256, 128) if K % t == 0 and (t <= 512 or t * N * 2 <= MM_BLOCK_BYTES)), K)

    def body(*refs):
        if add is None:
            d_ref, w_ref, o_ref = refs
            o_ref[...] = lax.dot_general(d_ref[...], w_ref[...], NT, preferred_element_type=f32)
        else:
            d_ref, w_ref, a_ref, o_ref = refs
            o_ref[...] = lax.dot_general(d_ref[...], w_ref[...], NT, preferred_element_type=f32) + a_ref[...]

    in_specs = [pl.BlockSpec((tm, N), lambda i, j: (i, 0)), pl.BlockSpec((tk, N), lambda i, j: (j, 0))]
    args = [dy, w]
    if add is not None:
        in_specs.append(pl.BlockSpec((tm, tk), lambda i, j: (i, j)))
        args.append(add)
    return pl.pallas_call(
        body, name=name, grid=(M // tm, K // tk), in_specs=in_specs,
        out_specs=pl.BlockSpec((tm, tk), lambda i, j: (i, j)),
        out_shape=S((M, K), f32), compiler_params=_params(("parallel", "parallel")))(*args)


def _mm_tn(a, dy, *, name, tt=1024):
    M, K = a.shape
    N = dy.shape[1]
    tk = K if K <= 1024 else (1024 if K % 1024 == 0 else K // 2)
    tn = min(512, N)
    tt = min(tt, M)

    def body(a_ref, d_ref, o_ref):
        t = pl.program_id(2)
        part = lax.dot_general(a_ref[...], d_ref[...], TN, preferred_element_type=f32)

        @pl.when(t == 0)
        def _():
            o_ref[...] = part

        @pl.when(t > 0)
        def _():
            o_ref[...] += part

    return pl.pallas_call(
        body, name=name, grid=(K // tk, N // tn, M // tt),
        in_specs=[pl.BlockSpec((tt, tk), lambda k, n, t: (t, k)), pl.BlockSpec((tt, tn), lambda k, n, t: (t, n))],
        out_specs=pl.BlockSpec((tk, tn), lambda k, n, t: (k, n)),
        out_shape=S((K, N), f32), compiler_params=_params(("parallel", "parallel", "arbitrary")))(a, dy)


def _tile_spec(tm, width, base):
    return pl.BlockSpec((tm, width), lambda j, i: (i, base + j))


def _par_spec(width, base):
    return pl.BlockSpec((1, width), lambda j, i: (0, base + j))


def _full_spec(a):
    nd = a.ndim
    return pl.BlockSpec(a.shape, lambda j, i: (0,) * nd)


def _rowwise(name, f, tiles, ctiles, params, consts, outs, *, tm, J=1):
    T = tiles[0][0].shape[0]
    nt, nct, npar, nc = len(tiles), len(ctiles), len(params), len(consts)

    def body(*refs):
        tv = [r[...].astype(f32) for r in refs[:nt + nct]]
        pv = [r[...] for r in refs[nt + nct:nt + nct + npar + nc]]
        res = f(*tv, *pv)
        for o, v in zip(refs[nt + nct + npar + nc:], res):
            o[...] = v.astype(o.dtype)

    in_specs = ([_tile_spec(tm, w, b) for _, w, b in tiles + ctiles] + [_par_spec(w, b) for _, w, b in params]
                + [_full_spec(c) for c in consts])
    return pl.pallas_call(
        body, name=name, grid=(J, T // tm), in_specs=in_specs,
        out_specs=[_tile_spec(tm, w, 0) for w, _ in outs],
        out_shape=[S((T, J * w), dt) for w, dt in outs],
        compiler_params=_params(("parallel", "parallel")))(
            *[a for a, _, _ in tiles + ctiles], *[a for a, _, _ in params], *consts)


def _rowwise_bwd(name, f, tiles, ctiles, params, consts, douts, dtile_dtypes, *, tm, J=1):
    T = tiles[0][0].shape[0]
    nt, nct, npar, nc, nd = len(tiles), len(ctiles), len(params), len(consts), len(douts)

    def body(*refs):
        i = pl.program_id(1)
        k = 0
        tv = [r[...].astype(f32) for r in refs[k:k + nt]]; k += nt
        cv = [r[...].astype(f32) for r in refs[k:k + nct]]; k += nct
        pv = [r[...] for r in refs[k:k + npar]]; k += npar
        kv = [r[...] for r in refs[k:k + nc]]; k += nc
        dv = [r[...].astype(f32) for r in refs[k:k + nd]]; k += nd
        dt_refs = refs[k:k + nt]; k += nt
        dp_refs = refs[k:k + npar]
        _, vjp = jax.vjp(lambda tv_, pv_: tuple(f(*tv_, *cv, *pv_, *kv)), tv, pv)
        dts, dps = vjp(tuple(dv))
        for r, g in zip(dt_refs, dts):
            r[...] = g.astype(r.dtype)
        for r, g in zip(dp_refs, dps):
            @pl.when(i == 0)
            def _(r=r, g=g):
                r[...] = g

            @pl.when(i > 0)
            def _(r=r, g=g):
                r[...] += g

    in_specs = ([_tile_spec(tm, w, b) for _, w, b in tiles + ctiles] + [_par_spec(w, b) for _, w, b in params]
                + [_full_spec(c) for c in consts] + [_tile_spec(tm, w, b) for _, w, b in douts])
    res = pl.pallas_call(
        body, name=name, grid=(J, T // tm), in_specs=in_specs,
        out_specs=[_tile_spec(tm, w, 0) for _, w, _ in tiles] + [_par_spec(w, b) for _, w, b in params],
        out_shape=[S((T, J * w), dt) for (_, w, _), dt in zip(tiles, dtile_dtypes)] + [S(a.shape, f32) for a, _, _ in params],
        compiler_params=_params(("parallel", "arbitrary")))(
            *[a for a, _, _ in tiles + ctiles], *[a for a, _, _ in params], *consts, *[a for a, _, _ in douts])
    return res[:nt], res[nt:]


def _rms(x, g):
    return x * lax.rsqrt(jnp.mean(x * x, axis=-1, keepdims=True) + EPS) * g


def _f_first(x, g1):
    return (_rms(x, g1),)


def _f_first_bwd(x, g1):
    return (x, _rms(x, g1))


def _f_mid(x, m, g2, g3):
    x1 = x + _rms(m, g2)
    return (x1, _rms(x1, g3))


def _f_end(x1, fo, g4, g1n):
    x2 = x1 + _rms(fo, g4)
    return (x2, _rms(x2, g1n))


def _f_last(x1, fo, g4):
    return (x1 + _rms(fo, g4),)


@jax.custom_vjp
def _swap_halves(x):
    c = x.shape[1]
    lane = lax.broadcasted_iota(jnp.int32, x.shape, 1) % RDH
    return jnp.where(lane < RDH // 2, pltpu.roll(x, c - RDH // 2, axis=1), pltpu.roll(x, RDH // 2, axis=1))


_swap_halves.defvjp(lambda x: (_swap_halves(x), None), lambda _, g: (_swap_halves(g),))


def _f_rprep(rq, rk, cos, sin):
    rot = lambda t: t * cos + _swap_halves(t) * sin
    return (rot(rq), rot(rk) * (RDH ** -0.5))


def _split3(x):
    h1 = x.astype(bf16)
    r1 = x - h1.astype(f32)
    h2 = r1.astype(bf16)
    return h1, h2, (r1 - h2.astype(f32)).astype(bf16)


@jax.custom_vjp
def _dot_sel(x, m):
    mb = m.astype(bf16)
    h1, h2, h3 = _split3(x)
    return jnp.dot(h1, mb, preferred_element_type=f32) + jnp.dot(h2, mb, preferred_element_type=f32) + jnp.dot(h3, mb, preferred_element_type=f32)


def _dot_sel_bwd(m, g):
    mb = m.astype(bf16)
    g1, g2, g3 = _split3(g)
    nt = lambda a: lax.dot_general(a, mb, NT, preferred_element_type=f32)
    return nt(g1) + nt(g2) + nt(g3), jnp.zeros_like(m)


_dot_sel.defvjp(lambda x, m: (_dot_sel(x, m), m), _dot_sel_bwd)


def _f_rpost(y, rg, gn, gavg):
    mu = _dot_sel(y, gavg)
    yc = y - mu
    var = _dot_sel(yc * yc, gavg)
    return (jax.nn.silu(rg) * (yc * lax.rsqrt(var + EPS) * gn),)


def _f_sprep(xs, dtr, dtb, alog, ex0, ex1):
    dt = jax.nn.softplus(dtr + dtb)
    la = dt * (-jnp.exp(alog))
    return (xs * _dot_sel(dt, ex0), xs * _dot_sel(dt, ex1), la)


def _f_sprep_bwd(xs, dtr, dtb, alog, ex0, ex1):
    return _f_sprep(xs, dtr, dtb, alog, ex0, ex1) + (xs,)


def _f_spost(y, xs, z, dsk, ng):
    y = (y + xs * dsk) * jax.nn.silu(z)
    y = y * lax.rsqrt(jnp.mean(y * y, axis=-1, keepdims=True) + EPS)
    return (y * ng,)


def _loss_call(y, tgt, *, tm=ROW_TILE):
    T = y.shape[0]

    def body(y_ref, t_ref, dy_ref, l_ref):
        i = pl.program_id(0)
        e = y_ref[...] - t_ref[...]
        dy_ref[...] = e * (1.0 / D)
        part = jnp.zeros((8, LANES), f32) + 0.5 * jnp.sum(jnp.mean(e * e, axis=-1, keepdims=True))

        @pl.when(i == 0)
        def _():
            l_ref[...] = part

        @pl.when(i > 0)
        def _():
            l_ref[...] += part

    return pl.pallas_call(
        body, name="loss_head", grid=(T // tm,),
        in_specs=[pl.BlockSpec((tm, D), lambda i: (i, 0))] * 2,
        out_specs=[pl.BlockSpec((tm, D), lambda i: (i, 0)), pl.BlockSpec((8, LANES), lambda i: (0, 0))],
        out_shape=[S((T, D), f32), S((8, LANES), f32)], compiler_params=_params(("arbitrary",)))(y, tgt)


def _colsum(x, *, name, tm=512):
    T, C = x.shape

    def body(x_ref, o_ref):
        i = pl.program_id(0)
        part = jnp.sum(x_ref[...], axis=0, keepdims=True)

        @pl.when(i == 0)
        def _():
            o_ref[...] = part

        @pl.when(i > 0)
        def _():
            o_ref[...] += part

    return pl.pallas_call(
        body, name=name, grid=(T // tm,), in_specs=[pl.BlockSpec((tm, C), lambda i: (i, 0))],
        out_specs=pl.BlockSpec((1, C), lambda i: (0, 0)), out_shape=S((1, C), f32),
        compiler_params=_params(("arbitrary",)))(x)


def _nn(a, b):
    if a.ndim == 3:
        return lax.dot_general(a, b, (((2,), (1,)), ((0,), (0,))), preferred_element_type=f32)
    return jnp.dot(a, b, preferred_element_type=f32)


def _nt(a, b):
    if a.ndim == 3:
        return lax.dot_general(a, b, (((2,), (2,)), ((0,), (0,))), preferred_element_type=f32)
    return lax.dot_general(a, b, NT, preferred_element_type=f32)


def _lift(x, like):
    return jnp.broadcast_to(x[None], like.shape[:1] + x.shape) if x.ndim < like.ndim else x


def _drop(g, like):
    return jnp.sum(g, axis=0) if like.ndim < g.ndim else g


@jax.custom_vjp
def _mm_lt(a, a_t, b):
    return _nn(_lift(a_t, b), b)


_mm_lt.defvjp(lambda a, a_t, b: (_nn(_lift(a_t, b), b), (a, b)),
              lambda res, g: (jnp.zeros_like(res[0]), _drop(_nt(g, res[1]), res[0]), _nn(_lift(res[0], g), g)))


@jax.custom_vjp
def _mm_rt(a, a_t, b):
    return _nn(_lift(a, b), b)


_mm_rt.defvjp(lambda a, a_t, b: (_nn(_lift(a, b), b), (a_t, b)),
              lambda res, g: (_drop(_nt(g, res[1]), res[0]), jnp.zeros_like(res[0]), _nn(_lift(res[0], g), g)))


@jax.custom_vjp
def _masked_mm(s, s_t, d, d_t, v):
    return _nn(s * d, v)


def _masked_mm_bwd(res, g):
    s, s_t, d, d_t, v = res
    da = _nt(g, v)
    return (_drop(da * d, s), jnp.zeros_like(s_t), da * s, jnp.zeros_like(d_t), _nn(s_t * d_t, g))


_masked_mm.defvjp(lambda s, s_t, d, d_t, v: (_nn(s * d, v), (s, s_t, d, d_t, v)), _masked_mm_bwd)


def _t(x):
    return jnp.swapaxes(x, -1, -2)


@jax.custom_vjp
def _cumsums(a, tri, tri_t):
    pieces = _split3(a)
    cs = sum(jnp.dot(tri, p, preferred_element_type=f32) for p in pieces)
    cs_t = sum(lax.dot_general(p, tri_t, TN, preferred_element_type=f32) for p in pieces)
    return cs, cs_t


def _cumsums_bwd(res, g):
    tri, tri_t = res
    g_cs, g_cs_t = g
    da = sum(jnp.dot(tri_t, p, preferred_element_type=f32) for p in _split3(g_cs))
    da = da + sum(lax.dot_general(tri_t, p, NT, preferred_element_type=f32) for p in _split3(g_cs_t))
    return da, jnp.zeros_like(tri), jnp.zeros_like(tri_t)


_cumsums.defvjp(lambda a, tri, tri_t: (_cumsums(a, tri, tri_t), (tri, tri_t)), _cumsums_bwd)


def _scan_step_heads(h, q, k, v, a, rev, for_vjp=False):
    B, L, P = v.shape
    ii = lax.broadcasted_iota(jnp.int32, (L, L), 0)
    jj = lax.broadcasted_iota(jnp.int32, (L, L), 1)
    if rev:
        tri, tri_t, dmask, dmask_t = (jj >= ii), (ii >= jj), (jj > ii), (ii > jj)
    else:
        tri, tri_t, dmask, dmask_t = (jj <= ii), (ii <= jj), (jj <= ii), (ii <= jj)
    cs, cs_t = _cumsums(a, tri.astype(bf16), tri_t.astype(bf16))
    tot = jnp.sum(a, axis=0, keepdims=True)
    c_col = jnp.stack([jnp.broadcast_to(cs[:, b:b + 1], (L, L)) for b in range(B)])
    c_row = jnp.stack([cs_t[b:b + 1, :] for b in range(B)])
    t_all = jnp.stack([tot[:, b:b + 1] for b in range(B)])
    dec = jnp.exp(jnp.where(dmask[None], c_col - c_row, -1e30))
    e_in, e_out = jnp.exp(c_col)[:, :, :P], jnp.exp(t_all - c_col)[:, :, :P]
    qk = _nt(q, k)
    k_t = _t(k)
    w = v * e_out
    if for_vjp:
        q_t = lax.stop_gradient(_t(q))
        qk_t = lax.stop_gradient(_nt(k, q))
        dec_t = lax.stop_gradient(jnp.exp(jnp.where(dmask_t[None], c_row - c_col, -1e30)))
        y = _masked_mm(qk, qk_t, dec, dec_t, v) + _mm_rt(q, q_t, h) * e_in
        hn = h * jnp.exp(t_all) + _mm_lt(lax.stop_gradient(k), k_t, w)
    else:
        y = _nn(qk * dec, v) + _nn(_lift(q, h), h) * e_in
        hn = h * jnp.exp(t_all) + _nn(_lift(k_t, w), w)
    return hn, y


def _scan_specs(gb, N, Hg, P, Ha, cm, qcol, kcol, vcol):
    qs = lambda col: pl.BlockSpec((CHUNK, gb * N), lambda g, c: (cm(c), col + g))
    vs = lambda col: pl.BlockSpec((CHUNK, gb * Hg * P), lambda g, c: (cm(c), col + g))
    as_ = pl.BlockSpec((1, CHUNK, Ha), lambda g, c: (g, cm(c), 0))
    hs = pl.BlockSpec((gb, 1, Hg, N, P), lambda g, c: (g, cm(c), 0, 0, 0))
    return qs(qcol), qs(kcol), vs(vcol), qs(0), vs(0), as_, hs


def _lanes(ref, n, width):
    return jnp.stack([ref[:, j * width:(j + 1) * width] for j in range(n)])


def _scan_fwd(q, k, v, a, *, G, N, Hg, P, qcol=0, kcol=0, vcol=0, rev, name, add_y=None):
    T, Ha, NC = q.shape[0], a.shape[2], q.shape[0] // CHUNK
    gb = SCAN_HEADS_PER_STEP // Hg
    cm = (lambda c: NC - 1 - c) if rev else (lambda c: c)
    qs, ks, vs, _, ys, as_, hs = _scan_specs(gb, N, Hg, P, Ha, cm, qcol, kcol, vcol)
    extra = [] if add_y is None else [add_y]

    def body(q_ref, k_ref, v_ref, a_ref, *rest):
        y_ref, hs_ref, h_scr = rest[len(extra):]

        @pl.when(pl.program_id(1) == 0)
        def _():
            h_scr[...] = jnp.zeros_like(h_scr)

        if Hg == 1:
            h = h_scr[:, 0]
            hs_ref[:, 0, 0] = h
            hn, y = _scan_step_heads(h, _lanes(q_ref, gb, N), _lanes(k_ref, gb, N), _lanes(v_ref, gb, P), a_ref[0], rev)
            h_scr[:, 0] = hn
        else:
            h = h_scr[0]
            hs_ref[0, 0] = h
            hn, y = _scan_step_heads(h, q_ref[...], k_ref[...], _lanes(v_ref, Hg, P), a_ref[0], rev)
            h_scr[0] = hn
        for j in range(gb * Hg):
            cols = slice(j * P, (j + 1) * P)
            y_ref[:, cols] = y[j] if add_y is None else y[j] + rest[0][:, cols]

    return pl.pallas_call(
        body, name=name, grid=(G // gb, NC), in_specs=[qs, ks, vs, as_] + [ys] * len(extra), out_specs=[ys, hs],
        out_shape=[S((T, G * Hg * P), f32), S((G, NC, Hg, N, P), f32)],
        scratch_shapes=[pltpu.VMEM((gb, Hg, N, P), f32)],
        compiler_params=_params(("parallel", "arbitrary")))(q, k, v, a, *extra)


def _scan_bwd(q, k, v, a, hsave, dy, *, G, N, Hg, P, qcol=0, kcol=0, vcol=0, rev, name, add_to=(None, None, None)):
    T, Ha, NC = q.shape[0], a.shape[2], q.shape[0] // CHUNK
    gb = SCAN_HEADS_PER_STEP // Hg
    cm = (lambda c: c) if rev else (lambda c: NC - 1 - c)
    qs, ks, vs, dqs, dvs, as_, hs = _scan_specs(gb, N, Hg, P, Ha, cm, qcol, kcol, vcol)
    extra = [(x, s) for x, s in zip(add_to, (dqs, dqs, dvs)) if x is not None]

    def body(q_ref, k_ref, v_ref, a_ref, hs_ref, dy_ref, *rest):
        dq_ref, dk_ref, dv_ref, da_ref, dh_scr = rest[len(extra):]
        prev = iter(rest[:len(extra)])
        pq, pk, pv = [next(prev) if x is not None else None for x in add_to]

        @pl.when(pl.program_id(1) == 0)
        def _():
            dh_scr[...] = jnp.zeros_like(dh_scr)

        if Hg == 1:
            _, vjp = jax.vjp(functools.partial(_scan_step_heads, rev=rev, for_vjp=True), hs_ref[:, 0, 0], _lanes(q_ref, gb, N),
                             _lanes(k_ref, gb, N), _lanes(v_ref, gb, P), a_ref[0])
            dh, dq, dk, dv, da = vjp((dh_scr[:, 0], _lanes(dy_ref, gb, P)))
            dh_scr[:, 0] = dh
            for j in range(gb):
                cols = slice(j * N, (j + 1) * N)
                dq_ref[:, cols] = dq[j] if pq is None else dq[j] + pq[:, cols]
                dk_ref[:, cols] = dk[j] if pk is None else dk[j] + pk[:, cols]
        else:
            _, vjp = jax.vjp(functools.partial(_scan_step_heads, rev=rev, for_vjp=True), hs_ref[0, 0], q_ref[...], k_ref[...],
                             _lanes(v_ref, Hg, P), a_ref[0])
            dh, dq, dk, dv, da = vjp((dh_scr[0], _lanes(dy_ref, Hg, P)))
            dh_scr[0] = dh
            dq_ref[...] = dq if pq is None else dq + pq[...]
            dk_ref[...] = dk if pk is None else dk + pk[...]
        for j in range(gb * Hg):
            cols = slice(j * P, (j + 1) * P)
            dv_ref[:, cols] = dv[j] if pv is None else dv[j] + pv[:, cols]
        da_ref[0] = da

    return pl.pallas_call(
        body, name=name, grid=(G // gb, NC), in_specs=[qs, ks, vs, as_, hs, dvs] + [s for _, s in extra],
        out_specs=[dqs, dqs, dvs, as_],
        out_shape=[S((T, G * N), f32), S((T, G * N), f32), S((T, G * Hg * P), f32), S(a.shape, f32)],
        scratch_shapes=[pltpu.VMEM((gb, Hg, N, P), f32)],
        compiler_params=_params(("parallel", "arbitrary")))(q, k, v, a, hsave, dy, *[x for x, _ in extra])


def _na_block_case(rb, nrb):
    return jnp.where(rb == 0, 0, jnp.where(rb == nrb - 1, 2, 1))


def _na_key_start(rb, rows):
    return pl.multiple_of(jnp.clip(rb * NA_QROWS - NA_WR // 2, 0, rows - NA_KROWS) * GRID_W, 256)


def _na_specs(T, nrb):
    nq, nk, wb = NA_QROWS * GRID_W, NA_KROWS * GRID_W, NA_PAIR * NADH
    qs = lambda col: pl.BlockSpec((nq, wb), lambda p, r: (r, col + p))
    fs = lambda col: pl.BlockSpec((T, wb), lambda p, r: (0, col + p))
    bs = pl.BlockSpec((NA_PAIR, 1, nq, nk), lambda p, r: (p, _na_block_case(r, nrb), 0, 0))
    ls = pl.BlockSpec((1, nq, NA_PAIR), lambda p, r: (p, r, 0))
    return qs, fs, bs, ls


def _na_fwd(q, k, v, bias, *, qcol, kcol, vcol, name):
    T = q.shape[0]
    rows = T // GRID_W
    nq, nk = NA_QROWS * GRID_W, NA_KROWS * GRID_W
    nrb = T // nq
    scale = NADH ** -0.5
    qs, fs, bs, ls = _na_specs(T, nrb)

    def body(q_ref, k_ref, v_ref, b_ref, o_ref, l_ref):
        ks = _na_key_start(pl.program_id(1), rows)
        for hh in range(NA_PAIR):
            sl = slice(hh * NADH, (hh + 1) * NADH)
            kw = k_ref[pl.ds(ks, nk), sl]
            vw = v_ref[pl.ds(ks, nk), sl]
            s = lax.dot_general(q_ref[:, sl], kw, NT, preferred_element_type=f32) * scale + b_ref[hh, 0]
            m = jnp.max(s, axis=1, keepdims=True)
            p = jnp.exp(s - m)
            l = jnp.sum(p, axis=1, keepdims=True)
            o_ref[:, sl] = jnp.dot(p.astype(bf16), vw, preferred_element_type=f32) / l
            l_ref[0, :, hh:hh + 1] = m + jnp.log(l)

    return pl.pallas_call(
        body, name=name, grid=(NAH // NA_PAIR, nrb), in_specs=[qs(qcol), fs(kcol), fs(vcol), bs],
        out_specs=[qs(0), ls], out_shape=[S((T, NAW), f32), S((NAH // NA_PAIR, T, NA_PAIR), f32)],
        compiler_params=_params(("parallel", "arbitrary")))(q, k, v, bias)


def _na_bwd(q, k, v, bias, o, lse, do, *, qcol, kcol, vcol, docol, name):
    T = q.shape[0]
    rows = T // GRID_W
    nq, nk = NA_QROWS * GRID_W, NA_KROWS * GRID_W
    nrb = T // nq
    scale = NADH ** -0.5
    qs, fs, bs, ls = _na_specs(T, nrb)

    def body(q_ref, k_ref, v_ref, b_ref, o_ref, l_ref, do_ref, dq_ref, dk_ref, dv_ref, db_ref):
        rb = pl.program_id(1)

        @pl.when(rb == 0)
        def _():
            dk_ref[...] = jnp.zeros_like(dk_ref)
            dv_ref[...] = jnp.zeros_like(dv_ref)

        ks = _na_key_start(rb, rows)
        first = (rb == 0) | (rb == 1) | (rb == nrb - 1)
        for hh in range(NA_PAIR):
            sl = slice(hh * NADH, (hh + 1) * NADH)
            qv = q_ref[:, sl]
            kw = k_ref[pl.ds(ks, nk), sl]
            vw = v_ref[pl.ds(ks, nk), sl]
            s = lax.dot_general(qv, kw, NT, preferred_element_type=f32) * scale + b_ref[hh, 0]
            p = jnp.exp(s - l_ref[0, :, hh:hh + 1])
            do_ = do_ref[:, sl]
            dob = do_.astype(bf16)
            dp = lax.dot_general(dob, vw, NT, preferred_element_type=f32)
            ds = p * (dp - jnp.sum(do_ * o_ref[:, sl], axis=1, keepdims=True))
            dsb = ds.astype(bf16)
            dq_ref[:, sl] = jnp.dot(dsb, kw, preferred_element_type=f32) * scale
            dk_ref[pl.ds(ks, nk), sl] += lax.dot_general(dsb, qv, TN, preferred_element_type=f32) * scale
            dv_ref[pl.ds(ks, nk), sl] += lax.dot_general(p.astype(bf16), dob, TN, preferred_element_type=f32)

            @pl.when(first)
            def _(hh=hh, ds=ds):
                db_ref[hh, 0] = ds

            @pl.when(jnp.logical_not(first))
            def _(hh=hh, ds=ds):
                db_ref[hh, 0] += ds

    return pl.pallas_call(
        body, name=name, grid=(NAH // NA_PAIR, nrb),
        in_specs=[qs(qcol), fs(kcol), fs(vcol), bs, qs(0), ls, qs(docol)],
        out_specs=[qs(0), fs(0), fs(0), bs],
        out_shape=[S((T, NAW), f32), S((T, NAW), f32), S((T, NAW), f32), S(bias.shape, f32)],
        compiler_params=_params(("parallel", "arbitrary")))(q, k, v, bias, o, lse, do)


def _na_col_tables():
    c = np.arange(GRID_W)[:, None]
    kc = np.arange(GRID_W)[None, :]
    cstart = np.clip(c - NA_WC // 2, 0, GRID_W - NA_WC)
    valid_c = (kc >= cstart) & (kc < cstart + NA_WC)
    dc = kc - c + NA_WC - 1
    E = (valid_c[:, :, None] & (dc[:, :, None] == np.arange(2 * NA_WC - 1)[None, None, :])).astype(np.float32)
    return E, np.where(valid_c, 0.0, -1e30).astype(np.float32)


def _na_row_offsets(rows):
    table = []
    for r0 in (0, NA_QROWS, rows - NA_QROWS):
        ks = int(np.clip(r0 - NA_WR // 2, 0, rows - NA_KROWS))
        case = []
        for ri in range(NA_QROWS):
            r = r0 + ri
            rs = int(np.clip(r - NA_WR // 2, 0, rows - NA_WR))
            case.append([ks + kri - r + NA_WR - 1 if rs <= ks + kri < rs + NA_WR else None for kri in range(NA_KROWS)])
        table.append(case)
    return table


def _na_col_bias(rpb):
    E, cmask = _na_col_tables()
    return jnp.einsum("hde,cke->hdck", rpb, E, precision=HI) + cmask


def _na_bias_build(r1, rows, *, name):
    H = r1.shape[0]
    offs = _na_row_offsets(rows)

    def body(r_ref, o_ref):
        outside = jnp.full((GRID_W, GRID_W), -1e30, f32)
        for z in range(3):
            for a in range(NA_QROWS):
                for b in range(NA_KROWS):
                    d = offs[z][a][b]
                    o_ref[0, z, a * GRID_W:(a + 1) * GRID_W, b * GRID_W:(b + 1) * GRID_W] = outside if d is None else r_ref[0, d]

    return pl.pallas_call(
        body, name=name, grid=(H,), in_specs=[pl.BlockSpec((1,) + r1.shape[1:], lambda h: (h, 0, 0, 0))],
        out_specs=pl.BlockSpec((1, 3, NA_QROWS * GRID_W, NA_KROWS * GRID_W), lambda h: (h, 0, 0, 0)),
        out_shape=S((H, 3, NA_QROWS * GRID_W, NA_KROWS * GRID_W), f32), compiler_params=_params(("parallel",)))(r1)


def _na_bias_fold(dbias, rows, *, name):
    H = dbias.shape[0]
    offs = _na_row_offsets(rows)
    nd = 2 * NA_WR - 1

    def body(d_ref, o_ref):
        acc = [None] * nd
        for z in range(3):
            for a in range(NA_QROWS):
                for b in range(NA_KROWS):
                    d = offs[z][a][b]
                    if d is not None:
                        t = d_ref[0, z, a * GRID_W:(a + 1) * GRID_W, b * GRID_W:(b + 1) * GRID_W]
                        acc[d] = t if acc[d] is None else acc[d] + t
        for d in range(nd):
            o_ref[0, d] = acc[d]

    return pl.pallas_call(
        body, name=name, grid=(H,), in_specs=[pl.BlockSpec((1,) + dbias.shape[1:], lambda h: (h, 0, 0, 0))],
        out_specs=pl.BlockSpec((1, nd, GRID_W, GRID_W), lambda h: (h, 0, 0, 0)),
        out_shape=S((H, nd, GRID_W, GRID_W), f32), compiler_params=_params(("parallel",)))(dbias)


def _conv_shifts(prev, cur, nxt, i, n_i, W):
    tm = cur.shape[0]
    prev = jnp.where(i > 0, prev, 0.0)
    nxt = jnp.where(i < n_i - 1, nxt, 0.0)
    ext = jnp.concatenate([prev, cur, nxt], axis=0)
    out = []
    for w in range(W):
        s = (W // 2 - w) % (tm + 16)
        out.append((ext if s == 0 else pltpu.roll(ext, s, axis=0))[8:8 + tm])
    return out


def _conv_act(u, mode):
    if mode == "silu":
        return jax.nn.silu(u)
    assert mode == "geglu"
    half = u.shape[1] // 2
    return jax.nn.gelu(u[:, :half], approximate=True) * u[:, half:]


def _conv_specs(T, tm, tc, xbase):
    r8 = tm // 8
    last = T // 8 - 1
    cur = pl.BlockSpec((tm, tc), lambda j, i: (i, xbase + j))
    prev = pl.BlockSpec((8, tc), lambda j, i: (jnp.maximum(i * r8 - 1, 0), xbase + j))
    nxt = pl.BlockSpec((8, tc), lambda j, i: (jnp.minimum((i + 1) * r8, last), xbase + j))
    return cur, prev, nxt


def _conv(x, w8, b, *, mode, W, name, C, xbase=0, tm=CONV_TILE, tc=512, out_dtype=f32):
    T = x.shape[0]
    NI, J = T // tm, C // tc
    tco = tc // 2 if mode == "geglu" else tc
    cur, prev, nxt = _conv_specs(T, tm, tc, xbase)

    def body(xc, xp, xn, w_ref, b_ref, o_ref):
        sh = _conv_shifts(xp[...].astype(f32), xc[...].astype(f32), xn[...].astype(f32), pl.program_id(1), NI, W)
        wv = w_ref[...]
        u = sh[0] * wv[0:1, :]
        for w in range(1, W):
            u = u + sh[w] * wv[w:w + 1, :]
        o_ref[...] = _conv_act(u + b_ref[...], mode).astype(o_ref.dtype)

    return pl.pallas_call(
        body, name=name, grid=(J, NI),
        in_specs=[cur, prev, nxt, pl.BlockSpec((8, tc), lambda j, i: (0, j)), pl.BlockSpec((1, tc), lambda j, i: (0, j))],
        out_specs=pl.BlockSpec((tm, tco), lambda j, i: (i, j)), out_shape=S((T, J * tco), out_dtype),
        compiler_params=_params(("parallel", "parallel")))(x, x, x, w8, b)


def _conv_bwd(x, w8, b, dact, *, mode, W, name, C, xbase=0, tm=CONV_TILE, tc=512):
    T = x.shape[0]
    NI, J = T // tm, C // tc
    tco = tc // 2 if mode == "geglu" else tc
    rows = tm + 16
    pad = W // 2
    cur, prev, nxt = _conv_specs(T, tm, tc, xbase)
    dcur, dprev, dnxt = _conv_specs(T, tm, tco, 0)

    def body(xc, xp, xn, w_ref, b_ref, dc, dp, dn, dx_ref, dw_ref, db_ref):
        i = pl.program_id(1)
        ext = jnp.concatenate([jnp.where(i > 0, xp[...], 0.0), xc[...], jnp.where(i < NI - 1, xn[...], 0.0)], axis=0)
        dext = jnp.concatenate([jnp.where(i > 0, dp[...], 0.0), dc[...], jnp.where(i < NI - 1, dn[...], 0.0)], axis=0)
        wv = w_ref[...]
        shift = lambda t, w: t if w == pad else pltpu.roll(t, (pad - w) % rows, axis=0)
        xs = [shift(ext, w) for w in range(W)]
        u = b_ref[...] + xs[0] * wv[0:1, :]
        for w in range(1, W):
            u = u + xs[w] * wv[w:w + 1, :]
        _, vjp = jax.vjp(functools.partial(_conv_act, mode=mode), u)
        du = vjp(dext.astype(f32))[0]
        dx = shift(du, 0)[8:8 + tm] * wv[W - 1:W, :]
        for w in range(1, W):
            dx = dx + shift(du, w)[8:8 + tm] * wv[W - 1 - w:W - w, :]
        dx_ref[...] = dx.astype(dx_ref.dtype)

        @pl.when(i == 0)
        def _():
            dw_ref[...] = jnp.zeros_like(dw_ref)
            db_ref[...] = jnp.zeros_like(db_ref)

        dum = du[8:8 + tm]
        db_ref[...] += jnp.sum(dum, axis=0, keepdims=True)
        for w in range(W):
            dw_ref[w:w + 1, :] += jnp.sum(dum * xs[w][8:8 + tm], axis=0, keepdims=True)

    return pl.pallas_call(
        body, name=name, grid=(J, NI),
        in_specs=[cur, prev, nxt, pl.BlockSpec((8, tc), lambda j, i: (0, j)), pl.BlockSpec((1, tc), lambda j, i: (0, j)),
                  dcur, dprev, dnxt],
        out_specs=[pl.BlockSpec((tm, tc), lambda j, i: (i, j)), pl.BlockSpec((8, tc), lambda j, i: (0, j)),
                   pl.BlockSpec((1, tc), lambda j, i: (0, j))],
        out_shape=[S((T, C), bf16), S((8, C), f32), S((1, C), f32)],
        compiler_params=_params(("parallel", "arbitrary")))(x, x, x, w8, b, dact, dact, dact)


def _pad8(w):
    return jnp.concatenate([w, jnp.zeros((8 - w.shape[0], w.shape[1]), w.dtype)], axis=0)


def _all_gather(arrs, *, name):
    n = len(arrs)

    def body(*refs):
        ins, outs = refs[:n], refs[n:2 * n]
        send_sems, recv_sems, loc_sems = refs[2 * n:]
        x, y, c = lax.axis_index("x"), lax.axis_index("y"), lax.axis_index("c")
        ident = lambda px, py, pc: 4 * px + 2 * py + pc
        me, sibling = (x, y, c), (x, y, 1 - c)
        chips = [(1 - x, y), (x, 1 - y), (1 - x, 1 - y)]

        def copy(a, k, block, to, src=None):
            slot = outs[a].at[ident(*block)]
            return pltpu.make_async_remote_copy(
                src_ref=slot if src is None else src, dst_ref=slot, send_sem=send_sems.at[a * 7 + k], recv_sem=recv_sems.at[a * 7 + k],
                device_id=to, device_id_type=pl.DeviceIdType.MESH)

        local = [pltpu.make_async_copy(ins[a], outs[a].at[ident(*me)], loc_sems.at[a]) for a in range(n)]
        for cp in local:
            cp.start()
        first = []
        for a in range(n):
            first.append(copy(a, 0, me, sibling, src=ins[a]))
            first += [copy(a, 1 + j, me, (*chip, c), src=ins[a]) for j, chip in enumerate(chips)]
        for cp in first:
            cp.start()
        passed = []
        for j, chip in enumerate(chips):
            for a in range(n):
                copy(a, 1 + j, (*chip, c), me).wait_recv()
                fwd = copy(a, 4 + j, (*chip, c), sibling)
                fwd.start()
                passed.append(fwd)
        for a in range(n):
            copy(a, 0, sibling, me).wait_recv()
            for j, chip in enumerate(chips):
                copy(a, 4 + j, (*chip, 1 - c), me).wait_recv()
        for cp in first + passed:
            cp.wait_send()
        for cp in local:
            cp.wait()

    any_spec = pl.BlockSpec(memory_space=pl.ANY)
    return pl.pallas_call(
        body, name=name, in_specs=[any_spec] * n, out_specs=[any_spec] * n,
        out_shape=[S((NDEV,) + a.shape, a.dtype) for a in arrs],
        scratch_shapes=[pltpu.SemaphoreType.DMA((7 * n,)), pltpu.SemaphoreType.DMA((7 * n,)), pltpu.SemaphoreType.DMA((n,))],
        )(*arrs)


NCHIP = NDEV // 2


def _to_sibling(arrs, *, name):
    n = len(arrs)
    ncopy = sum(NCHIP if ps else 1 for _, ps in arrs)

    def body(*refs):
        ins, outs = refs[:n], refs[n:2 * n]
        send_sems, recv_sems = refs[2 * n:]
        x, y, c = lax.axis_index("x"), lax.axis_index("y"), lax.axis_index("c")
        copies, idx = [], 0
        for a, (_, per_slot) in enumerate(arrs):
            pairs = [(ins[a].at[2 * q + (1 - c)], outs[a].at[q]) for q in range(NCHIP)] if per_slot else [(ins[a], outs[a])]
            for src, dst in pairs:
                copies.append(pltpu.make_async_remote_copy(
                    src_ref=src, dst_ref=dst, send_sem=send_sems.at[idx], recv_sem=recv_sems.at[idx],
                    device_id=(x, y, 1 - c), device_id_type=pl.DeviceIdType.MESH))
                idx += 1
        for cp in copies:
            cp.start()
        for cp in copies:
            cp.wait_recv()
        for cp in copies:
            cp.wait_send()

    any_spec = pl.BlockSpec(memory_space=pl.ANY)
    return pl.pallas_call(
        body, name=name, in_specs=[any_spec] * n, out_specs=[any_spec] * n,
        out_shape=[S((NCHIP,) + a.shape[1:] if ps else a.shape, a.dtype) for a, ps in arrs],
        scratch_shapes=[pltpu.SemaphoreType.DMA((ncopy,)), pltpu.SemaphoreType.DMA((ncopy,))])(*[a for a, _ in arrs])


def _add_partials(mine, theirs, *, per_slot, tr, name):
    R, C = mine.shape[-2:]

    def body(a_ref, b_ref, o_ref):
        a = a_ref[lax.axis_index("c")] if per_slot else a_ref[...]
        b = b_ref[0] if per_slot else b_ref[...]
        s = a.astype(f32) + b.astype(f32)
        if per_slot:
            o_ref[0] = s.astype(o_ref.dtype)
        else:
            o_ref[...] = s.astype(o_ref.dtype)

    if per_slot:
        grid = (NCHIP, R // tr)
        in_specs = [pl.BlockSpec((2, tr, C), lambda q, i: (q, i, 0)), pl.BlockSpec((1, tr, C), lambda q, i: (q, i, 0))]
        out_spec, out_shape = pl.BlockSpec((1, tr, C), lambda q, i: (q, i, 0)), S((NCHIP, R, C), mine.dtype)
    else:
        grid = (1, R // tr)
        in_specs = [pl.BlockSpec((tr, C), lambda q, i: (i, 0))] * 2
        out_spec, out_shape = pl.BlockSpec((tr, C), lambda q, i: (i, 0)), S((R, C), mine.dtype)
    return pl.pallas_call(body, name=name, grid=grid, in_specs=in_specs, out_specs=out_spec, out_shape=out_shape,
                          compiler_params=_params(("parallel", "parallel")))(mine, theirs)


def _to_chips(arrs, *, name):
    n = len(arrs)

    def body(*refs):
        ins, outs = refs[:n], refs[n:2 * n]
        send_sems, recv_sems, loc_sems = refs[2 * n:]
        x, y, c = lax.axis_index("x"), lax.axis_index("y"), lax.axis_index("c")
        my_q = 2 * x + y
        src = lambda a, q: ins[a].at[q] if arrs[a][1] else ins[a]
        local = [pltpu.make_async_copy(src(a, my_q), outs[a].at[my_q], loc_sems.at[a]) for a in range(n)]
        for cp in local:
            cp.start()
        sent = []
        for j, (px, py) in enumerate([(1 - x, y), (x, 1 - y), (1 - x, 1 - y)]):
            q = 2 * px + py
            for a in range(n):
                mk = lambda slot, a=a, j=j, q=q, dev=(px, py, c): pltpu.make_async_remote_copy(
                    src_ref=src(a, q), dst_ref=outs[a].at[slot], send_sem=send_sems.at[3 * a + j], recv_sem=recv_sems.at[3 * a + j],
                    device_id=dev, device_id_type=pl.DeviceIdType.MESH)
                mk(my_q).start()
                sent.append((mk, q))
        for mk, q in sent:
            mk(q).wait_recv()
        for mk, q in sent:
            mk(q).wait_send()
        for cp in local:
            cp.wait()

    any_spec = pl.BlockSpec(memory_space=pl.ANY)
    return pl.pallas_call(
        body, name=name, in_specs=[any_spec] * n, out_specs=[any_spec] * n,
        out_shape=[S(a.shape if ps else (NCHIP,) + a.shape, a.dtype) for a, ps in arrs],
        scratch_shapes=[pltpu.SemaphoreType.DMA((3 * n,)), pltpu.SemaphoreType.DMA((3 * n,)), pltpu.SemaphoreType.DMA((n,))],
        )(*[a for a, _ in arrs])


def _adamw(r, w, m, v, *, name, tr):
    M, C = w.shape
    nparts = r.shape[0]

    def body(r_ref, w_ref, m_ref, v_ref, g_ref, d_ref, nm_ref, nv_ref):
        g = r_ref[0].astype(f32)
        for s in range(1, nparts):
            g = g + r_ref[s].astype(f32)
        m_ = B1 * m_ref[...] + (1.0 - B1) * g
        v_ = B2 * v_ref[...] + (1.0 - B2) * jnp.square(g)
        m_hat = m_ / (1.0 - B1 ** STEP)
        v_hat = v_ / (1.0 - B2 ** STEP)
        g_ref[...] = g
        d_ref[...] = -LR * (m_hat / (jnp.sqrt(v_hat) + AEPS) + WD * w_ref[...])
        nm_ref[...] = m_
        nv_ref[...] = v_

    row = pl.BlockSpec((tr, C), lambda i: (i, 0))
    return pl.pallas_call(
        body, name=name, grid=(M // tr,),
        in_specs=[pl.BlockSpec((nparts, tr, C), lambda i: (0, i, 0)), row, row, row],
        out_specs=[row] * 4, out_shape=[S((M, C), f32)] * 4, compiler_params=_params(("parallel",)))(r, w, m, v)


def _colmove(ins, in_slots, outs, moves, *, tk, name):
    R = ins[0].shape[1] if in_slots[0] else ins[0].shape[0]
    n_in = len(ins)

    def body(*refs):
        for ii, isl, ic, oi, osl, oc, w in moves:
            src, dst = refs[ii], refs[n_in + oi]
            val = src[:, ic:ic + w] if isl is None else src[isl, :, ic:ic + w]
            if osl is None:
                dst[:, oc:oc + w] = val.astype(dst.dtype)
            else:
                dst[osl, :, oc:oc + w] = val.astype(dst.dtype)

    def spec(is_slots, C):
        return pl.BlockSpec((NDEV, tk, C), lambda i: (0, i, 0)) if is_slots else pl.BlockSpec((tk, C), lambda i: (i, 0))

    return pl.pallas_call(
        body, name=name, grid=(R // tk,),
        in_specs=[spec(sl, a.shape[-1]) for a, sl in zip(ins, in_slots)],
        out_specs=[spec(sl, C) for sl, C, _ in outs],
        out_shape=[S((NDEV, R, C) if sl else (R, C), dt) for sl, C, dt in outs],
        compiler_params=_params(("parallel",)))(*ins)


def _col_pieces(n8, cuts, place):
    out = []
    for p in range(NDEV):
        lo, hi = p * n8, (p + 1) * n8
        edges = [lo] + [c for c in cuts if lo < c < hi] + [hi]
        for a, b in zip(edges[:-1], edges[1:]):
            out.append((p, a - lo) + place(a) + (b - a,))
    return out


def _place_plain(c):
    return (0, c)


def _place_ssd_in(c):
    return (0, c) if c < SSD_INNER + SSD_XBC else (1, c - (SSD_INNER + SSD_XBC))


def _place_ffn_up(c):
    h = FFN_TC // 2
    return (0, (c // h) * FFN_TC + c % h) if c < FFN else (0, ((c - FFN) // h) * FFN_TC + h + (c - FFN) % h)


_COL_LAYOUTS = {
    "ab_w_in": ([], _place_plain, [4 * RW + 3 * NAW]),
    "c_w_in": ([SSD_INNER + SSD_XBC], _place_ssd_in, [SSD_INNER + SSD_XBC, 2 * SSD_H]),
    "ffn_w_up": (list(range(FFN_TC // 2, 2 * FFN, FFN_TC // 2)), _place_ffn_up, [2 * FFN]),
}


def _cols_from_slots(g, which, *, name):
    cuts, place, widths = _COL_LAYOUTS[which]
    moves = [(0, p, sc, mi, None, mc, w) for p, sc, mi, mc, w in _col_pieces(g.shape[2], cuts, place)]
    return _colmove([g], [True], [(False, w, g.dtype) for w in widths], moves, tk=256, name=name)


def _cols_to_slots(mats, which, dtype, *, name):
    cuts, place, widths = _COL_LAYOUTS[which]
    n8 = sum(widths) // NDEV
    moves = [(mi, None, mc, 0, p, sc, w) for p, sc, mi, mc, w in _col_pieces(n8, cuts, place)]
    return _colmove(list(mats), [False] * len(mats), [(True, n8, dtype)], moves, tk=256, name=name)[0]


def _pack(parts, dtype, row_mult):
    flat = jnp.concatenate([p.reshape(-1).astype(dtype) for p in parts])
    rows = -(-flat.shape[0] // LANES)
    rows = -(-rows // row_mult) * row_mult
    return jnp.pad(flat, (0, rows * LANES - flat.shape[0])).reshape(rows, LANES)


def _unpack(buf, shapes, lead=()):
    flat = buf.reshape(lead + (-1,))
    out, off = [], 0
    for shp in shapes:
        n = int(np.prod(shp))
        out.append(flat[..., off:off + n].reshape(lead + tuple(shp)))
        off += n
    return out


def _to_slots(full, ax):
    shp = full.shape
    return jnp.moveaxis(full.reshape(shp[:ax] + (NDEV, shp[ax] // NDEV) + shp[ax + 1:]), ax, 0)


def _from_slots(g, ax):
    t = jnp.moveaxis(g, 0, ax)
    shp = t.shape
    return t.reshape(shp[:ax] + (shp[ax] * shp[ax + 1],) + shp[ax + 2:])


def _ffn_perm(a):
    lead = a.shape[:-1]
    h = FFN_TC // 2
    return jnp.swapaxes(a.reshape(lead + (2, FFN // h, h)), -3, -2).reshape(lead + (2 * FFN,))


def _ffn_unperm(a):
    lead = a.shape[:-1]
    h = FFN_TC // 2
    return jnp.swapaxes(a.reshape(lead + (FFN // h, 2, h)), -3, -2).reshape(lead + (2 * FFN,))


def _rope_tables(T):
    half = RDH // 2
    inv = 1.0 / (ROPE_BASE ** (jnp.arange(half, dtype=f32) / half))
    ang = jnp.arange(T, dtype=f32)[:, None] * inv[None, :]
    cos, sin = jnp.cos(ang), jnp.sin(ang)
    cos_t = jnp.tile(jnp.concatenate([cos, cos], axis=1), (1, RH))
    sin_t = jnp.tile(jnp.concatenate([-sin, sin], axis=1), (1, RH))
    return cos_t, sin_t


def _group_avg():
    g = np.arange(RW) // RDH
    return jnp.asarray((g[:, None] == g[None, :]).astype(np.float32) / RDH)


def _head_expand():
    hd = np.arange(SSD_INNER) // SSD_HD
    rows = np.arange(2 * SSD_H)
    ex0 = (rows[:, None] == hd[None, :]).astype(np.float32)
    ex1 = (rows[:, None] == SSD_H + hd[None, :]).astype(np.float32)
    return jnp.asarray(ex0), jnp.asarray(ex1)


def kernel(x, norm_mix_pre, norm_mix_post, norm_ffn_pre, norm_ffn_post, ab_w_in, ab_ret_decay_logit, ab_ret_gn_g, ab_na_rpb, ab_w_out, c_w_in, c_conv_w, c_conv_b, c_dt_bias, c_a_log, c_d_skip, c_norm_g, c_w_out, ffn_w_up, ffn_conv_w, ffn_conv_b, ffn_w_down, loss_target, m_norm_mix_pre, m_norm_mix_post, m_norm_ffn_pre, m_norm_ffn_post, m_ab_w_in, m_ab_ret_decay_logit, m_ab_ret_gn_g, m_ab_na_rpb, m_ab_w_out, m_c_w_in, m_c_conv_w, m_c_conv_b, m_c_dt_bias, m_c_a_log, m_c_d_skip, m_c_norm_g, m_c_w_out, m_ffn_w_up, m_ffn_conv_w, m_ffn_conv_b, m_ffn_w_down, v_norm_mix_pre, v_norm_mix_post, v_norm_ffn_pre, v_norm_ffn_post, v_ab_w_in, v_ab_ret_decay_logit, v_ab_ret_gn_g, v_ab_na_rpb, v_ab_w_out, v_c_w_in, v_c_conv_w, v_c_conv_b, v_c_dt_bias, v_c_a_log, v_c_d_skip, v_c_norm_g, v_c_w_out, v_ffn_w_up, v_ffn_conv_w, v_ffn_conv_b, v_ffn_w_down):
    W = dict(norm_mix_pre=norm_mix_pre, norm_mix_post=norm_mix_post, norm_ffn_pre=norm_ffn_pre, norm_ffn_post=norm_ffn_post, ab_w_in=ab_w_in, ab_ret_decay_logit=ab_ret_decay_logit, ab_ret_gn_g=ab_ret_gn_g, ab_na_rpb=ab_na_rpb, ab_w_out=ab_w_out, c_w_in=c_w_in, c_conv_w=c_conv_w, c_conv_b=c_conv_b, c_dt_bias=c_dt_bias, c_a_log=c_a_log, c_d_skip=c_d_skip, c_norm_g=c_norm_g, c_w_out=c_w_out, ffn_w_up=ffn_w_up, ffn_conv_w=ffn_conv_w, ffn_conv_b=ffn_conv_b, ffn_w_down=ffn_w_down)
    Mo = dict(norm_mix_pre=m_norm_mix_pre, norm_mix_post=m_norm_mix_post, norm_ffn_pre=m_norm_ffn_pre, norm_ffn_post=m_norm_ffn_post, ab_w_in=m_ab_w_in, ab_ret_decay_logit=m_ab_ret_decay_logit, ab_ret_gn_g=m_ab_ret_gn_g, ab_na_rpb=m_ab_na_rpb, ab_w_out=m_ab_w_out, c_w_in=m_c_w_in, c_conv_w=m_c_conv_w, c_conv_b=m_c_conv_b, c_dt_bias=m_c_dt_bias, c_a_log=m_c_a_log, c_d_skip=m_c_d_skip, c_norm_g=m_c_norm_g, c_w_out=m_c_w_out, ffn_w_up=m_ffn_w_up, ffn_conv_w=m_ffn_conv_w, ffn_conv_b=m_ffn_conv_b, ffn_w_down=m_ffn_w_down)
    Vo = dict(norm_mix_pre=v_norm_mix_pre, norm_mix_post=v_norm_mix_post, norm_ffn_pre=v_norm_ffn_pre, norm_ffn_post=v_norm_ffn_post, ab_w_in=v_ab_w_in, ab_ret_decay_logit=v_ab_ret_decay_logit, ab_ret_gn_g=v_ab_ret_gn_g, ab_na_rpb=v_ab_na_rpb, ab_w_out=v_ab_w_out, c_w_in=v_c_w_in, c_conv_w=v_c_conv_w, c_conv_b=v_c_conv_b, c_dt_bias=v_c_dt_bias, c_a_log=v_c_a_log, c_d_skip=v_c_d_skip, c_norm_g=v_c_norm_g, c_w_out=v_c_w_out, ffn_w_up=v_ffn_w_up, ffn_conv_w=v_ffn_conv_w, ffn_conv_b=v_ffn_conv_b, ffn_w_down=v_ffn_w_down)
    return _train_step(x[0], loss_target[0], W, Mo, Vo)


def _train_step(x, tgt, W, Mo, Vo):
    T = x.shape[0]
    rows = T // GRID_W

    col = lambda d, n, dt: d[n].reshape(-1, d[n].shape[-1]).astype(dt)
    rows_of = lambda d, dt: jnp.concatenate([col(d, n, dt) for n in ROW_SHARDED], axis=0)
    small = _pack([W[n] for n, _ in SHARDED[N_BIG:]], f32, 8)
    gat = _all_gather([col(W, n, bf16) for n in COL_SHARDED] + [rows_of(W, bf16), small], name="gather_weights")
    per_layer = lambda m: m.reshape(-1, D, m.shape[-1])
    w_ab_in = per_layer(_cols_from_slots(gat[0], "ab_w_in", name="cols_ab_w_in")[0])
    w_zx, w_dt = [per_layer(m) for m in _cols_from_slots(gat[1], "c_w_in", name="cols_c_w_in")]
    w_up = per_layer(_cols_from_slots(gat[2], "ffn_w_up", name="cols_ffn_w_up")[0])
    full, off = {}, 0
    for n in ROW_SHARDED:
        L, r = W[n].shape[0], W[n].shape[1]
        full[n] = jnp.swapaxes(gat[3][:, off:off + L * r].reshape(NDEV, L, r, D), 0, 1).reshape(L, NDEV * r, D)
        off += L * r
    gs = _unpack(gat[4], [W[n].shape for n, _ in SHARDED[N_BIG:]], (NDEV,))
    full.update({n: _from_slots(g, ax) for (n, ax), g in zip(SHARDED[N_BIG:], gs)})
    w_ab_out, w_c_out, w_down = full["ab_w_out"], full["c_w_out"], full["ffn_w_down"]
    c_cw8 = [_pad8(full["c_conv_w"][i]) for i in range(2)]
    c_cb = [full["c_conv_b"][i][None] for i in range(2)]
    c_ng = [full["c_norm_g"][i][None] for i in range(2)]
    f_cw8 = [_pad8(_ffn_perm(full["ffn_conv_w"][l])) for l in range(DEPTH)]
    f_cb = [_ffn_perm(W["ffn_conv_b"][l])[None] for l in range(DEPTH)]

    g1 = [W["norm_mix_pre"][l][None] for l in range(DEPTH)]
    g2 = [W["norm_mix_post"][l][None] for l in range(DEPTH)]
    g3 = [W["norm_ffn_pre"][l][None] for l in range(DEPTH)]
    g4 = [W["norm_ffn_post"][l][None] for l in range(DEPTH)]
    cos_t, sin_t = _rope_tables(T)
    gavg = _group_avg()
    ex0, ex1 = _head_expand()

    def log_gamma(logit):
        return -jax.nn.softplus(-logit)

    def ret_decays(lg):
        return [jnp.broadcast_to(lg[d][None, None, :], (1, T, RH)) for d in range(2)]

    saved = []
    xs_ = x
    hn = _rowwise("norm_first", _f_first, [(x, D, 0)], [], [(g1[0], D, 0)], [], [(D, bf16)], tm=ROW_TILE)[0]
    for l in range(DEPTH):
        i = l // 2
        sv = dict(x=xs_, hn=hn)
        if l % 2 == 0:
            proj = _mm_nn(hn, w_ab_in[i], name=f"ab_in_{l}")
            qr, kr = _rowwise(f"ret_prep_{l}", _f_rprep, [(proj, RW, 0), (proj, RW, 1)], [(cos_t, RW, 0), (sin_t, RW, 0)], [], [],
                              [(RW, f32), (RW, f32)], tm=ROW_TILE)
            lg, lg_vjp = jax.vjp(log_gamma, W["ab_ret_decay_logit"][i])
            a_f, a_b = ret_decays(lg)
            rscan = dict(G=RH, N=RDH, Hg=1, P=RDH, vcol=2)
            yf_t, hsf = _scan_fwd(qr, kr, proj, a_f, rev=False, name=f"ret_scan_f_{l}", **rscan)
            y_t, hsb = _scan_fwd(qr, kr, proj, a_b, rev=True, name=f"ret_scan_b_{l}", add_y=yf_t, **rscan)
            gn = W["ab_ret_gn_g"][i][None]
            ret = _rowwise(f"ret_post_{l}", _f_rpost, [(y_t, RW, 0), (proj, RW, 3)], [], [(gn, RW, 0)], [gavg],
                           [(RW, bf16)], tm=ROW_TILE)[0]
            nqkv = proj[:, 4 * RW:].astype(bf16)
            ncols = dict(qcol=0, kcol=NAW // 128, vcol=2 * NAW // 128)
            r1, bias_vjp = jax.vjp(_na_col_bias, W["ab_na_rpb"][i])
            bias = _na_bias_build(r1, rows, name=f"na_bias_{l}")
            na_o, na_l = _na_fwd(nqkv, nqkv, nqkv, bias, name=f"na_fwd_{l}", **ncols)
            cat = jnp.concatenate([ret, na_o.astype(bf16)], axis=1)
            mo = _mm_nn(cat, w_ab_out[i], name=f"ab_out_{l}")
            sv.update(proj=proj, qr=qr, kr=kr, a_f=a_f, a_b=a_b, hsf=hsf, hsb=hsb, y_t=y_t, gn=gn, rscan=rscan,
                      nqkv=nqkv, ncols=ncols, bias=bias, bias_vjp=bias_vjp, lg_vjp=lg_vjp, na_o=na_o, na_l=na_l, cat=cat)
        else:
            zx = _mm_nn(hn, w_zx[i], name=f"c_in_{l}")
            dtr = _mm_nn(hn, w_dt[i], name=f"c_in_dt_{l}")
            xa = _conv(zx, c_cw8[i], c_cb[i], mode="silu", W=SSD_CONV, name=f"c_conv_{l}", C=SSD_XBC, xbase=SSD_INNER // 512)
            dtb, alog = W["c_dt_bias"][i].reshape(1, 2 * SSD_H), W["c_a_log"][i].reshape(1, 2 * SSD_H)
            vf, vb, la = _rowwise(f"ssd_prep_{l}", _f_sprep, [(xa, SSD_INNER, 0), (dtr, 2 * SSD_H, 0)], [],
                                  [(dtb, 2 * SSD_H, 0), (alog, 2 * SSD_H, 0)], [ex0, ex1],
                                  [(SSD_INNER, f32), (SSD_INNER, f32), (2 * SSD_H, f32)], tm=128)
            a_f = la[:, :SSD_H].reshape(T, SSD_G, SSD_HPG).transpose(1, 0, 2)
            a_b = la[:, SSD_H:].reshape(T, SSD_G, SSD_HPG).transpose(1, 0, 2)
            sscan = dict(G=SSD_G, N=SSD_N, Hg=SSD_HPG, P=SSD_HD, qcol=(SSD_INNER + SSD_G * SSD_N) // SSD_N, kcol=SSD_INNER // SSD_N)
            yf_t, hsf = _scan_fwd(xa, xa, vf, a_f, rev=False, name=f"ssd_scan_f_{l}", **sscan)
            y_t, hsb = _scan_fwd(xa, xa, vb, a_b, rev=True, name=f"ssd_scan_b_{l}", add_y=yf_t, **sscan)
            dsk = jnp.repeat(W["c_d_skip"][i], SSD_HD)[None]
            yo = _rowwise(f"ssd_post_{l}", _f_spost, [(y_t, 512, 0), (xa, 512, 0), (zx, 512, 0)], [],
                          [(dsk, 512, 0), (c_ng[i], 512, 0)], [], [(512, bf16)], tm=ROW_TILE, J=SSD_G)[0]
            mo = _mm_nn(yo, w_c_out[i], name=f"c_out_{l}")
            sv.update(zx=zx, dtr=dtr, xa=xa, dtb=dtb, alog=alog, a_f=a_f, a_b=a_b, vf=vf, vb=vb, sscan=sscan,
                      hsf=hsf, hsb=hsb, y_t=y_t, dsk=dsk, yo=yo)
        x1, hf = _rowwise(f"norm_mid_{l}", _f_mid, [(xs_, D, 0), (mo, D, 0)], [], [(g2[l], D, 0), (g3[l], D, 0)], [],
                          [(D, f32), (D, bf16)], tm=ROW_TILE)
        pre = _mm_nn(hf, w_up[l], name=f"ffn_up_{l}")
        act = _conv(pre, f_cw8[l], f_cb[l], mode="geglu", W=FFN_CONV, name=f"ffn_conv_{l}", C=2 * FFN, tc=FFN_TC, out_dtype=bf16)
        fo = _mm_nn(act, w_down[l], name=f"ffn_down_{l}")
        sv.update(mo=mo, x1=x1, hf=hf, pre=pre, act=act, fo=fo)
        if l < DEPTH - 1:
            xs_, hn = _rowwise(f"norm_end_{l}", _f_end, [(x1, D, 0), (fo, D, 0)], [], [(g4[l], D, 0), (g1[l + 1], D, 0)], [],
                               [(D, f32), (D, bf16)], tm=ROW_TILE)
        else:
            xs_ = _rowwise(f"norm_end_{l}", _f_last, [(x1, D, 0), (fo, D, 0)], [], [(g4[l], D, 0)], [], [(D, f32)], tm=ROW_TILE)[0]
        saved.append(sv)

    dx, lpart = _loss_call(xs_, tgt)
    loss = lax.psum(lpart[0, 0], ("x", "y", "c"))

    G = {n: [None] * W[n].shape[0] for n in WEIGHTS}
    dhn = None
    for l in reversed(range(DEPTH)):
        i = l // 2
        sv = saved[l]
        if l == DEPTH - 1:
            (dx1, dfo), (dg4,) = _rowwise_bwd(f"norm_end_bwd_{l}", _f_last, [(sv["x1"], D, 0), (sv["fo"], D, 0)], [],
                                              [(g4[l], D, 0)], [], [(dx, D, 0)], [f32, bf16], tm=ROW_TILE)
        else:
            (dx1, dfo), (dg4, dg1n) = _rowwise_bwd(f"norm_end_bwd_{l}", _f_end, [(sv["x1"], D, 0), (sv["fo"], D, 0)], [],
                                                   [(g4[l], D, 0), (g1[l + 1], D, 0)], [], [(dx, D, 0), (dhn, D, 0)],
                                                   [f32, bf16], tm=ROW_TILE)
            G["norm_mix_pre"][l + 1] = dg1n[0]
        G["norm_ffn_post"][l] = dg4[0]
        dact = _mm_nt(dfo, w_down[l], name=f"ffn_down_dx_{l}")
        G["ffn_w_down"][l] = _mm_tn(sv["act"], dfo, name=f"ffn_down_dw_{l}")
        dpre, dfw, dfb = _conv_bwd(sv["pre"], f_cw8[l], f_cb[l], dact, mode="geglu", W=FFN_CONV, name=f"ffn_conv_bwd_{l}",
                                   C=2 * FFN, tc=FFN_TC)
        dhf = _mm_nt(dpre, w_up[l], name=f"ffn_up_dx_{l}")
        G["ffn_w_up"][l] = _cols_to_slots([_mm_tn(sv["hf"], dpre, name=f"ffn_up_dw_{l}")], "ffn_w_up", bf16, name=f"slots_ffn_up_{l}")
        G["ffn_conv_w"][l] = _ffn_unperm(dfw[:FFN_CONV])
        G["ffn_conv_b"][l] = _ffn_unperm(dfb[0])
        (dxl, dmo), (dg2, dg3) = _rowwise_bwd(f"norm_mid_bwd_{l}", _f_mid, [(sv["x"], D, 0), (sv["mo"], D, 0)], [],
                                              [(g2[l], D, 0), (g3[l], D, 0)], [], [(dx1, D, 0), (dhf, D, 0)], [f32, bf16], tm=ROW_TILE)
        G["norm_mix_post"][l] = dg2[0]
        G["norm_ffn_pre"][l] = dg3[0]
        if l % 2 == 0:
            dcat = _mm_nt(dmo, w_ab_out[i], name=f"ab_out_dx_{l}")
            G["ab_w_out"][i] = _mm_tn(sv["cat"], dmo, name=f"ab_out_dw_{l}")
            (dy, drg), (dgn,) = _rowwise_bwd(
                f"ret_post_bwd_{l}", _f_rpost, [(sv["y_t"], RW, 0), (sv["proj"], RW, 3)], [],
                [(sv["gn"], RW, 0)], [gavg], [(dcat, RW, 0)], [f32, bf16], tm=ROW_TILE)
            G["ab_ret_gn_g"][i] = dgn[0]
            dqf, dkf, dvf, daf = _scan_bwd(sv["qr"], sv["kr"], sv["proj"], sv["a_f"], sv["hsf"], dy, rev=False,
                                           name=f"ret_scan_f_bwd_{l}", **sv["rscan"])
            dq_t, dk_t, dv_t, dab = _scan_bwd(sv["qr"], sv["kr"], sv["proj"], sv["a_b"], sv["hsb"], dy, rev=True,
                                              name=f"ret_scan_b_bwd_{l}", add_to=(dqf, dkf, dvf), **sv["rscan"])
            drv = dv_t.astype(bf16)
            (drq, drk), _ = _rowwise_bwd(f"ret_prep_bwd_{l}", _f_rprep, [(sv["proj"], RW, 0), (sv["proj"], RW, 1)],
                                         [(cos_t, RW, 0), (sin_t, RW, 0)], [], [], [(dq_t, RW, 0), (dk_t, RW, 0)], [bf16, bf16], tm=ROW_TILE)
            da_cols = jnp.concatenate([daf[0], dab[0]], axis=1)
            dlg = _colsum(da_cols, name=f"ret_decay_sum_{l}").reshape(2, RH)
            G["ab_ret_decay_logit"][i] = sv["lg_vjp"](dlg)[0]
            dnq, dnk, dnv, dbias = _na_bwd(sv["nqkv"], sv["nqkv"], sv["nqkv"], sv["bias"], sv["na_o"], sv["na_l"], dcat,
                                           docol=RW // 128, name=f"na_bwd_{l}", **sv["ncols"])
            G["ab_na_rpb"][i] = sv["bias_vjp"](_na_bias_fold(dbias, rows, name=f"na_bias_fold_{l}"))[0]
            dproj = jnp.concatenate([drq, drk, drv, drg] + [t.astype(bf16) for t in (dnq, dnk, dnv)], axis=1)
            dhn = _mm_nt(dproj, w_ab_in[i], name=f"ab_in_dx_{l}")
            G["ab_w_in"][i] = _cols_to_slots([_mm_tn(sv["hn"], dproj, name=f"ab_in_dw_{l}")], "ab_w_in", bf16, name=f"slots_ab_in_{l}")
        else:
            dyo = _mm_nt(dmo, w_c_out[i], name=f"c_out_dx_{l}")
            G["c_w_out"][i] = _mm_tn(sv["yo"], dmo, name=f"c_out_dw_{l}")
            (dy, dxs1, dz), (ddsk, dng) = _rowwise_bwd(
                f"ssd_post_bwd_{l}", _f_spost, [(sv["y_t"], 512, 0), (sv["xa"], 512, 0), (sv["zx"], 512, 0)],
                [], [(sv["dsk"], 512, 0), (c_ng[i], 512, 0)], [], [(dyo, 512, 0)], [f32, f32, bf16], tm=ROW_TILE, J=SSD_G)
            G["c_d_skip"][i] = ddsk.reshape(SSD_H, SSD_HD).sum(axis=1)
            G["c_norm_g"][i] = dng[0]
            dqf, dkf, dvf, daf = _scan_bwd(sv["xa"], sv["xa"], sv["vf"], sv["a_f"], sv["hsf"], dy, rev=False,
                                           name=f"ssd_scan_f_bwd_{l}", **sv["sscan"])
            dq_t, dk_t, dvb, dab = _scan_bwd(sv["xa"], sv["xa"], sv["vb"], sv["a_b"], sv["hsb"], dy, rev=True,
                                             name=f"ssd_scan_b_bwd_{l}", add_to=(dqf, dkf, None), **sv["sscan"])
            dla = jnp.concatenate([daf.transpose(1, 0, 2).reshape(T, SSD_H), dab.transpose(1, 0, 2).reshape(T, SSD_H)], axis=1)
            (dxs, ddtr), (ddtb, dalog) = _rowwise_bwd(
                f"ssd_prep_bwd_{l}", _f_sprep_bwd, [(sv["xa"], SSD_INNER, 0), (sv["dtr"], 2 * SSD_H, 0)], [],
                [(sv["dtb"], 2 * SSD_H, 0), (sv["alog"], 2 * SSD_H, 0)], [ex0, ex1],
                [(dvf, SSD_INNER, 0), (dvb, SSD_INNER, 0), (dla, 2 * SSD_H, 0), (dxs1, SSD_INNER, 0)], [f32, bf16], tm=128)
            G["c_dt_bias"][i] = ddtb.reshape(2, SSD_H)
            G["c_a_log"][i] = dalog.reshape(2, SSD_H)
            dxa = jnp.concatenate([dxs, dk_t, dq_t], axis=1)
            dxbc, dcw, dcb = _conv_bwd(sv["zx"], c_cw8[i], c_cb[i], dxa, mode="silu", W=SSD_CONV, name=f"c_conv_bwd_{l}",
                                       C=SSD_XBC, xbase=SSD_INNER // 512)
            G["c_conv_w"][i] = dcw[:SSD_CONV]
            G["c_conv_b"][i] = dcb[0]
            dzx = jnp.concatenate([dz, dxbc], axis=1)
            t1 = _mm_nt(ddtr, w_dt[i], name=f"c_in_dt_dx_{l}")
            dhn = _mm_nt(dzx, w_zx[i], add=t1, name=f"c_in_dx_{l}")
            G["c_w_in"][i] = _cols_to_slots([_mm_tn(sv["hn"], dzx, name=f"c_in_dw_{l}"), _mm_tn(sv["hn"], ddtr, name=f"c_in_dt_dw_{l}")],
                                            "c_w_in", bf16, name=f"slots_c_in_{l}")
        dx = dxl
    (grad_x,), (dg1,) = _rowwise_bwd("norm_first_bwd", _f_first_bwd, [(x, D, 0)], [], [(g1[0], D, 0)], [], [(dx, D, 0), (dhn, D, 0)],
                                     [f32], tm=ROW_TILE)
    G["norm_mix_pre"][0] = dg1[0]

    small_names = [n for n, _ in SHARDED[N_BIG:]]
    col_slots = [jnp.concatenate(G[n], axis=1) for n in COL_SHARDED]
    row_slots = jnp.concatenate([g.reshape(NDEV, -1, D).astype(bf16) for n in ROW_SHARDED for g in G[n]], axis=1)
    small_slots = _pack_slots([_to_slots(jnp.stack(G[n]), ax) for n, ax in SHARDED[N_BIG:]], 8)
    ar = _pack([jnp.stack(G[n]) for n in REPLICATED], f32, 8)
    parts = [(a, True) for a in col_slots + [row_slots, small_slots]] + [(ar, False)]
    from_sib = _to_sibling(parts, name="grads_to_sibling")
    tiles = [256, 256, 256, 64, small_slots.shape[1], ar.shape[0]]
    chip = [_add_partials(a, b, per_slot=ps, tr=t, name=f"grads_add_{j}")
            for j, ((a, ps), b, t) in enumerate(zip(parts, from_sib, tiles))]
    exch = _to_chips([(a, ps) for a, (_, ps) in zip(chip, parts)], name="grads_to_chips")
    pk = lambda d, names: _pack([d[n] for n in names], f32, 8)
    upd = [_adamw(exch[j], col(W, n, f32), col(Mo, n, f32), col(Vo, n, f32), name=f"adamw_{n}", tr=256)
           for j, n in enumerate(COL_SHARDED)]
    upd_rows = _adamw(exch[3], rows_of(W, f32), rows_of(Mo, f32), rows_of(Vo, f32), name="adamw_rows", tr=64)
    upd_small = _adamw(exch[4], pk(W, small_names), pk(Mo, small_names), pk(Vo, small_names), name="adamw_small",
                       tr=small_slots.shape[1])
    upd_rep = _adamw(exch[5], pk(W, REPLICATED), pk(Mo, REPLICATED), pk(Vo, REPLICATED), name="adamw_replicated", tr=ar.shape[0])
    res = []
    for k in range(4):
        d = {n: upd[j][k].reshape(W[n].shape) for j, n in enumerate(COL_SHARDED)}
        off = 0
        for n in ROW_SHARDED:
            cnt = W[n].shape[0] * W[n].shape[1]
            d[n] = upd_rows[k][off:off + cnt].reshape(W[n].shape)
            off += cnt
        d.update(zip(small_names, _unpack(upd_small[k], [W[n].shape for n in small_names])))
        d.update(zip(REPLICATED, _unpack(upd_rep[k], [W[n].shape for n in REPLICATED])))
        res.append(d)
    outs = [loss, grad_x[None]]
    for k in range(4):
        outs += [res[k][n] for n in WEIGHTS]
    return tuple(outs)


def _pack_slots(slot_arrays, row_mult):
    flat = jnp.concatenate([a.reshape(NDEV, -1) for a in slot_arrays], axis=1)
    rows = -(-flat.shape[1] // LANES)
    rows = -(-rows // row_mult) * row_mult
    return jnp.pad(flat, ((0, 0), (0, rows * LANES - flat.shape[1]))).reshape(NDEV, rows, LANES)
```

```python
import functools
import numpy as np
import jax
import jax.numpy as jnp
from jax import lax
from jax.experimental import pallas as pl
from jax.experimental.pallas import tpu as pltpu

f32, bf16 = jnp.float32, jnp.bfloat16
S = jax.ShapeDtypeStruct
HI = lax.Precision.HIGHEST

D = 1024
DEPTH = 4
GRID_W = 64
CHUNK = 128
EPS = 1e-6
RH, RDH, RW = 8, 64, 512
NAH, NADH, NAW = 8, 64, 512
NA_WR, NA_WC = 8, 16
NA_QROWS = 8
NA_KROWS = 16
NA_PAIR = 2
SSD_INNER, SSD_HD, SSD_H, SSD_G, SSD_HPG, SSD_N, SSD_CONV = 2048, 64, 32, 4, 8, 128, 5
SSD_XBC = SSD_INNER + 2 * SSD_G * SSD_N
FFN, FFN_CONV = 2816, 3
FFN_TC = 512
SCAN_HEADS_PER_STEP = 8
ROPE_BASE = 10000.0
LR, B1, B2, AEPS, WD, STEP = 0.001, 0.9, 0.999, 1e-08, 0.01, 10
NDEV = 8
LANES = 128
VMEM_LIMIT = 56 * 1024 * 1024
MM_BLOCK_BYTES = 6 * 1024 * 1024
ROW_TILE = 512
WIDE_ROW_TILE = 256
CONV_TILE = 1024

NT = (((1,), (1,)), ((), ()))
TN = (((0,), (0,)), ((), ()))

SHARDED = [("ab_w_in", 2), ("ab_w_out", 1), ("c_w_in", 2), ("c_w_out", 1), ("ffn_w_up", 2), ("ffn_w_down", 1),
           ("c_conv_w", 2), ("c_conv_b", 1), ("c_norm_g", 1), ("ffn_conv_w", 2)]
N_BIG = 6
COL_SHARDED = ["ab_w_in", "c_w_in", "ffn_w_up"]
ROW_SHARDED = ["ab_w_out", "c_w_out", "ffn_w_down"]
REPLICATED = ["norm_mix_pre", "norm_mix_post", "norm_ffn_pre", "norm_ffn_post", "ab_ret_decay_logit", "ab_ret_gn_g",
              "ab_na_rpb", "c_dt_bias", "c_a_log", "c_d_skip", "ffn_conv_b"]
WEIGHTS = ["norm_mix_pre", "norm_mix_post", "norm_ffn_pre", "norm_ffn_post", "ab_w_in", "ab_ret_decay_logit",
           "ab_ret_gn_g", "ab_na_rpb", "ab_w_out", "c_w_in", "c_conv_w", "c_conv_b", "c_dt_bias", "c_a_log", "c_d_skip",
           "c_norm_g", "c_w_out", "ffn_w_up", "ffn_conv_w", "ffn_conv_b", "ffn_w_down"]


def _params(sem=None):
    return pltpu.CompilerParams(dimension_semantics=sem, vmem_limit_bytes=VMEM_LIMIT)


def _mm_nn(a, w, *, name, tm=1024, tn=512, out_dtype=f32):
    M, K = a.shape
    N = w.shape[1]
    tn = min(tn, N)

    def body(a_ref, w_ref, o_ref):
        o_ref[...] = jnp.dot(a_ref[...], w_ref[...], preferred_element_type=f32).astype(o_ref.dtype)

    return pl.pallas_call(
        body, name=name, grid=(M // tm, N // tn),
        in_specs=[pl.BlockSpec((tm, K), lambda i, j: (i, 0)), pl.BlockSpec((K, tn), lambda i, j: (0, j))],
        out_specs=pl.BlockSpec((tm, tn), lambda i, j: (i, j)),
        out_shape=S((M, N), out_dtype), compiler_params=_params(("parallel", "parallel")))(a, w)


def _mm_nt(dy, w, *, name, add=None, tm=512):
    M, N = dy.shape
    K = w.shape[0]
    tk = next((t for t in (1024, 1408, 512, 256, 128) if K % t == 0 and (t <= 512 or t * N * 2 <= MM_BLOCK_BYTES)), K)

    def body(*refs):
        if add is None:
            d_ref, w_ref, o_ref = refs
            o_ref[...] = lax.dot_general(d_ref[...], w_ref[...], NT, preferred_element_type=f32)
        else:
            d_ref, w_ref, a_ref, o_ref = refs
            o_ref[...] = lax.dot_general(d_ref[...], w_ref[...], NT, preferred_element_type=f32) + a_ref[...]

    in_specs = [pl.BlockSpec((tm, N), lambda i, j: (i, 0)), pl.BlockSpec((tk, N), lambda i, j: (j, 0))]
    args = [dy, w]
    if add is not None:
        in_specs.append(pl.BlockSpec((tm, tk), lambda i, j: (i, j)))
        args.append(add)
    return pl.pallas_call(
        body, name=name, grid=(M // tm, K // tk), in_specs=in_specs,
        out_specs=pl.BlockSpec((tm, tk), lambda i, j: (i, j)),
        out_shape=S((M, K), f32), compiler_params=_params(("parallel", "parallel")))(*args)


def _mm_tn(a, dy, *, name, tt=1024):
    M, K = a.shape
    N = dy.shape[1]
    tk = K if K <= 1024 else (1024 if K % 1024 == 0 else K // 2)
    tn = min(512, N)
    tt = min(tt, M)

    def body(a_ref, d_ref, o_ref):
        t = pl.program_id(2)
        part = lax.dot_general(a_ref[...], d_ref[...], TN, preferred_element_type=f32)

        @pl.when(t == 0)
        def _():
            o_ref[...] = part

        @pl.when(t > 0)
        def _():
            o_ref[...] += part

    return pl.pallas_call(
        body, name=name, grid=(K // tk, N // tn, M // tt),
        in_specs=[pl.BlockSpec((tt, tk), lambda k, n, t: (t, k)), pl.BlockSpec((tt, tn), lambda k, n, t: (t, n))],
        out_specs=pl.BlockSpec((tk, tn), lambda k, n, t: (k, n)),
        out_shape=S((K, N), f32), compiler_params=_params(("parallel", "parallel", "arbitrary")))(a, dy)


def _tile_spec(tm, width, base):
    return pl.BlockSpec((tm, width), lambda j, i: (i, base + j))


def _par_spec(width, base):
    return pl.BlockSpec((1, width), lambda j, i: (0, base + j))


def _full_spec(a):
    nd = a.ndim
    return pl.BlockSpec(a.shape, lambda j, i: (0,) * nd)


def _rowwise(name, f, tiles, ctiles, params, consts, outs, *, tm, J=1):
    T = tiles[0][0].shape[0]
    nt, nct, npar, nc = len(tiles), len(ctiles), len(params), len(consts)

    def body(*refs):
        tv = [r[...].astype(f32) for r in refs[:nt + nct]]
        pv = [r[...] for r in refs[nt + nct:nt + nct + npar + nc]]
        res = f(*tv, *pv)
        for o, v in zip(refs[nt + nct + npar + nc:], res):
            o[...] = v.astype(o.dtype)

    in_specs = ([_tile_spec(tm, w, b) for _, w, b in tiles + ctiles] + [_par_spec(w, b) for _, w, b in params]
                + [_full_spec(c) for c in consts])
    return pl.pallas_call(
        body, name=name, grid=(J, T // tm), in_specs=in_specs,
        out_specs=[_tile_spec(tm, w, 0) for w, _ in outs],
        out_shape=[S((T, J * w), dt) for w, dt in outs],
        compiler_params=_params(("parallel", "parallel")))(
            *[a for a, _, _ in tiles + ctiles], *[a for a, _, _ in params], *consts)


def _rowwise_bwd(name, f, tiles, ctiles, params, consts, douts, dtile_dtypes, *, tm, J=1):
    T = tiles[0][0].shape[0]
    nt, nct, npar, nc, nd = len(tiles), len(ctiles), len(params), len(consts), len(douts)

    def body(*refs):
        i = pl.program_id(1)
        k = 0
        tv = [r[...].astype(f32) for r in refs[k:k + nt]]; k += nt
        cv = [r[...].astype(f32) for r in refs[k:k + nct]]; k += nct
        pv = [r[...] for r in refs[k:k + npar]]; k += npar
        kv = [r[...] for r in refs[k:k + nc]]; k += nc
        dv = [r[...].astype(f32) for r in refs[k:k + nd]]; k += nd
        dt_refs = refs[k:k + nt]; k += nt
        dp_refs = refs[k:k + npar]
        _, vjp = jax.vjp(lambda tv_, pv_: tuple(f(*tv_, *cv, *pv_, *kv)), tv, pv)
        dts, dps = vjp(tuple(dv))
        for r, g in zip(dt_refs, dts):
            r[...] = g.astype(r.dtype)
        for r, g in zip(dp_refs, dps):
            @pl.when(i == 0)
            def _(r=r, g=g):
                r[...] = g

            @pl.when(i > 0)
            def _(r=r, g=g):
                r[...] += g

    in_specs = ([_tile_spec(tm, w, b) for _, w, b in tiles + ctiles] + [_par_spec(w, b) for _, w, b in params]
                + [_full_spec(c) for c in consts] + [_tile_spec(tm, w, b) for _, w, b in douts])
    res = pl.pallas_call(
        body, name=name, grid=(J, T // tm), in_specs=in_specs,
        out_specs=[_tile_spec(tm, w, 0) for _, w, _ in tiles] + [_par_spec(w, b) for _, w, b in params],
        out_shape=[S((T, J * w), dt) for (_, w, _), dt in zip(tiles, dtile_dtypes)] + [S(a.shape, f32) for a, _, _ in params],
        compiler_params=_params(("parallel", "arbitrary")))(
            *[a for a, _, _ in tiles + ctiles], *[a for a, _, _ in params], *consts, *[a for a, _, _ in douts])
    return res[:nt], res[nt:]


def _rms(x, g):
    return x * lax.rsqrt(jnp.mean(x * x, axis=-1, keepdims=True) + EPS) * g


def _f_first(x, g1):
    return (_rms(x, g1),)


def _f_first_bwd(x, g1):
    return (x, _rms(x, g1))


def _f_mid(x, m, g2, g3):
    x1 = x + _rms(m, g2)
    return (x1, _rms(x1, g3))


def _f_end(x1, fo, g4, g1n):
    x2 = x1 + _rms(fo, g4)
    return (x2, _rms(x2, g1n))


def _f_last(x1, fo, g4):
    return (x1 + _rms(fo, g4),)


@jax.custom_vjp
def _swap_halves(x):
    c = x.shape[1]
    lane = lax.broadcasted_iota(jnp.int32, x.shape, 1) % RDH
    return jnp.where(lane < RDH // 2, pltpu.roll(x, c - RDH // 2, axis=1), pltpu.roll(x, RDH // 2, axis=1))


_swap_halves.defvjp(lambda x: (_swap_halves(x), None), lambda _, g: (_swap_halves(g),))


def _f_rprep(rq, rk, cos, sin):
    rot = lambda t: t * cos + _swap_halves(t) * sin
    return (rot(rq), rot(rk) * (RDH ** -0.5))


def _split3(x):
    h1 = x.astype(bf16)
    r1 = x - h1.astype(f32)
    h2 = r1.astype(bf16)
    return h1, h2, (r1 - h2.astype(f32)).astype(bf16)


@jax.custom_vjp
def _dot_sel(x, m):
    mb = m.astype(bf16)
    h1, h2, h3 = _split3(x)
    return jnp.dot(h1, mb, preferred_element_type=f32) + jnp.dot(h2, mb, preferred_element_type=f32) + jnp.dot(h3, mb, preferred_element_type=f32)


def _dot_sel_bwd(m, g):
    mb = m.astype(bf16)
    g1, g2, g3 = _split3(g)
    nt = lambda a: lax.dot_general(a, mb, NT, preferred_element_type=f32)
    return nt(g1) + nt(g2) + nt(g3), jnp.zeros_like(m)


_dot_sel.defvjp(lambda x, m: (_dot_sel(x, m), m), _dot_sel_bwd)


def _f_rpost(y, rg, gn, gavg):
    mu = _dot_sel(y, gavg)
    yc = y - mu
    var = _dot_sel(yc * yc, gavg)
    return (jax.nn.silu(rg) * (yc * lax.rsqrt(var + EPS) * gn),)


def _f_sprep(xs, dtr, dtb, alog, ex0, ex1):
    dt = jax.nn.softplus(dtr + dtb)
    la = dt * (-jnp.exp(alog))
    return (xs * _dot_sel(dt, ex0), xs * _dot_sel(dt, ex1), la)


def _f_sprep_bwd(xs, dtr, dtb, alog, ex0, ex1):
    return _f_sprep(xs, dtr, dtb, alog, ex0, ex1) + (xs,)


def _f_spost(y, xs, z, dsk, ng):
    y = (y + xs * dsk) * jax.nn.silu(z)
    y = y * lax.rsqrt(jnp.mean(y * y, axis=-1, keepdims=True) + EPS)
    return (y * ng,)


def _loss_call(y, tgt, *, tm=ROW_TILE):
    T = y.shape[0]

    def body(y_ref, t_ref, dy_ref, l_ref):
        i = pl.program_id(0)
        e = y_ref[...] - t_ref[...]
        dy_ref[...] = e * (1.0 / D)
        part = jnp.zeros((8, LANES), f32) + 0.5 * jnp.sum(jnp.mean(e * e, axis=-1, keepdims=True))

        @pl.when(i == 0)
        def _():
            l_ref[...] = part

        @pl.when(i > 0)
        def _():
            l_ref[...] += part

    return pl.pallas_call(
        body, name="loss_head", grid=(T // tm,),
        in_specs=[pl.BlockSpec((tm, D), lambda i: (i, 0))] * 2,
        out_specs=[pl.BlockSpec((tm, D), lambda i: (i, 0)), pl.BlockSpec((8, LANES), lambda i: (0, 0))],
        out_shape=[S((T, D), f32), S((8, LANES), f32)], compiler_params=_params(("arbitrary",)))(y, tgt)


def _colsum(x, *, name, tm=512):
    T, C = x.shape

    def body(x_ref, o_ref):
        i = pl.program_id(0)
        part = jnp.sum(x_ref[...], axis=0, keepdims=True)

        @pl.when(i == 0)
        def _():
            o_ref[...] = part

        @pl.when(i > 0)
        def _():
            o_ref[...] += part

    return pl.pallas_call(
        body, name=name, grid=(T // tm,), in_specs=[pl.BlockSpec((tm, C), lambda i: (i, 0))],
        out_specs=pl.BlockSpec((1, C), lambda i: (0, 0)), out_shape=S((1, C), f32),
        compiler_params=_params(("arbitrary",)))(x)


def _nn(a, b):
    if a.ndim == 3:
        return lax.dot_general(a, b, (((2,), (1,)), ((0,), (0,))), preferred_element_type=f32)
    return jnp.dot(a, b, preferred_element_type=f32)


def _nt(a, b):
    if a.ndim == 3:
        return lax.dot_general(a, b, (((2,), (2,)), ((0,), (0,))), preferred_element_type=f32)
    return lax.dot_general(a, b, NT, preferred_element_type=f32)


def _lift(x, like):
    return jnp.broadcast_to(x[None], like.shape[:1] + x.shape) if x.ndim < like.ndim else x


def _drop(g, like):
    return jnp.sum(g, axis=0) if like.ndim < g.ndim else g


@jax.custom_vjp
def _mm_lt(a, a_t, b):
    return _nn(_lift(a_t, b), b)


_mm_lt.defvjp(lambda a, a_t, b: (_nn(_lift(a_t, b), b), (a, b)),
              lambda res, g: (jnp.zeros_like(res[0]), _drop(_nt(g, res[1]), res[0]), _nn(_lift(res[0], g), g)))


@jax.custom_vjp
def _mm_rt(a, a_t, b):
    return _nn(_lift(a, b), b)


_mm_rt.defvjp(lambda a, a_t, b: (_nn(_lift(a, b), b), (a_t, b)),
              lambda res, g: (_drop(_nt(g, res[1]), res[0]), jnp.zeros_like(res[0]), _nn(_lift(res[0], g), g)))


@jax.custom_vjp
def _masked_mm(s, s_t, d, d_t, v):
    return _nn(s * d, v)


def _masked_mm_bwd(res, g):
    s, s_t, d, d_t, v = res
    da = _nt(g, v)
    return (_drop(da * d, s), jnp.zeros_like(s_t), da * s, jnp.zeros_like(d_t), _nn(s_t * d_t, g))


_masked_mm.defvjp(lambda s, s_t, d, d_t, v: (_nn(s * d, v), (s, s_t, d, d_t, v)), _masked_mm_bwd)


def _t(x):
    return jnp.swapaxes(x, -1, -2)


@jax.custom_vjp
def _cumsums(a, tri, tri_t):
    pieces = _split3(a)
    cs = sum(jnp.dot(tri, p, preferred_element_type=f32) for p in pieces)
    cs_t = sum(lax.dot_general(p, tri_t, TN, preferred_element_type=f32) for p in pieces)
    return cs, cs_t


def _cumsums_bwd(res, g):
    tri, tri_t = res
    g_cs, g_cs_t = g
    da = sum(jnp.dot(tri_t, p, preferred_element_type=f32) for p in _split3(g_cs))
    da = da + sum(lax.dot_general(tri_t, p, NT, preferred_element_type=f32) for p in _split3(g_cs_t))
    return da, jnp.zeros_like(tri), jnp.zeros_like(tri_t)


_cumsums.defvjp(lambda a, tri, tri_t: (_cumsums(a, tri, tri_t), (tri, tri_t)), _cumsums_bwd)


def _scan_step_heads(h, q, k, v, a, rev, for_vjp=False):
    B, L, P = v.shape
    ii = lax.broadcasted_iota(jnp.int32, (L, L), 0)
    jj = lax.broadcasted_iota(jnp.int32, (L, L), 1)
    if rev:
        tri, tri_t, dmask, dmask_t = (jj >= ii), (ii >= jj), (jj > ii), (ii > jj)
    else:
        tri, tri_t, dmask, dmask_t = (jj <= ii), (ii <= jj), (jj <= ii), (ii <= jj)
    cs, cs_t = _cumsums(a, tri.astype(bf16), tri_t.astype(bf16))
    tot = jnp.sum(a, axis=0, keepdims=True)
    c_col = jnp.stack([jnp.broadcast_to(cs[:, b:b + 1], (L, L)) for b in range(B)])
    c_row = jnp.stack([cs_t[b:b + 1, :] for b in range(B)])
    t_all = jnp.stack([tot[:, b:b + 1] for b in range(B)])
    dec = jnp.exp(jnp.where(dmask[None], c_col - c_row, -1e30))
    e_in, e_out = jnp.exp(c_col)[:, :, :P], jnp.exp(t_all - c_col)[:, :, :P]
    qk = _nt(q, k)
    k_t = _t(k)
    w = v * e_out
    if for_vjp:
        q_t = lax.stop_gradient(_t(q))
        qk_t = lax.stop_gradient(_nt(k, q))
        dec_t = lax.stop_gradient(jnp.exp(jnp.where(dmask_t[None], c_row - c_col, -1e30)))
        y = _masked_mm(qk, qk_t, dec, dec_t, v) + _mm_rt(q, q_t, h) * e_in
        hn = h * jnp.exp(t_all) + _mm_lt(lax.stop_gradient(k), k_t, w)
    else:
        y = _nn(qk * dec, v) + _nn(_lift(q, h), h) * e_in
        hn = h * jnp.exp(t_all) + _nn(_lift(k_t, w), w)
    return hn, y


def _scan_specs(gb, N, Hg, P, Ha, cm, qcol, kcol, vcol):
    qs = lambda col: pl.BlockSpec((CHUNK, gb * N), lambda g, c: (cm(c), col + g))
    vs = lambda col: pl.BlockSpec((CHUNK, gb * Hg * P), lambda g, c: (cm(c), col + g))
    as_ = pl.BlockSpec((1, CHUNK, Ha), lambda g, c: (g, cm(c), 0))
    hs = pl.BlockSpec((gb, 1, Hg, N, P), lambda g, c: (g, cm(c), 0, 0, 0))
    return qs(qcol), qs(kcol), vs(vcol), qs(0), vs(0), as_, hs


def _lanes(ref, n, width):
    return jnp.stack([ref[:, j * width:(j + 1) * width] for j in range(n)])


def _scan_fwd(q, k, v, a, *, G, N, Hg, P, qcol=0, kcol=0, vcol=0, rev, name, add_y=None):
    T, Ha, NC = q.shape[0], a.shape[2], q.shape[0] // CHUNK
    gb = SCAN_HEADS_PER_STEP // Hg
    cm = (lambda c: NC - 1 - c) if rev else (lambda c: c)
    qs, ks, vs, _, ys, as_, hs = _scan_specs(gb, N, Hg, P, Ha, cm, qcol, kcol, vcol)
    extra = [] if add_y is None else [add_y]

    def body(q_ref, k_ref, v_ref, a_ref, *rest):
        y_ref, hs_ref, h_scr = rest[len(extra):]

        @pl.when(pl.program_id(1) == 0)
        def _():
            h_scr[...] = jnp.zeros_like(h_scr)

        if Hg == 1:
            h = h_scr[:, 0]
            hs_ref[:, 0, 0] = h
            hn, y = _scan_step_heads(h, _lanes(q_ref, gb, N), _lanes(k_ref, gb, N), _lanes(v_ref, gb, P), a_ref[0], rev)
            h_scr[:, 0] = hn
        else:
            h = h_scr[0]
            hs_ref[0, 0] = h
            hn, y = _scan_step_heads(h, q_ref[...], k_ref[...], _lanes(v_ref, Hg, P), a_ref[0], rev)
            h_scr[0] = hn
        for j in range(gb * Hg):
            cols = slice(j * P, (j + 1) * P)
            y_ref[:, cols] = y[j] if add_y is None else y[j] + rest[0][:, cols]

    return pl.pallas_call(
        body, name=name, grid=(G // gb, NC), in_specs=[qs, ks, vs, as_] + [ys] * len(extra), out_specs=[ys, hs],
        out_shape=[S((T, G * Hg * P), f32), S((G, NC, Hg, N, P), f32)],
        scratch_shapes=[pltpu.VMEM((gb, Hg, N, P), f32)],
        compiler_params=_params(("parallel", "arbitrary")))(q, k, v, a, *extra)


def _scan_bwd(q, k, v, a, hsave, dy, *, G, N, Hg, P, qcol=0, kcol=0, vcol=0, rev, name, add_to=(None, None, None)):
    T, Ha, NC = q.shape[0], a.shape[2], q.shape[0] // CHUNK
    gb = SCAN_HEADS_PER_STEP // Hg
    cm = (lambda c: c) if rev else (lambda c: NC - 1 - c)
    qs, ks, vs, dqs, dvs, as_, hs = _scan_specs(gb, N, Hg, P, Ha, cm, qcol, kcol, vcol)
    extra = [(x, s) for x, s in zip(add_to, (dqs, dqs, dvs)) if x is not None]

    def body(q_ref, k_ref, v_ref, a_ref, hs_ref, dy_ref, *rest):
        dq_ref, dk_ref, dv_ref, da_ref, dh_scr = rest[len(extra):]
        prev = iter(rest[:len(extra)])
        pq, pk, pv = [next(prev) if x is not None else None for x in add_to]

        @pl.when(pl.program_id(1) == 0)
        def _():
            dh_scr[...] = jnp.zeros_like(dh_scr)

        if Hg == 1:
            _, vjp = jax.vjp(functools.partial(_scan_step_heads, rev=rev, for_vjp=True), hs_ref[:, 0, 0], _lanes(q_ref, gb, N),
                             _lanes(k_ref, gb, N), _lanes(v_ref, gb, P), a_ref[0])
            dh, dq, dk, dv, da = vjp((dh_scr[:, 0], _lanes(dy_ref, gb, P)))
            dh_scr[:, 0] = dh
            for j in range(gb):
                cols = slice(j * N, (j + 1) * N)
                dq_ref[:, cols] = dq[j] if pq is None else dq[j] + pq[:, cols]
                dk_ref[:, cols] = dk[j] if pk is None else dk[j] + pk[:, cols]
        else:
            _, vjp = jax.vjp(functools.partial(_scan_step_heads, rev=rev, for_vjp=True), hs_ref[0, 0], q_ref[...], k_ref[...],
                             _lanes(v_ref, Hg, P), a_ref[0])
            dh, dq, dk, dv, da = vjp((dh_scr[0], _lanes(dy_ref, Hg, P)))
            dh_scr[0] = dh
            dq_ref[...] = dq if pq is None else dq + pq[...]
            dk_ref[...] = dk if pk is None else dk + pk[...]
        for j in range(gb * Hg):
            cols = slice(j * P, (j + 1) * P)
            dv_ref[:, cols] = dv[j] if pv is None else dv[j] + pv[:, cols]
        da_ref[0] = da

    return pl.pallas_call(
        body, name=name, grid=(G // gb, NC), in_specs=[qs, ks, vs, as_, hs, dvs] + [s for _, s in extra],
        out_specs=[dqs, dqs, dvs, as_],
        out_shape=[S((T, G * N), f32), S((T, G * N), f32), S((T, G * Hg * P), f32), S(a.shape, f32)],
        scratch_shapes=[pltpu.VMEM((gb, Hg, N, P), f32)],
        compiler_params=_params(("parallel", "arbitrary")))(q, k, v, a, hsave, dy, *[x for x, _ in extra])


def _na_block_case(rb, nrb):
    return jnp.where(rb == 0, 0, jnp.where(rb == nrb - 1, 2, 1))


def _na_key_start(rb, rows):
    return pl.multiple_of(jnp.clip(rb * NA_QROWS - NA_WR // 2, 0, rows - NA_KROWS) * GRID_W, 256)


def _na_specs(T, nrb):
    nq, nk, wb = NA_QROWS * GRID_W, NA_KROWS * GRID_W, NA_PAIR * NADH
    qs = lambda col: pl.BlockSpec((nq, wb), lambda p, r: (r, col + p))
    fs = lambda col: pl.BlockSpec((T, wb), lambda p, r: (0, col + p))
    bs = pl.BlockSpec((NA_PAIR, 1, nq, nk), lambda p, r: (p, _na_block_case(r, nrb), 0, 0))
    ls = pl.BlockSpec((1, nq, NA_PAIR), lambda p, r: (p, r, 0))
    return qs, fs, bs, ls


def _na_fwd(q, k, v, bias, *, qcol, kcol, vcol, name):
    T = q.shape[0]
    rows = T // GRID_W
    nq, nk = NA_QROWS * GRID_W, NA_KROWS * GRID_W
    nrb = T // nq
    scale = NADH ** -0.5
    qs, fs, bs, ls = _na_specs(T, nrb)

    def body(q_ref, k_ref, v_ref, b_ref, o_ref, l_ref):
        ks = _na_key_start(pl.program_id(1), rows)
        for hh in range(NA_PAIR):
            sl = slice(hh * NADH, (hh + 1) * NADH)
            kw = k_ref[pl.ds(ks, nk), sl]
            vw = v_ref[pl.ds(ks, nk), sl]
            s = lax.dot_general(q_ref[:, sl], kw, NT, preferred_element_type=f32) * scale + b_ref[hh, 0]
            m = jnp.max(s, axis=1, keepdims=True)
            p = jnp.exp(s - m)
            l = jnp.sum(p, axis=1, keepdims=True)
            o_ref[:, sl] = jnp.dot(p.astype(bf16), vw, preferred_element_type=f32) / l
            l_ref[0, :, hh:hh + 1] = m + jnp.log(l)

    return pl.pallas_call(
        body, name=name, grid=(NAH // NA_PAIR, nrb), in_specs=[qs(qcol), fs(kcol), fs(vcol), bs],
        out_specs=[qs(0), ls], out_shape=[S((T, NAW), f32), S((NAH // NA_PAIR, T, NA_PAIR), f32)],
        compiler_params=_params(("parallel", "arbitrary")))(q, k, v, bias)


def _na_bwd(q, k, v, bias, o, lse, do, *, qcol, kcol, vcol, docol, name):
    T = q.shape[0]
    rows = T // GRID_W
    nq, nk = NA_QROWS * GRID_W, NA_KROWS * GRID_W
    nrb = T // nq
    scale = NADH ** -0.5
    qs, fs, bs, ls = _na_specs(T, nrb)

    def body(q_ref, k_ref, v_ref, b_ref, o_ref, l_ref, do_ref, dq_ref, dk_ref, dv_ref, db_ref):
        rb = pl.program_id(1)

        @pl.when(rb == 0)
        def _():
            dk_ref[...] = jnp.zeros_like(dk_ref)
            dv_ref[...] = jnp.zeros_like(dv_ref)

        ks = _na_key_start(rb, rows)
        first = (rb == 0) | (rb == 1) | (rb == nrb - 1)
        for hh in range(NA_PAIR):
            sl = slice(hh * NADH, (hh + 1) * NADH)
            qv = q_ref[:, sl]
            kw = k_ref[pl.ds(ks, nk), sl]
            vw = v_ref[pl.ds(ks, nk), sl]
            s = lax.dot_general(qv, kw, NT, preferred_element_type=f32) * scale + b_ref[hh, 0]
            p = jnp.exp(s - l_ref[0, :, hh:hh + 1])
            do_ = do_ref[:, sl]
            dob = do_.astype(bf16)
            dp = lax.dot_general(dob, vw, NT, preferred_element_type=f32)
            ds = p * (dp - jnp.sum(do_ * o_ref[:, sl], axis=1, keepdims=True))
            dsb = ds.astype(bf16)
            dq_ref[:, sl] = jnp.dot(dsb, kw, preferred_element_type=f32) * scale
            dk_ref[pl.ds(ks, nk), sl] += lax.dot_general(dsb, qv, TN, preferred_element_type=f32) * scale
            dv_ref[pl.ds(ks, nk), sl] += lax.dot_general(p.astype(bf16), dob, TN, preferred_element_type=f32)

            @pl.when(first)
            def _(hh=hh, ds=ds):
                db_ref[hh, 0] = ds

            @pl.when(jnp.logical_not(first))
            def _(hh=hh, ds=ds):
                db_ref[hh, 0] += ds

    return pl.pallas_call(
        body, name=name, grid=(NAH // NA_PAIR, nrb),
        in_specs=[qs(qcol), fs(kcol), fs(vcol), bs, qs(0), ls, qs(docol)],
        out_specs=[qs(0), fs(0), fs(0), bs],
        out_shape=[S((T, NAW), f32), S((T, NAW), f32), S((T, NAW), f32), S(bias.shape, f32)],
        compiler_params=_params(("parallel", "arbitrary")))(q, k, v, bias, o, lse, do)


def _na_col_tables():
    c = np.arange(GRID_W)[:, None]
    kc = np.arange(GRID_W)[None, :]
    cstart = np.clip(c - NA_WC // 2, 0, GRID_W - NA_WC)
    valid_c = (kc >= cstart) & (kc < cstart + NA_WC)
    dc = kc - c + NA_WC - 1
    E = (valid_c[:, :, None] & (dc[:, :, None] == np.arange(2 * NA_WC - 1)[None, None, :])).astype(np.float32)
    return E, np.where(valid_c, 0.0, -1e30).astype(np.float32)


def _na_row_offsets(rows):
    table = []
    for r0 in (0, NA_QROWS, rows - NA_QROWS):
        ks = int(np.clip(r0 - NA_WR // 2, 0, rows - NA_KROWS))
        case = []
        for ri in range(NA_QROWS):
            r = r0 + ri
            rs = int(np.clip(r - NA_WR // 2, 0, rows - NA_WR))
            case.append([ks + kri - r + NA_WR - 1 if rs <= ks + kri < rs + NA_WR else None for kri in range(NA_KROWS)])
        table.append(case)
    return table


def _na_col_bias(rpb):
    E, cmask = _na_col_tables()
    return jnp.einsum("hde,cke->hdck", rpb, E, precision=HI) + cmask


def _na_bias_build(r1, rows, *, name):
    H = r1.shape[0]
    offs = _na_row_offsets(rows)

    def body(r_ref, o_ref):
        outside = jnp.full((GRID_W, GRID_W), -1e30, f32)
        for z in range(3):
            for a in range(NA_QROWS):
                for b in range(NA_KROWS):
                    d = offs[z][a][b]
                    o_ref[0, z, a * GRID_W:(a + 1) * GRID_W, b * GRID_W:(b + 1) * GRID_W] = outside if d is None else r_ref[0, d]

    return pl.pallas_call(
        body, name=name, grid=(H,), in_specs=[pl.BlockSpec((1,) + r1.shape[1:], lambda h: (h, 0, 0, 0))],
        out_specs=pl.BlockSpec((1, 3, NA_QROWS * GRID_W, NA_KROWS * GRID_W), lambda h: (h, 0, 0, 0)),
        out_shape=S((H, 3, NA_QROWS * GRID_W, NA_KROWS * GRID_W), f32), compiler_params=_params(("parallel",)))(r1)


def _na_bias_fold(dbias, rows, *, name):
    H = dbias.shape[0]
    offs = _na_row_offsets(rows)
    nd = 2 * NA_WR - 1

    def body(d_ref, o_ref):
        acc = [None] * nd
        for z in range(3):
            for a in range(NA_QROWS):
                for b in range(NA_KROWS):
                    d = offs[z][a][b]
                    if d is not None:
                        t = d_ref[0, z, a * GRID_W:(a + 1) * GRID_W, b * GRID_W:(b + 1) * GRID_W]
                        acc[d] = t if acc[d] is None else acc[d] + t
        for d in range(nd):
            o_ref[0, d] = acc[d]

    return pl.pallas_call(
        body, name=name, grid=(H,), in_specs=[pl.BlockSpec((1,) + dbias.shape[1:], lambda h: (h, 0, 0, 0))],
        out_specs=pl.BlockSpec((1, nd, GRID_W, GRID_W), lambda h: (h, 0, 0, 0)),
        out_shape=S((H, nd, GRID_W, GRID_W), f32), compiler_params=_params(("parallel",)))(dbias)


def _conv_shifts(prev, cur, nxt, i, n_i, W):
    tm = cur.shape[0]
    prev = jnp.where(i > 0, prev, 0.0)
    nxt = jnp.where(i < n_i - 1, nxt, 0.0)
    ext = jnp.concatenate([prev, cur, nxt], axis=0)
    out = []
    for w in range(W):
        s = (W // 2 - w) % (tm + 16)
        out.append((ext if s == 0 else pltpu.roll(ext, s, axis=0))[8:8 + tm])
    return out


def _conv_act(u, mode):
    if mode == "silu":
        return jax.nn.silu(u)
    assert mode == "geglu"
    half = u.shape[1] // 2
    return jax.nn.gelu(u[:, :half], approximate=True) * u[:, half:]


def _conv_specs(T, tm, tc, xbase):
    r8 = tm // 8
    last = T // 8 - 1
    cur = pl.BlockSpec((tm, tc), lambda j, i: (i, xbase + j))
    prev = pl.BlockSpec((8, tc), lambda j, i: (jnp.maximum(i * r8 - 1, 0), xbase + j))
    nxt = pl.BlockSpec((8, tc), lambda j, i: (jnp.minimum((i + 1) * r8, last), xbase + j))
    return cur, prev, nxt


def _conv(x, w8, b, *, mode, W, name, C, xbase=0, tm=CONV_TILE, tc=512, out_dtype=f32):
    T = x.shape[0]
    NI, J = T // tm, C // tc
    tco = tc // 2 if mode == "geglu" else tc
    cur, prev, nxt = _conv_specs(T, tm, tc, xbase)

    def body(xc, xp, xn, w_ref, b_ref, o_ref):
        sh = _conv_shifts(xp[...].astype(f32), xc[...].astype(f32), xn[...].astype(f32), pl.program_id(1), NI, W)
        wv = w_ref[...]
        u = sh[0] * wv[0:1, :]
        for w in range(1, W):
            u = u + sh[w] * wv[w:w + 1, :]
        o_ref[...] = _conv_act(u + b_ref[...], mode).astype(o_ref.dtype)

    return pl.pallas_call(
        body, name=name, grid=(J, NI),
        in_specs=[cur, prev, nxt, pl.BlockSpec((8, tc), lambda j, i: (0, j)), pl.BlockSpec((1, tc), lambda j, i: (0, j))],
        out_specs=pl.BlockSpec((tm, tco), lambda j, i: (i, j)), out_shape=S((T, J * tco), out_dtype),
        compiler_params=_params(("parallel", "parallel")))(x, x, x, w8, b)


def _conv_bwd(x, w8, b, dact, *, mode, W, name, C, xbase=0, tm=CONV_TILE, tc=512):
    T = x.shape[0]
    NI, J = T // tm, C // tc
    tco = tc // 2 if mode == "geglu" else tc
    rows = tm + 16
    pad = W // 2
    cur, prev, nxt = _conv_specs(T, tm, tc, xbase)
    dcur, dprev, dnxt = _conv_specs(T, tm, tco, 0)

    def body(xc, xp, xn, w_ref, b_ref, dc, dp, dn, dx_ref, dw_ref, db_ref):
        i = pl.program_id(1)
        ext = jnp.concatenate([jnp.where(i > 0, xp[...], 0.0), xc[...], jnp.where(i < NI - 1, xn[...], 0.0)], axis=0)
        dext = jnp.concatenate([jnp.where(i > 0, dp[...], 0.0), dc[...], jnp.where(i < NI - 1, dn[...], 0.0)], axis=0)
        wv = w_ref[...]
        shift = lambda t, w: t if w == pad else pltpu.roll(t, (pad - w) % rows, axis=0)
        xs = [shift(ext, w) for w in range(W)]
        u = b_ref[...] + xs[0] * wv[0:1, :]
        for w in range(1, W):
            u = u + xs[w] * wv[w:w + 1, :]
        _, vjp = jax.vjp(functools.partial(_conv_act, mode=mode), u)
        du = vjp(dext.astype(f32))[0]
        dx = shift(du, 0)[8:8 + tm] * wv[W - 1:W, :]
        for w in range(1, W):
            dx = dx + shift(du, w)[8:8 + tm] * wv[W - 1 - w:W - w, :]
        dx_ref[...] = dx.astype(dx_ref.dtype)

        @pl.when(i == 0)
        def _():
            dw_ref[...] = jnp.zeros_like(dw_ref)
            db_ref[...] = jnp.zeros_like(db_ref)

        dum = du[8:8 + tm]
        db_ref[...] += jnp.sum(dum, axis=0, keepdims=True)
        for w in range(W):
            dw_ref[w:w + 1, :] += jnp.sum(dum * xs[w][8:8 + tm], axis=0, keepdims=True)

    return pl.pallas_call(
        body, name=name, grid=(J, NI),
        in_specs=[cur, prev, nxt, pl.BlockSpec((8, tc), lambda j, i: (0, j)), pl.BlockSpec((1, tc), lambda j, i: (0, j)),
                  dcur, dprev, dnxt],
        out_specs=[pl.BlockSpec((tm, tc), lambda j, i: (i, j)), pl.BlockSpec((8, tc), lambda j, i: (0, j)),
                   pl.BlockSpec((1, tc), lambda j, i: (0, j))],
        out_shape=[S((T, C), bf16), S((8, C), f32), S((1, C), f32)],
        compiler_params=_params(("parallel", "arbitrary")))(x, x, x, w8, b, dact, dact, dact)


def _pad8(w):
    return jnp.concatenate([w, jnp.zeros((8 - w.shape[0], w.shape[1]), w.dtype)], axis=0)


def _all_gather(arrs, *, name):
    n = len(arrs)

    def body(*refs):
        ins, outs = refs[:n], refs[n:2 * n]
        send_sems, recv_sems, loc_sems = refs[2 * n:]
        x, y, c = lax.axis_index("x"), lax.axis_index("y"), lax.axis_index("c")
        ident = lambda px, py, pc: 4 * px + 2 * py + pc
        me, sibling = (x, y, c), (x, y, 1 - c)
        chips = [(1 - x, y), (x, 1 - y), (1 - x, 1 - y)]

        def copy(a, k, block, to, src=None):
            slot = outs[a].at[ident(*block)]
            return pltpu.make_async_remote_copy(
                src_ref=slot if src is None else src, dst_ref=slot, send_sem=send_sems.at[a * 7 + k], recv_sem=recv_sems.at[a * 7 + k],
                device_id=to, device_id_type=pl.DeviceIdType.MESH)

        local = [pltpu.make_async_copy(ins[a], outs[a].at[ident(*me)], loc_sems.at[a]) for a in range(n)]
        for cp in local:
            cp.start()
        first = []
        for a in range(n):
            first.append(copy(a, 0, me, sibling, src=ins[a]))
            first += [copy(a, 1 + j, me, (*chip, c), src=ins[a]) for j, chip in enumerate(chips)]
        for cp in first:
            cp.start()
        passed = []
        for j, chip in enumerate(chips):
            for a in range(n):
                copy(a, 1 + j, (*chip, c), me).wait_recv()
                fwd = copy(a, 4 + j, (*chip, c), sibling)
                fwd.start()
                passed.append(fwd)
        for a in range(n):
            copy(a, 0, sibling, me).wait_recv()
            for j, chip in enumerate(chips):
                copy(a, 4 + j, (*chip, 1 - c), me).wait_recv()
        for cp in first + passed:
            cp.wait_send()
        for cp in local:
            cp.wait()

    any_spec = pl.BlockSpec(memory_space=pl.ANY)
    return pl.pallas_call(
        body, name=name, in_specs=[any_spec] * n, out_specs=[any_spec] * n,
        out_shape=[S((NDEV,) + a.shape, a.dtype) for a in arrs],
        scratch_shapes=[pltpu.SemaphoreType.DMA((7 * n,)), pltpu.SemaphoreType.DMA((7 * n,)), pltpu.SemaphoreType.DMA((n,))],
        )(*arrs)


NCHIP = NDEV // 2


def _to_sibling(arrs, *, name):
    n = len(arrs)
    ncopy = sum(NCHIP if ps else 1 for _, ps in arrs)

    def body(*refs):
        ins, outs = refs[:n], refs[n:2 * n]
        send_sems, recv_sems = refs[2 * n:]
        x, y, c = lax.axis_index("x"), lax.axis_index("y"), lax.axis_index("c")
        copies, idx = [], 0
        for a, (_, per_slot) in enumerate(arrs):
            pairs = [(ins[a].at[2 * q + (1 - c)], outs[a].at[q]) for q in range(NCHIP)] if per_slot else [(ins[a], outs[a])]
            for src, dst in pairs:
                copies.append(pltpu.make_async_remote_copy(
                    src_ref=src, dst_ref=dst, send_sem=send_sems.at[idx], recv_sem=recv_sems.at[idx],
                    device_id=(x, y, 1 - c), device_id_type=pl.DeviceIdType.MESH))
                idx += 1
        for cp in copies:
            cp.start()
        for cp in copies:
            cp.wait_recv()
        for cp in copies:
            cp.wait_send()

    any_spec = pl.BlockSpec(memory_space=pl.ANY)
    return pl.pallas_call(
        body, name=name, in_specs=[any_spec] * n, out_specs=[any_spec] * n,
        out_shape=[S((NCHIP,) + a.shape[1:] if ps else a.shape, a.dtype) for a, ps in arrs],
        scratch_shapes=[pltpu.SemaphoreType.DMA((ncopy,)), pltpu.SemaphoreType.DMA((ncopy,))])(*[a for a, _ in arrs])


def _add_partials(mine, theirs, *, per_slot, tr, name):
    R, C = mine.shape[-2:]

    def body(a_ref, b_ref, o_ref):
        a = a_ref[lax.axis_index("c")] if per_slot else a_ref[...]
        b = b_ref[0] if per_slot else b_ref[...]
        s = a.astype(f32) + b.astype(f32)
        if per_slot:
            o_ref[0] = s.astype(o_ref.dtype)
        else:
            o_ref[...] = s.astype(o_ref.dtype)

    if per_slot:
        grid = (NCHIP, R // tr)
        in_specs = [pl.BlockSpec((2, tr, C), lambda q, i: (q, i, 0)), pl.BlockSpec((1, tr, C), lambda q, i: (q, i, 0))]
        out_spec, out_shape = pl.BlockSpec((1, tr, C), lambda q, i: (q, i, 0)), S((NCHIP, R, C), mine.dtype)
    else:
        grid = (1, R // tr)
        in_specs = [pl.BlockSpec((tr, C), lambda q, i: (i, 0))] * 2
        out_spec, out_shape = pl.BlockSpec((tr, C), lambda q, i: (i, 0)), S((R, C), mine.dtype)
    return pl.pallas_call(body, name=name, grid=grid, in_specs=in_specs, out_specs=out_spec, out_shape=out_shape,
                          compiler_params=_params(("parallel", "parallel")))(mine, theirs)


def _to_chips(arrs, *, name):
    n = len(arrs)

    def body(*refs):
        ins, outs = refs[:n], refs[n:2 * n]
        send_sems, recv_sems, loc_sems = refs[2 * n:]
        x, y, c = lax.axis_index("x"), lax.axis_index("y"), lax.axis_index("c")
        my_q = 2 * x + y
        src = lambda a, q: ins[a].at[q] if arrs[a][1] else ins[a]
        local = [pltpu.make_async_copy(src(a, my_q), outs[a].at[my_q], loc_sems.at[a]) for a in range(n)]
        for cp in local:
            cp.start()
        sent = []
        for j, (px, py) in enumerate([(1 - x, y), (x, 1 - y), (1 - x, 1 - y)]):
            q = 2 * px + py
            for a in range(n):
                mk = lambda slot, a=a, j=j, q=q, dev=(px, py, c): pltpu.make_async_remote_copy(
                    src_ref=src(a, q), dst_ref=outs[a].at[slot], send_sem=send_sems.at[3 * a + j], recv_sem=recv_sems.at[3 * a + j],
                    device_id=dev, device_id_type=pl.DeviceIdType.MESH)
                mk(my_q).start()
                sent.append((mk, q))
        for mk, q in sent:
            mk(q).wait_recv()
        for mk, q in sent:
            mk(q).wait_send()
        for cp in local:
            cp.wait()

    any_spec = pl.BlockSpec(memory_space=pl.ANY)
    return pl.pallas_call(
        body, name=name, in_specs=[any_spec] * n, out_specs=[any_spec] * n,
        out_shape=[S(a.shape if ps else (NCHIP,) + a.shape, a.dtype) for a, ps in arrs],
        scratch_shapes=[pltpu.SemaphoreType.DMA((3 * n,)), pltpu.SemaphoreType.DMA((3 * n,)), pltpu.SemaphoreType.DMA((n,))],
        )(*[a for a, _ in arrs])


def _adamw(r, w, m, v, *, name, tr):
    M, C = w.shape
    nparts = r.shape[0]

    def body(r_ref, w_ref, m_ref, v_ref, g_ref, d_ref, nm_ref, nv_ref):
        g = r_ref[0].astype(f32)
        for s in range(1, nparts):
            g = g + r_ref[s].astype(f32)
        m_ = B1 * m_ref[...] + (1.0 - B1) * g
        v_ = B2 * v_ref[...] + (1.0 - B2) * jnp.square(g)
        m_hat = m_ / (1.0 - B1 ** STEP)
        v_hat = v_ / (1.0 - B2 ** STEP)
        g_ref[...] = g
        d_ref[...] = -LR * (m_hat / (jnp.sqrt(v_hat) + AEPS) + WD * w_ref[...])
        nm_ref[...] = m_
        nv_ref[...] = v_

    row = pl.BlockSpec((tr, C), lambda i: (i, 0))
    return pl.pallas_call(
        body, name=name, grid=(M // tr,),
        in_specs=[pl.BlockSpec((nparts, tr, C), lambda i: (0, i, 0)), row, row, row],
        out_specs=[row] * 4, out_shape=[S((M, C), f32)] * 4, compiler_params=_params(("parallel",)))(r, w, m, v)


def _colmove(ins, in_slots, outs, moves, *, tk, name):
    R = ins[0].shape[1] if in_slots[0] else ins[0].shape[0]
    n_in = len(ins)

    def body(*refs):
        for ii, isl, ic, oi, osl, oc, w in moves:
            src, dst = refs[ii], refs[n_in + oi]
            val = src[:, ic:ic + w] if isl is None else src[isl, :, ic:ic + w]
            if osl is None:
                dst[:, oc:oc + w] = val.astype(dst.dtype)
            else:
                dst[osl, :, oc:oc + w] = val.astype(dst.dtype)

    def spec(is_slots, C):
        return pl.BlockSpec((NDEV, tk, C), lambda i: (0, i, 0)) if is_slots else pl.BlockSpec((tk, C), lambda i: (i, 0))

    return pl.pallas_call(
        body, name=name, grid=(R // tk,),
        in_specs=[spec(sl, a.shape[-1]) for a, sl in zip(ins, in_slots)],
        out_specs=[spec(sl, C) for sl, C, _ in outs],
        out_shape=[S((NDEV, R, C) if sl else (R, C), dt) for sl, C, dt in outs],
        compiler_params=_params(("parallel",)))(*ins)


def _col_pieces(n8, cuts, place):
    out = []
    for p in range(NDEV):
        lo, hi = p * n8, (p + 1) * n8
        edges = [lo] + [c for c in cuts if lo < c < hi] + [hi]
        for a, b in zip(edges[:-1], edges[1:]):
            out.append((p, a - lo) + place(a) + (b - a,))
    return out


def _place_plain(c):
    return (0, c)


def _place_ssd_in(c):
    return (0, c) if c < SSD_INNER + SSD_XBC else (1, c - (SSD_INNER + SSD_XBC))


def _place_ffn_up(c):
    h = FFN_TC // 2
    return (0, (c // h) * FFN_TC + c % h) if c < FFN else (0, ((c - FFN) // h) * FFN_TC + h + (c - FFN) % h)


_COL_LAYOUTS = {
    "ab_w_in": ([], _place_plain, [4 * RW + 3 * NAW]),
    "c_w_in": ([SSD_INNER + SSD_XBC], _place_ssd_in, [SSD_INNER + SSD_XBC, 2 * SSD_H]),
    "ffn_w_up": (list(range(FFN_TC // 2, 2 * FFN, FFN_TC // 2)), _place_ffn_up, [2 * FFN]),
}


def _cols_from_slots(g, which, *, name):
    cuts, place, widths = _COL_LAYOUTS[which]
    moves = [(0, p, sc, mi, None, mc, w) for p, sc, mi, mc, w in _col_pieces(g.shape[2], cuts, place)]
    return _colmove([g], [True], [(False, w, g.dtype) for w in widths], moves, tk=256, name=name)


def _cols_to_slots(mats, which, dtype, *, name):
    cuts, place, widths = _COL_LAYOUTS[which]
    n8 = sum(widths) // NDEV
    moves = [(mi, None, mc, 0, p, sc, w) for p, sc, mi, mc, w in _col_pieces(n8, cuts, place)]
    return _colmove(list(mats), [False] * len(mats), [(True, n8, dtype)], moves, tk=256, name=name)[0]


def _pack(parts, dtype, row_mult):
    flat = jnp.concatenate([p.reshape(-1).astype(dtype) for p in parts])
    rows = -(-flat.shape[0] // LANES)
    rows = -(-rows // row_mult) * row_mult
    return jnp.pad(flat, (0, rows * LANES - flat.shape[0])).reshape(rows, LANES)


def _unpack(buf, shapes, lead=()):
    flat = buf.reshape(lead + (-1,))
    out, off = [], 0
    for shp in shapes:
        n = int(np.prod(shp))
        out.append(flat[..., off:off + n].reshape(lead + tuple(shp)))
        off += n
    return out


def _to_slots(full, ax):
    shp = full.shape
    return jnp.moveaxis(full.reshape(shp[:ax] + (NDEV, shp[ax] // NDEV) + shp[ax + 1:]), ax, 0)


def _from_slots(g, ax):
    t = jnp.moveaxis(g, 0, ax)
    shp = t.shape
    return t.reshape(shp[:ax] + (shp[ax] * shp[ax + 1],) + shp[ax + 2:])


def _ffn_perm(a):
    lead = a.shape[:-1]
    h = FFN_TC // 2
    return jnp.swapaxes(a.reshape(lead + (2, FFN // h, h)), -3, -2).reshape(lead + (2 * FFN,))


def _ffn_unperm(a):
    lead = a.shape[:-1]
    h = FFN_TC // 2
    return jnp.swapaxes(a.reshape(lead + (FFN // h, 2, h)), -3, -2).reshape(lead + (2 * FFN,))


def _rope_tables(T):
    half = RDH // 2
    inv = 1.0 / (ROPE_BASE ** (jnp.arange(half, dtype=f32) / half))
    ang = jnp.arange(T, dtype=f32)[:, None] * inv[None, :]
    cos, sin = jnp.cos(ang), jnp.sin(ang)
    cos_t = jnp.tile(jnp.concatenate([cos, cos], axis=1), (1, RH))
    sin_t = jnp.tile(jnp.concatenate([-sin, sin], axis=1), (1, RH))
    return cos_t, sin_t


def _group_avg():
    g = np.arange(RW) // RDH
    return jnp.asarray((g[:, None] == g[None, :]).astype(np.float32) / RDH)


def _head_expand():
    hd = np.arange(SSD_INNER) // SSD_HD
    rows = np.arange(2 * SSD_H)
    ex0 = (rows[:, None] == hd[None, :]).astype(np.float32)
    ex1 = (rows[:, None] == SSD_H + hd[None, :]).astype(np.float32)
    return jnp.asarray(ex0), jnp.asarray(ex1)


def kernel(x, norm_mix_pre, norm_mix_post, norm_ffn_pre, norm_ffn_post, ab_w_in, ab_ret_decay_logit, ab_ret_gn_g, ab_na_rpb, ab_w_out, c_w_in, c_conv_w, c_conv_b, c_dt_bias, c_a_log, c_d_skip, c_norm_g, c_w_out, ffn_w_up, ffn_conv_w, ffn_conv_b, ffn_w_down, loss_target, m_norm_mix_pre, m_norm_mix_post, m_norm_ffn_pre, m_norm_ffn_post, m_ab_w_in, m_ab_ret_decay_logit, m_ab_ret_gn_g, m_ab_na_rpb, m_ab_w_out, m_c_w_in, m_c_conv_w, m_c_conv_b, m_c_dt_bias, m_c_a_log, m_c_d_skip, m_c_norm_g, m_c_w_out, m_ffn_w_up, m_ffn_conv_w, m_ffn_conv_b, m_ffn_w_down, v_norm_mix_pre, v_norm_mix_post, v_norm_ffn_pre, v_norm_ffn_post, v_ab_w_in, v_ab_ret_decay_logit, v_ab_ret_gn_g, v_ab_na_rpb, v_ab_w_out, v_c_w_in, v_c_conv_w, v_c_conv_b, v_c_dt_bias, v_c_a_log, v_c_d_skip, v_c_norm_g, v_c_w_out, v_ffn_w_up, v_ffn_conv_w, v_ffn_conv_b, v_ffn_w_down):
    W = dict(norm_mix_pre=norm_mix_pre, norm_mix_post=norm_mix_post, norm_ffn_pre=norm_ffn_pre, norm_ffn_post=norm_ffn_post, ab_w_in=ab_w_in, ab_ret_decay_logit=ab_ret_decay_logit, ab_ret_gn_g=ab_ret_gn_g, ab_na_rpb=ab_na_rpb, ab_w_out=ab_w_out, c_w_in=c_w_in, c_conv_w=c_conv_w, c_conv_b=c_conv_b, c_dt_bias=c_dt_bias, c_a_log=c_a_log, c_d_skip=c_d_skip, c_norm_g=c_norm_g, c_w_out=c_w_out, ffn_w_up=ffn_w_up, ffn_conv_w=ffn_conv_w, ffn_conv_b=ffn_conv_b, ffn_w_down=ffn_w_down)
    Mo = dict(norm_mix_pre=m_norm_mix_pre, norm_mix_post=m_norm_mix_post, norm_ffn_pre=m_norm_ffn_pre, norm_ffn_post=m_norm_ffn_post, ab_w_in=m_ab_w_in, ab_ret_decay_logit=m_ab_ret_decay_logit, ab_ret_gn_g=m_ab_ret_gn_g, ab_na_rpb=m_ab_na_rpb, ab_w_out=m_ab_w_out, c_w_in=m_c_w_in, c_conv_w=m_c_conv_w, c_conv_b=m_c_conv_b, c_dt_bias=m_c_dt_bias, c_a_log=m_c_a_log, c_d_skip=m_c_d_skip, c_norm_g=m_c_norm_g, c_w_out=m_c_w_out, ffn_w_up=m_ffn_w_up, ffn_conv_w=m_ffn_conv_w, ffn_conv_b=m_ffn_conv_b, ffn_w_down=m_ffn_w_down)
    Vo = dict(norm_mix_pre=v_norm_mix_pre, norm_mix_post=v_norm_mix_post, norm_ffn_pre=v_norm_ffn_pre, norm_ffn_post=v_norm_ffn_post, ab_w_in=v_ab_w_in, ab_ret_decay_logit=v_ab_ret_decay_logit, ab_ret_gn_g=v_ab_ret_gn_g, ab_na_rpb=v_ab_na_rpb, ab_w_out=v_ab_w_out, c_w_in=v_c_w_in, c_conv_w=v_c_conv_w, c_conv_b=v_c_conv_b, c_dt_bias=v_c_dt_bias, c_a_log=v_c_a_log, c_d_skip=v_c_d_skip, c_norm_g=v_c_norm_g, c_w_out=v_c_w_out, ffn_w_up=v_ffn_w_up, ffn_conv_w=v_ffn_conv_w, ffn_conv_b=v_ffn_conv_b, ffn_w_down=v_ffn_w_down)
    return _train_step(x[0], loss_target[0], W, Mo, Vo)


def _train_step(x, tgt, W, Mo, Vo):
    T = x.shape[0]
    rows = T // GRID_W

    col = lambda d, n, dt: d[n].reshape(-1, d[n].shape[-1]).astype(dt)
    rows_of = lambda d, dt: jnp.concatenate([col(d, n, dt) for n in ROW_SHARDED], axis=0)
    small = _pack([W[n] for n, _ in SHARDED[N_BIG:]], f32, 8)
    gat = _all_gather([col(W, n, bf16) for n in COL_SHARDED] + [rows_of(W, bf16), small], name="gather_weights")
    per_layer = lambda m: m.reshape(-1, D, m.shape[-1])
    w_ab_in = per_layer(_cols_from_slots(gat[0], "ab_w_in", name="cols_ab_w_in")[0])
    w_zx, w_dt = [per_layer(m) for m in _cols_from_slots(gat[1], "c_w_in", name="cols_c_w_in")]
    w_up = per_layer(_cols_from_slots(gat[2], "ffn_w_up", name="cols_ffn_w_up")[0])
    full, off = {}, 0
    for n in ROW_SHARDED:
        L, r = W[n].shape[0], W[n].shape[1]
        full[n] = jnp.swapaxes(gat[3][:, off:off + L * r].reshape(NDEV, L, r, D), 0, 1).reshape(L, NDEV * r, D)
        off += L * r
    gs = _unpack(gat[4], [W[n].shape for n, _ in SHARDED[N_BIG:]], (NDEV,))
    full.update({n: _from_slots(g, ax) for (n, ax), g in zip(SHARDED[N_BIG:], gs)})
    w_ab_out, w_c_out, w_down = full["ab_w_out"], full["c_w_out"], full["ffn_w_down"]
    c_cw8 = [_pad8(full["c_conv_w"][i]) for i in range(2)]
    c_cb = [full["c_conv_b"][i][None] for i in range(2)]
    c_ng = [full["c_norm_g"][i][None] for i in range(2)]
    f_cw8 = [_pad8(_ffn_perm(full["ffn_conv_w"][l])) for l in range(DEPTH)]
    f_cb = [_ffn_perm(W["ffn_conv_b"][l])[None] for l in range(DEPTH)]

    g1 = [W["norm_mix_pre"][l][None] for l in range(DEPTH)]
    g2 = [W["norm_mix_post"][l][None] for l in range(DEPTH)]
    g3 = [W["norm_ffn_pre"][l][None] for l in range(DEPTH)]
    g4 = [W["norm_ffn_post"][l][None] for l in range(DEPTH)]
    cos_t, sin_t = _rope_tables(T)
    gavg = _group_avg()
    ex0, ex1 = _head_expand()

    def log_gamma(logit):
        return -jax.nn.softplus(-logit)

    def ret_decays(lg):
        return [jnp.broadcast_to(lg[d][None, None, :], (1, T, RH)) for d in range(2)]

    saved = []
    xs_ = x
    hn = _rowwise("norm_first", _f_first, [(x, D, 0)], [], [(g1[0], D, 0)], [], [(D, bf16)], tm=ROW_TILE)[0]
    for l in range(DEPTH):
        i = l // 2
        sv = dict(x=xs_, hn=hn)
        if l % 2 == 0:
            proj = _mm_nn(hn, w_ab_in[i], name=f"ab_in_{l}")
            qr, kr = _rowwise(f"ret_prep_{l}", _f_rprep, [(proj, RW, 0), (proj, RW, 1)], [(cos_t, RW, 0), (sin_t, RW, 0)], [], [],
                              [(RW, f32), (RW, f32)], tm=ROW_TILE)
            lg, lg_vjp = jax.vjp(log_gamma, W["ab_ret_decay_logit"][i])
            a_f, a_b = ret_decays(lg)
            rscan = dict(G=RH, N=RDH, Hg=1, P=RDH, vcol=2)
            yf_t, hsf = _scan_fwd(qr, kr, proj, a_f, rev=False, name=f"ret_scan_f_{l}", **rscan)
            y_t, hsb = _scan_fwd(qr, kr, proj, a_b, rev=True, name=f"ret_scan_b_{l}", add_y=yf_t, **rscan)
            gn = W["ab_ret_gn_g"][i][None]
            ret = _rowwise(f"ret_post_{l}", _f_rpost, [(y_t, RW, 0), (proj, RW, 3)], [], [(gn, RW, 0)], [gavg],
                           [(RW, bf16)], tm=ROW_TILE)[0]
            nqkv = proj[:, 4 * RW:].astype(bf16)
            ncols = dict(qcol=0, kcol=NAW // 128, vcol=2 * NAW // 128)
            r1, bias_vjp = jax.vjp(_na_col_bias, W["ab_na_rpb"][i])
            bias = _na_bias_build(r1, rows, name=f"na_bias_{l}")
            na_o, na_l = _na_fwd(nqkv, nqkv, nqkv, bias, name=f"na_fwd_{l}", **ncols)
            cat = jnp.concatenate([ret, na_o.astype(bf16)], axis=1)
            mo = _mm_nn(cat, w_ab_out[i], name=f"ab_out_{l}")
            sv.update(proj=proj, qr=qr, kr=kr, a_f=a_f, a_b=a_b, hsf=hsf, hsb=hsb, y_t=y_t, gn=gn, rscan=rscan,
                      nqkv=nqkv, ncols=ncols, bias=bias, bias_vjp=bias_vjp, lg_vjp=lg_vjp, na_o=na_o, na_l=na_l, cat=cat)
        else:
            zx = _mm_nn(hn, w_zx[i], name=f"c_in_{l}")
            dtr = _mm_nn(hn, w_dt[i], name=f"c_in_dt_{l}")
            xa = _conv(zx, c_cw8[i], c_cb[i], mode="silu", W=SSD_CONV, name=f"c_conv_{l}", C=SSD_XBC, xbase=SSD_INNER // 512)
            dtb, alog = W["c_dt_bias"][i].reshape(1, 2 * SSD_H), W["c_a_log"][i].reshape(1, 2 * SSD_H)
            vf, vb, la = _rowwise(f"ssd_prep_{l}", _f_sprep, [(xa, SSD_INNER, 0), (dtr, 2 * SSD_H, 0)], [],
                                  [(dtb, 2 * SSD_H, 0), (alog, 2 * SSD_H, 0)], [ex0, ex1],
                                  [(SSD_INNER, f32), (SSD_INNER, f32), (2 * SSD_H, f32)], tm=WIDE_ROW_TILE)
            a_f = la[:, :SSD_H].reshape(T, SSD_G, SSD_HPG).transpose(1, 0, 2)
            a_b = la[:, SSD_H:].reshape(T, SSD_G, SSD_HPG).transpose(1, 0, 2)
            sscan = dict(G=SSD_G, N=SSD_N, Hg=SSD_HPG, P=SSD_HD, qcol=(SSD_INNER + SSD_G * SSD_N) // SSD_N, kcol=SSD_INNER // SSD_N)
            yf_t, hsf = _scan_fwd(xa, xa, vf, a_f, rev=False, name=f"ssd_scan_f_{l}", **sscan)
            y_t, hsb = _scan_fwd(xa, xa, vb, a_b, rev=True, name=f"ssd_scan_b_{l}", add_y=yf_t, **sscan)
            dsk = jnp.repeat(W["c_d_skip"][i], SSD_HD)[None]
            yo = _rowwise(f"ssd_post_{l}", _f_spost, [(y_t, 512, 0), (xa, 512, 0), (zx, 512, 0)], [],
                          [(dsk, 512, 0), (c_ng[i], 512, 0)], [], [(512, bf16)], tm=ROW_TILE, J=SSD_G)[0]
            mo = _mm_nn(yo, w_c_out[i], name=f"c_out_{l}")
            sv.update(zx=zx, dtr=dtr, xa=xa, dtb=dtb, alog=alog, a_f=a_f, a_b=a_b, vf=vf, vb=vb, sscan=sscan,
                      hsf=hsf, hsb=hsb, y_t=y_t, dsk=dsk, yo=yo)
        x1, hf = _rowwise(f"norm_mid_{l}", _f_mid, [(xs_, D, 0), (mo, D, 0)], [], [(g2[l], D, 0), (g3[l], D, 0)], [],
                          [(D, f32), (D, bf16)], tm=ROW_TILE)
        pre = _mm_nn(hf, w_up[l], name=f"ffn_up_{l}")
        act = _conv(pre, f_cw8[l], f_cb[l], mode="geglu", W=FFN_CONV, name=f"ffn_conv_{l}", C=2 * FFN, tc=FFN_TC, out_dtype=bf16)
        fo = _mm_nn(act, w_down[l], name=f"ffn_down_{l}")
        sv.update(mo=mo, x1=x1, hf=hf, pre=pre, act=act, fo=fo)
        if l < DEPTH - 1:
            xs_, hn = _rowwise(f"norm_end_{l}", _f_end, [(x1, D, 0), (fo, D, 0)], [], [(g4[l], D, 0), (g1[l + 1], D, 0)], [],
                               [(D, f32), (D, bf16)], tm=ROW_TILE)
        else:
            xs_ = _rowwise(f"norm_end_{l}", _f_last, [(x1, D, 0), (fo, D, 0)], [], [(g4[l], D, 0)], [], [(D, f32)], tm=ROW_TILE)[0]
        saved.append(sv)

    dx, lpart = _loss_call(xs_, tgt)
    loss = lax.psum(lpart[0, 0], ("x", "y", "c"))

    G = {n: [None] * W[n].shape[0] for n in WEIGHTS}
    dhn = None
    for l in reversed(range(DEPTH)):
        i = l // 2
        sv = saved[l]
        if l == DEPTH - 1:
            (dx1, dfo), (dg4,) = _rowwise_bwd(f"norm_end_bwd_{l}", _f_last, [(sv["x1"], D, 0), (sv["fo"], D, 0)], [],
                                              [(g4[l], D, 0)], [], [(dx, D, 0)], [f32, bf16], tm=ROW_TILE)
        else:
            (dx1, dfo), (dg4, dg1n) = _rowwise_bwd(f"norm_end_bwd_{l}", _f_end, [(sv["x1"], D, 0), (sv["fo"], D, 0)], [],
                                                   [(g4[l], D, 0), (g1[l + 1], D, 0)], [], [(dx, D, 0), (dhn, D, 0)],
                                                   [f32, bf16], tm=ROW_TILE)
            G["norm_mix_pre"][l + 1] = dg1n[0]
        G["norm_ffn_post"][l] = dg4[0]
        dact = _mm_nt(dfo, w_down[l], name=f"ffn_down_dx_{l}")
        G["ffn_w_down"][l] = _mm_tn(sv["act"], dfo, name=f"ffn_down_dw_{l}")
        dpre, dfw, dfb = _conv_bwd(sv["pre"], f_cw8[l], f_cb[l], dact, mode="geglu", W=FFN_CONV, name=f"ffn_conv_bwd_{l}",
                                   C=2 * FFN, tc=FFN_TC)
        dhf = _mm_nt(dpre, w_up[l], name=f"ffn_up_dx_{l}")
        G["ffn_w_up"][l] = _cols_to_slots([_mm_tn(sv["hf"], dpre, name=f"ffn_up_dw_{l}")], "ffn_w_up", bf16, name=f"slots_ffn_up_{l}")
        G["ffn_conv_w"][l] = _ffn_unperm(dfw[:FFN_CONV])
        G["ffn_conv_b"][l] = _ffn_unperm(dfb[0])
        (dxl, dmo), (dg2, dg3) = _rowwise_bwd(f"norm_mid_bwd_{l}", _f_mid, [(sv["x"], D, 0), (sv["mo"], D, 0)], [],
                                              [(g2[l], D, 0), (g3[l], D, 0)], [], [(dx1, D, 0), (dhf, D, 0)], [f32, bf16], tm=ROW_TILE)
        G["norm_mix_post"][l] = dg2[0]
        G["norm_ffn_pre"][l] = dg3[0]
        if l % 2 == 0:
            dcat = _mm_nt(dmo, w_ab_out[i], name=f"ab_out_dx_{l}")
            G["ab_w_out"][i] = _mm_tn(sv["cat"], dmo, name=f"ab_out_dw_{l}")
            (dy, drg), (dgn,) = _rowwise_bwd(
                f"ret_post_bwd_{l}", _f_rpost, [(sv["y_t"], RW, 0), (sv["proj"], RW, 3)], [],
                [(sv["gn"], RW, 0)], [gavg], [(dcat, RW, 0)], [f32, bf16], tm=ROW_TILE)
            G["ab_ret_gn_g"][i] = dgn[0]
            dqf, dkf, dvf, daf = _scan_bwd(sv["qr"], sv["kr"], sv["proj"], sv["a_f"], sv["hsf"], dy, rev=False,
                                           name=f"ret_scan_f_bwd_{l}", **sv["rscan"])
            dq_t, dk_t, dv_t, dab = _scan_bwd(sv["qr"], sv["kr"], sv["proj"], sv["a_b"], sv["hsb"], dy, rev=True,
                                              name=f"ret_scan_b_bwd_{l}", add_to=(dqf, dkf, dvf), **sv["rscan"])
            drv = dv_t.astype(bf16)
            (drq, drk), _ = _rowwise_bwd(f"ret_prep_bwd_{l}", _f_rprep, [(sv["proj"], RW, 0), (sv["proj"], RW, 1)],
                                         [(cos_t, RW, 0), (sin_t, RW, 0)], [], [], [(dq_t, RW, 0), (dk_t, RW, 0)], [bf16, bf16], tm=ROW_TILE)
            da_cols = jnp.concatenate([daf[0], dab[0]], axis=1)
            dlg = _colsum(da_cols, name=f"ret_decay_sum_{l}").reshape(2, RH)
            G["ab_ret_decay_logit"][i] = sv["lg_vjp"](dlg)[0]
            dnq, dnk, dnv, dbias = _na_bwd(sv["nqkv"], sv["nqkv"], sv["nqkv"], sv["bias"], sv["na_o"], sv["na_l"], dcat,
                                           docol=RW // 128, name=f"na_bwd_{l}", **sv["ncols"])
            G["ab_na_rpb"][i] = sv["bias_vjp"](_na_bias_fold(dbias, rows, name=f"na_bias_fold_{l}"))[0]
            dproj = jnp.concatenate([drq, drk, drv, drg] + [t.astype(bf16) for t in (dnq, dnk, dnv)], axis=1)
            dhn = _mm_nt(dproj, w_ab_in[i], name=f"ab_in_dx_{l}")
            G["ab_w_in"][i] = _cols_to_slots([_mm_tn(sv["hn"], dproj, name=f"ab_in_dw_{l}")], "ab_w_in", bf16, name=f"slots_ab_in_{l}")
        else:
            dyo = _mm_nt(dmo, w_c_out[i], name=f"c_out_dx_{l}")
            G["c_w_out"][i] = _mm_tn(sv["yo"], dmo, name=f"c_out_dw_{l}")
            (dy, dxs1, dz), (ddsk, dng) = _rowwise_bwd(
                f"ssd_post_bwd_{l}", _f_spost, [(sv["y_t"], 512, 0), (sv["xa"], 512, 0), (sv["zx"], 512, 0)],
                [], [(sv["dsk"], 512, 0), (c_ng[i], 512, 0)], [], [(dyo, 512, 0)], [f32, f32, bf16], tm=ROW_TILE, J=SSD_G)
            G["c_d_skip"][i] = ddsk.reshape(SSD_H, SSD_HD).sum(axis=1)
            G["c_norm_g"][i] = dng[0]
            dqf, dkf, dvf, daf = _scan_bwd(sv["xa"], sv["xa"], sv["vf"], sv["a_f"], sv["hsf"], dy, rev=False,
                                           name=f"ssd_scan_f_bwd_{l}", **sv["sscan"])
            dq_t, dk_t, dvb, dab = _scan_bwd(sv["xa"], sv["xa"], sv["vb"], sv["a_b"], sv["hsb"], dy, rev=True,
                                             name=f"ssd_scan_b_bwd_{l}", add_to=(dqf, dkf, None), **sv["sscan"])
            dla = jnp.concatenate([daf.transpose(1, 0, 2).reshape(T, SSD_H), dab.transpose(1, 0, 2).reshape(T, SSD_H)], axis=1)
            (dxs, ddtr), (ddtb, dalog) = _rowwise_bwd(
                f"ssd_prep_bwd_{l}", _f_sprep_bwd, [(sv["xa"], SSD_INNER, 0), (sv["dtr"], 2 * SSD_H, 0)], [],
                [(sv["dtb"], 2 * SSD_H, 0), (sv["alog"], 2 * SSD_H, 0)], [ex0, ex1],
                [(dvf, SSD_INNER, 0), (dvb, SSD_INNER, 0), (dla, 2 * SSD_H, 0), (dxs1, SSD_INNER, 0)], [f32, bf16], tm=WIDE_ROW_TILE)
            G["c_dt_bias"][i] = ddtb.reshape(2, SSD_H)
            G["c_a_log"][i] = dalog.reshape(2, SSD_H)
            dxa = jnp.concatenate([dxs, dk_t, dq_t], axis=1)
            dxbc, dcw, dcb = _conv_bwd(sv["zx"], c_cw8[i], c_cb[i], dxa, mode="silu", W=SSD_CONV, name=f"c_conv_bwd_{l}",
                                       C=SSD_XBC, xbase=SSD_INNER // 512)
            G["c_conv_w"][i] = dcw[:SSD_CONV]
            G["c_conv_b"][i] = dcb[0]
            dzx = jnp.concatenate([dz, dxbc], axis=1)
            t1 = _mm_nt(ddtr, w_dt[i], name=f"c_in_dt_dx_{l}")
            dhn = _mm_nt(dzx, w_zx[i], add=t1, name=f"c_in_dx_{l}")
            G["c_w_in"][i] = _cols_to_slots([_mm_tn(sv["hn"], dzx, name=f"c_in_dw_{l}"), _mm_tn(sv["hn"], ddtr, name=f"c_in_dt_dw_{l}")],
                                            "c_w_in", bf16, name=f"slots_c_in_{l}")
        dx = dxl
    (grad_x,), (dg1,) = _rowwise_bwd("norm_first_bwd", _f_first_bwd, [(x, D, 0)], [], [(g1[0], D, 0)], [], [(dx, D, 0), (dhn, D, 0)],
                                     [f32], tm=ROW_TILE)
    G["norm_mix_pre"][0] = dg1[0]

    small_names = [n for n, _ in SHARDED[N_BIG:]]
    col_slots = [jnp.concatenate(G[n], axis=1) for n in COL_SHARDED]
    row_slots = jnp.concatenate([g.reshape(NDEV, -1, D).astype(bf16) for n in ROW_SHARDED for g in G[n]], axis=1)
    small_slots = _pack_slots([_to_slots(jnp.stack(G[n]), ax) for n, ax in SHARDED[N_BIG:]], 8)
    ar = _pack([jnp.stack(G[n]) for n in REPLICATED], f32, 8)
    parts = [(a, True) for a in col_slots + [row_slots, small_slots]] + [(ar, False)]
    from_sib = _to_sibling(parts, name="grads_to_sibling")
    tiles = [256, 256, 256, 64, small_slots.shape[1], ar.shape[0]]
    chip = [_add_partials(a, b, per_slot=ps, tr=t, name=f"grads_add_{j}")
            for j, ((a, ps), b, t) in enumerate(zip(parts, from_sib, tiles))]
    exch = _to_chips([(a, ps) for a, (_, ps) in zip(chip, parts)], name="grads_to_chips")
    pk = lambda d, names: _pack([d[n] for n in names], f32, 8)
    upd = [_adamw(exch[j], col(W, n, f32), col(Mo, n, f32), col(Vo, n, f32), name=f"adamw_{n}", tr=256)
           for j, n in enumerate(COL_SHARDED)]
    upd_rows = _adamw(exch[3], rows_of(W, f32), rows_of(Mo, f32), rows_of(Vo, f32), name="adamw_rows", tr=64)
    upd_small = _adamw(exch[4], pk(W, small_names), pk(Mo, small_names), pk(Vo, small_names), name="adamw_small",
                       tr=small_slots.shape[1])
    upd_rep = _adamw(exch[5], pk(W, REPLICATED), pk(Mo, REPLICATED), pk(Vo, REPLICATED), name="adamw_replicated", tr=ar.shape[0])
    res = []
    for k in range(4):
        d = {n: upd[j][k].reshape(W[n].shape) for j, n in enumerate(COL_SHARDED)}
        off = 0
        for n in ROW_SHARDED:
            cnt = W[n].shape[0] * W[n].shape[1]
            d[n] = upd_rows[k][off:off + cnt].reshape(W[n].shape)
            off += cnt
        d.update(zip(small_names, _unpack(upd_small[k], [W[n].shape for n in small_names])))
        d.update(zip(REPLICATED, _unpack(upd_rep[k], [W[n].shape for n in REPLICATED])))
        res.append(d)
    outs = [loss, grad_x[None]]
    for k in range(4):
        outs += [res[k][n] for n in WEIGHTS]
    return tuple(outs)


def _pack_slots(slot_arrays, row_mult):
    flat = jnp.concatenate([a.reshape(NDEV, -1) for a in slot_arrays], axis=1)
    rows = -(-flat.shape[1] // LANES)
    rows = -(-rows // row_mult) * row_mult
    return jnp.pad(flat, ((0, 0), (0, rows * LANES - flat.shape[1]))).reshape(NDEV, rows, LANES)
```

```python
import functools
import numpy as np
import jax
import jax.numpy as jnp
from jax import lax
from jax.experimental import pallas as pl
from jax.experimental.pallas import tpu as pltpu

f32, bf16 = jnp.float32, jnp.bfloat16
S = jax.ShapeDtypeStruct
HI = lax.Precision.HIGHEST

D = 1024
DEPTH = 4
GRID_W = 64
CHUNK = 128
EPS = 1e-6
RH, RDH, RW = 8, 64, 512
NAH, NADH, NAW = 8, 64, 512
NA_WR, NA_WC = 8, 16
NA_QROWS = 8
NA_KROWS = 16
NA_PAIR = 2
SSD_INNER, SSD_HD, SSD_H, SSD_G, SSD_HPG, SSD_N, SSD_CONV = 2048, 64, 32, 4, 8, 128, 5
SSD_XBC = SSD_INNER + 2 * SSD_G * SSD_N
FFN, FFN_CONV = 2816, 3
FFN_TC = 512
SCAN_HEADS_PER_STEP = 8
ROPE_BASE = 10000.0
LR, B1, B2, AEPS, WD, STEP = 0.001, 0.9, 0.999, 1e-08, 0.01, 10
NDEV = 8
LANES = 128
VMEM_LIMIT = 56 * 1024 * 1024
MM_BLOCK_BYTES = 6 * 1024 * 1024
ROW_TILE = 512
WIDE_ROW_TILE = 256
CONV_TILE = 1024

NT = (((1,), (1,)), ((), ()))
TN = (((0,), (0,)), ((), ()))

SHARDED = [("ab_w_in", 2), ("ab_w_out", 1), ("c_w_in", 2), ("c_w_out", 1), ("ffn_w_up", 2), ("ffn_w_down", 1),
           ("c_conv_w", 2), ("c_conv_b", 1), ("c_norm_g", 1), ("ffn_conv_w", 2)]
N_BIG = 6
COL_SHARDED = ["ab_w_in", "c_w_in", "ffn_w_up"]
ROW_SHARDED = ["ab_w_out", "c_w_out", "ffn_w_down"]
REPLICATED = ["norm_mix_pre", "norm_mix_post", "norm_ffn_pre", "norm_ffn_post", "ab_ret_decay_logit", "ab_ret_gn_g",
              "ab_na_rpb", "c_dt_bias", "c_a_log", "c_d_skip", "ffn_conv_b"]
WEIGHTS = ["norm_mix_pre", "norm_mix_post", "norm_ffn_pre", "norm_ffn_post", "ab_w_in", "ab_ret_decay_logit",
           "ab_ret_gn_g", "ab_na_rpb", "ab_w_out", "c_w_in", "c_conv_w", "c_conv_b", "c_dt_bias", "c_a_log", "c_d_skip",
           "c_norm_g", "c_w_out", "ffn_w_up", "ffn_conv_w", "ffn_conv_b", "ffn_w_down"]


def _params(sem=None):
    return pltpu.CompilerParams(dimension_semantics=sem, vmem_limit_bytes=VMEM_LIMIT)


def _mm_nn(a, w, *, name, tm=1024, tn=512, out_dtype=f32):
    M, K = a.shape
    N = w.shape[1]
    tn = min(tn, N)

    def body(a_ref, w_ref, o_ref):
        o_ref[...] = jnp.dot(a_ref[...], w_ref[...], preferred_element_type=f32).astype(o_ref.dtype)

    return pl.pallas_call(
        body, name=name, grid=(M // tm, N // tn),
        in_specs=[pl.BlockSpec((tm, K), lambda i, j: (i, 0)), pl.BlockSpec((K, tn), lambda i, j: (0, j))],
        out_specs=pl.BlockSpec((tm, tn), lambda i, j: (i, j)),
        out_shape=S((M, N), out_dtype), compiler_params=_params(("parallel", "parallel")))(a, w)


def _mm_nt(dy, w, *, name, add=None, tm=1024):
    M, N = dy.shape
    K = w.shape[0]
    tk = next((t for t in (1024, 1408, 512, 256, 128) if K % t == 0 and (t <= 512 or t * N * 2 <= MM_BLOCK_BYTES)), K)

    def body(*refs):
        if add is None:
            d_ref, w_ref, o_ref = refs
            o_ref[...] = lax.dot_general(d_ref[...], w_ref[...], NT, preferred_element_type=f32)
        else:
            d_ref, w_ref, a_ref, o_ref = refs
            o_ref[...] = lax.dot_general(d_ref[...], w_ref[...], NT, preferred_element_type=f32) + a_ref[...]

    in_specs = [pl.BlockSpec((tm, N), lambda i, j: (i, 0)), pl.BlockSpec((tk, N), lambda i, j: (j, 0))]
    args = [dy, w]
    if add is not None:
        in_specs.append(pl.BlockSpec((tm, tk), lambda i, j: (i, j)))
        args.append(add)
    return pl.pallas_call(
        body, name=name, grid=(M // tm, K // tk), in_specs=in_specs,
        out_specs=pl.BlockSpec((tm, tk), lambda i, j: (i, j)),
        out_shape=S((M, K), f32), compiler_params=_params(("parallel", "parallel")))(*args)


def _mm_tn(a, dy, *, name, tt=2048):
    M, K = a.shape
    N = dy.shape[1]
    tk = K if K <= 1024 else (1024 if K % 1024 == 0 else K // 2)
    tn = min(512, N)
    tt = min(tt, M)

    def body(a_ref, d_ref, o_ref):
        t = pl.program_id(2)
        part = lax.dot_general(a_ref[...], d_ref[...], TN, preferred_element_type=f32)

        @pl.when(t == 0)
        def _():
            o_ref[...] = part

        @pl.when(t > 0)
        def _():
            o_ref[...] += part

    return pl.pallas_call(
        body, name=name, grid=(K // tk, N // tn, M // tt),
        in_specs=[pl.BlockSpec((tt, tk), lambda k, n, t: (t, k)), pl.BlockSpec((tt, tn), lambda k, n, t: (t, n))],
        out_specs=pl.BlockSpec((tk, tn), lambda k, n, t: (k, n)),
        out_shape=S((K, N), f32), compiler_params=_params(("parallel", "parallel", "arbitrary")))(a, dy)


def _tile_spec(tm, width, base):
    return pl.BlockSpec((tm, width), lambda j, i: (i, base + j))


def _par_spec(width, base):
    return pl.BlockSpec((1, width), lambda j, i: (0, base + j))


def _full_spec(a):
    nd = a.ndim
    return pl.BlockSpec(a.shape, lambda j, i: (0,) * nd)


def _rowwise(name, f, tiles, ctiles, params, consts, outs, *, tm, J=1):
    T = tiles[0][0].shape[0]
    nt, nct, npar, nc = len(tiles), len(ctiles), len(params), len(consts)

    def body(*refs):
        tv = [r[...].astype(f32) for r in refs[:nt + nct]]
        pv = [r[...] for r in refs[nt + nct:nt + nct + npar + nc]]
        res = f(*tv, *pv)
        for o, v in zip(refs[nt + nct + npar + nc:], res):
            o[...] = v.astype(o.dtype)

    in_specs = ([_tile_spec(tm, w, b) for _, w, b in tiles + ctiles] + [_par_spec(w, b) for _, w, b in params]
                + [_full_spec(c) for c in consts])
    return pl.pallas_call(
        body, name=name, grid=(J, T // tm), in_specs=in_specs,
        out_specs=[_tile_spec(tm, w, 0) for w, _ in outs],
        out_shape=[S((T, J * w), dt) for w, dt in outs],
        compiler_params=_params(("parallel", "parallel")))(
            *[a for a, _, _ in tiles + ctiles], *[a for a, _, _ in params], *consts)


def _rowwise_bwd(name, f, tiles, ctiles, params, consts, douts, dtile_dtypes, *, tm, J=1):
    T = tiles[0][0].shape[0]
    nt, nct, npar, nc, nd = len(tiles), len(ctiles), len(params), len(consts), len(douts)

    def body(*refs):
        i = pl.program_id(1)
        k = 0
        tv = [r[...].astype(f32) for r in refs[k:k + nt]]; k += nt
        cv = [r[...].astype(f32) for r in refs[k:k + nct]]; k += nct
        pv = [r[...] for r in refs[k:k + npar]]; k += npar
        kv = [r[...] for r in refs[k:k + nc]]; k += nc
        dv = [r[...].astype(f32) for r in refs[k:k + nd]]; k += nd
        dt_refs = refs[k:k + nt]; k += nt
        dp_refs = refs[k:k + npar]
        _, vjp = jax.vjp(lambda tv_, pv_: tuple(f(*tv_, *cv, *pv_, *kv)), tv, pv)
        dts, dps = vjp(tuple(dv))
        for r, g in zip(dt_refs, dts):
            r[...] = g.astype(r.dtype)
        for r, g in zip(dp_refs, dps):
            @pl.when(i == 0)
            def _(r=r, g=g):
                r[...] = g

            @pl.when(i > 0)
            def _(r=r, g=g):
                r[...] += g

    in_specs = ([_tile_spec(tm, w, b) for _, w, b in tiles + ctiles] + [_par_spec(w, b) for _, w, b in params]
                + [_full_spec(c) for c in consts] + [_tile_spec(tm, w, b) for _, w, b in douts])
    res = pl.pallas_call(
        body, name=name, grid=(J, T // tm), in_specs=in_specs,
        out_specs=[_tile_spec(tm, w, 0) for _, w, _ in tiles] + [_par_spec(w, b) for _, w, b in params],
        out_shape=[S((T, J * w), dt) for (_, w, _), dt in zip(tiles, dtile_dtypes)] + [S(a.shape, f32) for a, _, _ in params],
        compiler_params=_params(("parallel", "arbitrary")))(
            *[a for a, _, _ in tiles + ctiles], *[a for a, _, _ in params], *consts, *[a for a, _, _ in douts])
    return res[:nt], res[nt:]


def _rms(x, g):
    return x * lax.rsqrt(jnp.mean(x * x, axis=-1, keepdims=True) + EPS) * g


def _f_first(x, g1):
    return (_rms(x, g1),)


def _f_first_bwd(x, g1):
    return (x, _rms(x, g1))


def _f_mid(x, m, g2, g3):
    x1 = x + _rms(m, g2)
    return (x1, _rms(x1, g3))


def _f_end(x1, fo, g4, g1n):
    x2 = x1 + _rms(fo, g4)
    return (x2, _rms(x2, g1n))


def _f_last(x1, fo, g4):
    return (x1 + _rms(fo, g4),)


@jax.custom_vjp
def _swap_halves(x):
    c = x.shape[1]
    lane = lax.broadcasted_iota(jnp.int32, x.shape, 1) % RDH
    return jnp.where(lane < RDH // 2, pltpu.roll(x, c - RDH // 2, axis=1), pltpu.roll(x, RDH // 2, axis=1))


_swap_halves.defvjp(lambda x: (_swap_halves(x), None), lambda _, g: (_swap_halves(g),))


def _f_rprep(rq, rk, cos, sin):
    rot = lambda t: t * cos + _swap_halves(t) * sin
    return (rot(rq), rot(rk) * (RDH ** -0.5))


def _split3(x):
    h1 = x.astype(bf16)
    r1 = x - h1.astype(f32)
    h2 = r1.astype(bf16)
    return h1, h2, (r1 - h2.astype(f32)).astype(bf16)


@jax.custom_vjp
def _dot_sel(x, m):
    mb = m.astype(bf16)
    h1, h2, h3 = _split3(x)
    return jnp.dot(h1, mb, preferred_element_type=f32) + jnp.dot(h2, mb, preferred_element_type=f32) + jnp.dot(h3, mb, preferred_element_type=f32)


def _dot_sel_bwd(m, g):
    mb = m.astype(bf16)
    g1, g2, g3 = _split3(g)
    nt = lambda a: lax.dot_general(a, mb, NT, preferred_element_type=f32)
    return nt(g1) + nt(g2) + nt(g3), jnp.zeros_like(m)


_dot_sel.defvjp(lambda x, m: (_dot_sel(x, m), m), _dot_sel_bwd)


def _f_rpost(y, rg, gn, gavg):
    mu = _dot_sel(y, gavg)
    yc = y - mu
    var = _dot_sel(yc * yc, gavg)
    return (jax.nn.silu(rg) * (yc * lax.rsqrt(var + EPS) * gn),)


def _f_sprep(xs, dtr, dtb, alog, ex0, ex1):
    dt = jax.nn.softplus(dtr + dtb)
    la = dt * (-jnp.exp(alog))
    return (xs * _dot_sel(dt, ex0), xs * _dot_sel(dt, ex1), la)


def _f_sprep_bwd(xs, dtr, dtb, alog, ex0, ex1):
    return _f_sprep(xs, dtr, dtb, alog, ex0, ex1) + (xs,)


def _f_spost(y, xs, z, dsk, ng):
    y = (y + xs * dsk) * jax.nn.silu(z)
    y = y * lax.rsqrt(jnp.mean(y * y, axis=-1, keepdims=True) + EPS)
    return (y * ng,)


def _loss_call(y, tgt, *, tm=ROW_TILE):
    T = y.shape[0]

    def body(y_ref, t_ref, dy_ref, l_ref):
        i = pl.program_id(0)
        e = y_ref[...] - t_ref[...]
        dy_ref[...] = e * (1.0 / D)
        part = jnp.zeros((8, LANES), f32) + 0.5 * jnp.sum(jnp.mean(e * e, axis=-1, keepdims=True))

        @pl.when(i == 0)
        def _():
            l_ref[...] = part

        @pl.when(i > 0)
        def _():
            l_ref[...] += part

    return pl.pallas_call(
        body, name="loss_head", grid=(T // tm,),
        in_specs=[pl.BlockSpec((tm, D), lambda i: (i, 0))] * 2,
        out_specs=[pl.BlockSpec((tm, D), lambda i: (i, 0)), pl.BlockSpec((8, LANES), lambda i: (0, 0))],
        out_shape=[S((T, D), f32), S((8, LANES), f32)], compiler_params=_params(("arbitrary",)))(y, tgt)


def _colsum(x, *, name, tm=512):
    T, C = x.shape

    def body(x_ref, o_ref):
        i = pl.program_id(0)
        part = jnp.sum(x_ref[...], axis=0, keepdims=True)

        @pl.when(i == 0)
        def _():
            o_ref[...] = part

        @pl.when(i > 0)
        def _():
            o_ref[...] += part

    return pl.pallas_call(
        body, name=name, grid=(T // tm,), in_specs=[pl.BlockSpec((tm, C), lambda i: (i, 0))],
        out_specs=pl.BlockSpec((1, C), lambda i: (0, 0)), out_shape=S((1, C), f32),
        compiler_params=_params(("arbitrary",)))(x)


def _nn(a, b):
    if a.ndim == 3:
        return lax.dot_general(a, b, (((2,), (1,)), ((0,), (0,))), preferred_element_type=f32)
    return jnp.dot(a, b, preferred_element_type=f32)


def _nt(a, b):
    if a.ndim == 3:
        return lax.dot_general(a, b, (((2,), (2,)), ((0,), (0,))), preferred_element_type=f32)
    return lax.dot_general(a, b, NT, preferred_element_type=f32)


def _lift(x, like):
    return jnp.broadcast_to(x[None], like.shape[:1] + x.shape) if x.ndim < like.ndim else x


def _drop(g, like):
    return jnp.sum(g, axis=0) if like.ndim < g.ndim else g


@jax.custom_vjp
def _mm_lt(a, a_t, b):
    return _nn(_lift(a_t, b), b)


_mm_lt.defvjp(lambda a, a_t, b: (_nn(_lift(a_t, b), b), (a, b)),
              lambda res, g: (jnp.zeros_like(res[0]), _drop(_nt(g, res[1]), res[0]), _nn(_lift(res[0], g), g)))


@jax.custom_vjp
def _mm_rt(a, a_t, b):
    return _nn(_lift(a, b), b)


_mm_rt.defvjp(lambda a, a_t, b: (_nn(_lift(a, b), b), (a_t, b)),
              lambda res, g: (_drop(_nt(g, res[1]), res[0]), jnp.zeros_like(res[0]), _nn(_lift(res[0], g), g)))


@jax.custom_vjp
def _masked_mm(s, s_t, d, d_t, v):
    return _nn(s * d, v)


def _masked_mm_bwd(res, g):
    s, s_t, d, d_t, v = res
    da = _nt(g, v)
    return (_drop(da * d, s), jnp.zeros_like(s_t), da * s, jnp.zeros_like(d_t), _nn(s_t * d_t, g))


_masked_mm.defvjp(lambda s, s_t, d, d_t, v: (_nn(s * d, v), (s, s_t, d, d_t, v)), _masked_mm_bwd)


def _t(x):
    return jnp.swapaxes(x, -1, -2)


@jax.custom_vjp
def _cumsums(a, tri, tri_t):
    pieces = _split3(a)
    cs = sum(jnp.dot(tri, p, preferred_element_type=f32) for p in pieces)
    cs_t = sum(lax.dot_general(p, tri_t, TN, preferred_element_type=f32) for p in pieces)
    return cs, cs_t


def _cumsums_bwd(res, g):
    tri, tri_t = res
    g_cs, g_cs_t = g
    da = sum(jnp.dot(tri_t, p, preferred_element_type=f32) for p in _split3(g_cs))
    da = da + sum(lax.dot_general(tri_t, p, NT, preferred_element_type=f32) for p in _split3(g_cs_t))
    return da, jnp.zeros_like(tri), jnp.zeros_like(tri_t)


_cumsums.defvjp(lambda a, tri, tri_t: (_cumsums(a, tri, tri_t), (tri, tri_t)), _cumsums_bwd)


def _scan_step_heads(h, q, k, v, a, rev, for_vjp=False):
    B, L, P = v.shape
    ii = lax.broadcasted_iota(jnp.int32, (L, L), 0)
    jj = lax.broadcasted_iota(jnp.int32, (L, L), 1)
    if rev:
        tri, tri_t, dmask, dmask_t = (jj >= ii), (ii >= jj), (jj > ii), (ii > jj)
    else:
        tri, tri_t, dmask, dmask_t = (jj <= ii), (ii <= jj), (jj <= ii), (ii <= jj)
    cs, cs_t = _cumsums(a, tri.astype(bf16), tri_t.astype(bf16))
    tot = jnp.sum(a, axis=0, keepdims=True)
    c_col = jnp.stack([jnp.broadcast_to(cs[:, b:b + 1], (L, L)) for b in range(B)])
    c_row = jnp.stack([cs_t[b:b + 1, :] for b in range(B)])
    t_all = jnp.stack([tot[:, b:b + 1] for b in range(B)])
    dec = jnp.exp(jnp.where(dmask[None], c_col - c_row, -1e30))
    e_in, e_out = jnp.exp(c_col)[:, :, :P], jnp.exp(t_all - c_col)[:, :, :P]
    qk = _nt(q, k)
    k_t = _t(k)
    w = v * e_out
    if for_vjp:
        q_t = lax.stop_gradient(_t(q))
        qk_t = lax.stop_gradient(_nt(k, q))
        dec_t = lax.stop_gradient(jnp.exp(jnp.where(dmask_t[None], c_row - c_col, -1e30)))
        y = _masked_mm(qk, qk_t, dec, dec_t, v) + _mm_rt(q, q_t, h) * e_in
        hn = h * jnp.exp(t_all) + _mm_lt(lax.stop_gradient(k), k_t, w)
    else:
        y = _nn(qk * dec, v) + _nn(_lift(q, h), h) * e_in
        hn = h * jnp.exp(t_all) + _nn(_lift(k_t, w), w)
    return hn, y


def _scan_specs(gb, N, Hg, P, Ha, cm, qcol, kcol, vcol):
    qs = lambda col: pl.BlockSpec((CHUNK, gb * N), lambda g, c: (cm(c), col + g))
    vs = lambda col: pl.BlockSpec((CHUNK, gb * Hg * P), lambda g, c: (cm(c), col + g))
    as_ = pl.BlockSpec((1, CHUNK, Ha), lambda g, c: (g, cm(c), 0))
    hs = pl.BlockSpec((gb, 1, Hg, N, P), lambda g, c: (g, cm(c), 0, 0, 0))
    return qs(qcol), qs(kcol), vs(vcol), qs(0), vs(0), as_, hs


def _lanes(ref, n, width):
    return jnp.stack([ref[:, j * width:(j + 1) * width] for j in range(n)])


def _scan_fwd(q, k, v, a, *, G, N, Hg, P, qcol=0, kcol=0, vcol=0, rev, name, add_y=None):
    T, Ha, NC = q.shape[0], a.shape[2], q.shape[0] // CHUNK
    gb = SCAN_HEADS_PER_STEP // Hg
    cm = (lambda c: NC - 1 - c) if rev else (lambda c: c)
    qs, ks, vs, _, ys, as_, hs = _scan_specs(gb, N, Hg, P, Ha, cm, qcol, kcol, vcol)
    extra = [] if add_y is None else [add_y]

    def body(q_ref, k_ref, v_ref, a_ref, *rest):
        y_ref, hs_ref, h_scr = rest[len(extra):]

        @pl.when(pl.program_id(1) == 0)
        def _():
            h_scr[...] = jnp.zeros_like(h_scr)

        if Hg == 1:
            h = h_scr[:, 0]
            hs_ref[:, 0, 0] = h
            hn, y = _scan_step_heads(h, _lanes(q_ref, gb, N), _lanes(k_ref, gb, N), _lanes(v_ref, gb, P), a_ref[0], rev)
            h_scr[:, 0] = hn
        else:
            h = h_scr[0]
            hs_ref[0, 0] = h
            hn, y = _scan_step_heads(h, q_ref[...], k_ref[...], _lanes(v_ref, Hg, P), a_ref[0], rev)
            h_scr[0] = hn
        for j in range(gb * Hg):
            cols = slice(j * P, (j + 1) * P)
            y_ref[:, cols] = y[j] if add_y is None else y[j] + rest[0][:, cols]

    return pl.pallas_call(
        body, name=name, grid=(G // gb, NC), in_specs=[qs, ks, vs, as_] + [ys] * len(extra), out_specs=[ys, hs],
        out_shape=[S((T, G * Hg * P), f32), S((G, NC, Hg, N, P), f32)],
        scratch_shapes=[pltpu.VMEM((gb, Hg, N, P), f32)],
        compiler_params=_params(("parallel", "arbitrary")))(q, k, v, a, *extra)


def _scan_bwd(q, k, v, a, hsave, dy, *, G, N, Hg, P, qcol=0, kcol=0, vcol=0, rev, name, add_to=(None, None, None)):
    T, Ha, NC = q.shape[0], a.shape[2], q.shape[0] // CHUNK
    gb = SCAN_HEADS_PER_STEP // Hg
    cm = (lambda c: c) if rev else (lambda c: NC - 1 - c)
    qs, ks, vs, dqs, dvs, as_, hs = _scan_specs(gb, N, Hg, P, Ha, cm, qcol, kcol, vcol)
    extra = [(x, s) for x, s in zip(add_to, (dqs, dqs, dvs)) if x is not None]

    def body(q_ref, k_ref, v_ref, a_ref, hs_ref, dy_ref, *rest):
        dq_ref, dk_ref, dv_ref, da_ref, dh_scr = rest[len(extra):]
        prev = iter(rest[:len(extra)])
        pq, pk, pv = [next(prev) if x is not None else None for x in add_to]

        @pl.when(pl.program_id(1) == 0)
        def _():
            dh_scr[...] = jnp.zeros_like(dh_scr)

        if Hg == 1:
            _, vjp = jax.vjp(functools.partial(_scan_step_heads, rev=rev, for_vjp=True), hs_ref[:, 0, 0], _lanes(q_ref, gb, N),
                             _lanes(k_ref, gb, N), _lanes(v_ref, gb, P), a_ref[0])
            dh, dq, dk, dv, da = vjp((dh_scr[:, 0], _lanes(dy_ref, gb, P)))
            dh_scr[:, 0] = dh
            for j in range(gb):
                cols = slice(j * N, (j + 1) * N)
                dq_ref[:, cols] = dq[j] if pq is None else dq[j] + pq[:, cols]
                dk_ref[:, cols] = dk[j] if pk is None else dk[j] + pk[:, cols]
        else:
            _, vjp = jax.vjp(functools.partial(_scan_step_heads, rev=rev, for_vjp=True), hs_ref[0, 0], q_ref[...], k_ref[...],
                             _lanes(v_ref, Hg, P), a_ref[0])
            dh, dq, dk, dv, da = vjp((dh_scr[0], _lanes(dy_ref, Hg, P)))
            dh_scr[0] = dh
            dq_ref[...] = dq if pq is None else dq + pq[...]
            dk_ref[...] = dk if pk is None else dk + pk[...]
        for j in range(gb * Hg):
            cols = slice(j * P, (j + 1) * P)
            dv_ref[:, cols] = dv[j] if pv is None else dv[j] + pv[:, cols]
        da_ref[0] = da

    return pl.pallas_call(
        body, name=name, grid=(G // gb, NC), in_specs=[qs, ks, vs, as_, hs, dvs] + [s for _, s in extra],
        out_specs=[dqs, dqs, dvs, as_],
        out_shape=[S((T, G * N), f32), S((T, G * N), f32), S((T, G * Hg * P), f32), S(a.shape, f32)],
        scratch_shapes=[pltpu.VMEM((gb, Hg, N, P), f32)],
        compiler_params=_params(("parallel", "arbitrary")))(q, k, v, a, hsave, dy, *[x for x, _ in extra])


def _na_block_case(rb, nrb):
    return jnp.where(rb == 0, 0, jnp.where(rb == nrb - 1, 2, 1))


def _na_key_start(rb, rows):
    return pl.multiple_of(jnp.clip(rb * NA_QROWS - NA_WR // 2, 0, rows - NA_KROWS) * GRID_W, 256)


def _na_specs(T, nrb):
    nq, nk, wb = NA_QROWS * GRID_W, NA_KROWS * GRID_W, NA_PAIR * NADH
    qs = lambda col: pl.BlockSpec((nq, wb), lambda p, r: (r, col + p))
    fs = lambda col: pl.BlockSpec((T, wb), lambda p, r: (0, col + p))
    bs = pl.BlockSpec((NA_PAIR, 1, nq, nk), lambda p, r: (p, _na_block_case(r, nrb), 0, 0))
    ls = pl.BlockSpec((1, nq, NA_PAIR), lambda p, r: (p, r, 0))
    return qs, fs, bs, ls


def _na_fwd(q, k, v, bias, *, qcol, kcol, vcol, name):
    T = q.shape[0]
    rows = T // GRID_W
    nq, nk = NA_QROWS * GRID_W, NA_KROWS * GRID_W
    nrb = T // nq
    scale = NADH ** -0.5
    qs, fs, bs, ls = _na_specs(T, nrb)

    def body(q_ref, k_ref, v_ref, b_ref, o_ref, l_ref):
        ks = _na_key_start(pl.program_id(1), rows)
        for hh in range(NA_PAIR):
            sl = slice(hh * NADH, (hh + 1) * NADH)
            kw = k_ref[pl.ds(ks, nk), sl]
            vw = v_ref[pl.ds(ks, nk), sl]
            s = lax.dot_general(q_ref[:, sl], kw, NT, preferred_element_type=f32) * scale + b_ref[hh, 0]
            m = jnp.max(s, axis=1, keepdims=True)
            p = jnp.exp(s - m)
            l = jnp.sum(p, axis=1, keepdims=True)
            o_ref[:, sl] = jnp.dot(p.astype(bf16), vw, preferred_element_type=f32) / l
            l_ref[0, :, hh:hh + 1] = m + jnp.log(l)

    return pl.pallas_call(
        body, name=name, grid=(NAH // NA_PAIR, nrb), in_specs=[qs(qcol), fs(kcol), fs(vcol), bs],
        out_specs=[qs(0), ls], out_shape=[S((T, NAW), f32), S((NAH // NA_PAIR, T, NA_PAIR), f32)],
        compiler_params=_params(("parallel", "arbitrary")))(q, k, v, bias)


def _na_bwd(q, k, v, bias, o, lse, do, *, qcol, kcol, vcol, docol, name):
    T = q.shape[0]
    rows = T // GRID_W
    nq, nk = NA_QROWS * GRID_W, NA_KROWS * GRID_W
    nrb = T // nq
    scale = NADH ** -0.5
    qs, fs, bs, ls = _na_specs(T, nrb)

    def body(q_ref, k_ref, v_ref, b_ref, o_ref, l_ref, do_ref, dq_ref, dk_ref, dv_ref, db_ref):
        rb = pl.program_id(1)

        @pl.when(rb == 0)
        def _():
            dk_ref[...] = jnp.zeros_like(dk_ref)
            dv_ref[...] = jnp.zeros_like(dv_ref)

        ks = _na_key_start(rb, rows)
        first = (rb == 0) | (rb == 1) | (rb == nrb - 1)
        for hh in range(NA_PAIR):
            sl = slice(hh * NADH, (hh + 1) * NADH)
            qv = q_ref[:, sl]
            kw = k_ref[pl.ds(ks, nk), sl]
            vw = v_ref[pl.ds(ks, nk), sl]
            s = lax.dot_general(qv, kw, NT, preferred_element_type=f32) * scale + b_ref[hh, 0]
            p = jnp.exp(s - l_ref[0, :, hh:hh + 1])
            do_ = do_ref[:, sl]
            dob = do_.astype(bf16)
            dp = lax.dot_general(dob, vw, NT, preferred_element_type=f32)
            ds = p * (dp - jnp.sum(do_ * o_ref[:, sl], axis=1, keepdims=True))
            dsb = ds.astype(bf16)
            dq_ref[:, sl] = jnp.dot(dsb, kw, preferred_element_type=f32) * scale
            dk_ref[pl.ds(ks, nk), sl] += lax.dot_general(dsb, qv, TN, preferred_element_type=f32) * scale
            dv_ref[pl.ds(ks, nk), sl] += lax.dot_general(p.astype(bf16), dob, TN, preferred_element_type=f32)

            @pl.when(first)
            def _(hh=hh, ds=ds):
                db_ref[hh, 0] = ds

            @pl.when(jnp.logical_not(first))
            def _(hh=hh, ds=ds):
                db_ref[hh, 0] += ds

    return pl.pallas_call(
        body, name=name, grid=(NAH // NA_PAIR, nrb),
        in_specs=[qs(qcol), fs(kcol), fs(vcol), bs, qs(0), ls, qs(docol)],
        out_specs=[qs(0), fs(0), fs(0), bs],
        out_shape=[S((T, NAW), f32), S((T, NAW), f32), S((T, NAW), f32), S(bias.shape, f32)],
        compiler_params=_params(("parallel", "arbitrary")))(q, k, v, bias, o, lse, do)


def _na_col_tables():
    c = np.arange(GRID_W)[:, None]
    kc = np.arange(GRID_W)[None, :]
    cstart = np.clip(c - NA_WC // 2, 0, GRID_W - NA_WC)
    valid_c = (kc >= cstart) & (kc < cstart + NA_WC)
    dc = kc - c + NA_WC - 1
    E = (valid_c[:, :, None] & (dc[:, :, None] == np.arange(2 * NA_WC - 1)[None, None, :])).astype(np.float32)
    return E, np.where(valid_c, 0.0, -1e30).astype(np.float32)


def _na_row_offsets(rows):
    table = []
    for r0 in (0, NA_QROWS, rows - NA_QROWS):
        ks = int(np.clip(r0 - NA_WR // 2, 0, rows - NA_KROWS))
        case = []
        for ri in range(NA_QROWS):
            r = r0 + ri
            rs = int(np.clip(r - NA_WR // 2, 0, rows - NA_WR))
            case.append([ks + kri - r + NA_WR - 1 if rs <= ks + kri < rs + NA_WR else None for kri in range(NA_KROWS)])
        table.append(case)
    return table


def _na_col_bias(rpb):
    E, cmask = _na_col_tables()
    return jnp.einsum("hde,cke->hdck", rpb, E, precision=HI) + cmask


def _na_bias_build(r1, rows, *, name):
    H = r1.shape[0]
    offs = _na_row_offsets(rows)

    def body(r_ref, o_ref):
        outside = jnp.full((GRID_W, GRID_W), -1e30, f32)
        for z in range(3):
            for a in range(NA_QROWS):
                for b in range(NA_KROWS):
                    d = offs[z][a][b]
                    o_ref[0, z, a * GRID_W:(a + 1) * GRID_W, b * GRID_W:(b + 1) * GRID_W] = outside if d is None else r_ref[0, d]

    return pl.pallas_call(
        body, name=name, grid=(H,), in_specs=[pl.BlockSpec((1,) + r1.shape[1:], lambda h: (h, 0, 0, 0))],
        out_specs=pl.BlockSpec((1, 3, NA_QROWS * GRID_W, NA_KROWS * GRID_W), lambda h: (h, 0, 0, 0)),
        out_shape=S((H, 3, NA_QROWS * GRID_W, NA_KROWS * GRID_W), f32), compiler_params=_params(("parallel",)))(r1)


def _na_bias_fold(dbias, rows, *, name):
    H = dbias.shape[0]
    offs = _na_row_offsets(rows)
    nd = 2 * NA_WR - 1

    def body(d_ref, o_ref):
        acc = [None] * nd
        for z in range(3):
            for a in range(NA_QROWS):
                for b in range(NA_KROWS):
                    d = offs[z][a][b]
                    if d is not None:
                        t = d_ref[0, z, a * GRID_W:(a + 1) * GRID_W, b * GRID_W:(b + 1) * GRID_W]
                        acc[d] = t if acc[d] is None else acc[d] + t
        for d in range(nd):
            o_ref[0, d] = acc[d]

    return pl.pallas_call(
        body, name=name, grid=(H,), in_specs=[pl.BlockSpec((1,) + dbias.shape[1:], lambda h: (h, 0, 0, 0))],
        out_specs=pl.BlockSpec((1, nd, GRID_W, GRID_W), lambda h: (h, 0, 0, 0)),
        out_shape=S((H, nd, GRID_W, GRID_W), f32), compiler_params=_params(("parallel",)))(dbias)


def _conv_shifts(prev, cur, nxt, i, n_i, W):
    tm = cur.shape[0]
    prev = jnp.where(i > 0, prev, 0.0)
    nxt = jnp.where(i < n_i - 1, nxt, 0.0)
    ext = jnp.concatenate([prev, cur, nxt], axis=0)
    out = []
    for w in range(W):
        s = (W // 2 - w) % (tm + 16)
        out.append((ext if s == 0 else pltpu.roll(ext, s, axis=0))[8:8 + tm])
    return out


def _conv_act(u, mode):
    if mode == "silu":
        return jax.nn.silu(u)
    assert mode == "geglu"
    half = u.shape[1] // 2
    return jax.nn.gelu(u[:, :half], approximate=True) * u[:, half:]


def _conv_specs(T, tm, tc, xbase):
    r8 = tm // 8
    last = T // 8 - 1
    cur = pl.BlockSpec((tm, tc), lambda j, i: (i, xbase + j))
    prev = pl.BlockSpec((8, tc), lambda j, i: (jnp.maximum(i * r8 - 1, 0), xbase + j))
    nxt = pl.BlockSpec((8, tc), lambda j, i: (jnp.minimum((i + 1) * r8, last), xbase + j))
    return cur, prev, nxt


def _conv(x, w8, b, *, mode, W, name, C, xbase=0, tm=CONV_TILE, tc=512, out_dtype=f32):
    T = x.shape[0]
    NI, J = T // tm, C // tc
    tco = tc // 2 if mode == "geglu" else tc
    cur, prev, nxt = _conv_specs(T, tm, tc, xbase)

    def body(xc, xp, xn, w_ref, b_ref, o_ref):
        sh = _conv_shifts(xp[...].astype(f32), xc[...].astype(f32), xn[...].astype(f32), pl.program_id(1), NI, W)
        wv = w_ref[...]
        u = sh[0] * wv[0:1, :]
        for w in range(1, W):
            u = u + sh[w] * wv[w:w + 1, :]
        o_ref[...] = _conv_act(u + b_ref[...], mode).astype(o_ref.dtype)

    return pl.pallas_call(
        body, name=name, grid=(J, NI),
        in_specs=[cur, prev, nxt, pl.BlockSpec((8, tc), lambda j, i: (0, j)), pl.BlockSpec((1, tc), lambda j, i: (0, j))],
        out_specs=pl.BlockSpec((tm, tco), lambda j, i: (i, j)), out_shape=S((T, J * tco), out_dtype),
        compiler_params=_params(("parallel", "parallel")))(x, x, x, w8, b)


def _conv_bwd(x, w8, b, dact, *, mode, W, name, C, xbase=0, tm=CONV_TILE, tc=512):
    T = x.shape[0]
    NI, J = T // tm, C // tc
    tco = tc // 2 if mode == "geglu" else tc
    rows = tm + 16
    pad = W // 2
    cur, prev, nxt = _conv_specs(T, tm, tc, xbase)
    dcur, dprev, dnxt = _conv_specs(T, tm, tco, 0)

    def body(xc, xp, xn, w_ref, b_ref, dc, dp, dn, dx_ref, dw_ref, db_ref):
        i = pl.program_id(1)
        ext = jnp.concatenate([jnp.where(i > 0, xp[...], 0.0), xc[...], jnp.where(i < NI - 1, xn[...], 0.0)], axis=0)
        dext = jnp.concatenate([jnp.where(i > 0, dp[...], 0.0), dc[...], jnp.where(i < NI - 1, dn[...], 0.0)], axis=0)
        wv = w_ref[...]
        shift = lambda t, w: t if w == pad else pltpu.roll(t, (pad - w) % rows, axis=0)
        xs = [shift(ext, w) for w in range(W)]
        u = b_ref[...] + xs[0] * wv[0:1, :]
        for w in range(1, W):
            u = u + xs[w] * wv[w:w + 1, :]
        _, vjp = jax.vjp(functools.partial(_conv_act, mode=mode), u)
        du = vjp(dext.astype(f32))[0]
        dx = shift(du, 0)[8:8 + tm] * wv[W - 1:W, :]
        for w in range(1, W):
            dx = dx + shift(du, w)[8:8 + tm] * wv[W - 1 - w:W - w, :]
        dx_ref[...] = dx.astype(dx_ref.dtype)

        @pl.when(i == 0)
        def _():
            dw_ref[...] = jnp.zeros_like(dw_ref)
            db_ref[...] = jnp.zeros_like(db_ref)

        dum = du[8:8 + tm]
        db_ref[...] += jnp.sum(dum, axis=0, keepdims=True)
        for w in range(W):
            dw_ref[w:w + 1, :] += jnp.sum(dum * xs[w][8:8 + tm], axis=0, keepdims=True)

    return pl.pallas_call(
        body, name=name, grid=(J, NI),
        in_specs=[cur, prev, nxt, pl.BlockSpec((8, tc), lambda j, i: (0, j)), pl.BlockSpec((1, tc), lambda j, i: (0, j)),
                  dcur, dprev, dnxt],
        out_specs=[pl.BlockSpec((tm, tc), lambda j, i: (i, j)), pl.BlockSpec((8, tc), lambda j, i: (0, j)),
                   pl.BlockSpec((1, tc), lambda j, i: (0, j))],
        out_shape=[S((T, C), bf16), S((8, C), f32), S((1, C), f32)],
        compiler_params=_params(("parallel", "arbitrary")))(x, x, x, w8, b, dact, dact, dact)


def _pad8(w):
    return jnp.concatenate([w, jnp.zeros((8 - w.shape[0], w.shape[1]), w.dtype)], axis=0)


def _all_gather(arrs, *, name):
    n = len(arrs)

    def body(*refs):
        ins, outs = refs[:n], refs[n:2 * n]
        send_sems, recv_sems, loc_sems = refs[2 * n:]
        x, y, c = lax.axis_index("x"), lax.axis_index("y"), lax.axis_index("c")
        ident = lambda px, py, pc: 4 * px + 2 * py + pc
        me, sibling = (x, y, c), (x, y, 1 - c)
        chips = [(1 - x, y), (x, 1 - y), (1 - x, 1 - y)]

        def copy(a, k, block, to, src=None):
            slot = outs[a].at[ident(*block)]
            return pltpu.make_async_remote_copy(
                src_ref=slot if src is None else src, dst_ref=slot, send_sem=send_sems.at[a * 7 + k], recv_sem=recv_sems.at[a * 7 + k],
                device_id=to, device_id_type=pl.DeviceIdType.MESH)

        local = [pltpu.make_async_copy(ins[a], outs[a].at[ident(*me)], loc_sems.at[a]) for a in range(n)]
        for cp in local:
            cp.start()
        first = []
        for a in range(n):
            first.append(copy(a, 0, me, sibling, src=ins[a]))
            first += [copy(a, 1 + j, me, (*chip, c), src=ins[a]) for j, chip in enumerate(chips)]
        for cp in first:
            cp.start()
        passed = []
        for j, chip in enumerate(chips):
            for a in range(n):
                copy(a, 1 + j, (*chip, c), me).wait_recv()
                fwd = copy(a, 4 + j, (*chip, c), sibling)
                fwd.start()
                passed.append(fwd)
        for a in range(n):
            copy(a, 0, sibling, me).wait_recv()
            for j, chip in enumerate(chips):
                copy(a, 4 + j, (*chip, 1 - c), me).wait_recv()
        for cp in first + passed:
            cp.wait_send()
        for cp in local:
            cp.wait()

    any_spec = pl.BlockSpec(memory_space=pl.ANY)
    return pl.pallas_call(
        body, name=name, in_specs=[any_spec] * n, out_specs=[any_spec] * n,
        out_shape=[S((NDEV,) + a.shape, a.dtype) for a in arrs],
        scratch_shapes=[pltpu.SemaphoreType.DMA((7 * n,)), pltpu.SemaphoreType.DMA((7 * n,)), pltpu.SemaphoreType.DMA((n,))],
        )(*arrs)


NCHIP = NDEV // 2


def _to_sibling(arrs, *, name):
    n = len(arrs)
    ncopy = sum(NCHIP if ps else 1 for _, ps in arrs)

    def body(*refs):
        ins, outs = refs[:n], refs[n:2 * n]
        send_sems, recv_sems = refs[2 * n:]
        x, y, c = lax.axis_index("x"), lax.axis_index("y"), lax.axis_index("c")
        copies, idx = [], 0
        for a, (_, per_slot) in enumerate(arrs):
            pairs = [(ins[a].at[2 * q + (1 - c)], outs[a].at[q]) for q in range(NCHIP)] if per_slot else [(ins[a], outs[a])]
            for src, dst in pairs:
                copies.append(pltpu.make_async_remote_copy(
                    src_ref=src, dst_ref=dst, send_sem=send_sems.at[idx], recv_sem=recv_sems.at[idx],
                    device_id=(x, y, 1 - c), device_id_type=pl.DeviceIdType.MESH))
                idx += 1
        for cp in copies:
            cp.start()
        for cp in copies:
            cp.wait_recv()
        for cp in copies:
            cp.wait_send()

    any_spec = pl.BlockSpec(memory_space=pl.ANY)
    return pl.pallas_call(
        body, name=name, in_specs=[any_spec] * n, out_specs=[any_spec] * n,
        out_shape=[S((NCHIP,) + a.shape[1:] if ps else a.shape, a.dtype) for a, ps in arrs],
        scratch_shapes=[pltpu.SemaphoreType.DMA((ncopy,)), pltpu.SemaphoreType.DMA((ncopy,))])(*[a for a, _ in arrs])


def _add_partials(mine, theirs, *, per_slot, tr, name):
    R, C = mine.shape[-2:]

    def body(a_ref, b_ref, o_ref):
        a = a_ref[lax.axis_index("c")] if per_slot else a_ref[...]
        b = b_ref[0] if per_slot else b_ref[...]
        s = a.astype(f32) + b.astype(f32)
        if per_slot:
            o_ref[0] = s.astype(o_ref.dtype)
        else:
            o_ref[...] = s.astype(o_ref.dtype)

    if per_slot:
        grid = (NCHIP, R // tr)
        in_specs = [pl.BlockSpec((2, tr, C), lambda q, i: (q, i, 0)), pl.BlockSpec((1, tr, C), lambda q, i: (q, i, 0))]
        out_spec, out_shape = pl.BlockSpec((1, tr, C), lambda q, i: (q, i, 0)), S((NCHIP, R, C), mine.dtype)
    else:
        grid = (1, R // tr)
        in_specs = [pl.BlockSpec((tr, C), lambda q, i: (i, 0))] * 2
        out_spec, out_shape = pl.BlockSpec((tr, C), lambda q, i: (i, 0)), S((R, C), mine.dtype)
    return pl.pallas_call(body, name=name, grid=grid, in_specs=in_specs, out_specs=out_spec, out_shape=out_shape,
                          compiler_params=_params(("parallel", "parallel")))(mine, theirs)


def _to_chips(arrs, *, name):
    n = len(arrs)

    def body(*refs):
        ins, outs = refs[:n], refs[n:2 * n]
        send_sems, recv_sems, loc_sems = refs[2 * n:]
        x, y, c = lax.axis_index("x"), lax.axis_index("y"), lax.axis_index("c")
        my_q = 2 * x + y
        src = lambda a, q: ins[a].at[q] if arrs[a][1] else ins[a]
        local = [pltpu.make_async_copy(src(a, my_q), outs[a].at[my_q], loc_sems.at[a]) for a in range(n)]
        for cp in local:
            cp.start()
        sent = []
        for j, (px, py) in enumerate([(1 - x, y), (x, 1 - y), (1 - x, 1 - y)]):
            q = 2 * px + py
            for a in range(n):
                mk = lambda slot, a=a, j=j, q=q, dev=(px, py, c): pltpu.make_async_remote_copy(
                    src_ref=src(a, q), dst_ref=outs[a].at[slot], send_sem=send_sems.at[3 * a + j], recv_sem=recv_sems.at[3 * a + j],
                    device_id=dev, device_id_type=pl.DeviceIdType.MESH)
                mk(my_q).start()
                sent.append((mk, q))
        for mk, q in sent:
            mk(q).wait_recv()
        for mk, q in sent:
            mk(q).wait_send()
        for cp in local:
            cp.wait()

    any_spec = pl.BlockSpec(memory_space=pl.ANY)
    return pl.pallas_call(
        body, name=name, in_specs=[any_spec] * n, out_specs=[any_spec] * n,
        out_shape=[S(a.shape if ps else (NCHIP,) + a.shape, a.dtype) for a, ps in arrs],
        scratch_shapes=[pltpu.SemaphoreType.DMA((3 * n,)), pltpu.SemaphoreType.DMA((3 * n,)), pltpu.SemaphoreType.DMA((n,))],
        )(*[a for a, _ in arrs])


def _adamw(r, w, m, v, *, name, tr):
    M, C = w.shape
    nparts = r.shape[0]

    def body(r_ref, w_ref, m_ref, v_ref, g_ref, d_ref, nm_ref, nv_ref):
        g = r_ref[0].astype(f32)
        for s in range(1, nparts):
            g = g + r_ref[s].astype(f32)
        m_ = B1 * m_ref[...] + (1.0 - B1) * g
        v_ = B2 * v_ref[...] + (1.0 - B2) * jnp.square(g)
        m_hat = m_ / (1.0 - B1 ** STEP)
        v_hat = v_ / (1.0 - B2 ** STEP)
        g_ref[...] = g
        d_ref[...] = -LR * (m_hat / (jnp.sqrt(v_hat) + AEPS) + WD * w_ref[...])
        nm_ref[...] = m_
        nv_ref[...] = v_

    row = pl.BlockSpec((tr, C), lambda i: (i, 0))
    return pl.pallas_call(
        body, name=name, grid=(M // tr,),
        in_specs=[pl.BlockSpec((nparts, tr, C), lambda i: (0, i, 0)), row, row, row],
        out_specs=[row] * 4, out_shape=[S((M, C), f32)] * 4, compiler_params=_params(("parallel",)))(r, w, m, v)


def _colmove(ins, in_slots, outs, moves, *, tk, name):
    R = ins[0].shape[1] if in_slots[0] else ins[0].shape[0]
    n_in = len(ins)

    def body(*refs):
        for ii, isl, ic, oi, osl, oc, w in moves:
            src, dst = refs[ii], refs[n_in + oi]
            val = src[:, ic:ic + w] if isl is None else src[isl, :, ic:ic + w]
            if osl is None:
                dst[:, oc:oc + w] = val.astype(dst.dtype)
            else:
                dst[osl, :, oc:oc + w] = val.astype(dst.dtype)

    def spec(is_slots, C):
        return pl.BlockSpec((NDEV, tk, C), lambda i: (0, i, 0)) if is_slots else pl.BlockSpec((tk, C), lambda i: (i, 0))

    return pl.pallas_call(
        body, name=name, grid=(R // tk,),
        in_specs=[spec(sl, a.shape[-1]) for a, sl in zip(ins, in_slots)],
        out_specs=[spec(sl, C) for sl, C, _ in outs],
        out_shape=[S((NDEV, R, C) if sl else (R, C), dt) for sl, C, dt in outs],
        compiler_params=_params(("parallel",)))(*ins)


def _col_pieces(n8, cuts, place):
    out = []
    for p in range(NDEV):
        lo, hi = p * n8, (p + 1) * n8
        edges = [lo] + [c for c in cuts if lo < c < hi] + [hi]
        for a, b in zip(edges[:-1], edges[1:]):
            out.append((p, a - lo) + place(a) + (b - a,))
    return out


def _place_plain(c):
    return (0, c)


def _place_ssd_in(c):
    return (0, c) if c < SSD_INNER + SSD_XBC else (1, c - (SSD_INNER + SSD_XBC))


def _place_ffn_up(c):
    h = FFN_TC // 2
    return (0, (c // h) * FFN_TC + c % h) if c < FFN else (0, ((c - FFN) // h) * FFN_TC + h + (c - FFN) % h)


_COL_LAYOUTS = {
    "ab_w_in": ([], _place_plain, [4 * RW + 3 * NAW]),
    "c_w_in": ([SSD_INNER + SSD_XBC], _place_ssd_in, [SSD_INNER + SSD_XBC, 2 * SSD_H]),
    "ffn_w_up": (list(range(FFN_TC // 2, 2 * FFN, FFN_TC // 2)), _place_ffn_up, [2 * FFN]),
}


def _cols_from_slots(g, which, *, name):
    cuts, place, widths = _COL_LAYOUTS[which]
    moves = [(0, p, sc, mi, None, mc, w) for p, sc, mi, mc, w in _col_pieces(g.shape[2], cuts, place)]
    return _colmove([g], [True], [(False, w, g.dtype) for w in widths], moves, tk=256, name=name)


def _cols_to_slots(mats, which, dtype, *, name):
    cuts, place, widths = _COL_LAYOUTS[which]
    n8 = sum(widths) // NDEV
    moves = [(mi, None, mc, 0, p, sc, w) for p, sc, mi, mc, w in _col_pieces(n8, cuts, place)]
    return _colmove(list(mats), [False] * len(mats), [(True, n8, dtype)], moves, tk=256, name=name)[0]


def _pack(parts, dtype, row_mult):
    flat = jnp.concatenate([p.reshape(-1).astype(dtype) for p in parts])
    rows = -(-flat.shape[0] // LANES)
    rows = -(-rows // row_mult) * row_mult
    return jnp.pad(flat, (0, rows * LANES - flat.shape[0])).reshape(rows, LANES)


def _unpack(buf, shapes, lead=()):
    flat = buf.reshape(lead + (-1,))
    out, off = [], 0
    for shp in shapes:
        n = int(np.prod(shp))
        out.append(flat[..., off:off + n].reshape(lead + tuple(shp)))
        off += n
    return out


def _to_slots(full, ax):
    shp = full.shape
    return jnp.moveaxis(full.reshape(shp[:ax] + (NDEV, shp[ax] // NDEV) + shp[ax + 1:]), ax, 0)


def _from_slots(g, ax):
    t = jnp.moveaxis(g, 0, ax)
    shp = t.shape
    return t.reshape(shp[:ax] + (shp[ax] * shp[ax + 1],) + shp[ax + 2:])


def _ffn_perm(a):
    lead = a.shape[:-1]
    h = FFN_TC // 2
    return jnp.swapaxes(a.reshape(lead + (2, FFN // h, h)), -3, -2).reshape(lead + (2 * FFN,))


def _ffn_unperm(a):
    lead = a.shape[:-1]
    h = FFN_TC // 2
    return jnp.swapaxes(a.reshape(lead + (FFN // h, 2, h)), -3, -2).reshape(lead + (2 * FFN,))


def _rope_tables(T):
    half = RDH // 2
    inv = 1.0 / (ROPE_BASE ** (jnp.arange(half, dtype=f32) / half))
    ang = jnp.arange(T, dtype=f32)[:, None] * inv[None, :]
    cos, sin = jnp.cos(ang), jnp.sin(ang)
    cos_t = jnp.tile(jnp.concatenate([cos, cos], axis=1), (1, RH))
    sin_t = jnp.tile(jnp.concatenate([-sin, sin], axis=1), (1, RH))
    return cos_t, sin_t


def _group_avg():
    g = np.arange(RW) // RDH
    return jnp.asarray((g[:, None] == g[None, :]).astype(np.float32) / RDH)


def _head_expand():
    hd = np.arange(SSD_INNER) // SSD_HD
    rows = np.arange(2 * SSD_H)
    ex0 = (rows[:, None] == hd[None, :]).astype(np.float32)
    ex1 = (rows[:, None] == SSD_H + hd[None, :]).astype(np.float32)
    return jnp.asarray(ex0), jnp.asarray(ex1)


def kernel(x, norm_mix_pre, norm_mix_post, norm_ffn_pre, norm_ffn_post, ab_w_in, ab_ret_decay_logit, ab_ret_gn_g, ab_na_rpb, ab_w_out, c_w_in, c_conv_w, c_conv_b, c_dt_bias, c_a_log, c_d_skip, c_norm_g, c_w_out, ffn_w_up, ffn_conv_w, ffn_conv_b, ffn_w_down, loss_target, m_norm_mix_pre, m_norm_mix_post, m_norm_ffn_pre, m_norm_ffn_post, m_ab_w_in, m_ab_ret_decay_logit, m_ab_ret_gn_g, m_ab_na_rpb, m_ab_w_out, m_c_w_in, m_c_conv_w, m_c_conv_b, m_c_dt_bias, m_c_a_log, m_c_d_skip, m_c_norm_g, m_c_w_out, m_ffn_w_up, m_ffn_conv_w, m_ffn_conv_b, m_ffn_w_down, v_norm_mix_pre, v_norm_mix_post, v_norm_ffn_pre, v_norm_ffn_post, v_ab_w_in, v_ab_ret_decay_logit, v_ab_ret_gn_g, v_ab_na_rpb, v_ab_w_out, v_c_w_in, v_c_conv_w, v_c_conv_b, v_c_dt_bias, v_c_a_log, v_c_d_skip, v_c_norm_g, v_c_w_out, v_ffn_w_up, v_ffn_conv_w, v_ffn_conv_b, v_ffn_w_down):
    W = dict(norm_mix_pre=norm_mix_pre, norm_mix_post=norm_mix_post, norm_ffn_pre=norm_ffn_pre, norm_ffn_post=norm_ffn_post, ab_w_in=ab_w_in, ab_ret_decay_logit=ab_ret_decay_logit, ab_ret_gn_g=ab_ret_gn_g, ab_na_rpb=ab_na_rpb, ab_w_out=ab_w_out, c_w_in=c_w_in, c_conv_w=c_conv_w, c_conv_b=c_conv_b, c_dt_bias=c_dt_bias, c_a_log=c_a_log, c_d_skip=c_d_skip, c_norm_g=c_norm_g, c_w_out=c_w_out, ffn_w_up=ffn_w_up, ffn_conv_w=ffn_conv_w, ffn_conv_b=ffn_conv_b, ffn_w_down=ffn_w_down)
    Mo = dict(norm_mix_pre=m_norm_mix_pre, norm_mix_post=m_norm_mix_post, norm_ffn_pre=m_norm_ffn_pre, norm_ffn_post=m_norm_ffn_post, ab_w_in=m_ab_w_in, ab_ret_decay_logit=m_ab_ret_decay_logit, ab_ret_gn_g=m_ab_ret_gn_g, ab_na_rpb=m_ab_na_rpb, ab_w_out=m_ab_w_out, c_w_in=m_c_w_in, c_conv_w=m_c_conv_w, c_conv_b=m_c_conv_b, c_dt_bias=m_c_dt_bias, c_a_log=m_c_a_log, c_d_skip=m_c_d_skip, c_norm_g=m_c_norm_g, c_w_out=m_c_w_out, ffn_w_up=m_ffn_w_up, ffn_conv_w=m_ffn_conv_w, ffn_conv_b=m_ffn_conv_b, ffn_w_down=m_ffn_w_down)
    Vo = dict(norm_mix_pre=v_norm_mix_pre, norm_mix_post=v_norm_mix_post, norm_ffn_pre=v_norm_ffn_pre, norm_ffn_post=v_norm_ffn_post, ab_w_in=v_ab_w_in, ab_ret_decay_logit=v_ab_ret_decay_logit, ab_ret_gn_g=v_ab_ret_gn_g, ab_na_rpb=v_ab_na_rpb, ab_w_out=v_ab_w_out, c_w_in=v_c_w_in, c_conv_w=v_c_conv_w, c_conv_b=v_c_conv_b, c_dt_bias=v_c_dt_bias, c_a_log=v_c_a_log, c_d_skip=v_c_d_skip, c_norm_g=v_c_norm_g, c_w_out=v_c_w_out, ffn_w_up=v_ffn_w_up, ffn_conv_w=v_ffn_conv_w, ffn_conv_b=v_ffn_conv_b, ffn_w_down=v_ffn_w_down)
    return _train_step(x[0], loss_target[0], W, Mo, Vo)


def _train_step(x, tgt, W, Mo, Vo):
    T = x.shape[0]
    rows = T // GRID_W

    col = lambda d, n, dt: d[n].reshape(-1, d[n].shape[-1]).astype(dt)
    rows_of = lambda d, dt: jnp.concatenate([col(d, n, dt) for n in ROW_SHARDED], axis=0)
    small = _pack([W[n] for n, _ in SHARDED[N_BIG:]], f32, 8)
    gat = _all_gather([col(W, n, bf16) for n in COL_SHARDED] + [rows_of(W, bf16), small], name="gather_weights")
    per_layer = lambda m: m.reshape(-1, D, m.shape[-1])
    w_ab_in = per_layer(_cols_from_slots(gat[0], "ab_w_in", name="cols_ab_w_in")[0])
    w_zx, w_dt = [per_layer(m) for m in _cols_from_slots(gat[1], "c_w_in", name="cols_c_w_in")]
    w_up = per_layer(_cols_from_slots(gat[2], "ffn_w_up", name="cols_ffn_w_up")[0])
    full, off = {}, 0
    for n in ROW_SHARDED:
        L, r = W[n].shape[0], W[n].shape[1]
        full[n] = jnp.swapaxes(gat[3][:, off:off + L * r].reshape(NDEV, L, r, D), 0, 1).reshape(L, NDEV * r, D)
        off += L * r
    gs = _unpack(gat[4], [W[n].shape for n, _ in SHARDED[N_BIG:]], (NDEV,))
    full.update({n: _from_slots(g, ax) for (n, ax), g in zip(SHARDED[N_BIG:], gs)})
    w_ab_out, w_c_out, w_down = full["ab_w_out"], full["c_w_out"], full["ffn_w_down"]
    c_cw8 = [_pad8(full["c_conv_w"][i]) for i in range(2)]
    c_cb = [full["c_conv_b"][i][None] for i in range(2)]
    c_ng = [full["c_norm_g"][i][None] for i in range(2)]
    f_cw8 = [_pad8(_ffn_perm(full["ffn_conv_w"][l])) for l in range(DEPTH)]
    f_cb = [_ffn_perm(W["ffn_conv_b"][l])[None] for l in range(DEPTH)]

    g1 = [W["norm_mix_pre"][l][None] for l in range(DEPTH)]
    g2 = [W["norm_mix_post"][l][None] for l in range(DEPTH)]
    g3 = [W["norm_ffn_pre"][l][None] for l in range(DEPTH)]
    g4 = [W["norm_ffn_post"][l][None] for l in range(DEPTH)]
    cos_t, sin_t = _rope_tables(T)
    gavg = _group_avg()
    ex0, ex1 = _head_expand()

    def log_gamma(logit):
        return -jax.nn.softplus(-logit)

    def ret_decays(lg):
        return [jnp.broadcast_to(lg[d][None, None, :], (1, T, RH)) for d in range(2)]

    saved = []
    xs_ = x
    hn = _rowwise("norm_first", _f_first, [(x, D, 0)], [], [(g1[0], D, 0)], [], [(D, bf16)], tm=ROW_TILE)[0]
    for l in range(DEPTH):
        i = l // 2
        sv = dict(x=xs_, hn=hn)
        if l % 2 == 0:
            proj = _mm_nn(hn, w_ab_in[i], name=f"ab_in_{l}")
            qr, kr = _rowwise(f"ret_prep_{l}", _f_rprep, [(proj, RW, 0), (proj, RW, 1)], [(cos_t, RW, 0), (sin_t, RW, 0)], [], [],
                              [(RW, f32), (RW, f32)], tm=ROW_TILE)
            lg, lg_vjp = jax.vjp(log_gamma, W["ab_ret_decay_logit"][i])
            a_f, a_b = ret_decays(lg)
            rscan = dict(G=RH, N=RDH, Hg=1, P=RDH, vcol=2)
            yf_t, hsf = _scan_fwd(qr, kr, proj, a_f, rev=False, name=f"ret_scan_f_{l}", **rscan)
            y_t, hsb = _scan_fwd(qr, kr, proj, a_b, rev=True, name=f"ret_scan_b_{l}", add_y=yf_t, **rscan)
            gn = W["ab_ret_gn_g"][i][None]
            ret = _rowwise(f"ret_post_{l}", _f_rpost, [(y_t, RW, 0), (proj, RW, 3)], [], [(gn, RW, 0)], [gavg],
                           [(RW, bf16)], tm=ROW_TILE)[0]
            nqkv = proj[:, 4 * RW:].astype(bf16)
            ncols = dict(qcol=0, kcol=NAW // 128, vcol=2 * NAW // 128)
            r1, bias_vjp = jax.vjp(_na_col_bias, W["ab_na_rpb"][i])
            bias = _na_bias_build(r1, rows, name=f"na_bias_{l}")
            na_o, na_l = _na_fwd(nqkv, nqkv, nqkv, bias, name=f"na_fwd_{l}", **ncols)
            cat = jnp.concatenate([ret, na_o.astype(bf16)], axis=1)
            mo = _mm_nn(cat, w_ab_out[i], name=f"ab_out_{l}")
            sv.update(proj=proj, qr=qr, kr=kr, a_f=a_f, a_b=a_b, hsf=hsf, hsb=hsb, y_t=y_t, gn=gn, rscan=rscan,
                      nqkv=nqkv, ncols=ncols, bias=bias, bias_vjp=bias_vjp, lg_vjp=lg_vjp, na_o=na_o, na_l=na_l, cat=cat)
        else:
            zx = _mm_nn(hn, w_zx[i], name=f"c_in_{l}")
            dtr = _mm_nn(hn, w_dt[i], name=f"c_in_dt_{l}")
            xa = _conv(zx, c_cw8[i], c_cb[i], mode="silu", W=SSD_CONV, name=f"c_conv_{l}", C=SSD_XBC, xbase=SSD_INNER // 512)
            dtb, alog = W["c_dt_bias"][i].reshape(1, 2 * SSD_H), W["c_a_log"][i].reshape(1, 2 * SSD_H)
            vf, vb, la = _rowwise(f"ssd_prep_{l}", _f_sprep, [(xa, SSD_INNER, 0), (dtr, 2 * SSD_H, 0)], [],
                                  [(dtb, 2 * SSD_H, 0), (alog, 2 * SSD_H, 0)], [ex0, ex1],
                                  [(SSD_INNER, f32), (SSD_INNER, f32), (2 * SSD_H, f32)], tm=WIDE_ROW_TILE)
            a_f = la[:, :SSD_H].reshape(T, SSD_G, SSD_HPG).transpose(1, 0, 2)
            a_b = la[:, SSD_H:].reshape(T, SSD_G, SSD_HPG).transpose(1, 0, 2)
            sscan = dict(G=SSD_G, N=SSD_N, Hg=SSD_HPG, P=SSD_HD, qcol=(SSD_INNER + SSD_G * SSD_N) // SSD_N, kcol=SSD_INNER // SSD_N)
            yf_t, hsf = _scan_fwd(xa, xa, vf, a_f, rev=False, name=f"ssd_scan_f_{l}", **sscan)
            y_t, hsb = _scan_fwd(xa, xa, vb, a_b, rev=True, name=f"ssd_scan_b_{l}", add_y=yf_t, **sscan)
            dsk = jnp.repeat(W["c_d_skip"][i], SSD_HD)[None]
            yo = _rowwise(f"ssd_post_{l}", _f_spost, [(y_t, 512, 0), (xa, 512, 0), (zx, 512, 0)], [],
                          [(dsk, 512, 0), (c_ng[i], 512, 0)], [], [(512, bf16)], tm=ROW_TILE, J=SSD_G)[0]
            mo = _mm_nn(yo, w_c_out[i], name=f"c_out_{l}")
            sv.update(zx=zx, dtr=dtr, xa=xa, dtb=dtb, alog=alog, a_f=a_f, a_b=a_b, vf=vf, vb=vb, sscan=sscan,
                      hsf=hsf, hsb=hsb, y_t=y_t, dsk=dsk, yo=yo)
        x1, hf = _rowwise(f"norm_mid_{l}", _f_mid, [(xs_, D, 0), (mo, D, 0)], [], [(g2[l], D, 0), (g3[l], D, 0)], [],
                          [(D, f32), (D, bf16)], tm=ROW_TILE)
        pre = _mm_nn(hf, w_up[l], name=f"ffn_up_{l}")
        act = _conv(pre, f_cw8[l], f_cb[l], mode="geglu", W=FFN_CONV, name=f"ffn_conv_{l}", C=2 * FFN, tc=FFN_TC, out_dtype=bf16)
        fo = _mm_nn(act, w_down[l], name=f"ffn_down_{l}")
        sv.update(mo=mo, x1=x1, hf=hf, pre=pre, act=act, fo=fo)
        if l < DEPTH - 1:
            xs_, hn = _rowwise(f"norm_end_{l}", _f_end, [(x1, D, 0), (fo, D, 0)], [], [(g4[l], D, 0), (g1[l + 1], D, 0)], [],
                               [(D, f32), (D, bf16)], tm=ROW_TILE)
        else:
            xs_ = _rowwise(f"norm_end_{l}", _f_last, [(x1, D, 0), (fo, D, 0)], [], [(g4[l], D, 0)], [], [(D, f32)], tm=ROW_TILE)[0]
        saved.append(sv)

    dx, lpart = _loss_call(xs_, tgt)
    loss = lax.psum(lpart[0, 0], ("x", "y", "c"))

    G = {n: [None] * W[n].shape[0] for n in WEIGHTS}
    dhn = None
    for l in reversed(range(DEPTH)):
        i = l // 2
        sv = saved[l]
        if l == DEPTH - 1:
            (dx1, dfo), (dg4,) = _rowwise_bwd(f"norm_end_bwd_{l}", _f_last, [(sv["x1"], D, 0), (sv["fo"], D, 0)], [],
                                              [(g4[l], D, 0)], [], [(dx, D, 0)], [f32, bf16], tm=ROW_TILE)
        else:
            (dx1, dfo), (dg4, dg1n) = _rowwise_bwd(f"norm_end_bwd_{l}", _f_end, [(sv["x1"], D, 0), (sv["fo"], D, 0)], [],
                                                   [(g4[l], D, 0), (g1[l + 1], D, 0)], [], [(dx, D, 0), (dhn, D, 0)],
                                                   [f32, bf16], tm=ROW_TILE)
            G["norm_mix_pre"][l + 1] = dg1n[0]
        G["norm_ffn_post"][l] = dg4[0]
        dact = _mm_nt(dfo, w_down[l], name=f"ffn_down_dx_{l}")
        G["ffn_w_down"][l] = _mm_tn(sv["act"], dfo, name=f"ffn_down_dw_{l}")
        dpre, dfw, dfb = _conv_bwd(sv["pre"], f_cw8[l], f_cb[l], dact, mode="geglu", W=FFN_CONV, name=f"ffn_conv_bwd_{l}",
                                   C=2 * FFN, tc=FFN_TC)
        dhf = _mm_nt(dpre, w_up[l], name=f"ffn_up_dx_{l}")
        G["ffn_w_up"][l] = _cols_to_slots([_mm_tn(sv["hf"], dpre, name=f"ffn_up_dw_{l}")], "ffn_w_up", bf16, name=f"slots_ffn_up_{l}")
        G["ffn_conv_w"][l] = _ffn_unperm(dfw[:FFN_CONV])
        G["ffn_conv_b"][l] = _ffn_unperm(dfb[0])
        (dxl, dmo), (dg2, dg3) = _rowwise_bwd(f"norm_mid_bwd_{l}", _f_mid, [(sv["x"], D, 0), (sv["mo"], D, 0)], [],
                                              [(g2[l], D, 0), (g3[l], D, 0)], [], [(dx1, D, 0), (dhf, D, 0)], [f32, bf16], tm=ROW_TILE)
        G["norm_mix_post"][l] = dg2[0]
        G["norm_ffn_pre"][l] = dg3[0]
        if l % 2 == 0:
            dcat = _mm_nt(dmo, w_ab_out[i], name=f"ab_out_dx_{l}")
            G["ab_w_out"][i] = _mm_tn(sv["cat"], dmo, name=f"ab_out_dw_{l}")
            (dy, drg), (dgn,) = _rowwise_bwd(
                f"ret_post_bwd_{l}", _f_rpost, [(sv["y_t"], RW, 0), (sv["proj"], RW, 3)], [],
                [(sv["gn"], RW, 0)], [gavg], [(dcat, RW, 0)], [f32, bf16], tm=ROW_TILE)
            G["ab_ret_gn_g"][i] = dgn[0]
            dqf, dkf, dvf, daf = _scan_bwd(sv["qr"], sv["kr"], sv["proj"], sv["a_f"], sv["hsf"], dy, rev=False,
                                           name=f"ret_scan_f_bwd_{l}", **sv["rscan"])
            dq_t, dk_t, dv_t, dab = _scan_bwd(sv["qr"], sv["kr"], sv["proj"], sv["a_b"], sv["hsb"], dy, rev=True,
                                              name=f"ret_scan_b_bwd_{l}", add_to=(dqf, dkf, dvf), **sv["rscan"])
            drv = dv_t.astype(bf16)
            (drq, drk), _ = _rowwise_bwd(f"ret_prep_bwd_{l}", _f_rprep, [(sv["proj"], RW, 0), (sv["proj"], RW, 1)],
                                         [(cos_t, RW, 0), (sin_t, RW, 0)], [], [], [(dq_t, RW, 0), (dk_t, RW, 0)], [bf16, bf16], tm=ROW_TILE)
            da_cols = jnp.concatenate([daf[0], dab[0]], axis=1)
            dlg = _colsum(da_cols, name=f"ret_decay_sum_{l}").reshape(2, RH)
            G["ab_ret_decay_logit"][i] = sv["lg_vjp"](dlg)[0]
            dnq, dnk, dnv, dbias = _na_bwd(sv["nqkv"], sv["nqkv"], sv["nqkv"], sv["bias"], sv["na_o"], sv["na_l"], dcat,
                                           docol=RW // 128, name=f"na_bwd_{l}", **sv["ncols"])
            G["ab_na_rpb"][i] = sv["bias_vjp"](_na_bias_fold(dbias, rows, name=f"na_bias_fold_{l}"))[0]
            dproj = jnp.concatenate([drq, drk, drv, drg] + [t.astype(bf16) for t in (dnq, dnk, dnv)], axis=1)
            dhn = _mm_nt(dproj, w_ab_in[i], name=f"ab_in_dx_{l}")
            G["ab_w_in"][i] = _cols_to_slots([_mm_tn(sv["hn"], dproj, name=f"ab_in_dw_{l}")], "ab_w_in", bf16, name=f"slots_ab_in_{l}")
        else:
            dyo = _mm_nt(dmo, w_c_out[i], name=f"c_out_dx_{l}")
            G["c_w_out"][i] = _mm_tn(sv["yo"], dmo, name=f"c_out_dw_{l}")
            (dy, dxs1, dz), (ddsk, dng) = _rowwise_bwd(
                f"ssd_post_bwd_{l}", _f_spost, [(sv["y_t"], 512, 0), (sv["xa"], 512, 0), (sv["zx"], 512, 0)],
                [], [(sv["dsk"], 512, 0), (c_ng[i], 512, 0)], [], [(dyo, 512, 0)], [f32, f32, bf16], tm=ROW_TILE, J=SSD_G)
            G["c_d_skip"][i] = ddsk.reshape(SSD_H, SSD_HD).sum(axis=1)
            G["c_norm_g"][i] = dng[0]
            dqf, dkf, dvf, daf = _scan_bwd(sv["xa"], sv["xa"], sv["vf"], sv["a_f"], sv["hsf"], dy, rev=False,
                                           name=f"ssd_scan_f_bwd_{l}", **sv["sscan"])
            dq_t, dk_t, dvb, dab = _scan_bwd(sv["xa"], sv["xa"], sv["vb"], sv["a_b"], sv["hsb"], dy, rev=True,
                                             name=f"ssd_scan_b_bwd_{l}", add_to=(dqf, dkf, None), **sv["sscan"])
            dla = jnp.concatenate([daf.transpose(1, 0, 2).reshape(T, SSD_H), dab.transpose(1, 0, 2).reshape(T, SSD_H)], axis=1)
            (dxs, ddtr), (ddtb, dalog) = _rowwise_bwd(
                f"ssd_prep_bwd_{l}", _f_sprep_bwd, [(sv["xa"], SSD_INNER, 0), (sv["dtr"], 2 * SSD_H, 0)], [],
                [(sv["dtb"], 2 * SSD_H, 0), (sv["alog"], 2 * SSD_H, 0)], [ex0, ex1],
                [(dvf, SSD_INNER, 0), (dvb, SSD_INNER, 0), (dla, 2 * SSD_H, 0), (dxs1, SSD_INNER, 0)], [f32, bf16], tm=WIDE_ROW_TILE)
            G["c_dt_bias"][i] = ddtb.reshape(2, SSD_H)
            G["c_a_log"][i] = dalog.reshape(2, SSD_H)
            dxa = jnp.concatenate([dxs, dk_t, dq_t], axis=1)
            dxbc, dcw, dcb = _conv_bwd(sv["zx"], c_cw8[i], c_cb[i], dxa, mode="silu", W=SSD_CONV, name=f"c_conv_bwd_{l}",
                                       C=SSD_XBC, xbase=SSD_INNER // 512)
            G["c_conv_w"][i] = dcw[:SSD_CONV]
            G["c_conv_b"][i] = dcb[0]
            dzx = jnp.concatenate([dz, dxbc], axis=1)
            t1 = _mm_nt(ddtr, w_dt[i], name=f"c_in_dt_dx_{l}")
            dhn = _mm_nt(dzx, w_zx[i], add=t1, name=f"c_in_dx_{l}")
            G["c_w_in"][i] = _cols_to_slots([_mm_tn(sv["hn"], dzx, name=f"c_in_dw_{l}"), _mm_tn(sv["hn"], ddtr, name=f"c_in_dt_dw_{l}")],
                                            "c_w_in", bf16, name=f"slots_c_in_{l}")
        dx = dxl
    (grad_x,), (dg1,) = _rowwise_bwd("norm_first_bwd", _f_first_bwd, [(x, D, 0)], [], [(g1[0], D, 0)], [], [(dx, D, 0), (dhn, D, 0)],
                                     [f32], tm=ROW_TILE)
    G["norm_mix_pre"][0] = dg1[0]

    small_names = [n for n, _ in SHARDED[N_BIG:]]
    col_slots = [jnp.concatenate(G[n], axis=1) for n in COL_SHARDED]
    row_slots = jnp.concatenate([g.reshape(NDEV, -1, D).astype(bf16) for n in ROW_SHARDED for g in G[n]], axis=1)
    small_slots = _pack_slots([_to_slots(jnp.stack(G[n]), ax) for n, ax in SHARDED[N_BIG:]], 8)
    ar = _pack([jnp.stack(G[n]) for n in REPLICATED], f32, 8)
    parts = [(a, True) for a in col_slots + [row_slots, small_slots]] + [(ar, False)]
    from_sib = _to_sibling(parts, name="grads_to_sibling")
    tiles = [256, 256, 256, 64, small_slots.shape[1], ar.shape[0]]
    chip = [_add_partials(a, b, per_slot=ps, tr=t, name=f"grads_add_{j}")
            for j, ((a, ps), b, t) in enumerate(zip(parts, from_sib, tiles))]
    exch = _to_chips([(a, ps) for a, (_, ps) in zip(chip, parts)], name="grads_to_chips")
    pk = lambda d, names: _pack([d[n] for n in names], f32, 8)
    upd = [_adamw(exch[j], col(W, n, f32), col(Mo, n, f32), col(Vo, n, f32), name=f"adamw_{n}", tr=256)
           for j, n in enumerate(COL_SHARDED)]
    upd_rows = _adamw(exch[3], rows_of(W, f32), rows_of(Mo, f32), rows_of(Vo, f32), name="adamw_rows", tr=64)
    upd_small = _adamw(exch[4], pk(W, small_names), pk(Mo, small_names), pk(Vo, small_names), name="adamw_small",
                       tr=small_slots.shape[1])
    upd_rep = _adamw(exch[5], pk(W, REPLICATED), pk(Mo, REPLICATED), pk(Vo, REPLICATED), name="adamw_replicated", tr=ar.shape[0])
    res = []
    for k in range(4):
        d = {n: upd[j][k].reshape(W[n].shape) for j, n in enumerate(COL_SHARDED)}
        off = 0
        for n in ROW_SHARDED:
            cnt = W[n].shape[0] * W[n].shape[1]
            d[n] = upd_rows[k][off:off + cnt].reshape(W[n].shape)
            off += cnt
        d.update(zip(small_names, _unpack(upd_small[k], [W[n].shape for n in small_names])))
        d.update(zip(REPLICATED, _unpack(upd_rep[k], [W[n].shape for n in REPLICATED])))
        res.append(d)
    outs = [loss, grad_x[None]]
    for k in range(4):
        outs += [res[k][n] for n in WEIGHTS]
    return tuple(outs)


def _pack_slots(slot_arrays, row_mult):
    flat = jnp.concatenate([a.reshape(NDEV, -1) for a in slot_arrays], axis=1)
    rows = -(-flat.shape[1] // LANES)
    rows = -(-rows // row_mult) * row_mult
    return jnp.pad(flat, ((0, 0), (0, rows * LANES - flat.shape[1]))).reshape(NDEV, rows, LANES)
```
